```python
import jax, jax.numpy as jnp
from jax import lax
import numpy as np

D_MODEL = 2048
BATCH = 8
SEQ = 8192
DEPTH = 1

HEAD_DIM = 64
SWA_Q_HEADS = 16
SWA_KV_HEADS = 2
SWA_GROUP = SWA_Q_HEADS // SWA_KV_HEADS
WINDOW = 128
FOX_HEADS = 16
FOX_BLOCK = 128
D_FF = 4 * D_MODEL
ROPE_THETA = 10000.0
RMS_EPS = 1e-6

SWA_Q_W = SWA_Q_HEADS * HEAD_DIM
SWA_KV_W = SWA_KV_HEADS * HEAD_DIM
FOX_W = FOX_HEADS * HEAD_DIM
IN_WIDTHS = (SWA_Q_W, SWA_KV_W, SWA_KV_W, FOX_W, FOX_W, FOX_W, FOX_HEADS, D_MODEL, D_MODEL)
D_IN = sum(IN_WIDTHS)
IN_SPLITS = tuple(int(v) for v in np.cumsum(IN_WIDTHS)[:-1])

kernel_name = "hybrid_swa_sink_fox_gated_block"


def rmsnorm(x, gain):
    xf = x.astype(jnp.float32)
    out = xf * lax.rsqrt(jnp.mean(xf * xf, axis=-1, keepdims=True) + RMS_EPS) * gain.astype(jnp.float32)
    return out.astype(x.dtype)


def apply_rope(t, positions):
    inv_freq = ROPE_THETA ** (-jnp.arange(0, HEAD_DIM, 2, dtype=jnp.float32) / HEAD_DIM)
    ang = positions.astype(jnp.float32)[..., None] * inv_freq
    cos = jnp.cos(ang)[:, :, None, :]
    sin = jnp.sin(ang)[:, :, None, :]
    tf = t.astype(jnp.float32)
    t1, t2 = tf[..., : HEAD_DIM // 2], tf[..., HEAD_DIM // 2:]
    out = jnp.concatenate([t1 * cos - t2 * sin, t2 * cos + t1 * sin], axis=-1)
    return out.astype(t.dtype)


def sliding_window_gqa_sinks(q, k, v, sinks):
    B, S = q.shape[0], q.shape[1]
    nb = S // WINDOW
    scale = HEAD_DIM ** -0.5
    qb = q.reshape(B, nb, WINDOW, SWA_KV_HEADS, SWA_GROUP, HEAD_DIM)
    kb = k.reshape(B, nb, WINDOW, SWA_KV_HEADS, HEAD_DIM)
    vb = v.reshape(B, nb, WINDOW, SWA_KV_HEADS, HEAD_DIM)
    pad = ((0, 0), (1, 0), (0, 0), (0, 0), (0, 0))
    kk = jnp.concatenate([jnp.pad(kb, pad)[:, :-1], kb], axis=2)
    vv = jnp.concatenate([jnp.pad(vb, pad)[:, :-1], vb], axis=2)
    logits = jnp.einsum('bnqhgd,bnkhd->bnhgqk', qb, kk).astype(jnp.float32) * scale
    blk = jnp.arange(nb)[:, None, None]
    qi = jnp.arange(WINDOW)[None, :, None] + WINDOW
    kj = jnp.arange(2 * WINDOW)[None, None, :]
    diff = qi - kj
    allowed = (diff >= 0) & (diff < WINDOW) & (blk * WINDOW + kj - WINDOW >= 0)
    logits = jnp.where(allowed[None, :, None, None], logits, -jnp.inf)
    sink_col = jnp.broadcast_to(
        sinks.astype(jnp.float32).reshape(SWA_KV_HEADS, SWA_GROUP)[None, None, :, :, None, None],
        logits.shape[:-1] + (1,))
    probs = jax.nn.softmax(jnp.concatenate([logits, sink_col], axis=-1), axis=-1)[..., :-1]
    out = jnp.einsum('bnhgqk,bnkhd->bnqhgd', probs.astype(v.dtype), vv)
    return out.reshape(B, S, SWA_Q_HEADS * HEAD_DIM)


def forgetting_attention(q, k, v, log_f):
    B, S = q.shape[0], q.shape[1]
    nb = S // FOX_BLOCK
    scale = HEAD_DIM ** -0.5
    qh = jnp.transpose(q, (0, 2, 1, 3))
    kh = jnp.transpose(k, (0, 2, 1, 3))
    vh = jnp.transpose(v, (0, 2, 1, 3))
    c = jnp.cumsum(jnp.transpose(log_f, (0, 2, 1)), axis=-1)
    key_pos = jnp.arange(S)

    def block(i):
        start = i * FOX_BLOCK
        qi = lax.dynamic_slice_in_dim(qh, start, FOX_BLOCK, axis=2)
        ci = lax.dynamic_slice_in_dim(c, start, FOX_BLOCK, axis=2)
        logits = jnp.einsum('bhqd,bhkd->bhqk', qi, kh).astype(jnp.float32) * scale
        logits = logits + (ci[..., :, None] - c[..., None, :])
        qpos = start + jnp.arange(FOX_BLOCK)
        causal = key_pos[None, :] <= qpos[:, None]
        logits = jnp.where(causal[None, None], logits, -jnp.inf)
        probs = jax.nn.softmax(logits, axis=-1)
        return jnp.einsum('bhqk,bhkd->bhqd', probs.astype(vh.dtype), vh)

    out = lax.map(block, jnp.arange(nb))
    out = jnp.transpose(out, (1, 0, 3, 2, 4))
    return out.reshape(B, S, FOX_HEADS * HEAD_DIM)


def _fwd_setup_inputs(seed: int = 0) -> dict:
    key = jax.random.key(seed)
    ks = jax.random.split(key, 14)
    f32 = jnp.float32
    x = jax.random.normal(ks[0], (BATCH, SEQ, D_MODEL), f32)
    positions = jnp.broadcast_to(jnp.arange(SEQ, dtype=jnp.int32)[None, :], (BATCH, SEQ))
    attn_norm = 1.0 + 0.05 * jax.random.normal(ks[1], (DEPTH, D_MODEL), f32)
    w_in = jax.random.normal(ks[2], (DEPTH, D_MODEL, D_IN), f32) * D_MODEL ** -0.5
    fox_f_bias = jax.random.uniform(ks[3], (DEPTH, FOX_HEADS), f32, 1.0, 6.0)
    swa_sinks = 0.5 * jax.random.normal(ks[4], (DEPTH, SWA_Q_HEADS), f32)
    w_branch_swa = jax.random.normal(ks[5], (DEPTH, SWA_Q_W, D_MODEL), f32) * SWA_Q_W ** -0.5
    w_branch_fox = jax.random.normal(ks[6], (DEPTH, FOX_W, D_MODEL), f32) * FOX_W ** -0.5
    w_out = jax.random.normal(ks[7], (DEPTH, D_MODEL, D_MODEL), f32) * D_MODEL ** -0.5
    mlp_norm = 1.0 + 0.05 * jax.random.normal(ks[8], (DEPTH, D_MODEL), f32)
    w_up = jax.random.normal(ks[9], (DEPTH, D_MODEL, D_FF), f32) * D_MODEL ** -0.5
    w_down = jax.random.normal(ks[10], (DEPTH, D_FF, D_MODEL), f32) * D_FF ** -0.5
    final_norm = 1.0 + 0.05 * jax.random.normal(ks[11], (D_MODEL,), f32)
    return {"x": x, "positions": positions, "attn_norm": attn_norm, "w_in": w_in,
            "fox_f_bias": fox_f_bias, "swa_sinks": swa_sinks, "w_branch_swa": w_branch_swa,
            "w_branch_fox": w_branch_fox, "w_out": w_out, "mlp_norm": mlp_norm,
            "w_up": w_up, "w_down": w_down, "final_norm": final_norm}


def _fwd_reference(x, positions, attn_norm, w_in, fox_f_bias, swa_sinks, w_branch_swa,
              w_branch_fox, w_out, mlp_norm, w_up, w_down, final_norm):
    B, S = x.shape[0], x.shape[1]
    for l in range(DEPTH):
        h = rmsnorm(x, attn_norm[l])
        proj = jnp.einsum('bsd,de->bse', h, w_in[l])
        (a_q, a_k, a_v, f_q, f_k, f_v, f_logit, g_a, g_b) = jnp.split(proj, IN_SPLITS, axis=-1)
        a_q = apply_rope(a_q.reshape(B, S, SWA_Q_HEADS, HEAD_DIM), positions)
        a_k = apply_rope(a_k.reshape(B, S, SWA_KV_HEADS, HEAD_DIM), positions)
        a_v = a_v.reshape(B, S, SWA_KV_HEADS, HEAD_DIM)
        o_a = sliding_window_gqa_sinks(a_q, a_k, a_v, swa_sinks[l])
        log_f = jax.nn.log_sigmoid(f_logit.astype(jnp.float32) + fox_f_bias[l].astype(jnp.float32))
        o_b = forgetting_attention(f_q.reshape(B, S, FOX_HEADS, HEAD_DIM),
                                   f_k.reshape(B, S, FOX_HEADS, HEAD_DIM),
                                   f_v.reshape(B, S, FOX_HEADS, HEAD_DIM), log_f)
        merged = (jax.nn.sigmoid(g_a) * jnp.einsum('bse,ed->bsd', o_a, w_branch_swa[l])
                  + jax.nn.sigmoid(g_b) * jnp.einsum('bse,ed->bsd', o_b, w_branch_fox[l]))
        x = x + jnp.einsum('bsd,de->bse', merged, w_out[l])
        h = rmsnorm(x, mlp_norm[l])
        u = jax.nn.relu(jnp.einsum('bsd,df->bsf', h, w_up[l]))
        x = x + jnp.einsum('bsf,fd->bsd', u * u, w_down[l])
    return rmsnorm(x, final_norm)


import jax as _jax
import jax.numpy as _jnp

TWIN_FORMAT = 'train_step'
FWD_PARAMS = ['x', 'positions', 'attn_norm', 'w_in', 'fox_f_bias', 'swa_sinks', 'w_branch_swa', 'w_branch_fox', 'w_out', 'mlp_norm', 'w_up', 'w_down', 'final_norm']
TWIN_WEIGHTS = ['attn_norm', 'w_in', 'fox_f_bias', 'swa_sinks', 'w_branch_swa', 'w_branch_fox', 'w_out', 'mlp_norm', 'w_up', 'w_down', 'final_norm']
TWIN_DIFF_INPUT = 'x'
TWIN_INPUTS = ['x', 'positions', 'attn_norm', 'w_in', 'fox_f_bias', 'swa_sinks', 'w_branch_swa', 'w_branch_fox', 'w_out', 'mlp_norm', 'w_up', 'w_down', 'final_norm', 'loss_target', 'm_attn_norm', 'm_w_in', 'm_fox_f_bias', 'm_swa_sinks', 'm_w_branch_swa', 'm_w_branch_fox', 'm_w_out', 'm_mlp_norm', 'm_w_up', 'm_w_down', 'm_final_norm', 'v_attn_norm', 'v_w_in', 'v_fox_f_bias', 'v_swa_sinks', 'v_w_branch_swa', 'v_w_branch_fox', 'v_w_out', 'v_mlp_norm', 'v_w_up', 'v_w_down', 'v_final_norm']
TWIN_OUTPUTS = ['loss', 'grad_x', 'grad_attn_norm', 'grad_w_in', 'grad_fox_f_bias', 'grad_swa_sinks', 'grad_w_branch_swa', 'grad_w_branch_fox', 'grad_w_out', 'grad_mlp_norm', 'grad_w_up', 'grad_w_down', 'grad_final_norm', 'delta_attn_norm', 'delta_w_in', 'delta_fox_f_bias', 'delta_swa_sinks', 'delta_w_branch_swa', 'delta_w_branch_fox', 'delta_w_out', 'delta_mlp_norm', 'delta_w_up', 'delta_w_down', 'delta_final_norm', 'new_m_attn_norm', 'new_m_w_in', 'new_m_fox_f_bias', 'new_m_swa_sinks', 'new_m_w_branch_swa', 'new_m_w_branch_fox', 'new_m_w_out', 'new_m_mlp_norm', 'new_m_w_up', 'new_m_w_down', 'new_m_final_norm', 'new_v_attn_norm', 'new_v_w_in', 'new_v_fox_f_bias', 'new_v_swa_sinks', 'new_v_w_branch_swa', 'new_v_w_branch_fox', 'new_v_w_out', 'new_v_mlp_norm', 'new_v_w_up', 'new_v_w_down', 'new_v_final_norm']
TWIN_LEAF_KINDS = {'loss': 'loss', 'grad_x': 'grad_x', 'grad_attn_norm': 'grad_w', 'grad_w_in': 'grad_w', 'grad_fox_f_bias': 'grad_w', 'grad_swa_sinks': 'grad_w', 'grad_w_branch_swa': 'grad_w', 'grad_w_branch_fox': 'grad_w', 'grad_w_out': 'grad_w', 'grad_mlp_norm': 'grad_w', 'grad_w_up': 'grad_w', 'grad_w_down': 'grad_w', 'grad_final_norm': 'grad_w', 'delta_attn_norm': 'delta_w', 'delta_w_in': 'delta_w', 'delta_fox_f_bias': 'delta_w', 'delta_swa_sinks': 'delta_w', 'delta_w_branch_swa': 'delta_w', 'delta_w_branch_fox': 'delta_w', 'delta_w_out': 'delta_w', 'delta_mlp_norm': 'delta_w', 'delta_w_up': 'delta_w', 'delta_w_down': 'delta_w', 'delta_final_norm': 'delta_w', 'new_m_attn_norm': 'new_m', 'new_m_w_in': 'new_m', 'new_m_fox_f_bias': 'new_m', 'new_m_swa_sinks': 'new_m', 'new_m_w_branch_swa': 'new_m', 'new_m_w_branch_fox': 'new_m', 'new_m_w_out': 'new_m', 'new_m_mlp_norm': 'new_m', 'new_m_w_up': 'new_m', 'new_m_w_down': 'new_m', 'new_m_final_norm': 'new_m', 'new_v_attn_norm': 'new_v', 'new_v_w_in': 'new_v', 'new_v_fox_f_bias': 'new_v', 'new_v_swa_sinks': 'new_v', 'new_v_w_branch_swa': 'new_v', 'new_v_w_branch_fox': 'new_v', 'new_v_w_out': 'new_v', 'new_v_mlp_norm': 'new_v', 'new_v_w_up': 'new_v', 'new_v_w_down': 'new_v', 'new_v_final_norm': 'new_v'}


def _forward(args):
    return _fwd_reference(*[args[k] for k in FWD_PARAMS])


def _output_shape():
    def fwd():
        inp = _fwd_setup_inputs(0)
        return _fwd_reference(*[inp[k] for k in FWD_PARAMS])
    out = _jax.eval_shape(fwd)
    return out.shape, out.dtype

N_MICROBATCH = 1
ADAM_LR = 0.001
ADAM_B1 = 0.9
ADAM_B2 = 0.999
ADAM_EPS = 1e-08
ADAM_WD = 0.01
ADAM_STEP = 10
PER_EXAMPLE_BATCH_AXIS = {'x': 0, 'positions': 0, 'loss_target': 0}
SHARED_INPUTS = []
_WEIGHT_DTYPES = {'attn_norm': _jnp.float32, 'w_in': _jnp.float32, 'fox_f_bias': _jnp.float32, 'swa_sinks': _jnp.float32, 'w_branch_swa': _jnp.float32, 'w_branch_fox': _jnp.float32, 'w_out': _jnp.float32, 'mlp_norm': _jnp.float32, 'w_up': _jnp.float32, 'w_down': _jnp.float32, 'final_norm': _jnp.float32}
MOMENT_SCALE = {'attn_norm': 4.552435e-02, 'w_in': 2.228842e-02, 'fox_f_bias': 1.371105e-01, 'swa_sinks': 1.668048e-02, 'w_branch_swa': 1.473592e-02, 'w_branch_fox': 2.491238e-02, 'w_out': 2.898032e-02, 'mlp_norm': 1.177809e-01, 'w_up': 5.669474e-02, 'w_down': 2.226364e-01, 'final_norm': 3.231080e+01}


def _to_microbatches(a, axis):
    t = _jnp.moveaxis(a, axis, 0)
    t = t.reshape((N_MICROBATCH, t.shape[0] // N_MICROBATCH) + t.shape[1:])
    return _jnp.moveaxis(t, 1, axis + 1)


def setup_inputs(seed: int = 0) -> dict:
    inp = _fwd_setup_inputs(seed)
    key = _jax.random.fold_in(_jax.random.key(seed), 7919)
    shape, _ = _output_shape()
    out = dict(inp)
    out["loss_target"] = _jax.random.normal(_jax.random.fold_in(key, 0), shape, _jnp.float32)
    for i, name in enumerate(TWIN_WEIGHTS):
        w = inp[name].astype(_jnp.float32)
        if MOMENT_SCALE is None:
            s = _jnp.sqrt(_jnp.mean(_jnp.square(w)) + 1e-30)
        else:
            s = MOMENT_SCALE[name]
        km, kv = _jax.random.split(_jax.random.fold_in(key, i + 1))
        out[name] = w
        out["m_" + name] = s * _jax.random.normal(km, w.shape, _jnp.float32)
        out["v_" + name] = (s * s) * _jax.random.uniform(kv, w.shape, _jnp.float32, 0.5, 1.5)
    if N_MICROBATCH > 1:
        for name, axis in PER_EXAMPLE_BATCH_AXIS.items():
            out[name] = _to_microbatches(out[name], axis)
    return {'x': out['x'], 'positions': out['positions'], 'attn_norm': out['attn_norm'], 'w_in': out['w_in'], 'fox_f_bias': out['fox_f_bias'], 'swa_sinks': out['swa_sinks'], 'w_branch_swa': out['w_branch_swa'], 'w_branch_fox': out['w_branch_fox'], 'w_out': out['w_out'], 'mlp_norm': out['mlp_norm'], 'w_up': out['w_up'], 'w_down': out['w_down'], 'final_norm': out['final_norm'], 'loss_target': out['loss_target'], 'm_attn_norm': out['m_attn_norm'], 'm_w_in': out['m_w_in'], 'm_fox_f_bias': out['m_fox_f_bias'], 'm_swa_sinks': out['m_swa_sinks'], 'm_w_branch_swa': out['m_w_branch_swa'], 'm_w_branch_fox': out['m_w_branch_fox'], 'm_w_out': out['m_w_out'], 'm_mlp_norm': out['m_mlp_norm'], 'm_w_up': out['m_w_up'], 'm_w_down': out['m_w_down'], 'm_final_norm': out['m_final_norm'], 'v_attn_norm': out['v_attn_norm'], 'v_w_in': out['v_w_in'], 'v_fox_f_bias': out['v_fox_f_bias'], 'v_swa_sinks': out['v_swa_sinks'], 'v_w_branch_swa': out['v_w_branch_swa'], 'v_w_branch_fox': out['v_w_branch_fox'], 'v_w_out': out['v_w_out'], 'v_mlp_norm': out['v_mlp_norm'], 'v_w_up': out['v_w_up'], 'v_w_down': out['v_w_down'], 'v_final_norm': out['v_final_norm']}


def _loss(weights, diff, rest, loss_target):
    with _jax.named_scope("forward"):
        args = {**rest, TWIN_DIFF_INPUT: diff, **{k: w.astype(_WEIGHT_DTYPES[k]) for k, w in weights.items()}}
        y = _forward(args)
    with _jax.named_scope("loss_head"):
        err = _jnp.square(y.astype(_jnp.float32) - loss_target)
        return 0.5 * _jnp.sum(_jnp.mean(err, axis=-1)) if err.ndim else 0.5 * err


def _adamw(w, g, m, v):
    m = ADAM_B1 * m + (1.0 - ADAM_B1) * g
    v = ADAM_B2 * v + (1.0 - ADAM_B2) * _jnp.square(g)
    m_hat = m / (1.0 - ADAM_B1 ** ADAM_STEP)
    v_hat = v / (1.0 - ADAM_B2 ** ADAM_STEP)
    delta = -ADAM_LR * (m_hat / (_jnp.sqrt(v_hat) + ADAM_EPS) + ADAM_WD * w)
    return delta, m, v


def reference(x, positions, attn_norm, w_in, fox_f_bias, swa_sinks, w_branch_swa, w_branch_fox, w_out, mlp_norm, w_up, w_down, final_norm, loss_target, m_attn_norm, m_w_in, m_fox_f_bias, m_swa_sinks, m_w_branch_swa, m_w_branch_fox, m_w_out, m_mlp_norm, m_w_up, m_w_down, m_final_norm, v_attn_norm, v_w_in, v_fox_f_bias, v_swa_sinks, v_w_branch_swa, v_w_branch_fox, v_w_out, v_mlp_norm, v_w_up, v_w_down, v_final_norm):
    given = dict(x=x, positions=positions, attn_norm=attn_norm, w_in=w_in, fox_f_bias=fox_f_bias, swa_sinks=swa_sinks, w_branch_swa=w_branch_swa, w_branch_fox=w_branch_fox, w_out=w_out, mlp_norm=mlp_norm, w_up=w_up, w_down=w_down, final_norm=final_norm, loss_target=loss_target, m_attn_norm=m_attn_norm, m_w_in=m_w_in, m_fox_f_bias=m_fox_f_bias, m_swa_sinks=m_swa_sinks, m_w_branch_swa=m_w_branch_swa, m_w_branch_fox=m_w_branch_fox, m_w_out=m_w_out, m_mlp_norm=m_mlp_norm, m_w_up=m_w_up, m_w_down=m_w_down, m_final_norm=m_final_norm, v_attn_norm=v_attn_norm, v_w_in=v_w_in, v_fox_f_bias=v_fox_f_bias, v_swa_sinks=v_swa_sinks, v_w_branch_swa=v_w_branch_swa, v_w_branch_fox=v_w_branch_fox, v_w_out=v_w_out, v_mlp_norm=v_mlp_norm, v_w_up=v_w_up, v_w_down=v_w_down, v_final_norm=v_final_norm)
    weights = {n: given[n] for n in TWIN_WEIGHTS}
    shared = {n: given[n] for n in SHARED_INPUTS}
    per_example = {n: given[n] for n in ['x', 'positions']}
    grad_fn = _jax.value_and_grad(_loss, argnums=(0, 1))

    def one_microbatch(ex, loss_target):
        ex = dict(ex)
        diff = ex.pop(TWIN_DIFF_INPUT)
        return grad_fn(weights, diff, {**shared, **ex}, loss_target)

    if N_MICROBATCH == 1:
        loss, (grad_w, grad_x) = one_microbatch(per_example, given["loss_target"])
    else:
        def body(carry, xs):
            loss_sum, grad_sum = carry
            l_k, (gw_k, gx_k) = one_microbatch(xs[0], xs[1])
            with _jax.named_scope("update"):
                return (loss_sum + l_k, _jax.tree.map(_jnp.add, grad_sum, gw_k)), gx_k

        init = (_jnp.zeros((), _jnp.float32), _jax.tree.map(_jnp.zeros_like, weights))
        (loss, grad_w), grad_x = _jax.lax.scan(body, init, (per_example, given["loss_target"]))
    with _jax.named_scope("update"):
        delta_w, new_m, new_v = {}, {}, {}
        for n in TWIN_WEIGHTS:
            delta_w[n], new_m[n], new_v[n] = _adamw(weights[n], grad_w[n], given["m_" + n], given["v_" + n])
    return (loss, grad_x, *[grad_w[n] for n in TWIN_WEIGHTS], *[delta_w[n] for n in TWIN_WEIGHTS],
            *[new_m[n] for n in TWIN_WEIGHTS], *[new_v[n] for n in TWIN_WEIGHTS])
```

```python
import functools

import jax
import jax.numpy as jnp
from jax import lax
from jax.experimental import pallas as pl
from jax.experimental.pallas import tpu as pltpu

F32 = jnp.float32
BF16 = jnp.bfloat16

N_DEV = 8
HEAD_DIM = 64
SWA_Q_W = 1024
SWA_KV_W = 128
SWA_GROUP = 8
WINDOW = 128
FOX_W = 1024
FOX_HEADS = 16
QKV_W = SWA_Q_W + 2 * SWA_KV_W + 3 * FOX_W
FL_PAD = 256
ROPE_THETA = 10000.0
RMS_EPS = 1e-6
ATT_SCALE = 0.125
NEG = -1e30

ADAM_LR = 0.001
ADAM_B1 = 0.9
ADAM_B2 = 0.999
ADAM_EPS = 1e-08
ADAM_WD = 0.01
ADAM_STEP = 10

LANES = 128
VMEM_LIMIT = 56 * 1024 * 1024
STEP_BYTES = 12 * 1024 * 1024


def _cparams(*sem):
    return pltpu.CompilerParams(dimension_semantics=sem, vmem_limit_bytes=VMEM_LIMIT)


def _pick(dim, pref, align=LANES):
    best = None
    t = align
    while t <= min(dim, pref):
        if dim % t == 0:
            best = t
        t += align
    return best if best is not None else dim


_DIMS = {"nn": ((1,), (0,)), "nt": ((1,), (1,)), "tn": ((0,), (0,))}


def _matmul(a, b, *, mode, name, out_dtypes, tm, tn, tk, extras=(), extra_maps=None,
            a_fn=None, epilogue=None):
    if mode == "nn":
        (M, K), (K2, N) = a.shape, b.shape
    elif mode == "nt":
        (M, K), (N, K2) = a.shape, b.shape
    else:
        (K, M), (K2, N) = a.shape, b.shape
    assert K == K2, (name, a.shape, b.shape)
    assert M % tm == 0 and N % tn == 0 and K % tk == 0, (name, M, N, K, tm, tn, tk)
    nk = K // tk
    ne, no = len(extras), len(out_dtypes)
    dims = (_DIMS[mode], ((), ()))

    def body(*refs):
        a_ref, b_ref = refs[0], refs[1]
        ex_refs = refs[2:2 + ne]
        out_refs = refs[2 + ne:2 + ne + no]

        def finish(acc):
            res = (acc,) if epilogue is None else epilogue(acc, *[e[...] for e in ex_refs])
            for o_ref, r in zip(out_refs, res):
                o_ref[...] = r.astype(o_ref.dtype)

        av = a_ref[...]
        if a_fn is not None:
            av = a_fn(av)
        part = lax.dot_general(av, b_ref[...], dims, preferred_element_type=F32)
        if nk == 1:
            finish(part)
        else:
            acc_ref = refs[-1]
            k = pl.program_id(2)

            @pl.when(k == 0)
            def _():
                acc_ref[...] = part

            @pl.when(k > 0)
            def _():
                acc_ref[...] += part

            @pl.when(k == nk - 1)
            def _():
                finish(acc_ref[...])

    if mode == "tn":
        a_spec = pl.BlockSpec((tk, tm), lambda i, j, k: (k, i))
    else:
        a_spec = pl.BlockSpec((tm, tk), lambda i, j, k: (i, k))
    if mode == "nt":
        b_spec = pl.BlockSpec((tn, tk), lambda i, j, k: (j, k))
    else:
        b_spec = pl.BlockSpec((tk, tn), lambda i, j, k: (k, j))
    if extra_maps is None:
        extra_maps = [lambda i, j, k: (i, j)] * ne
    ex_specs = [pl.BlockSpec((tm, tn), m) for m in extra_maps]
    out_spec = [pl.BlockSpec((tm, tn), lambda i, j, k: (i, j)) for _ in range(no)]
    res = pl.pallas_call(
        body,
        name=name,
        grid=(M // tm, N // tn, nk),
        in_specs=[a_spec, b_spec] + ex_specs,
        out_specs=out_spec,
        out_shape=[jax.ShapeDtypeStruct((M, N), d) for d in out_dtypes],
        scratch_shapes=[pltpu.VMEM((tm, tn), F32)] if nk > 1 else [],
        compiler_params=_cparams("parallel", "parallel", "arbitrary"),
    )(a, b, *extras)
    return res


def _square_bf16(t):
    tf = t.astype(F32)
    return (tf * tf).astype(BF16)


def _sigmoid(g):
    return 1.0 / (1.0 + jnp.exp(-g))


def _rms_fwd(x, gain, *, name):
    S, D = x.shape
    tr = _pick(S, 512, 8)

    def body(x_ref, g_ref, h_ref):
        xv = x_ref[...]
        r = lax.rsqrt(jnp.mean(xv * xv, axis=-1, keepdims=True) + RMS_EPS)
        h_ref[...] = (xv * r * g_ref[...]).astype(BF16)

    return pl.pallas_call(
        body, name=name, grid=(S // tr,),
        in_specs=[pl.BlockSpec((tr, D), lambda i: (i, 0)), pl.BlockSpec((1, D), lambda i: (0, 0))],
        out_specs=pl.BlockSpec((tr, D), lambda i: (i, 0)),
        out_shape=jax.ShapeDtypeStruct((S, D), BF16),
        compiler_params=_cparams("parallel"),
    )(x, gain)


def _rms_bwd(dh, x, gain, dres, *, name, want_bf16):
    S, D = x.shape
    tr = _pick(S, 256, 8)

    def body(dh_ref, x_ref, g_ref, dres_ref, *outs):
        dx_ref, dg_ref = outs[0], outs[-1]
        xv = x_ref[...]
        r = lax.rsqrt(jnp.mean(xv * xv, axis=-1, keepdims=True) + RMS_EPS)
        xh = xv * r
        dhv = dh_ref[...]
        t = dhv * g_ref[...]
        dx = r * (t - xh * jnp.mean(t * xh, axis=-1, keepdims=True)) + dres_ref[...]
        dx_ref[...] = dx
        if want_bf16:
            outs[1][...] = dx.astype(BF16)
        part = jnp.sum(dhv * xh, axis=0, keepdims=True)

        @pl.when(pl.program_id(0) == 0)
        def _():
            dg_ref[...] = part

        @pl.when(pl.program_id(0) > 0)
        def _():
            dg_ref[...] += part

    row = pl.BlockSpec((tr, D), lambda i: (i, 0))
    vec = pl.BlockSpec((1, D), lambda i: (0, 0))
    out_shape = [jax.ShapeDtypeStruct((S, D), F32)]
    out_specs = [row]
    if want_bf16:
        out_shape.append(jax.ShapeDtypeStruct((S, D), BF16))
        out_specs.append(row)
    out_shape.append(jax.ShapeDtypeStruct((1, D), F32))
    out_specs.append(vec)
    return pl.pallas_call(
        body, name=name, grid=(S // tr,),
        in_specs=[row, row, vec, row], out_specs=out_specs, out_shape=out_shape,
        compiler_params=_cparams("arbitrary"),
    )(dh, x, gain, dres)


def _loss_head(x3, target, gain, *, name):
    S, D = x3.shape
    tr = _pick(S, 256, 8)

    def body(x_ref, t_ref, g_ref, dx_ref, dxb_ref, dg_ref, loss_ref):
        xv = x_ref[...]
        r = lax.rsqrt(jnp.mean(xv * xv, axis=-1, keepdims=True) + RMS_EPS)
        xh = xv * r
        gv = g_ref[...]
        err = xh * gv - t_ref[...]
        lpart = jnp.zeros((1, LANES), F32) + (0.5 / D) * jnp.sum(err * err)
        dy = err * (1.0 / D)
        t = dy * gv
        dx = r * (t - xh * jnp.mean(t * xh, axis=-1, keepdims=True))
        dx_ref[...] = dx
        dxb_ref[...] = dx.astype(BF16)
        part = jnp.sum(dy * xh, axis=0, keepdims=True)

        @pl.when(pl.program_id(0) == 0)
        def _():
            dg_ref[...] = part
            loss_ref[...] = lpart

        @pl.when(pl.program_id(0) > 0)
        def _():
            dg_ref[...] += part
            loss_ref[...] += lpart

    row = pl.BlockSpec((tr, D), lambda i: (i, 0))
    vec = pl.BlockSpec((1, D), lambda i: (0, 0))
    return pl.pallas_call(
        body, name=name, grid=(S // tr,),
        in_specs=[row, row, vec],
        out_specs=[row, row, vec, pl.BlockSpec((1, LANES), lambda i: (0, 0))],
        out_shape=[jax.ShapeDtypeStruct((S, D), F32), jax.ShapeDtypeStruct((S, D), BF16),
                   jax.ShapeDtypeStruct((1, D), F32), jax.ShapeDtypeStruct((1, LANES), F32)],
        compiler_params=_cparams("arbitrary"),
    )(x3, target, gain)


def _rope_tables(pos_col, invf, *, name):
    S = pos_col.shape[0]
    tr = _pick(S, 512, 8)

    def body(p_ref, f_ref, cos_ref, sin_ref):
        ang = p_ref[...].astype(F32) * f_ref[...]
        lane = lax.broadcasted_iota(jnp.int32, (1, LANES), 1)
        first = (lane % HEAD_DIM) < HEAD_DIM // 2
        sn = jnp.sin(ang)
        cos_ref[...] = jnp.cos(ang)
        sin_ref[...] = jnp.where(first, -sn, sn)

    return pl.pallas_call(
        body, name=name, grid=(S // tr,),
        in_specs=[pl.BlockSpec((tr, 1), lambda i: (i, 0)), pl.BlockSpec((1, LANES), lambda i: (0, 0))],
        out_specs=[pl.BlockSpec((tr, LANES), lambda i: (i, 0))] * 2,
        out_shape=[jax.ShapeDtypeStruct((S, LANES), F32)] * 2,
        compiler_params=_cparams("parallel"),
    )(pos_col, invf)


def _swap_halves(t):
    lane = lax.broadcasted_iota(jnp.int32, (1, LANES), 1)
    first = (lane % HEAD_DIM) < HEAD_DIM // 2
    return jnp.where(first, pltpu.roll(t, LANES - HEAD_DIM // 2, 1), pltpu.roll(t, HEAD_DIM // 2, 1))


def _rope_fwd(proj, cos_t, sin_t, *, q_off, k_off, name):
    S = proj.shape[0]
    tr = _pick(S, 256, 8)
    nqb = SWA_Q_W // LANES

    def body(q_ref, k_ref, c_ref, s_ref, qo_ref, ko_ref):
        cv, sv = c_ref[...], s_ref[...]
        for b in range(nqb):
            t = q_ref[:, b * LANES:(b + 1) * LANES].astype(F32)
            qo_ref[:, b * LANES:(b + 1) * LANES] = (t * cv + _swap_halves(t) * sv).astype(BF16)
        t = k_ref[...].astype(F32)
        ko_ref[...] = (t * cv + _swap_halves(t) * sv).astype(BF16)

    tab = pl.BlockSpec((tr, LANES), lambda i: (i, 0))
    return pl.pallas_call(
        body, name=name, grid=(S // tr,),
        in_specs=[pl.BlockSpec((tr, SWA_Q_W), lambda i: (i, q_off // SWA_Q_W)),
                  pl.BlockSpec((tr, LANES), lambda i: (i, k_off // LANES)), tab, tab],
        out_specs=[pl.BlockSpec((tr, SWA_Q_W), lambda i: (i, 0)), tab],
        out_shape=[jax.ShapeDtypeStruct((S, SWA_Q_W), BF16), jax.ShapeDtypeStruct((S, LANES), BF16)],
        compiler_params=_cparams("parallel"),
    )(proj, proj, cos_t, sin_t)


def _rope_bwd(dq, dk_cur, dk_prev, dv_cur, dv_prev, cos_t, sin_t, *, name):
    S = dq.shape[0]
    tr = WINDOW
    nb = S // tr
    nqb = SWA_Q_W // LANES

    def body(dq_ref, kc_ref, kp_ref, vc_ref, vp_ref, c_ref, s_ref, dqo_ref, dko_ref, dvo_ref):
        cv, sv = c_ref[...], s_ref[...]
        has_next = (pl.program_id(0) + 1 < nb).astype(F32)
        for b in range(nqb):
            d = dq_ref[:, b * LANES:(b + 1) * LANES]
            dqo_ref[:, b * LANES:(b + 1) * LANES] = (d * cv + _swap_halves(d * sv)).astype(BF16)
        d = kc_ref[0] + kc_ref[1] + has_next * (kp_ref[0] + kp_ref[1])
        dko_ref[...] = (d * cv + _swap_halves(d * sv)).astype(BF16)
        dvo_ref[...] = (vc_ref[0] + vc_ref[1] + has_next * (vp_ref[0] + vp_ref[1])).astype(BF16)

    tab = pl.BlockSpec((tr, LANES), lambda i: (i, 0))
    cur = pl.BlockSpec((2, tr, LANES), lambda i: (0, i, 0))
    nxt = pl.BlockSpec((2, tr, LANES), lambda i: (0, jnp.minimum(i + 1, nb - 1), 0))
    return pl.pallas_call(
        body, name=name, grid=(nb,),
        in_specs=[pl.BlockSpec((tr, SWA_Q_W), lambda i: (i, 0)), cur, nxt, cur, nxt, tab, tab],
        out_specs=[pl.BlockSpec((tr, SWA_Q_W), lambda i: (i, 0)), tab, tab],
        out_shape=[jax.ShapeDtypeStruct((S, SWA_Q_W), BF16), jax.ShapeDtypeStruct((S, LANES), BF16),
                   jax.ShapeDtypeStruct((S, LANES), BF16)],
        compiler_params=_cparams("parallel"),
    )(dq, dk_cur, dk_prev, dv_cur, dv_prev, cos_t, sin_t)


def _dot_nt(a, b):
    return lax.dot_general(a, b, (((1,), (1,)), ((), ())), preferred_element_type=F32)


def _dot_tn(a, b):
    return lax.dot_general(a, b, (((0,), (0,)), ((), ())), preferred_element_type=F32)


def _dot_nn(a, b):
    return lax.dot_general(a, b, (((1,), (0,)), ((), ())), preferred_element_type=F32)


def _roll_half(t):
    return pltpu.roll(t.astype(F32), HEAD_DIM, 1).astype(t.dtype)


def _swa_common(hk, n, kp_ref, kc_ref, vp_ref, vc_ref):
    k2 = jnp.concatenate([kp_ref[...], kc_ref[...]], axis=0)
    v2 = jnp.concatenate([vp_ref[...], vc_ref[...]], axis=0)
    k_sw, v_sw = _roll_half(k2), _roll_half(v2)
    row = lax.broadcasted_iota(jnp.int32, (WINDOW, 2 * WINDOW), 0)
    col = lax.broadcasted_iota(jnp.int32, (WINDOW, 2 * WINDOW), 1)
    diff = row + WINDOW - col
    allowed = (diff >= 0) & (diff < WINDOW) & ((col >= WINDOW) | (n > 0))
    lane = lax.broadcasted_iota(jnp.int32, (1, LANES), 1)
    half = [lane < HEAD_DIM, lane >= HEAD_DIM]
    kk = [jnp.where(hk == a, k2, k_sw) for a in range(2)]
    vv = [jnp.where(hk == a, v2, v_sw) for a in range(2)]
    return allowed, half, kk, vv


def _swa_probs(qm, kk, allowed, sink):
    s = jnp.where(allowed, _dot_nt(qm, kk), NEG)
    m = jnp.maximum(jnp.max(s, axis=1, keepdims=True), sink)
    e = jnp.exp(s - m)
    es = jnp.exp(sink - m)
    inv = 1.0 / (jnp.sum(e, axis=1, keepdims=True) + es)
    return e * inv, es * inv


def _swa_fwd(q_rope, k_rope, proj, sinks, *, v_off, name):
    S = q_rope.shape[0]
    nb = S // WINDOW
    gw = SWA_GROUP * HEAD_DIM

    def body(sink_ref, q_ref, kp_ref, kc_ref, vp_ref, vc_ref, o_ref):
        hk, n = pl.program_id(0), pl.program_id(1)
        allowed, half, kk, vv = _swa_common(hk, n, kp_ref, kc_ref, vp_ref, vc_ref)
        for t in range(SWA_GROUP // 2):
            qp = q_ref[:, t * LANES:(t + 1) * LANES] * jnp.asarray(ATT_SCALE, BF16)
            outs = []
            for a in range(2):
                qm = jnp.where(half[a], qp, jnp.zeros_like(qp))
                p, _ = _swa_probs(qm, kk[a], allowed, sink_ref[hk * SWA_GROUP + 2 * t + a])
                outs.append(_dot_nn(p.astype(BF16), vv[a]))
            o_ref[:, t * LANES:(t + 1) * LANES] = jnp.where(half[0], outs[0], outs[1]).astype(BF16)

    prev = lambda hk, n: (jnp.maximum(n - 1, 0), 0)
    cur = lambda hk, n: (n, 0)
    vprev = lambda hk, n: (jnp.maximum(n - 1, 0), v_off // LANES)
    vcur = lambda hk, n: (n, v_off // LANES)
    blk = lambda m: pl.BlockSpec((WINDOW, LANES), m)
    return pl.pallas_call(
        body, name=name, grid=(2, nb),
        in_specs=[pl.BlockSpec(memory_space=pltpu.SMEM),
                  pl.BlockSpec((WINDOW, gw), lambda hk, n: (n, hk)),
                  blk(prev), blk(cur), blk(vprev), blk(vcur)],
        out_specs=pl.BlockSpec((WINDOW, gw), lambda hk, n: (n, hk)),
        out_shape=jax.ShapeDtypeStruct((S, SWA_Q_W), BF16),
        compiler_params=_cparams("parallel", "parallel"),
    )(sinks, q_rope, k_rope, k_rope, proj, proj)


def _swa_bwd(q_rope, k_rope, proj, sinks, d_o, *, v_off, name):
    S = q_rope.shape[0]
    nb = S // WINDOW
    gw = SWA_GROUP * HEAD_DIM

    def body(sink_ref, q_ref, kp_ref, kc_ref, vp_ref, vc_ref, do_ref,
             dq_ref, dkc_ref, dkp_ref, dvc_ref, dvp_ref, dsink_ref):
        hk, n = pl.program_id(0), pl.program_id(1)
        allowed, half, kk, vv = _swa_common(hk, n, kp_ref, kc_ref, vp_ref, vc_ref)
        dk_acc = jnp.zeros((2 * WINDOW, LANES), F32)
        dv_acc = jnp.zeros((2 * WINDOW, LANES), F32)
        srow = lax.broadcasted_iota(jnp.int32, (SWA_GROUP, LANES), 0)
        dsink = jnp.zeros((SWA_GROUP, LANES), F32)
        for t in range(SWA_GROUP // 2):
            qp = q_ref[:, t * LANES:(t + 1) * LANES] * jnp.asarray(ATT_SCALE, BF16)
            dop = do_ref[:, t * LANES:(t + 1) * LANES]
            dqs = []
            for a in range(2):
                g = 2 * t + a
                qm = jnp.where(half[a], qp, jnp.zeros_like(qp))
                dom = jnp.where(half[a], dop, jnp.zeros_like(dop))
                p, psink = _swa_probs(qm, kk[a], allowed, sink_ref[hk * SWA_GROUP + g])
                dp = _dot_nt(dom, vv[a])
                delta = jnp.sum(p * dp, axis=1, keepdims=True)
                ds = (p * (dp - delta)).astype(BF16)
                dsink = dsink + jnp.where(srow == g, -jnp.sum(psink * delta), 0.0)
                dqs.append(_dot_nn(ds, kk[a]) * ATT_SCALE)
                dk_acc = dk_acc + _dot_tn(ds, qm)
                dv_acc = dv_acc + _dot_tn(p.astype(BF16), dom)
            dq_ref[:, t * LANES:(t + 1) * LANES] = jnp.where(half[0], dqs[0], dqs[1])
        lane = lax.broadcasted_iota(jnp.int32, (1, LANES), 1)
        mine = (lane >= HEAD_DIM) == (hk == 1)
        dk_t = jnp.where(mine, dk_acc + pltpu.roll(dk_acc, HEAD_DIM, 1), 0.0)
        dv_t = jnp.where(mine, dv_acc + pltpu.roll(dv_acc, HEAD_DIM, 1), 0.0)
        dkp_ref[0] = dk_t[:WINDOW]
        dkc_ref[0] = dk_t[WINDOW:]
        dvp_ref[0] = dv_t[:WINDOW]
        dvc_ref[0] = dv_t[WINDOW:]

        @pl.when(n == 0)
        def _():
            dsink_ref[0] = dsink

        @pl.when(n > 0)
        def _():
            dsink_ref[0] += dsink

    prev = lambda hk, n: (jnp.maximum(n - 1, 0), 0)
    cur = lambda hk, n: (n, 0)
    vprev = lambda hk, n: (jnp.maximum(n - 1, 0), v_off // LANES)
    vcur = lambda hk, n: (n, v_off // LANES)
    blk = lambda m: pl.BlockSpec((WINDOW, LANES), m)
    qblk = pl.BlockSpec((WINDOW, gw), lambda hk, n: (n, hk))
    part = pl.BlockSpec((1, WINDOW, LANES), lambda hk, n: (hk, n, 0))
    part_shape = jax.ShapeDtypeStruct((2, S, LANES), F32)
    return pl.pallas_call(
        body, name=name, grid=(2, nb),
        in_specs=[pl.BlockSpec(memory_space=pltpu.SMEM), qblk, blk(prev), blk(cur), blk(vprev), blk(vcur), qblk],
        out_specs=[qblk, part, part, part, part,
                   pl.BlockSpec((1, SWA_GROUP, LANES), lambda hk, n: (hk, 0, 0))],
        out_shape=[jax.ShapeDtypeStruct((S, SWA_Q_W), F32), part_shape, part_shape, part_shape, part_shape,
                   jax.ShapeDtypeStruct((2, SWA_GROUP, LANES), F32)],
        compiler_params=_cparams("parallel", "arbitrary"),
    )(sinks, q_rope, k_rope, k_rope, proj, proj, d_o)


def _fox_prep(z_t, bias_col, *, name):
    H, S = z_t.shape
    tb = _pick(S, 512)

    def body(z_ref, b_ref, o_ref, carry_ref):
        @pl.when(pl.program_id(0) == 0)
        def _():
            carry_ref[...] = jnp.zeros_like(carry_ref)

        zz = z_ref[...] + b_ref[...]
        t = jnp.exp(-jnp.abs(zz))
        log1p = jnp.where(t < 1e-2, t * (1.0 - t * (0.5 - t * (1.0 / 3.0))), jnp.log(1.0 + t))
        logf = jnp.minimum(zz, 0.0) - log1p
        r = lax.broadcasted_iota(jnp.int32, (tb, tb), 0)
        c = lax.broadcasted_iota(jnp.int32, (tb, tb), 1)
        tri = (r <= c).astype(BF16)
        hi = logf.astype(BF16)
        r1 = logf - hi.astype(F32)
        mid = r1.astype(BF16)
        lo = (r1 - mid.astype(F32)).astype(BF16)
        cs = _dot_nn(hi, tri) + _dot_nn(mid, tri) + _dot_nn(lo, tri) + carry_ref[:, 0:1]
        o_ref[...] = -cs
        carry_ref[...] = jnp.zeros_like(carry_ref) + cs[:, tb - 1:tb]

    return pl.pallas_call(
        body, name=name, grid=(S // tb,),
        in_specs=[pl.BlockSpec((H, tb), lambda i: (0, i)), pl.BlockSpec((H, 1), lambda i: (0, 0))],
        out_specs=pl.BlockSpec((H, tb), lambda i: (0, i)),
        out_shape=jax.ShapeDtypeStruct((H, S), F32),
        scratch_shapes=[pltpu.VMEM((H, LANES), F32)],
        compiler_params=_cparams("arbitrary"),
    )(z_t, bias_col)


def _fox_post(drow, dcol, z_t, bias_col, *, name):
    H, S = z_t.shape
    tb = _pick(S, 512)
    nb = S // tb

    def body(dr_ref, d_ref, z_ref, b_ref, dz_ref, db_ref, carry_ref):
        @pl.when(pl.program_id(0) == 0)
        def _():
            carry_ref[...] = jnp.zeros_like(carry_ref)
            db_ref[...] = jnp.zeros_like(db_ref)

        dc = dr_ref[...] - d_ref[...]
        r = lax.broadcasted_iota(jnp.int32, (tb, tb), 0)
        c = lax.broadcasted_iota(jnp.int32, (tb, tb), 1)
        tri = (r >= c).astype(BF16)
        hi = dc.astype(BF16)
        r1 = dc - hi.astype(F32)
        mid = r1.astype(BF16)
        lo = (r1 - mid.astype(F32)).astype(BF16)
        dlogf = _dot_nn(hi, tri) + _dot_nn(mid, tri) + _dot_nn(lo, tri) + carry_ref[:, 0:1]
        carry_ref[...] = jnp.zeros_like(carry_ref) + dlogf[:, 0:1]
        dz = dlogf * _sigmoid(-(z_ref[...] + b_ref[...]))
        dz_ref[...] = dz
        db_ref[...] += jnp.sum(dz, axis=1, keepdims=True)

    rev = lambda i: (0, nb - 1 - i)
    return pl.pallas_call(
        body, name=name, grid=(nb,),
        in_specs=[pl.BlockSpec((H, tb), rev), pl.BlockSpec((H, tb), rev), pl.BlockSpec((H, tb), rev),
                  pl.BlockSpec((H, 1), lambda i: (0, 0))],
        out_specs=[pl.BlockSpec((H, tb), rev), pl.BlockSpec((H, LANES), lambda i: (0, 0))],
        out_shape=[jax.ShapeDtypeStruct((H, S), F32), jax.ShapeDtypeStruct((H, LANES), F32)],
        scratch_shapes=[pltpu.VMEM((H, LANES), F32)],
        compiler_params=_cparams("arbitrary"),
    )(drow, dcol, z_t, bias_col)


def _fox_block(S):
    return min(512, max(LANES, S // 4))


def _fox_fwd(proj, negc4, *, q_off, k_off, v_off, name):
    S = proj.shape[0]
    bq = _fox_block(S)
    nq = S // bq
    npair = FOX_HEADS // 2

    def body(q_ref, k_ref, v_ref, nc_ref, o_ref, lse_ref):
        i = pl.program_id(1)
        lane = lax.broadcasted_iota(jnp.int32, (1, LANES), 1)
        half = [lane < HEAD_DIM, lane >= HEAD_DIM]
        q2 = q_ref[...] * jnp.asarray(ATT_SCALE, BF16)
        qh = [jnp.where(half[h], q2, jnp.zeros_like(q2)) for h in range(2)]
        row = lax.broadcasted_iota(jnp.int32, (bq, bq), 0)
        col = lax.broadcasted_iota(jnp.int32, (bq, bq), 1)
        causal = row >= col

        def step(j, carry, masked):
            start = pl.multiple_of(j * bq, bq)
            ks = k_ref[pl.ds(start, bq), :]
            vs = v_ref[pl.ds(start, bq), :]
            nb = nc_ref[0, j]
            new = []
            for h in range(2):
                m, l, acc = carry[3 * h:3 * h + 3]
                s = _dot_nt(qh[h], ks) + nb[h:h + 1, :]
                if masked:
                    s = jnp.where(causal, s, NEG)
                m_new = jnp.maximum(m, jnp.max(s, axis=1, keepdims=True))
                alpha = jnp.exp(m - m_new)
                p = jnp.exp(s - m_new)
                l = alpha * l + jnp.sum(p, axis=1, keepdims=True)
                acc = alpha * acc + _dot_nn(p.astype(BF16), vs)
                new += [m_new, l, acc]
            return tuple(new)

        init = (jnp.full((bq, 1), NEG, F32), jnp.zeros((bq, 1), F32), jnp.zeros((bq, LANES), F32)) * 2
        carry = lax.fori_loop(0, i, lambda j, c: step(j, c, False), init)
        carry = step(i, carry, True)
        outs, lses = [], []
        for h in range(2):
            m, l, acc = carry[3 * h:3 * h + 3]
            outs.append(acc * (1.0 / l))
            lses.append(m + jnp.log(l))
        o_ref[...] = jnp.where(half[0], outs[0], outs[1]).astype(BF16)
        lse_ref[0] = jnp.where(half[0], lses[0], lses[1])

    seq = lambda off: pl.BlockSpec((S, LANES), lambda hp, i: (0, off // LANES + hp))
    return pl.pallas_call(
        body, name=name, grid=(npair, nq),
        in_specs=[pl.BlockSpec((bq, LANES), lambda hp, i: (i, q_off // LANES + hp)), seq(k_off), seq(v_off),
                  pl.BlockSpec((1, nq, 2, bq), lambda hp, i: (hp, 0, 0, 0))],
        out_specs=[pl.BlockSpec((bq, LANES), lambda hp, i: (i, hp)),
                   pl.BlockSpec((1, bq, LANES), lambda hp, i: (hp, i, 0))],
        out_shape=[jax.ShapeDtypeStruct((S, FOX_W), BF16), jax.ShapeDtypeStruct((npair, S, LANES), F32)],
        compiler_params=_cparams("parallel", "parallel"),
    )(proj, proj, proj, negc4)


def _fox_bwd(proj, negc4, o, lse, d_o, *, q_off, k_off, v_off, name):
    S = proj.shape[0]
    bq = _fox_block(S)
    nq = S // bq
    npair = FOX_HEADS // 2

    def body(q_ref, k_ref, v_ref, nc_ref, o_ref, lse_ref, do_ref, dq_ref, dk_ref, dv_ref, dn_ref, dr_ref):
        j = pl.program_id(1)
        lane = lax.broadcasted_iota(jnp.int32, (1, LANES), 1)
        half = [lane < HEAD_DIM, lane >= HEAD_DIM]
        k2, v2 = k_ref[...], v_ref[...]
        kh = [jnp.where(half[h], k2, jnp.zeros_like(k2)) for h in range(2)]
        nb = nc_ref[0, 0]
        row = lax.broadcasted_iota(jnp.int32, (bq, bq), 0)
        col = lax.broadcasted_iota(jnp.int32, (bq, bq), 1)
        causal = row >= col

        @pl.when(j == 0)
        def _():
            dq_ref[...] = jnp.zeros_like(dq_ref)
            dr_ref[...] = jnp.zeros_like(dr_ref)

        def step(i, carry, masked):
            dk_acc, dv_acc, dn0, dn1 = carry
            start = pl.multiple_of(i * bq, bq)
            q2 = q_ref[pl.ds(start, bq), :] * jnp.asarray(ATT_SCALE, BF16)
            do2 = do_ref[pl.ds(start, bq), :]
            of = o_ref[pl.ds(start, bq), :].astype(F32)
            lse2 = lse_ref[0, pl.ds(start, bq), :]
            dq_new = jnp.zeros((bq, LANES), F32)
            dns = [dn0, dn1]
            rsum = []
            for h in range(2):
                qm = jnp.where(half[h], q2, jnp.zeros_like(q2))
                dom = jnp.where(half[h], do2, jnp.zeros_like(do2))
                lse_h = lse2[:, h * HEAD_DIM:h * HEAD_DIM + 1]
                p = jnp.exp(_dot_nt(qm, k2) + nb[h:h + 1, :] - lse_h)
                if masked:
                    p = jnp.where(causal, p, 0.0)
                dp = _dot_nt(dom, v2)
                delta = jnp.sum(dom.astype(F32) * of, axis=1, keepdims=True)
                ds = p * (dp - delta)
                dsb = ds.astype(BF16)
                dns[h] = dns[h] + jnp.sum(ds, axis=0, keepdims=True)
                rsum.append(jnp.sum(ds, axis=1, keepdims=True))
                dv_acc = dv_acc + _dot_tn(p.astype(BF16), dom)
                dk_acc = dk_acc + _dot_tn(dsb, qm)
                dq_new = dq_new + _dot_nn(dsb, kh[h])
            dq_ref[pl.ds(start, bq), :] += dq_new * ATT_SCALE
            dr_ref[0, pl.ds(start, bq), :] += jnp.where(half[0], rsum[0], rsum[1])
            return dk_acc, dv_acc, dns[0], dns[1]

        init = (jnp.zeros((bq, LANES), F32), jnp.zeros((bq, LANES), F32),
                jnp.zeros((1, bq), F32), jnp.zeros((1, bq), F32))
        carry = step(j, init, True)
        dk_acc, dv_acc, dn0, dn1 = lax.fori_loop(j + 1, nq, lambda i, c: step(i, c, False), carry)
        dk_ref[...] = dk_acc.astype(BF16)
        dv_ref[...] = dv_acc.astype(BF16)
        dn_ref[0, 0] = jnp.concatenate([dn0, dn1], axis=0)

    seq = lambda off: pl.BlockSpec((S, LANES), lambda hp, j: (0, off // LANES + hp))
    blk = lambda off: pl.BlockSpec((bq, LANES), lambda hp, j: (j, off // LANES + hp))
    nc = pl.BlockSpec((1, 1, 2, bq), lambda hp, j: (hp, j, 0, 0))
    return pl.pallas_call(
        body, name=name, grid=(npair, nq),
        in_specs=[seq(q_off), blk(k_off), blk(v_off), nc, seq(0),
                  pl.BlockSpec((1, S, LANES), lambda hp, j: (hp, 0, 0)), seq(0)],
        out_specs=[seq(0), blk(0), blk(0), nc, pl.BlockSpec((1, S, LANES), lambda hp, j: (hp, 0, 0))],
        out_shape=[jax.ShapeDtypeStruct((S, FOX_W), F32), jax.ShapeDtypeStruct((S, FOX_W), BF16),
                   jax.ShapeDtypeStruct((S, FOX_W), BF16), jax.ShapeDtypeStruct((npair, nq, 2, bq), F32),
                   jax.ShapeDtypeStruct((npair, S, LANES), F32)],
        compiler_params=_cparams("parallel", "arbitrary"),
    )(proj, proj, proj, negc4, o, lse, d_o)


def _exchange(arrs, *, gather, name):
    n = len(arrs)
    npeer = N_DEV - 1

    def body(*refs):
        ins, outs = refs[:n], refs[n:2 * n]
        send_sems, recv_sems, loc_sems = refs[2 * n:]
        x, y, c = lax.axis_index("x"), lax.axis_index("y"), lax.axis_index("c")
        me = 4 * x + 2 * y + c
        peers = []
        for k in range(1, N_DEV):
            px = 1 - x if k & 4 else x
            py = 1 - y if k & 2 else y
            pc = 1 - c if k & 1 else c
            peers.append(((px, py, pc), 4 * px + 2 * py + pc))

        def remote(w, k):
            dev, idx = peers[k]
            src = ins[w] if gather else ins[w].at[idx]
            return pltpu.make_async_remote_copy(
                src_ref=src, dst_ref=outs[w].at[me],
                send_sem=send_sems.at[w * npeer + k], recv_sem=recv_sems.at[w * npeer + k],
                device_id=dev, device_id_type=pl.DeviceIdType.MESH)

        def arrival(w, k):
            dev, idx = peers[k]
            src = ins[w] if gather else ins[w].at[idx]
            return pltpu.make_async_remote_copy(
                src_ref=src, dst_ref=outs[w].at[idx],
                send_sem=send_sems.at[w * npeer + k], recv_sem=recv_sems.at[w * npeer + k],
                device_id=dev, device_id_type=pl.DeviceIdType.MESH)

        local = []
        for w in range(n):
            for k in range(npeer):
                remote(w, k).start()
            cp = pltpu.make_async_copy(ins[w] if gather else ins[w].at[me], outs[w].at[me], loc_sems.at[w])
            cp.start()
            local.append(cp)
        for w in range(n):
            for k in range(npeer):
                arrival(w, k).wait_recv()
        for w in range(n):
            for k in range(npeer):
                remote(w, k).wait_send()
            local[w].wait()

    hbm = pl.BlockSpec(memory_space=pl.ANY)
    out_shape = [jax.ShapeDtypeStruct((N_DEV,) + (a.shape if gather else a.shape[1:]), a.dtype) for a in arrs]
    return pl.pallas_call(
        body, name=name,
        in_specs=[hbm] * n, out_specs=[hbm] * n, out_shape=out_shape,
        scratch_shapes=[pltpu.SemaphoreType.DMA((n * npeer,)), pltpu.SemaphoreType.DMA((n * npeer,)),
                        pltpu.SemaphoreType.DMA((n,))],
        compiler_params=pltpu.CompilerParams(has_side_effects=True),
    )(*arrs)


def _sum8(stack, *, name):
    _, R, C = stack.shape
    if R % 8 == 0:
        tr, tc = _pick(R, max(8, STEP_BYTES // (C * 4 * (N_DEV + 1))), 8), C
    else:
        tr, tc = R, _pick(C, max(LANES, STEP_BYTES // (R * 4 * (N_DEV + 1))))

    def body(s_ref, o_ref):
        acc = s_ref[0]
        for q in range(1, N_DEV):
            acc = acc + s_ref[q]
        o_ref[...] = acc

    return pl.pallas_call(
        body, name=name, grid=(R // tr, C // tc),
        in_specs=[pl.BlockSpec((N_DEV, tr, tc), lambda i, j: (0, i, j))],
        out_specs=pl.BlockSpec((tr, tc), lambda i, j: (i, j)),
        out_shape=jax.ShapeDtypeStruct((R, C), F32),
        compiler_params=_cparams("parallel", "parallel"),
    )(stack)


def _adamw_math(w, g, m, v):
    m = ADAM_B1 * m + (1.0 - ADAM_B1) * g
    v = ADAM_B2 * v + (1.0 - ADAM_B2) * (g * g)
    m_hat = m / (1.0 - ADAM_B1 ** ADAM_STEP)
    v_hat = v / (1.0 - ADAM_B2 ** ADAM_STEP)
    delta = -ADAM_LR * (m_hat / (jnp.sqrt(v_hat) + ADAM_EPS) + ADAM_WD * w)
    return delta, m, v


def _adamw(w, g, m, v, *, name, stacked):
    R, C = w.shape
    tr = _pick(R, max(8, STEP_BYTES // (C * 4 * (7 + (N_DEV if stacked else 1)))), 8)

    def body(w_ref, g_ref, m_ref, v_ref, go_ref, d_ref, mo_ref, vo_ref):
        if stacked:
            g = g_ref[0]
            for q in range(1, N_DEV):
                g = g + g_ref[q]
        else:
            g = g_ref[...]
        delta, m2, v2 = _adamw_math(w_ref[...], g, m_ref[...], v_ref[...])
        go_ref[...] = g
        d_ref[...] = delta
        mo_ref[...] = m2
        vo_ref[...] = v2

    row = pl.BlockSpec((tr, C), lambda i: (i, 0))
    g_spec = pl.BlockSpec((N_DEV, tr, C), lambda i: (0, i, 0)) if stacked else row
    return pl.pallas_call(
        body, name=name, grid=(R // tr,),
        in_specs=[row, g_spec, row, row], out_specs=[row] * 4,
        out_shape=[jax.ShapeDtypeStruct((R, C), F32)] * 4,
        compiler_params=_cparams("parallel"),
    )(w, g, m, v)


def kernel(x, positions, attn_norm, w_in, fox_f_bias, swa_sinks, w_branch_swa, w_branch_fox, w_out, mlp_norm, w_up, w_down, final_norm, loss_target, m_attn_norm, m_w_in, m_fox_f_bias, m_swa_sinks, m_w_branch_swa, m_w_branch_fox, m_w_out, m_mlp_norm, m_w_up, m_w_down, m_final_norm, v_attn_norm, v_w_in, v_fox_f_bias, v_swa_sinks, v_w_branch_swa, v_w_branch_fox, v_w_out, v_mlp_norm, v_w_up, v_w_down, v_final_norm):
    S, D = x.shape[1], x.shape[2]
    DFF = w_up.shape[2] * N_DEV
    d_in = w_in.shape[2] * N_DEV
    assert d_in == QKV_W + FOX_HEADS + 2 * D and (2 * D) % SWA_Q_W == 0 and S % (4 * LANES) == 0
    q_off = 2 * D
    k_off = q_off + SWA_Q_W
    v_off = k_off + SWA_KV_W
    fq_off = v_off + SWA_KV_W
    fk_off = fq_off + FOX_W
    fv_off = fk_off + FOX_W
    fl_off = fv_off + FOX_W
    NP = fl_off + FL_PAD
    x2d, tgt = x[0], loss_target[0]

    shards = [w_in[0].T.astype(BF16), w_branch_swa[0].T.astype(BF16), w_branch_fox[0].T.astype(BF16),
              w_out[0].astype(BF16), w_up[0].T.astype(BF16), w_down[0].astype(BF16)]
    g_in, g_bs, g_bf, g_o, g_up, g_dn = _exchange(shards, gather=True, name="gather_weights")
    w_in_t = g_in.reshape(d_in, D)
    w_in_p = jnp.concatenate([w_in_t[QKV_W + FOX_HEADS:], w_in_t[:QKV_W], w_in_t[QKV_W:QKV_W + FOX_HEADS],
                              jnp.zeros((FL_PAD - FOX_HEADS, D), BF16)], axis=0)
    w_fl_t = w_in_t[QKV_W:QKV_W + FOX_HEADS]
    w_bs_t = g_bs.reshape(D, SWA_Q_W)
    w_bf_t = g_bf.reshape(D, FOX_W)
    w_o = g_o.reshape(D, D)
    w_up_t = g_up.reshape(DFF, D)
    w_dn = g_dn.reshape(DFF, D)

    tm = _pick(S, 1024)
    td = _pick(D, 1024)
    tf = _pick(DFF, 1024)
    tnp = _pick(NP, 1024)

    h1 = _rms_fwd(x2d, attn_norm, name="rms1")
    proj, = _matmul(h1, w_in_p, mode="nt", name="mm_in", out_dtypes=[BF16], tm=tm, tn=tnp, tk=D)
    z_t, = _matmul(w_fl_t, h1, mode="nt", name="mm_flogit", out_dtypes=[F32],
                   tm=FOX_HEADS, tn=_pick(S, 2048), tk=D)
    bias_col = fox_f_bias.reshape(FOX_HEADS, 1)
    negc = _fox_prep(z_t, bias_col, name="fox_prep")
    bq = _fox_block(S)
    negc4 = negc.reshape(FOX_HEADS // 2, 2, S // bq, bq).transpose(0, 2, 1, 3)
    inv_freq = ROPE_THETA ** (-jnp.arange(0, HEAD_DIM, 2, dtype=F32) / HEAD_DIM)
    invf = jnp.tile(inv_freq, LANES // (HEAD_DIM // 2)).reshape(1, LANES)
    cos_t, sin_t = _rope_tables(positions.reshape(S, 1), invf, name="rope_tables")
    q_rope, k_rope = _rope_fwd(proj, cos_t, sin_t, q_off=q_off, k_off=k_off, name="rope_fwd")
    sinks = swa_sinks.reshape(-1)
    o_a = _swa_fwd(q_rope, k_rope, proj, sinks, v_off=v_off, name="swa_fwd")
    o_b, lse = _fox_fwd(proj, negc4, q_off=fq_off, k_off=fk_off, v_off=fv_off, name="fox_fwd")
    ya, = _matmul(o_a, w_bs_t, mode="nt", name="mm_branch_swa", out_dtypes=[BF16], tm=tm, tn=td, tk=SWA_Q_W)
    gate_maps = [lambda i, j, k: (i, j), lambda i, j, k: (i, j), lambda i, j, k: (i, j + D // td)]

    def merge_epi(acc, ya_t, ga_t, gb_t):
        merged = _sigmoid(ga_t.astype(F32)) * ya_t.astype(F32) + _sigmoid(gb_t.astype(F32)) * acc
        return acc, merged

    yb, merged = _matmul(o_b, w_bf_t, mode="nt", name="mm_branch_fox", out_dtypes=[BF16, BF16],
                         tm=tm, tn=td, tk=FOX_W, extras=[ya, proj, proj], extra_maps=gate_maps,
                         epilogue=merge_epi)
    x_mid, = _matmul(merged, w_o, mode="nn", name="mm_out", out_dtypes=[F32], tm=tm, tn=td, tk=D,
                     extras=[x2d], epilogue=lambda acc, r: (acc + r,))
    h2 = _rms_fwd(x_mid, mlp_norm, name="rms2")
    u, = _matmul(h2, w_up_t, mode="nt", name="mm_up", out_dtypes=[BF16], tm=tm, tn=tf, tk=D,
                 epilogue=lambda acc: (jnp.maximum(acc, 0.0),))
    x_fin, = _matmul(u, w_dn, mode="nn", name="mm_down", out_dtypes=[F32], tm=tm, tn=td, tk=_pick(DFF, 2048),
                     a_fn=_square_bf16, extras=[x_mid], epilogue=lambda acc, r: (acc + r,))

    dx3, dx3b, dg3, loss_part = _loss_head(x_fin, tgt, final_norm.reshape(1, D), name="loss_head")
    d_up, = _matmul(dx3b, w_dn, mode="nt", name="mm_d_act", out_dtypes=[BF16], tm=tm, tn=tf, tk=D,
                    extras=[u], epilogue=lambda acc, ut: (acc * (2.0 * ut.astype(F32)),))
    tks = _pick(S, 1024)
    dw_dn, = _matmul(u, dx3b, mode="tn", name="mm_dw_down", out_dtypes=[F32], tm=tf, tn=td, tk=tks,
                     a_fn=_square_bf16)
    dh2, = _matmul(d_up, w_up_t, mode="nn", name="mm_dh2", out_dtypes=[F32], tm=tm, tn=td, tk=_pick(DFF, 2048))
    dw_up_t, = _matmul(d_up, h2, mode="tn", name="mm_dw_up", out_dtypes=[F32], tm=tf, tn=td, tk=tks)
    dx2, dx2b, dg2 = _rms_bwd(dh2, x_mid, mlp_norm, dx3, name="rms2_bwd", want_bf16=True)

    def gate_bwd_epi(dm, ya_t, yb_t, ga_t, gb_t):
        sa, sb = _sigmoid(ga_t.astype(F32)), _sigmoid(gb_t.astype(F32))
        return (dm * sa, dm * sb, dm * ya_t.astype(F32) * sa * (1.0 - sa), dm * yb_t.astype(F32) * sb * (1.0 - sb))

    gmaps = [lambda i, j, k: (i, j), lambda i, j, k: (i, j), lambda i, j, k: (i, j),
             lambda i, j, k: (i, j + D // td)]
    d_ya, d_yb, d_ga, d_gb = _matmul(dx2b, w_o, mode="nt", name="mm_d_merged", out_dtypes=[BF16] * 4,
                                     tm=tm, tn=td, tk=D, extras=[ya, yb, proj, proj], extra_maps=gmaps,
                                     epilogue=gate_bwd_epi)
    dw_o, = _matmul(merged, dx2b, mode="tn", name="mm_dw_out", out_dtypes=[F32], tm=td, tn=td, tk=tks)
    d_oa, = _matmul(d_ya, w_bs_t, mode="nn", name="mm_d_oa", out_dtypes=[BF16], tm=tm, tn=SWA_Q_W, tk=D)
    d_ob, = _matmul(d_yb, w_bf_t, mode="nn", name="mm_d_ob", out_dtypes=[BF16], tm=tm, tn=FOX_W, tk=D)
    dw_bs_t, = _matmul(d_ya, o_a, mode="tn", name="mm_dw_bs", out_dtypes=[F32], tm=td, tn=SWA_Q_W, tk=tks)
    dw_bf_t, = _matmul(d_yb, o_b, mode="tn", name="mm_dw_bf", out_dtypes=[F32], tm=td, tn=FOX_W, tk=tks)
    d_fq, d_fk, d_fv, dcol4, drow3 = _fox_bwd(proj, negc4, o_b, lse, d_ob, q_off=fq_off, k_off=fk_off,
                                              v_off=fv_off, name="fox_bwd")
    dcol = dcol4.transpose(0, 2, 1, 3).reshape(FOX_HEADS, S)
    drow = drow3[:, :, ::HEAD_DIM].transpose(0, 2, 1).reshape(FOX_HEADS, S)
    dz_t, dbias_l = _fox_post(drow, dcol, z_t, bias_col, name="fox_post")
    dq_r, dk_c, dk_p, dv_c, dv_p, dsink_l = _swa_bwd(q_rope, k_rope, proj, sinks, d_oa, v_off=v_off, name="swa_bwd")
    d_aq, d_ak, d_av = _rope_bwd(dq_r, dk_c, dk_p, dv_c, dv_p, cos_t, sin_t, name="rope_bwd")
    dz_pad = jnp.pad(dz_t.T.astype(BF16), ((0, 0), (0, FL_PAD - FOX_HEADS)))
    d_proj = jnp.concatenate([d_ga, d_gb, d_aq, d_ak, d_av, d_fq.astype(BF16), d_fk, d_fv, dz_pad], axis=1)
    tkp = _pick(NP, 2304)
    dh1, = _matmul(d_proj, w_in_p, mode="nn", name="mm_dh1", out_dtypes=[F32], tm=tm, tn=td, tk=tkp)
    dw_in_p, = _matmul(d_proj, h1, mode="tn", name="mm_dw_in", out_dtypes=[F32], tm=_pick(NP, 512), tn=D, tk=tks)
    dx, dg1 = _rms_bwd(dh1, x2d, attn_norm, dx2, name="rms1_bwd", want_bf16=False)

    dw_in_t = jnp.concatenate([dw_in_p[q_off:q_off + QKV_W], dw_in_p[fl_off:fl_off + FOX_HEADS], dw_in_p[:q_off]],
                              axis=0)
    chunks = [dw_in_t.reshape(N_DEV, d_in // N_DEV, D), dw_bs_t.reshape(N_DEV, D // N_DEV, SWA_Q_W),
              dw_bf_t.reshape(N_DEV, D // N_DEV, FOX_W), dw_o.reshape(N_DEV, D // N_DEV, D),
              dw_up_t.reshape(N_DEV, DFF // N_DEV, D), dw_dn.reshape(N_DEV, DFF // N_DEV, D)]
    r_in, r_bs, r_bf, r_o, r_up, r_dn = _exchange(chunks, gather=False, name="scatter_grads")

    def update_t(stack, w, m, v, nm):
        g = _sum8(stack, name="sum_" + nm).T
        return _adamw(w[0], g, m[0], v[0], name="adamw_" + nm, stacked=False)

    u_in = update_t(r_in, w_in, m_w_in, v_w_in, "w_in")
    u_bs = update_t(r_bs, w_branch_swa, m_w_branch_swa, v_w_branch_swa, "w_bs")
    u_bf = update_t(r_bf, w_branch_fox, m_w_branch_fox, v_w_branch_fox, "w_bf")
    u_up = update_t(r_up, w_up, m_w_up, v_w_up, "w_up")
    u_o = _adamw(w_out[0], r_o, m_w_out[0], v_w_out[0], name="adamw_w_out", stacked=True)
    u_dn = _adamw(w_down[0], r_dn, m_w_down[0], v_w_down[0], name="adamw_w_down", stacked=True)

    dbias = dbias_l[:, 0]
    dsinks = dsink_l[:, :, 0].reshape(-1)
    nsm = 3 * D + 2 * LANES
    tail = jnp.zeros((2 * LANES,), F32)
    small_g = jnp.concatenate([dg1[0], dg2[0], dg3[0],
                               tail.at[0:16].set(dbias).at[16:32].set(dsinks).at[32].set(loss_part[0, 0])])

    def pack(a_norm, b_norm, f_norm, bias, snk):
        return jnp.concatenate([a_norm[0], b_norm[0], f_norm,
                                tail.at[0:16].set(bias[0]).at[16:32].set(snk[0])]).reshape(1, nsm)

    small_stack, = _exchange([small_g.reshape(1, nsm)], gather=True, name="gather_small")
    u_sm = _adamw(pack(attn_norm, mlp_norm, final_norm, fox_f_bias, swa_sinks), small_stack,
                  pack(m_attn_norm, m_mlp_norm, m_final_norm, m_fox_f_bias, m_swa_sinks),
                  pack(v_attn_norm, v_mlp_norm, v_final_norm, v_fox_f_bias, v_swa_sinks),
                  name="adamw_small", stacked=True)
    loss = u_sm[0][0, 3 * D + 32]

    def small(kind):
        a = u_sm[kind][0]
        return dict(attn_norm=a[0:D][None], mlp_norm=a[D:2 * D][None], final_norm=a[2 * D:3 * D],
                    fox_f_bias=a[3 * D:3 * D + 16][None], swa_sinks=a[3 * D + 16:3 * D + 32][None])

    big = dict(w_in=u_in, w_branch_swa=u_bs, w_branch_fox=u_bf, w_out=u_o, w_up=u_up, w_down=u_dn)
    order = ["attn_norm", "w_in", "fox_f_bias", "swa_sinks", "w_branch_swa", "w_branch_fox", "w_out", "mlp_norm",
             "w_up", "w_down", "final_norm"]
    outs = [loss, dx[None]]
    for kind in range(4):
        sm = small(kind)
        for nm in order:
            outs.append(big[nm][kind][None] if nm in big else sm[nm])
    return tuple(outs)
```

```python
import functools

import jax
import jax.numpy as jnp
from jax import lax
from jax.experimental import pallas as pl
from jax.experimental.pallas import tpu as pltpu

F32 = jnp.float32
BF16 = jnp.bfloat16

N_DEV = 8
HEAD_DIM = 64
SWA_Q_W = 1024
SWA_KV_W = 128
SWA_GROUP = 8
WINDOW = 128
FOX_W = 1024
FOX_HEADS = 16
QKV_W = SWA_Q_W + 2 * SWA_KV_W + 3 * FOX_W
FL_PAD = 256
ROPE_THETA = 10000.0
RMS_EPS = 1e-6
ATT_SCALE = 0.125
NEG = -1e30

ADAM_LR = 0.001
ADAM_B1 = 0.9
ADAM_B2 = 0.999
ADAM_EPS = 1e-08
ADAM_WD = 0.01
ADAM_STEP = 10

LANES = 128
VMEM_LIMIT = 56 * 1024 * 1024
STEP_BYTES = 12 * 1024 * 1024


def _cparams(*sem):
    return pltpu.CompilerParams(dimension_semantics=sem, vmem_limit_bytes=VMEM_LIMIT)


def _pick(dim, pref, align=LANES):
    best = None
    t = align
    while t <= min(dim, pref):
        if dim % t == 0:
            best = t
        t += align
    return best if best is not None else dim


_DIMS = {"nn": ((1,), (0,)), "nt": ((1,), (1,)), "tn": ((0,), (0,))}


_ANY = pl.BlockSpec(memory_space=pl.ANY)


def _matmul(a, b, *, mode, name, out_dtypes, tm, tn, tk, extras=(), extra_maps=None,
            a_fn=None, epilogue=None, deps=()):
    if mode == "nn":
        (M, K), (K2, N) = a.shape, b.shape
    elif mode == "nt":
        (M, K), (N, K2) = a.shape, b.shape
    else:
        (K, M), (K2, N) = a.shape, b.shape
    assert K == K2, (name, a.shape, b.shape)
    assert M % tm == 0 and N % tn == 0 and K % tk == 0, (name, M, N, K, tm, tn, tk)
    nk = K // tk
    ne, no = len(extras), len(out_dtypes)
    dims = (_DIMS[mode], ((), ()))

    def body(*refs):
        a_ref, b_ref = refs[0], refs[1]
        ex_refs = refs[2:2 + ne]
        out_refs = refs[2 + ne + len(deps):2 + ne + len(deps) + no]

        def finish(acc):
            res = (acc,) if epilogue is None else epilogue(acc, *[e[...] for e in ex_refs])
            for o_ref, r in zip(out_refs, res):
                o_ref[...] = r.astype(o_ref.dtype)

        av = a_ref[...]
        if a_fn is not None:
            av = a_fn(av)
        part = lax.dot_general(av, b_ref[...], dims, preferred_element_type=F32)
        if nk == 1:
            finish(part)
        else:
            acc_ref = refs[-1]
            k = pl.program_id(2)

            @pl.when(k == 0)
            def _():
                acc_ref[...] = part

            @pl.when(k > 0)
            def _():
                acc_ref[...] += part

            @pl.when(k == nk - 1)
            def _():
                finish(acc_ref[...])

    if mode == "tn":
        a_spec = pl.BlockSpec((tk, tm), lambda i, j, k: (k, i))
    else:
        a_spec = pl.BlockSpec((tm, tk), lambda i, j, k: (i, k))
    if mode == "nt":
        b_spec = pl.BlockSpec((tn, tk), lambda i, j, k: (j, k))
    else:
        b_spec = pl.BlockSpec((tk, tn), lambda i, j, k: (k, j))
    if extra_maps is None:
        extra_maps = [lambda i, j, k: (i, j)] * ne
    ex_specs = [pl.BlockSpec((tm, tn), m) for m in extra_maps]
    out_spec = [pl.BlockSpec((tm, tn), lambda i, j, k: (i, j)) for _ in range(no)]
    res = pl.pallas_call(
        body,
        name=name,
        grid=(M // tm, N // tn, nk),
        in_specs=[a_spec, b_spec] + ex_specs + [_ANY] * len(deps),
        out_specs=out_spec,
        out_shape=[jax.ShapeDtypeStruct((M, N), d) for d in out_dtypes],
        scratch_shapes=[pltpu.VMEM((tm, tn), F32)] if nk > 1 else [],
        compiler_params=_cparams("parallel", "parallel", "arbitrary"),
    )(a, b, *extras, *deps)
    return res


def _square_bf16(t):
    tf = t.astype(F32)
    return (tf * tf).astype(BF16)


def _sigmoid(g):
    return 1.0 / (1.0 + jnp.exp(-g))


def _rms_fwd(x, gain, *, name, deps=()):
    S, D = x.shape
    tr = _pick(S, 512, 8)

    def body(x_ref, g_ref, *rest):
        h_ref = rest[-1]
        xv = x_ref[...]
        r = lax.rsqrt(jnp.mean(xv * xv, axis=-1, keepdims=True) + RMS_EPS)
        h_ref[...] = (xv * r * g_ref[...]).astype(BF16)

    return pl.pallas_call(
        body, name=name, grid=(S // tr,),
        in_specs=[pl.BlockSpec((tr, D), lambda i: (i, 0)), pl.BlockSpec((1, D), lambda i: (0, 0))] + [_ANY] * len(deps),
        out_specs=pl.BlockSpec((tr, D), lambda i: (i, 0)),
        out_shape=jax.ShapeDtypeStruct((S, D), BF16),
        compiler_params=_cparams("parallel"),
    )(x, gain, *deps)


def _rms_bwd(dh, x, gain, dres, *, name, want_bf16, deps=()):
    S, D = x.shape
    tr = _pick(S, 256, 8)

    def body(dh_ref, x_ref, g_ref, dres_ref, *rest):
        outs = rest[len(deps):]
        dx_ref, dg_ref = outs[0], outs[-1]
        xv = x_ref[...]
        r = lax.rsqrt(jnp.mean(xv * xv, axis=-1, keepdims=True) + RMS_EPS)
        xh = xv * r
        dhv = dh_ref[...]
        t = dhv * g_ref[...]
        dx = r * (t - xh * jnp.mean(t * xh, axis=-1, keepdims=True)) + dres_ref[...]
        dx_ref[...] = dx
        if want_bf16:
            outs[1][...] = dx.astype(BF16)
        part = jnp.sum(dhv * xh, axis=0, keepdims=True)

        @pl.when(pl.program_id(0) == 0)
        def _():
            dg_ref[...] = part

        @pl.when(pl.program_id(0) > 0)
        def _():
            dg_ref[...] += part

    row = pl.BlockSpec((tr, D), lambda i: (i, 0))
    vec = pl.BlockSpec((1, D), lambda i: (0, 0))
    out_shape = [jax.ShapeDtypeStruct((S, D), F32)]
    out_specs = [row]
    if want_bf16:
        out_shape.append(jax.ShapeDtypeStruct((S, D), BF16))
        out_specs.append(row)
    out_shape.append(jax.ShapeDtypeStruct((1, D), F32))
    out_specs.append(vec)
    return pl.pallas_call(
        body, name=name, grid=(S // tr,),
        in_specs=[row, row, vec, row] + [_ANY] * len(deps), out_specs=out_specs, out_shape=out_shape,
        compiler_params=_cparams("arbitrary"),
    )(dh, x, gain, dres, *deps)


def _loss_head(x3, target, gain, *, name):
    S, D = x3.shape
    tr = _pick(S, 256, 8)

    def body(x_ref, t_ref, g_ref, dx_ref, dxb_ref, dg_ref, loss_ref):
        xv = x_ref[...]
        r = lax.rsqrt(jnp.mean(xv * xv, axis=-1, keepdims=True) + RMS_EPS)
        xh = xv * r
        gv = g_ref[...]
        err = xh * gv - t_ref[...]
        lpart = jnp.zeros((1, LANES), F32) + (0.5 / D) * jnp.sum(err * err)
        dy = err * (1.0 / D)
        t = dy * gv
        dx = r * (t - xh * jnp.mean(t * xh, axis=-1, keepdims=True))
        dx_ref[...] = dx
        dxb_ref[...] = dx.astype(BF16)
        part = jnp.sum(dy * xh, axis=0, keepdims=True)

        @pl.when(pl.program_id(0) == 0)
        def _():
            dg_ref[...] = part
            loss_ref[...] = lpart

        @pl.when(pl.program_id(0) > 0)
        def _():
            dg_ref[...] += part
            loss_ref[...] += lpart

    row = pl.BlockSpec((tr, D), lambda i: (i, 0))
    vec = pl.BlockSpec((1, D), lambda i: (0, 0))
    return pl.pallas_call(
        body, name=name, grid=(S // tr,),
        in_specs=[row, row, vec],
        out_specs=[row, row, vec, pl.BlockSpec((1, LANES), lambda i: (0, 0))],
        out_shape=[jax.ShapeDtypeStruct((S, D), F32), jax.ShapeDtypeStruct((S, D), BF16),
                   jax.ShapeDtypeStruct((1, D), F32), jax.ShapeDtypeStruct((1, LANES), F32)],
        compiler_params=_cparams("arbitrary"),
    )(x3, target, gain)


def _rope_tables(pos_col, invf, *, name):
    S = pos_col.shape[0]
    tr = _pick(S, 512, 8)

    def body(p_ref, f_ref, cos_ref, sin_ref):
        ang = p_ref[...].astype(F32) * f_ref[...]
        lane = lax.broadcasted_iota(jnp.int32, (1, LANES), 1)
        first = (lane % HEAD_DIM) < HEAD_DIM // 2
        sn = jnp.sin(ang)
        cos_ref[...] = jnp.cos(ang)
        sin_ref[...] = jnp.where(first, -sn, sn)

    return pl.pallas_call(
        body, name=name, grid=(S // tr,),
        in_specs=[pl.BlockSpec((tr, 1), lambda i: (i, 0)), pl.BlockSpec((1, LANES), lambda i: (0, 0))],
        out_specs=[pl.BlockSpec((tr, LANES), lambda i: (i, 0))] * 2,
        out_shape=[jax.ShapeDtypeStruct((S, LANES), F32)] * 2,
        compiler_params=_cparams("parallel"),
    )(pos_col, invf)


def _swap_halves(t):
    lane = lax.broadcasted_iota(jnp.int32, (1, LANES), 1)
    first = (lane % HEAD_DIM) < HEAD_DIM // 2
    return jnp.where(first, pltpu.roll(t, LANES - HEAD_DIM // 2, 1), pltpu.roll(t, HEAD_DIM // 2, 1))


def _rope_fwd(proj, cos_t, sin_t, *, q_off, k_off, name):
    S = proj.shape[0]
    tr = _pick(S, 256, 8)
    nqb = SWA_Q_W // LANES

    def body(q_ref, k_ref, c_ref, s_ref, qo_ref, ko_ref):
        cv, sv = c_ref[...], s_ref[...]
        for b in range(nqb):
            t = q_ref[:, b * LANES:(b + 1) * LANES].astype(F32)
            qo_ref[:, b * LANES:(b + 1) * LANES] = (t * cv + _swap_halves(t) * sv).astype(BF16)
        t = k_ref[...].astype(F32)
        ko_ref[...] = (t * cv + _swap_halves(t) * sv).astype(BF16)

    tab = pl.BlockSpec((tr, LANES), lambda i: (i, 0))
    return pl.pallas_call(
        body, name=name, grid=(S // tr,),
        in_specs=[pl.BlockSpec((tr, SWA_Q_W), lambda i: (i, q_off // SWA_Q_W)),
                  pl.BlockSpec((tr, LANES), lambda i: (i, k_off // LANES)), tab, tab],
        out_specs=[pl.BlockSpec((tr, SWA_Q_W), lambda i: (i, 0)), tab],
        out_shape=[jax.ShapeDtypeStruct((S, SWA_Q_W), BF16), jax.ShapeDtypeStruct((S, LANES), BF16)],
        compiler_params=_cparams("parallel"),
    )(proj, proj, cos_t, sin_t)


def _rope_bwd(dq, dk_cur, dk_prev, dv_cur, dv_prev, cos_t, sin_t, *, name):
    S = dq.shape[0]
    tr = WINDOW
    nb = S // tr
    nqb = SWA_Q_W // LANES

    def body(dq_ref, kc_ref, kp_ref, vc_ref, vp_ref, c_ref, s_ref, dqo_ref, dko_ref, dvo_ref):
        cv, sv = c_ref[...], s_ref[...]
        has_next = (pl.program_id(0) + 1 < nb).astype(F32)
        for b in range(nqb):
            d = dq_ref[:, b * LANES:(b + 1) * LANES]
            dqo_ref[:, b * LANES:(b + 1) * LANES] = (d * cv + _swap_halves(d * sv)).astype(BF16)
        d = kc_ref[0] + kc_ref[1] + has_next * (kp_ref[0] + kp_ref[1])
        dko_ref[...] = (d * cv + _swap_halves(d * sv)).astype(BF16)
        dvo_ref[...] = (vc_ref[0] + vc_ref[1] + has_next * (vp_ref[0] + vp_ref[1])).astype(BF16)

    tab = pl.BlockSpec((tr, LANES), lambda i: (i, 0))
    cur = pl.BlockSpec((2, tr, LANES), lambda i: (0, i, 0))
    nxt = pl.BlockSpec((2, tr, LANES), lambda i: (0, jnp.minimum(i + 1, nb - 1), 0))
    return pl.pallas_call(
        body, name=name, grid=(nb,),
        in_specs=[pl.BlockSpec((tr, SWA_Q_W), lambda i: (i, 0)), cur, nxt, cur, nxt, tab, tab],
        out_specs=[pl.BlockSpec((tr, SWA_Q_W), lambda i: (i, 0)), tab, tab],
        out_shape=[jax.ShapeDtypeStruct((S, SWA_Q_W), BF16), jax.ShapeDtypeStruct((S, LANES), BF16),
                   jax.ShapeDtypeStruct((S, LANES), BF16)],
        compiler_params=_cparams("parallel"),
    )(dq, dk_cur, dk_prev, dv_cur, dv_prev, cos_t, sin_t)


def _dot_nt(a, b):
    return lax.dot_general(a, b, (((1,), (1,)), ((), ())), preferred_element_type=F32)


def _dot_tn(a, b):
    return lax.dot_general(a, b, (((0,), (0,)), ((), ())), preferred_element_type=F32)


def _dot_nn(a, b):
    return lax.dot_general(a, b, (((1,), (0,)), ((), ())), preferred_element_type=F32)


def _roll_half(t):
    return pltpu.roll(t.astype(F32), HEAD_DIM, 1).astype(t.dtype)


def _swa_common(hk, n, kp_ref, kc_ref, vp_ref, vc_ref):
    k2 = jnp.concatenate([kp_ref[...], kc_ref[...]], axis=0)
    v2 = jnp.concatenate([vp_ref[...], vc_ref[...]], axis=0)
    k_sw, v_sw = _roll_half(k2), _roll_half(v2)
    row = lax.broadcasted_iota(jnp.int32, (WINDOW, 2 * WINDOW), 0)
    col = lax.broadcasted_iota(jnp.int32, (WINDOW, 2 * WINDOW), 1)
    diff = row + WINDOW - col
    allowed = (diff >= 0) & (diff < WINDOW) & ((col >= WINDOW) | (n > 0))
    lane = lax.broadcasted_iota(jnp.int32, (1, LANES), 1)
    half = [lane < HEAD_DIM, lane >= HEAD_DIM]
    kk = [jnp.where(hk == a, k2, k_sw) for a in range(2)]
    vv = [jnp.where(hk == a, v2, v_sw) for a in range(2)]
    return allowed, half, kk, vv


def _swa_probs(qm, kk, allowed, sink):
    s = jnp.where(allowed, _dot_nt(qm, kk), NEG)
    m = jnp.maximum(jnp.max(s, axis=1, keepdims=True), sink)
    e = jnp.exp(s - m)
    es = jnp.exp(sink - m)
    inv = 1.0 / (jnp.sum(e, axis=1, keepdims=True) + es)
    return e * inv, es * inv


def _swa_fwd(q_rope, k_rope, proj, sinks, *, v_off, name):
    S = q_rope.shape[0]
    nb = S // WINDOW
    gw = SWA_GROUP * HEAD_DIM

    def body(sink_ref, q_ref, kp_ref, kc_ref, vp_ref, vc_ref, o_ref):
        hk, n = pl.program_id(0), pl.program_id(1)
        allowed, half, kk, vv = _swa_common(hk, n, kp_ref, kc_ref, vp_ref, vc_ref)
        for t in range(SWA_GROUP // 2):
            qp = q_ref[:, t * LANES:(t + 1) * LANES] * jnp.asarray(ATT_SCALE, BF16)
            outs = []
            for a in range(2):
                qm = jnp.where(half[a], qp, jnp.zeros_like(qp))
                p, _ = _swa_probs(qm, kk[a], allowed, sink_ref[hk * SWA_GROUP + 2 * t + a])
                outs.append(_dot_nn(p.astype(BF16), vv[a]))
            o_ref[:, t * LANES:(t + 1) * LANES] = jnp.where(half[0], outs[0], outs[1]).astype(BF16)

    prev = lambda hk, n: (jnp.maximum(n - 1, 0), 0)
    cur = lambda hk, n: (n, 0)
    vprev = lambda hk, n: (jnp.maximum(n - 1, 0), v_off // LANES)
    vcur = lambda hk, n: (n, v_off // LANES)
    blk = lambda m: pl.BlockSpec((WINDOW, LANES), m)
    return pl.pallas_call(
        body, name=name, grid=(2, nb),
        in_specs=[pl.BlockSpec(memory_space=pltpu.SMEM),
                  pl.BlockSpec((WINDOW, gw), lambda hk, n: (n, hk)),
                  blk(prev), blk(cur), blk(vprev), blk(vcur)],
        out_specs=pl.BlockSpec((WINDOW, gw), lambda hk, n: (n, hk)),
        out_shape=jax.ShapeDtypeStruct((S, SWA_Q_W), BF16),
        compiler_params=_cparams("parallel", "parallel"),
    )(sinks, q_rope, k_rope, k_rope, proj, proj)


def _swa_bwd(q_rope, k_rope, proj, sinks, d_o, *, v_off, name):
    S = q_rope.shape[0]
    nb = S // WINDOW
    gw = SWA_GROUP * HEAD_DIM

    def body(sink_ref, q_ref, kp_ref, kc_ref, vp_ref, vc_ref, do_ref,
             dq_ref, dkc_ref, dkp_ref, dvc_ref, dvp_ref, dsink_ref):
        hk, n = pl.program_id(0), pl.program_id(1)
        allowed, half, kk, vv = _swa_common(hk, n, kp_ref, kc_ref, vp_ref, vc_ref)
        dk_acc = jnp.zeros((2 * WINDOW, LANES), F32)
        dv_acc = jnp.zeros((2 * WINDOW, LANES), F32)
        srow = lax.broadcasted_iota(jnp.int32, (SWA_GROUP, LANES), 0)
        dsink = jnp.zeros((SWA_GROUP, LANES), F32)
        for t in range(SWA_GROUP // 2):
            qp = q_ref[:, t * LANES:(t + 1) * LANES] * jnp.asarray(ATT_SCALE, BF16)
            dop = do_ref[:, t * LANES:(t + 1) * LANES]
            dqs = []
            for a in range(2):
                g = 2 * t + a
                qm = jnp.where(half[a], qp, jnp.zeros_like(qp))
                dom = jnp.where(half[a], dop, jnp.zeros_like(dop))
                p, psink = _swa_probs(qm, kk[a], allowed, sink_ref[hk * SWA_GROUP + g])
                dp = _dot_nt(dom, vv[a])
                delta = jnp.sum(p * dp, axis=1, keepdims=True)
                ds = (p * (dp - delta)).astype(BF16)
                dsink = dsink + jnp.where(srow == g, -jnp.sum(psink * delta), 0.0)
                dqs.append(_dot_nn(ds, kk[a]) * ATT_SCALE)
                dk_acc = dk_acc + _dot_tn(ds, qm)
                dv_acc = dv_acc + _dot_tn(p.astype(BF16), dom)
            dq_ref[:, t * LANES:(t + 1) * LANES] = jnp.where(half[0], dqs[0], dqs[1])
        lane = lax.broadcasted_iota(jnp.int32, (1, LANES), 1)
        mine = (lane >= HEAD_DIM) == (hk == 1)
        dk_t = jnp.where(mine, dk_acc + pltpu.roll(dk_acc, HEAD_DIM, 1), 0.0)
        dv_t = jnp.where(mine, dv_acc + pltpu.roll(dv_acc, HEAD_DIM, 1), 0.0)
        dkp_ref[0] = dk_t[:WINDOW]
        dkc_ref[0] = dk_t[WINDOW:]
        dvp_ref[0] = dv_t[:WINDOW]
        dvc_ref[0] = dv_t[WINDOW:]

        @pl.when(n == 0)
        def _():
            dsink_ref[0] = dsink

        @pl.when(n > 0)
        def _():
            dsink_ref[0] += dsink

    prev = lambda hk, n: (jnp.maximum(n - 1, 0), 0)
    cur = lambda hk, n: (n, 0)
    vprev = lambda hk, n: (jnp.maximum(n - 1, 0), v_off // LANES)
    vcur = lambda hk, n: (n, v_off // LANES)
    blk = lambda m: pl.BlockSpec((WINDOW, LANES), m)
    qblk = pl.BlockSpec((WINDOW, gw), lambda hk, n: (n, hk))
    part = pl.BlockSpec((1, WINDOW, LANES), lambda hk, n: (hk, n, 0))
    part_shape = jax.ShapeDtypeStruct((2, S, LANES), F32)
    return pl.pallas_call(
        body, name=name, grid=(2, nb),
        in_specs=[pl.BlockSpec(memory_space=pltpu.SMEM), qblk, blk(prev), blk(cur), blk(vprev), blk(vcur), qblk],
        out_specs=[qblk, part, part, part, part,
                   pl.BlockSpec((1, SWA_GROUP, LANES), lambda hk, n: (hk, 0, 0))],
        out_shape=[jax.ShapeDtypeStruct((S, SWA_Q_W), F32), part_shape, part_shape, part_shape, part_shape,
                   jax.ShapeDtypeStruct((2, SWA_GROUP, LANES), F32)],
        compiler_params=_cparams("parallel", "arbitrary"),
    )(sinks, q_rope, k_rope, k_rope, proj, proj, d_o)


def _fox_prep(z_t, bias_col, *, name):
    H, S = z_t.shape
    tb = _pick(S, 512)

    def body(z_ref, b_ref, o_ref, carry_ref):
        @pl.when(pl.program_id(0) == 0)
        def _():
            carry_ref[...] = jnp.zeros_like(carry_ref)

        zz = z_ref[...] + b_ref[...]
        t = jnp.exp(-jnp.abs(zz))
        log1p = jnp.where(t < 1e-2, t * (1.0 - t * (0.5 - t * (1.0 / 3.0))), jnp.log(1.0 + t))
        logf = jnp.minimum(zz, 0.0) - log1p
        r = lax.broadcasted_iota(jnp.int32, (tb, tb), 0)
        c = lax.broadcasted_iota(jnp.int32, (tb, tb), 1)
        tri = (r <= c).astype(BF16)
        hi = logf.astype(BF16)
        r1 = logf - hi.astype(F32)
        mid = r1.astype(BF16)
        lo = (r1 - mid.astype(F32)).astype(BF16)
        cs = _dot_nn(hi, tri) + _dot_nn(mid, tri) + _dot_nn(lo, tri) + carry_ref[:, 0:1]
        o_ref[...] = -cs
        carry_ref[...] = jnp.zeros_like(carry_ref) + cs[:, tb - 1:tb]

    return pl.pallas_call(
        body, name=name, grid=(S // tb,),
        in_specs=[pl.BlockSpec((H, tb), lambda i: (0, i)), pl.BlockSpec((H, 1), lambda i: (0, 0))],
        out_specs=pl.BlockSpec((H, tb), lambda i: (0, i)),
        out_shape=jax.ShapeDtypeStruct((H, S), F32),
        scratch_shapes=[pltpu.VMEM((H, LANES), F32)],
        compiler_params=_cparams("arbitrary"),
    )(z_t, bias_col)


def _fox_post(drow, dcol, z_t, bias_col, *, name):
    H, S = z_t.shape
    tb = _pick(S, 512)
    nb = S // tb

    def body(dr_ref, d_ref, z_ref, b_ref, dz_ref, db_ref, carry_ref):
        @pl.when(pl.program_id(0) == 0)
        def _():
            carry_ref[...] = jnp.zeros_like(carry_ref)
            db_ref[...] = jnp.zeros_like(db_ref)

        dc = dr_ref[...] - d_ref[...]
        r = lax.broadcasted_iota(jnp.int32, (tb, tb), 0)
        c = lax.broadcasted_iota(jnp.int32, (tb, tb), 1)
        tri = (r >= c).astype(BF16)
        hi = dc.astype(BF16)
        r1 = dc - hi.astype(F32)
        mid = r1.astype(BF16)
        lo = (r1 - mid.astype(F32)).astype(BF16)
        dlogf = _dot_nn(hi, tri) + _dot_nn(mid, tri) + _dot_nn(lo, tri) + carry_ref[:, 0:1]
        carry_ref[...] = jnp.zeros_like(carry_ref) + dlogf[:, 0:1]
        dz = dlogf * _sigmoid(-(z_ref[...] + b_ref[...]))
        dz_ref[...] = dz
        db_ref[...] += jnp.sum(dz, axis=1, keepdims=True)

    rev = lambda i: (0, nb - 1 - i)
    return pl.pallas_call(
        body, name=name, grid=(nb,),
        in_specs=[pl.BlockSpec((H, tb), rev), pl.BlockSpec((H, tb), rev), pl.BlockSpec((H, tb), rev),
                  pl.BlockSpec((H, 1), lambda i: (0, 0))],
        out_specs=[pl.BlockSpec((H, tb), rev), pl.BlockSpec((H, LANES), lambda i: (0, 0))],
        out_shape=[jax.ShapeDtypeStruct((H, S), F32), jax.ShapeDtypeStruct((H, LANES), F32)],
        scratch_shapes=[pltpu.VMEM((H, LANES), F32)],
        compiler_params=_cparams("arbitrary"),
    )(drow, dcol, z_t, bias_col)


def _fox_block(S):
    return min(512, max(LANES, S // 4))


def _fox_fwd(proj, negc4, *, q_off, k_off, v_off, name):
    S = proj.shape[0]
    bq = _fox_block(S)
    nq = S // bq
    npair = FOX_HEADS // 2

    def body(q_ref, k_ref, v_ref, nc_ref, o_ref, lse_ref):
        i = pl.program_id(1)
        lane = lax.broadcasted_iota(jnp.int32, (1, LANES), 1)
        half = [lane < HEAD_DIM, lane >= HEAD_DIM]
        q2 = q_ref[...] * jnp.asarray(ATT_SCALE, BF16)
        qh = [jnp.where(half[h], q2, jnp.zeros_like(q2)) for h in range(2)]
        row = lax.broadcasted_iota(jnp.int32, (bq, bq), 0)
        col = lax.broadcasted_iota(jnp.int32, (bq, bq), 1)
        causal = row >= col

        def step(j, carry, masked):
            start = pl.multiple_of(j * bq, bq)
            ks = k_ref[pl.ds(start, bq), :]
            vs = v_ref[pl.ds(start, bq), :]
            nb = nc_ref[0, j]
            new = []
            for h in range(2):
                m, l, acc = carry[3 * h:3 * h + 3]
                s = _dot_nt(qh[h], ks) + nb[h:h + 1, :]
                if masked:
                    s = jnp.where(causal, s, NEG)
                m_new = jnp.maximum(m, jnp.max(s, axis=1, keepdims=True))
                alpha = jnp.exp(m - m_new)
                p = jnp.exp(s - m_new)
                l = alpha * l + jnp.sum(p, axis=1, keepdims=True)
                acc = alpha * acc + _dot_nn(p.astype(BF16), vs)
                new += [m_new, l, acc]
            return tuple(new)

        init = (jnp.full((bq, 1), NEG, F32), jnp.zeros((bq, 1), F32), jnp.zeros((bq, LANES), F32)) * 2
        carry = lax.fori_loop(0, i, lambda j, c: step(j, c, False), init)
        carry = step(i, carry, True)
        outs, lses = [], []
        for h in range(2):
            m, l, acc = carry[3 * h:3 * h + 3]
            outs.append(acc * (1.0 / l))
            lses.append(m + jnp.log(l))
        o_ref[...] = jnp.where(half[0], outs[0], outs[1]).astype(BF16)
        lse_ref[0] = jnp.where(half[0], lses[0], lses[1])

    seq = lambda off: pl.BlockSpec((S, LANES), lambda hp, i: (0, off // LANES + hp))
    return pl.pallas_call(
        body, name=name, grid=(npair, nq),
        in_specs=[pl.BlockSpec((bq, LANES), lambda hp, i: (i, q_off // LANES + hp)), seq(k_off), seq(v_off),
                  pl.BlockSpec((1, nq, 2, bq), lambda hp, i: (hp, 0, 0, 0))],
        out_specs=[pl.BlockSpec((bq, LANES), lambda hp, i: (i, hp)),
                   pl.BlockSpec((1, bq, LANES), lambda hp, i: (hp, i, 0))],
        out_shape=[jax.ShapeDtypeStruct((S, FOX_W), BF16), jax.ShapeDtypeStruct((npair, S, LANES), F32)],
        compiler_params=_cparams("parallel", "parallel"),
    )(proj, proj, proj, negc4)


def _fox_bwd(proj, negc4, o, lse, d_o, *, q_off, k_off, v_off, name, deps=()):
    S = proj.shape[0]
    bq = _fox_block(S)
    nq = S // bq
    npair = FOX_HEADS // 2

    def body(q_ref, k_ref, v_ref, nc_ref, o_ref, lse_ref, do_ref, *rest):
        dq_ref, dk_ref, dv_ref, dn_ref, dr_ref = rest[len(deps):]
        j = pl.program_id(1)
        lane = lax.broadcasted_iota(jnp.int32, (1, LANES), 1)
        half = [lane < HEAD_DIM, lane >= HEAD_DIM]
        k2, v2 = k_ref[...], v_ref[...]
        kh = [jnp.where(half[h], k2, jnp.zeros_like(k2)) for h in range(2)]
        nb = nc_ref[0, 0]
        row = lax.broadcasted_iota(jnp.int32, (bq, bq), 0)
        col = lax.broadcasted_iota(jnp.int32, (bq, bq), 1)
        causal = row >= col

        @pl.when(j == 0)
        def _():
            dq_ref[...] = jnp.zeros_like(dq_ref)
            dr_ref[...] = jnp.zeros_like(dr_ref)

        def step(i, carry, masked):
            dk_acc, dv_acc, dn0, dn1 = carry
            start = pl.multiple_of(i * bq, bq)
            q2 = q_ref[pl.ds(start, bq), :] * jnp.asarray(ATT_SCALE, BF16)
            do2 = do_ref[pl.ds(start, bq), :]
            of = o_ref[pl.ds(start, bq), :].astype(F32)
            lse2 = lse_ref[0, pl.ds(start, bq), :]
            dq_new = jnp.zeros((bq, LANES), F32)
            dns = [dn0, dn1]
            rsum = []
            for h in range(2):
                qm = jnp.where(half[h], q2, jnp.zeros_like(q2))
                dom = jnp.where(half[h], do2, jnp.zeros_like(do2))
                lse_h = lse2[:, h * HEAD_DIM:h * HEAD_DIM + 1]
                p = jnp.exp(_dot_nt(qm, k2) + nb[h:h + 1, :] - lse_h)
                if masked:
                    p = jnp.where(causal, p, 0.0)
                dp = _dot_nt(dom, v2)
                delta = jnp.sum(dom.astype(F32) * of, axis=1, keepdims=True)
                ds = p * (dp - delta)
                dsb = ds.astype(BF16)
                dns[h] = dns[h] + jnp.sum(ds, axis=0, keepdims=True)
                rsum.append(jnp.sum(ds, axis=1, keepdims=True))
                dv_acc = dv_acc + _dot_tn(p.astype(BF16), dom)
                dk_acc = dk_acc + _dot_tn(dsb, qm)
                dq_new = dq_new + _dot_nn(dsb, kh[h])
            dq_ref[pl.ds(start, bq), :] += dq_new * ATT_SCALE
            dr_ref[0, pl.ds(start, bq), :] += jnp.where(half[0], rsum[0], rsum[1])
            return dk_acc, dv_acc, dns[0], dns[1]

        init = (jnp.zeros((bq, LANES), F32), jnp.zeros((bq, LANES), F32),
                jnp.zeros((1, bq), F32), jnp.zeros((1, bq), F32))
        carry = step(j, init, True)
        dk_acc, dv_acc, dn0, dn1 = lax.fori_loop(j + 1, nq, lambda i, c: step(i, c, False), carry)
        dk_ref[...] = dk_acc.astype(BF16)
        dv_ref[...] = dv_acc.astype(BF16)
        dn_ref[0, 0] = jnp.concatenate([dn0, dn1], axis=0)

    seq = lambda off: pl.BlockSpec((S, LANES), lambda hp, j: (0, off // LANES + hp))
    blk = lambda off: pl.BlockSpec((bq, LANES), lambda hp, j: (j, off // LANES + hp))
    nc = pl.BlockSpec((1, 1, 2, bq), lambda hp, j: (hp, j, 0, 0))
    return pl.pallas_call(
        body, name=name, grid=(npair, nq),
        in_specs=[seq(q_off), blk(k_off), blk(v_off), nc, seq(0),
                  pl.BlockSpec((1, S, LANES), lambda hp, j: (hp, 0, 0)), seq(0)] + [_ANY] * len(deps),
        out_specs=[seq(0), blk(0), blk(0), nc, pl.BlockSpec((1, S, LANES), lambda hp, j: (hp, 0, 0))],
        out_shape=[jax.ShapeDtypeStruct((S, FOX_W), F32), jax.ShapeDtypeStruct((S, FOX_W), BF16),
                   jax.ShapeDtypeStruct((S, FOX_W), BF16), jax.ShapeDtypeStruct((npair, nq, 2, bq), F32),
                   jax.ShapeDtypeStruct((npair, S, LANES), F32)],
        compiler_params=_cparams("parallel", "arbitrary"),
    )(proj, proj, proj, negc4, o, lse, d_o, *deps)


def _exchange(arrs, *, gather, name):
    n = len(arrs)
    npeer = N_DEV - 1

    def body(*refs):
        ins, outs = refs[:n], refs[n:2 * n]
        send_sems, recv_sems, loc_sems = refs[2 * n:]
        x, y, c = lax.axis_index("x"), lax.axis_index("y"), lax.axis_index("c")
        me = 4 * x + 2 * y + c
        peers = []
        for k in range(1, N_DEV):
            px = 1 - x if k & 4 else x
            py = 1 - y if k & 2 else y
            pc = 1 - c if k & 1 else c
            peers.append(((px, py, pc), 4 * px + 2 * py + pc))

        def remote(w, k):
            dev, idx = peers[k]
            src = ins[w] if gather else ins[w].at[idx]
            return pltpu.make_async_remote_copy(
                src_ref=src, dst_ref=outs[w].at[me],
                send_sem=send_sems.at[w * npeer + k], recv_sem=recv_sems.at[w * npeer + k],
                device_id=dev, device_id_type=pl.DeviceIdType.MESH)

        def arrival(w, k):
            dev, idx = peers[k]
            src = ins[w] if gather else ins[w].at[idx]
            return pltpu.make_async_remote_copy(
                src_ref=src, dst_ref=outs[w].at[idx],
                send_sem=send_sems.at[w * npeer + k], recv_sem=recv_sems.at[w * npeer + k],
                device_id=dev, device_id_type=pl.DeviceIdType.MESH)

        local = []
        for w in range(n):
            for k in range(npeer):
                remote(w, k).start()
            cp = pltpu.make_async_copy(ins[w] if gather else ins[w].at[me], outs[w].at[me], loc_sems.at[w])
            cp.start()
            local.append(cp)
        for w in range(n):
            for k in range(npeer):
                arrival(w, k).wait_recv()
        for w in range(n):
            for k in range(npeer):
                remote(w, k).wait_send()
            local[w].wait()

    hbm = pl.BlockSpec(memory_space=pl.ANY)
    out_shape = [jax.ShapeDtypeStruct((N_DEV,) + (a.shape if gather else a.shape[1:]), a.dtype) for a in arrs]
    return pl.pallas_call(
        body, name=name,
        in_specs=[hbm] * n, out_specs=[hbm] * n, out_shape=out_shape,
        scratch_shapes=[pltpu.SemaphoreType.DMA((n * npeer,)), pltpu.SemaphoreType.DMA((n * npeer,)),
                        pltpu.SemaphoreType.DMA((n,))],
        compiler_params=pltpu.CompilerParams(has_side_effects=True),
    )(*arrs)


_HBM = pl.BlockSpec(memory_space=pltpu.HBM)
_SEM = pl.BlockSpec(memory_space=pltpu.SEMAPHORE)
_EFFECT = pltpu.SideEffectType.DATAFLOW_SIDE_EFFECTING
NPEER = N_DEV - 1


def _peer_table():
    x, y, c = lax.axis_index("x"), lax.axis_index("y"), lax.axis_index("c")
    peers = []
    for k in range(1, N_DEV):
        px = 1 - x if k & 4 else x
        py = 1 - y if k & 2 else y
        pc = 1 - c if k & 1 else c
        peers.append(((px, py, pc), 4 * px + 2 * py + pc))
    return 4 * x + 2 * y + c, peers


def _split_copy(ins, lands, send_sems, recv_sems, gather, me, peers, w, k, arriving):
    dev, idx = peers[k]
    return pltpu.make_async_remote_copy(
        src_ref=ins[w] if gather else ins[w].at[idx],
        dst_ref=lands[w].at[idx if arriving else me],
        send_sem=send_sems.at[w * NPEER + k], recv_sem=recv_sems.at[w * NPEER + k],
        device_id=dev, device_id_type=pl.DeviceIdType.MESH)


def _exchange_start(arrs, *, gather, name):
    n = len(arrs)
    land_shapes = [(N_DEV,) + (a.shape if gather else a.shape[1:]) for a in arrs]

    def body(*refs):
        ins, lands = refs[:n], refs[n:2 * n]
        send_sems, recv_sems = refs[2 * n], refs[2 * n + 1]
        token = refs[-1]
        me, peers = _peer_table()
        for w in range(n):
            for k in range(NPEER):
                _split_copy(ins, lands, send_sems, recv_sems, gather, me, peers, w, k, False).start()
        token[...] = jnp.zeros_like(token)

    out_shape = ([pltpu.SemaphoreType.DMA((n * NPEER,)), pltpu.SemaphoreType.DMA((n * NPEER,))]
                 + [pltpu.HBM(a.shape, a.dtype) for a in arrs]
                 + [pltpu.HBM(s, a.dtype) for s, a in zip(land_shapes, arrs)]
                 + [jax.ShapeDtypeStruct((8, LANES), F32)])
    res = pl.pallas_call(
        body, name=name,
        in_specs=[_HBM] * (2 * n),
        out_specs=[_SEM, _SEM] + [_HBM] * (2 * n) + [pl.BlockSpec(memory_space=pltpu.VMEM)],
        out_shape=out_shape,
        input_output_aliases={i: 2 + i for i in range(2 * n)},
        compiler_params=pltpu.CompilerParams(has_side_effects=_EFFECT),
    )(*[pltpu.with_memory_space_constraint(a, pltpu.HBM) for a in arrs],
      *[pltpu.with_memory_space_constraint(lax.empty(s, a.dtype), pltpu.HBM) for s, a in zip(land_shapes, arrs)])
    return (n, gather, res[0], res[1], res[2:2 + n], res[2 + n:2 + 2 * n]), res[-1]


def _exchange_wait(handle, after, *, name):
    n, gather, send_sems, recv_sems, ins_thru, lands_thru = handle

    def body(*refs):
        ins, lands = refs[:n], refs[n:2 * n]
        send_s, recv_s = refs[2 * n], refs[2 * n + 1]
        me, peers = _peer_table()
        for w in range(n):
            for k in range(NPEER):
                _split_copy(ins, lands, send_s, recv_s, gather, me, peers, w, k, False).wait_send()
                _split_copy(ins, lands, send_s, recv_s, gather, me, peers, w, k, True).wait_recv()

    res = pl.pallas_call(
        body, name=name,
        in_specs=[_HBM] * (2 * n) + [_SEM, _SEM, pl.BlockSpec(memory_space=pl.ANY)],
        out_specs=[_HBM] * (2 * n),
        out_shape=[pltpu.HBM(a.shape, a.dtype) for a in list(ins_thru) + list(lands_thru)],
        input_output_aliases={i: i for i in range(2 * n)},
        compiler_params=pltpu.CompilerParams(has_side_effects=_EFFECT),
    )(*ins_thru, *lands_thru, send_sems, recv_sems, after)
    return res[:n], res[n:2 * n]


def _ordered_sum(s_ref, own_ref):
    if own_ref is None:
        blocks = [s_ref[q] for q in range(N_DEV)]
    else:
        me = 4 * lax.axis_index("x") + 2 * lax.axis_index("y") + lax.axis_index("c")
        own = own_ref[...]
        blocks = [jnp.where(me == q, own, s_ref[q]) for q in range(N_DEV)]
    acc = blocks[0]
    for b in blocks[1:]:
        acc = acc + b
    return acc


def _sum8(stack, own, *, name):
    _, R, C = stack.shape
    if R % 8 == 0:
        tr, tc = _pick(R, max(8, STEP_BYTES // (C * 4 * (N_DEV + 2))), 8), C
    else:
        tr, tc = R, _pick(C, max(LANES, STEP_BYTES // (R * 4 * (N_DEV + 2))))

    def body(s_ref, own_ref, o_ref):
        o_ref[...] = _ordered_sum(s_ref, own_ref)

    blk = pl.BlockSpec((tr, tc), lambda i, j: (i, j))
    return pl.pallas_call(
        body, name=name, grid=(R // tr, C // tc),
        in_specs=[pl.BlockSpec((N_DEV, tr, tc), lambda i, j: (0, i, j)), blk],
        out_specs=blk,
        out_shape=jax.ShapeDtypeStruct((R, C), F32),
        compiler_params=_cparams("parallel", "parallel"),
    )(stack, own)


def _adamw_math(w, g, m, v):
    m = ADAM_B1 * m + (1.0 - ADAM_B1) * g
    v = ADAM_B2 * v + (1.0 - ADAM_B2) * (g * g)
    m_hat = m / (1.0 - ADAM_B1 ** ADAM_STEP)
    v_hat = v / (1.0 - ADAM_B2 ** ADAM_STEP)
    delta = -ADAM_LR * (m_hat / (jnp.sqrt(v_hat) + ADAM_EPS) + ADAM_WD * w)
    return delta, m, v


def _adamw(w, g, m, v, *, name, stacked, own=None):
    R, C = w.shape
    tr = _pick(R, max(8, STEP_BYTES // (C * 4 * (8 + (N_DEV if stacked else 1)))), 8)
    has_own = own is not None

    def body(w_ref, g_ref, m_ref, v_ref, *rest):
        go_ref, d_ref, mo_ref, vo_ref = rest[-4:]
        g = _ordered_sum(g_ref, rest[0] if has_own else None) if stacked else g_ref[...]
        delta, m2, v2 = _adamw_math(w_ref[...], g, m_ref[...], v_ref[...])
        go_ref[...] = g
        d_ref[...] = delta
        mo_ref[...] = m2
        vo_ref[...] = v2

    row = pl.BlockSpec((tr, C), lambda i: (i, 0))
    g_spec = pl.BlockSpec((N_DEV, tr, C), lambda i: (0, i, 0)) if stacked else row
    return pl.pallas_call(
        body, name=name, grid=(R // tr,),
        in_specs=[row, g_spec, row, row] + [row] * has_own, out_specs=[row] * 4,
        out_shape=[jax.ShapeDtypeStruct((R, C), F32)] * 4,
        compiler_params=_cparams("parallel"),
    )(w, g, m, v, *([own] if has_own else []))


def kernel(x, positions, attn_norm, w_in, fox_f_bias, swa_sinks, w_branch_swa, w_branch_fox, w_out, mlp_norm, w_up, w_down, final_norm, loss_target, m_attn_norm, m_w_in, m_fox_f_bias, m_swa_sinks, m_w_branch_swa, m_w_branch_fox, m_w_out, m_mlp_norm, m_w_up, m_w_down, m_final_norm, v_attn_norm, v_w_in, v_fox_f_bias, v_swa_sinks, v_w_branch_swa, v_w_branch_fox, v_w_out, v_mlp_norm, v_w_up, v_w_down, v_final_norm):
    S, D = x.shape[1], x.shape[2]
    DFF = w_up.shape[2] * N_DEV
    d_in = w_in.shape[2] * N_DEV
    assert d_in == QKV_W + FOX_HEADS + 2 * D and (2 * D) % SWA_Q_W == 0 and S % (4 * LANES) == 0
    q_off = 2 * D
    k_off = q_off + SWA_Q_W
    v_off = k_off + SWA_KV_W
    fq_off = v_off + SWA_KV_W
    fk_off = fq_off + FOX_W
    fv_off = fk_off + FOX_W
    fl_off = fv_off + FOX_W
    NP = fl_off + FL_PAD
    x2d, tgt = x[0], loss_target[0]

    shards = [w_in[0].T.astype(BF16), w_branch_swa[0].T.astype(BF16), w_branch_fox[0].T.astype(BF16),
              w_out[0].astype(BF16), w_up[0].T.astype(BF16), w_down[0].astype(BF16)]
    me = 4 * lax.axis_index("x") + 2 * lax.axis_index("y") + lax.axis_index("c")

    def filled(stack, own):
        return lax.dynamic_update_slice(stack, own[None], (me,) + (0,) * own.ndim)

    h_in, tok_in = _exchange_start(shards[:1], gather=True, name="gather_w_in_start")
    h_rest, tok_rest = _exchange_start(shards[1:], gather=True, name="gather_rest_start")

    tm = _pick(S, 1024)
    td = _pick(D, 1024)
    tf = _pick(DFF, 1024)
    tnp = _pick(NP, 1024)

    h1 = _rms_fwd(x2d, attn_norm, name="rms1", deps=[tok_in, tok_rest])
    (s_in,), (g_in,) = _exchange_wait(h_in, h1, name="gather_w_in_wait")
    w_in_t = filled(g_in, s_in).reshape(d_in, D)
    w_in_p = jnp.concatenate([w_in_t[QKV_W + FOX_HEADS:], w_in_t[:QKV_W], w_in_t[QKV_W:QKV_W + FOX_HEADS],
                              jnp.zeros((FL_PAD - FOX_HEADS, D), BF16)], axis=0)
    w_fl_t = w_in_t[QKV_W:QKV_W + FOX_HEADS]
    proj, = _matmul(h1, w_in_p, mode="nt", name="mm_in", out_dtypes=[BF16], tm=tm, tn=tnp, tk=D)
    z_t, = _matmul(w_fl_t, h1, mode="nt", name="mm_flogit", out_dtypes=[F32],
                   tm=FOX_HEADS, tn=_pick(S, 2048), tk=D)
    bias_col = fox_f_bias.reshape(FOX_HEADS, 1)
    negc = _fox_prep(z_t, bias_col, name="fox_prep")
    bq = _fox_block(S)
    negc4 = negc.reshape(FOX_HEADS // 2, 2, S // bq, bq).transpose(0, 2, 1, 3)
    inv_freq = ROPE_THETA ** (-jnp.arange(0, HEAD_DIM, 2, dtype=F32) / HEAD_DIM)
    invf = jnp.tile(inv_freq, LANES // (HEAD_DIM // 2)).reshape(1, LANES)
    cos_t, sin_t = _rope_tables(positions.reshape(S, 1), invf, name="rope_tables")
    q_rope, k_rope = _rope_fwd(proj, cos_t, sin_t, q_off=q_off, k_off=k_off, name="rope_fwd")
    sinks = swa_sinks.reshape(-1)
    o_a = _swa_fwd(q_rope, k_rope, proj, sinks, v_off=v_off, name="swa_fwd")
    o_b, lse = _fox_fwd(proj, negc4, q_off=fq_off, k_off=fk_off, v_off=fv_off, name="fox_fwd")
    s_rest, g_rest = _exchange_wait(h_rest, o_b, name="gather_rest_wait")
    g_bs, g_bf, g_o, g_up, g_dn = [filled(g, s) for g, s in zip(g_rest, s_rest)]
    w_bs_t = g_bs.reshape(D, SWA_Q_W)
    w_bf_t = g_bf.reshape(D, FOX_W)
    w_o = g_o.reshape(D, D)
    w_up_t = g_up.reshape(DFF, D)
    w_dn = g_dn.reshape(DFF, D)
    ya, = _matmul(o_a, w_bs_t, mode="nt", name="mm_branch_swa", out_dtypes=[BF16], tm=tm, tn=td, tk=SWA_Q_W)
    gate_maps = [lambda i, j, k: (i, j), lambda i, j, k: (i, j), lambda i, j, k: (i, j + D // td)]

    def merge_epi(acc, ya_t, ga_t, gb_t):
        merged = _sigmoid(ga_t.astype(F32)) * ya_t.astype(F32) + _sigmoid(gb_t.astype(F32)) * acc
        return acc, merged

    yb, merged = _matmul(o_b, w_bf_t, mode="nt", name="mm_branch_fox", out_dtypes=[BF16, BF16],
                         tm=tm, tn=td, tk=FOX_W, extras=[ya, proj, proj], extra_maps=gate_maps,
                         epilogue=merge_epi)
    x_mid, = _matmul(merged, w_o, mode="nn", name="mm_out", out_dtypes=[F32], tm=tm, tn=td, tk=D,
                     extras=[x2d], epilogue=lambda acc, r: (acc + r,))
    h2 = _rms_fwd(x_mid, mlp_norm, name="rms2")
    u, = _matmul(h2, w_up_t, mode="nt", name="mm_up", out_dtypes=[BF16], tm=tm, tn=tf, tk=D,
                 epilogue=lambda acc: (jnp.maximum(acc, 0.0),))
    x_fin, = _matmul(u, w_dn, mode="nn", name="mm_down", out_dtypes=[F32], tm=tm, tn=td, tk=_pick(DFF, 2048),
                     a_fn=_square_bf16, extras=[x_mid], epilogue=lambda acc, r: (acc + r,))

    dx3, dx3b, dg3, loss_part = _loss_head(x_fin, tgt, final_norm.reshape(1, D), name="loss_head")
    d_up, = _matmul(dx3b, w_dn, mode="nt", name="mm_d_act", out_dtypes=[BF16], tm=tm, tn=tf, tk=D,
                    extras=[u], epilogue=lambda acc, ut: (acc * (2.0 * ut.astype(F32)),))
    tks = _pick(S, 1024)
    dw_dn, = _matmul(u, dx3b, mode="tn", name="mm_dw_down", out_dtypes=[F32], tm=tf, tn=td, tk=tks,
                     a_fn=_square_bf16)
    dh2, = _matmul(d_up, w_up_t, mode="nn", name="mm_dh2", out_dtypes=[F32], tm=tm, tn=td, tk=_pick(DFF, 2048))
    dw_up_t, = _matmul(d_up, h2, mode="tn", name="mm_dw_up", out_dtypes=[F32], tm=tf, tn=td, tk=tks)
    h_s1, tok_s1 = _exchange_start([dw_up_t.reshape(N_DEV, DFF // N_DEV, D), dw_dn.reshape(N_DEV, DFF // N_DEV, D)],
                                   gather=False, name="scatter_mlp_start")
    dx2, dx2b, dg2 = _rms_bwd(dh2, x_mid, mlp_norm, dx3, name="rms2_bwd", want_bf16=True, deps=[tok_s1])

    def gate_bwd_epi(dm, ya_t, yb_t, ga_t, gb_t):
        sa, sb = _sigmoid(ga_t.astype(F32)), _sigmoid(gb_t.astype(F32))
        return (dm * sa, dm * sb, dm * ya_t.astype(F32) * sa * (1.0 - sa), dm * yb_t.astype(F32) * sb * (1.0 - sb))

    gmaps = [lambda i, j, k: (i, j), lambda i, j, k: (i, j), lambda i, j, k: (i, j),
             lambda i, j, k: (i, j + D // td)]
    d_ya, d_yb, d_ga, d_gb = _matmul(dx2b, w_o, mode="nt", name="mm_d_merged", out_dtypes=[BF16] * 4,
                                     tm=tm, tn=td, tk=D, extras=[ya, yb, proj, proj], extra_maps=gmaps,
                                     epilogue=gate_bwd_epi)
    dw_o, = _matmul(merged, dx2b, mode="tn", name="mm_dw_out", out_dtypes=[F32], tm=td, tn=td, tk=tks)
    d_oa, = _matmul(d_ya, w_bs_t, mode="nn", name="mm_d_oa", out_dtypes=[BF16], tm=tm, tn=SWA_Q_W, tk=D)
    d_ob, = _matmul(d_yb, w_bf_t, mode="nn", name="mm_d_ob", out_dtypes=[BF16], tm=tm, tn=FOX_W, tk=D)
    dw_bs_t, = _matmul(d_ya, o_a, mode="tn", name="mm_dw_bs", out_dtypes=[F32], tm=td, tn=SWA_Q_W, tk=tks)
    dw_bf_t, = _matmul(d_yb, o_b, mode="tn", name="mm_dw_bf", out_dtypes=[F32], tm=td, tn=FOX_W, tk=tks)
    h_s2, tok_s2 = _exchange_start([dw_bs_t.reshape(N_DEV, D // N_DEV, SWA_Q_W),
                                    dw_bf_t.reshape(N_DEV, D // N_DEV, FOX_W), dw_o.reshape(N_DEV, D // N_DEV, D)],
                                   gather=False, name="scatter_attn_start")
    d_fq, d_fk, d_fv, dcol4, drow3 = _fox_bwd(proj, negc4, o_b, lse, d_ob, q_off=fq_off, k_off=fk_off,
                                              v_off=fv_off, name="fox_bwd", deps=[tok_s2])
    dcol = dcol4.transpose(0, 2, 1, 3).reshape(FOX_HEADS, S)
    drow = drow3[:, :, ::HEAD_DIM].transpose(0, 2, 1).reshape(FOX_HEADS, S)
    dz_t, dbias_l = _fox_post(drow, dcol, z_t, bias_col, name="fox_post")
    dq_r, dk_c, dk_p, dv_c, dv_p, dsink_l = _swa_bwd(q_rope, k_rope, proj, sinks, d_oa, v_off=v_off, name="swa_bwd")
    d_aq, d_ak, d_av = _rope_bwd(dq_r, dk_c, dk_p, dv_c, dv_p, cos_t, sin_t, name="rope_bwd")
    dz_pad = jnp.pad(dz_t.T.astype(BF16), ((0, 0), (0, FL_PAD - FOX_HEADS)))
    d_proj = jnp.concatenate([d_ga, d_gb, d_aq, d_ak, d_av, d_fq.astype(BF16), d_fk, d_fv, dz_pad], axis=1)
    tkp = _pick(NP, 2304)
    dw_in_p, = _matmul(d_proj, h1, mode="tn", name="mm_dw_in", out_dtypes=[F32], tm=_pick(NP, 512), tn=D, tk=tks)
    dw_in_t = jnp.concatenate([dw_in_p[q_off:q_off + QKV_W], dw_in_p[fl_off:fl_off + FOX_HEADS], dw_in_p[:q_off]],
                              axis=0)
    h_s3, tok_s3 = _exchange_start([dw_in_t.reshape(N_DEV, d_in // N_DEV, D)], gather=False,
                                   name="scatter_in_start")
    dh1, = _matmul(d_proj, w_in_p, mode="nn", name="mm_dh1", out_dtypes=[F32], tm=tm, tn=td, tk=tkp, deps=[tok_s3])
    dx, dg1 = _rms_bwd(dh1, x2d, attn_norm, dx2, name="rms1_bwd", want_bf16=False)

    dbias = dbias_l[:, 0]
    dsinks = dsink_l[:, :, 0].reshape(-1)
    nsm = 3 * D + 2 * LANES
    tail = jnp.zeros((2 * LANES,), F32)
    small_g = jnp.concatenate([dg1[0], dg2[0], dg3[0],
                               tail.at[0:16].set(dbias).at[16:32].set(dsinks).at[32].set(loss_part[0, 0])])

    def pack(a_norm, b_norm, f_norm, bias, snk):
        return jnp.concatenate([a_norm[0], b_norm[0], f_norm,
                                tail.at[0:16].set(bias[0]).at[16:32].set(snk[0])]).reshape(1, nsm)

    small_stack, = _exchange([small_g.reshape(1, nsm)], gather=True, name="gather_small")
    u_sm = _adamw(pack(attn_norm, mlp_norm, final_norm, fox_f_bias, swa_sinks), small_stack,
                  pack(m_attn_norm, m_mlp_norm, m_final_norm, m_fox_f_bias, m_swa_sinks),
                  pack(v_attn_norm, v_mlp_norm, v_final_norm, v_fox_f_bias, v_swa_sinks),
                  name="adamw_small", stacked=True)
    loss = u_sm[0][0, 3 * D + 32]

    def own_of(src):
        return lax.dynamic_index_in_dim(src, me, 0, keepdims=False)

    def update_t(stack, src, w, m, v, nm):
        g = _sum8(stack, own_of(src), name="sum_" + nm).T
        return _adamw(w[0], g, m[0], v[0], name="adamw_" + nm, stacked=False)

    def update(stack, src, w, m, v, nm):
        return _adamw(w[0], stack, m[0], v[0], name="adamw_" + nm, stacked=True, own=own_of(src))

    (s_up, s_dn), (r_up, r_dn) = _exchange_wait(h_s1, u_sm[1], name="scatter_mlp_wait")
    u_up = update_t(r_up, s_up, w_up, m_w_up, v_w_up, "w_up")
    u_dn = update(r_dn, s_dn, w_down, m_w_down, v_w_down, "w_down")
    (s_bs, s_bf, s_o), (r_bs, r_bf, r_o) = _exchange_wait(h_s2, u_dn[1], name="scatter_attn_wait")
    u_bs = update_t(r_bs, s_bs, w_branch_swa, m_w_branch_swa, v_w_branch_swa, "w_bs")
    u_bf = update_t(r_bf, s_bf, w_branch_fox, m_w_branch_fox, v_w_branch_fox, "w_bf")
    u_o = update(r_o, s_o, w_out, m_w_out, v_w_out, "w_out")
    (s_w_in,), (r_in,) = _exchange_wait(h_s3, u_o[1], name="scatter_in_wait")
    u_in = update_t(r_in, s_w_in, w_in, m_w_in, v_w_in, "w_in")

    def small(kind):
        a = u_sm[kind][0]
        return dict(attn_norm=a[0:D][None], mlp_norm=a[D:2 * D][None], final_norm=a[2 * D:3 * D],
                    fox_f_bias=a[3 * D:3 * D + 16][None], swa_sinks=a[3 * D + 16:3 * D + 32][None])

    big = dict(w_in=u_in, w_branch_swa=u_bs, w_branch_fox=u_bf, w_out=u_o, w_up=u_up, w_down=u_dn)
    order = ["attn_norm", "w_in", "fox_f_bias", "swa_sinks", "w_branch_swa", "w_branch_fox", "w_out", "mlp_norm",
             "w_up", "w_down", "final_norm"]
    outs = [loss, dx[None]]
    for kind in range(4):
        sm = small(kind)
        for nm in order:
            outs.append(big[nm][kind][None] if nm in big else sm[nm])
    return tuple(outs)
```

```python
import functools

import jax
import jax.numpy as jnp
from jax import lax
from jax.experimental import pallas as pl
from jax.experimental.pallas import tpu as pltpu

F32 = jnp.float32
BF16 = jnp.bfloat16

N_DEV = 8
HEAD_DIM = 64
SWA_Q_W = 1024
SWA_KV_W = 128
SWA_GROUP = 8
WINDOW = 128
FOX_W = 1024
FOX_HEADS = 16
QKV_W = SWA_Q_W + 2 * SWA_KV_W + 3 * FOX_W
FL_PAD = 256
ROPE_THETA = 10000.0
RMS_EPS = 1e-6
ATT_SCALE = 0.125
NEG = -1e30

ADAM_LR = 0.001
ADAM_B1 = 0.9
ADAM_B2 = 0.999
ADAM_EPS = 1e-08
ADAM_WD = 0.01
ADAM_STEP = 10

LANES = 128
VMEM_LIMIT = 56 * 1024 * 1024
STEP_BYTES = 12 * 1024 * 1024


def _cparams(*sem):
    return pltpu.CompilerParams(dimension_semantics=sem, vmem_limit_bytes=VMEM_LIMIT)


def _pick(dim, pref, align=LANES):
    best = None
    t = align
    while t <= min(dim, pref):
        if dim % t == 0:
            best = t
        t += align
    return best if best is not None else dim


_DIMS = {"nn": ((1,), (0,)), "nt": ((1,), (1,)), "tn": ((0,), (0,))}


_ANY = pl.BlockSpec(memory_space=pl.ANY)


def _matmul(a, b, *, mode, name, out_dtypes, tm, tn, tk, extras=(), extra_maps=None,
            a_fn=None, epilogue=None, deps=()):
    if mode == "nn":
        (M, K), (K2, N) = a.shape, b.shape
    elif mode == "nt":
        (M, K), (N, K2) = a.shape, b.shape
    else:
        (K, M), (K2, N) = a.shape, b.shape
    assert K == K2, (name, a.shape, b.shape)
    assert M % tm == 0 and N % tn == 0 and K % tk == 0, (name, M, N, K, tm, tn, tk)
    nk = K // tk
    ne, no = len(extras), len(out_dtypes)
    dims = (_DIMS[mode], ((), ()))

    def body(*refs):
        a_ref, b_ref = refs[0], refs[1]
        ex_refs = refs[2:2 + ne]
        out_refs = refs[2 + ne + len(deps):2 + ne + len(deps) + no]

        def finish(acc):
            res = (acc,) if epilogue is None else epilogue(acc, *[e[...] for e in ex_refs])
            for o_ref, r in zip(out_refs, res):
                o_ref[...] = r.astype(o_ref.dtype)

        av = a_ref[...]
        if a_fn is not None:
            av = a_fn(av)
        part = lax.dot_general(av, b_ref[...], dims, preferred_element_type=F32)
        if nk == 1:
            finish(part)
        else:
            acc_ref = refs[-1]
            k = pl.program_id(2)

            @pl.when(k == 0)
            def _():
                acc_ref[...] = part

            @pl.when(k > 0)
            def _():
                acc_ref[...] += part

            @pl.when(k == nk - 1)
            def _():
                finish(acc_ref[...])

    if mode == "tn":
        a_spec = pl.BlockSpec((tk, tm), lambda i, j, k: (k, i))
    else:
        a_spec = pl.BlockSpec((tm, tk), lambda i, j, k: (i, k))
    if mode == "nt":
        b_spec = pl.BlockSpec((tn, tk), lambda i, j, k: (j, k))
    else:
        b_spec = pl.BlockSpec((tk, tn), lambda i, j, k: (k, j))
    if extra_maps is None:
        extra_maps = [lambda i, j, k: (i, j)] * ne
    ex_specs = [pl.BlockSpec((tm, tn), m) for m in extra_maps]
    out_spec = [pl.BlockSpec((tm, tn), lambda i, j, k: (i, j)) for _ in range(no)]
    res = pl.pallas_call(
        body,
        name=name,
        grid=(M // tm, N // tn, nk),
        in_specs=[a_spec, b_spec] + ex_specs + [_ANY] * len(deps),
        out_specs=out_spec,
        out_shape=[jax.ShapeDtypeStruct((M, N), d) for d in out_dtypes],
        scratch_shapes=[pltpu.VMEM((tm, tn), F32)] if nk > 1 else [],
        compiler_params=_cparams("parallel", "parallel", "arbitrary"),
    )(a, b, *extras, *deps)
    return res


def _square_bf16(t):
    tf = t.astype(F32)
    return (tf * tf).astype(BF16)


def _sigmoid(g):
    return 1.0 / (1.0 + jnp.exp(-g))


def _rms_fwd(x, gain, *, name, deps=()):
    S, D = x.shape
    tr = _pick(S, 512, 8)

    def body(x_ref, g_ref, *rest):
        h_ref = rest[-1]
        xv = x_ref[...]
        r = lax.rsqrt(jnp.mean(xv * xv, axis=-1, keepdims=True) + RMS_EPS)
        h_ref[...] = (xv * r * g_ref[...]).astype(BF16)

    return pl.pallas_call(
        body, name=name, grid=(S // tr,),
        in_specs=[pl.BlockSpec((tr, D), lambda i: (i, 0)), pl.BlockSpec((1, D), lambda i: (0, 0))] + [_ANY] * len(deps),
        out_specs=pl.BlockSpec((tr, D), lambda i: (i, 0)),
        out_shape=jax.ShapeDtypeStruct((S, D), BF16),
        compiler_params=_cparams("parallel"),
    )(x, gain, *deps)


def _rms_bwd(dh, x, gain, dres, *, name, want_bf16, deps=()):
    S, D = x.shape
    tr = _pick(S, 256, 8)

    def body(dh_ref, x_ref, g_ref, dres_ref, *rest):
        outs = rest[len(deps):]
        dx_ref, dg_ref = outs[0], outs[-1]
        xv = x_ref[...]
        r = lax.rsqrt(jnp.mean(xv * xv, axis=-1, keepdims=True) + RMS_EPS)
        xh = xv * r
        dhv = dh_ref[...]
        t = dhv * g_ref[...]
        dx = r * (t - xh * jnp.mean(t * xh, axis=-1, keepdims=True)) + dres_ref[...]
        dx_ref[...] = dx
        if want_bf16:
            outs[1][...] = dx.astype(BF16)
        part = jnp.sum(dhv * xh, axis=0, keepdims=True)

        @pl.when(pl.program_id(0) == 0)
        def _():
            dg_ref[...] = part

        @pl.when(pl.program_id(0) > 0)
        def _():
            dg_ref[...] += part

    row = pl.BlockSpec((tr, D), lambda i: (i, 0))
    vec = pl.BlockSpec((1, D), lambda i: (0, 0))
    out_shape = [jax.ShapeDtypeStruct((S, D), F32)]
    out_specs = [row]
    if want_bf16:
        out_shape.append(jax.ShapeDtypeStruct((S, D), BF16))
        out_specs.append(row)
    out_shape.append(jax.ShapeDtypeStruct((1, D), F32))
    out_specs.append(vec)
    return pl.pallas_call(
        body, name=name, grid=(S // tr,),
        in_specs=[row, row, vec, row] + [_ANY] * len(deps), out_specs=out_specs, out_shape=out_shape,
        compiler_params=_cparams("arbitrary"),
    )(dh, x, gain, dres, *deps)


def _loss_head(x3, target, gain, *, name):
    S, D = x3.shape
    tr = _pick(S, 256, 8)

    def body(x_ref, t_ref, g_ref, dx_ref, dxb_ref, dg_ref, loss_ref):
        xv = x_ref[...]
        r = lax.rsqrt(jnp.mean(xv * xv, axis=-1, keepdims=True) + RMS_EPS)
        xh = xv * r
        gv = g_ref[...]
        err = xh * gv - t_ref[...]
        lpart = jnp.zeros((1, LANES), F32) + (0.5 / D) * jnp.sum(err * err)
        dy = err * (1.0 / D)
        t = dy * gv
        dx = r * (t - xh * jnp.mean(t * xh, axis=-1, keepdims=True))
        dx_ref[...] = dx
        dxb_ref[...] = dx.astype(BF16)
        part = jnp.sum(dy * xh, axis=0, keepdims=True)

        @pl.when(pl.program_id(0) == 0)
        def _():
            dg_ref[...] = part
            loss_ref[...] = lpart

        @pl.when(pl.program_id(0) > 0)
        def _():
            dg_ref[...] += part
            loss_ref[...] += lpart

    row = pl.BlockSpec((tr, D), lambda i: (i, 0))
    vec = pl.BlockSpec((1, D), lambda i: (0, 0))
    return pl.pallas_call(
        body, name=name, grid=(S // tr,),
        in_specs=[row, row, vec],
        out_specs=[row, row, vec, pl.BlockSpec((1, LANES), lambda i: (0, 0))],
        out_shape=[jax.ShapeDtypeStruct((S, D), F32), jax.ShapeDtypeStruct((S, D), BF16),
                   jax.ShapeDtypeStruct((1, D), F32), jax.ShapeDtypeStruct((1, LANES), F32)],
        compiler_params=_cparams("arbitrary"),
    )(x3, target, gain)


def _rope_tables(pos_col, invf, *, name):
    S = pos_col.shape[0]
    tr = _pick(S, 512, 8)

    def body(p_ref, f_ref, cos_ref, sin_ref):
        ang = p_ref[...].astype(F32) * f_ref[...]
        lane = lax.broadcasted_iota(jnp.int32, (1, LANES), 1)
        first = (lane % HEAD_DIM) < HEAD_DIM // 2
        sn = jnp.sin(ang)
        cos_ref[...] = jnp.cos(ang)
        sin_ref[...] = jnp.where(first, -sn, sn)

    return pl.pallas_call(
        body, name=name, grid=(S // tr,),
        in_specs=[pl.BlockSpec((tr, 1), lambda i: (i, 0)), pl.BlockSpec((1, LANES), lambda i: (0, 0))],
        out_specs=[pl.BlockSpec((tr, LANES), lambda i: (i, 0))] * 2,
        out_shape=[jax.ShapeDtypeStruct((S, LANES), F32)] * 2,
        compiler_params=_cparams("parallel"),
    )(pos_col, invf)


def _swap_halves(t):
    lane = lax.broadcasted_iota(jnp.int32, (1, LANES), 1)
    first = (lane % HEAD_DIM) < HEAD_DIM // 2
    return jnp.where(first, pltpu.roll(t, LANES - HEAD_DIM // 2, 1), pltpu.roll(t, HEAD_DIM // 2, 1))


def _rope_fwd(proj, cos_t, sin_t, *, q_off, k_off, name):
    S = proj.shape[0]
    tr = _pick(S, 256, 8)
    nqb = SWA_Q_W // LANES

    def body(q_ref, k_ref, c_ref, s_ref, qo_ref, ko_ref):
        cv, sv = c_ref[...], s_ref[...]
        for b in range(nqb):
            t = q_ref[:, b * LANES:(b + 1) * LANES].astype(F32)
            qo_ref[:, b * LANES:(b + 1) * LANES] = (t * cv + _swap_halves(t) * sv).astype(BF16)
        t = k_ref[...].astype(F32)
        ko_ref[...] = (t * cv + _swap_halves(t) * sv).astype(BF16)

    tab = pl.BlockSpec((tr, LANES), lambda i: (i, 0))
    return pl.pallas_call(
        body, name=name, grid=(S // tr,),
        in_specs=[pl.BlockSpec((tr, SWA_Q_W), lambda i: (i, q_off // SWA_Q_W)),
                  pl.BlockSpec((tr, LANES), lambda i: (i, k_off // LANES)), tab, tab],
        out_specs=[pl.BlockSpec((tr, SWA_Q_W), lambda i: (i, 0)), tab],
        out_shape=[jax.ShapeDtypeStruct((S, SWA_Q_W), BF16), jax.ShapeDtypeStruct((S, LANES), BF16)],
        compiler_params=_cparams("parallel"),
    )(proj, proj, cos_t, sin_t)


def _rope_bwd(dq, dk_cur, dk_prev, dv_cur, dv_prev, cos_t, sin_t, *, name):
    S = dq.shape[0]
    tr = WINDOW
    nb = S // tr
    nqb = SWA_Q_W // LANES

    def body(dq_ref, kc_ref, kp_ref, vc_ref, vp_ref, c_ref, s_ref, dqo_ref, dko_ref, dvo_ref):
        cv, sv = c_ref[...], s_ref[...]
        has_next = (pl.program_id(0) + 1 < nb).astype(F32)
        for b in range(nqb):
            d = dq_ref[:, b * LANES:(b + 1) * LANES]
            dqo_ref[:, b * LANES:(b + 1) * LANES] = (d * cv + _swap_halves(d * sv)).astype(BF16)
        d = kc_ref[0] + kc_ref[1] + has_next * (kp_ref[0] + kp_ref[1])
        dko_ref[...] = (d * cv + _swap_halves(d * sv)).astype(BF16)
        dvo_ref[...] = (vc_ref[0] + vc_ref[1] + has_next * (vp_ref[0] + vp_ref[1])).astype(BF16)

    tab = pl.BlockSpec((tr, LANES), lambda i: (i, 0))
    cur = pl.BlockSpec((2, tr, LANES), lambda i: (0, i, 0))
    nxt = pl.BlockSpec((2, tr, LANES), lambda i: (0, jnp.minimum(i + 1, nb - 1), 0))
    return pl.pallas_call(
        body, name=name, grid=(nb,),
        in_specs=[pl.BlockSpec((tr, SWA_Q_W), lambda i: (i, 0)), cur, nxt, cur, nxt, tab, tab],
        out_specs=[pl.BlockSpec((tr, SWA_Q_W), lambda i: (i, 0)), tab, tab],
        out_shape=[jax.ShapeDtypeStruct((S, SWA_Q_W), BF16), jax.ShapeDtypeStruct((S, LANES), BF16),
                   jax.ShapeDtypeStruct((S, LANES), BF16)],
        compiler_params=_cparams("parallel"),
    )(dq, dk_cur, dk_prev, dv_cur, dv_prev, cos_t, sin_t)


def _dot_nt(a, b):
    return lax.dot_general(a, b, (((1,), (1,)), ((), ())), preferred_element_type=F32)


def _dot_tn(a, b):
    return lax.dot_general(a, b, (((0,), (0,)), ((), ())), preferred_element_type=F32)


def _dot_nn(a, b):
    return lax.dot_general(a, b, (((1,), (0,)), ((), ())), preferred_element_type=F32)


def _roll_half(t):
    return pltpu.roll(t.astype(F32), HEAD_DIM, 1).astype(t.dtype)


def _swa_common(hk, n, kp_ref, kc_ref, vp_ref, vc_ref):
    k2 = jnp.concatenate([kp_ref[...], kc_ref[...]], axis=0)
    v2 = jnp.concatenate([vp_ref[...], vc_ref[...]], axis=0)
    k_sw, v_sw = _roll_half(k2), _roll_half(v2)
    row = lax.broadcasted_iota(jnp.int32, (WINDOW, 2 * WINDOW), 0)
    col = lax.broadcasted_iota(jnp.int32, (WINDOW, 2 * WINDOW), 1)
    diff = row + WINDOW - col
    allowed = (diff >= 0) & (diff < WINDOW) & ((col >= WINDOW) | (n > 0))
    lane = lax.broadcasted_iota(jnp.int32, (1, LANES), 1)
    half = [lane < HEAD_DIM, lane >= HEAD_DIM]
    kk = [jnp.where(hk == a, k2, k_sw) for a in range(2)]
    vv = [jnp.where(hk == a, v2, v_sw) for a in range(2)]
    return allowed, half, kk, vv


def _swa_probs(qm, kk, allowed, sink):
    s = jnp.where(allowed, _dot_nt(qm, kk), NEG)
    m = jnp.maximum(jnp.max(s, axis=1, keepdims=True), sink)
    e = jnp.exp(s - m)
    es = jnp.exp(sink - m)
    inv = 1.0 / (jnp.sum(e, axis=1, keepdims=True) + es)
    return e * inv, es * inv


def _swa_fwd(q_rope, k_rope, proj, sinks, *, v_off, name):
    S = q_rope.shape[0]
    nb = S // WINDOW
    gw = SWA_GROUP * HEAD_DIM

    def body(sink_ref, q_ref, kp_ref, kc_ref, vp_ref, vc_ref, o_ref):
        hk, n = pl.program_id(0), pl.program_id(1)
        allowed, half, kk, vv = _swa_common(hk, n, kp_ref, kc_ref, vp_ref, vc_ref)
        for t in range(SWA_GROUP // 2):
            qp = q_ref[:, t * LANES:(t + 1) * LANES] * jnp.asarray(ATT_SCALE, BF16)
            outs = []
            for a in range(2):
                qm = jnp.where(half[a], qp, jnp.zeros_like(qp))
                p, _ = _swa_probs(qm, kk[a], allowed, sink_ref[hk * SWA_GROUP + 2 * t + a])
                outs.append(_dot_nn(p.astype(BF16), vv[a]))
            o_ref[:, t * LANES:(t + 1) * LANES] = jnp.where(half[0], outs[0], outs[1]).astype(BF16)

    prev = lambda hk, n: (jnp.maximum(n - 1, 0), 0)
    cur = lambda hk, n: (n, 0)
    vprev = lambda hk, n: (jnp.maximum(n - 1, 0), v_off // LANES)
    vcur = lambda hk, n: (n, v_off // LANES)
    blk = lambda m: pl.BlockSpec((WINDOW, LANES), m)
    return pl.pallas_call(
        body, name=name, grid=(2, nb),
        in_specs=[pl.BlockSpec(memory_space=pltpu.SMEM),
                  pl.BlockSpec((WINDOW, gw), lambda hk, n: (n, hk)),
                  blk(prev), blk(cur), blk(vprev), blk(vcur)],
        out_specs=pl.BlockSpec((WINDOW, gw), lambda hk, n: (n, hk)),
        out_shape=jax.ShapeDtypeStruct((S, SWA_Q_W), BF16),
        compiler_params=_cparams("parallel", "parallel"),
    )(sinks, q_rope, k_rope, k_rope, proj, proj)


def _swa_bwd(q_rope, k_rope, proj, sinks, d_o, *, v_off, name):
    S = q_rope.shape[0]
    nb = S // WINDOW
    gw = SWA_GROUP * HEAD_DIM

    def body(sink_ref, q_ref, kp_ref, kc_ref, vp_ref, vc_ref, do_ref,
             dq_ref, dkc_ref, dkp_ref, dvc_ref, dvp_ref, dsink_ref):
        hk, n = pl.program_id(0), pl.program_id(1)
        allowed, half, kk, vv = _swa_common(hk, n, kp_ref, kc_ref, vp_ref, vc_ref)
        dk_acc = jnp.zeros((2 * WINDOW, LANES), F32)
        dv_acc = jnp.zeros((2 * WINDOW, LANES), F32)
        srow = lax.broadcasted_iota(jnp.int32, (SWA_GROUP, LANES), 0)
        dsink = jnp.zeros((SWA_GROUP, LANES), F32)
        for t in range(SWA_GROUP // 2):
            qp = q_ref[:, t * LANES:(t + 1) * LANES] * jnp.asarray(ATT_SCALE, BF16)
            dop = do_ref[:, t * LANES:(t + 1) * LANES]
            dqs = []
            for a in range(2):
                g = 2 * t + a
                qm = jnp.where(half[a], qp, jnp.zeros_like(qp))
                dom = jnp.where(half[a], dop, jnp.zeros_like(dop))
                p, psink = _swa_probs(qm, kk[a], allowed, sink_ref[hk * SWA_GROUP + g])
                dp = _dot_nt(dom, vv[a])
                delta = jnp.sum(p * dp, axis=1, keepdims=True)
                ds = (p * (dp - delta)).astype(BF16)
                dsink = dsink + jnp.where(srow == g, -jnp.sum(psink * delta), 0.0)
                dqs.append(_dot_nn(ds, kk[a]) * ATT_SCALE)
                dk_acc = dk_acc + _dot_tn(ds, qm)
                dv_acc = dv_acc + _dot_tn(p.astype(BF16), dom)
            dq_ref[:, t * LANES:(t + 1) * LANES] = jnp.where(half[0], dqs[0], dqs[1])
        lane = lax.broadcasted_iota(jnp.int32, (1, LANES), 1)
        mine = (lane >= HEAD_DIM) == (hk == 1)
        dk_t = jnp.where(mine, dk_acc + pltpu.roll(dk_acc, HEAD_DIM, 1), 0.0)
        dv_t = jnp.where(mine, dv_acc + pltpu.roll(dv_acc, HEAD_DIM, 1), 0.0)
        dkp_ref[0] = dk_t[:WINDOW]
        dkc_ref[0] = dk_t[WINDOW:]
        dvp_ref[0] = dv_t[:WINDOW]
        dvc_ref[0] = dv_t[WINDOW:]

        @pl.when(n == 0)
        def _():
            dsink_ref[0] = dsink

        @pl.when(n > 0)
        def _():
            dsink_ref[0] += dsink

    prev = lambda hk, n: (jnp.maximum(n - 1, 0), 0)
    cur = lambda hk, n: (n, 0)
    vprev = lambda hk, n: (jnp.maximum(n - 1, 0), v_off // LANES)
    vcur = lambda hk, n: (n, v_off // LANES)
    blk = lambda m: pl.BlockSpec((WINDOW, LANES), m)
    qblk = pl.BlockSpec((WINDOW, gw), lambda hk, n: (n, hk))
    part = pl.BlockSpec((1, WINDOW, LANES), lambda hk, n: (hk, n, 0))
    part_shape = jax.ShapeDtypeStruct((2, S, LANES), F32)
    return pl.pallas_call(
        body, name=name, grid=(2, nb),
        in_specs=[pl.BlockSpec(memory_space=pltpu.SMEM), qblk, blk(prev), blk(cur), blk(vprev), blk(vcur), qblk],
        out_specs=[qblk, part, part, part, part,
                   pl.BlockSpec((1, SWA_GROUP, LANES), lambda hk, n: (hk, 0, 0))],
        out_shape=[jax.ShapeDtypeStruct((S, SWA_Q_W), F32), part_shape, part_shape, part_shape, part_shape,
                   jax.ShapeDtypeStruct((2, SWA_GROUP, LANES), F32)],
        compiler_params=_cparams("parallel", "arbitrary"),
    )(sinks, q_rope, k_rope, k_rope, proj, proj, d_o)


def _fox_prep(z_t, bias_col, *, name):
    H, S = z_t.shape
    tb = _pick(S, 512)

    def body(z_ref, b_ref, o_ref, carry_ref):
        @pl.when(pl.program_id(0) == 0)
        def _():
            carry_ref[...] = jnp.zeros_like(carry_ref)

        zz = z_ref[...] + b_ref[...]
        t = jnp.exp(-jnp.abs(zz))
        log1p = jnp.where(t < 1e-2, t * (1.0 - t * (0.5 - t * (1.0 / 3.0))), jnp.log(1.0 + t))
        logf = jnp.minimum(zz, 0.0) - log1p
        r = lax.broadcasted_iota(jnp.int32, (tb, tb), 0)
        c = lax.broadcasted_iota(jnp.int32, (tb, tb), 1)
        tri = (r <= c).astype(BF16)
        hi = logf.astype(BF16)
        r1 = logf - hi.astype(F32)
        mid = r1.astype(BF16)
        lo = (r1 - mid.astype(F32)).astype(BF16)
        cs = _dot_nn(hi, tri) + _dot_nn(mid, tri) + _dot_nn(lo, tri) + carry_ref[:, 0:1]
        o_ref[...] = -cs
        carry_ref[...] = jnp.zeros_like(carry_ref) + cs[:, tb - 1:tb]

    return pl.pallas_call(
        body, name=name, grid=(S // tb,),
        in_specs=[pl.BlockSpec((H, tb), lambda i: (0, i)), pl.BlockSpec((H, 1), lambda i: (0, 0))],
        out_specs=pl.BlockSpec((H, tb), lambda i: (0, i)),
        out_shape=jax.ShapeDtypeStruct((H, S), F32),
        scratch_shapes=[pltpu.VMEM((H, LANES), F32)],
        compiler_params=_cparams("arbitrary"),
    )(z_t, bias_col)


def _fox_post(drow, dcol, z_t, bias_col, *, name):
    H, S = z_t.shape
    tb = _pick(S, 512)
    nb = S // tb

    def body(dr_ref, d_ref, z_ref, b_ref, dz_ref, db_ref, carry_ref):
        @pl.when(pl.program_id(0) == 0)
        def _():
            carry_ref[...] = jnp.zeros_like(carry_ref)
            db_ref[...] = jnp.zeros_like(db_ref)

        dc = dr_ref[...] - d_ref[...]
        r = lax.broadcasted_iota(jnp.int32, (tb, tb), 0)
        c = lax.broadcasted_iota(jnp.int32, (tb, tb), 1)
        tri = (r >= c).astype(BF16)
        hi = dc.astype(BF16)
        r1 = dc - hi.astype(F32)
        mid = r1.astype(BF16)
        lo = (r1 - mid.astype(F32)).astype(BF16)
        dlogf = _dot_nn(hi, tri) + _dot_nn(mid, tri) + _dot_nn(lo, tri) + carry_ref[:, 0:1]
        carry_ref[...] = jnp.zeros_like(carry_ref) + dlogf[:, 0:1]
        dz = dlogf * _sigmoid(-(z_ref[...] + b_ref[...]))
        dz_ref[...] = dz
        db_ref[...] += jnp.sum(dz, axis=1, keepdims=True)

    rev = lambda i: (0, nb - 1 - i)
    return pl.pallas_call(
        body, name=name, grid=(nb,),
        in_specs=[pl.BlockSpec((H, tb), rev), pl.BlockSpec((H, tb), rev), pl.BlockSpec((H, tb), rev),
                  pl.BlockSpec((H, 1), lambda i: (0, 0))],
        out_specs=[pl.BlockSpec((H, tb), rev), pl.BlockSpec((H, LANES), lambda i: (0, 0))],
        out_shape=[jax.ShapeDtypeStruct((H, S), F32), jax.ShapeDtypeStruct((H, LANES), F32)],
        scratch_shapes=[pltpu.VMEM((H, LANES), F32)],
        compiler_params=_cparams("arbitrary"),
    )(drow, dcol, z_t, bias_col)


def _fox_block(S):
    return min(512, max(LANES, S // 4))


def _fox_fwd(proj, negc4, *, q_off, k_off, v_off, name):
    S = proj.shape[0]
    bq = _fox_block(S)
    nq = S // bq
    npair = FOX_HEADS // 2

    def body(q_ref, k_ref, v_ref, nc_ref, o_ref, lse_ref):
        i = pl.program_id(1)
        lane = lax.broadcasted_iota(jnp.int32, (1, LANES), 1)
        half = [lane < HEAD_DIM, lane >= HEAD_DIM]
        q2 = q_ref[...] * jnp.asarray(ATT_SCALE, BF16)
        qh = [jnp.where(half[h], q2, jnp.zeros_like(q2)) for h in range(2)]
        row = lax.broadcasted_iota(jnp.int32, (bq, bq), 0)
        col = lax.broadcasted_iota(jnp.int32, (bq, bq), 1)
        causal = row >= col

        def step(j, carry, masked):
            start = pl.multiple_of(j * bq, bq)
            ks = k_ref[pl.ds(start, bq), :]
            vs = v_ref[pl.ds(start, bq), :]
            nb = nc_ref[0, j]
            new = []
            for h in range(2):
                m, l, acc = carry[3 * h:3 * h + 3]
                s = _dot_nt(qh[h], ks) + nb[h:h + 1, :]
                if masked:
                    s = jnp.where(causal, s, NEG)
                m_new = jnp.maximum(m, jnp.max(s, axis=1, keepdims=True))
                alpha = jnp.exp(m - m_new)
                p = jnp.exp(s - m_new)
                l = alpha * l + jnp.sum(p, axis=1, keepdims=True)
                acc = alpha * acc + _dot_nn(p.astype(BF16), vs)
                new += [m_new, l, acc]
            return tuple(new)

        init = (jnp.full((bq, 1), NEG, F32), jnp.zeros((bq, 1), F32), jnp.zeros((bq, LANES), F32)) * 2
        carry = lax.fori_loop(0, i, lambda j, c: step(j, c, False), init)
        carry = step(i, carry, True)
        outs, lses = [], []
        for h in range(2):
            m, l, acc = carry[3 * h:3 * h + 3]
            outs.append(acc * (1.0 / l))
            lses.append(m + jnp.log(l))
        o_ref[...] = jnp.where(half[0], outs[0], outs[1]).astype(BF16)
        lse_ref[0] = jnp.where(half[0], lses[0], lses[1])

    seq = lambda off: pl.BlockSpec((S, LANES), lambda hp, i: (0, off // LANES + hp))
    return pl.pallas_call(
        body, name=name, grid=(npair, nq),
        in_specs=[pl.BlockSpec((bq, LANES), lambda hp, i: (i, q_off // LANES + hp)), seq(k_off), seq(v_off),
                  pl.BlockSpec((1, nq, 2, bq), lambda hp, i: (hp, 0, 0, 0))],
        out_specs=[pl.BlockSpec((bq, LANES), lambda hp, i: (i, hp)),
                   pl.BlockSpec((1, bq, LANES), lambda hp, i: (hp, i, 0))],
        out_shape=[jax.ShapeDtypeStruct((S, FOX_W), BF16), jax.ShapeDtypeStruct((npair, S, LANES), F32)],
        compiler_params=_cparams("parallel", "parallel"),
    )(proj, proj, proj, negc4)


def _fox_bwd(proj, negc4, o, lse, d_o, *, q_off, k_off, v_off, name, deps=()):
    S = proj.shape[0]
    bq = _fox_block(S)
    nq = S // bq
    npair = FOX_HEADS // 2

    def body(q_ref, k_ref, v_ref, nc_ref, o_ref, lse_ref, do_ref, *rest):
        dq_ref, dk_ref, dv_ref, dn_ref, dr_ref, delta_ref, rs_ref = rest[len(deps):]
        j = pl.program_id(1)
        lane = lax.broadcasted_iota(jnp.int32, (1, LANES), 1)
        half = [lane < HEAD_DIM, lane >= HEAD_DIM]
        spare = [HEAD_DIM, 0]
        ones_lane = [lane == spare[h] for h in range(2)]
        k2, v2 = k_ref[...], v_ref[...]
        one_k = jnp.ones_like(k2)
        kh = [jnp.where(half[h], k2, jnp.where(ones_lane[h], one_k, jnp.zeros_like(k2))) for h in range(2)]
        nb = nc_ref[0, 0]
        row = lax.broadcasted_iota(jnp.int32, (bq, bq), 0)
        col = lax.broadcasted_iota(jnp.int32, (bq, bq), 1)
        causal = row >= col

        @pl.when(j == 0)
        def _():
            dq_ref[...] = jnp.zeros_like(dq_ref)
            rs_ref[...] = jnp.zeros_like(rs_ref)
            for b in range(nq):
                prod = do_ref[b * bq:(b + 1) * bq, :].astype(F32) * o_ref[b * bq:(b + 1) * bq, :].astype(F32)
                d0 = jnp.sum(jnp.where(half[0], prod, 0.0), axis=1, keepdims=True)
                d1 = jnp.sum(jnp.where(half[1], prod, 0.0), axis=1, keepdims=True)
                delta_ref[b * bq:(b + 1) * bq, :] = jnp.where(half[0], d0, d1)

        def step(i, carry, masked):
            dk_a, dk_b, dv_acc = carry
            dks = [dk_a, dk_b]
            start = pl.multiple_of(i * bq, bq)
            q2 = q_ref[pl.ds(start, bq), :] * jnp.asarray(ATT_SCALE, BF16)
            do2 = do_ref[pl.ds(start, bq), :]
            lse2 = lse_ref[0, pl.ds(start, bq), :]
            del2 = delta_ref[pl.ds(start, bq), :]
            dqf = []
            for h in range(2):
                qm = jnp.where(half[h], q2, jnp.zeros_like(q2))
                qm1 = jnp.where(ones_lane[h], jnp.ones_like(q2), qm)
                dom = jnp.where(half[h], do2, jnp.zeros_like(do2))
                c0 = h * HEAD_DIM
                p = jnp.exp(_dot_nt(qm, k2) + nb[h:h + 1, :] - lse2[:, c0:c0 + 1])
                if masked:
                    p = jnp.where(causal, p, 0.0)
                dp = _dot_nt(dom, v2)
                dsb = (p * (dp - del2[:, c0:c0 + 1])).astype(BF16)
                dv_acc = dv_acc + _dot_tn(p.astype(BF16), dom)
                dks[h] = dks[h] + _dot_tn(dsb, qm1)
                dqf.append(_dot_nn(dsb, kh[h]))
            dq_ref[pl.ds(start, bq), :] += jnp.where(half[0], dqf[0], dqf[1]) * ATT_SCALE
            rs_ref[pl.ds(start, bq), :] += jnp.where(ones_lane[0], dqf[0], jnp.where(ones_lane[1], dqf[1], 0.0))
            return dks[0], dks[1], dv_acc

        zero = jnp.zeros((bq, LANES), F32)
        carry = step(j, (zero, zero, zero), True)
        dk_a, dk_b, dv_acc = lax.fori_loop(j + 1, nq, lambda i, c: step(i, c, False), carry)
        dk_ref[...] = jnp.where(half[0], dk_a, dk_b).astype(BF16)
        dv_ref[...] = dv_acc.astype(BF16)
        dn_ref[0, 0] = jnp.concatenate([dk_a.T[spare[0]:spare[0] + 1], dk_b.T[spare[1]:spare[1] + 1]], axis=0)

        @pl.when(j == nq - 1)
        def _():
            for b in range(nq):
                t = rs_ref[b * bq:(b + 1) * bq, :].T
                dr_ref[0, b] = jnp.concatenate([t[spare[0]:spare[0] + 1], t[spare[1]:spare[1] + 1]], axis=0)

    seq = lambda off: pl.BlockSpec((S, LANES), lambda hp, j: (0, off // LANES + hp))
    blk = lambda off: pl.BlockSpec((bq, LANES), lambda hp, j: (j, off // LANES + hp))
    nc = pl.BlockSpec((1, 1, 2, bq), lambda hp, j: (hp, j, 0, 0))
    sums_shape = jax.ShapeDtypeStruct((npair, nq, 2, bq), F32)
    return pl.pallas_call(
        body, name=name, grid=(npair, nq),
        in_specs=[seq(q_off), blk(k_off), blk(v_off), nc, seq(0),
                  pl.BlockSpec((1, S, LANES), lambda hp, j: (hp, 0, 0)), seq(0)] + [_ANY] * len(deps),
        out_specs=[seq(0), blk(0), blk(0), nc, pl.BlockSpec((1, nq, 2, bq), lambda hp, j: (hp, 0, 0, 0))],
        out_shape=[jax.ShapeDtypeStruct((S, FOX_W), F32), jax.ShapeDtypeStruct((S, FOX_W), BF16),
                   jax.ShapeDtypeStruct((S, FOX_W), BF16), sums_shape, sums_shape],
        scratch_shapes=[pltpu.VMEM((S, LANES), F32), pltpu.VMEM((S, LANES), F32)],
        compiler_params=_cparams("parallel", "arbitrary"),
    )(proj, proj, proj, negc4, o, lse, d_o, *deps)


def _exchange(arrs, *, gather, name):
    n = len(arrs)
    npeer = N_DEV - 1

    def body(*refs):
        ins, outs = refs[:n], refs[n:2 * n]
        send_sems, recv_sems, loc_sems = refs[2 * n:]
        x, y, c = lax.axis_index("x"), lax.axis_index("y"), lax.axis_index("c")
        me = 4 * x + 2 * y + c
        peers = []
        for k in range(1, N_DEV):
            px = 1 - x if k & 4 else x
            py = 1 - y if k & 2 else y
            pc = 1 - c if k & 1 else c
            peers.append(((px, py, pc), 4 * px + 2 * py + pc))

        def remote(w, k):
            dev, idx = peers[k]
            src = ins[w] if gather else ins[w].at[idx]
            return pltpu.make_async_remote_copy(
                src_ref=src, dst_ref=outs[w].at[me],
                send_sem=send_sems.at[w * npeer + k], recv_sem=recv_sems.at[w * npeer + k],
                device_id=dev, device_id_type=pl.DeviceIdType.MESH)

        def arrival(w, k):
            dev, idx = peers[k]
            src = ins[w] if gather else ins[w].at[idx]
            return pltpu.make_async_remote_copy(
                src_ref=src, dst_ref=outs[w].at[idx],
                send_sem=send_sems.at[w * npeer + k], recv_sem=recv_sems.at[w * npeer + k],
                device_id=dev, device_id_type=pl.DeviceIdType.MESH)

        local = []
        for w in range(n):
            for k in range(npeer):
                remote(w, k).start()
            cp = pltpu.make_async_copy(ins[w] if gather else ins[w].at[me], outs[w].at[me], loc_sems.at[w])
            cp.start()
            local.append(cp)
        for w in range(n):
            for k in range(npeer):
                arrival(w, k).wait_recv()
        for w in range(n):
            for k in range(npeer):
                remote(w, k).wait_send()
            local[w].wait()

    hbm = pl.BlockSpec(memory_space=pl.ANY)
    out_shape = [jax.ShapeDtypeStruct((N_DEV,) + (a.shape if gather else a.shape[1:]), a.dtype) for a in arrs]
    return pl.pallas_call(
        body, name=name,
        in_specs=[hbm] * n, out_specs=[hbm] * n, out_shape=out_shape,
        scratch_shapes=[pltpu.SemaphoreType.DMA((n * npeer,)), pltpu.SemaphoreType.DMA((n * npeer,)),
                        pltpu.SemaphoreType.DMA((n,))],
        compiler_params=pltpu.CompilerParams(has_side_effects=True),
    )(*arrs)


_HBM = pl.BlockSpec(memory_space=pltpu.HBM)
_SEM = pl.BlockSpec(memory_space=pltpu.SEMAPHORE)
_EFFECT = pltpu.SideEffectType.DATAFLOW_SIDE_EFFECTING
NPEER = N_DEV - 1


def _peer_table():
    x, y, c = lax.axis_index("x"), lax.axis_index("y"), lax.axis_index("c")
    peers = []
    for k in range(1, N_DEV):
        px = 1 - x if k & 4 else x
        py = 1 - y if k & 2 else y
        pc = 1 - c if k & 1 else c
        peers.append(((px, py, pc), 4 * px + 2 * py + pc))
    return 4 * x + 2 * y + c, peers


def _split_copy(ins, lands, send_sems, recv_sems, gather, me, peers, w, k, arriving):
    dev, idx = peers[k]
    return pltpu.make_async_remote_copy(
        src_ref=ins[w] if gather else ins[w].at[idx],
        dst_ref=lands[w].at[idx if arriving else me],
        send_sem=send_sems.at[w * NPEER + k], recv_sem=recv_sems.at[w * NPEER + k],
        device_id=dev, device_id_type=pl.DeviceIdType.MESH)


def _exchange_start(arrs, *, gather, name):
    n = len(arrs)
    land_shapes = [(N_DEV,) + (a.shape if gather else a.shape[1:]) for a in arrs]

    def body(*refs):
        ins, lands = refs[:n], refs[n:2 * n]
        send_sems, recv_sems = refs[2 * n], refs[2 * n + 1]
        token = refs[-1]
        me, peers = _peer_table()
        for w in range(n):
            for k in range(NPEER):
                _split_copy(ins, lands, send_sems, recv_sems, gather, me, peers, w, k, False).start()
        token[...] = jnp.zeros_like(token)

    out_shape = ([pltpu.SemaphoreType.DMA((n * NPEER,)), pltpu.SemaphoreType.DMA((n * NPEER,))]
                 + [pltpu.HBM(a.shape, a.dtype) for a in arrs]
                 + [pltpu.HBM(s, a.dtype) for s, a in zip(land_shapes, arrs)]
                 + [jax.ShapeDtypeStruct((8, LANES), F32)])
    res = pl.pallas_call(
        body, name=name,
        in_specs=[_HBM] * (2 * n),
        out_specs=[_SEM, _SEM] + [_HBM] * (2 * n) + [pl.BlockSpec(memory_space=pltpu.VMEM)],
        out_shape=out_shape,
        input_output_aliases={i: 2 + i for i in range(2 * n)},
        compiler_params=pltpu.CompilerParams(has_side_effects=_EFFECT),
    )(*[pltpu.with_memory_space_constraint(a, pltpu.HBM) for a in arrs],
      *[pltpu.with_memory_space_constraint(lax.empty(s, a.dtype), pltpu.HBM) for s, a in zip(land_shapes, arrs)])
    return (n, gather, res[0], res[1], res[2:2 + n], res[2 + n:2 + 2 * n]), res[-1]


def _exchange_wait(handle, after, *, name):
    n, gather, send_sems, recv_sems, ins_thru, lands_thru = handle

    def body(*refs):
        ins, lands = refs[:n], refs[n:2 * n]
        send_s, recv_s = refs[2 * n], refs[2 * n + 1]
        me, peers = _peer_table()
        for w in range(n):
            for k in range(NPEER):
                _split_copy(ins, lands, send_s, recv_s, gather, me, peers, w, k, False).wait_send()
                _split_copy(ins, lands, send_s, recv_s, gather, me, peers, w, k, True).wait_recv()

    res = pl.pallas_call(
        body, name=name,
        in_specs=[_HBM] * (2 * n) + [_SEM, _SEM, pl.BlockSpec(memory_space=pl.ANY)],
        out_specs=[_HBM] * (2 * n),
        out_shape=[pltpu.HBM(a.shape, a.dtype) for a in list(ins_thru) + list(lands_thru)],
        input_output_aliases={i: i for i in range(2 * n)},
        compiler_params=pltpu.CompilerParams(has_side_effects=_EFFECT),
    )(*ins_thru, *lands_thru, send_sems, recv_sems, after)
    return res[:n], res[n:2 * n]


def _ordered_sum(s_ref, own_ref):
    if own_ref is None:
        blocks = [s_ref[q].astype(F32) for q in range(N_DEV)]
    else:
        me = 4 * lax.axis_index("x") + 2 * lax.axis_index("y") + lax.axis_index("c")
        own = own_ref[...]
        blocks = [jnp.where(me == q, own, s_ref[q]).astype(F32) for q in range(N_DEV)]
    acc = blocks[0]
    for b in blocks[1:]:
        acc = acc + b
    return acc


def _sum8(stack, own, *, name):
    _, R, C = stack.shape
    if R % 8 == 0:
        tr, tc = _pick(R, max(8, STEP_BYTES // (C * 4 * (N_DEV + 2))), 8), C
    else:
        tr, tc = R, _pick(C, max(LANES, STEP_BYTES // (R * 4 * (N_DEV + 2))))

    def body(s_ref, own_ref, o_ref):
        o_ref[...] = _ordered_sum(s_ref, own_ref)

    blk = pl.BlockSpec((tr, tc), lambda i, j: (i, j))
    return pl.pallas_call(
        body, name=name, grid=(R // tr, C // tc),
        in_specs=[pl.BlockSpec((N_DEV, tr, tc), lambda i, j: (0, i, j)), blk],
        out_specs=blk,
        out_shape=jax.ShapeDtypeStruct((R, C), F32),
        compiler_params=_cparams("parallel", "parallel"),
    )(stack, own)


def _adamw_math(w, g, m, v):
    m = ADAM_B1 * m + (1.0 - ADAM_B1) * g
    v = ADAM_B2 * v + (1.0 - ADAM_B2) * (g * g)
    m_hat = m / (1.0 - ADAM_B1 ** ADAM_STEP)
    v_hat = v / (1.0 - ADAM_B2 ** ADAM_STEP)
    delta = -ADAM_LR * (m_hat / (jnp.sqrt(v_hat) + ADAM_EPS) + ADAM_WD * w)
    return delta, m, v


def _adamw(w, g, m, v, *, name, stacked, own=None):
    R, C = w.shape
    tr = _pick(R, max(8, STEP_BYTES // (C * 4 * (8 + (N_DEV if stacked else 1)))), 8)
    has_own = own is not None

    def body(w_ref, g_ref, m_ref, v_ref, *rest):
        go_ref, d_ref, mo_ref, vo_ref = rest[-4:]
        g = _ordered_sum(g_ref, rest[0] if has_own else None) if stacked else g_ref[...]
        delta, m2, v2 = _adamw_math(w_ref[...], g, m_ref[...], v_ref[...])
        go_ref[...] = g
        d_ref[...] = delta
        mo_ref[...] = m2
        vo_ref[...] = v2

    row = pl.BlockSpec((tr, C), lambda i: (i, 0))
    g_spec = pl.BlockSpec((N_DEV, tr, C), lambda i: (0, i, 0)) if stacked else row
    return pl.pallas_call(
        body, name=name, grid=(R // tr,),
        in_specs=[row, g_spec, row, row] + [row] * has_own, out_specs=[row] * 4,
        out_shape=[jax.ShapeDtypeStruct((R, C), F32)] * 4,
        compiler_params=_cparams("parallel"),
    )(w, g, m, v, *([own] if has_own else []))


def kernel(x, positions, attn_norm, w_in, fox_f_bias, swa_sinks, w_branch_swa, w_branch_fox, w_out, mlp_norm, w_up, w_down, final_norm, loss_target, m_attn_norm, m_w_in, m_fox_f_bias, m_swa_sinks, m_w_branch_swa, m_w_branch_fox, m_w_out, m_mlp_norm, m_w_up, m_w_down, m_final_norm, v_attn_norm, v_w_in, v_fox_f_bias, v_swa_sinks, v_w_branch_swa, v_w_branch_fox, v_w_out, v_mlp_norm, v_w_up, v_w_down, v_final_norm):
    S, D = x.shape[1], x.shape[2]
    DFF = w_up.shape[2] * N_DEV
    d_in = w_in.shape[2] * N_DEV
    assert d_in == QKV_W + FOX_HEADS + 2 * D and (2 * D) % SWA_Q_W == 0 and S % (4 * LANES) == 0
    q_off = 2 * D
    k_off = q_off + SWA_Q_W
    v_off = k_off + SWA_KV_W
    fq_off = v_off + SWA_KV_W
    fk_off = fq_off + FOX_W
    fv_off = fk_off + FOX_W
    fl_off = fv_off + FOX_W
    NP = fl_off + FL_PAD
    x2d, tgt = x[0], loss_target[0]

    shards = [w_in[0].T.astype(BF16), w_branch_swa[0].T.astype(BF16), w_branch_fox[0].T.astype(BF16),
              w_out[0].astype(BF16), w_up[0].T.astype(BF16), w_down[0].astype(BF16)]
    me = 4 * lax.axis_index("x") + 2 * lax.axis_index("y") + lax.axis_index("c")

    def filled(stack, own):
        return lax.dynamic_update_slice(stack, own[None], (me,) + (0,) * own.ndim)

    h_in, tok_in = _exchange_start(shards[:1], gather=True, name="gather_w_in_start")
    h_rest, tok_rest = _exchange_start(shards[1:], gather=True, name="gather_rest_start")

    tm = _pick(S, 1024)
    td = _pick(D, 1024)
    tf = _pick(DFF, 1024)
    tnp = _pick(NP, 1024)

    h1 = _rms_fwd(x2d, attn_norm, name="rms1", deps=[tok_in, tok_rest])
    (s_in,), (g_in,) = _exchange_wait(h_in, h1, name="gather_w_in_wait")
    w_in_t = filled(g_in, s_in).reshape(d_in, D)
    w_in_p = jnp.concatenate([w_in_t[QKV_W + FOX_HEADS:], w_in_t[:QKV_W], w_in_t[QKV_W:QKV_W + FOX_HEADS],
                              jnp.zeros((FL_PAD - FOX_HEADS, D), BF16)], axis=0)
    w_fl_t = w_in_t[QKV_W:QKV_W + FOX_HEADS]
    proj, = _matmul(h1, w_in_p, mode="nt", name="mm_in", out_dtypes=[BF16], tm=tm, tn=tnp, tk=D)
    z_t, = _matmul(w_fl_t, h1, mode="nt", name="mm_flogit", out_dtypes=[F32],
                   tm=FOX_HEADS, tn=_pick(S, 2048), tk=D)
    bias_col = fox_f_bias.reshape(FOX_HEADS, 1)
    negc = _fox_prep(z_t, bias_col, name="fox_prep")
    bq = _fox_block(S)
    negc4 = negc.reshape(FOX_HEADS // 2, 2, S // bq, bq).transpose(0, 2, 1, 3)
    inv_freq = ROPE_THETA ** (-jnp.arange(0, HEAD_DIM, 2, dtype=F32) / HEAD_DIM)
    invf = jnp.tile(inv_freq, LANES // (HEAD_DIM // 2)).reshape(1, LANES)
    cos_t, sin_t = _rope_tables(positions.reshape(S, 1), invf, name="rope_tables")
    q_rope, k_rope = _rope_fwd(proj, cos_t, sin_t, q_off=q_off, k_off=k_off, name="rope_fwd")
    sinks = swa_sinks.reshape(-1)
    o_a = _swa_fwd(q_rope, k_rope, proj, sinks, v_off=v_off, name="swa_fwd")
    o_b, lse = _fox_fwd(proj, negc4, q_off=fq_off, k_off=fk_off, v_off=fv_off, name="fox_fwd")
    s_rest, g_rest = _exchange_wait(h_rest, o_b, name="gather_rest_wait")
    g_bs, g_bf, g_o, g_up, g_dn = [filled(g, s) for g, s in zip(g_rest, s_rest)]
    w_bs_t = g_bs.reshape(D, SWA_Q_W)
    w_bf_t = g_bf.reshape(D, FOX_W)
    w_o = g_o.reshape(D, D)
    w_up_t = g_up.reshape(DFF, D)
    w_dn = g_dn.reshape(DFF, D)
    ya, = _matmul(o_a, w_bs_t, mode="nt", name="mm_branch_swa", out_dtypes=[BF16], tm=tm, tn=td, tk=SWA_Q_W)
    gate_maps = [lambda i, j, k: (i, j), lambda i, j, k: (i, j), lambda i, j, k: (i, j + D // td)]

    def merge_epi(acc, ya_t, ga_t, gb_t):
        merged = _sigmoid(ga_t.astype(F32)) * ya_t.astype(F32) + _sigmoid(gb_t.astype(F32)) * acc
        return acc, merged

    yb, merged = _matmul(o_b, w_bf_t, mode="nt", name="mm_branch_fox", out_dtypes=[BF16, BF16],
                         tm=tm, tn=td, tk=FOX_W, extras=[ya, proj, proj], extra_maps=gate_maps,
                         epilogue=merge_epi)
    x_mid, = _matmul(merged, w_o, mode="nn", name="mm_out", out_dtypes=[F32], tm=tm, tn=td, tk=D,
                     extras=[x2d], epilogue=lambda acc, r: (acc + r,))
    h2 = _rms_fwd(x_mid, mlp_norm, name="rms2")
    u, = _matmul(h2, w_up_t, mode="nt", name="mm_up", out_dtypes=[BF16], tm=tm, tn=tf, tk=D,
                 epilogue=lambda acc: (jnp.maximum(acc, 0.0),))
    x_fin, = _matmul(u, w_dn, mode="nn", name="mm_down", out_dtypes=[F32], tm=tm, tn=td, tk=_pick(DFF, 2048),
                     a_fn=_square_bf16, extras=[x_mid], epilogue=lambda acc, r: (acc + r,))

    dx3, dx3b, dg3, loss_part = _loss_head(x_fin, tgt, final_norm.reshape(1, D), name="loss_head")
    d_up, = _matmul(dx3b, w_dn, mode="nt", name="mm_d_act", out_dtypes=[BF16], tm=tm, tn=tf, tk=D,
                    extras=[u], epilogue=lambda acc, ut: (acc * (2.0 * ut.astype(F32)),))
    tks = _pick(S, 1024)
    dw_dn, = _matmul(u, dx3b, mode="tn", name="mm_dw_down", out_dtypes=[F32], tm=tf, tn=td, tk=tks,
                     a_fn=_square_bf16)
    dh2, = _matmul(d_up, w_up_t, mode="nn", name="mm_dh2", out_dtypes=[F32], tm=tm, tn=td, tk=_pick(DFF, 2048))
    dw_up_t, = _matmul(d_up, h2, mode="tn", name="mm_dw_up", out_dtypes=[F32], tm=tf, tn=td, tk=tks)
    h_s1, tok_s1 = _exchange_start([dw_up_t.reshape(N_DEV, DFF // N_DEV, D), dw_dn.reshape(N_DEV, DFF // N_DEV, D)],
                                   gather=False, name="scatter_mlp_start")
    dx2, dx2b, dg2 = _rms_bwd(dh2, x_mid, mlp_norm, dx3, name="rms2_bwd", want_bf16=True, deps=[tok_s1])

    def gate_bwd_epi(dm, ya_t, yb_t, ga_t, gb_t):
        sa, sb = _sigmoid(ga_t.astype(F32)), _sigmoid(gb_t.astype(F32))
        return (dm * sa, dm * sb, dm * ya_t.astype(F32) * sa * (1.0 - sa), dm * yb_t.astype(F32) * sb * (1.0 - sb))

    gmaps = [lambda i, j, k: (i, j), lambda i, j, k: (i, j), lambda i, j, k: (i, j),
             lambda i, j, k: (i, j + D // td)]
    d_ya, d_yb, d_ga, d_gb = _matmul(dx2b, w_o, mode="nt", name="mm_d_merged", out_dtypes=[BF16] * 4,
                                     tm=tm, tn=td, tk=D, extras=[ya, yb, proj, proj], extra_maps=gmaps,
                                     epilogue=gate_bwd_epi)
    dw_o, = _matmul(merged, dx2b, mode="tn", name="mm_dw_out", out_dtypes=[F32], tm=td, tn=td, tk=tks)
    d_oa, = _matmul(d_ya, w_bs_t, mode="nn", name="mm_d_oa", out_dtypes=[BF16], tm=tm, tn=SWA_Q_W, tk=D)
    d_ob, = _matmul(d_yb, w_bf_t, mode="nn", name="mm_d_ob", out_dtypes=[BF16], tm=tm, tn=FOX_W, tk=D)
    dw_bs_t, = _matmul(d_ya, o_a, mode="tn", name="mm_dw_bs", out_dtypes=[F32], tm=td, tn=SWA_Q_W, tk=tks)
    dw_bf_t, = _matmul(d_yb, o_b, mode="tn", name="mm_dw_bf", out_dtypes=[F32], tm=td, tn=FOX_W, tk=tks)
    h_s2, tok_s2 = _exchange_start([dw_bs_t.reshape(N_DEV, D // N_DEV, SWA_Q_W),
                                    dw_bf_t.reshape(N_DEV, D // N_DEV, FOX_W), dw_o.reshape(N_DEV, D // N_DEV, D)],
                                   gather=False, name="scatter_attn_start")
    d_fq, d_fk, d_fv, dcol4, drow4 = _fox_bwd(proj, negc4, o_b, lse, d_ob, q_off=fq_off, k_off=fk_off,
                                              v_off=fv_off, name="fox_bwd", deps=[tok_s2])
    dcol = dcol4.transpose(0, 2, 1, 3).reshape(FOX_HEADS, S)
    drow = drow4.transpose(0, 2, 1, 3).reshape(FOX_HEADS, S)
    dz_t, dbias_l = _fox_post(drow, dcol, z_t, bias_col, name="fox_post")
    dq_r, dk_c, dk_p, dv_c, dv_p, dsink_l = _swa_bwd(q_rope, k_rope, proj, sinks, d_oa, v_off=v_off, name="swa_bwd")
    d_aq, d_ak, d_av = _rope_bwd(dq_r, dk_c, dk_p, dv_c, dv_p, cos_t, sin_t, name="rope_bwd")
    dz_pad = jnp.pad(dz_t.T.astype(BF16), ((0, 0), (0, FL_PAD - FOX_HEADS)))
    d_proj = jnp.concatenate([d_ga, d_gb, d_aq, d_ak, d_av, d_fq.astype(BF16), d_fk, d_fv, dz_pad], axis=1)
    tkp = _pick(NP, 2304)
    dw_in_p, = _matmul(d_proj, h1, mode="tn", name="mm_dw_in", out_dtypes=[F32], tm=_pick(NP, 512), tn=D, tk=tks)
    dw_in_t = jnp.concatenate([dw_in_p[q_off:q_off + QKV_W], dw_in_p[fl_off:fl_off + FOX_HEADS], dw_in_p[:q_off]],
                              axis=0).astype(BF16)
    h_s3, tok_s3 = _exchange_start([dw_in_t.reshape(N_DEV, d_in // N_DEV, D)], gather=False,
                                   name="scatter_in_start")
    dh1, = _matmul(d_proj, w_in_p, mode="nn", name="mm_dh1", out_dtypes=[F32], tm=tm, tn=td, tk=tkp, deps=[tok_s3])
    dx, dg1 = _rms_bwd(dh1, x2d, attn_norm, dx2, name="rms1_bwd", want_bf16=False)

    dbias = dbias_l[:, 0]
    dsinks = dsink_l[:, :, 0].reshape(-1)
    nsm = 3 * D + 2 * LANES
    tail = jnp.zeros((2 * LANES,), F32)
    small_g = jnp.concatenate([dg1[0], dg2[0], dg3[0],
                               tail.at[0:16].set(dbias).at[16:32].set(dsinks).at[32].set(loss_part[0, 0])])

    def pack(a_norm, b_norm, f_norm, bias, snk):
        return jnp.concatenate([a_norm[0], b_norm[0], f_norm,
                                tail.at[0:16].set(bias[0]).at[16:32].set(snk[0])]).reshape(1, nsm)

    small_stack, = _exchange([small_g.reshape(1, nsm)], gather=True, name="gather_small")
    u_sm = _adamw(pack(attn_norm, mlp_norm, final_norm, fox_f_bias, swa_sinks), small_stack,
                  pack(m_attn_norm, m_mlp_norm, m_final_norm, m_fox_f_bias, m_swa_sinks),
                  pack(v_attn_norm, v_mlp_norm, v_final_norm, v_fox_f_bias, v_swa_sinks),
                  name="adamw_small", stacked=True)
    loss = u_sm[0][0, 3 * D + 32]

    def own_of(src):
        return lax.dynamic_index_in_dim(src, me, 0, keepdims=False)

    def update_t(stack, src, w, m, v, nm):
        g = _sum8(stack, own_of(src), name="sum_" + nm).T
        return _adamw(w[0], g, m[0], v[0], name="adamw_" + nm, stacked=False)

    def update(stack, src, w, m, v, nm):
        return _adamw(w[0], stack, m[0], v[0], name="adamw_" + nm, stacked=True, own=own_of(src))

    (s_up, s_dn), (r_up, r_dn) = _exchange_wait(h_s1, u_sm[1], name="scatter_mlp_wait")
    u_up = update_t(r_up, s_up, w_up, m_w_up, v_w_up, "w_up")
    u_dn = update(r_dn, s_dn, w_down, m_w_down, v_w_down, "w_down")
    (s_bs, s_bf, s_o), (r_bs, r_bf, r_o) = _exchange_wait(h_s2, u_dn[1], name="scatter_attn_wait")
    u_bs = update_t(r_bs, s_bs, w_branch_swa, m_w_branch_swa, v_w_branch_swa, "w_bs")
    u_bf = update_t(r_bf, s_bf, w_branch_fox, m_w_branch_fox, v_w_branch_fox, "w_bf")
    u_o = update(r_o, s_o, w_out, m_w_out, v_w_out, "w_out")
    (s_w_in,), (r_in,) = _exchange_wait(h_s3, u_o[1], name="scatter_in_wait")
    u_in = update_t(r_in, s_w_in, w_in, m_w_in, v_w_in, "w_in")

    def small(kind):
        a = u_sm[kind][0]
        return dict(attn_norm=a[0:D][None], mlp_norm=a[D:2 * D][None], final_norm=a[2 * D:3 * D],
                    fox_f_bias=a[3 * D:3 * D + 16][None], swa_sinks=a[3 * D + 16:3 * D + 32][None])

    big = dict(w_in=u_in, w_branch_swa=u_bs, w_branch_fox=u_bf, w_out=u_o, w_up=u_up, w_down=u_dn)
    order = ["attn_norm", "w_in", "fox_f_bias", "swa_sinks", "w_branch_swa", "w_branch_fox", "w_out", "mlp_norm",
             "w_up", "w_down", "final_norm"]
    outs = [loss, dx[None]]
    for kind in range(4):
        sm = small(kind)
        for nm in order:
            outs.append(big[nm][kind][None] if nm in big else sm[nm])
    return tuple(outs)
```

```python
import functools

import jax
import jax.numpy as jnp
from jax import lax
from jax.experimental import pallas as pl
from jax.experimental.pallas import tpu as pltpu

F32 = jnp.float32
BF16 = jnp.bfloat16

N_DEV = 8
HEAD_DIM = 64
SWA_Q_W = 1024
SWA_KV_W = 128
SWA_GROUP = 8
WINDOW = 128
FOX_W = 1024
FOX_HEADS = 16
QKV_W = SWA_Q_W + 2 * SWA_KV_W + 3 * FOX_W
FL_PAD = 256
ROPE_THETA = 10000.0
RMS_EPS = 1e-6
ATT_SCALE = 0.125
NEG = -1e30

ADAM_LR = 0.001
ADAM_B1 = 0.9
ADAM_B2 = 0.999
ADAM_EPS = 1e-08
ADAM_WD = 0.01
ADAM_STEP = 10

FOX_FWD_BLOCKS = (512, 512)
FOX_BWD_BLOCKS = (512, 512)
FOX_FWD_PAIRS = 2

LANES = 128
VMEM_LIMIT = 56 * 1024 * 1024
STEP_BYTES = 12 * 1024 * 1024


def _cparams(*sem):
    return pltpu.CompilerParams(dimension_semantics=sem, vmem_limit_bytes=VMEM_LIMIT)


def _pick(dim, pref, align=LANES):
    best = None
    t = align
    while t <= min(dim, pref):
        if dim % t == 0:
            best = t
        t += align
    return best if best is not None else dim


_DIMS = {"nn": ((1,), (0,)), "nt": ((1,), (1,)), "tn": ((0,), (0,))}


_ANY = pl.BlockSpec(memory_space=pl.ANY)


def _matmul(a, b, *, mode, name, out_dtypes, tm, tn, tk, extras=(), extra_maps=None,
            a_fn=None, epilogue=None, deps=()):
    if mode == "nn":
        (M, K), (K2, N) = a.shape, b.shape
    elif mode == "nt":
        (M, K), (N, K2) = a.shape, b.shape
    else:
        (K, M), (K2, N) = a.shape, b.shape
    assert K == K2, (name, a.shape, b.shape)
    assert M % tm == 0 and N % tn == 0 and K % tk == 0, (name, M, N, K, tm, tn, tk)
    nk = K // tk
    ne, no = len(extras), len(out_dtypes)
    dims = (_DIMS[mode], ((), ()))

    def body(*refs):
        a_ref, b_ref = refs[0], refs[1]
        ex_refs = refs[2:2 + ne]
        out_refs = refs[2 + ne + len(deps):2 + ne + len(deps) + no]

        def finish(acc):
            res = (acc,) if epilogue is None else epilogue(acc, *[e[...] for e in ex_refs])
            for o_ref, r in zip(out_refs, res):
                o_ref[...] = r.astype(o_ref.dtype)

        def product():
            av = a_ref[...]
            if a_fn is not None:
                av = a_fn(av)
            return lax.dot_general(av, b_ref[...], dims, preferred_element_type=F32)

        if nk == 1:
            finish(product())
        else:
            acc_ref = refs[-1]
            k = pl.program_id(2)

            @pl.when(k == 0)
            def _():
                acc_ref[...] = jnp.zeros_like(acc_ref)

            acc_ref[...] += product()

            @pl.when(k == nk - 1)
            def _():
                finish(acc_ref[...])

    if mode == "tn":
        a_spec = pl.BlockSpec((tk, tm), lambda i, j, k: (k, i))
    else:
        a_spec = pl.BlockSpec((tm, tk), lambda i, j, k: (i, k))
    if mode == "nt":
        b_spec = pl.BlockSpec((tn, tk), lambda i, j, k: (j, k))
    else:
        b_spec = pl.BlockSpec((tk, tn), lambda i, j, k: (k, j))
    if extra_maps is None:
        extra_maps = [lambda i, j, k: (i, j)] * ne
    ex_specs = [pl.BlockSpec((tm, tn), m) for m in extra_maps]
    out_spec = [pl.BlockSpec((tm, tn), lambda i, j, k: (i, j)) for _ in range(no)]
    res = pl.pallas_call(
        body,
        name=name,
        grid=(M // tm, N // tn, nk),
        in_specs=[a_spec, b_spec] + ex_specs + [_ANY] * len(deps),
        out_specs=out_spec,
        out_shape=[jax.ShapeDtypeStruct((M, N), d) for d in out_dtypes],
        scratch_shapes=[pltpu.VMEM((tm, tn), F32)] if nk > 1 else [],
        compiler_params=_cparams("parallel", "parallel", "arbitrary"),
    )(a, b, *extras, *deps)
    return res


def _square_bf16(t):
    tf = t.astype(F32)
    return (tf * tf).astype(BF16)


def _sigmoid(g):
    return 1.0 / (1.0 + jnp.exp(-g))


def _rms_fwd(x, gain, *, name, deps=()):
    S, D = x.shape
    tr = _pick(S, 512, 8)

    def body(x_ref, g_ref, *rest):
        h_ref = rest[-1]
        xv = x_ref[...]
        r = lax.rsqrt(jnp.mean(xv * xv, axis=-1, keepdims=True) + RMS_EPS)
        h_ref[...] = (xv * r * g_ref[...]).astype(BF16)

    return pl.pallas_call(
        body, name=name, grid=(S // tr,),
        in_specs=[pl.BlockSpec((tr, D), lambda i: (i, 0)), pl.BlockSpec((1, D), lambda i: (0, 0))] + [_ANY] * len(deps),
        out_specs=pl.BlockSpec((tr, D), lambda i: (i, 0)),
        out_shape=jax.ShapeDtypeStruct((S, D), BF16),
        compiler_params=_cparams("parallel"),
    )(x, gain, *deps)


def _rms_bwd(dh, x, gain, dres, *, name, want_bf16, deps=()):
    S, D = x.shape
    tr = _pick(S, 256, 8)

    def body(dh_ref, x_ref, g_ref, dres_ref, *rest):
        outs = rest[len(deps):]
        dx_ref, dg_ref = outs[0], outs[-1]
        xv = x_ref[...]
        r = lax.rsqrt(jnp.mean(xv * xv, axis=-1, keepdims=True) + RMS_EPS)
        xh = xv * r
        dhv = dh_ref[...]
        t = dhv * g_ref[...]
        dx = r * (t - xh * jnp.mean(t * xh, axis=-1, keepdims=True)) + dres_ref[...]
        dx_ref[...] = dx
        if want_bf16:
            outs[1][...] = dx.astype(BF16)
        part = jnp.sum(dhv * xh, axis=0, keepdims=True)

        @pl.when(pl.program_id(0) == 0)
        def _():
            dg_ref[...] = part

        @pl.when(pl.program_id(0) > 0)
        def _():
            dg_ref[...] += part

    row = pl.BlockSpec((tr, D), lambda i: (i, 0))
    vec = pl.BlockSpec((1, D), lambda i: (0, 0))
    out_shape = [jax.ShapeDtypeStruct((S, D), F32)]
    out_specs = [row]
    if want_bf16:
        out_shape.append(jax.ShapeDtypeStruct((S, D), BF16))
        out_specs.append(row)
    out_shape.append(jax.ShapeDtypeStruct((1, D), F32))
    out_specs.append(vec)
    return pl.pallas_call(
        body, name=name, grid=(S // tr,),
        in_specs=[row, row, vec, row] + [_ANY] * len(deps), out_specs=out_specs, out_shape=out_shape,
        compiler_params=_cparams("arbitrary"),
    )(dh, x, gain, dres, *deps)


def _loss_head(x3, target, gain, *, name):
    S, D = x3.shape
    tr = _pick(S, 256, 8)

    def body(x_ref, t_ref, g_ref, dx_ref, dxb_ref, dg_ref, loss_ref):
        xv = x_ref[...]
        r = lax.rsqrt(jnp.mean(xv * xv, axis=-1, keepdims=True) + RMS_EPS)
        xh = xv * r
        gv = g_ref[...]
        err = xh * gv - t_ref[...]
        lpart = jnp.zeros((1, LANES), F32) + (0.5 / D) * jnp.sum(err * err)
        dy = err * (1.0 / D)
        t = dy * gv
        dx = r * (t - xh * jnp.mean(t * xh, axis=-1, keepdims=True))
        dx_ref[...] = dx
        dxb_ref[...] = dx.astype(BF16)
        part = jnp.sum(dy * xh, axis=0, keepdims=True)

        @pl.when(pl.program_id(0) == 0)
        def _():
            dg_ref[...] = part
            loss_ref[...] = lpart

        @pl.when(pl.program_id(0) > 0)
        def _():
            dg_ref[...] += part
            loss_ref[...] += lpart

    row = pl.BlockSpec((tr, D), lambda i: (i, 0))
    vec = pl.BlockSpec((1, D), lambda i: (0, 0))
    return pl.pallas_call(
        body, name=name, grid=(S // tr,),
        in_specs=[row, row, vec],
        out_specs=[row, row, vec, pl.BlockSpec((1, LANES), lambda i: (0, 0))],
        out_shape=[jax.ShapeDtypeStruct((S, D), F32), jax.ShapeDtypeStruct((S, D), BF16),
                   jax.ShapeDtypeStruct((1, D), F32), jax.ShapeDtypeStruct((1, LANES), F32)],
        compiler_params=_cparams("arbitrary"),
    )(x3, target, gain)


def _rope_tables(pos_col, invf, *, name):
    S = pos_col.shape[0]
    tr = _pick(S, 512, 8)

    def body(p_ref, f_ref, cos_ref, sin_ref):
        ang = p_ref[...].astype(F32) * f_ref[...]
        lane = lax.broadcasted_iota(jnp.int32, (1, LANES), 1)
        first = (lane % HEAD_DIM) < HEAD_DIM // 2
        sn = jnp.sin(ang)
        cos_ref[...] = jnp.cos(ang)
        sin_ref[...] = jnp.where(first, -sn, sn)

    return pl.pallas_call(
        body, name=name, grid=(S // tr,),
        in_specs=[pl.BlockSpec((tr, 1), lambda i: (i, 0)), pl.BlockSpec((1, LANES), lambda i: (0, 0))],
        out_specs=[pl.BlockSpec((tr, LANES), lambda i: (i, 0))] * 2,
        out_shape=[jax.ShapeDtypeStruct((S, LANES), F32)] * 2,
        compiler_params=_cparams("parallel"),
    )(pos_col, invf)


def _swap_halves(t):
    lane = lax.broadcasted_iota(jnp.int32, (1, LANES), 1)
    first = (lane % HEAD_DIM) < HEAD_DIM // 2
    return jnp.where(first, pltpu.roll(t, LANES - HEAD_DIM // 2, 1), pltpu.roll(t, HEAD_DIM // 2, 1))


def _rope_fwd(proj, cos_t, sin_t, *, q_off, k_off, name):
    S = proj.shape[0]
    tr = _pick(S, 256, 8)
    nqb = SWA_Q_W // LANES

    def body(q_ref, k_ref, c_ref, s_ref, qo_ref, ko_ref):
        cv, sv = c_ref[...], s_ref[...]
        for b in range(nqb):
            t = q_ref[:, b * LANES:(b + 1) * LANES].astype(F32)
            qo_ref[:, b * LANES:(b + 1) * LANES] = (t * cv + _swap_halves(t) * sv).astype(BF16)
        t = k_ref[...].astype(F32)
        ko_ref[...] = (t * cv + _swap_halves(t) * sv).astype(BF16)

    tab = pl.BlockSpec((tr, LANES), lambda i: (i, 0))
    return pl.pallas_call(
        body, name=name, grid=(S // tr,),
        in_specs=[pl.BlockSpec((tr, SWA_Q_W), lambda i: (i, q_off // SWA_Q_W)),
                  pl.BlockSpec((tr, LANES), lambda i: (i, k_off // LANES)), tab, tab],
        out_specs=[pl.BlockSpec((tr, SWA_Q_W), lambda i: (i, 0)), tab],
        out_shape=[jax.ShapeDtypeStruct((S, SWA_Q_W), BF16), jax.ShapeDtypeStruct((S, LANES), BF16)],
        compiler_params=_cparams("parallel"),
    )(proj, proj, cos_t, sin_t)


def _rope_bwd(dq, dk_cur, dk_prev, dv_cur, dv_prev, cos_t, sin_t, *, name):
    S = dq.shape[0]
    tr = WINDOW
    nb = S // tr
    nqb = SWA_Q_W // LANES

    def body(dq_ref, kc_ref, kp_ref, vc_ref, vp_ref, c_ref, s_ref, dqo_ref, dko_ref, dvo_ref):
        cv, sv = c_ref[...], s_ref[...]
        has_next = (pl.program_id(0) + 1 < nb).astype(F32)
        for b in range(nqb):
            d = dq_ref[:, b * LANES:(b + 1) * LANES]
            dqo_ref[:, b * LANES:(b + 1) * LANES] = (d * cv + _swap_halves(d * sv)).astype(BF16)
        d = kc_ref[0] + kc_ref[1] + has_next * (kp_ref[0] + kp_ref[1])
        dko_ref[...] = (d * cv + _swap_halves(d * sv)).astype(BF16)
        dvo_ref[...] = (vc_ref[0] + vc_ref[1] + has_next * (vp_ref[0] + vp_ref[1])).astype(BF16)

    tab = pl.BlockSpec((tr, LANES), lambda i: (i, 0))
    cur = pl.BlockSpec((2, tr, LANES), lambda i: (0, i, 0))
    nxt = pl.BlockSpec((2, tr, LANES), lambda i: (0, jnp.minimum(i + 1, nb - 1), 0))
    return pl.pallas_call(
        body, name=name, grid=(nb,),
        in_specs=[pl.BlockSpec((tr, SWA_Q_W), lambda i: (i, 0)), cur, nxt, cur, nxt, tab, tab],
        out_specs=[pl.BlockSpec((tr, SWA_Q_W), lambda i: (i, 0)), tab, tab],
        out_shape=[jax.ShapeDtypeStruct((S, SWA_Q_W), BF16), jax.ShapeDtypeStruct((S, LANES), BF16),
                   jax.ShapeDtypeStruct((S, LANES), BF16)],
        compiler_params=_cparams("parallel"),
    )(dq, dk_cur, dk_prev, dv_cur, dv_prev, cos_t, sin_t)


def _dot_nt(a, b):
    return lax.dot_general(a, b, (((1,), (1,)), ((), ())), preferred_element_type=F32)


def _dot_tn(a, b):
    return lax.dot_general(a, b, (((0,), (0,)), ((), ())), preferred_element_type=F32)


def _dot_nn(a, b):
    return lax.dot_general(a, b, (((1,), (0,)), ((), ())), preferred_element_type=F32)


def _roll_half(t):
    return pltpu.roll(t.astype(F32), HEAD_DIM, 1).astype(t.dtype)


def _swa_common(hk, n, kp_ref, kc_ref, vp_ref, vc_ref):
    k2 = jnp.concatenate([kp_ref[...], kc_ref[...]], axis=0)
    v2 = jnp.concatenate([vp_ref[...], vc_ref[...]], axis=0)
    k_sw, v_sw = _roll_half(k2), _roll_half(v2)
    row = lax.broadcasted_iota(jnp.int32, (WINDOW, 2 * WINDOW), 0)
    col = lax.broadcasted_iota(jnp.int32, (WINDOW, 2 * WINDOW), 1)
    diff = row + WINDOW - col
    allowed = (diff >= 0) & (diff < WINDOW) & ((col >= WINDOW) | (n > 0))
    lane = lax.broadcasted_iota(jnp.int32, (1, LANES), 1)
    half = [lane < HEAD_DIM, lane >= HEAD_DIM]
    kk = [jnp.where(hk == a, k2, k_sw) for a in range(2)]
    vv = [jnp.where(hk == a, v2, v_sw) for a in range(2)]
    return allowed, half, kk, vv


def _swa_probs(qm, kk, allowed, sink):
    s = jnp.where(allowed, _dot_nt(qm, kk), NEG)
    m = jnp.maximum(jnp.max(s, axis=1, keepdims=True), sink)
    e = jnp.exp(s - m)
    es = jnp.exp(sink - m)
    inv = 1.0 / (jnp.sum(e, axis=1, keepdims=True) + es)
    return e * inv, es * inv


def _swa_fwd(q_rope, k_rope, proj, sinks, *, v_off, name):
    S = q_rope.shape[0]
    nb = S // WINDOW
    gw = SWA_GROUP * HEAD_DIM

    def body(sink_ref, q_ref, kp_ref, kc_ref, vp_ref, vc_ref, o_ref):
        hk, n = pl.program_id(0), pl.program_id(1)
        allowed, half, kk, vv = _swa_common(hk, n, kp_ref, kc_ref, vp_ref, vc_ref)
        for t in range(SWA_GROUP // 2):
            qp = q_ref[:, t * LANES:(t + 1) * LANES] * jnp.asarray(ATT_SCALE, BF16)
            outs = []
            for a in range(2):
                qm = jnp.where(half[a], qp, jnp.zeros_like(qp))
                p, _ = _swa_probs(qm, kk[a], allowed, sink_ref[hk * SWA_GROUP + 2 * t + a])
                outs.append(_dot_nn(p.astype(BF16), vv[a]))
            o_ref[:, t * LANES:(t + 1) * LANES] = jnp.where(half[0], outs[0], outs[1]).astype(BF16)

    prev = lambda hk, n: (jnp.maximum(n - 1, 0), 0)
    cur = lambda hk, n: (n, 0)
    vprev = lambda hk, n: (jnp.maximum(n - 1, 0), v_off // LANES)
    vcur = lambda hk, n: (n, v_off // LANES)
    blk = lambda m: pl.BlockSpec((WINDOW, LANES), m)
    return pl.pallas_call(
        body, name=name, grid=(2, nb),
        in_specs=[pl.BlockSpec(memory_space=pltpu.SMEM),
                  pl.BlockSpec((WINDOW, gw), lambda hk, n: (n, hk)),
                  blk(prev), blk(cur), blk(vprev), blk(vcur)],
        out_specs=pl.BlockSpec((WINDOW, gw), lambda hk, n: (n, hk)),
        out_shape=jax.ShapeDtypeStruct((S, SWA_Q_W), BF16),
        compiler_params=_cparams("parallel", "parallel"),
    )(sinks, q_rope, k_rope, k_rope, proj, proj)


def _swa_bwd(q_rope, k_rope, proj, sinks, d_o, *, v_off, name):
    S = q_rope.shape[0]
    nb = S // WINDOW
    gw = SWA_GROUP * HEAD_DIM

    def body(sink_ref, q_ref, kp_ref, kc_ref, vp_ref, vc_ref, do_ref,
             dq_ref, dkc_ref, dkp_ref, dvc_ref, dvp_ref, dsink_ref):
        hk, n = pl.program_id(0), pl.program_id(1)
        allowed, half, kk, vv = _swa_common(hk, n, kp_ref, kc_ref, vp_ref, vc_ref)
        dk_acc = jnp.zeros((2 * WINDOW, LANES), F32)
        dv_acc = jnp.zeros((2 * WINDOW, LANES), F32)
        srow = lax.broadcasted_iota(jnp.int32, (SWA_GROUP, LANES), 0)
        dsink = jnp.zeros((SWA_GROUP, LANES), F32)
        for t in range(SWA_GROUP // 2):
            qp = q_ref[:, t * LANES:(t + 1) * LANES] * jnp.asarray(ATT_SCALE, BF16)
            dop = do_ref[:, t * LANES:(t + 1) * LANES]
            dqs = []
            for a in range(2):
                g = 2 * t + a
                qm = jnp.where(half[a], qp, jnp.zeros_like(qp))
                dom = jnp.where(half[a], dop, jnp.zeros_like(dop))
                p, psink = _swa_probs(qm, kk[a], allowed, sink_ref[hk * SWA_GROUP + g])
                dp = _dot_nt(dom, vv[a])
                delta = jnp.sum(p * dp, axis=1, keepdims=True)
                ds = (p * (dp - delta)).astype(BF16)
                dsink = dsink + jnp.where(srow == g, -jnp.sum(psink * delta), 0.0)
                dqs.append(_dot_nn(ds, kk[a]) * ATT_SCALE)
                dk_acc = dk_acc + _dot_tn(ds, qm)
                dv_acc = dv_acc + _dot_tn(p.astype(BF16), dom)
            dq_ref[:, t * LANES:(t + 1) * LANES] = jnp.where(half[0], dqs[0], dqs[1])
        lane = lax.broadcasted_iota(jnp.int32, (1, LANES), 1)
        mine = (lane >= HEAD_DIM) == (hk == 1)
        dk_t = jnp.where(mine, dk_acc + pltpu.roll(dk_acc, HEAD_DIM, 1), 0.0)
        dv_t = jnp.where(mine, dv_acc + pltpu.roll(dv_acc, HEAD_DIM, 1), 0.0)
        dkp_ref[0] = dk_t[:WINDOW]
        dkc_ref[0] = dk_t[WINDOW:]
        dvp_ref[0] = dv_t[:WINDOW]
        dvc_ref[0] = dv_t[WINDOW:]

        @pl.when(n == 0)
        def _():
            dsink_ref[0] = dsink

        @pl.when(n > 0)
        def _():
            dsink_ref[0] += dsink

    prev = lambda hk, n: (jnp.maximum(n - 1, 0), 0)
    cur = lambda hk, n: (n, 0)
    vprev = lambda hk, n: (jnp.maximum(n - 1, 0), v_off // LANES)
    vcur = lambda hk, n: (n, v_off // LANES)
    blk = lambda m: pl.BlockSpec((WINDOW, LANES), m)
    qblk = pl.BlockSpec((WINDOW, gw), lambda hk, n: (n, hk))
    part = pl.BlockSpec((1, WINDOW, LANES), lambda hk, n: (hk, n, 0))
    part_shape = jax.ShapeDtypeStruct((2, S, LANES), F32)
    return pl.pallas_call(
        body, name=name, grid=(2, nb),
        in_specs=[pl.BlockSpec(memory_space=pltpu.SMEM), qblk, blk(prev), blk(cur), blk(vprev), blk(vcur), qblk],
        out_specs=[qblk, part, part, part, part,
                   pl.BlockSpec((1, SWA_GROUP, LANES), lambda hk, n: (hk, 0, 0))],
        out_shape=[jax.ShapeDtypeStruct((S, SWA_Q_W), F32), part_shape, part_shape, part_shape, part_shape,
                   jax.ShapeDtypeStruct((2, SWA_GROUP, LANES), F32)],
        compiler_params=_cparams("parallel", "arbitrary"),
    )(sinks, q_rope, k_rope, k_rope, proj, proj, d_o)


def _fox_prep(z_t, bias_col, *, name):
    H, S = z_t.shape
    tb = _pick(S, 512)

    def body(z_ref, b_ref, o_ref, carry_ref):
        @pl.when(pl.program_id(0) == 0)
        def _():
            carry_ref[...] = jnp.zeros_like(carry_ref)

        zz = z_ref[...] + b_ref[...]
        t = jnp.exp(-jnp.abs(zz))
        log1p = jnp.where(t < 1e-2, t * (1.0 - t * (0.5 - t * (1.0 / 3.0))), jnp.log(1.0 + t))
        logf = jnp.minimum(zz, 0.0) - log1p
        r = lax.broadcasted_iota(jnp.int32, (tb, tb), 0)
        c = lax.broadcasted_iota(jnp.int32, (tb, tb), 1)
        tri = (r <= c).astype(BF16)
        hi = logf.astype(BF16)
        r1 = logf - hi.astype(F32)
        mid = r1.astype(BF16)
        lo = (r1 - mid.astype(F32)).astype(BF16)
        cs = _dot_nn(hi, tri) + _dot_nn(mid, tri) + _dot_nn(lo, tri) + carry_ref[:, 0:1]
        o_ref[...] = -cs
        carry_ref[...] = jnp.zeros_like(carry_ref) + cs[:, tb - 1:tb]

    return pl.pallas_call(
        body, name=name, grid=(S // tb,),
        in_specs=[pl.BlockSpec((H, tb), lambda i: (0, i)), pl.BlockSpec((H, 1), lambda i: (0, 0))],
        out_specs=pl.BlockSpec((H, tb), lambda i: (0, i)),
        out_shape=jax.ShapeDtypeStruct((H, S), F32),
        scratch_shapes=[pltpu.VMEM((H, LANES), F32)],
        compiler_params=_cparams("arbitrary"),
    )(z_t, bias_col)


def _fox_post(drow, dcol, z_t, bias_col, *, name):
    H, S = z_t.shape
    tb = _pick(S, 512)
    nb = S // tb

    def body(dr_ref, d_ref, z_ref, b_ref, dz_ref, db_ref, carry_ref):
        @pl.when(pl.program_id(0) == 0)
        def _():
            carry_ref[...] = jnp.zeros_like(carry_ref)
            db_ref[...] = jnp.zeros_like(db_ref)

        dc = dr_ref[...] - d_ref[...]
        r = lax.broadcasted_iota(jnp.int32, (tb, tb), 0)
        c = lax.broadcasted_iota(jnp.int32, (tb, tb), 1)
        tri = (r >= c).astype(BF16)
        hi = dc.astype(BF16)
        r1 = dc - hi.astype(F32)
        mid = r1.astype(BF16)
        lo = (r1 - mid.astype(F32)).astype(BF16)
        dlogf = _dot_nn(hi, tri) + _dot_nn(mid, tri) + _dot_nn(lo, tri) + carry_ref[:, 0:1]
        carry_ref[...] = jnp.zeros_like(carry_ref) + dlogf[:, 0:1]
        dz = dlogf * _sigmoid(-(z_ref[...] + b_ref[...]))
        dz_ref[...] = dz
        db_ref[...] += jnp.sum(dz, axis=1, keepdims=True)

    rev = lambda i: (0, nb - 1 - i)
    return pl.pallas_call(
        body, name=name, grid=(nb,),
        in_specs=[pl.BlockSpec((H, tb), rev), pl.BlockSpec((H, tb), rev), pl.BlockSpec((H, tb), rev),
                  pl.BlockSpec((H, 1), lambda i: (0, 0))],
        out_specs=[pl.BlockSpec((H, tb), rev), pl.BlockSpec((H, LANES), lambda i: (0, 0))],
        out_shape=[jax.ShapeDtypeStruct((H, S), F32), jax.ShapeDtypeStruct((H, LANES), F32)],
        scratch_shapes=[pltpu.VMEM((H, LANES), F32)],
        compiler_params=_cparams("arbitrary"),
    )(drow, dcol, z_t, bias_col)


def _fox_blocks(S):
    cap = max(LANES, S // 4)
    return (min(FOX_FWD_BLOCKS[0], cap), min(FOX_FWD_BLOCKS[1], cap)), \
           (min(FOX_BWD_BLOCKS[0], cap), min(FOX_BWD_BLOCKS[1], cap))


def _key_bias_blocks(negc, bk):
    H, S = negc.shape
    return negc.reshape(H // 2, 2, S // bk, bk).transpose(0, 2, 1, 3)


def _fox_fwd(proj, negc4, *, q_off, k_off, v_off, bq, bk, name):
    S = proj.shape[0]
    nq, nk = S // bq, S // bk
    npair = FOX_HEADS // 2
    assert bq % bk == 0 or bk % bq == 0
    nmask = max(1, bq // bk)

    gp = FOX_FWD_PAIRS
    gw = gp * LANES
    assert q_off % gw == 0 and k_off % gw == 0 and v_off % gw == 0 and npair % gp == 0

    def body(q_ref, k_ref, v_ref, nc_ref, o_ref, lse_ref):
        i = pl.program_id(1)
        lane = lax.broadcasted_iota(jnp.int32, (1, LANES), 1)
        half = [lane < HEAD_DIM, lane >= HEAD_DIM]
        qh = []
        for g in range(gp):
            q2 = q_ref[:, g * LANES:(g + 1) * LANES] * jnp.asarray(ATT_SCALE, BF16)
            qh += [jnp.where(half[h], q2, jnp.zeros_like(q2)) for h in range(2)]
        row = lax.broadcasted_iota(jnp.int32, (bq, bk), 0)
        col = lax.broadcasted_iota(jnp.int32, (bq, bk), 1)
        rel = row - col
        nfull = (i * bq) // bk

        spare = [HEAD_DIM, 0]
        ones_lane = [lane == spare[h] for h in range(2)]

        def step(j, carry, masked):
            start = pl.multiple_of(j * bk, bk)
            new = []
            for g in range(gp):
                ks = k_ref[pl.ds(start, bk), g * LANES:(g + 1) * LANES]
                vs = v_ref[pl.ds(start, bk), g * LANES:(g + 1) * LANES]
                nb = nc_ref[g, j]
                for h in range(2):
                    m, acc = carry[4 * g + 2 * h:4 * g + 2 * h + 2]
                    vh = jnp.where(half[h], vs, jnp.where(ones_lane[h], jnp.ones_like(vs), jnp.zeros_like(vs)))
                    s = _dot_nt(qh[2 * g + h], ks) + nb[h:h + 1, :]
                    if masked:
                        s = jnp.where(rel >= j * bk - i * bq, s, NEG)
                    m_new = jnp.maximum(m, jnp.max(s, axis=1, keepdims=True))
                    p = jnp.exp(s - m_new).astype(BF16)
                    acc = jnp.exp(m - m_new) * acc + _dot_nn(p, vh)
                    new += [m_new, acc]
            return tuple(new)

        init = (jnp.full((bq, 1), NEG, F32), jnp.zeros((bq, LANES), F32)) * (2 * gp)
        carry = lax.fori_loop(0, nfull, lambda j, c: step(j, c, False), init)
        for t in range(nmask):
            carry = step(nfull + t, carry, True)
        for g in range(gp):
            outs, lses = [], []
            for h in range(2):
                m, acc = carry[4 * g + 2 * h:4 * g + 2 * h + 2]
                l = acc[:, spare[h]:spare[h] + 1]
                outs.append(acc * (1.0 / l))
                lses.append(m + jnp.log(l))
            o_ref[:, g * LANES:(g + 1) * LANES] = jnp.where(half[0], outs[0], outs[1]).astype(BF16)
            lse_ref[g] = jnp.where(half[0], lses[0], lses[1])

    seq = lambda off: pl.BlockSpec((S, gw), lambda hp, i: (0, off // gw + hp))
    return pl.pallas_call(
        body, name=name, grid=(npair // gp, nq),
        in_specs=[pl.BlockSpec((bq, gw), lambda hp, i: (i, q_off // gw + hp)), seq(k_off), seq(v_off),
                  pl.BlockSpec((gp, nk, 2, bk), lambda hp, i: (hp, 0, 0, 0))],
        out_specs=[pl.BlockSpec((bq, gw), lambda hp, i: (i, hp)),
                   pl.BlockSpec((gp, bq, LANES), lambda hp, i: (hp, i, 0))],
        out_shape=[jax.ShapeDtypeStruct((S, FOX_W), BF16), jax.ShapeDtypeStruct((npair, S, LANES), F32)],
        compiler_params=_cparams("parallel", "parallel"),
    )(proj, proj, proj, negc4)


def _fox_bwd(proj, negc4, o, lse, d_o, *, q_off, k_off, v_off, bq, bk, name, deps=()):
    S = proj.shape[0]
    nq, nk = S // bq, S // bk
    npair = FOX_HEADS // 2
    assert bq % bk == 0 or bk % bq == 0
    nmask = max(1, bk // bq)

    def body(q_ref, k_ref, v_ref, nc_ref, o_ref, lse_ref, do_ref, *rest):
        dq_ref, dk_ref, dv_ref, dn_ref, dr_ref, delta_ref, rs_ref = rest[len(deps):]
        j = pl.program_id(1)
        lane = lax.broadcasted_iota(jnp.int32, (1, LANES), 1)
        half = [lane < HEAD_DIM, lane >= HEAD_DIM]
        spare = [HEAD_DIM, 0]
        ones_lane = [lane == spare[h] for h in range(2)]
        k2, v2 = k_ref[...], v_ref[...]
        one_k = jnp.ones_like(k2)
        kh = [jnp.where(half[h], k2, jnp.where(ones_lane[h], one_k, jnp.zeros_like(k2))) for h in range(2)]
        nb = nc_ref[0, 0]
        row = lax.broadcasted_iota(jnp.int32, (bq, bk), 0)
        col = lax.broadcasted_iota(jnp.int32, (bq, bk), 1)
        rel = row - col
        i_first = (j * bk) // bq

        @pl.when(j == 0)
        def _():
            dq_ref[...] = jnp.zeros_like(dq_ref)
            rs_ref[...] = jnp.zeros_like(rs_ref)
            for b in range(nq):
                prod = do_ref[b * bq:(b + 1) * bq, :].astype(F32) * o_ref[b * bq:(b + 1) * bq, :].astype(F32)
                d0 = jnp.sum(jnp.where(half[0], prod, 0.0), axis=1, keepdims=True)
                d1 = jnp.sum(jnp.where(half[1], prod, 0.0), axis=1, keepdims=True)
                delta_ref[b * bq:(b + 1) * bq, :] = jnp.where(half[0], d0, d1)

        def step(i, carry, masked):
            dk_a, dk_b, dv_acc = carry
            dks = [dk_a, dk_b]
            start = pl.multiple_of(i * bq, bq)
            q2 = q_ref[pl.ds(start, bq), :] * jnp.asarray(ATT_SCALE, BF16)
            do2 = do_ref[pl.ds(start, bq), :]
            lse2 = lse_ref[0, pl.ds(start, bq), :]
            del2 = delta_ref[pl.ds(start, bq), :]
            dqf = []
            for h in range(2):
                qm = jnp.where(half[h], q2, jnp.zeros_like(q2))
                qm1 = jnp.where(ones_lane[h], jnp.ones_like(q2), qm)
                dom = jnp.where(half[h], do2, jnp.zeros_like(do2))
                c0 = h * HEAD_DIM
                p = jnp.exp(_dot_nt(qm, k2) + nb[h:h + 1, :] - lse2[:, c0:c0 + 1])
                if masked:
                    p = jnp.where(rel >= j * bk - i * bq, p, 0.0)
                dp = _dot_nt(dom, v2)
                dsb = (p * (dp - del2[:, c0:c0 + 1])).astype(BF16)
                dv_acc = dv_acc + _dot_tn(p.astype(BF16), dom)
                dks[h] = dks[h] + _dot_tn(dsb, qm1)
                dqf.append(_dot_nn(dsb, kh[h]))
            dq_ref[pl.ds(start, bq), :] += jnp.where(half[0], dqf[0], dqf[1]) * ATT_SCALE
            rs_ref[pl.ds(start, bq), :] += jnp.where(ones_lane[0], dqf[0], jnp.where(ones_lane[1], dqf[1], 0.0))
            return dks[0], dks[1], dv_acc

        zero = jnp.zeros((bk, LANES), F32)
        carry = (zero, zero, zero)
        for t in range(nmask):
            carry = step(i_first + t, carry, True)
        dk_a, dk_b, dv_acc = lax.fori_loop(i_first + nmask, nq, lambda i, c: step(i, c, False), carry)
        dk_ref[...] = jnp.where(half[0], dk_a, dk_b).astype(BF16)
        dv_ref[...] = dv_acc.astype(BF16)
        dn_ref[0, 0] = jnp.concatenate([dk_a.T[spare[0]:spare[0] + 1], dk_b.T[spare[1]:spare[1] + 1]], axis=0)

        @pl.when(j == nk - 1)
        def _():
            for b in range(nq):
                t = rs_ref[b * bq:(b + 1) * bq, :].T
                dr_ref[0, b] = jnp.concatenate([t[spare[0]:spare[0] + 1], t[spare[1]:spare[1] + 1]], axis=0)

    seq = lambda off: pl.BlockSpec((S, LANES), lambda hp, j: (0, off // LANES + hp))
    blk = lambda off: pl.BlockSpec((bk, LANES), lambda hp, j: (j, off // LANES + hp))
    nc = pl.BlockSpec((1, 1, 2, bk), lambda hp, j: (hp, j, 0, 0))
    return pl.pallas_call(
        body, name=name, grid=(npair, nk),
        in_specs=[seq(q_off), blk(k_off), blk(v_off), nc, seq(0),
                  pl.BlockSpec((1, S, LANES), lambda hp, j: (hp, 0, 0)), seq(0)] + [_ANY] * len(deps),
        out_specs=[seq(0), blk(0), blk(0), nc, pl.BlockSpec((1, nq, 2, bq), lambda hp, j: (hp, 0, 0, 0))],
        out_shape=[jax.ShapeDtypeStruct((S, FOX_W), F32), jax.ShapeDtypeStruct((S, FOX_W), BF16),
                   jax.ShapeDtypeStruct((S, FOX_W), BF16), jax.ShapeDtypeStruct((npair, nk, 2, bk), F32),
                   jax.ShapeDtypeStruct((npair, nq, 2, bq), F32)],
        scratch_shapes=[pltpu.VMEM((S, LANES), F32), pltpu.VMEM((S, LANES), F32)],
        compiler_params=_cparams("parallel", "arbitrary"),
    )(proj, proj, proj, negc4, o, lse, d_o, *deps)


def _exchange(arrs, *, gather, name):
    n = len(arrs)
    npeer = N_DEV - 1

    def body(*refs):
        ins, outs = refs[:n], refs[n:2 * n]
        send_sems, recv_sems, loc_sems = refs[2 * n:]
        x, y, c = lax.axis_index("x"), lax.axis_index("y"), lax.axis_index("c")
        me = 4 * x + 2 * y + c
        peers = []
        for k in range(1, N_DEV):
            px = 1 - x if k & 4 else x
            py = 1 - y if k & 2 else y
            pc = 1 - c if k & 1 else c
            peers.append(((px, py, pc), 4 * px + 2 * py + pc))

        def remote(w, k):
            dev, idx = peers[k]
            src = ins[w] if gather else ins[w].at[idx]
            return pltpu.make_async_remote_copy(
                src_ref=src, dst_ref=outs[w].at[me],
                send_sem=send_sems.at[w * npeer + k], recv_sem=recv_sems.at[w * npeer + k],
                device_id=dev, device_id_type=pl.DeviceIdType.MESH)

        def arrival(w, k):
            dev, idx = peers[k]
            src = ins[w] if gather else ins[w].at[idx]
            return pltpu.make_async_remote_copy(
                src_ref=src, dst_ref=outs[w].at[idx],
                send_sem=send_sems.at[w * npeer + k], recv_sem=recv_sems.at[w * npeer + k],
                device_id=dev, device_id_type=pl.DeviceIdType.MESH)

        local = []
        for w in range(n):
            for k in range(npeer):
                remote(w, k).start()
            cp = pltpu.make_async_copy(ins[w] if gather else ins[w].at[me], outs[w].at[me], loc_sems.at[w])
            cp.start()
            local.append(cp)
        for w in range(n):
            for k in range(npeer):
                arrival(w, k).wait_recv()
        for w in range(n):
            for k in range(npeer):
                remote(w, k).wait_send()
            local[w].wait()

    hbm = pl.BlockSpec(memory_space=pl.ANY)
    out_shape = [jax.ShapeDtypeStruct((N_DEV,) + (a.shape if gather else a.shape[1:]), a.dtype) for a in arrs]
    return pl.pallas_call(
        body, name=name,
        in_specs=[hbm] * n, out_specs=[hbm] * n, out_shape=out_shape,
        scratch_shapes=[pltpu.SemaphoreType.DMA((n * npeer,)), pltpu.SemaphoreType.DMA((n * npeer,)),
                        pltpu.SemaphoreType.DMA((n,))],
        compiler_params=pltpu.CompilerParams(has_side_effects=True),
    )(*arrs)


_HBM = pl.BlockSpec(memory_space=pltpu.HBM)
_SEM = pl.BlockSpec(memory_space=pltpu.SEMAPHORE)
_EFFECT = pltpu.SideEffectType.DATAFLOW_SIDE_EFFECTING
NPEER = N_DEV - 1


def _peer_table():
    x, y, c = lax.axis_index("x"), lax.axis_index("y"), lax.axis_index("c")
    peers = []
    for k in range(1, N_DEV):
        px = 1 - x if k & 4 else x
        py = 1 - y if k & 2 else y
        pc = 1 - c if k & 1 else c
        peers.append(((px, py, pc), 4 * px + 2 * py + pc))
    return 4 * x + 2 * y + c, peers


def _split_copy(ins, lands, send_sems, recv_sems, gather, me, peers, w, k, arriving):
    dev, idx = peers[k]
    return pltpu.make_async_remote_copy(
        src_ref=ins[w] if gather else ins[w].at[idx],
        dst_ref=lands[w].at[idx if arriving else me],
        send_sem=send_sems.at[w * NPEER + k], recv_sem=recv_sems.at[w * NPEER + k],
        device_id=dev, device_id_type=pl.DeviceIdType.MESH)


def _exchange_start(arrs, *, gather, name):
    n = len(arrs)
    land_shapes = [(N_DEV,) + (a.shape if gather else a.shape[1:]) for a in arrs]

    def body(*refs):
        ins, lands = refs[:n], refs[n:2 * n]
        send_sems, recv_sems = refs[2 * n], refs[2 * n + 1]
        token = refs[-1]
        me, peers = _peer_table()
        for w in range(n):
            for k in range(NPEER):
                _split_copy(ins, lands, send_sems, recv_sems, gather, me, peers, w, k, False).start()
        token[...] = jnp.zeros_like(token)

    out_shape = ([pltpu.SemaphoreType.DMA((n * NPEER,)), pltpu.SemaphoreType.DMA((n * NPEER,))]
                 + [pltpu.HBM(a.shape, a.dtype) for a in arrs]
                 + [pltpu.HBM(s, a.dtype) for s, a in zip(land_shapes, arrs)]
                 + [jax.ShapeDtypeStruct((8, LANES), F32)])
    res = pl.pallas_call(
        body, name=name,
        in_specs=[_HBM] * (2 * n),
        out_specs=[_SEM, _SEM] + [_HBM] * (2 * n) + [pl.BlockSpec(memory_space=pltpu.VMEM)],
        out_shape=out_shape,
        input_output_aliases={i: 2 + i for i in range(2 * n)},
        compiler_params=pltpu.CompilerParams(has_side_effects=_EFFECT),
    )(*[pltpu.with_memory_space_constraint(a, pltpu.HBM) for a in arrs],
      *[pltpu.with_memory_space_constraint(lax.empty(s, a.dtype), pltpu.HBM) for s, a in zip(land_shapes, arrs)])
    return (n, gather, res[0], res[1], res[2:2 + n], res[2 + n:2 + 2 * n]), res[-1]


def _exchange_wait(handle, after, *, name):
    n, gather, send_sems, recv_sems, ins_thru, lands_thru = handle

    def body(*refs):
        ins, lands = refs[:n], refs[n:2 * n]
        send_s, recv_s = refs[2 * n], refs[2 * n + 1]
        me, peers = _peer_table()
        for w in range(n):
            for k in range(NPEER):
                _split_copy(ins, lands, send_s, recv_s, gather, me, peers, w, k, False).wait_send()
                _split_copy(ins, lands, send_s, recv_s, gather, me, peers, w, k, True).wait_recv()

    res = pl.pallas_call(
        body, name=name,
        in_specs=[_HBM] * (2 * n) + [_SEM, _SEM, pl.BlockSpec(memory_space=pl.ANY)],
        out_specs=[_HBM] * (2 * n),
        out_shape=[pltpu.HBM(a.shape, a.dtype) for a in list(ins_thru) + list(lands_thru)],
        input_output_aliases={i: i for i in range(2 * n)},
        compiler_params=pltpu.CompilerParams(has_side_effects=_EFFECT),
    )(*ins_thru, *lands_thru, send_sems, recv_sems, after)
    return res[:n], res[n:2 * n]


def _ordered_sum(s_ref, own_ref):
    if own_ref is None:
        blocks = [s_ref[q].astype(F32) for q in range(N_DEV)]
    else:
        me = 4 * lax.axis_index("x") + 2 * lax.axis_index("y") + lax.axis_index("c")
        own = own_ref[...]
        blocks = [jnp.where(me == q, own, s_ref[q]).astype(F32) for q in range(N_DEV)]
    acc = blocks[0]
    for b in blocks[1:]:
        acc = acc + b
    return acc


def _sum8(stack, own, *, name):
    _, R, C = stack.shape
    if R % 8 == 0:
        tr, tc = _pick(R, max(8, STEP_BYTES // (C * 4 * (N_DEV + 2))), 8), C
    else:
        tr, tc = R, _pick(C, max(LANES, STEP_BYTES // (R * 4 * (N_DEV + 2))))

    def body(s_ref, own_ref, o_ref):
        o_ref[...] = _ordered_sum(s_ref, own_ref)

    blk = pl.BlockSpec((tr, tc), lambda i, j: (i, j))
    return pl.pallas_call(
        body, name=name, grid=(R // tr, C // tc),
        in_specs=[pl.BlockSpec((N_DEV, tr, tc), lambda i, j: (0, i, j)), blk],
        out_specs=blk,
        out_shape=jax.ShapeDtypeStruct((R, C), F32),
        compiler_params=_cparams("parallel", "parallel"),
    )(stack, own)


def _adamw_math(w, g, m, v):
    m = ADAM_B1 * m + (1.0 - ADAM_B1) * g
    v = ADAM_B2 * v + (1.0 - ADAM_B2) * (g * g)
    m_hat = m / (1.0 - ADAM_B1 ** ADAM_STEP)
    v_hat = v / (1.0 - ADAM_B2 ** ADAM_STEP)
    delta = -ADAM_LR * (m_hat / (jnp.sqrt(v_hat) + ADAM_EPS) + ADAM_WD * w)
    return delta, m, v


def _adamw(w, g, m, v, *, name, stacked, own=None):
    R, C = w.shape
    tr = _pick(R, max(8, STEP_BYTES // (C * 4 * (8 + (N_DEV if stacked else 1)))), 8)
    has_own = own is not None

    def body(w_ref, g_ref, m_ref, v_ref, *rest):
        go_ref, d_ref, mo_ref, vo_ref = rest[-4:]
        g = _ordered_sum(g_ref, rest[0] if has_own else None) if stacked else g_ref[...]
        delta, m2, v2 = _adamw_math(w_ref[...], g, m_ref[...], v_ref[...])
        go_ref[...] = g
        d_ref[...] = delta
        mo_ref[...] = m2
        vo_ref[...] = v2

    row = pl.BlockSpec((tr, C), lambda i: (i, 0))
    g_spec = pl.BlockSpec((N_DEV, tr, C), lambda i: (0, i, 0)) if stacked else row
    return pl.pallas_call(
        body, name=name, grid=(R // tr,),
        in_specs=[row, g_spec, row, row] + [row] * has_own, out_specs=[row] * 4,
        out_shape=[jax.ShapeDtypeStruct((R, C), F32)] * 4,
        compiler_params=_cparams("parallel"),
    )(w, g, m, v, *([own] if has_own else []))


def kernel(x, positions, attn_norm, w_in, fox_f_bias, swa_sinks, w_branch_swa, w_branch_fox, w_out, mlp_norm, w_up, w_down, final_norm, loss_target, m_attn_norm, m_w_in, m_fox_f_bias, m_swa_sinks, m_w_branch_swa, m_w_branch_fox, m_w_out, m_mlp_norm, m_w_up, m_w_down, m_final_norm, v_attn_norm, v_w_in, v_fox_f_bias, v_swa_sinks, v_w_branch_swa, v_w_branch_fox, v_w_out, v_mlp_norm, v_w_up, v_w_down, v_final_norm):
    S, D = x.shape[1], x.shape[2]
    DFF = w_up.shape[2] * N_DEV
    d_in = w_in.shape[2] * N_DEV
    assert d_in == QKV_W + FOX_HEADS + 2 * D and (2 * D) % SWA_Q_W == 0 and S % (4 * LANES) == 0
    q_off = 2 * D
    k_off = q_off + SWA_Q_W
    v_off = k_off + SWA_KV_W
    fq_off = v_off + SWA_KV_W
    fk_off = fq_off + FOX_W
    fv_off = fk_off + FOX_W
    fl_off = fv_off + FOX_W
    NP = fl_off + FL_PAD
    x2d, tgt = x[0], loss_target[0]

    shards = [w_in[0].T.astype(BF16), w_branch_swa[0].T.astype(BF16), w_branch_fox[0].T.astype(BF16),
              w_out[0].astype(BF16), w_up[0].T.astype(BF16), w_down[0].astype(BF16)]
    me = 4 * lax.axis_index("x") + 2 * lax.axis_index("y") + lax.axis_index("c")

    def filled(stack, own):
        return lax.dynamic_update_slice(stack, own[None], (me,) + (0,) * own.ndim)

    h_in, tok_in = _exchange_start(shards[:1], gather=True, name="gather_w_in_start")
    h_rest, tok_rest = _exchange_start(shards[1:], gather=True, name="gather_rest_start")

    tm = _pick(S, 1024)
    td = _pick(D, 1024)
    tf = _pick(DFF, 1024)
    tnp = _pick(NP, 1024)

    h1 = _rms_fwd(x2d, attn_norm, name="rms1", deps=[tok_in, tok_rest])
    (s_in,), (g_in,) = _exchange_wait(h_in, h1, name="gather_w_in_wait")
    w_in_t = filled(g_in, s_in).reshape(d_in, D)
    w_in_p = jnp.concatenate([w_in_t[QKV_W + FOX_HEADS:], w_in_t[:QKV_W], w_in_t[QKV_W:QKV_W + FOX_HEADS],
                              jnp.zeros((FL_PAD - FOX_HEADS, D), BF16)], axis=0)
    w_fl_t = w_in_t[QKV_W:QKV_W + FOX_HEADS]
    proj, = _matmul(h1, w_in_p, mode="nt", name="mm_in", out_dtypes=[BF16], tm=tm, tn=tnp, tk=D)
    z_t, = _matmul(w_fl_t, h1, mode="nt", name="mm_flogit", out_dtypes=[F32],
                   tm=FOX_HEADS, tn=_pick(S, 2048), tk=D)
    bias_col = fox_f_bias.reshape(FOX_HEADS, 1)
    negc = _fox_prep(z_t, bias_col, name="fox_prep")
    (fbq, fbk), (bbq, bbk) = _fox_blocks(S)
    inv_freq = ROPE_THETA ** (-jnp.arange(0, HEAD_DIM, 2, dtype=F32) / HEAD_DIM)
    invf = jnp.tile(inv_freq, LANES // (HEAD_DIM // 2)).reshape(1, LANES)
    cos_t, sin_t = _rope_tables(positions.reshape(S, 1), invf, name="rope_tables")
    q_rope, k_rope = _rope_fwd(proj, cos_t, sin_t, q_off=q_off, k_off=k_off, name="rope_fwd")
    sinks = swa_sinks.reshape(-1)
    o_a = _swa_fwd(q_rope, k_rope, proj, sinks, v_off=v_off, name="swa_fwd")
    o_b, lse = _fox_fwd(proj, _key_bias_blocks(negc, fbk), q_off=fq_off, k_off=fk_off, v_off=fv_off,
                        bq=fbq, bk=fbk, name="fox_fwd")
    s_rest, g_rest = _exchange_wait(h_rest, o_b, name="gather_rest_wait")
    g_bs, g_bf, g_o, g_up, g_dn = [filled(g, s) for g, s in zip(g_rest, s_rest)]
    w_bs_t = g_bs.reshape(D, SWA_Q_W)
    w_bf_t = g_bf.reshape(D, FOX_W)
    w_o = g_o.reshape(D, D)
    w_up_t = g_up.reshape(DFF, D)
    w_dn = g_dn.reshape(DFF, D)
    ya, = _matmul(o_a, w_bs_t, mode="nt", name="mm_branch_swa", out_dtypes=[BF16], tm=tm, tn=td, tk=SWA_Q_W)
    gate_maps = [lambda i, j, k: (i, j), lambda i, j, k: (i, j), lambda i, j, k: (i, j + D // td)]

    def merge_epi(acc, ya_t, ga_t, gb_t):
        merged = _sigmoid(ga_t.astype(F32)) * ya_t.astype(F32) + _sigmoid(gb_t.astype(F32)) * acc
        return acc, merged

    yb, merged = _matmul(o_b, w_bf_t, mode="nt", name="mm_branch_fox", out_dtypes=[BF16, BF16],
                         tm=tm, tn=td, tk=FOX_W, extras=[ya, proj, proj], extra_maps=gate_maps,
                         epilogue=merge_epi)
    x_mid, = _matmul(merged, w_o, mode="nn", name="mm_out", out_dtypes=[F32], tm=tm, tn=td, tk=D,
                     extras=[x2d], epilogue=lambda acc, r: (acc + r,))
    h2 = _rms_fwd(x_mid, mlp_norm, name="rms2")
    u, = _matmul(h2, w_up_t, mode="nt", name="mm_up", out_dtypes=[BF16], tm=tm, tn=tf, tk=D,
                 epilogue=lambda acc: (jnp.maximum(acc, 0.0),))
    x_fin, = _matmul(u, w_dn, mode="nn", name="mm_down", out_dtypes=[F32], tm=tm, tn=td, tk=_pick(DFF, 2048),
                     a_fn=_square_bf16, extras=[x_mid], epilogue=lambda acc, r: (acc + r,))

    dx3, dx3b, dg3, loss_part = _loss_head(x_fin, tgt, final_norm.reshape(1, D), name="loss_head")
    d_up, = _matmul(dx3b, w_dn, mode="nt", name="mm_d_act", out_dtypes=[BF16], tm=tm, tn=tf, tk=D,
                    extras=[u], epilogue=lambda acc, ut: (acc * (2.0 * ut.astype(F32)),))
    tks = _pick(S, 1024)
    dw_dn, = _matmul(u, dx3b, mode="tn", name="mm_dw_down", out_dtypes=[F32], tm=tf, tn=td, tk=tks,
                     a_fn=_square_bf16)
    dh2, = _matmul(d_up, w_up_t, mode="nn", name="mm_dh2", out_dtypes=[F32], tm=tm, tn=td, tk=_pick(DFF, 2048))
    dw_up_t, = _matmul(d_up, h2, mode="tn", name="mm_dw_up", out_dtypes=[F32], tm=tf, tn=td, tk=tks)
    h_s1, tok_s1 = _exchange_start([dw_up_t.reshape(N_DEV, DFF // N_DEV, D), dw_dn.reshape(N_DEV, DFF // N_DEV, D)],
                                   gather=False, name="scatter_mlp_start")
    dx2, dx2b, dg2 = _rms_bwd(dh2, x_mid, mlp_norm, dx3, name="rms2_bwd", want_bf16=True, deps=[tok_s1])

    def gate_bwd_epi(dm, ya_t, yb_t, ga_t, gb_t):
        sa, sb = _sigmoid(ga_t.astype(F32)), _sigmoid(gb_t.astype(F32))
        return (dm * sa, dm * sb, dm * ya_t.astype(F32) * sa * (1.0 - sa), dm * yb_t.astype(F32) * sb * (1.0 - sb))

    gmaps = [lambda i, j, k: (i, j), lambda i, j, k: (i, j), lambda i, j, k: (i, j),
             lambda i, j, k: (i, j + D // td)]
    d_ya, d_yb, d_ga, d_gb = _matmul(dx2b, w_o, mode="nt", name="mm_d_merged", out_dtypes=[BF16] * 4,
                                     tm=tm, tn=td, tk=D, extras=[ya, yb, proj, proj], extra_maps=gmaps,
                                     epilogue=gate_bwd_epi)
    dw_o, = _matmul(merged, dx2b, mode="tn", name="mm_dw_out", out_dtypes=[F32], tm=td, tn=td, tk=tks)
    d_oa, = _matmul(d_ya, w_bs_t, mode="nn", name="mm_d_oa", out_dtypes=[BF16], tm=tm, tn=SWA_Q_W, tk=D)
    d_ob, = _matmul(d_yb, w_bf_t, mode="nn", name="mm_d_ob", out_dtypes=[BF16], tm=tm, tn=FOX_W, tk=D)
    dw_bs_t, = _matmul(d_ya, o_a, mode="tn", name="mm_dw_bs", out_dtypes=[F32], tm=td, tn=SWA_Q_W, tk=tks)
    dw_bf_t, = _matmul(d_yb, o_b, mode="tn", name="mm_dw_bf", out_dtypes=[F32], tm=td, tn=FOX_W, tk=tks)
    h_s2, tok_s2 = _exchange_start([dw_bs_t.reshape(N_DEV, D // N_DEV, SWA_Q_W),
                                    dw_bf_t.reshape(N_DEV, D // N_DEV, FOX_W), dw_o.reshape(N_DEV, D // N_DEV, D)],
                                   gather=False, name="scatter_attn_start")
    d_fq, d_fk, d_fv, dcol4, drow4 = _fox_bwd(proj, _key_bias_blocks(negc, bbk), o_b, lse, d_ob, q_off=fq_off,
                                              k_off=fk_off, v_off=fv_off, bq=bbq, bk=bbk, name="fox_bwd",
                                              deps=[tok_s2])
    dcol = dcol4.transpose(0, 2, 1, 3).reshape(FOX_HEADS, S)
    drow = drow4.transpose(0, 2, 1, 3).reshape(FOX_HEADS, S)
    dz_t, dbias_l = _fox_post(drow, dcol, z_t, bias_col, name="fox_post")
    dq_r, dk_c, dk_p, dv_c, dv_p, dsink_l = _swa_bwd(q_rope, k_rope, proj, sinks, d_oa, v_off=v_off, name="swa_bwd")
    d_aq, d_ak, d_av = _rope_bwd(dq_r, dk_c, dk_p, dv_c, dv_p, cos_t, sin_t, name="rope_bwd")
    dz_pad = jnp.pad(dz_t.T.astype(BF16), ((0, 0), (0, FL_PAD - FOX_HEADS)))
    d_proj = jnp.concatenate([d_ga, d_gb, d_aq, d_ak, d_av, d_fq.astype(BF16), d_fk, d_fv, dz_pad], axis=1)
    tkp = _pick(NP, 2304)
    dw_in_p, = _matmul(d_proj, h1, mode="tn", name="mm_dw_in", out_dtypes=[F32], tm=_pick(NP, 512), tn=D, tk=tks)
    dw_in_t = jnp.concatenate([dw_in_p[q_off:q_off + QKV_W], dw_in_p[fl_off:fl_off + FOX_HEADS], dw_in_p[:q_off]],
                              axis=0).astype(BF16)
    h_s3, tok_s3 = _exchange_start([dw_in_t.reshape(N_DEV, d_in // N_DEV, D)], gather=False,
                                   name="scatter_in_start")
    dh1, = _matmul(d_proj, w_in_p, mode="nn", name="mm_dh1", out_dtypes=[F32], tm=tm, tn=td, tk=tkp, deps=[tok_s3])
    dx, dg1 = _rms_bwd(dh1, x2d, attn_norm, dx2, name="rms1_bwd", want_bf16=False)

    dbias = dbias_l[:, 0]
    dsinks = dsink_l[:, :, 0].reshape(-1)
    nsm = 3 * D + 2 * LANES
    tail = jnp.zeros((2 * LANES,), F32)
    small_g = jnp.concatenate([dg1[0], dg2[0], dg3[0],
                               tail.at[0:16].set(dbias).at[16:32].set(dsinks).at[32].set(loss_part[0, 0])])

    def pack(a_norm, b_norm, f_norm, bias, snk):
        return jnp.concatenate([a_norm[0], b_norm[0], f_norm,
                                tail.at[0:16].set(bias[0]).at[16:32].set(snk[0])]).reshape(1, nsm)

    small_stack, = _exchange([small_g.reshape(1, nsm)], gather=True, name="gather_small")
    u_sm = _adamw(pack(attn_norm, mlp_norm, final_norm, fox_f_bias, swa_sinks), small_stack,
                  pack(m_attn_norm, m_mlp_norm, m_final_norm, m_fox_f_bias, m_swa_sinks),
                  pack(v_attn_norm, v_mlp_norm, v_final_norm, v_fox_f_bias, v_swa_sinks),
                  name="adamw_small", stacked=True)
    loss = u_sm[0][0, 3 * D + 32]

    def own_of(src):
        return lax.dynamic_index_in_dim(src, me, 0, keepdims=False)

    def update_t(stack, src, w, m, v, nm):
        g = _sum8(stack, own_of(src), name="sum_" + nm).T
        return _adamw(w[0], g, m[0], v[0], name="adamw_" + nm, stacked=False)

    def update(stack, src, w, m, v, nm):
        return _adamw(w[0], stack, m[0], v[0], name="adamw_" + nm, stacked=True, own=own_of(src))

    (s_up, s_dn), (r_up, r_dn) = _exchange_wait(h_s1, u_sm[1], name="scatter_mlp_wait")
    u_up = update_t(r_up, s_up, w_up, m_w_up, v_w_up, "w_up")
    u_dn = update(r_dn, s_dn, w_down, m_w_down, v_w_down, "w_down")
    (s_bs, s_bf, s_o), (r_bs, r_bf, r_o) = _exchange_wait(h_s2, u_dn[1], name="scatter_attn_wait")
    u_bs = update_t(r_bs, s_bs, w_branch_swa, m_w_branch_swa, v_w_branch_swa, "w_bs")
    u_bf = update_t(r_bf, s_bf, w_branch_fox, m_w_branch_fox, v_w_branch_fox, "w_bf")
    u_o = update(r_o, s_o, w_out, m_w_out, v_w_out, "w_out")
    (s_w_in,), (r_in,) = _exchange_wait(h_s3, u_o[1], name="scatter_in_wait")
    u_in = update_t(r_in, s_w_in, w_in, m_w_in, v_w_in, "w_in")

    def small(kind):
        a = u_sm[kind][0]
        return dict(attn_norm=a[0:D][None], mlp_norm=a[D:2 * D][None], final_norm=a[2 * D:3 * D],
                    fox_f_bias=a[3 * D:3 * D + 16][None], swa_sinks=a[3 * D + 16:3 * D + 32][None])

    big = dict(w_in=u_in, w_branch_swa=u_bs, w_branch_fox=u_bf, w_out=u_o, w_up=u_up, w_down=u_dn)
    order = ["attn_norm", "w_in", "fox_f_bias", "swa_sinks", "w_branch_swa", "w_branch_fox", "w_out", "mlp_norm",
             "w_up", "w_down", "final_norm"]
    outs = [loss, dx[None]]
    for kind in range(4):
        sm = small(kind)
        for nm in order:
            outs.append(big[nm][kind][None] if nm in big else sm[nm])
    return tuple(outs)
```

```python
import functools

import jax
import jax.numpy as jnp
from jax import lax
from jax.experimental import pallas as pl
from jax.experimental.pallas import tpu as pltpu

F32 = jnp.float32
BF16 = jnp.bfloat16

N_DEV = 8
HEAD_DIM = 64
SWA_Q_W = 1024
SWA_KV_W = 128
SWA_GROUP = 8
WINDOW = 128
FOX_W = 1024
FOX_HEADS = 16
QKV_W = SWA_Q_W + 2 * SWA_KV_W + 3 * FOX_W
FL_PAD = 256
ROPE_THETA = 10000.0
RMS_EPS = 1e-6
ATT_SCALE = 0.125
NEG = -1e30

ADAM_LR = 0.001
ADAM_B1 = 0.9
ADAM_B2 = 0.999
ADAM_EPS = 1e-08
ADAM_WD = 0.01
ADAM_STEP = 10

FOX_FWD_BLOCKS = (512, 512)
FOX_BWD_BLOCKS = (512, 512)
FOX_FWD_PAIRS = 2

LANES = 128
VMEM_LIMIT = 56 * 1024 * 1024
STEP_BYTES = 12 * 1024 * 1024


def _cparams(*sem):
    return pltpu.CompilerParams(dimension_semantics=sem, vmem_limit_bytes=VMEM_LIMIT)


def _pick(dim, pref, align=LANES):
    best = None
    t = align
    while t <= min(dim, pref):
        if dim % t == 0:
            best = t
        t += align
    return best if best is not None else dim


_DIMS = {"nn": ((1,), (0,)), "nt": ((1,), (1,)), "tn": ((0,), (0,))}


_ANY = pl.BlockSpec(memory_space=pl.ANY)


def _matmul(a, b, *, mode, name, out_dtypes, tm, tn, tk, extras=(), extra_maps=None,
            a_fn=None, epilogue=None, deps=()):
    if mode == "nn":
        (M, K), (K2, N) = a.shape, b.shape
    elif mode == "nt":
        (M, K), (N, K2) = a.shape, b.shape
    else:
        (K, M), (K2, N) = a.shape, b.shape
    assert K == K2, (name, a.shape, b.shape)
    assert M % tm == 0 and N % tn == 0 and K % tk == 0, (name, M, N, K, tm, tn, tk)
    nk = K // tk
    ne, no = len(extras), len(out_dtypes)
    dims = (_DIMS[mode], ((), ()))

    def body(*refs):
        a_ref, b_ref = refs[0], refs[1]
        ex_refs = refs[2:2 + ne]
        out_refs = refs[2 + ne + len(deps):2 + ne + len(deps) + no]

        def finish(acc):
            res = (acc,) if epilogue is None else epilogue(acc, *[e[...] for e in ex_refs])
            for o_ref, r in zip(out_refs, res):
                o_ref[...] = r.astype(o_ref.dtype)

        def product():
            av = a_ref[...]
            if a_fn is not None:
                av = a_fn(av)
            return lax.dot_general(av, b_ref[...], dims, preferred_element_type=F32)

        if nk == 1:
            finish(product())
        else:
            acc_ref = refs[-1]
            k = pl.program_id(2)

            @pl.when(k == 0)
            def _():
                acc_ref[...] = jnp.zeros_like(acc_ref)

            acc_ref[...] += product()

            @pl.when(k == nk - 1)
            def _():
                finish(acc_ref[...])

    if mode == "tn":
        a_spec = pl.BlockSpec((tk, tm), lambda i, j, k: (k, i))
    else:
        a_spec = pl.BlockSpec((tm, tk), lambda i, j, k: (i, k))
    if mode == "nt":
        b_spec = pl.BlockSpec((tn, tk), lambda i, j, k: (j, k))
    else:
        b_spec = pl.BlockSpec((tk, tn), lambda i, j, k: (k, j))
    if extra_maps is None:
        extra_maps = [lambda i, j, k: (i, j)] * ne
    ex_specs = [pl.BlockSpec((tm, tn), m) for m in extra_maps]
    out_spec = [pl.BlockSpec((tm, tn), lambda i, j, k: (i, j)) for _ in range(no)]
    res = pl.pallas_call(
        body,
        name=name,
        grid=(M // tm, N // tn, nk),
        in_specs=[a_spec, b_spec] + ex_specs + [_ANY] * len(deps),
        out_specs=out_spec,
        out_shape=[jax.ShapeDtypeStruct((M, N), d) for d in out_dtypes],
        scratch_shapes=[pltpu.VMEM((tm, tn), F32)] if nk > 1 else [],
        compiler_params=_cparams("parallel", "parallel", "arbitrary"),
    )(a, b, *extras, *deps)
    return res


def _square_bf16(t):
    tf = t.astype(F32)
    return (tf * tf).astype(BF16)


def _sigmoid(g):
    return 1.0 / (1.0 + jnp.exp(-g))


def _rms_fwd(x, gain, *, name, deps=()):
    S, D = x.shape
    tr = _pick(S, 512, 8)

    def body(x_ref, g_ref, *rest):
        h_ref = rest[-1]
        xv = x_ref[...]
        r = lax.rsqrt(jnp.mean(xv * xv, axis=-1, keepdims=True) + RMS_EPS)
        h_ref[...] = (xv * r * g_ref[...]).astype(BF16)

    return pl.pallas_call(
        body, name=name, grid=(S // tr,),
        in_specs=[pl.BlockSpec((tr, D), lambda i: (i, 0)), pl.BlockSpec((1, D), lambda i: (0, 0))] + [_ANY] * len(deps),
        out_specs=pl.BlockSpec((tr, D), lambda i: (i, 0)),
        out_shape=jax.ShapeDtypeStruct((S, D), BF16),
        compiler_params=_cparams("parallel"),
    )(x, gain, *deps)


def _rms_bwd(dh, x, gain, dres, *, name, want_bf16, deps=()):
    S, D = x.shape
    tr = _pick(S, 256, 8)

    def body(dh_ref, x_ref, g_ref, dres_ref, *rest):
        outs = rest[len(deps):]
        dx_ref, dg_ref = outs[0], outs[-1]
        xv = x_ref[...]
        r = lax.rsqrt(jnp.mean(xv * xv, axis=-1, keepdims=True) + RMS_EPS)
        xh = xv * r
        dhv = dh_ref[...]
        t = dhv * g_ref[...]
        dx = r * (t - xh * jnp.mean(t * xh, axis=-1, keepdims=True)) + dres_ref[...]
        dx_ref[...] = dx
        if want_bf16:
            outs[1][...] = dx.astype(BF16)
        part = jnp.sum(dhv * xh, axis=0, keepdims=True)

        @pl.when(pl.program_id(0) == 0)
        def _():
            dg_ref[...] = part

        @pl.when(pl.program_id(0) > 0)
        def _():
            dg_ref[...] += part

    row = pl.BlockSpec((tr, D), lambda i: (i, 0))
    vec = pl.BlockSpec((1, D), lambda i: (0, 0))
    out_shape = [jax.ShapeDtypeStruct((S, D), F32)]
    out_specs = [row]
    if want_bf16:
        out_shape.append(jax.ShapeDtypeStruct((S, D), BF16))
        out_specs.append(row)
    out_shape.append(jax.ShapeDtypeStruct((1, D), F32))
    out_specs.append(vec)
    return pl.pallas_call(
        body, name=name, grid=(S // tr,),
        in_specs=[row, row, vec, row] + [_ANY] * len(deps), out_specs=out_specs, out_shape=out_shape,
        compiler_params=_cparams("arbitrary"),
    )(dh, x, gain, dres, *deps)


def _loss_head(x3, target, gain, *, name):
    S, D = x3.shape
    tr = _pick(S, 256, 8)

    def body(x_ref, t_ref, g_ref, dx_ref, dxb_ref, dg_ref, loss_ref):
        xv = x_ref[...]
        r = lax.rsqrt(jnp.mean(xv * xv, axis=-1, keepdims=True) + RMS_EPS)
        xh = xv * r
        gv = g_ref[...]
        err = xh * gv - t_ref[...]
        lpart = jnp.zeros((1, LANES), F32) + (0.5 / D) * jnp.sum(err * err)
        dy = err * (1.0 / D)
        t = dy * gv
        dx = r * (t - xh * jnp.mean(t * xh, axis=-1, keepdims=True))
        dx_ref[...] = dx
        dxb_ref[...] = dx.astype(BF16)
        part = jnp.sum(dy * xh, axis=0, keepdims=True)

        @pl.when(pl.program_id(0) == 0)
        def _():
            dg_ref[...] = part
            loss_ref[...] = lpart

        @pl.when(pl.program_id(0) > 0)
        def _():
            dg_ref[...] += part
            loss_ref[...] += lpart

    row = pl.BlockSpec((tr, D), lambda i: (i, 0))
    vec = pl.BlockSpec((1, D), lambda i: (0, 0))
    return pl.pallas_call(
        body, name=name, grid=(S // tr,),
        in_specs=[row, row, vec],
        out_specs=[row, row, vec, pl.BlockSpec((1, LANES), lambda i: (0, 0))],
        out_shape=[jax.ShapeDtypeStruct((S, D), F32), jax.ShapeDtypeStruct((S, D), BF16),
                   jax.ShapeDtypeStruct((1, D), F32), jax.ShapeDtypeStruct((1, LANES), F32)],
        compiler_params=_cparams("arbitrary"),
    )(x3, target, gain)


def _rope_tables(pos_col, invf, *, name):
    S = pos_col.shape[0]
    tr = _pick(S, 512, 8)

    def body(p_ref, f_ref, cos_ref, sin_ref):
        ang = p_ref[...].astype(F32) * f_ref[...]
        lane = lax.broadcasted_iota(jnp.int32, (1, LANES), 1)
        first = (lane % HEAD_DIM) < HEAD_DIM // 2
        sn = jnp.sin(ang)
        cos_ref[...] = jnp.cos(ang)
        sin_ref[...] = jnp.where(first, -sn, sn)

    return pl.pallas_call(
        body, name=name, grid=(S // tr,),
        in_specs=[pl.BlockSpec((tr, 1), lambda i: (i, 0)), pl.BlockSpec((1, LANES), lambda i: (0, 0))],
        out_specs=[pl.BlockSpec((tr, LANES), lambda i: (i, 0))] * 2,
        out_shape=[jax.ShapeDtypeStruct((S, LANES), F32)] * 2,
        compiler_params=_cparams("parallel"),
    )(pos_col, invf)


def _swap_halves(t):
    lane = lax.broadcasted_iota(jnp.int32, (1, LANES), 1)
    first = (lane % HEAD_DIM) < HEAD_DIM // 2
    return jnp.where(first, pltpu.roll(t, LANES - HEAD_DIM // 2, 1), pltpu.roll(t, HEAD_DIM // 2, 1))


def _rope_fwd(proj, cos_t, sin_t, *, q_off, k_off, name):
    S = proj.shape[0]
    tr = _pick(S, 256, 8)
    nqb = SWA_Q_W // LANES

    def body(q_ref, k_ref, c_ref, s_ref, qo_ref, ko_ref):
        cv, sv = c_ref[...], s_ref[...]
        for b in range(nqb):
            t = q_ref[:, b * LANES:(b + 1) * LANES].astype(F32)
            qo_ref[:, b * LANES:(b + 1) * LANES] = (t * cv + _swap_halves(t) * sv).astype(BF16)
        t = k_ref[...].astype(F32)
        ko_ref[...] = (t * cv + _swap_halves(t) * sv).astype(BF16)

    tab = pl.BlockSpec((tr, LANES), lambda i: (i, 0))
    return pl.pallas_call(
        body, name=name, grid=(S // tr,),
        in_specs=[pl.BlockSpec((tr, SWA_Q_W), lambda i: (i, q_off // SWA_Q_W)),
                  pl.BlockSpec((tr, LANES), lambda i: (i, k_off // LANES)), tab, tab],
        out_specs=[pl.BlockSpec((tr, SWA_Q_W), lambda i: (i, 0)), tab],
        out_shape=[jax.ShapeDtypeStruct((S, SWA_Q_W), BF16), jax.ShapeDtypeStruct((S, LANES), BF16)],
        compiler_params=_cparams("parallel"),
    )(proj, proj, cos_t, sin_t)


def _rope_bwd(dq, dk_cur, dk_prev, dv_cur, dv_prev, cos_t, sin_t, *, name):
    S = dq.shape[0]
    tr = WINDOW
    nb = S // tr
    nqb = SWA_Q_W // LANES

    def body(dq_ref, kc_ref, kp_ref, vc_ref, vp_ref, c_ref, s_ref, dqo_ref, dko_ref, dvo_ref):
        cv, sv = c_ref[...], s_ref[...]
        has_next = (pl.program_id(0) + 1 < nb).astype(F32)
        for b in range(nqb):
            d = dq_ref[:, b * LANES:(b + 1) * LANES]
            dqo_ref[:, b * LANES:(b + 1) * LANES] = (d * cv + _swap_halves(d * sv)).astype(BF16)
        d = kc_ref[0] + kc_ref[1] + has_next * (kp_ref[0] + kp_ref[1])
        dko_ref[...] = (d * cv + _swap_halves(d * sv)).astype(BF16)
        dvo_ref[...] = (vc_ref[0] + vc_ref[1] + has_next * (vp_ref[0] + vp_ref[1])).astype(BF16)

    tab = pl.BlockSpec((tr, LANES), lambda i: (i, 0))
    cur = pl.BlockSpec((2, tr, LANES), lambda i: (0, i, 0))
    nxt = pl.BlockSpec((2, tr, LANES), lambda i: (0, jnp.minimum(i + 1, nb - 1), 0))
    return pl.pallas_call(
        body, name=name, grid=(nb,),
        in_specs=[pl.BlockSpec((tr, SWA_Q_W), lambda i: (i, 0)), cur, nxt, cur, nxt, tab, tab],
        out_specs=[pl.BlockSpec((tr, SWA_Q_W), lambda i: (i, 0)), tab, tab],
        out_shape=[jax.ShapeDtypeStruct((S, SWA_Q_W), BF16), jax.ShapeDtypeStruct((S, LANES), BF16),
                   jax.ShapeDtypeStruct((S, LANES), BF16)],
        compiler_params=_cparams("parallel"),
    )(dq, dk_cur, dk_prev, dv_cur, dv_prev, cos_t, sin_t)


def _dot_nt(a, b):
    return lax.dot_general(a, b, (((1,), (1,)), ((), ())), preferred_element_type=F32)


def _dot_tn(a, b):
    return lax.dot_general(a, b, (((0,), (0,)), ((), ())), preferred_element_type=F32)


def _dot_nn(a, b):
    return lax.dot_general(a, b, (((1,), (0,)), ((), ())), preferred_element_type=F32)


def _roll_half(t):
    return pltpu.roll(t.astype(F32), HEAD_DIM, 1).astype(t.dtype)


SWA_STACK = SWA_GROUP // 2


def _swa_common(hk, n, kp_ref, kc_ref, vp_ref, vc_ref):
    k2 = jnp.concatenate([kp_ref[...], kc_ref[...]], axis=0)
    v2 = jnp.concatenate([vp_ref[...], vc_ref[...]], axis=0)
    k_sw, v_sw = _roll_half(k2), _roll_half(v2)
    rows = SWA_STACK * WINDOW
    row = lax.broadcasted_iota(jnp.int32, (rows, 2 * WINDOW), 0) % WINDOW
    col = lax.broadcasted_iota(jnp.int32, (rows, 2 * WINDOW), 1)
    diff = row + WINDOW - col
    allowed = (diff >= 0) & (diff < WINDOW) & ((col >= WINDOW) | (n > 0))
    lane = lax.broadcasted_iota(jnp.int32, (1, LANES), 1)
    half = [lane < HEAD_DIM, lane >= HEAD_DIM]
    kk = [jnp.where(hk == a, k2, k_sw) for a in range(2)]
    vv = [jnp.where(hk == a, v2, v_sw) for a in range(2)]
    return allowed, half, kk, vv


def _swa_stack(ref, mask, scale=None):
    parts = []
    for t in range(SWA_STACK):
        blk = ref[:, t * LANES:(t + 1) * LANES]
        if scale is not None:
            blk = blk * jnp.asarray(scale, blk.dtype)
        parts.append(jnp.where(mask, blk, jnp.zeros_like(blk)))
    return jnp.concatenate(parts, axis=0)


def _swa_sink_column(sink_ref, hk, a):
    blk = lax.broadcasted_iota(jnp.int32, (SWA_STACK * WINDOW, 1), 0) // WINDOW
    col = jnp.zeros((SWA_STACK * WINDOW, 1), F32)
    for t in range(SWA_STACK):
        col = jnp.where(blk == t, sink_ref[hk * SWA_GROUP + 2 * t + a], col)
    return col


def _swa_probs(qm, kk, allowed, sink):
    s = jnp.where(allowed, _dot_nt(qm, kk), NEG)
    m = jnp.maximum(jnp.max(s, axis=1, keepdims=True), sink)
    e = jnp.exp(s - m)
    es = jnp.exp(sink - m)
    inv = 1.0 / (jnp.sum(e, axis=1, keepdims=True) + es)
    return e * inv, es * inv


def _swa_fwd(q_rope, k_rope, proj, sinks, *, v_off, name):
    S = q_rope.shape[0]
    nb = S // WINDOW
    gw = SWA_GROUP * HEAD_DIM

    def body(sink_ref, q_ref, kp_ref, kc_ref, vp_ref, vc_ref, o_ref):
        hk, n = pl.program_id(0), pl.program_id(1)
        allowed, half, kk, vv = _swa_common(hk, n, kp_ref, kc_ref, vp_ref, vc_ref)
        outs = []
        for a in range(2):
            qm = _swa_stack(q_ref, half[a], ATT_SCALE)
            p, _ = _swa_probs(qm, kk[a], allowed, _swa_sink_column(sink_ref, hk, a))
            outs.append(_dot_nn(p.astype(BF16), vv[a]))
        for t in range(SWA_STACK):
            rows = slice(t * WINDOW, (t + 1) * WINDOW)
            o_ref[:, t * LANES:(t + 1) * LANES] = jnp.where(half[0], outs[0][rows], outs[1][rows]).astype(BF16)

    prev = lambda hk, n: (jnp.maximum(n - 1, 0), 0)
    cur = lambda hk, n: (n, 0)
    vprev = lambda hk, n: (jnp.maximum(n - 1, 0), v_off // LANES)
    vcur = lambda hk, n: (n, v_off // LANES)
    blk = lambda m: pl.BlockSpec((WINDOW, LANES), m)
    return pl.pallas_call(
        body, name=name, grid=(2, nb),
        in_specs=[pl.BlockSpec(memory_space=pltpu.SMEM),
                  pl.BlockSpec((WINDOW, gw), lambda hk, n: (n, hk)),
                  blk(prev), blk(cur), blk(vprev), blk(vcur)],
        out_specs=pl.BlockSpec((WINDOW, gw), lambda hk, n: (n, hk)),
        out_shape=jax.ShapeDtypeStruct((S, SWA_Q_W), BF16),
        compiler_params=_cparams("parallel", "parallel"),
    )(sinks, q_rope, k_rope, k_rope, proj, proj)


def _swa_bwd(q_rope, k_rope, proj, sinks, d_o, *, v_off, name):
    S = q_rope.shape[0]
    nb = S // WINDOW
    gw = SWA_GROUP * HEAD_DIM

    def body(sink_ref, q_ref, kp_ref, kc_ref, vp_ref, vc_ref, do_ref,
             dq_ref, dkc_ref, dkp_ref, dvc_ref, dvp_ref, dsink_ref):
        hk, n = pl.program_id(0), pl.program_id(1)
        allowed, half, kk, vv = _swa_common(hk, n, kp_ref, kc_ref, vp_ref, vc_ref)
        dk_acc = jnp.zeros((2 * WINDOW, LANES), F32)
        dv_acc = jnp.zeros((2 * WINDOW, LANES), F32)
        srow = lax.broadcasted_iota(jnp.int32, (SWA_GROUP, LANES), 0)
        dsink = jnp.zeros((SWA_GROUP, LANES), F32)
        dqs = []
        for a in range(2):
            qm = _swa_stack(q_ref, half[a], ATT_SCALE)
            dom = _swa_stack(do_ref, half[a])
            p, psink = _swa_probs(qm, kk[a], allowed, _swa_sink_column(sink_ref, hk, a))
            dp = _dot_nt(dom, vv[a])
            delta = jnp.sum(p * dp, axis=1, keepdims=True)
            ds = (p * (dp - delta)).astype(BF16)
            dsk = psink * delta
            for t in range(SWA_STACK):
                dsink = dsink + jnp.where(srow == 2 * t + a, -jnp.sum(dsk[t * WINDOW:(t + 1) * WINDOW]), 0.0)
            dqs.append(_dot_nn(ds, kk[a]) * ATT_SCALE)
            dk_acc = dk_acc + _dot_tn(ds, qm)
            dv_acc = dv_acc + _dot_tn(p.astype(BF16), dom)
        for t in range(SWA_STACK):
            rows = slice(t * WINDOW, (t + 1) * WINDOW)
            dq_ref[:, t * LANES:(t + 1) * LANES] = jnp.where(half[0], dqs[0][rows], dqs[1][rows])
        lane = lax.broadcasted_iota(jnp.int32, (1, LANES), 1)
        mine = (lane >= HEAD_DIM) == (hk == 1)
        dk_t = jnp.where(mine, dk_acc + pltpu.roll(dk_acc, HEAD_DIM, 1), 0.0)
        dv_t = jnp.where(mine, dv_acc + pltpu.roll(dv_acc, HEAD_DIM, 1), 0.0)
        dkp_ref[0] = dk_t[:WINDOW]
        dkc_ref[0] = dk_t[WINDOW:]
        dvp_ref[0] = dv_t[:WINDOW]
        dvc_ref[0] = dv_t[WINDOW:]

        @pl.when(n == 0)
        def _():
            dsink_ref[0] = dsink

        @pl.when(n > 0)
        def _():
            dsink_ref[0] += dsink

    prev = lambda hk, n: (jnp.maximum(n - 1, 0), 0)
    cur = lambda hk, n: (n, 0)
    vprev = lambda hk, n: (jnp.maximum(n - 1, 0), v_off // LANES)
    vcur = lambda hk, n: (n, v_off // LANES)
    blk = lambda m: pl.BlockSpec((WINDOW, LANES), m)
    qblk = pl.BlockSpec((WINDOW, gw), lambda hk, n: (n, hk))
    part = pl.BlockSpec((1, WINDOW, LANES), lambda hk, n: (hk, n, 0))
    part_shape = jax.ShapeDtypeStruct((2, S, LANES), F32)
    return pl.pallas_call(
        body, name=name, grid=(2, nb),
        in_specs=[pl.BlockSpec(memory_space=pltpu.SMEM), qblk, blk(prev), blk(cur), blk(vprev), blk(vcur), qblk],
        out_specs=[qblk, part, part, part, part,
                   pl.BlockSpec((1, SWA_GROUP, LANES), lambda hk, n: (hk, 0, 0))],
        out_shape=[jax.ShapeDtypeStruct((S, SWA_Q_W), F32), part_shape, part_shape, part_shape, part_shape,
                   jax.ShapeDtypeStruct((2, SWA_GROUP, LANES), F32)],
        compiler_params=_cparams("parallel", "arbitrary"),
    )(sinks, q_rope, k_rope, k_rope, proj, proj, d_o)


def _fox_prep(z_t, bias_col, *, name):
    H, S = z_t.shape
    tb = _pick(S, 512)

    def body(z_ref, b_ref, o_ref, carry_ref):
        @pl.when(pl.program_id(0) == 0)
        def _():
            carry_ref[...] = jnp.zeros_like(carry_ref)

        zz = z_ref[...] + b_ref[...]
        t = jnp.exp(-jnp.abs(zz))
        log1p = jnp.where(t < 1e-2, t * (1.0 - t * (0.5 - t * (1.0 / 3.0))), jnp.log(1.0 + t))
        logf = jnp.minimum(zz, 0.0) - log1p
        r = lax.broadcasted_iota(jnp.int32, (tb, tb), 0)
        c = lax.broadcasted_iota(jnp.int32, (tb, tb), 1)
        tri = (r <= c).astype(BF16)
        hi = logf.astype(BF16)
        r1 = logf - hi.astype(F32)
        mid = r1.astype(BF16)
        lo = (r1 - mid.astype(F32)).astype(BF16)
        cs = _dot_nn(hi, tri) + _dot_nn(mid, tri) + _dot_nn(lo, tri) + carry_ref[:, 0:1]
        o_ref[...] = -cs
        carry_ref[...] = jnp.zeros_like(carry_ref) + cs[:, tb - 1:tb]

    return pl.pallas_call(
        body, name=name, grid=(S // tb,),
        in_specs=[pl.BlockSpec((H, tb), lambda i: (0, i)), pl.BlockSpec((H, 1), lambda i: (0, 0))],
        out_specs=pl.BlockSpec((H, tb), lambda i: (0, i)),
        out_shape=jax.ShapeDtypeStruct((H, S), F32),
        scratch_shapes=[pltpu.VMEM((H, LANES), F32)],
        compiler_params=_cparams("arbitrary"),
    )(z_t, bias_col)


def _fox_post(drow, dcol, z_t, bias_col, *, name):
    H, S = z_t.shape
    tb = _pick(S, 512)
    nb = S // tb

    def body(dr_ref, d_ref, z_ref, b_ref, dz_ref, db_ref, carry_ref):
        @pl.when(pl.program_id(0) == 0)
        def _():
            carry_ref[...] = jnp.zeros_like(carry_ref)
            db_ref[...] = jnp.zeros_like(db_ref)

        dc = dr_ref[...] - d_ref[...]
        r = lax.broadcasted_iota(jnp.int32, (tb, tb), 0)
        c = lax.broadcasted_iota(jnp.int32, (tb, tb), 1)
        tri = (r >= c).astype(BF16)
        hi = dc.astype(BF16)
        r1 = dc - hi.astype(F32)
        mid = r1.astype(BF16)
        lo = (r1 - mid.astype(F32)).astype(BF16)
        dlogf = _dot_nn(hi, tri) + _dot_nn(mid, tri) + _dot_nn(lo, tri) + carry_ref[:, 0:1]
        carry_ref[...] = jnp.zeros_like(carry_ref) + dlogf[:, 0:1]
        dz = dlogf * _sigmoid(-(z_ref[...] + b_ref[...]))
        dz_ref[...] = dz
        db_ref[...] += jnp.sum(dz, axis=1, keepdims=True)

    rev = lambda i: (0, nb - 1 - i)
    return pl.pallas_call(
        body, name=name, grid=(nb,),
        in_specs=[pl.BlockSpec((H, tb), rev), pl.BlockSpec((H, tb), rev), pl.BlockSpec((H, tb), rev),
                  pl.BlockSpec((H, 1), lambda i: (0, 0))],
        out_specs=[pl.BlockSpec((H, tb), rev), pl.BlockSpec((H, LANES), lambda i: (0, 0))],
        out_shape=[jax.ShapeDtypeStruct((H, S), F32), jax.ShapeDtypeStruct((H, LANES), F32)],
        scratch_shapes=[pltpu.VMEM((H, LANES), F32)],
        compiler_params=_cparams("arbitrary"),
    )(drow, dcol, z_t, bias_col)


def _fox_blocks(S):
    cap = max(LANES, S // 4)
    return (min(FOX_FWD_BLOCKS[0], cap), min(FOX_FWD_BLOCKS[1], cap)), \
           (min(FOX_BWD_BLOCKS[0], cap), min(FOX_BWD_BLOCKS[1], cap))


def _key_bias_blocks(negc, bk):
    H, S = negc.shape
    return negc.reshape(H // 2, 2, S // bk, bk).transpose(0, 2, 1, 3)


def _fox_fwd(proj, negc4, *, q_off, k_off, v_off, bq, bk, name):
    S = proj.shape[0]
    nq, nk = S // bq, S // bk
    npair = FOX_HEADS // 2
    assert bq % bk == 0 or bk % bq == 0
    nmask = max(1, bq // bk)

    gp = FOX_FWD_PAIRS
    gw = gp * LANES
    assert q_off % gw == 0 and k_off % gw == 0 and v_off % gw == 0 and npair % gp == 0

    def body(q_ref, k_ref, v_ref, nc_ref, o_ref, lse_ref):
        i = pl.program_id(1)
        lane = lax.broadcasted_iota(jnp.int32, (1, LANES), 1)
        half = [lane < HEAD_DIM, lane >= HEAD_DIM]
        qh = []
        for g in range(gp):
            q2 = q_ref[:, g * LANES:(g + 1) * LANES] * jnp.asarray(ATT_SCALE, BF16)
            qh += [jnp.where(half[h], q2, jnp.zeros_like(q2)) for h in range(2)]
        row = lax.broadcasted_iota(jnp.int32, (bq, bk), 0)
        col = lax.broadcasted_iota(jnp.int32, (bq, bk), 1)
        rel = row - col
        nfull = (i * bq) // bk

        spare = [HEAD_DIM, 0]
        ones_lane = [lane == spare[h] for h in range(2)]

        def step(j, carry, masked):
            start = pl.multiple_of(j * bk, bk)
            new = []
            for g in range(gp):
                ks = k_ref[pl.ds(start, bk), g * LANES:(g + 1) * LANES]
                vs = v_ref[pl.ds(start, bk), g * LANES:(g + 1) * LANES]
                nb = nc_ref[g, j]
                for h in range(2):
                    m, acc = carry[4 * g + 2 * h:4 * g + 2 * h + 2]
                    vh = jnp.where(half[h], vs, jnp.where(ones_lane[h], jnp.ones_like(vs), jnp.zeros_like(vs)))
                    s = _dot_nt(qh[2 * g + h], ks) + nb[h:h + 1, :]
                    if masked:
                        s = jnp.where(rel >= j * bk - i * bq, s, NEG)
                    m_new = jnp.maximum(m, jnp.max(s, axis=1, keepdims=True))
                    p = jnp.exp(s - m_new).astype(BF16)
                    acc = jnp.exp(m - m_new) * acc + _dot_nn(p, vh)
                    new += [m_new, acc]
            return tuple(new)

        init = (jnp.full((bq, 1), NEG, F32), jnp.zeros((bq, LANES), F32)) * (2 * gp)
        carry = lax.fori_loop(0, nfull, lambda j, c: step(j, c, False), init)
        for t in range(nmask):
            carry = step(nfull + t, carry, True)
        for g in range(gp):
            outs, lses = [], []
            for h in range(2):
                m, acc = carry[4 * g + 2 * h:4 * g + 2 * h + 2]
                l = acc[:, spare[h]:spare[h] + 1]
                outs.append(acc * (1.0 / l))
                lses.append(m + jnp.log(l))
            o_ref[:, g * LANES:(g + 1) * LANES] = jnp.where(half[0], outs[0], outs[1]).astype(BF16)
            lse_ref[g] = jnp.where(half[0], lses[0], lses[1])

    seq = lambda off: pl.BlockSpec((S, gw), lambda hp, i: (0, off // gw + hp))
    return pl.pallas_call(
        body, name=name, grid=(npair // gp, nq),
        in_specs=[pl.BlockSpec((bq, gw), lambda hp, i: (i, q_off // gw + hp)), seq(k_off), seq(v_off),
                  pl.BlockSpec((gp, nk, 2, bk), lambda hp, i: (hp, 0, 0, 0))],
        out_specs=[pl.BlockSpec((bq, gw), lambda hp, i: (i, hp)),
                   pl.BlockSpec((gp, bq, LANES), lambda hp, i: (hp, i, 0))],
        out_shape=[jax.ShapeDtypeStruct((S, FOX_W), BF16), jax.ShapeDtypeStruct((npair, S, LANES), F32)],
        compiler_params=_cparams("parallel", "parallel"),
    )(proj, proj, proj, negc4)


def _fox_bwd(proj, negc4, o, lse, d_o, *, q_off, k_off, v_off, bq, bk, name, deps=()):
    S = proj.shape[0]
    nq, nk = S // bq, S // bk
    npair = FOX_HEADS // 2
    assert bq % bk == 0 or bk % bq == 0
    nmask = max(1, bk // bq)

    def body(q_ref, k_ref, v_ref, nc_ref, o_ref, lse_ref, do_ref, *rest):
        dq_ref, dk_ref, dv_ref, dn_ref, dr_ref, delta_ref, rs_ref = rest[len(deps):]
        j = pl.program_id(1)
        lane = lax.broadcasted_iota(jnp.int32, (1, LANES), 1)
        half = [lane < HEAD_DIM, lane >= HEAD_DIM]
        spare = [HEAD_DIM, 0]
        ones_lane = [lane == spare[h] for h in range(2)]
        k2, v2 = k_ref[...], v_ref[...]
        one_k = jnp.ones_like(k2)
        kh = [jnp.where(half[h], k2, jnp.where(ones_lane[h], one_k, jnp.zeros_like(k2))) for h in range(2)]
        nb = nc_ref[0, 0]
        row = lax.broadcasted_iota(jnp.int32, (bq, bk), 0)
        col = lax.broadcasted_iota(jnp.int32, (bq, bk), 1)
        rel = row - col
        i_first = (j * bk) // bq

        @pl.when(j == 0)
        def _():
            dq_ref[...] = jnp.zeros_like(dq_ref)
            rs_ref[...] = jnp.zeros_like(rs_ref)
            for b in range(nq):
                prod = do_ref[b * bq:(b + 1) * bq, :].astype(F32) * o_ref[b * bq:(b + 1) * bq, :].astype(F32)
                d0 = jnp.sum(jnp.where(half[0], prod, 0.0), axis=1, keepdims=True)
                d1 = jnp.sum(jnp.where(half[1], prod, 0.0), axis=1, keepdims=True)
                delta_ref[b * bq:(b + 1) * bq, :] = jnp.where(half[0], d0, d1)

        def step(i, carry, masked):
            dk_a, dk_b, dv_acc = carry
            dks = [dk_a, dk_b]
            start = pl.multiple_of(i * bq, bq)
            q2 = q_ref[pl.ds(start, bq), :] * jnp.asarray(ATT_SCALE, BF16)
            do2 = do_ref[pl.ds(start, bq), :]
            lse2 = lse_ref[0, pl.ds(start, bq), :]
            del2 = delta_ref[pl.ds(start, bq), :]
            dqf = []
            for h in range(2):
                qm = jnp.where(half[h], q2, jnp.zeros_like(q2))
                qm1 = jnp.where(ones_lane[h], jnp.ones_like(q2), qm)
                dom = jnp.where(half[h], do2, jnp.zeros_like(do2))
                c0 = h * HEAD_DIM
                p = jnp.exp(_dot_nt(qm, k2) + nb[h:h + 1, :] - lse2[:, c0:c0 + 1])
                if masked:
                    p = jnp.where(rel >= j * bk - i * bq, p, 0.0)
                dp = _dot_nt(dom, v2)
                dsb = (p * (dp - del2[:, c0:c0 + 1])).astype(BF16)
                dv_acc = dv_acc + _dot_tn(p.astype(BF16), dom)
                dks[h] = dks[h] + _dot_tn(dsb, qm1)
                dqf.append(_dot_nn(dsb, kh[h]))
            dq_ref[pl.ds(start, bq), :] += jnp.where(half[0], dqf[0], dqf[1]) * ATT_SCALE
            rs_ref[pl.ds(start, bq), :] += jnp.where(ones_lane[0], dqf[0], jnp.where(ones_lane[1], dqf[1], 0.0))
            return dks[0], dks[1], dv_acc

        zero = jnp.zeros((bk, LANES), F32)
        carry = (zero, zero, zero)
        for t in range(nmask):
            carry = step(i_first + t, carry, True)
        dk_a, dk_b, dv_acc = lax.fori_loop(i_first + nmask, nq, lambda i, c: step(i, c, False), carry)
        dk_ref[...] = jnp.where(half[0], dk_a, dk_b).astype(BF16)
        dv_ref[...] = dv_acc.astype(BF16)
        dn_ref[0, 0] = jnp.concatenate([dk_a.T[spare[0]:spare[0] + 1], dk_b.T[spare[1]:spare[1] + 1]], axis=0)

        @pl.when(j == nk - 1)
        def _():
            for b in range(nq):
                t = rs_ref[b * bq:(b + 1) * bq, :].T
                dr_ref[0, b] = jnp.concatenate([t[spare[0]:spare[0] + 1], t[spare[1]:spare[1] + 1]], axis=0)

    seq = lambda off: pl.BlockSpec((S, LANES), lambda hp, j: (0, off // LANES + hp))
    blk = lambda off: pl.BlockSpec((bk, LANES), lambda hp, j: (j, off // LANES + hp))
    nc = pl.BlockSpec((1, 1, 2, bk), lambda hp, j: (hp, j, 0, 0))
    return pl.pallas_call(
        body, name=name, grid=(npair, nk),
        in_specs=[seq(q_off), blk(k_off), blk(v_off), nc, seq(0),
                  pl.BlockSpec((1, S, LANES), lambda hp, j: (hp, 0, 0)), seq(0)] + [_ANY] * len(deps),
        out_specs=[seq(0), blk(0), blk(0), nc, pl.BlockSpec((1, nq, 2, bq), lambda hp, j: (hp, 0, 0, 0))],
        out_shape=[jax.ShapeDtypeStruct((S, FOX_W), F32), jax.ShapeDtypeStruct((S, FOX_W), BF16),
                   jax.ShapeDtypeStruct((S, FOX_W), BF16), jax.ShapeDtypeStruct((npair, nk, 2, bk), F32),
                   jax.ShapeDtypeStruct((npair, nq, 2, bq), F32)],
        scratch_shapes=[pltpu.VMEM((S, LANES), F32), pltpu.VMEM((S, LANES), F32)],
        compiler_params=_cparams("parallel", "arbitrary"),
    )(proj, proj, proj, negc4, o, lse, d_o, *deps)


def _exchange(arrs, *, gather, name):
    n = len(arrs)
    npeer = N_DEV - 1

    def body(*refs):
        ins, outs = refs[:n], refs[n:2 * n]
        send_sems, recv_sems, loc_sems = refs[2 * n:]
        x, y, c = lax.axis_index("x"), lax.axis_index("y"), lax.axis_index("c")
        me = 4 * x + 2 * y + c
        peers = []
        for k in range(1, N_DEV):
            px = 1 - x if k & 4 else x
            py = 1 - y if k & 2 else y
            pc = 1 - c if k & 1 else c
            peers.append(((px, py, pc), 4 * px + 2 * py + pc))

        def remote(w, k):
            dev, idx = peers[k]
            src = ins[w] if gather else ins[w].at[idx]
            return pltpu.make_async_remote_copy(
                src_ref=src, dst_ref=outs[w].at[me],
                send_sem=send_sems.at[w * npeer + k], recv_sem=recv_sems.at[w * npeer + k],
                device_id=dev, device_id_type=pl.DeviceIdType.MESH)

        def arrival(w, k):
            dev, idx = peers[k]
            src = ins[w] if gather else ins[w].at[idx]
            return pltpu.make_async_remote_copy(
                src_ref=src, dst_ref=outs[w].at[idx],
                send_sem=send_sems.at[w * npeer + k], recv_sem=recv_sems.at[w * npeer + k],
                device_id=dev, device_id_type=pl.DeviceIdType.MESH)

        local = []
        for w in range(n):
            for k in range(npeer):
                remote(w, k).start()
            cp = pltpu.make_async_copy(ins[w] if gather else ins[w].at[me], outs[w].at[me], loc_sems.at[w])
            cp.start()
            local.append(cp)
        for w in range(n):
            for k in range(npeer):
                arrival(w, k).wait_recv()
        for w in range(n):
            for k in range(npeer):
                remote(w, k).wait_send()
            local[w].wait()

    hbm = pl.BlockSpec(memory_space=pl.ANY)
    out_shape = [jax.ShapeDtypeStruct((N_DEV,) + (a.shape if gather else a.shape[1:]), a.dtype) for a in arrs]
    return pl.pallas_call(
        body, name=name,
        in_specs=[hbm] * n, out_specs=[hbm] * n, out_shape=out_shape,
        scratch_shapes=[pltpu.SemaphoreType.DMA((n * npeer,)), pltpu.SemaphoreType.DMA((n * npeer,)),
                        pltpu.SemaphoreType.DMA((n,))],
        compiler_params=pltpu.CompilerParams(has_side_effects=True),
    )(*arrs)


def _gather_two_level(shard, *, name):
    def body(x_ref, out_ref, send_sems, recv_sems, local_sem):
        x, y, c = lax.axis_index("x"), lax.axis_index("y"), lax.axis_index("c")
        me, sibling = (x, y, c), (x, y, 1 - c)
        chips = [(1 - x, y), (x, 1 - y), (1 - x, 1 - y)]

        def slot(px, py, pc):
            return out_ref.at[4 * px + 2 * py + pc]

        def copy(k, block, to, src=None):
            return pltpu.make_async_remote_copy(
                src_ref=slot(*block) if src is None else src, dst_ref=slot(*block),
                send_sem=send_sems.at[k], recv_sem=recv_sems.at[k],
                device_id=to, device_id_type=pl.DeviceIdType.MESH)

        mine = pltpu.make_async_copy(x_ref, slot(*me), local_sem)
        mine.start()
        first = [copy(0, me, sibling, src=x_ref)]
        first += [copy(1 + j, me, (*chip, c), src=x_ref) for j, chip in enumerate(chips)]
        for cp in first:
            cp.start()
        passed = [copy(4 + j, (*chip, c), sibling) for j, chip in enumerate(chips)]
        for j, chip in enumerate(chips):
            copy(1 + j, (*chip, c), me).wait_recv()
            passed[j].start()
        copy(0, sibling, me).wait_recv()
        for j, chip in enumerate(chips):
            copy(4 + j, (*chip, 1 - c), me).wait_recv()
        for cp in first + passed:
            cp.wait_send()
        mine.wait()

    return pl.pallas_call(
        body, name=name,
        in_specs=[_ANY], out_specs=_ANY,
        out_shape=jax.ShapeDtypeStruct((N_DEV,) + shard.shape, shard.dtype),
        scratch_shapes=[pltpu.SemaphoreType.DMA((N_DEV - 1,)), pltpu.SemaphoreType.DMA((N_DEV - 1,)),
                        pltpu.SemaphoreType.DMA],
        compiler_params=pltpu.CompilerParams(has_side_effects=True),
    )(shard)


_HBM = pl.BlockSpec(memory_space=pltpu.HBM)
_SEM = pl.BlockSpec(memory_space=pltpu.SEMAPHORE)
_EFFECT = pltpu.SideEffectType.DATAFLOW_SIDE_EFFECTING
NPEER = N_DEV - 1


def _peer_table():
    x, y, c = lax.axis_index("x"), lax.axis_index("y"), lax.axis_index("c")
    peers = []
    for k in range(1, N_DEV):
        px = 1 - x if k & 4 else x
        py = 1 - y if k & 2 else y
        pc = 1 - c if k & 1 else c
        peers.append(((px, py, pc), 4 * px + 2 * py + pc))
    return 4 * x + 2 * y + c, peers


def _split_copy(ins, lands, send_sems, recv_sems, gather, me, peers, w, k, arriving):
    dev, idx = peers[k]
    return pltpu.make_async_remote_copy(
        src_ref=ins[w] if gather else ins[w].at[idx],
        dst_ref=lands[w].at[idx if arriving else me],
        send_sem=send_sems.at[w * NPEER + k], recv_sem=recv_sems.at[w * NPEER + k],
        device_id=dev, device_id_type=pl.DeviceIdType.MESH)


def _exchange_start(arrs, *, gather, name, deps=()):
    n = len(arrs)
    land_shapes = [(N_DEV,) + (a.shape if gather else a.shape[1:]) for a in arrs]

    def body(*refs):
        ins, lands = refs[:n], refs[n:2 * n]
        send_sems, recv_sems = refs[2 * n + len(deps)], refs[2 * n + len(deps) + 1]
        token = refs[-1]
        me, peers = _peer_table()
        for w in range(n):
            for k in range(NPEER):
                _split_copy(ins, lands, send_sems, recv_sems, gather, me, peers, w, k, False).start()
        token[...] = jnp.zeros_like(token)

    out_shape = ([pltpu.SemaphoreType.DMA((n * NPEER,)), pltpu.SemaphoreType.DMA((n * NPEER,))]
                 + [pltpu.HBM(a.shape, a.dtype) for a in arrs]
                 + [pltpu.HBM(s, a.dtype) for s, a in zip(land_shapes, arrs)]
                 + [jax.ShapeDtypeStruct((8, LANES), F32)])
    res = pl.pallas_call(
        body, name=name,
        in_specs=[_HBM] * (2 * n) + [_ANY] * len(deps),
        out_specs=[_SEM, _SEM] + [_HBM] * (2 * n) + [pl.BlockSpec(memory_space=pltpu.VMEM)],
        out_shape=out_shape,
        input_output_aliases={i: 2 + i for i in range(2 * n)},
        compiler_params=pltpu.CompilerParams(has_side_effects=_EFFECT),
    )(*[pltpu.with_memory_space_constraint(a, pltpu.HBM) for a in arrs],
      *[pltpu.with_memory_space_constraint(lax.empty(s, a.dtype), pltpu.HBM) for s, a in zip(land_shapes, arrs)],
      *deps)
    return (n, gather, res[0], res[1], res[2:2 + n], res[2 + n:2 + 2 * n]), res[-1]


def _exchange_wait(handle, after, *, name):
    n, gather, send_sems, recv_sems, ins_thru, lands_thru = handle

    def body(*refs):
        ins, lands = refs[:n], refs[n:2 * n]
        send_s, recv_s = refs[2 * n], refs[2 * n + 1]
        me, peers = _peer_table()
        for w in range(n):
            for k in range(NPEER):
                _split_copy(ins, lands, send_s, recv_s, gather, me, peers, w, k, False).wait_send()
                _split_copy(ins, lands, send_s, recv_s, gather, me, peers, w, k, True).wait_recv()

    res = pl.pallas_call(
        body, name=name,
        in_specs=[_HBM] * (2 * n) + [_SEM, _SEM, pl.BlockSpec(memory_space=pl.ANY)],
        out_specs=[_HBM] * (2 * n),
        out_shape=[pltpu.HBM(a.shape, a.dtype) for a in list(ins_thru) + list(lands_thru)],
        input_output_aliases={i: i for i in range(2 * n)},
        compiler_params=pltpu.CompilerParams(has_side_effects=_EFFECT),
    )(*ins_thru, *lands_thru, send_sems, recv_sems, after)
    return res[:n], res[n:2 * n]


def _ordered_sum(s_ref, own_ref):
    if own_ref is None:
        blocks = [s_ref[q].astype(F32) for q in range(N_DEV)]
    else:
        me = 4 * lax.axis_index("x") + 2 * lax.axis_index("y") + lax.axis_index("c")
        own = own_ref[...]
        blocks = [jnp.where(me == q, own, s_ref[q]).astype(F32) for q in range(N_DEV)]
    acc = blocks[0]
    for b in blocks[1:]:
        acc = acc + b
    return acc


def _sum8(stack, own, *, name):
    _, R, C = stack.shape
    if R % 8 == 0:
        tr, tc = _pick(R, max(8, STEP_BYTES // (C * 4 * (N_DEV + 2))), 8), C
    else:
        tr, tc = R, _pick(C, max(LANES, STEP_BYTES // (R * 4 * (N_DEV + 2))))

    def body(s_ref, own_ref, o_ref):
        o_ref[...] = _ordered_sum(s_ref, own_ref)

    blk = pl.BlockSpec((tr, tc), lambda i, j: (i, j))
    return pl.pallas_call(
        body, name=name, grid=(R // tr, C // tc),
        in_specs=[pl.BlockSpec((N_DEV, tr, tc), lambda i, j: (0, i, j)), blk],
        out_specs=blk,
        out_shape=jax.ShapeDtypeStruct((R, C), F32),
        compiler_params=_cparams("parallel", "parallel"),
    )(stack, own)


def _adamw_math(w, g, m, v):
    m = ADAM_B1 * m + (1.0 - ADAM_B1) * g
    v = ADAM_B2 * v + (1.0 - ADAM_B2) * (g * g)
    m_hat = m / (1.0 - ADAM_B1 ** ADAM_STEP)
    v_hat = v / (1.0 - ADAM_B2 ** ADAM_STEP)
    delta = -ADAM_LR * (m_hat / (jnp.sqrt(v_hat) + ADAM_EPS) + ADAM_WD * w)
    return delta, m, v


def _adamw(w, g, m, v, *, name, stacked, own=None):
    R, C = w.shape
    tr = _pick(R, max(8, STEP_BYTES // (C * 4 * (8 + (N_DEV if stacked else 1)))), 8)
    has_own = own is not None

    def body(w_ref, g_ref, m_ref, v_ref, *rest):
        go_ref, d_ref, mo_ref, vo_ref = rest[-4:]
        g = _ordered_sum(g_ref, rest[0] if has_own else None) if stacked else g_ref[...]
        delta, m2, v2 = _adamw_math(w_ref[...], g, m_ref[...], v_ref[...])
        go_ref[...] = g
        d_ref[...] = delta
        mo_ref[...] = m2
        vo_ref[...] = v2

    row = pl.BlockSpec((tr, C), lambda i: (i, 0))
    g_spec = pl.BlockSpec((N_DEV, tr, C), lambda i: (0, i, 0)) if stacked else row
    return pl.pallas_call(
        body, name=name, grid=(R // tr,),
        in_specs=[row, g_spec, row, row] + [row] * has_own, out_specs=[row] * 4,
        out_shape=[jax.ShapeDtypeStruct((R, C), F32)] * 4,
        compiler_params=_cparams("parallel"),
    )(w, g, m, v, *([own] if has_own else []))


def kernel(x, positions, attn_norm, w_in, fox_f_bias, swa_sinks, w_branch_swa, w_branch_fox, w_out, mlp_norm, w_up, w_down, final_norm, loss_target, m_attn_norm, m_w_in, m_fox_f_bias, m_swa_sinks, m_w_branch_swa, m_w_branch_fox, m_w_out, m_mlp_norm, m_w_up, m_w_down, m_final_norm, v_attn_norm, v_w_in, v_fox_f_bias, v_swa_sinks, v_w_branch_swa, v_w_branch_fox, v_w_out, v_mlp_norm, v_w_up, v_w_down, v_final_norm):
    S, D = x.shape[1], x.shape[2]
    DFF = w_up.shape[2] * N_DEV
    d_in = w_in.shape[2] * N_DEV
    assert d_in == QKV_W + FOX_HEADS + 2 * D and (2 * D) % SWA_Q_W == 0 and S % (4 * LANES) == 0
    q_off = 2 * D
    k_off = q_off + SWA_Q_W
    v_off = k_off + SWA_KV_W
    fq_off = v_off + SWA_KV_W
    fk_off = fq_off + FOX_W
    fv_off = fk_off + FOX_W
    fl_off = fv_off + FOX_W
    NP = fl_off + FL_PAD
    x2d, tgt = x[0], loss_target[0]

    shards = [w_in[0].T.astype(BF16), w_branch_swa[0].T.astype(BF16), w_branch_fox[0].T.astype(BF16),
              w_out[0].astype(BF16), w_up[0].T.astype(BF16), w_down[0].astype(BF16)]
    me = 4 * lax.axis_index("x") + 2 * lax.axis_index("y") + lax.axis_index("c")

    def filled(stack, own):
        return lax.dynamic_update_slice(stack, own[None], (me,) + (0,) * own.ndim)

    g_in = _gather_two_level(shards[0], name="gather_w_in")
    h_rest, tok_rest = _exchange_start(shards[1:], gather=True, name="gather_rest_start", deps=[g_in])

    tm = _pick(S, 1024)
    td = _pick(D, 1024)
    tf = _pick(DFF, 1024)
    tnp = _pick(NP, 1024)

    h1 = _rms_fwd(x2d, attn_norm, name="rms1", deps=[tok_rest])
    w_in_t = g_in.reshape(d_in, D)
    w_in_p = jnp.concatenate([w_in_t[QKV_W + FOX_HEADS:], w_in_t[:QKV_W], w_in_t[QKV_W:QKV_W + FOX_HEADS],
                              jnp.zeros((FL_PAD - FOX_HEADS, D), BF16)], axis=0)
    w_fl_t = w_in_t[QKV_W:QKV_W + FOX_HEADS]
    proj, = _matmul(h1, w_in_p, mode="nt", name="mm_in", out_dtypes=[BF16], tm=tm, tn=tnp, tk=D)
    z_t, = _matmul(w_fl_t, h1, mode="nt", name="mm_flogit", out_dtypes=[F32],
                   tm=FOX_HEADS, tn=_pick(S, 2048), tk=D)
    bias_col = fox_f_bias.reshape(FOX_HEADS, 1)
    negc = _fox_prep(z_t, bias_col, name="fox_prep")
    (fbq, fbk), (bbq, bbk) = _fox_blocks(S)
    inv_freq = ROPE_THETA ** (-jnp.arange(0, HEAD_DIM, 2, dtype=F32) / HEAD_DIM)
    invf = jnp.tile(inv_freq, LANES // (HEAD_DIM // 2)).reshape(1, LANES)
    cos_t, sin_t = _rope_tables(positions.reshape(S, 1), invf, name="rope_tables")
    q_rope, k_rope = _rope_fwd(proj, cos_t, sin_t, q_off=q_off, k_off=k_off, name="rope_fwd")
    sinks = swa_sinks.reshape(-1)
    o_a = _swa_fwd(q_rope, k_rope, proj, sinks, v_off=v_off, name="swa_fwd")
    o_b, lse = _fox_fwd(proj, _key_bias_blocks(negc, fbk), q_off=fq_off, k_off=fk_off, v_off=fv_off,
                        bq=fbq, bk=fbk, name="fox_fwd")
    s_rest, g_rest = _exchange_wait(h_rest, o_b, name="gather_rest_wait")
    g_bs, g_bf, g_o, g_up, g_dn = [filled(g, s) for g, s in zip(g_rest, s_rest)]
    w_bs_t = g_bs.reshape(D, SWA_Q_W)
    w_bf_t = g_bf.reshape(D, FOX_W)
    w_o = g_o.reshape(D, D)
    w_up_t = g_up.reshape(DFF, D)
    w_dn = g_dn.reshape(DFF, D)
    ya, = _matmul(o_a, w_bs_t, mode="nt", name="mm_branch_swa", out_dtypes=[BF16], tm=tm, tn=td, tk=SWA_Q_W)
    gate_maps = [lambda i, j, k: (i, j), lambda i, j, k: (i, j), lambda i, j, k: (i, j + D // td)]

    def merge_epi(acc, ya_t, ga_t, gb_t):
        merged = _sigmoid(ga_t.astype(F32)) * ya_t.astype(F32) + _sigmoid(gb_t.astype(F32)) * acc
        return acc, merged

    yb, merged = _matmul(o_b, w_bf_t, mode="nt", name="mm_branch_fox", out_dtypes=[BF16, BF16],
                         tm=tm, tn=td, tk=FOX_W, extras=[ya, proj, proj], extra_maps=gate_maps,
                         epilogue=merge_epi)
    x_mid, = _matmul(merged, w_o, mode="nn", name="mm_out", out_dtypes=[F32], tm=tm, tn=td, tk=D,
                     extras=[x2d], epilogue=lambda acc, r: (acc + r,))
    h2 = _rms_fwd(x_mid, mlp_norm, name="rms2")
    u, = _matmul(h2, w_up_t, mode="nt", name="mm_up", out_dtypes=[BF16], tm=tm, tn=tf, tk=D,
                 epilogue=lambda acc: (jnp.maximum(acc, 0.0),))
    x_fin, = _matmul(u, w_dn, mode="nn", name="mm_down", out_dtypes=[F32], tm=tm, tn=td, tk=_pick(DFF, 2048),
                     a_fn=_square_bf16, extras=[x_mid], epilogue=lambda acc, r: (acc + r,))

    dx3, dx3b, dg3, loss_part = _loss_head(x_fin, tgt, final_norm.reshape(1, D), name="loss_head")
    d_up, = _matmul(dx3b, w_dn, mode="nt", name="mm_d_act", out_dtypes=[BF16], tm=tm, tn=tf, tk=D,
                    extras=[u], epilogue=lambda acc, ut: (acc * (2.0 * ut.astype(F32)),))
    tks = _pick(S, 1024)
    dw_dn, = _matmul(u, dx3b, mode="tn", name="mm_dw_down", out_dtypes=[F32], tm=tf, tn=td, tk=tks,
                     a_fn=_square_bf16)
    dh2, = _matmul(d_up, w_up_t, mode="nn", name="mm_dh2", out_dtypes=[F32], tm=tm, tn=td, tk=_pick(DFF, 2048))
    dw_up_t, = _matmul(d_up, h2, mode="tn", name="mm_dw_up", out_dtypes=[F32], tm=tf, tn=td, tk=tks)
    h_s1, tok_s1 = _exchange_start([dw_up_t.reshape(N_DEV, DFF // N_DEV, D), dw_dn.reshape(N_DEV, DFF // N_DEV, D)],
                                   gather=False, name="scatter_mlp_start")
    dx2, dx2b, dg2 = _rms_bwd(dh2, x_mid, mlp_norm, dx3, name="rms2_bwd", want_bf16=True, deps=[tok_s1])

    def gate_bwd_epi(dm, ya_t, yb_t, ga_t, gb_t):
        sa, sb = _sigmoid(ga_t.astype(F32)), _sigmoid(gb_t.astype(F32))
        return (dm * sa, dm * sb, dm * ya_t.astype(F32) * sa * (1.0 - sa), dm * yb_t.astype(F32) * sb * (1.0 - sb))

    gmaps = [lambda i, j, k: (i, j), lambda i, j, k: (i, j), lambda i, j, k: (i, j),
             lambda i, j, k: (i, j + D // td)]
    d_ya, d_yb, d_ga, d_gb = _matmul(dx2b, w_o, mode="nt", name="mm_d_merged", out_dtypes=[BF16] * 4,
                                     tm=tm, tn=td, tk=D, extras=[ya, yb, proj, proj], extra_maps=gmaps,
                                     epilogue=gate_bwd_epi)
    dw_o, = _matmul(merged, dx2b, mode="tn", name="mm_dw_out", out_dtypes=[F32], tm=td, tn=td, tk=tks)
    d_oa, = _matmul(d_ya, w_bs_t, mode="nn", name="mm_d_oa", out_dtypes=[BF16], tm=tm, tn=SWA_Q_W, tk=D)
    d_ob, = _matmul(d_yb, w_bf_t, mode="nn", name="mm_d_ob", out_dtypes=[BF16], tm=tm, tn=FOX_W, tk=D)
    dw_bs_t, = _matmul(d_ya, o_a, mode="tn", name="mm_dw_bs", out_dtypes=[F32], tm=td, tn=SWA_Q_W, tk=tks)
    dw_bf_t, = _matmul(d_yb, o_b, mode="tn", name="mm_dw_bf", out_dtypes=[F32], tm=td, tn=FOX_W, tk=tks)
    h_s2, tok_s2 = _exchange_start([dw_bs_t.reshape(N_DEV, D // N_DEV, SWA_Q_W),
                                    dw_bf_t.reshape(N_DEV, D // N_DEV, FOX_W), dw_o.reshape(N_DEV, D // N_DEV, D)],
                                   gather=False, name="scatter_attn_start")
    d_fq, d_fk, d_fv, dcol4, drow4 = _fox_bwd(proj, _key_bias_blocks(negc, bbk), o_b, lse, d_ob, q_off=fq_off,
                                              k_off=fk_off, v_off=fv_off, bq=bbq, bk=bbk, name="fox_bwd",
                                              deps=[tok_s2])
    dcol = dcol4.transpose(0, 2, 1, 3).reshape(FOX_HEADS, S)
    drow = drow4.transpose(0, 2, 1, 3).reshape(FOX_HEADS, S)
    dz_t, dbias_l = _fox_post(drow, dcol, z_t, bias_col, name="fox_post")
    dq_r, dk_c, dk_p, dv_c, dv_p, dsink_l = _swa_bwd(q_rope, k_rope, proj, sinks, d_oa, v_off=v_off, name="swa_bwd")
    d_aq, d_ak, d_av = _rope_bwd(dq_r, dk_c, dk_p, dv_c, dv_p, cos_t, sin_t, name="rope_bwd")
    dz_pad = jnp.pad(dz_t.T.astype(BF16), ((0, 0), (0, FL_PAD - FOX_HEADS)))
    d_proj = jnp.concatenate([d_ga, d_gb, d_aq, d_ak, d_av, d_fq.astype(BF16), d_fk, d_fv, dz_pad], axis=1)
    tkp = _pick(NP, 2304)
    dw_in_p, = _matmul(d_proj, h1, mode="tn", name="mm_dw_in", out_dtypes=[F32], tm=_pick(NP, 512), tn=D, tk=tks)
    dw_in_t = jnp.concatenate([dw_in_p[q_off:q_off + QKV_W], dw_in_p[fl_off:fl_off + FOX_HEADS], dw_in_p[:q_off]],
                              axis=0).astype(BF16)
    h_s3, tok_s3 = _exchange_start([dw_in_t.reshape(N_DEV, d_in // N_DEV, D)], gather=False,
                                   name="scatter_in_start")
    dh1, = _matmul(d_proj, w_in_p, mode="nn", name="mm_dh1", out_dtypes=[F32], tm=tm, tn=td, tk=tkp, deps=[tok_s3])
    dx, dg1 = _rms_bwd(dh1, x2d, attn_norm, dx2, name="rms1_bwd", want_bf16=False)

    dbias = dbias_l[:, 0]
    dsinks = dsink_l[:, :, 0].reshape(-1)
    nsm = 3 * D + 2 * LANES
    tail = jnp.zeros((2 * LANES,), F32)
    small_g = jnp.concatenate([dg1[0], dg2[0], dg3[0],
                               tail.at[0:16].set(dbias).at[16:32].set(dsinks).at[32].set(loss_part[0, 0])])

    def pack(a_norm, b_norm, f_norm, bias, snk):
        return jnp.concatenate([a_norm[0], b_norm[0], f_norm,
                                tail.at[0:16].set(bias[0]).at[16:32].set(snk[0])]).reshape(1, nsm)

    small_stack, = _exchange([small_g.reshape(1, nsm)], gather=True, name="gather_small")
    u_sm = _adamw(pack(attn_norm, mlp_norm, final_norm, fox_f_bias, swa_sinks), small_stack,
                  pack(m_attn_norm, m_mlp_norm, m_final_norm, m_fox_f_bias, m_swa_sinks),
                  pack(v_attn_norm, v_mlp_norm, v_final_norm, v_fox_f_bias, v_swa_sinks),
                  name="adamw_small", stacked=True)
    loss = u_sm[0][0, 3 * D + 32]

    def own_of(src):
        return lax.dynamic_index_in_dim(src, me, 0, keepdims=False)

    def update_t(stack, src, w, m, v, nm):
        g = _sum8(stack, own_of(src), name="sum_" + nm).T
        return _adamw(w[0], g, m[0], v[0], name="adamw_" + nm, stacked=False)

    def update(stack, src, w, m, v, nm):
        return _adamw(w[0], stack, m[0], v[0], name="adamw_" + nm, stacked=True, own=own_of(src))

    (s_up, s_dn), (r_up, r_dn) = _exchange_wait(h_s1, u_sm[1], name="scatter_mlp_wait")
    u_up = update_t(r_up, s_up, w_up, m_w_up, v_w_up, "w_up")
    u_dn = update(r_dn, s_dn, w_down, m_w_down, v_w_down, "w_down")
    (s_bs, s_bf, s_o), (r_bs, r_bf, r_o) = _exchange_wait(h_s2, u_dn[1], name="scatter_attn_wait")
    u_bs = update_t(r_bs, s_bs, w_branch_swa, m_w_branch_swa, v_w_branch_swa, "w_bs")
    u_bf = update_t(r_bf, s_bf, w_branch_fox, m_w_branch_fox, v_w_branch_fox, "w_bf")
    u_o = update(r_o, s_o, w_out, m_w_out, v_w_out, "w_out")
    (s_w_in,), (r_in,) = _exchange_wait(h_s3, u_o[1], name="scatter_in_wait")
    u_in = update_t(r_in, s_w_in, w_in, m_w_in, v_w_in, "w_in")

    def small(kind):
        a = u_sm[kind][0]
        return dict(attn_norm=a[0:D][None], mlp_norm=a[D:2 * D][None], final_norm=a[2 * D:3 * D],
                    fox_f_bias=a[3 * D:3 * D + 16][None], swa_sinks=a[3 * D + 16:3 * D + 32][None])

    big = dict(w_in=u_in, w_branch_swa=u_bs, w_branch_fox=u_bf, w_out=u_o, w_up=u_up, w_down=u_dn)
    order = ["attn_norm", "w_in", "fox_f_bias", "swa_sinks", "w_branch_swa", "w_branch_fox", "w_out", "mlp_norm",
             "w_up", "w_down", "final_norm"]
    outs = [loss, dx[None]]
    for kind in range(4):
        sm = small(kind)
        for nm in order:
            outs.append(big[nm][kind][None] if nm in big else sm[nm])
    return tuple(outs)
```

```python
import functools

import jax
import jax.numpy as jnp
from jax import lax
from jax.experimental import pallas as pl
from jax.experimental.pallas import tpu as pltpu

F32 = jnp.float32
BF16 = jnp.bfloat16

N_DEV = 8
HEAD_DIM = 64
SWA_Q_W = 1024
SWA_KV_W = 128
SWA_GROUP = 8
WINDOW = 128
FOX_W = 1024
FOX_HEADS = 16
QKV_W = SWA_Q_W + 2 * SWA_KV_W + 3 * FOX_W
FL_PAD = 256
ROPE_THETA = 10000.0
RMS_EPS = 1e-6
ATT_SCALE = 0.125
NEG = -1e30

ADAM_LR = 0.001
ADAM_B1 = 0.9
ADAM_B2 = 0.999
ADAM_EPS = 1e-08
ADAM_WD = 0.01
ADAM_STEP = 10

FOX_FWD_BLOCKS = (512, 512)
FOX_BWD_BLOCKS = (512, 512)
FOX_FWD_PAIRS = 2

LANES = 128
VMEM_LIMIT = 56 * 1024 * 1024
STEP_BYTES = 12 * 1024 * 1024


def _cparams(*sem):
    return pltpu.CompilerParams(dimension_semantics=sem, vmem_limit_bytes=VMEM_LIMIT)


def _pick(dim, pref, align=LANES):
    best = None
    t = align
    while t <= min(dim, pref):
        if dim % t == 0:
            best = t
        t += align
    return best if best is not None else dim


_DIMS = {"nn": ((1,), (0,)), "nt": ((1,), (1,)), "tn": ((0,), (0,))}


_ANY = pl.BlockSpec(memory_space=pl.ANY)


def _matmul(a, b, *, mode, name, out_dtypes, tm, tn, tk, extras=(), extra_maps=None,
            a_fn=None, epilogue=None, deps=()):
    if mode == "nn":
        (M, K), (K2, N) = a.shape, b.shape
    elif mode == "nt":
        (M, K), (N, K2) = a.shape, b.shape
    else:
        (K, M), (K2, N) = a.shape, b.shape
    assert K == K2, (name, a.shape, b.shape)
    assert M % tm == 0 and N % tn == 0 and K % tk == 0, (name, M, N, K, tm, tn, tk)
    nk = K // tk
    ne, no = len(extras), len(out_dtypes)
    dims = (_DIMS[mode], ((), ()))

    def body(*refs):
        a_ref, b_ref = refs[0], refs[1]
        ex_refs = refs[2:2 + ne]
        out_refs = refs[2 + ne + len(deps):2 + ne + len(deps) + no]

        def finish(acc):
            res = (acc,) if epilogue is None else epilogue(acc, *[e[...] for e in ex_refs])
            for o_ref, r in zip(out_refs, res):
                o_ref[...] = r.astype(o_ref.dtype)

        def product():
            av = a_ref[...]
            if a_fn is not None:
                av = a_fn(av)
            return lax.dot_general(av, b_ref[...], dims, preferred_element_type=F32)

        if nk == 1:
            finish(product())
        else:
            acc_ref = refs[-1]
            k = pl.program_id(2)

            @pl.when(k == 0)
            def _():
                acc_ref[...] = jnp.zeros_like(acc_ref)

            acc_ref[...] += product()

            @pl.when(k == nk - 1)
            def _():
                finish(acc_ref[...])

    if mode == "tn":
        a_spec = pl.BlockSpec((tk, tm), lambda i, j, k: (k, i))
    else:
        a_spec = pl.BlockSpec((tm, tk), lambda i, j, k: (i, k))
    if mode == "nt":
        b_spec = pl.BlockSpec((tn, tk), lambda i, j, k: (j, k))
    else:
        b_spec = pl.BlockSpec((tk, tn), lambda i, j, k: (k, j))
    if extra_maps is None:
        extra_maps = [lambda i, j, k: (i, j)] * ne
    ex_specs = [pl.BlockSpec((tm, tn), m) for m in extra_maps]
    out_spec = [pl.BlockSpec((tm, tn), lambda i, j, k: (i, j)) for _ in range(no)]
    res = pl.pallas_call(
        body,
        name=name,
        grid=(M // tm, N // tn, nk),
        in_specs=[a_spec, b_spec] + ex_specs + [_ANY] * len(deps),
        out_specs=out_spec,
        out_shape=[jax.ShapeDtypeStruct((M, N), d) for d in out_dtypes],
        scratch_shapes=[pltpu.VMEM((tm, tn), F32)] if nk > 1 else [],
        compiler_params=_cparams("parallel", "parallel", "arbitrary"),
    )(a, b, *extras, *deps)
    return res


def _square_bf16(t):
    tf = t.astype(F32)
    return (tf * tf).astype(BF16)


def _sigmoid(g):
    return 1.0 / (1.0 + jnp.exp(-g))


def _rms_fwd(x, gain, *, name, deps=()):
    S, D = x.shape
    tr = _pick(S, 512, 8)

    def body(x_ref, g_ref, *rest):
        h_ref = rest[-1]
        xv = x_ref[...]
        r = lax.rsqrt(jnp.mean(xv * xv, axis=-1, keepdims=True) + RMS_EPS)
        h_ref[...] = (xv * r * g_ref[...]).astype(BF16)

    return pl.pallas_call(
        body, name=name, grid=(S // tr,),
        in_specs=[pl.BlockSpec((tr, D), lambda i: (i, 0)), pl.BlockSpec((1, D), lambda i: (0, 0))] + [_ANY] * len(deps),
        out_specs=pl.BlockSpec((tr, D), lambda i: (i, 0)),
        out_shape=jax.ShapeDtypeStruct((S, D), BF16),
        compiler_params=_cparams("parallel"),
    )(x, gain, *deps)


def _rms_bwd(dh, x, gain, dres, *, name, want_bf16, deps=()):
    S, D = x.shape
    tr = _pick(S, 256, 8)

    def body(dh_ref, x_ref, g_ref, dres_ref, *rest):
        outs = rest[len(deps):]
        dx_ref, dg_ref = outs[0], outs[-1]
        xv = x_ref[...]
        r = lax.rsqrt(jnp.mean(xv * xv, axis=-1, keepdims=True) + RMS_EPS)
        xh = xv * r
        dhv = dh_ref[...]
        t = dhv * g_ref[...]
        dx = r * (t - xh * jnp.mean(t * xh, axis=-1, keepdims=True)) + dres_ref[...]
        dx_ref[...] = dx
        if want_bf16:
            outs[1][...] = dx.astype(BF16)
        part = jnp.sum(dhv * xh, axis=0, keepdims=True)

        @pl.when(pl.program_id(0) == 0)
        def _():
            dg_ref[...] = part

        @pl.when(pl.program_id(0) > 0)
        def _():
            dg_ref[...] += part

    row = pl.BlockSpec((tr, D), lambda i: (i, 0))
    vec = pl.BlockSpec((1, D), lambda i: (0, 0))
    out_shape = [jax.ShapeDtypeStruct((S, D), F32)]
    out_specs = [row]
    if want_bf16:
        out_shape.append(jax.ShapeDtypeStruct((S, D), BF16))
        out_specs.append(row)
    out_shape.append(jax.ShapeDtypeStruct((1, D), F32))
    out_specs.append(vec)
    return pl.pallas_call(
        body, name=name, grid=(S // tr,),
        in_specs=[row, row, vec, row] + [_ANY] * len(deps), out_specs=out_specs, out_shape=out_shape,
        compiler_params=_cparams("arbitrary"),
    )(dh, x, gain, dres, *deps)


def _loss_head(x3, target, gain, *, name):
    S, D = x3.shape
    tr = _pick(S, 256, 8)

    def body(x_ref, t_ref, g_ref, dx_ref, dxb_ref, dg_ref, loss_ref):
        xv = x_ref[...]
        r = lax.rsqrt(jnp.mean(xv * xv, axis=-1, keepdims=True) + RMS_EPS)
        xh = xv * r
        gv = g_ref[...]
        err = xh * gv - t_ref[...]
        lpart = jnp.zeros((1, LANES), F32) + (0.5 / D) * jnp.sum(err * err)
        dy = err * (1.0 / D)
        t = dy * gv
        dx = r * (t - xh * jnp.mean(t * xh, axis=-1, keepdims=True))
        dx_ref[...] = dx
        dxb_ref[...] = dx.astype(BF16)
        part = jnp.sum(dy * xh, axis=0, keepdims=True)

        @pl.when(pl.program_id(0) == 0)
        def _():
            dg_ref[...] = part
            loss_ref[...] = lpart

        @pl.when(pl.program_id(0) > 0)
        def _():
            dg_ref[...] += part
            loss_ref[...] += lpart

    row = pl.BlockSpec((tr, D), lambda i: (i, 0))
    vec = pl.BlockSpec((1, D), lambda i: (0, 0))
    return pl.pallas_call(
        body, name=name, grid=(S // tr,),
        in_specs=[row, row, vec],
        out_specs=[row, row, vec, pl.BlockSpec((1, LANES), lambda i: (0, 0))],
        out_shape=[jax.ShapeDtypeStruct((S, D), F32), jax.ShapeDtypeStruct((S, D), BF16),
                   jax.ShapeDtypeStruct((1, D), F32), jax.ShapeDtypeStruct((1, LANES), F32)],
        compiler_params=_cparams("arbitrary"),
    )(x3, target, gain)


def _rope_tables(pos_col, invf, *, name):
    S = pos_col.shape[0]
    tr = _pick(S, 512, 8)

    def body(p_ref, f_ref, cos_ref, sin_ref):
        ang = p_ref[...].astype(F32) * f_ref[...]
        lane = lax.broadcasted_iota(jnp.int32, (1, LANES), 1)
        first = (lane % HEAD_DIM) < HEAD_DIM // 2
        sn = jnp.sin(ang)
        cos_ref[...] = jnp.cos(ang)
        sin_ref[...] = jnp.where(first, -sn, sn)

    return pl.pallas_call(
        body, name=name, grid=(S // tr,),
        in_specs=[pl.BlockSpec((tr, 1), lambda i: (i, 0)), pl.BlockSpec((1, LANES), lambda i: (0, 0))],
        out_specs=[pl.BlockSpec((tr, LANES), lambda i: (i, 0))] * 2,
        out_shape=[jax.ShapeDtypeStruct((S, LANES), F32)] * 2,
        compiler_params=_cparams("parallel"),
    )(pos_col, invf)


def _swap_halves(t):
    lane = lax.broadcasted_iota(jnp.int32, (1, LANES), 1)
    first = (lane % HEAD_DIM) < HEAD_DIM // 2
    return jnp.where(first, pltpu.roll(t, LANES - HEAD_DIM // 2, 1), pltpu.roll(t, HEAD_DIM // 2, 1))


def _rope_fwd(proj, cos_t, sin_t, *, q_off, k_off, name):
    S = proj.shape[0]
    tr = _pick(S, 256, 8)
    nqb = SWA_Q_W // LANES

    def body(q_ref, k_ref, c_ref, s_ref, qo_ref, ko_ref):
        cv, sv = c_ref[...], s_ref[...]
        for b in range(nqb):
            t = q_ref[:, b * LANES:(b + 1) * LANES].astype(F32)
            qo_ref[:, b * LANES:(b + 1) * LANES] = (t * cv + _swap_halves(t) * sv).astype(BF16)
        t = k_ref[...].astype(F32)
        ko_ref[...] = (t * cv + _swap_halves(t) * sv).astype(BF16)

    tab = pl.BlockSpec((tr, LANES), lambda i: (i, 0))
    return pl.pallas_call(
        body, name=name, grid=(S // tr,),
        in_specs=[pl.BlockSpec((tr, SWA_Q_W), lambda i: (i, q_off // SWA_Q_W)),
                  pl.BlockSpec((tr, LANES), lambda i: (i, k_off // LANES)), tab, tab],
        out_specs=[pl.BlockSpec((tr, SWA_Q_W), lambda i: (i, 0)), tab],
        out_shape=[jax.ShapeDtypeStruct((S, SWA_Q_W), BF16), jax.ShapeDtypeStruct((S, LANES), BF16)],
        compiler_params=_cparams("parallel"),
    )(proj, proj, cos_t, sin_t)


def _rope_bwd(dq, dk_cur, dk_prev, dv_cur, dv_prev, cos_t, sin_t, *, name):
    S = dq.shape[0]
    tr = WINDOW
    nb = S // tr
    nqb = SWA_Q_W // LANES

    def body(dq_ref, kc_ref, kp_ref, vc_ref, vp_ref, c_ref, s_ref, dqo_ref, dko_ref, dvo_ref):
        cv, sv = c_ref[...], s_ref[...]
        has_next = (pl.program_id(0) + 1 < nb).astype(F32)
        for b in range(nqb):
            d = dq_ref[:, b * LANES:(b + 1) * LANES]
            dqo_ref[:, b * LANES:(b + 1) * LANES] = (d * cv + _swap_halves(d * sv)).astype(BF16)
        d = kc_ref[0] + kc_ref[1] + has_next * (kp_ref[0] + kp_ref[1])
        dko_ref[...] = (d * cv + _swap_halves(d * sv)).astype(BF16)
        dvo_ref[...] = (vc_ref[0] + vc_ref[1] + has_next * (vp_ref[0] + vp_ref[1])).astype(BF16)

    tab = pl.BlockSpec((tr, LANES), lambda i: (i, 0))
    cur = pl.BlockSpec((2, tr, LANES), lambda i: (0, i, 0))
    nxt = pl.BlockSpec((2, tr, LANES), lambda i: (0, jnp.minimum(i + 1, nb - 1), 0))
    return pl.pallas_call(
        body, name=name, grid=(nb,),
        in_specs=[pl.BlockSpec((tr, SWA_Q_W), lambda i: (i, 0)), cur, nxt, cur, nxt, tab, tab],
        out_specs=[pl.BlockSpec((tr, SWA_Q_W), lambda i: (i, 0)), tab, tab],
        out_shape=[jax.ShapeDtypeStruct((S, SWA_Q_W), BF16), jax.ShapeDtypeStruct((S, LANES), BF16),
                   jax.ShapeDtypeStruct((S, LANES), BF16)],
        compiler_params=_cparams("parallel"),
    )(dq, dk_cur, dk_prev, dv_cur, dv_prev, cos_t, sin_t)


def _dot_nt(a, b):
    return lax.dot_general(a, b, (((1,), (1,)), ((), ())), preferred_element_type=F32)


def _dot_tn(a, b):
    return lax.dot_general(a, b, (((0,), (0,)), ((), ())), preferred_element_type=F32)


def _dot_nn(a, b):
    return lax.dot_general(a, b, (((1,), (0,)), ((), ())), preferred_element_type=F32)


def _roll_half(t):
    return pltpu.roll(t.astype(F32), HEAD_DIM, 1).astype(t.dtype)


SWA_STACK = SWA_GROUP // 2


def _swa_common(hk, n, kp_ref, kc_ref, vp_ref, vc_ref):
    k2 = jnp.concatenate([kp_ref[...], kc_ref[...]], axis=0)
    v2 = jnp.concatenate([vp_ref[...], vc_ref[...]], axis=0)
    k_sw, v_sw = _roll_half(k2), _roll_half(v2)
    rows = SWA_STACK * WINDOW
    row = lax.broadcasted_iota(jnp.int32, (rows, 2 * WINDOW), 0) % WINDOW
    col = lax.broadcasted_iota(jnp.int32, (rows, 2 * WINDOW), 1)
    diff = row + WINDOW - col
    allowed = (diff >= 0) & (diff < WINDOW) & ((col >= WINDOW) | (n > 0))
    lane = lax.broadcasted_iota(jnp.int32, (1, LANES), 1)
    half = [lane < HEAD_DIM, lane >= HEAD_DIM]
    kk = [jnp.where(hk == a, k2, k_sw) for a in range(2)]
    vv = [jnp.where(hk == a, v2, v_sw) for a in range(2)]
    return allowed, half, kk, vv


def _swa_stack(ref, mask, scale=None):
    parts = []
    for t in range(SWA_STACK):
        blk = ref[:, t * LANES:(t + 1) * LANES]
        if scale is not None:
            blk = blk * jnp.asarray(scale, blk.dtype)
        parts.append(jnp.where(mask, blk, jnp.zeros_like(blk)))
    return jnp.concatenate(parts, axis=0)


def _swa_sink_column(sink_ref, hk, a):
    blk = lax.broadcasted_iota(jnp.int32, (SWA_STACK * WINDOW, 1), 0) // WINDOW
    col = jnp.zeros((SWA_STACK * WINDOW, 1), F32)
    for t in range(SWA_STACK):
        col = jnp.where(blk == t, sink_ref[hk * SWA_GROUP + 2 * t + a], col)
    return col


def _swa_probs(qm, kk, allowed, sink):
    s = jnp.where(allowed, _dot_nt(qm, kk), NEG)
    m = jnp.maximum(jnp.max(s, axis=1, keepdims=True), sink)
    e = jnp.exp(s - m)
    es = jnp.exp(sink - m)
    inv = 1.0 / (jnp.sum(e, axis=1, keepdims=True) + es)
    return e * inv, es * inv


def _swa_fwd(q_rope, k_rope, proj, sinks, *, v_off, name):
    S = q_rope.shape[0]
    nb = S // WINDOW
    gw = SWA_GROUP * HEAD_DIM

    def body(sink_ref, q_ref, kp_ref, kc_ref, vp_ref, vc_ref, o_ref):
        hk, n = pl.program_id(0), pl.program_id(1)
        allowed, half, kk, vv = _swa_common(hk, n, kp_ref, kc_ref, vp_ref, vc_ref)
        outs = []
        for a in range(2):
            qm = _swa_stack(q_ref, half[a], ATT_SCALE)
            p, _ = _swa_probs(qm, kk[a], allowed, _swa_sink_column(sink_ref, hk, a))
            outs.append(_dot_nn(p.astype(BF16), vv[a]))
        for t in range(SWA_STACK):
            rows = slice(t * WINDOW, (t + 1) * WINDOW)
            o_ref[:, t * LANES:(t + 1) * LANES] = jnp.where(half[0], outs[0][rows], outs[1][rows]).astype(BF16)

    prev = lambda hk, n: (jnp.maximum(n - 1, 0), 0)
    cur = lambda hk, n: (n, 0)
    vprev = lambda hk, n: (jnp.maximum(n - 1, 0), v_off // LANES)
    vcur = lambda hk, n: (n, v_off // LANES)
    blk = lambda m: pl.BlockSpec((WINDOW, LANES), m)
    return pl.pallas_call(
        body, name=name, grid=(2, nb),
        in_specs=[pl.BlockSpec(memory_space=pltpu.SMEM),
                  pl.BlockSpec((WINDOW, gw), lambda hk, n: (n, hk)),
                  blk(prev), blk(cur), blk(vprev), blk(vcur)],
        out_specs=pl.BlockSpec((WINDOW, gw), lambda hk, n: (n, hk)),
        out_shape=jax.ShapeDtypeStruct((S, SWA_Q_W), BF16),
        compiler_params=_cparams("parallel", "parallel"),
    )(sinks, q_rope, k_rope, k_rope, proj, proj)


def _swa_bwd(q_rope, k_rope, proj, sinks, d_o, *, v_off, name):
    S = q_rope.shape[0]
    nb = S // WINDOW
    gw = SWA_GROUP * HEAD_DIM

    def body(sink_ref, q_ref, kp_ref, kc_ref, vp_ref, vc_ref, do_ref,
             dq_ref, dkc_ref, dkp_ref, dvc_ref, dvp_ref, dsink_ref):
        hk, n = pl.program_id(0), pl.program_id(1)
        allowed, half, kk, vv = _swa_common(hk, n, kp_ref, kc_ref, vp_ref, vc_ref)
        dk_acc = jnp.zeros((2 * WINDOW, LANES), F32)
        dv_acc = jnp.zeros((2 * WINDOW, LANES), F32)
        srow = lax.broadcasted_iota(jnp.int32, (SWA_GROUP, LANES), 0)
        dsink = jnp.zeros((SWA_GROUP, LANES), F32)
        dqs = []
        for a in range(2):
            qm = _swa_stack(q_ref, half[a], ATT_SCALE)
            dom = _swa_stack(do_ref, half[a])
            p, psink = _swa_probs(qm, kk[a], allowed, _swa_sink_column(sink_ref, hk, a))
            dp = _dot_nt(dom, vv[a])
            delta = jnp.sum(p * dp, axis=1, keepdims=True)
            ds = (p * (dp - delta)).astype(BF16)
            dsk = psink * delta
            for t in range(SWA_STACK):
                dsink = dsink + jnp.where(srow == 2 * t + a, -jnp.sum(dsk[t * WINDOW:(t + 1) * WINDOW]), 0.0)
            dqs.append(_dot_nn(ds, kk[a]) * ATT_SCALE)
            dk_acc = dk_acc + _dot_tn(ds, qm)
            dv_acc = dv_acc + _dot_tn(p.astype(BF16), dom)
        for t in range(SWA_STACK):
            rows = slice(t * WINDOW, (t + 1) * WINDOW)
            dq_ref[:, t * LANES:(t + 1) * LANES] = jnp.where(half[0], dqs[0][rows], dqs[1][rows])
        lane = lax.broadcasted_iota(jnp.int32, (1, LANES), 1)
        mine = (lane >= HEAD_DIM) == (hk == 1)
        dk_t = jnp.where(mine, dk_acc + pltpu.roll(dk_acc, HEAD_DIM, 1), 0.0)
        dv_t = jnp.where(mine, dv_acc + pltpu.roll(dv_acc, HEAD_DIM, 1), 0.0)
        dkp_ref[0] = dk_t[:WINDOW]
        dkc_ref[0] = dk_t[WINDOW:]
        dvp_ref[0] = dv_t[:WINDOW]
        dvc_ref[0] = dv_t[WINDOW:]

        @pl.when(n == 0)
        def _():
            dsink_ref[0] = dsink

        @pl.when(n > 0)
        def _():
            dsink_ref[0] += dsink

    prev = lambda hk, n: (jnp.maximum(n - 1, 0), 0)
    cur = lambda hk, n: (n, 0)
    vprev = lambda hk, n: (jnp.maximum(n - 1, 0), v_off // LANES)
    vcur = lambda hk, n: (n, v_off // LANES)
    blk = lambda m: pl.BlockSpec((WINDOW, LANES), m)
    qblk = pl.BlockSpec((WINDOW, gw), lambda hk, n: (n, hk))
    part = pl.BlockSpec((1, WINDOW, LANES), lambda hk, n: (hk, n, 0))
    part_shape = jax.ShapeDtypeStruct((2, S, LANES), F32)
    return pl.pallas_call(
        body, name=name, grid=(2, nb),
        in_specs=[pl.BlockSpec(memory_space=pltpu.SMEM), qblk, blk(prev), blk(cur), blk(vprev), blk(vcur), qblk],
        out_specs=[qblk, part, part, part, part,
                   pl.BlockSpec((1, SWA_GROUP, LANES), lambda hk, n: (hk, 0, 0))],
        out_shape=[jax.ShapeDtypeStruct((S, SWA_Q_W), F32), part_shape, part_shape, part_shape, part_shape,
                   jax.ShapeDtypeStruct((2, SWA_GROUP, LANES), F32)],
        compiler_params=_cparams("parallel", "arbitrary"),
    )(sinks, q_rope, k_rope, k_rope, proj, proj, d_o)


def _fox_prep(z_t, bias_col, *, name):
    H, S = z_t.shape
    tb = _pick(S, 512)

    def body(z_ref, b_ref, o_ref, carry_ref):
        @pl.when(pl.program_id(0) == 0)
        def _():
            carry_ref[...] = jnp.zeros_like(carry_ref)

        zz = z_ref[...] + b_ref[...]
        t = jnp.exp(-jnp.abs(zz))
        log1p = jnp.where(t < 1e-2, t * (1.0 - t * (0.5 - t * (1.0 / 3.0))), jnp.log(1.0 + t))
        logf = jnp.minimum(zz, 0.0) - log1p
        r = lax.broadcasted_iota(jnp.int32, (tb, tb), 0)
        c = lax.broadcasted_iota(jnp.int32, (tb, tb), 1)
        tri = (r <= c).astype(BF16)
        hi = logf.astype(BF16)
        r1 = logf - hi.astype(F32)
        mid = r1.astype(BF16)
        lo = (r1 - mid.astype(F32)).astype(BF16)
        cs = _dot_nn(hi, tri) + _dot_nn(mid, tri) + _dot_nn(lo, tri) + carry_ref[:, 0:1]
        o_ref[...] = -cs
        carry_ref[...] = jnp.zeros_like(carry_ref) + cs[:, tb - 1:tb]

    return pl.pallas_call(
        body, name=name, grid=(S // tb,),
        in_specs=[pl.BlockSpec((H, tb), lambda i: (0, i)), pl.BlockSpec((H, 1), lambda i: (0, 0))],
        out_specs=pl.BlockSpec((H, tb), lambda i: (0, i)),
        out_shape=jax.ShapeDtypeStruct((H, S), F32),
        scratch_shapes=[pltpu.VMEM((H, LANES), F32)],
        compiler_params=_cparams("arbitrary"),
    )(z_t, bias_col)


def _fox_post(drow, dcol, z_t, bias_col, *, name):
    H, S = z_t.shape
    tb = _pick(S, 512)
    nb = S // tb

    def body(dr_ref, d_ref, z_ref, b_ref, dz_ref, db_ref, carry_ref):
        @pl.when(pl.program_id(0) == 0)
        def _():
            carry_ref[...] = jnp.zeros_like(carry_ref)
            db_ref[...] = jnp.zeros_like(db_ref)

        dc = dr_ref[...] - d_ref[...]
        r = lax.broadcasted_iota(jnp.int32, (tb, tb), 0)
        c = lax.broadcasted_iota(jnp.int32, (tb, tb), 1)
        tri = (r >= c).astype(BF16)
        hi = dc.astype(BF16)
        r1 = dc - hi.astype(F32)
        mid = r1.astype(BF16)
        lo = (r1 - mid.astype(F32)).astype(BF16)
        dlogf = _dot_nn(hi, tri) + _dot_nn(mid, tri) + _dot_nn(lo, tri) + carry_ref[:, 0:1]
        carry_ref[...] = jnp.zeros_like(carry_ref) + dlogf[:, 0:1]
        dz = dlogf * _sigmoid(-(z_ref[...] + b_ref[...]))
        dz_ref[...] = dz
        db_ref[...] += jnp.sum(dz, axis=1, keepdims=True)

    rev = lambda i: (0, nb - 1 - i)
    return pl.pallas_call(
        body, name=name, grid=(nb,),
        in_specs=[pl.BlockSpec((H, tb), rev), pl.BlockSpec((H, tb), rev), pl.BlockSpec((H, tb), rev),
                  pl.BlockSpec((H, 1), lambda i: (0, 0))],
        out_specs=[pl.BlockSpec((H, tb), rev), pl.BlockSpec((H, LANES), lambda i: (0, 0))],
        out_shape=[jax.ShapeDtypeStruct((H, S), F32), jax.ShapeDtypeStruct((H, LANES), F32)],
        scratch_shapes=[pltpu.VMEM((H, LANES), F32)],
        compiler_params=_cparams("arbitrary"),
    )(drow, dcol, z_t, bias_col)


def _fox_blocks(S):
    cap = max(LANES, S // 4)
    return (min(FOX_FWD_BLOCKS[0], cap), min(FOX_FWD_BLOCKS[1], cap)), \
           (min(FOX_BWD_BLOCKS[0], cap), min(FOX_BWD_BLOCKS[1], cap))


def _key_bias_blocks(negc, bk):
    H, S = negc.shape
    return negc.reshape(H // 2, 2, S // bk, bk).transpose(0, 2, 1, 3)


def _fox_fwd(proj, negc4, *, q_off, k_off, v_off, bq, bk, name):
    S = proj.shape[0]
    nq, nk = S // bq, S // bk
    npair = FOX_HEADS // 2
    assert bq % bk == 0 or bk % bq == 0
    nmask = max(1, bq // bk)

    gp = FOX_FWD_PAIRS
    gw = gp * LANES
    assert q_off % gw == 0 and k_off % gw == 0 and v_off % gw == 0 and npair % gp == 0

    def body(q_ref, k_ref, v_ref, nc_ref, o_ref, lse_ref):
        i = pl.program_id(1)
        lane = lax.broadcasted_iota(jnp.int32, (1, LANES), 1)
        half = [lane < HEAD_DIM, lane >= HEAD_DIM]
        qh = []
        for g in range(gp):
            q2 = q_ref[:, g * LANES:(g + 1) * LANES] * jnp.asarray(ATT_SCALE, BF16)
            qh += [jnp.where(half[h], q2, jnp.zeros_like(q2)) for h in range(2)]
        row = lax.broadcasted_iota(jnp.int32, (bq, bk), 0)
        col = lax.broadcasted_iota(jnp.int32, (bq, bk), 1)
        rel = row - col
        nfull = (i * bq) // bk

        spare = [HEAD_DIM, 0]
        ones_lane = [lane == spare[h] for h in range(2)]

        def step(j, carry, masked):
            start = pl.multiple_of(j * bk, bk)
            new = []
            for g in range(gp):
                ks = k_ref[pl.ds(start, bk), g * LANES:(g + 1) * LANES]
                vs = v_ref[pl.ds(start, bk), g * LANES:(g + 1) * LANES]
                nb = nc_ref[g, j]
                for h in range(2):
                    m, acc = carry[4 * g + 2 * h:4 * g + 2 * h + 2]
                    vh = jnp.where(half[h], vs, jnp.where(ones_lane[h], jnp.ones_like(vs), jnp.zeros_like(vs)))
                    s = _dot_nt(qh[2 * g + h], ks) + nb[h:h + 1, :]
                    if masked:
                        s = jnp.where(rel >= j * bk - i * bq, s, NEG)
                    m_new = jnp.maximum(m, jnp.max(s, axis=1, keepdims=True))
                    p = jnp.exp(s - m_new).astype(BF16)
                    acc = jnp.exp(m - m_new) * acc + _dot_nn(p, vh)
                    new += [m_new, acc]
            return tuple(new)

        init = (jnp.full((bq, 1), NEG, F32), jnp.zeros((bq, LANES), F32)) * (2 * gp)
        carry = lax.fori_loop(0, nfull, lambda j, c: step(j, c, False), init)
        for t in range(nmask):
            carry = step(nfull + t, carry, True)
        for g in range(gp):
            outs, lses = [], []
            for h in range(2):
                m, acc = carry[4 * g + 2 * h:4 * g + 2 * h + 2]
                l = acc[:, spare[h]:spare[h] + 1]
                outs.append(acc * (1.0 / l))
                lses.append(m + jnp.log(l))
            o_ref[:, g * LANES:(g + 1) * LANES] = jnp.where(half[0], outs[0], outs[1]).astype(BF16)
            lse_ref[g] = jnp.where(half[0], lses[0], lses[1])

    seq = lambda off: pl.BlockSpec((S, gw), lambda hp, i: (0, off // gw + hp))
    return pl.pallas_call(
        body, name=name, grid=(npair // gp, nq),
        in_specs=[pl.BlockSpec((bq, gw), lambda hp, i: (i, q_off // gw + hp)), seq(k_off), seq(v_off),
                  pl.BlockSpec((gp, nk, 2, bk), lambda hp, i: (hp, 0, 0, 0))],
        out_specs=[pl.BlockSpec((bq, gw), lambda hp, i: (i, hp)),
                   pl.BlockSpec((gp, bq, LANES), lambda hp, i: (hp, i, 0))],
        out_shape=[jax.ShapeDtypeStruct((S, FOX_W), BF16), jax.ShapeDtypeStruct((npair, S, LANES), F32)],
        compiler_params=_cparams("parallel", "parallel"),
    )(proj, proj, proj, negc4)


def _fox_bwd(proj, negc4, o, lse, d_o, q_t, do_t, *, q_off, k_off, v_off, bq, bk, name, deps=()):
    S = proj.shape[0]
    nq, nk = S // bq, S // bk
    npair = FOX_HEADS // 2
    assert bq % bk == 0 or bk % bq == 0
    nmask = max(1, bk // bq)

    def body(q_ref, k_ref, v_ref, nc_ref, o_ref, lse_ref, do_ref, qt_ref, dot_ref, *rest):
        dq_ref, dk_ref, dv_ref, dn_ref, dr_ref, delta_ref, rs_ref = rest[len(deps):]
        j = pl.program_id(1)
        lane = lax.broadcasted_iota(jnp.int32, (1, LANES), 1)
        half = [lane < HEAD_DIM, lane >= HEAD_DIM]
        spare = [HEAD_DIM, 0]
        ones_lane = [lane == spare[h] for h in range(2)]
        srow = lax.broadcasted_iota(jnp.int32, (LANES, 1), 0)
        rhalf = [srow < HEAD_DIM, srow >= HEAD_DIM]
        ones_row = [srow == spare[h] for h in range(2)]
        k2, v2 = k_ref[...], v_ref[...]
        one_k = jnp.ones_like(k2)
        kh = [jnp.where(half[h], k2, jnp.where(ones_lane[h], one_k, jnp.zeros_like(k2))) for h in range(2)]
        nb = nc_ref[0, 0]
        row = lax.broadcasted_iota(jnp.int32, (bq, bk), 0)
        col = lax.broadcasted_iota(jnp.int32, (bq, bk), 1)
        rel = row - col
        i_first = (j * bk) // bq

        @pl.when(j == 0)
        def _():
            dq_ref[...] = jnp.zeros_like(dq_ref)
            rs_ref[...] = jnp.zeros_like(rs_ref)
            for b in range(nq):
                prod = do_ref[b * bq:(b + 1) * bq, :].astype(F32) * o_ref[b * bq:(b + 1) * bq, :].astype(F32)
                d0 = jnp.sum(jnp.where(half[0], prod, 0.0), axis=1, keepdims=True)
                d1 = jnp.sum(jnp.where(half[1], prod, 0.0), axis=1, keepdims=True)
                delta_ref[b * bq:(b + 1) * bq, :] = jnp.where(half[0], d0, d1)

        def step(i, carry, masked):
            dkt_a, dkt_b, dvt = carry
            dkts = [dkt_a, dkt_b]
            start = pl.multiple_of(i * bq, bq)
            q2 = q_ref[pl.ds(start, bq), :] * jnp.asarray(ATT_SCALE, BF16)
            do2 = do_ref[pl.ds(start, bq), :]
            qt = qt_ref[i] * jnp.asarray(ATT_SCALE, BF16)
            dot = dot_ref[i]
            lse2 = lse_ref[0, pl.ds(start, bq), :]
            del2 = delta_ref[pl.ds(start, bq), :]
            dqf = []
            for h in range(2):
                qm = jnp.where(half[h], q2, jnp.zeros_like(q2))
                dom = jnp.where(half[h], do2, jnp.zeros_like(do2))
                qtm = jnp.where(rhalf[h], qt, jnp.where(ones_row[h], jnp.ones_like(qt), jnp.zeros_like(qt)))
                dotm = jnp.where(rhalf[h], dot, jnp.zeros_like(dot))
                c0 = h * HEAD_DIM
                p = jnp.exp(_dot_nt(qm, k2) + nb[h:h + 1, :] - lse2[:, c0:c0 + 1])
                if masked:
                    p = jnp.where(rel >= j * bk - i * bq, p, 0.0)
                dp = _dot_nt(dom, v2)
                dsb = (p * (dp - del2[:, c0:c0 + 1])).astype(BF16)
                dvt = dvt + _dot_nn(dotm, p.astype(BF16))
                dkts[h] = dkts[h] + _dot_nn(qtm, dsb)
                dqf.append(_dot_nn(dsb, kh[h]))
            dq_ref[pl.ds(start, bq), :] += jnp.where(half[0], dqf[0], dqf[1]) * ATT_SCALE
            rs_ref[pl.ds(start, bq), :] += jnp.where(ones_lane[0], dqf[0], jnp.where(ones_lane[1], dqf[1], 0.0))
            return dkts[0], dkts[1], dvt

        zero = jnp.zeros((LANES, bk), F32)
        carry = (zero, zero, zero)
        for t in range(nmask):
            carry = step(i_first + t, carry, True)
        dkt_a, dkt_b, dvt = lax.fori_loop(i_first + nmask, nq, lambda i, c: step(i, c, False), carry)
        dk_ref[...] = jnp.where(rhalf[0], dkt_a, dkt_b).T.astype(BF16)
        dv_ref[...] = dvt.T.astype(BF16)
        dn_ref[0, 0] = jnp.concatenate([dkt_a[spare[0]:spare[0] + 1], dkt_b[spare[1]:spare[1] + 1]], axis=0)

        @pl.when(j == nk - 1)
        def _():
            for b in range(nq):
                t = rs_ref[b * bq:(b + 1) * bq, :].T
                dr_ref[0, b] = jnp.concatenate([t[spare[0]:spare[0] + 1], t[spare[1]:spare[1] + 1]], axis=0)

    once = pl.Buffered(1)
    seq = lambda off: pl.BlockSpec((S, LANES), lambda hp, j: (0, off // LANES + hp), pipeline_mode=once)
    blk = lambda off: pl.BlockSpec((bk, LANES), lambda hp, j: (j, off // LANES + hp))
    nc = pl.BlockSpec((1, 1, 2, bk), lambda hp, j: (hp, j, 0, 0))
    tsp = pl.BlockSpec((nq, LANES, bq), lambda hp, j: (0, hp, 0), pipeline_mode=once)
    return pl.pallas_call(
        body, name=name, grid=(npair, nk),
        in_specs=[seq(q_off), blk(k_off), blk(v_off), nc, seq(0),
                  pl.BlockSpec((1, S, LANES), lambda hp, j: (hp, 0, 0), pipeline_mode=once), seq(0),
                  tsp, tsp] + [_ANY] * len(deps),
        out_specs=[pl.BlockSpec((S, LANES), lambda hp, j: (0, hp)), blk(0), blk(0), nc,
                   pl.BlockSpec((1, nq, 2, bq), lambda hp, j: (hp, 0, 0, 0))],
        out_shape=[jax.ShapeDtypeStruct((S, FOX_W), F32), jax.ShapeDtypeStruct((S, FOX_W), BF16),
                   jax.ShapeDtypeStruct((S, FOX_W), BF16), jax.ShapeDtypeStruct((npair, nk, 2, bk), F32),
                   jax.ShapeDtypeStruct((npair, nq, 2, bq), F32)],
        scratch_shapes=[pltpu.VMEM((S, LANES), F32), pltpu.VMEM((S, LANES), F32)],
        compiler_params=_cparams("parallel", "arbitrary"),
    )(proj, proj, proj, negc4, o, lse, d_o, q_t, do_t, *deps)


def _exchange(arrs, *, gather, name):
    n = len(arrs)
    npeer = N_DEV - 1

    def body(*refs):
        ins, outs = refs[:n], refs[n:2 * n]
        send_sems, recv_sems, loc_sems = refs[2 * n:]
        x, y, c = lax.axis_index("x"), lax.axis_index("y"), lax.axis_index("c")
        me = 4 * x + 2 * y + c
        peers = []
        for k in range(1, N_DEV):
            px = 1 - x if k & 4 else x
            py = 1 - y if k & 2 else y
            pc = 1 - c if k & 1 else c
            peers.append(((px, py, pc), 4 * px + 2 * py + pc))

        def remote(w, k):
            dev, idx = peers[k]
            src = ins[w] if gather else ins[w].at[idx]
            return pltpu.make_async_remote_copy(
                src_ref=src, dst_ref=outs[w].at[me],
                send_sem=send_sems.at[w * npeer + k], recv_sem=recv_sems.at[w * npeer + k],
                device_id=dev, device_id_type=pl.DeviceIdType.MESH)

        def arrival(w, k):
            dev, idx = peers[k]
            src = ins[w] if gather else ins[w].at[idx]
            return pltpu.make_async_remote_copy(
                src_ref=src, dst_ref=outs[w].at[idx],
                send_sem=send_sems.at[w * npeer + k], recv_sem=recv_sems.at[w * npeer + k],
                device_id=dev, device_id_type=pl.DeviceIdType.MESH)

        local = []
        for w in range(n):
            for k in range(npeer):
                remote(w, k).start()
            cp = pltpu.make_async_copy(ins[w] if gather else ins[w].at[me], outs[w].at[me], loc_sems.at[w])
            cp.start()
            local.append(cp)
        for w in range(n):
            for k in range(npeer):
                arrival(w, k).wait_recv()
        for w in range(n):
            for k in range(npeer):
                remote(w, k).wait_send()
            local[w].wait()

    hbm = pl.BlockSpec(memory_space=pl.ANY)
    out_shape = [jax.ShapeDtypeStruct((N_DEV,) + (a.shape if gather else a.shape[1:]), a.dtype) for a in arrs]
    return pl.pallas_call(
        body, name=name,
        in_specs=[hbm] * n, out_specs=[hbm] * n, out_shape=out_shape,
        scratch_shapes=[pltpu.SemaphoreType.DMA((n * npeer,)), pltpu.SemaphoreType.DMA((n * npeer,)),
                        pltpu.SemaphoreType.DMA((n,))],
        compiler_params=pltpu.CompilerParams(has_side_effects=True),
    )(*arrs)


def _gather_two_level(shard, *, name):
    def body(x_ref, out_ref, send_sems, recv_sems, local_sem):
        x, y, c = lax.axis_index("x"), lax.axis_index("y"), lax.axis_index("c")
        me, sibling = (x, y, c), (x, y, 1 - c)
        chips = [(1 - x, y), (x, 1 - y), (1 - x, 1 - y)]

        def slot(px, py, pc):
            return out_ref.at[4 * px + 2 * py + pc]

        def copy(k, block, to, src=None):
            return pltpu.make_async_remote_copy(
                src_ref=slot(*block) if src is None else src, dst_ref=slot(*block),
                send_sem=send_sems.at[k], recv_sem=recv_sems.at[k],
                device_id=to, device_id_type=pl.DeviceIdType.MESH)

        mine = pltpu.make_async_copy(x_ref, slot(*me), local_sem)
        mine.start()
        first = [copy(0, me, sibling, src=x_ref)]
        first += [copy(1 + j, me, (*chip, c), src=x_ref) for j, chip in enumerate(chips)]
        for cp in first:
            cp.start()
        passed = [copy(4 + j, (*chip, c), sibling) for j, chip in enumerate(chips)]
        for j, chip in enumerate(chips):
            copy(1 + j, (*chip, c), me).wait_recv()
            passed[j].start()
        copy(0, sibling, me).wait_recv()
        for j, chip in enumerate(chips):
            copy(4 + j, (*chip, 1 - c), me).wait_recv()
        for cp in first + passed:
            cp.wait_send()
        mine.wait()

    return pl.pallas_call(
        body, name=name,
        in_specs=[_ANY], out_specs=_ANY,
        out_shape=jax.ShapeDtypeStruct((N_DEV,) + shard.shape, shard.dtype),
        scratch_shapes=[pltpu.SemaphoreType.DMA((N_DEV - 1,)), pltpu.SemaphoreType.DMA((N_DEV - 1,)),
                        pltpu.SemaphoreType.DMA],
        compiler_params=pltpu.CompilerParams(has_side_effects=True),
    )(shard)


_HBM = pl.BlockSpec(memory_space=pltpu.HBM)
_SEM = pl.BlockSpec(memory_space=pltpu.SEMAPHORE)
_EFFECT = pltpu.SideEffectType.DATAFLOW_SIDE_EFFECTING
NPEER = N_DEV - 1


def _peer_table():
    x, y, c = lax.axis_index("x"), lax.axis_index("y"), lax.axis_index("c")
    peers = []
    for k in range(1, N_DEV):
        px = 1 - x if k & 4 else x
        py = 1 - y if k & 2 else y
        pc = 1 - c if k & 1 else c
        peers.append(((px, py, pc), 4 * px + 2 * py + pc))
    return 4 * x + 2 * y + c, peers


def _split_copy(ins, lands, send_sems, recv_sems, gather, me, peers, w, k, arriving):
    dev, idx = peers[k]
    return pltpu.make_async_remote_copy(
        src_ref=ins[w] if gather else ins[w].at[idx],
        dst_ref=lands[w].at[idx if arriving else me],
        send_sem=send_sems.at[w * NPEER + k], recv_sem=recv_sems.at[w * NPEER + k],
        device_id=dev, device_id_type=pl.DeviceIdType.MESH)


def _exchange_start(arrs, *, gather, name, deps=()):
    n = len(arrs)
    land_shapes = [(N_DEV,) + (a.shape if gather else a.shape[1:]) for a in arrs]

    def body(*refs):
        ins, lands = refs[:n], refs[n:2 * n]
        send_sems, recv_sems = refs[2 * n + len(deps)], refs[2 * n + len(deps) + 1]
        token = refs[-1]
        me, peers = _peer_table()
        for w in range(n):
            for k in range(NPEER):
                _split_copy(ins, lands, send_sems, recv_sems, gather, me, peers, w, k, False).start()
        token[...] = jnp.zeros_like(token)

    out_shape = ([pltpu.SemaphoreType.DMA((n * NPEER,)), pltpu.SemaphoreType.DMA((n * NPEER,))]
                 + [pltpu.HBM(a.shape, a.dtype) for a in arrs]
                 + [pltpu.HBM(s, a.dtype) for s, a in zip(land_shapes, arrs)]
                 + [jax.ShapeDtypeStruct((8, LANES), F32)])
    res = pl.pallas_call(
        body, name=name,
        in_specs=[_HBM] * (2 * n) + [_ANY] * len(deps),
        out_specs=[_SEM, _SEM] + [_HBM] * (2 * n) + [pl.BlockSpec(memory_space=pltpu.VMEM)],
        out_shape=out_shape,
        input_output_aliases={i: 2 + i for i in range(2 * n)},
        compiler_params=pltpu.CompilerParams(has_side_effects=_EFFECT),
    )(*[pltpu.with_memory_space_constraint(a, pltpu.HBM) for a in arrs],
      *[pltpu.with_memory_space_constraint(lax.empty(s, a.dtype), pltpu.HBM) for s, a in zip(land_shapes, arrs)],
      *deps)
    return (n, gather, res[0], res[1], res[2:2 + n], res[2 + n:2 + 2 * n]), res[-1]


def _exchange_wait(handle, after, *, name):
    n, gather, send_sems, recv_sems, ins_thru, lands_thru = handle

    def body(*refs):
        ins, lands = refs[:n], refs[n:2 * n]
        send_s, recv_s = refs[2 * n], refs[2 * n + 1]
        me, peers = _peer_table()
        for w in range(n):
            for k in range(NPEER):
                _split_copy(ins, lands, send_s, recv_s, gather, me, peers, w, k, False).wait_send()
                _split_copy(ins, lands, send_s, recv_s, gather, me, peers, w, k, True).wait_recv()

    res = pl.pallas_call(
        body, name=name,
        in_specs=[_HBM] * (2 * n) + [_SEM, _SEM, pl.BlockSpec(memory_space=pl.ANY)],
        out_specs=[_HBM] * (2 * n),
        out_shape=[pltpu.HBM(a.shape, a.dtype) for a in list(ins_thru) + list(lands_thru)],
        input_output_aliases={i: i for i in range(2 * n)},
        compiler_params=pltpu.CompilerParams(has_side_effects=_EFFECT),
    )(*ins_thru, *lands_thru, send_sems, recv_sems, after)
    return res[:n], res[n:2 * n]


def _ordered_sum(s_ref, own_ref):
    if own_ref is None:
        blocks = [s_ref[q].astype(F32) for q in range(N_DEV)]
    else:
        me = 4 * lax.axis_index("x") + 2 * lax.axis_index("y") + lax.axis_index("c")
        own = own_ref[...]
        blocks = [jnp.where(me == q, own, s_ref[q]).astype(F32) for q in range(N_DEV)]
    acc = blocks[0]
    for b in blocks[1:]:
        acc = acc + b
    return acc


def _sum8(stack, own, *, name):
    _, R, C = stack.shape
    if R % 8 == 0:
        tr, tc = _pick(R, max(8, STEP_BYTES // (C * 4 * (N_DEV + 2))), 8), C
    else:
        tr, tc = R, _pick(C, max(LANES, STEP_BYTES // (R * 4 * (N_DEV + 2))))

    def body(s_ref, own_ref, o_ref):
        o_ref[...] = _ordered_sum(s_ref, own_ref)

    blk = pl.BlockSpec((tr, tc), lambda i, j: (i, j))
    return pl.pallas_call(
        body, name=name, grid=(R // tr, C // tc),
        in_specs=[pl.BlockSpec((N_DEV, tr, tc), lambda i, j: (0, i, j)), blk],
        out_specs=blk,
        out_shape=jax.ShapeDtypeStruct((R, C), F32),
        compiler_params=_cparams("parallel", "parallel"),
    )(stack, own)


def _adamw_math(w, g, m, v):
    m = ADAM_B1 * m + (1.0 - ADAM_B1) * g
    v = ADAM_B2 * v + (1.0 - ADAM_B2) * (g * g)
    m_hat = m / (1.0 - ADAM_B1 ** ADAM_STEP)
    v_hat = v / (1.0 - ADAM_B2 ** ADAM_STEP)
    delta = -ADAM_LR * (m_hat / (jnp.sqrt(v_hat) + ADAM_EPS) + ADAM_WD * w)
    return delta, m, v


def _adamw(w, g, m, v, *, name, stacked, own=None):
    R, C = w.shape
    tr = _pick(R, max(8, STEP_BYTES // (C * 4 * (8 + (N_DEV if stacked else 1)))), 8)
    has_own = own is not None

    def body(w_ref, g_ref, m_ref, v_ref, *rest):
        go_ref, d_ref, mo_ref, vo_ref = rest[-4:]
        g = _ordered_sum(g_ref, rest[0] if has_own else None) if stacked else g_ref[...]
        delta, m2, v2 = _adamw_math(w_ref[...], g, m_ref[...], v_ref[...])
        go_ref[...] = g
        d_ref[...] = delta
        mo_ref[...] = m2
        vo_ref[...] = v2

    row = pl.BlockSpec((tr, C), lambda i: (i, 0))
    g_spec = pl.BlockSpec((N_DEV, tr, C), lambda i: (0, i, 0)) if stacked else row
    return pl.pallas_call(
        body, name=name, grid=(R // tr,),
        in_specs=[row, g_spec, row, row] + [row] * has_own, out_specs=[row] * 4,
        out_shape=[jax.ShapeDtypeStruct((R, C), F32)] * 4,
        compiler_params=_cparams("parallel"),
    )(w, g, m, v, *([own] if has_own else []))


def kernel(x, positions, attn_norm, w_in, fox_f_bias, swa_sinks, w_branch_swa, w_branch_fox, w_out, mlp_norm, w_up, w_down, final_norm, loss_target, m_attn_norm, m_w_in, m_fox_f_bias, m_swa_sinks, m_w_branch_swa, m_w_branch_fox, m_w_out, m_mlp_norm, m_w_up, m_w_down, m_final_norm, v_attn_norm, v_w_in, v_fox_f_bias, v_swa_sinks, v_w_branch_swa, v_w_branch_fox, v_w_out, v_mlp_norm, v_w_up, v_w_down, v_final_norm):
    S, D = x.shape[1], x.shape[2]
    DFF = w_up.shape[2] * N_DEV
    d_in = w_in.shape[2] * N_DEV
    assert d_in == QKV_W + FOX_HEADS + 2 * D and (2 * D) % SWA_Q_W == 0 and S % (4 * LANES) == 0
    q_off = 2 * D
    k_off = q_off + SWA_Q_W
    v_off = k_off + SWA_KV_W
    fq_off = v_off + SWA_KV_W
    fk_off = fq_off + FOX_W
    fv_off = fk_off + FOX_W
    fl_off = fv_off + FOX_W
    NP = fl_off + FL_PAD
    x2d, tgt = x[0], loss_target[0]

    shards = [w_in[0].T.astype(BF16), w_branch_swa[0].T.astype(BF16), w_branch_fox[0].T.astype(BF16),
              w_out[0].astype(BF16), w_up[0].T.astype(BF16), w_down[0].astype(BF16)]
    me = 4 * lax.axis_index("x") + 2 * lax.axis_index("y") + lax.axis_index("c")

    def filled(stack, own):
        return lax.dynamic_update_slice(stack, own[None], (me,) + (0,) * own.ndim)

    g_in = _gather_two_level(shards[0], name="gather_w_in")
    h_rest, tok_rest = _exchange_start(shards[1:], gather=True, name="gather_rest_start", deps=[g_in])

    tm = _pick(S, 1024)
    td = _pick(D, 1024)
    tf = _pick(DFF, 1024)
    tnp = _pick(NP, 1024)

    h1 = _rms_fwd(x2d, attn_norm, name="rms1", deps=[tok_rest])
    w_in_t = g_in.reshape(d_in, D)
    w_in_p = jnp.concatenate([w_in_t[QKV_W + FOX_HEADS:], w_in_t[:QKV_W], w_in_t[QKV_W:QKV_W + FOX_HEADS],
                              jnp.zeros((FL_PAD - FOX_HEADS, D), BF16)], axis=0)
    w_fl_t = w_in_t[QKV_W:QKV_W + FOX_HEADS]
    proj, = _matmul(h1, w_in_p, mode="nt", name="mm_in", out_dtypes=[BF16], tm=tm, tn=tnp, tk=D)
    z_t, = _matmul(w_fl_t, h1, mode="nt", name="mm_flogit", out_dtypes=[F32],
                   tm=FOX_HEADS, tn=_pick(S, 2048), tk=D)
    bias_col = fox_f_bias.reshape(FOX_HEADS, 1)
    negc = _fox_prep(z_t, bias_col, name="fox_prep")
    (fbq, fbk), (bbq, bbk) = _fox_blocks(S)
    inv_freq = ROPE_THETA ** (-jnp.arange(0, HEAD_DIM, 2, dtype=F32) / HEAD_DIM)
    invf = jnp.tile(inv_freq, LANES // (HEAD_DIM // 2)).reshape(1, LANES)
    cos_t, sin_t = _rope_tables(positions.reshape(S, 1), invf, name="rope_tables")
    q_rope, k_rope = _rope_fwd(proj, cos_t, sin_t, q_off=q_off, k_off=k_off, name="rope_fwd")
    sinks = swa_sinks.reshape(-1)
    o_a = _swa_fwd(q_rope, k_rope, proj, sinks, v_off=v_off, name="swa_fwd")
    o_b, lse = _fox_fwd(proj, _key_bias_blocks(negc, fbk), q_off=fq_off, k_off=fk_off, v_off=fv_off,
                        bq=fbq, bk=fbk, name="fox_fwd")
    s_rest, g_rest = _exchange_wait(h_rest, o_b, name="gather_rest_wait")
    g_bs, g_bf, g_o, g_up, g_dn = [filled(g, s) for g, s in zip(g_rest, s_rest)]
    w_bs_t = g_bs.reshape(D, SWA_Q_W)
    w_bf_t = g_bf.reshape(D, FOX_W)
    w_o = g_o.reshape(D, D)
    w_up_t = g_up.reshape(DFF, D)
    w_dn = g_dn.reshape(DFF, D)
    ya, = _matmul(o_a, w_bs_t, mode="nt", name="mm_branch_swa", out_dtypes=[BF16], tm=tm, tn=td, tk=SWA_Q_W)
    gate_maps = [lambda i, j, k: (i, j), lambda i, j, k: (i, j), lambda i, j, k: (i, j + D // td)]

    def merge_epi(acc, ya_t, ga_t, gb_t):
        merged = _sigmoid(ga_t.astype(F32)) * ya_t.astype(F32) + _sigmoid(gb_t.astype(F32)) * acc
        return acc, merged

    yb, merged = _matmul(o_b, w_bf_t, mode="nt", name="mm_branch_fox", out_dtypes=[BF16, BF16],
                         tm=tm, tn=td, tk=FOX_W, extras=[ya, proj, proj], extra_maps=gate_maps,
                         epilogue=merge_epi)
    x_mid, = _matmul(merged, w_o, mode="nn", name="mm_out", out_dtypes=[F32], tm=tm, tn=td, tk=D,
                     extras=[x2d], epilogue=lambda acc, r: (acc + r,))
    h2 = _rms_fwd(x_mid, mlp_norm, name="rms2")
    u, = _matmul(h2, w_up_t, mode="nt", name="mm_up", out_dtypes=[BF16], tm=tm, tn=tf, tk=D,
                 epilogue=lambda acc: (jnp.maximum(acc, 0.0),))
    x_fin, = _matmul(u, w_dn, mode="nn", name="mm_down", out_dtypes=[F32], tm=tm, tn=td, tk=_pick(DFF, 2048),
                     a_fn=_square_bf16, extras=[x_mid], epilogue=lambda acc, r: (acc + r,))

    dx3, dx3b, dg3, loss_part = _loss_head(x_fin, tgt, final_norm.reshape(1, D), name="loss_head")
    d_up, = _matmul(dx3b, w_dn, mode="nt", name="mm_d_act", out_dtypes=[BF16], tm=tm, tn=tf, tk=D,
                    extras=[u], epilogue=lambda acc, ut: (acc * (2.0 * ut.astype(F32)),))
    tks = _pick(S, 1024)
    dw_dn, = _matmul(u, dx3b, mode="tn", name="mm_dw_down", out_dtypes=[F32], tm=tf, tn=td, tk=tks,
                     a_fn=_square_bf16)
    dh2, = _matmul(d_up, w_up_t, mode="nn", name="mm_dh2", out_dtypes=[F32], tm=tm, tn=td, tk=_pick(DFF, 2048))
    dw_up_t, = _matmul(d_up, h2, mode="tn", name="mm_dw_up", out_dtypes=[F32], tm=tf, tn=td, tk=tks)
    h_s1, tok_s1 = _exchange_start([dw_up_t.reshape(N_DEV, DFF // N_DEV, D), dw_dn.reshape(N_DEV, DFF // N_DEV, D)],
                                   gather=False, name="scatter_mlp_start")
    dx2, dx2b, dg2 = _rms_bwd(dh2, x_mid, mlp_norm, dx3, name="rms2_bwd", want_bf16=True, deps=[tok_s1])

    def gate_bwd_epi(dm, ya_t, yb_t, ga_t, gb_t):
        sa, sb = _sigmoid(ga_t.astype(F32)), _sigmoid(gb_t.astype(F32))
        return (dm * sa, dm * sb, dm * ya_t.astype(F32) * sa * (1.0 - sa), dm * yb_t.astype(F32) * sb * (1.0 - sb))

    gmaps = [lambda i, j, k: (i, j), lambda i, j, k: (i, j), lambda i, j, k: (i, j),
             lambda i, j, k: (i, j + D // td)]
    d_ya, d_yb, d_ga, d_gb = _matmul(dx2b, w_o, mode="nt", name="mm_d_merged", out_dtypes=[BF16] * 4,
                                     tm=tm, tn=td, tk=D, extras=[ya, yb, proj, proj], extra_maps=gmaps,
                                     epilogue=gate_bwd_epi)
    dw_o, = _matmul(merged, dx2b, mode="tn", name="mm_dw_out", out_dtypes=[F32], tm=td, tn=td, tk=tks)
    d_oa, = _matmul(d_ya, w_bs_t, mode="nn", name="mm_d_oa", out_dtypes=[BF16], tm=tm, tn=SWA_Q_W, tk=D)
    d_ob, = _matmul(d_yb, w_bf_t, mode="nn", name="mm_d_ob", out_dtypes=[BF16], tm=tm, tn=FOX_W, tk=D)
    dw_bs_t, = _matmul(d_ya, o_a, mode="tn", name="mm_dw_bs", out_dtypes=[F32], tm=td, tn=SWA_Q_W, tk=tks)
    dw_bf_t, = _matmul(d_yb, o_b, mode="tn", name="mm_dw_bf", out_dtypes=[F32], tm=td, tn=FOX_W, tk=tks)
    h_s2, tok_s2 = _exchange_start([dw_bs_t.reshape(N_DEV, D // N_DEV, SWA_Q_W),
                                    dw_bf_t.reshape(N_DEV, D // N_DEV, FOX_W), dw_o.reshape(N_DEV, D // N_DEV, D)],
                                   gather=False, name="scatter_attn_start")
    def row_blocks_t(a):
        return a.reshape(S // bbq, bbq, FOX_W).transpose(0, 2, 1)

    d_fq, d_fk, d_fv, dcol4, drow4 = _fox_bwd(proj, _key_bias_blocks(negc, bbk), o_b, lse, d_ob,
                                              row_blocks_t(proj[:, fq_off:fq_off + FOX_W]), row_blocks_t(d_ob),
                                              q_off=fq_off, k_off=fk_off, v_off=fv_off, bq=bbq, bk=bbk,
                                              name="fox_bwd", deps=[tok_s2])
    dcol = dcol4.transpose(0, 2, 1, 3).reshape(FOX_HEADS, S)
    drow = drow4.transpose(0, 2, 1, 3).reshape(FOX_HEADS, S)
    dz_t, dbias_l = _fox_post(drow, dcol, z_t, bias_col, name="fox_post")
    dq_r, dk_c, dk_p, dv_c, dv_p, dsink_l = _swa_bwd(q_rope, k_rope, proj, sinks, d_oa, v_off=v_off, name="swa_bwd")
    d_aq, d_ak, d_av = _rope_bwd(dq_r, dk_c, dk_p, dv_c, dv_p, cos_t, sin_t, name="rope_bwd")
    dz_pad = jnp.pad(dz_t.T.astype(BF16), ((0, 0), (0, FL_PAD - FOX_HEADS)))
    d_proj = jnp.concatenate([d_ga, d_gb, d_aq, d_ak, d_av, d_fq.astype(BF16), d_fk, d_fv, dz_pad], axis=1)
    tkp = _pick(NP, 2304)
    dw_in_p, = _matmul(d_proj, h1, mode="tn", name="mm_dw_in", out_dtypes=[F32], tm=_pick(NP, 512), tn=D, tk=tks)
    dw_in_t = jnp.concatenate([dw_in_p[q_off:q_off + QKV_W], dw_in_p[fl_off:fl_off + FOX_HEADS], dw_in_p[:q_off]],
                              axis=0).astype(BF16)
    h_s3, tok_s3 = _exchange_start([dw_in_t.reshape(N_DEV, d_in // N_DEV, D)], gather=False,
                                   name="scatter_in_start")
    dh1, = _matmul(d_proj, w_in_p, mode="nn", name="mm_dh1", out_dtypes=[F32], tm=tm, tn=td, tk=tkp, deps=[tok_s3])
    dx, dg1 = _rms_bwd(dh1, x2d, attn_norm, dx2, name="rms1_bwd", want_bf16=False)

    dbias = dbias_l[:, 0]
    dsinks = dsink_l[:, :, 0].reshape(-1)
    nsm = 3 * D + 2 * LANES
    tail = jnp.zeros((2 * LANES,), F32)
    small_g = jnp.concatenate([dg1[0], dg2[0], dg3[0],
                               tail.at[0:16].set(dbias).at[16:32].set(dsinks).at[32].set(loss_part[0, 0])])

    def pack(a_norm, b_norm, f_norm, bias, snk):
        return jnp.concatenate([a_norm[0], b_norm[0], f_norm,
                                tail.at[0:16].set(bias[0]).at[16:32].set(snk[0])]).reshape(1, nsm)

    small_stack, = _exchange([small_g.reshape(1, nsm)], gather=True, name="gather_small")
    u_sm = _adamw(pack(attn_norm, mlp_norm, final_norm, fox_f_bias, swa_sinks), small_stack,
                  pack(m_attn_norm, m_mlp_norm, m_final_norm, m_fox_f_bias, m_swa_sinks),
                  pack(v_attn_norm, v_mlp_norm, v_final_norm, v_fox_f_bias, v_swa_sinks),
                  name="adamw_small", stacked=True)
    loss = u_sm[0][0, 3 * D + 32]

    def own_of(src):
        return lax.dynamic_index_in_dim(src, me, 0, keepdims=False)

    def update_t(stack, src, w, m, v, nm):
        g = _sum8(stack, own_of(src), name="sum_" + nm).T
        return _adamw(w[0], g, m[0], v[0], name="adamw_" + nm, stacked=False)

    def update(stack, src, w, m, v, nm):
        return _adamw(w[0], stack, m[0], v[0], name="adamw_" + nm, stacked=True, own=own_of(src))

    (s_up, s_dn), (r_up, r_dn) = _exchange_wait(h_s1, u_sm[1], name="scatter_mlp_wait")
    u_up = update_t(r_up, s_up, w_up, m_w_up, v_w_up, "w_up")
    u_dn = update(r_dn, s_dn, w_down, m_w_down, v_w_down, "w_down")
    (s_bs, s_bf, s_o), (r_bs, r_bf, r_o) = _exchange_wait(h_s2, u_dn[1], name="scatter_attn_wait")
    u_bs = update_t(r_bs, s_bs, w_branch_swa, m_w_branch_swa, v_w_branch_swa, "w_bs")
    u_bf = update_t(r_bf, s_bf, w_branch_fox, m_w_branch_fox, v_w_branch_fox, "w_bf")
    u_o = update(r_o, s_o, w_out, m_w_out, v_w_out, "w_out")
    (s_w_in,), (r_in,) = _exchange_wait(h_s3, u_o[1], name="scatter_in_wait")
    u_in = update_t(r_in, s_w_in, w_in, m_w_in, v_w_in, "w_in")

    def small(kind):
        a = u_sm[kind][0]
        return dict(attn_norm=a[0:D][None], mlp_norm=a[D:2 * D][None], final_norm=a[2 * D:3 * D],
                    fox_f_bias=a[3 * D:3 * D + 16][None], swa_sinks=a[3 * D + 16:3 * D + 32][None])

    big = dict(w_in=u_in, w_branch_swa=u_bs, w_branch_fox=u_bf, w_out=u_o, w_up=u_up, w_down=u_dn)
    order = ["attn_norm", "w_in", "fox_f_bias", "swa_sinks", "w_branch_swa", "w_branch_fox", "w_out", "mlp_norm",
             "w_up", "w_down", "final_norm"]
    outs = [loss, dx[None]]
    for kind in range(4):
        sm = small(kind)
        for nm in order:
            outs.append(big[nm][kind][None] if nm in big else sm[nm])
    return tuple(outs)
```

```python
import functools

import jax
import jax.numpy as jnp
from jax import lax
from jax.experimental import pallas as pl
from jax.experimental.pallas import tpu as pltpu

F32 = jnp.float32
BF16 = jnp.bfloat16

N_DEV = 8
HEAD_DIM = 64
SWA_Q_W = 1024
SWA_KV_W = 128
SWA_GROUP = 8
WINDOW = 128
FOX_W = 1024
FOX_HEADS = 16
QKV_W = SWA_Q_W + 2 * SWA_KV_W + 3 * FOX_W
FL_PAD = 256
ROPE_THETA = 10000.0
RMS_EPS = 1e-6
ATT_SCALE = 0.125
NEG = -1e30

ADAM_LR = 0.001
ADAM_B1 = 0.9
ADAM_B2 = 0.999
ADAM_EPS = 1e-08
ADAM_WD = 0.01
ADAM_STEP = 10

FOX_FWD_BLOCKS = (1024, 512)
FOX_BWD_BLOCKS = (512, 512)
FOX_FWD_PAIRS = 2

LANES = 128
VMEM_LIMIT = 56 * 1024 * 1024
STEP_BYTES = 12 * 1024 * 1024


def _cparams(*sem):
    return pltpu.CompilerParams(dimension_semantics=sem, vmem_limit_bytes=VMEM_LIMIT)


def _pick(dim, pref, align=LANES):
    best = None
    t = align
    while t <= min(dim, pref):
        if dim % t == 0:
            best = t
        t += align
    return best if best is not None else dim


_DIMS = {"nn": ((1,), (0,)), "nt": ((1,), (1,)), "tn": ((0,), (0,))}


_ANY = pl.BlockSpec(memory_space=pl.ANY)


def _matmul(a, b, *, mode, name, out_dtypes, tm, tn, tk, extras=(), extra_maps=None,
            a_fn=None, epilogue=None, deps=()):
    if mode == "nn":
        (M, K), (K2, N) = a.shape, b.shape
    elif mode == "nt":
        (M, K), (N, K2) = a.shape, b.shape
    else:
        (K, M), (K2, N) = a.shape, b.shape
    assert K == K2, (name, a.shape, b.shape)
    assert M % tm == 0 and N % tn == 0 and K % tk == 0, (name, M, N, K, tm, tn, tk)
    nk = K // tk
    ne, no = len(extras), len(out_dtypes)
    dims = (_DIMS[mode], ((), ()))

    def body(*refs):
        a_ref, b_ref = refs[0], refs[1]
        ex_refs = refs[2:2 + ne]
        out_refs = refs[2 + ne + len(deps):2 + ne + len(deps) + no]

        def finish(acc):
            res = (acc,) if epilogue is None else epilogue(acc, *[e[...] for e in ex_refs])
            for o_ref, r in zip(out_refs, res):
                o_ref[...] = r.astype(o_ref.dtype)

        def product():
            av = a_ref[...]
            if a_fn is not None:
                av = a_fn(av)
            return lax.dot_general(av, b_ref[...], dims, preferred_element_type=F32)

        if nk == 1:
            finish(product())
        else:
            acc_ref = refs[-1]
            k = pl.program_id(2)

            @pl.when(k == 0)
            def _():
                acc_ref[...] = jnp.zeros_like(acc_ref)

            acc_ref[...] += product()

            @pl.when(k == nk - 1)
            def _():
                finish(acc_ref[...])

    if mode == "tn":
        a_spec = pl.BlockSpec((tk, tm), lambda i, j, k: (k, i))
    else:
        a_spec = pl.BlockSpec((tm, tk), lambda i, j, k: (i, k))
    if mode == "nt":
        b_spec = pl.BlockSpec((tn, tk), lambda i, j, k: (j, k))
    else:
        b_spec = pl.BlockSpec((tk, tn), lambda i, j, k: (k, j))
    if extra_maps is None:
        extra_maps = [lambda i, j, k: (i, j)] * ne
    ex_specs = [pl.BlockSpec((tm, tn), m) for m in extra_maps]
    out_spec = [pl.BlockSpec((tm, tn), lambda i, j, k: (i, j)) for _ in range(no)]
    res = pl.pallas_call(
        body,
        name=name,
        grid=(M // tm, N // tn, nk),
        in_specs=[a_spec, b_spec] + ex_specs + [_ANY] * len(deps),
        out_specs=out_spec,
        out_shape=[jax.ShapeDtypeStruct((M, N), d) for d in out_dtypes],
        scratch_shapes=[pltpu.VMEM((tm, tn), F32)] if nk > 1 else [],
        compiler_params=_cparams("parallel", "parallel", "arbitrary"),
    )(a, b, *extras, *deps)
    return res


def _square_bf16(t):
    tf = t.astype(F32)
    return (tf * tf).astype(BF16)


def _sigmoid(g):
    return 1.0 / (1.0 + jnp.exp(-g))


def _rms_fwd(x, gain, *, name, deps=()):
    S, D = x.shape
    tr = _pick(S, 512, 8)

    def body(x_ref, g_ref, *rest):
        h_ref = rest[-1]
        xv = x_ref[...]
        r = lax.rsqrt(jnp.mean(xv * xv, axis=-1, keepdims=True) + RMS_EPS)
        h_ref[...] = (xv * r * g_ref[...]).astype(BF16)

    return pl.pallas_call(
        body, name=name, grid=(S // tr,),
        in_specs=[pl.BlockSpec((tr, D), lambda i: (i, 0)), pl.BlockSpec((1, D), lambda i: (0, 0))] + [_ANY] * len(deps),
        out_specs=pl.BlockSpec((tr, D), lambda i: (i, 0)),
        out_shape=jax.ShapeDtypeStruct((S, D), BF16),
        compiler_params=_cparams("parallel"),
    )(x, gain, *deps)


def _rms_bwd(dh, x, gain, dres, *, name, want_bf16, deps=()):
    S, D = x.shape
    tr = _pick(S, 256, 8)

    def body(dh_ref, x_ref, g_ref, dres_ref, *rest):
        outs = rest[len(deps):]
        dx_ref, dg_ref = outs[0], outs[-1]
        xv = x_ref[...]
        r = lax.rsqrt(jnp.mean(xv * xv, axis=-1, keepdims=True) + RMS_EPS)
        xh = xv * r
        dhv = dh_ref[...]
        t = dhv * g_ref[...]
        dx = r * (t - xh * jnp.mean(t * xh, axis=-1, keepdims=True)) + dres_ref[...]
        dx_ref[...] = dx
        if want_bf16:
            outs[1][...] = dx.astype(BF16)
        part = jnp.sum(dhv * xh, axis=0, keepdims=True)

        @pl.when(pl.program_id(0) == 0)
        def _():
            dg_ref[...] = part

        @pl.when(pl.program_id(0) > 0)
        def _():
            dg_ref[...] += part

    row = pl.BlockSpec((tr, D), lambda i: (i, 0))
    vec = pl.BlockSpec((1, D), lambda i: (0, 0))
    out_shape = [jax.ShapeDtypeStruct((S, D), F32)]
    out_specs = [row]
    if want_bf16:
        out_shape.append(jax.ShapeDtypeStruct((S, D), BF16))
        out_specs.append(row)
    out_shape.append(jax.ShapeDtypeStruct((1, D), F32))
    out_specs.append(vec)
    return pl.pallas_call(
        body, name=name, grid=(S // tr,),
        in_specs=[row, row, vec, row] + [_ANY] * len(deps), out_specs=out_specs, out_shape=out_shape,
        compiler_params=_cparams("arbitrary"),
    )(dh, x, gain, dres, *deps)


def _loss_head(x3, target, gain, *, name):
    S, D = x3.shape
    tr = _pick(S, 256, 8)

    def body(x_ref, t_ref, g_ref, dx_ref, dxb_ref, dg_ref, loss_ref):
        xv = x_ref[...]
        r = lax.rsqrt(jnp.mean(xv * xv, axis=-1, keepdims=True) + RMS_EPS)
        xh = xv * r
        gv = g_ref[...]
        err = xh * gv - t_ref[...]
        lpart = jnp.zeros((1, LANES), F32) + (0.5 / D) * jnp.sum(err * err)
        dy = err * (1.0 / D)
        t = dy * gv
        dx = r * (t - xh * jnp.mean(t * xh, axis=-1, keepdims=True))
        dx_ref[...] = dx
        dxb_ref[...] = dx.astype(BF16)
        part = jnp.sum(dy * xh, axis=0, keepdims=True)

        @pl.when(pl.program_id(0) == 0)
        def _():
            dg_ref[...] = part
            loss_ref[...] = lpart

        @pl.when(pl.program_id(0) > 0)
        def _():
            dg_ref[...] += part
            loss_ref[...] += lpart

    row = pl.BlockSpec((tr, D), lambda i: (i, 0))
    vec = pl.BlockSpec((1, D), lambda i: (0, 0))
    return pl.pallas_call(
        body, name=name, grid=(S // tr,),
        in_specs=[row, row, vec],
        out_specs=[row, row, vec, pl.BlockSpec((1, LANES), lambda i: (0, 0))],
        out_shape=[jax.ShapeDtypeStruct((S, D), F32), jax.ShapeDtypeStruct((S, D), BF16),
                   jax.ShapeDtypeStruct((1, D), F32), jax.ShapeDtypeStruct((1, LANES), F32)],
        compiler_params=_cparams("arbitrary"),
    )(x3, target, gain)


def _rope_tables(pos_col, invf, *, name):
    S = pos_col.shape[0]
    tr = _pick(S, 512, 8)

    def body(p_ref, f_ref, cos_ref, sin_ref):
        ang = p_ref[...].astype(F32) * f_ref[...]
        lane = lax.broadcasted_iota(jnp.int32, (1, LANES), 1)
        first = (lane % HEAD_DIM) < HEAD_DIM // 2
        sn = jnp.sin(ang)
        cos_ref[...] = jnp.cos(ang)
        sin_ref[...] = jnp.where(first, -sn, sn)

    return pl.pallas_call(
        body, name=name, grid=(S // tr,),
        in_specs=[pl.BlockSpec((tr, 1), lambda i: (i, 0)), pl.BlockSpec((1, LANES), lambda i: (0, 0))],
        out_specs=[pl.BlockSpec((tr, LANES), lambda i: (i, 0))] * 2,
        out_shape=[jax.ShapeDtypeStruct((S, LANES), F32)] * 2,
        compiler_params=_cparams("parallel"),
    )(pos_col, invf)


def _swap_halves(t):
    lane = lax.broadcasted_iota(jnp.int32, (1, LANES), 1)
    first = (lane % HEAD_DIM) < HEAD_DIM // 2
    return jnp.where(first, pltpu.roll(t, LANES - HEAD_DIM // 2, 1), pltpu.roll(t, HEAD_DIM // 2, 1))


def _rope_fwd(proj, cos_t, sin_t, *, q_off, k_off, name):
    S = proj.shape[0]
    tr = _pick(S, 256, 8)
    nqb = SWA_Q_W // LANES

    def body(q_ref, k_ref, c_ref, s_ref, qo_ref, ko_ref):
        cv, sv = c_ref[...], s_ref[...]
        for b in range(nqb):
            t = q_ref[:, b * LANES:(b + 1) * LANES].astype(F32)
            qo_ref[:, b * LANES:(b + 1) * LANES] = (t * cv + _swap_halves(t) * sv).astype(BF16)
        t = k_ref[...].astype(F32)
        ko_ref[...] = (t * cv + _swap_halves(t) * sv).astype(BF16)

    tab = pl.BlockSpec((tr, LANES), lambda i: (i, 0))
    return pl.pallas_call(
        body, name=name, grid=(S // tr,),
        in_specs=[pl.BlockSpec((tr, SWA_Q_W), lambda i: (i, q_off // SWA_Q_W)),
                  pl.BlockSpec((tr, LANES), lambda i: (i, k_off // LANES)), tab, tab],
        out_specs=[pl.BlockSpec((tr, SWA_Q_W), lambda i: (i, 0)), tab],
        out_shape=[jax.ShapeDtypeStruct((S, SWA_Q_W), BF16), jax.ShapeDtypeStruct((S, LANES), BF16)],
        compiler_params=_cparams("parallel"),
    )(proj, proj, cos_t, sin_t)


def _rope_bwd(dq, dk_cur, dk_prev, dv_cur, dv_prev, cos_t, sin_t, *, name):
    S = dq.shape[0]
    tr = WINDOW
    nb = S // tr
    nqb = SWA_Q_W // LANES

    def body(dq_ref, kc_ref, kp_ref, vc_ref, vp_ref, c_ref, s_ref, dqo_ref, dko_ref, dvo_ref):
        cv, sv = c_ref[...], s_ref[...]
        has_next = (pl.program_id(0) + 1 < nb).astype(F32)
        for b in range(nqb):
            d = dq_ref[:, b * LANES:(b + 1) * LANES]
            dqo_ref[:, b * LANES:(b + 1) * LANES] = (d * cv + _swap_halves(d * sv)).astype(BF16)
        d = kc_ref[0] + kc_ref[1] + has_next * (kp_ref[0] + kp_ref[1])
        dko_ref[...] = (d * cv + _swap_halves(d * sv)).astype(BF16)
        dvo_ref[...] = (vc_ref[0] + vc_ref[1] + has_next * (vp_ref[0] + vp_ref[1])).astype(BF16)

    tab = pl.BlockSpec((tr, LANES), lambda i: (i, 0))
    cur = pl.BlockSpec((2, tr, LANES), lambda i: (0, i, 0))
    nxt = pl.BlockSpec((2, tr, LANES), lambda i: (0, jnp.minimum(i + 1, nb - 1), 0))
    return pl.pallas_call(
        body, name=name, grid=(nb,),
        in_specs=[pl.BlockSpec((tr, SWA_Q_W), lambda i: (i, 0)), cur, nxt, cur, nxt, tab, tab],
        out_specs=[pl.BlockSpec((tr, SWA_Q_W), lambda i: (i, 0)), tab, tab],
        out_shape=[jax.ShapeDtypeStruct((S, SWA_Q_W), BF16), jax.ShapeDtypeStruct((S, LANES), BF16),
                   jax.ShapeDtypeStruct((S, LANES), BF16)],
        compiler_params=_cparams("parallel"),
    )(dq, dk_cur, dk_prev, dv_cur, dv_prev, cos_t, sin_t)


def _dot_nt(a, b):
    return lax.dot_general(a, b, (((1,), (1,)), ((), ())), preferred_element_type=F32)


def _dot_tn(a, b):
    return lax.dot_general(a, b, (((0,), (0,)), ((), ())), preferred_element_type=F32)


def _dot_nn(a, b):
    return lax.dot_general(a, b, (((1,), (0,)), ((), ())), preferred_element_type=F32)


def _roll_half(t):
    return pltpu.roll(t.astype(F32), HEAD_DIM, 1).astype(t.dtype)


SWA_STACK = SWA_GROUP // 2


def _swa_common(hk, n, kp_ref, kc_ref, vp_ref, vc_ref):
    k2 = jnp.concatenate([kp_ref[...], kc_ref[...]], axis=0)
    v2 = jnp.concatenate([vp_ref[...], vc_ref[...]], axis=0)
    k_sw, v_sw = _roll_half(k2), _roll_half(v2)
    rows = SWA_STACK * WINDOW
    row = lax.broadcasted_iota(jnp.int32, (rows, 2 * WINDOW), 0) % WINDOW
    col = lax.broadcasted_iota(jnp.int32, (rows, 2 * WINDOW), 1)
    diff = row + WINDOW - col
    allowed = (diff >= 0) & (diff < WINDOW) & ((col >= WINDOW) | (n > 0))
    lane = lax.broadcasted_iota(jnp.int32, (1, LANES), 1)
    half = [lane < HEAD_DIM, lane >= HEAD_DIM]
    kk = [jnp.where(hk == a, k2, k_sw) for a in range(2)]
    vv = [jnp.where(hk == a, v2, v_sw) for a in range(2)]
    return allowed, half, kk, vv


def _swa_stack(ref, mask, scale=None):
    parts = []
    for t in range(SWA_STACK):
        blk = ref[:, t * LANES:(t + 1) * LANES]
        if scale is not None:
            blk = blk * jnp.asarray(scale, blk.dtype)
        parts.append(jnp.where(mask, blk, jnp.zeros_like(blk)))
    return jnp.concatenate(parts, axis=0)


def _swa_sink_column(sink_ref, hk, a):
    blk = lax.broadcasted_iota(jnp.int32, (SWA_STACK * WINDOW, 1), 0) // WINDOW
    col = jnp.zeros((SWA_STACK * WINDOW, 1), F32)
    for t in range(SWA_STACK):
        col = jnp.where(blk == t, sink_ref[hk * SWA_GROUP + 2 * t + a], col)
    return col


def _swa_probs(qm, kk, allowed, sink):
    s = jnp.where(allowed, _dot_nt(qm, kk), NEG)
    m = jnp.maximum(jnp.max(s, axis=1, keepdims=True), sink)
    e = jnp.exp(s - m)
    es = jnp.exp(sink - m)
    inv = 1.0 / (jnp.sum(e, axis=1, keepdims=True) + es)
    return e * inv, es * inv


def _swa_fwd(q_rope, k_rope, proj, sinks, *, v_off, name):
    S = q_rope.shape[0]
    nb = S // WINDOW
    gw = SWA_GROUP * HEAD_DIM

    def body(sink_ref, q_ref, kp_ref, kc_ref, vp_ref, vc_ref, o_ref):
        hk, n = pl.program_id(0), pl.program_id(1)
        allowed, half, kk, vv = _swa_common(hk, n, kp_ref, kc_ref, vp_ref, vc_ref)
        outs = []
        for a in range(2):
            qm = _swa_stack(q_ref, half[a], ATT_SCALE)
            p, _ = _swa_probs(qm, kk[a], allowed, _swa_sink_column(sink_ref, hk, a))
            outs.append(_dot_nn(p.astype(BF16), vv[a]))
        for t in range(SWA_STACK):
            rows = slice(t * WINDOW, (t + 1) * WINDOW)
            o_ref[:, t * LANES:(t + 1) * LANES] = jnp.where(half[0], outs[0][rows], outs[1][rows]).astype(BF16)

    prev = lambda hk, n: (jnp.maximum(n - 1, 0), 0)
    cur = lambda hk, n: (n, 0)
    vprev = lambda hk, n: (jnp.maximum(n - 1, 0), v_off // LANES)
    vcur = lambda hk, n: (n, v_off // LANES)
    blk = lambda m: pl.BlockSpec((WINDOW, LANES), m)
    return pl.pallas_call(
        body, name=name, grid=(2, nb),
        in_specs=[pl.BlockSpec(memory_space=pltpu.SMEM),
                  pl.BlockSpec((WINDOW, gw), lambda hk, n: (n, hk)),
                  blk(prev), blk(cur), blk(vprev), blk(vcur)],
        out_specs=pl.BlockSpec((WINDOW, gw), lambda hk, n: (n, hk)),
        out_shape=jax.ShapeDtypeStruct((S, SWA_Q_W), BF16),
        compiler_params=_cparams("parallel", "parallel"),
    )(sinks, q_rope, k_rope, k_rope, proj, proj)


def _swa_bwd(q_rope, k_rope, proj, sinks, d_o, *, v_off, name):
    S = q_rope.shape[0]
    nb = S // WINDOW
    gw = SWA_GROUP * HEAD_DIM

    def body(sink_ref, q_ref, kp_ref, kc_ref, vp_ref, vc_ref, do_ref,
             dq_ref, dkc_ref, dkp_ref, dvc_ref, dvp_ref, dsink_ref):
        hk, n = pl.program_id(0), pl.program_id(1)
        allowed, half, kk, vv = _swa_common(hk, n, kp_ref, kc_ref, vp_ref, vc_ref)
        dk_acc = jnp.zeros((2 * WINDOW, LANES), F32)
        dv_acc = jnp.zeros((2 * WINDOW, LANES), F32)
        srow = lax.broadcasted_iota(jnp.int32, (SWA_GROUP, LANES), 0)
        dsink = jnp.zeros((SWA_GROUP, LANES), F32)
        dqs = []
        for a in range(2):
            qm = _swa_stack(q_ref, half[a], ATT_SCALE)
            dom = _swa_stack(do_ref, half[a])
            p, psink = _swa_probs(qm, kk[a], allowed, _swa_sink_column(sink_ref, hk, a))
            dp = _dot_nt(dom, vv[a])
            delta = jnp.sum(p * dp, axis=1, keepdims=True)
            ds = (p * (dp - delta)).astype(BF16)
            dsk = psink * delta
            for t in range(SWA_STACK):
                dsink = dsink + jnp.where(srow == 2 * t + a, -jnp.sum(dsk[t * WINDOW:(t + 1) * WINDOW]), 0.0)
            dqs.append(_dot_nn(ds, kk[a]) * ATT_SCALE)
            dk_acc = dk_acc + _dot_tn(ds, qm)
            dv_acc = dv_acc + _dot_tn(p.astype(BF16), dom)
        for t in range(SWA_STACK):
            rows = slice(t * WINDOW, (t + 1) * WINDOW)
            dq_ref[:, t * LANES:(t + 1) * LANES] = jnp.where(half[0], dqs[0][rows], dqs[1][rows])
        lane = lax.broadcasted_iota(jnp.int32, (1, LANES), 1)
        mine = (lane >= HEAD_DIM) == (hk == 1)
        dk_t = jnp.where(mine, dk_acc + pltpu.roll(dk_acc, HEAD_DIM, 1), 0.0)
        dv_t = jnp.where(mine, dv_acc + pltpu.roll(dv_acc, HEAD_DIM, 1), 0.0)
        dkp_ref[0] = dk_t[:WINDOW]
        dkc_ref[0] = dk_t[WINDOW:]
        dvp_ref[0] = dv_t[:WINDOW]
        dvc_ref[0] = dv_t[WINDOW:]

        @pl.when(n == 0)
        def _():
            dsink_ref[0] = dsink

        @pl.when(n > 0)
        def _():
            dsink_ref[0] += dsink

    prev = lambda hk, n: (jnp.maximum(n - 1, 0), 0)
    cur = lambda hk, n: (n, 0)
    vprev = lambda hk, n: (jnp.maximum(n - 1, 0), v_off // LANES)
    vcur = lambda hk, n: (n, v_off // LANES)
    blk = lambda m: pl.BlockSpec((WINDOW, LANES), m)
    qblk = pl.BlockSpec((WINDOW, gw), lambda hk, n: (n, hk))
    part = pl.BlockSpec((1, WINDOW, LANES), lambda hk, n: (hk, n, 0))
    part_shape = jax.ShapeDtypeStruct((2, S, LANES), F32)
    return pl.pallas_call(
        body, name=name, grid=(2, nb),
        in_specs=[pl.BlockSpec(memory_space=pltpu.SMEM), qblk, blk(prev), blk(cur), blk(vprev), blk(vcur), qblk],
        out_specs=[qblk, part, part, part, part,
                   pl.BlockSpec((1, SWA_GROUP, LANES), lambda hk, n: (hk, 0, 0))],
        out_shape=[jax.ShapeDtypeStruct((S, SWA_Q_W), F32), part_shape, part_shape, part_shape, part_shape,
                   jax.ShapeDtypeStruct((2, SWA_GROUP, LANES), F32)],
        compiler_params=_cparams("parallel", "arbitrary"),
    )(sinks, q_rope, k_rope, k_rope, proj, proj, d_o)


def _fox_prep(z_t, bias_col, *, name):
    H, S = z_t.shape
    tb = _pick(S, 512)

    def body(z_ref, b_ref, o_ref, carry_ref):
        @pl.when(pl.program_id(0) == 0)
        def _():
            carry_ref[...] = jnp.zeros_like(carry_ref)

        zz = z_ref[...] + b_ref[...]
        t = jnp.exp(-jnp.abs(zz))
        log1p = jnp.where(t < 1e-2, t * (1.0 - t * (0.5 - t * (1.0 / 3.0))), jnp.log(1.0 + t))
        logf = jnp.minimum(zz, 0.0) - log1p
        r = lax.broadcasted_iota(jnp.int32, (tb, tb), 0)
        c = lax.broadcasted_iota(jnp.int32, (tb, tb), 1)
        tri = (r <= c).astype(BF16)
        hi = logf.astype(BF16)
        r1 = logf - hi.astype(F32)
        mid = r1.astype(BF16)
        lo = (r1 - mid.astype(F32)).astype(BF16)
        cs = _dot_nn(hi, tri) + _dot_nn(mid, tri) + _dot_nn(lo, tri) + carry_ref[:, 0:1]
        o_ref[...] = -cs
        carry_ref[...] = jnp.zeros_like(carry_ref) + cs[:, tb - 1:tb]

    return pl.pallas_call(
        body, name=name, grid=(S // tb,),
        in_specs=[pl.BlockSpec((H, tb), lambda i: (0, i)), pl.BlockSpec((H, 1), lambda i: (0, 0))],
        out_specs=pl.BlockSpec((H, tb), lambda i: (0, i)),
        out_shape=jax.ShapeDtypeStruct((H, S), F32),
        scratch_shapes=[pltpu.VMEM((H, LANES), F32)],
        compiler_params=_cparams("arbitrary"),
    )(z_t, bias_col)


def _fox_post(drow, dcol, z_t, bias_col, *, name):
    H, S = z_t.shape
    tb = _pick(S, 512)
    nb = S // tb

    def body(dr_ref, d_ref, z_ref, b_ref, dz_ref, db_ref, carry_ref):
        @pl.when(pl.program_id(0) == 0)
        def _():
            carry_ref[...] = jnp.zeros_like(carry_ref)
            db_ref[...] = jnp.zeros_like(db_ref)

        dc = dr_ref[...] - d_ref[...]
        r = lax.broadcasted_iota(jnp.int32, (tb, tb), 0)
        c = lax.broadcasted_iota(jnp.int32, (tb, tb), 1)
        tri = (r >= c).astype(BF16)
        hi = dc.astype(BF16)
        r1 = dc - hi.astype(F32)
        mid = r1.astype(BF16)
        lo = (r1 - mid.astype(F32)).astype(BF16)
        dlogf = _dot_nn(hi, tri) + _dot_nn(mid, tri) + _dot_nn(lo, tri) + carry_ref[:, 0:1]
        carry_ref[...] = jnp.zeros_like(carry_ref) + dlogf[:, 0:1]
        dz = dlogf * _sigmoid(-(z_ref[...] + b_ref[...]))
        dz_ref[...] = dz
        db_ref[...] += jnp.sum(dz, axis=1, keepdims=True)

    rev = lambda i: (0, nb - 1 - i)
    return pl.pallas_call(
        body, name=name, grid=(nb,),
        in_specs=[pl.BlockSpec((H, tb), rev), pl.BlockSpec((H, tb), rev), pl.BlockSpec((H, tb), rev),
                  pl.BlockSpec((H, 1), lambda i: (0, 0))],
        out_specs=[pl.BlockSpec((H, tb), rev), pl.BlockSpec((H, LANES), lambda i: (0, 0))],
        out_shape=[jax.ShapeDtypeStruct((H, S), F32), jax.ShapeDtypeStruct((H, LANES), F32)],
        scratch_shapes=[pltpu.VMEM((H, LANES), F32)],
        compiler_params=_cparams("arbitrary"),
    )(drow, dcol, z_t, bias_col)


def _fox_blocks(S):
    cap = max(LANES, S // 4)
    return (min(FOX_FWD_BLOCKS[0], cap), min(FOX_FWD_BLOCKS[1], cap)), \
           (min(FOX_BWD_BLOCKS[0], cap), min(FOX_BWD_BLOCKS[1], cap))


def _key_bias_blocks(negc, bk):
    H, S = negc.shape
    return negc.reshape(H // 2, 2, S // bk, bk).transpose(0, 2, 1, 3)


def _fox_fwd(proj, negc4, *, q_off, k_off, v_off, bq, bk, name):
    S = proj.shape[0]
    nq, nk = S // bq, S // bk
    npair = FOX_HEADS // 2
    assert bq % bk == 0 or bk % bq == 0
    nmask = max(1, bq // bk)

    gp = FOX_FWD_PAIRS
    gw = gp * LANES
    assert q_off % gw == 0 and k_off % gw == 0 and v_off % gw == 0 and npair % gp == 0

    def body(q_ref, k_ref, v_ref, nc_ref, o_ref, lse_ref):
        i = pl.program_id(1)
        lane = lax.broadcasted_iota(jnp.int32, (1, LANES), 1)
        half = [lane < HEAD_DIM, lane >= HEAD_DIM]
        qh = []
        for g in range(gp):
            q2 = q_ref[:, g * LANES:(g + 1) * LANES] * jnp.asarray(ATT_SCALE, BF16)
            qh += [jnp.where(half[h], q2, jnp.zeros_like(q2)) for h in range(2)]
        row = lax.broadcasted_iota(jnp.int32, (bq, bk), 0)
        col = lax.broadcasted_iota(jnp.int32, (bq, bk), 1)
        rel = row - col
        nfull = (i * bq) // bk

        spare = [HEAD_DIM, 0]
        ones_lane = [lane == spare[h] for h in range(2)]

        def step(j, carry, masked):
            start = pl.multiple_of(j * bk, bk)
            new = []
            for g in range(gp):
                ks = k_ref[pl.ds(start, bk), g * LANES:(g + 1) * LANES]
                vs = v_ref[pl.ds(start, bk), g * LANES:(g + 1) * LANES]
                nb = nc_ref[g, j]
                for h in range(2):
                    m, acc = carry[4 * g + 2 * h:4 * g + 2 * h + 2]
                    vh = jnp.where(half[h], vs, jnp.where(ones_lane[h], jnp.ones_like(vs), jnp.zeros_like(vs)))
                    s = _dot_nt(qh[2 * g + h], ks) + nb[h:h + 1, :]
                    if masked:
                        s = jnp.where(rel >= j * bk - i * bq, s, NEG)
                    m_new = jnp.maximum(m, jnp.max(s, axis=1, keepdims=True))
                    p = jnp.exp(s - m_new).astype(BF16)
                    acc = jnp.exp(m - m_new) * acc + _dot_nn(p, vh)
                    new += [m_new, acc]
            return tuple(new)

        init = (jnp.full((bq, 1), NEG, F32), jnp.zeros((bq, LANES), F32)) * (2 * gp)
        carry = lax.fori_loop(0, nfull, lambda j, c: step(j, c, False), init)
        for t in range(nmask):
            carry = step(nfull + t, carry, True)
        for g in range(gp):
            outs, lses = [], []
            for h in range(2):
                m, acc = carry[4 * g + 2 * h:4 * g + 2 * h + 2]
                l = acc[:, spare[h]:spare[h] + 1]
                outs.append(acc * (1.0 / l))
                lses.append(m + jnp.log(l))
            o_ref[:, g * LANES:(g + 1) * LANES] = jnp.where(half[0], outs[0], outs[1]).astype(BF16)
            lse_ref[g] = jnp.where(half[0], lses[0], lses[1])

    seq = lambda off: pl.BlockSpec((S, gw), lambda hp, i: (0, off // gw + hp))
    return pl.pallas_call(
        body, name=name, grid=(npair // gp, nq),
        in_specs=[pl.BlockSpec((bq, gw), lambda hp, i: (i, q_off // gw + hp)), seq(k_off), seq(v_off),
                  pl.BlockSpec((gp, nk, 2, bk), lambda hp, i: (hp, 0, 0, 0))],
        out_specs=[pl.BlockSpec((bq, gw), lambda hp, i: (i, hp)),
                   pl.BlockSpec((gp, bq, LANES), lambda hp, i: (hp, i, 0))],
        out_shape=[jax.ShapeDtypeStruct((S, FOX_W), BF16), jax.ShapeDtypeStruct((npair, S, LANES), F32)],
        compiler_params=_cparams("parallel", "parallel"),
    )(proj, proj, proj, negc4)


def _fox_bwd(proj, negc4, o, lse, d_o, q_t, do_t, *, q_off, k_off, v_off, bq, bk, name, deps=()):
    S = proj.shape[0]
    nq, nk = S // bq, S // bk
    npair = FOX_HEADS // 2
    assert bq % bk == 0 or bk % bq == 0
    nmask = max(1, bk // bq)

    def body(q_ref, k_ref, v_ref, nc_ref, o_ref, lse_ref, do_ref, qt_ref, dot_ref, *rest):
        dqo_ref, dk_ref, dv_ref, dn_ref, dr_ref, delta_ref, rs_ref, dq_ref = rest[len(deps):]
        j = pl.program_id(1)
        lane = lax.broadcasted_iota(jnp.int32, (1, LANES), 1)
        half = [lane < HEAD_DIM, lane >= HEAD_DIM]
        spare = [HEAD_DIM, 0]
        ones_lane = [lane == spare[h] for h in range(2)]
        srow = lax.broadcasted_iota(jnp.int32, (LANES, 1), 0)
        rhalf = [srow < HEAD_DIM, srow >= HEAD_DIM]
        ones_row = [srow == spare[h] for h in range(2)]
        k2, v2 = k_ref[...], v_ref[...]
        one_k = jnp.ones_like(k2)
        kh = [jnp.where(half[h], k2, jnp.where(ones_lane[h], one_k, jnp.zeros_like(k2))) for h in range(2)]
        nb = nc_ref[0, 0]
        row = lax.broadcasted_iota(jnp.int32, (bq, bk), 0)
        col = lax.broadcasted_iota(jnp.int32, (bq, bk), 1)
        rel = row - col
        i_first = (j * bk) // bq

        @pl.when(j == 0)
        def _():
            dq_ref[...] = jnp.zeros_like(dq_ref)
            rs_ref[...] = jnp.zeros_like(rs_ref)
            for b in range(nq):
                prod = do_ref[b * bq:(b + 1) * bq, :].astype(F32) * o_ref[b * bq:(b + 1) * bq, :].astype(F32)
                d0 = jnp.sum(jnp.where(half[0], prod, 0.0), axis=1, keepdims=True)
                d1 = jnp.sum(jnp.where(half[1], prod, 0.0), axis=1, keepdims=True)
                delta_ref[b * bq:(b + 1) * bq, :] = jnp.where(half[0], d0, d1)

        def step(i, carry, masked):
            dkt_a, dkt_b, dvt = carry
            dkts = [dkt_a, dkt_b]
            start = pl.multiple_of(i * bq, bq)
            q2 = q_ref[pl.ds(start, bq), :] * jnp.asarray(ATT_SCALE, BF16)
            do2 = do_ref[pl.ds(start, bq), :]
            qt = qt_ref[i] * jnp.asarray(ATT_SCALE, BF16)
            dot = dot_ref[i]
            lse2 = lse_ref[0, pl.ds(start, bq), :]
            del2 = delta_ref[pl.ds(start, bq), :]
            dqf = []
            for h in range(2):
                qm = jnp.where(half[h], q2, jnp.zeros_like(q2))
                dom = jnp.where(half[h], do2, jnp.zeros_like(do2))
                qtm = jnp.where(rhalf[h], qt, jnp.where(ones_row[h], jnp.ones_like(qt), jnp.zeros_like(qt)))
                dotm = jnp.where(rhalf[h], dot, jnp.zeros_like(dot))
                c0 = h * HEAD_DIM
                p = jnp.exp(_dot_nt(qm, k2) + nb[h:h + 1, :] - lse2[:, c0:c0 + 1])
                if masked:
                    p = jnp.where(rel >= j * bk - i * bq, p, 0.0)
                dp = _dot_nt(dom, v2)
                dsb = (p * (dp - del2[:, c0:c0 + 1])).astype(BF16)
                dvt = dvt + _dot_nn(dotm, p.astype(BF16))
                dkts[h] = dkts[h] + _dot_nn(qtm, dsb)
                dqf.append(_dot_nn(dsb, kh[h]))
            dq_ref[pl.ds(start, bq), :] += jnp.where(half[0], dqf[0], dqf[1]) * ATT_SCALE
            rs_ref[pl.ds(start, bq), :] += jnp.where(ones_lane[0], dqf[0], jnp.where(ones_lane[1], dqf[1], 0.0))
            return dkts[0], dkts[1], dvt

        zero = jnp.zeros((LANES, bk), F32)
        carry = (zero, zero, zero)
        for t in range(nmask):
            carry = step(i_first + t, carry, True)
        dkt_a, dkt_b, dvt = lax.fori_loop(i_first + nmask, nq, lambda i, c: step(i, c, False), carry)
        dk_ref[...] = jnp.where(rhalf[0], dkt_a, dkt_b).T.astype(BF16)
        dv_ref[...] = dvt.T.astype(BF16)
        dn_ref[0, 0] = jnp.concatenate([dkt_a[spare[0]:spare[0] + 1], dkt_b[spare[1]:spare[1] + 1]], axis=0)

        @pl.when(j == nk - 1)
        def _():
            dqo_ref[...] = dq_ref[...].astype(BF16)
            for b in range(nq):
                t = rs_ref[b * bq:(b + 1) * bq, :].T
                dr_ref[0, b] = jnp.concatenate([t[spare[0]:spare[0] + 1], t[spare[1]:spare[1] + 1]], axis=0)

    once = pl.Buffered(1)
    seq = lambda off: pl.BlockSpec((S, LANES), lambda hp, j: (0, off // LANES + hp), pipeline_mode=once)
    blk = lambda off: pl.BlockSpec((bk, LANES), lambda hp, j: (j, off // LANES + hp))
    nc = pl.BlockSpec((1, 1, 2, bk), lambda hp, j: (hp, j, 0, 0))
    tsp = pl.BlockSpec((nq, LANES, bq), lambda hp, j: (0, hp, 0), pipeline_mode=once)
    return pl.pallas_call(
        body, name=name, grid=(npair, nk),
        in_specs=[seq(q_off), blk(k_off), blk(v_off), nc, seq(0),
                  pl.BlockSpec((1, S, LANES), lambda hp, j: (hp, 0, 0), pipeline_mode=once), seq(0),
                  tsp, tsp] + [_ANY] * len(deps),
        out_specs=[pl.BlockSpec((S, LANES), lambda hp, j: (0, hp)), blk(0), blk(0), nc,
                   pl.BlockSpec((1, nq, 2, bq), lambda hp, j: (hp, 0, 0, 0))],
        out_shape=[jax.ShapeDtypeStruct((S, FOX_W), BF16), jax.ShapeDtypeStruct((S, FOX_W), BF16),
                   jax.ShapeDtypeStruct((S, FOX_W), BF16), jax.ShapeDtypeStruct((npair, nk, 2, bk), F32),
                   jax.ShapeDtypeStruct((npair, nq, 2, bq), F32)],
        scratch_shapes=[pltpu.VMEM((S, LANES), F32), pltpu.VMEM((S, LANES), F32), pltpu.VMEM((S, LANES), F32)],
        compiler_params=_cparams("parallel", "arbitrary"),
    )(proj, proj, proj, negc4, o, lse, d_o, q_t, do_t, *deps)


def _exchange(arrs, *, gather, name):
    n = len(arrs)
    npeer = N_DEV - 1

    def body(*refs):
        ins, outs = refs[:n], refs[n:2 * n]
        send_sems, recv_sems, loc_sems = refs[2 * n:]
        x, y, c = lax.axis_index("x"), lax.axis_index("y"), lax.axis_index("c")
        me = 4 * x + 2 * y + c
        peers = []
        for k in range(1, N_DEV):
            px = 1 - x if k & 4 else x
            py = 1 - y if k & 2 else y
            pc = 1 - c if k & 1 else c
            peers.append(((px, py, pc), 4 * px + 2 * py + pc))

        def remote(w, k):
            dev, idx = peers[k]
            src = ins[w] if gather else ins[w].at[idx]
            return pltpu.make_async_remote_copy(
                src_ref=src, dst_ref=outs[w].at[me],
                send_sem=send_sems.at[w * npeer + k], recv_sem=recv_sems.at[w * npeer + k],
                device_id=dev, device_id_type=pl.DeviceIdType.MESH)

        def arrival(w, k):
            dev, idx = peers[k]
            src = ins[w] if gather else ins[w].at[idx]
            return pltpu.make_async_remote_copy(
                src_ref=src, dst_ref=outs[w].at[idx],
                send_sem=send_sems.at[w * npeer + k], recv_sem=recv_sems.at[w * npeer + k],
                device_id=dev, device_id_type=pl.DeviceIdType.MESH)

        local = []
        for w in range(n):
            for k in range(npeer):
                remote(w, k).start()
            cp = pltpu.make_async_copy(ins[w] if gather else ins[w].at[me], outs[w].at[me], loc_sems.at[w])
            cp.start()
            local.append(cp)
        for w in range(n):
            for k in range(npeer):
                arrival(w, k).wait_recv()
        for w in range(n):
            for k in range(npeer):
                remote(w, k).wait_send()
            local[w].wait()

    hbm = pl.BlockSpec(memory_space=pl.ANY)
    out_shape = [jax.ShapeDtypeStruct((N_DEV,) + (a.shape if gather else a.shape[1:]), a.dtype) for a in arrs]
    return pl.pallas_call(
        body, name=name,
        in_specs=[hbm] * n, out_specs=[hbm] * n, out_shape=out_shape,
        scratch_shapes=[pltpu.SemaphoreType.DMA((n * npeer,)), pltpu.SemaphoreType.DMA((n * npeer,)),
                        pltpu.SemaphoreType.DMA((n,))],
        compiler_params=pltpu.CompilerParams(has_side_effects=True),
    )(*arrs)


def _gather_two_level(shard, *, name):
    def body(x_ref, out_ref, send_sems, recv_sems, local_sem):
        x, y, c = lax.axis_index("x"), lax.axis_index("y"), lax.axis_index("c")
        me, sibling = (x, y, c), (x, y, 1 - c)
        chips = [(1 - x, y), (x, 1 - y), (1 - x, 1 - y)]

        def slot(px, py, pc):
            return out_ref.at[4 * px + 2 * py + pc]

        def copy(k, block, to, src=None):
            return pltpu.make_async_remote_copy(
                src_ref=slot(*block) if src is None else src, dst_ref=slot(*block),
                send_sem=send_sems.at[k], recv_sem=recv_sems.at[k],
                device_id=to, device_id_type=pl.DeviceIdType.MESH)

        mine = pltpu.make_async_copy(x_ref, slot(*me), local_sem)
        mine.start()
        first = [copy(0, me, sibling, src=x_ref)]
        first += [copy(1 + j, me, (*chip, c), src=x_ref) for j, chip in enumerate(chips)]
        for cp in first:
            cp.start()
        passed = [copy(4 + j, (*chip, c), sibling) for j, chip in enumerate(chips)]
        for j, chip in enumerate(chips):
            copy(1 + j, (*chip, c), me).wait_recv()
            passed[j].start()
        copy(0, sibling, me).wait_recv()
        for j, chip in enumerate(chips):
            copy(4 + j, (*chip, 1 - c), me).wait_recv()
        for cp in first + passed:
            cp.wait_send()
        mine.wait()

    return pl.pallas_call(
        body, name=name,
        in_specs=[_ANY], out_specs=_ANY,
        out_shape=jax.ShapeDtypeStruct((N_DEV,) + shard.shape, shard.dtype),
        scratch_shapes=[pltpu.SemaphoreType.DMA((N_DEV - 1,)), pltpu.SemaphoreType.DMA((N_DEV - 1,)),
                        pltpu.SemaphoreType.DMA],
        compiler_params=pltpu.CompilerParams(has_side_effects=True),
    )(shard)


_HBM = pl.BlockSpec(memory_space=pltpu.HBM)
_SEM = pl.BlockSpec(memory_space=pltpu.SEMAPHORE)
_EFFECT = pltpu.SideEffectType.DATAFLOW_SIDE_EFFECTING
NPEER = N_DEV - 1


def _peer_table():
    x, y, c = lax.axis_index("x"), lax.axis_index("y"), lax.axis_index("c")
    peers = []
    for k in range(1, N_DEV):
        px = 1 - x if k & 4 else x
        py = 1 - y if k & 2 else y
        pc = 1 - c if k & 1 else c
        peers.append(((px, py, pc), 4 * px + 2 * py + pc))
    return 4 * x + 2 * y + c, peers


def _split_copy(ins, lands, send_sems, recv_sems, gather, me, peers, w, k, arriving):
    dev, idx = peers[k]
    return pltpu.make_async_remote_copy(
        src_ref=ins[w] if gather else ins[w].at[idx],
        dst_ref=lands[w].at[idx if arriving else me],
        send_sem=send_sems.at[w * NPEER + k], recv_sem=recv_sems.at[w * NPEER + k],
        device_id=dev, device_id_type=pl.DeviceIdType.MESH)


def _exchange_start(arrs, *, gather, name, deps=()):
    n = len(arrs)
    land_shapes = [(N_DEV,) + (a.shape if gather else a.shape[1:]) for a in arrs]

    def body(*refs):
        ins, lands = refs[:n], refs[n:2 * n]
        send_sems, recv_sems = refs[2 * n + len(deps)], refs[2 * n + len(deps) + 1]
        token = refs[-1]
        me, peers = _peer_table()
        for w in range(n):
            for k in range(NPEER):
                _split_copy(ins, lands, send_sems, recv_sems, gather, me, peers, w, k, False).start()
        token[...] = jnp.zeros_like(token)

    out_shape = ([pltpu.SemaphoreType.DMA((n * NPEER,)), pltpu.SemaphoreType.DMA((n * NPEER,))]
                 + [pltpu.HBM(a.shape, a.dtype) for a in arrs]
                 + [pltpu.HBM(s, a.dtype) for s, a in zip(land_shapes, arrs)]
                 + [jax.ShapeDtypeStruct((8, LANES), F32)])
    res = pl.pallas_call(
        body, name=name,
        in_specs=[_HBM] * (2 * n) + [_ANY] * len(deps),
        out_specs=[_SEM, _SEM] + [_HBM] * (2 * n) + [pl.BlockSpec(memory_space=pltpu.VMEM)],
        out_shape=out_shape,
        input_output_aliases={i: 2 + i for i in range(2 * n)},
        compiler_params=pltpu.CompilerParams(has_side_effects=_EFFECT),
    )(*[pltpu.with_memory_space_constraint(a, pltpu.HBM) for a in arrs],
      *[pltpu.with_memory_space_constraint(lax.empty(s, a.dtype), pltpu.HBM) for s, a in zip(land_shapes, arrs)],
      *deps)
    return (n, gather, res[0], res[1], res[2:2 + n], res[2 + n:2 + 2 * n]), res[-1]


def _exchange_wait(handle, after, *, name):
    n, gather, send_sems, recv_sems, ins_thru, lands_thru = handle

    def body(*refs):
        ins, lands = refs[:n], refs[n:2 * n]
        send_s, recv_s = refs[2 * n], refs[2 * n + 1]
        me, peers = _peer_table()
        for w in range(n):
            for k in range(NPEER):
                _split_copy(ins, lands, send_s, recv_s, gather, me, peers, w, k, False).wait_send()
                _split_copy(ins, lands, send_s, recv_s, gather, me, peers, w, k, True).wait_recv()

    res = pl.pallas_call(
        body, name=name,
        in_specs=[_HBM] * (2 * n) + [_SEM, _SEM, pl.BlockSpec(memory_space=pl.ANY)],
        out_specs=[_HBM] * (2 * n),
        out_shape=[pltpu.HBM(a.shape, a.dtype) for a in list(ins_thru) + list(lands_thru)],
        input_output_aliases={i: i for i in range(2 * n)},
        compiler_params=pltpu.CompilerParams(has_side_effects=_EFFECT),
    )(*ins_thru, *lands_thru, send_sems, recv_sems, after)
    return res[:n], res[n:2 * n]


def _ordered_sum(s_ref, own_ref):
    if own_ref is None:
        blocks = [s_ref[q].astype(F32) for q in range(N_DEV)]
    else:
        me = 4 * lax.axis_index("x") + 2 * lax.axis_index("y") + lax.axis_index("c")
        own = own_ref[...]
        blocks = [jnp.where(me == q, own, s_ref[q]).astype(F32) for q in range(N_DEV)]
    acc = blocks[0]
    for b in blocks[1:]:
        acc = acc + b
    return acc


def _sum8(stack, own, *, name):
    _, R, C = stack.shape
    if R % 8 == 0:
        tr, tc = _pick(R, max(8, STEP_BYTES // (C * 4 * (N_DEV + 2))), 8), C
    else:
        tr, tc = R, _pick(C, max(LANES, STEP_BYTES // (R * 4 * (N_DEV + 2))))

    def body(s_ref, own_ref, o_ref):
        o_ref[...] = _ordered_sum(s_ref, own_ref)

    blk = pl.BlockSpec((tr, tc), lambda i, j: (i, j))
    return pl.pallas_call(
        body, name=name, grid=(R // tr, C // tc),
        in_specs=[pl.BlockSpec((N_DEV, tr, tc), lambda i, j: (0, i, j)), blk],
        out_specs=blk,
        out_shape=jax.ShapeDtypeStruct((R, C), F32),
        compiler_params=_cparams("parallel", "parallel"),
    )(stack, own)


def _adamw_math(w, g, m, v):
    m = ADAM_B1 * m + (1.0 - ADAM_B1) * g
    v = ADAM_B2 * v + (1.0 - ADAM_B2) * (g * g)
    m_hat = m / (1.0 - ADAM_B1 ** ADAM_STEP)
    v_hat = v / (1.0 - ADAM_B2 ** ADAM_STEP)
    delta = -ADAM_LR * (m_hat / (jnp.sqrt(v_hat) + ADAM_EPS) + ADAM_WD * w)
    return delta, m, v


def _adamw(w, g, m, v, *, name, stacked, own=None):
    R, C = w.shape
    tr = _pick(R, max(8, STEP_BYTES // (C * 4 * (8 + (N_DEV if stacked else 1)))), 8)
    has_own = own is not None

    def body(w_ref, g_ref, m_ref, v_ref, *rest):
        go_ref, d_ref, mo_ref, vo_ref = rest[-4:]
        g = _ordered_sum(g_ref, rest[0] if has_own else None) if stacked else g_ref[...]
        delta, m2, v2 = _adamw_math(w_ref[...], g, m_ref[...], v_ref[...])
        go_ref[...] = g
        d_ref[...] = delta
        mo_ref[...] = m2
        vo_ref[...] = v2

    row = pl.BlockSpec((tr, C), lambda i: (i, 0))
    g_spec = pl.BlockSpec((N_DEV, tr, C), lambda i: (0, i, 0)) if stacked else row
    return pl.pallas_call(
        body, name=name, grid=(R // tr,),
        in_specs=[row, g_spec, row, row] + [row] * has_own, out_specs=[row] * 4,
        out_shape=[jax.ShapeDtypeStruct((R, C), F32)] * 4,
        compiler_params=_cparams("parallel"),
    )(w, g, m, v, *([own] if has_own else []))


def kernel(x, positions, attn_norm, w_in, fox_f_bias, swa_sinks, w_branch_swa, w_branch_fox, w_out, mlp_norm, w_up, w_down, final_norm, loss_target, m_attn_norm, m_w_in, m_fox_f_bias, m_swa_sinks, m_w_branch_swa, m_w_branch_fox, m_w_out, m_mlp_norm, m_w_up, m_w_down, m_final_norm, v_attn_norm, v_w_in, v_fox_f_bias, v_swa_sinks, v_w_branch_swa, v_w_branch_fox, v_w_out, v_mlp_norm, v_w_up, v_w_down, v_final_norm):
    S, D = x.shape[1], x.shape[2]
    DFF = w_up.shape[2] * N_DEV
    d_in = w_in.shape[2] * N_DEV
    assert d_in == QKV_W + FOX_HEADS + 2 * D and (2 * D) % SWA_Q_W == 0 and S % (4 * LANES) == 0
    q_off = 2 * D
    k_off = q_off + SWA_Q_W
    v_off = k_off + SWA_KV_W
    fq_off = v_off + SWA_KV_W
    fk_off = fq_off + FOX_W
    fv_off = fk_off + FOX_W
    fl_off = fv_off + FOX_W
    NP = fl_off + FL_PAD
    x2d, tgt = x[0], loss_target[0]

    shards = [w_in[0].T.astype(BF16), w_branch_swa[0].T.astype(BF16), w_branch_fox[0].T.astype(BF16),
              w_out[0].astype(BF16), w_up[0].T.astype(BF16), w_down[0].astype(BF16)]
    me = 4 * lax.axis_index("x") + 2 * lax.axis_index("y") + lax.axis_index("c")

    def filled(stack, own):
        return lax.dynamic_update_slice(stack, own[None], (me,) + (0,) * own.ndim)

    g_in = _gather_two_level(shards[0], name="gather_w_in")
    h_rest, tok_rest = _exchange_start(shards[1:], gather=True, name="gather_rest_start", deps=[g_in])

    tm = _pick(S, 1024)
    td = _pick(D, 1024)
    tf = _pick(DFF, 1024)
    tnp = _pick(NP, 1024)

    h1 = _rms_fwd(x2d, attn_norm, name="rms1", deps=[tok_rest])
    w_in_t = g_in.reshape(d_in, D)
    w_in_p = jnp.concatenate([w_in_t[QKV_W + FOX_HEADS:], w_in_t[:QKV_W], w_in_t[QKV_W:QKV_W + FOX_HEADS],
                              jnp.zeros((FL_PAD - FOX_HEADS, D), BF16)], axis=0)
    w_fl_t = w_in_t[QKV_W:QKV_W + FOX_HEADS]
    proj, = _matmul(h1, w_in_p, mode="nt", name="mm_in", out_dtypes=[BF16], tm=_pick(S, 2048), tn=tnp, tk=D)
    z_t, = _matmul(w_fl_t, h1, mode="nt", name="mm_flogit", out_dtypes=[F32],
                   tm=FOX_HEADS, tn=_pick(S, 2048), tk=D)
    bias_col = fox_f_bias.reshape(FOX_HEADS, 1)
    negc = _fox_prep(z_t, bias_col, name="fox_prep")
    (fbq, fbk), (bbq, bbk) = _fox_blocks(S)
    inv_freq = ROPE_THETA ** (-jnp.arange(0, HEAD_DIM, 2, dtype=F32) / HEAD_DIM)
    invf = jnp.tile(inv_freq, LANES // (HEAD_DIM // 2)).reshape(1, LANES)
    cos_t, sin_t = _rope_tables(positions.reshape(S, 1), invf, name="rope_tables")
    q_rope, k_rope = _rope_fwd(proj, cos_t, sin_t, q_off=q_off, k_off=k_off, name="rope_fwd")
    sinks = swa_sinks.reshape(-1)
    o_a = _swa_fwd(q_rope, k_rope, proj, sinks, v_off=v_off, name="swa_fwd")
    o_b, lse = _fox_fwd(proj, _key_bias_blocks(negc, fbk), q_off=fq_off, k_off=fk_off, v_off=fv_off,
                        bq=fbq, bk=fbk, name="fox_fwd")
    s_rest, g_rest = _exchange_wait(h_rest, o_b, name="gather_rest_wait")
    g_bs, g_bf, g_o, g_up, g_dn = [filled(g, s) for g, s in zip(g_rest, s_rest)]
    w_bs_t = g_bs.reshape(D, SWA_Q_W)
    w_bf_t = g_bf.reshape(D, FOX_W)
    w_o = g_o.reshape(D, D)
    w_up_t = g_up.reshape(DFF, D)
    w_dn = g_dn.reshape(DFF, D)
    ya, = _matmul(o_a, w_bs_t, mode="nt", name="mm_branch_swa", out_dtypes=[BF16], tm=tm, tn=td, tk=SWA_Q_W)
    gate_maps = [lambda i, j, k: (i, j), lambda i, j, k: (i, j), lambda i, j, k: (i, j + D // td)]

    def merge_epi(acc, ya_t, ga_t, gb_t):
        merged = _sigmoid(ga_t.astype(F32)) * ya_t.astype(F32) + _sigmoid(gb_t.astype(F32)) * acc
        return acc, merged

    yb, merged = _matmul(o_b, w_bf_t, mode="nt", name="mm_branch_fox", out_dtypes=[BF16, BF16],
                         tm=tm, tn=td, tk=FOX_W, extras=[ya, proj, proj], extra_maps=gate_maps,
                         epilogue=merge_epi)
    x_mid, = _matmul(merged, w_o, mode="nn", name="mm_out", out_dtypes=[F32], tm=tm, tn=td, tk=D,
                     extras=[x2d], epilogue=lambda acc, r: (acc + r,))
    h2 = _rms_fwd(x_mid, mlp_norm, name="rms2")
    u, = _matmul(h2, w_up_t, mode="nt", name="mm_up", out_dtypes=[BF16], tm=tm, tn=tf, tk=D,
                 epilogue=lambda acc: (jnp.maximum(acc, 0.0),))
    x_fin, = _matmul(u, w_dn, mode="nn", name="mm_down", out_dtypes=[F32], tm=tm, tn=td, tk=_pick(DFF, 2048),
                     a_fn=_square_bf16, extras=[x_mid], epilogue=lambda acc, r: (acc + r,))

    dx3, dx3b, dg3, loss_part = _loss_head(x_fin, tgt, final_norm.reshape(1, D), name="loss_head")
    d_up, = _matmul(dx3b, w_dn, mode="nt", name="mm_d_act", out_dtypes=[BF16], tm=tm, tn=tf, tk=D,
                    extras=[u], epilogue=lambda acc, ut: (acc * (2.0 * ut.astype(F32)),))
    tks = _pick(S, 2048)
    dw_dn, = _matmul(u, dx3b, mode="tn", name="mm_dw_down", out_dtypes=[F32], tm=tf, tn=td, tk=tks,
                     a_fn=_square_bf16)
    dh2, = _matmul(d_up, w_up_t, mode="nn", name="mm_dh2", out_dtypes=[F32], tm=tm, tn=td, tk=_pick(DFF, 2048))
    dw_up_t, = _matmul(d_up, h2, mode="tn", name="mm_dw_up", out_dtypes=[F32], tm=tf, tn=td, tk=tks)
    h_s1, tok_s1 = _exchange_start([dw_up_t.reshape(N_DEV, DFF // N_DEV, D), dw_dn.reshape(N_DEV, DFF // N_DEV, D)],
                                   gather=False, name="scatter_mlp_start")
    dx2, dx2b, dg2 = _rms_bwd(dh2, x_mid, mlp_norm, dx3, name="rms2_bwd", want_bf16=True, deps=[tok_s1])

    def gate_bwd_epi(dm, ya_t, yb_t, ga_t, gb_t):
        sa, sb = _sigmoid(ga_t.astype(F32)), _sigmoid(gb_t.astype(F32))
        return (dm * sa, dm * sb, dm * ya_t.astype(F32) * sa * (1.0 - sa), dm * yb_t.astype(F32) * sb * (1.0 - sb))

    gmaps = [lambda i, j, k: (i, j), lambda i, j, k: (i, j), lambda i, j, k: (i, j),
             lambda i, j, k: (i, j + D // td)]
    d_ya, d_yb, d_ga, d_gb = _matmul(dx2b, w_o, mode="nt", name="mm_d_merged", out_dtypes=[BF16] * 4,
                                     tm=tm, tn=td, tk=D, extras=[ya, yb, proj, proj], extra_maps=gmaps,
                                     epilogue=gate_bwd_epi)
    dw_o, = _matmul(merged, dx2b, mode="tn", name="mm_dw_out", out_dtypes=[F32], tm=td, tn=td, tk=tks)
    d_oa, = _matmul(d_ya, w_bs_t, mode="nn", name="mm_d_oa", out_dtypes=[BF16], tm=tm, tn=SWA_Q_W, tk=D)
    d_ob, = _matmul(d_yb, w_bf_t, mode="nn", name="mm_d_ob", out_dtypes=[BF16], tm=tm, tn=FOX_W, tk=D)
    dw_bs_t, = _matmul(d_ya, o_a, mode="tn", name="mm_dw_bs", out_dtypes=[F32], tm=td, tn=SWA_Q_W, tk=tks)
    dw_bf_t, = _matmul(d_yb, o_b, mode="tn", name="mm_dw_bf", out_dtypes=[F32], tm=td, tn=FOX_W, tk=tks)
    h_s2, tok_s2 = _exchange_start([dw_bs_t.reshape(N_DEV, D // N_DEV, SWA_Q_W),
                                    dw_bf_t.reshape(N_DEV, D // N_DEV, FOX_W), dw_o.reshape(N_DEV, D // N_DEV, D)],
                                   gather=False, name="scatter_attn_start")
    def row_blocks_t(a):
        return a.reshape(S // bbq, bbq, FOX_W).transpose(0, 2, 1)

    d_fq, d_fk, d_fv, dcol4, drow4 = _fox_bwd(proj, _key_bias_blocks(negc, bbk), o_b, lse, d_ob,
                                              row_blocks_t(proj[:, fq_off:fq_off + FOX_W]), row_blocks_t(d_ob),
                                              q_off=fq_off, k_off=fk_off, v_off=fv_off, bq=bbq, bk=bbk,
                                              name="fox_bwd", deps=[tok_s2])
    dcol = dcol4.transpose(0, 2, 1, 3).reshape(FOX_HEADS, S)
    drow = drow4.transpose(0, 2, 1, 3).reshape(FOX_HEADS, S)
    dz_t, dbias_l = _fox_post(drow, dcol, z_t, bias_col, name="fox_post")
    dq_r, dk_c, dk_p, dv_c, dv_p, dsink_l = _swa_bwd(q_rope, k_rope, proj, sinks, d_oa, v_off=v_off, name="swa_bwd")
    d_aq, d_ak, d_av = _rope_bwd(dq_r, dk_c, dk_p, dv_c, dv_p, cos_t, sin_t, name="rope_bwd")
    dz_pad = jnp.pad(dz_t.T.astype(BF16), ((0, 0), (0, FL_PAD - FOX_HEADS)))
    d_proj = jnp.concatenate([d_ga, d_gb, d_aq, d_ak, d_av, d_fq, d_fk, d_fv, dz_pad], axis=1)
    tkp = _pick(NP, 2304)
    dw_in_p, = _matmul(d_proj, h1, mode="tn", name="mm_dw_in", out_dtypes=[BF16], tm=_pick(NP, 512), tn=D, tk=tks)
    dw_in_t = jnp.concatenate([dw_in_p[q_off:q_off + QKV_W], dw_in_p[fl_off:fl_off + FOX_HEADS], dw_in_p[:q_off]],
                              axis=0)
    h_s3, tok_s3 = _exchange_start([dw_in_t.reshape(N_DEV, d_in // N_DEV, D)], gather=False,
                                   name="scatter_in_start")
    dh1, = _matmul(d_proj, w_in_p, mode="nn", name="mm_dh1", out_dtypes=[F32], tm=tm, tn=td, tk=tkp, deps=[tok_s3])
    dx, dg1 = _rms_bwd(dh1, x2d, attn_norm, dx2, name="rms1_bwd", want_bf16=False)

    dbias = dbias_l[:, 0]
    dsinks = dsink_l[:, :, 0].reshape(-1)
    nsm = 3 * D + 2 * LANES
    tail = jnp.zeros((2 * LANES,), F32)
    small_g = jnp.concatenate([dg1[0], dg2[0], dg3[0],
                               tail.at[0:16].set(dbias).at[16:32].set(dsinks).at[32].set(loss_part[0, 0])])

    def pack(a_norm, b_norm, f_norm, bias, snk):
        return jnp.concatenate([a_norm[0], b_norm[0], f_norm,
                                tail.at[0:16].set(bias[0]).at[16:32].set(snk[0])]).reshape(1, nsm)

    small_stack, = _exchange([small_g.reshape(1, nsm)], gather=True, name="gather_small")
    u_sm = _adamw(pack(attn_norm, mlp_norm, final_norm, fox_f_bias, swa_sinks), small_stack,
                  pack(m_attn_norm, m_mlp_norm, m_final_norm, m_fox_f_bias, m_swa_sinks),
                  pack(v_attn_norm, v_mlp_norm, v_final_norm, v_fox_f_bias, v_swa_sinks),
                  name="adamw_small", stacked=True)
    loss = u_sm[0][0, 3 * D + 32]

    def own_of(src):
        return lax.dynamic_index_in_dim(src, me, 0, keepdims=False)

    def update_t(stack, src, w, m, v, nm):
        g = _sum8(stack, own_of(src), name="sum_" + nm).T
        return _adamw(w[0], g, m[0], v[0], name="adamw_" + nm, stacked=False)

    def update(stack, src, w, m, v, nm):
        return _adamw(w[0], stack, m[0], v[0], name="adamw_" + nm, stacked=True, own=own_of(src))

    (s_up, s_dn), (r_up, r_dn) = _exchange_wait(h_s1, u_sm[1], name="scatter_mlp_wait")
    u_up = update_t(r_up, s_up, w_up, m_w_up, v_w_up, "w_up")
    u_dn = update(r_dn, s_dn, w_down, m_w_down, v_w_down, "w_down")
    (s_bs, s_bf, s_o), (r_bs, r_bf, r_o) = _exchange_wait(h_s2, u_dn[1], name="scatter_attn_wait")
    u_bs = update_t(r_bs, s_bs, w_branch_swa, m_w_branch_swa, v_w_branch_swa, "w_bs")
    u_bf = update_t(r_bf, s_bf, w_branch_fox, m_w_branch_fox, v_w_branch_fox, "w_bf")
    u_o = update(r_o, s_o, w_out, m_w_out, v_w_out, "w_out")
    (s_w_in,), (r_in,) = _exchange_wait(h_s3, u_o[1], name="scatter_in_wait")
    u_in = update_t(r_in, s_w_in, w_in, m_w_in, v_w_in, "w_in")

    def small(kind):
        a = u_sm[kind][0]
        return dict(attn_norm=a[0:D][None], mlp_norm=a[D:2 * D][None], final_norm=a[2 * D:3 * D],
                    fox_f_bias=a[3 * D:3 * D + 16][None], swa_sinks=a[3 * D + 16:3 * D + 32][None])

    big = dict(w_in=u_in, w_branch_swa=u_bs, w_branch_fox=u_bf, w_out=u_o, w_up=u_up, w_down=u_dn)
    order = ["attn_norm", "w_in", "fox_f_bias", "swa_sinks", "w_branch_swa", "w_branch_fox", "w_out", "mlp_norm",
             "w_up", "w_down", "final_norm"]
    outs = [loss, dx[None]]
    for kind in range(4):
        sm = small(kind)
        for nm in order:
            outs.append(big[nm][kind][None] if nm in big else sm[nm])
    return tuple(outs)
```

```python
import functools

import jax
import jax.numpy as jnp
from jax import lax
from jax.experimental import pallas as pl
from jax.experimental.pallas import tpu as pltpu

F32 = jnp.float32
BF16 = jnp.bfloat16

N_DEV = 8
HEAD_DIM = 64
SWA_Q_W = 1024
SWA_KV_W = 128
SWA_GROUP = 8
WINDOW = 128
FOX_W = 1024
FOX_HEADS = 16
QKV_W = SWA_Q_W + 2 * SWA_KV_W + 3 * FOX_W
FL_PAD = 256
ROPE_THETA = 10000.0
RMS_EPS = 1e-6
ATT_SCALE = 0.125
NEG = -1e30

ADAM_LR = 0.001
ADAM_B1 = 0.9
ADAM_B2 = 0.999
ADAM_EPS = 1e-08
ADAM_WD = 0.01
ADAM_STEP = 10

FOX_FWD_BLOCKS = (1024, 512)
FOX_BWD_BLOCKS = (1024, 512)
FOX_FWD_PAIRS = 2

LANES = 128
VMEM_LIMIT = 56 * 1024 * 1024
STEP_BYTES = 12 * 1024 * 1024


def _cparams(*sem):
    return pltpu.CompilerParams(dimension_semantics=sem, vmem_limit_bytes=VMEM_LIMIT)


def _pick(dim, pref, align=LANES):
    best = None
    t = align
    while t <= min(dim, pref):
        if dim % t == 0:
            best = t
        t += align
    return best if best is not None else dim


_DIMS = {"nn": ((1,), (0,)), "nt": ((1,), (1,)), "tn": ((0,), (0,))}


_ANY = pl.BlockSpec(memory_space=pl.ANY)


def _matmul(a, b, *, mode, name, out_dtypes, tm, tn, tk, extras=(), extra_maps=None,
            a_fn=None, epilogue=None, deps=()):
    if mode == "nn":
        (M, K), (K2, N) = a.shape, b.shape
    elif mode == "nt":
        (M, K), (N, K2) = a.shape, b.shape
    else:
        (K, M), (K2, N) = a.shape, b.shape
    assert K == K2, (name, a.shape, b.shape)
    assert M % tm == 0 and N % tn == 0 and K % tk == 0, (name, M, N, K, tm, tn, tk)
    nk = K // tk
    ne, no = len(extras), len(out_dtypes)
    dims = (_DIMS[mode], ((), ()))

    def body(*refs):
        a_ref, b_ref = refs[0], refs[1]
        ex_refs = refs[2:2 + ne]
        out_refs = refs[2 + ne + len(deps):2 + ne + len(deps) + no]

        def finish(acc):
            res = (acc,) if epilogue is None else epilogue(acc, *[e[...] for e in ex_refs])
            for o_ref, r in zip(out_refs, res):
                o_ref[...] = r.astype(o_ref.dtype)

        def product():
            av = a_ref[...]
            if a_fn is not None:
                av = a_fn(av)
            return lax.dot_general(av, b_ref[...], dims, preferred_element_type=F32)

        if nk == 1:
            finish(product())
        else:
            acc_ref = refs[-1]
            k = pl.program_id(2)

            @pl.when(k == 0)
            def _():
                acc_ref[...] = jnp.zeros_like(acc_ref)

            acc_ref[...] += product()

            @pl.when(k == nk - 1)
            def _():
                finish(acc_ref[...])

    if mode == "tn":
        a_spec = pl.BlockSpec((tk, tm), lambda i, j, k: (k, i))
    else:
        a_spec = pl.BlockSpec((tm, tk), lambda i, j, k: (i, k))
    if mode == "nt":
        b_spec = pl.BlockSpec((tn, tk), lambda i, j, k: (j, k))
    else:
        b_spec = pl.BlockSpec((tk, tn), lambda i, j, k: (k, j))
    if extra_maps is None:
        extra_maps = [lambda i, j, k: (i, j)] * ne
    ex_specs = [pl.BlockSpec((tm, tn), m) for m in extra_maps]
    out_spec = [pl.BlockSpec((tm, tn), lambda i, j, k: (i, j)) for _ in range(no)]
    res = pl.pallas_call(
        body,
        name=name,
        grid=(M // tm, N // tn, nk),
        in_specs=[a_spec, b_spec] + ex_specs + [_ANY] * len(deps),
        out_specs=out_spec,
        out_shape=[jax.ShapeDtypeStruct((M, N), d) for d in out_dtypes],
        scratch_shapes=[pltpu.VMEM((tm, tn), F32)] if nk > 1 else [],
        compiler_params=_cparams("parallel", "parallel", "arbitrary"),
    )(a, b, *extras, *deps)
    return res


def _square_bf16(t):
    tf = t.astype(F32)
    return (tf * tf).astype(BF16)


def _sigmoid(g):
    return 1.0 / (1.0 + jnp.exp(-g))


def _rms_fwd(x, gain, *, name, deps=()):
    S, D = x.shape
    tr = _pick(S, 512, 8)

    def body(x_ref, g_ref, *rest):
        h_ref = rest[-1]
        xv = x_ref[...]
        r = lax.rsqrt(jnp.mean(xv * xv, axis=-1, keepdims=True) + RMS_EPS)
        h_ref[...] = (xv * r * g_ref[...]).astype(BF16)

    return pl.pallas_call(
        body, name=name, grid=(S // tr,),
        in_specs=[pl.BlockSpec((tr, D), lambda i: (i, 0)), pl.BlockSpec((1, D), lambda i: (0, 0))] + [_ANY] * len(deps),
        out_specs=pl.BlockSpec((tr, D), lambda i: (i, 0)),
        out_shape=jax.ShapeDtypeStruct((S, D), BF16),
        compiler_params=_cparams("parallel"),
    )(x, gain, *deps)


def _rms_bwd(dh, x, gain, dres, *, name, want_bf16, deps=()):
    S, D = x.shape
    tr = _pick(S, 256, 8)

    def body(dh_ref, x_ref, g_ref, dres_ref, *rest):
        outs = rest[len(deps):]
        dx_ref, dg_ref = outs[0], outs[-1]
        xv = x_ref[...]
        r = lax.rsqrt(jnp.mean(xv * xv, axis=-1, keepdims=True) + RMS_EPS)
        xh = xv * r
        dhv = dh_ref[...]
        t = dhv * g_ref[...]
        dx = r * (t - xh * jnp.mean(t * xh, axis=-1, keepdims=True)) + dres_ref[...]
        dx_ref[...] = dx
        if want_bf16:
            outs[1][...] = dx.astype(BF16)
        part = jnp.sum(dhv * xh, axis=0, keepdims=True)

        @pl.when(pl.program_id(0) == 0)
        def _():
            dg_ref[...] = part

        @pl.when(pl.program_id(0) > 0)
        def _():
            dg_ref[...] += part

    row = pl.BlockSpec((tr, D), lambda i: (i, 0))
    vec = pl.BlockSpec((1, D), lambda i: (0, 0))
    out_shape = [jax.ShapeDtypeStruct((S, D), F32)]
    out_specs = [row]
    if want_bf16:
        out_shape.append(jax.ShapeDtypeStruct((S, D), BF16))
        out_specs.append(row)
    out_shape.append(jax.ShapeDtypeStruct((1, D), F32))
    out_specs.append(vec)
    return pl.pallas_call(
        body, name=name, grid=(S // tr,),
        in_specs=[row, row, vec, row] + [_ANY] * len(deps), out_specs=out_specs, out_shape=out_shape,
        compiler_params=_cparams("arbitrary"),
    )(dh, x, gain, dres, *deps)


def _loss_head(x3, target, gain, *, name):
    S, D = x3.shape
    tr = _pick(S, 256, 8)

    def body(x_ref, t_ref, g_ref, dx_ref, dxb_ref, dg_ref, loss_ref):
        xv = x_ref[...]
        r = lax.rsqrt(jnp.mean(xv * xv, axis=-1, keepdims=True) + RMS_EPS)
        xh = xv * r
        gv = g_ref[...]
        err = xh * gv - t_ref[...]
        lpart = jnp.zeros((1, LANES), F32) + (0.5 / D) * jnp.sum(err * err)
        dy = err * (1.0 / D)
        t = dy * gv
        dx = r * (t - xh * jnp.mean(t * xh, axis=-1, keepdims=True))
        dx_ref[...] = dx
        dxb_ref[...] = dx.astype(BF16)
        part = jnp.sum(dy * xh, axis=0, keepdims=True)

        @pl.when(pl.program_id(0) == 0)
        def _():
            dg_ref[...] = part
            loss_ref[...] = lpart

        @pl.when(pl.program_id(0) > 0)
        def _():
            dg_ref[...] += part
            loss_ref[...] += lpart

    row = pl.BlockSpec((tr, D), lambda i: (i, 0))
    vec = pl.BlockSpec((1, D), lambda i: (0, 0))
    return pl.pallas_call(
        body, name=name, grid=(S // tr,),
        in_specs=[row, row, vec],
        out_specs=[row, row, vec, pl.BlockSpec((1, LANES), lambda i: (0, 0))],
        out_shape=[jax.ShapeDtypeStruct((S, D), F32), jax.ShapeDtypeStruct((S, D), BF16),
                   jax.ShapeDtypeStruct((1, D), F32), jax.ShapeDtypeStruct((1, LANES), F32)],
        compiler_params=_cparams("arbitrary"),
    )(x3, target, gain)


def _rope_tables(pos_col, invf, *, name):
    S = pos_col.shape[0]
    tr = _pick(S, 512, 8)

    def body(p_ref, f_ref, cos_ref, sin_ref):
        ang = p_ref[...].astype(F32) * f_ref[...]
        lane = lax.broadcasted_iota(jnp.int32, (1, LANES), 1)
        first = (lane % HEAD_DIM) < HEAD_DIM // 2
        sn = jnp.sin(ang)
        cos_ref[...] = jnp.cos(ang)
        sin_ref[...] = jnp.where(first, -sn, sn)

    return pl.pallas_call(
        body, name=name, grid=(S // tr,),
        in_specs=[pl.BlockSpec((tr, 1), lambda i: (i, 0)), pl.BlockSpec((1, LANES), lambda i: (0, 0))],
        out_specs=[pl.BlockSpec((tr, LANES), lambda i: (i, 0))] * 2,
        out_shape=[jax.ShapeDtypeStruct((S, LANES), F32)] * 2,
        compiler_params=_cparams("parallel"),
    )(pos_col, invf)


def _swap_halves(t):
    lane = lax.broadcasted_iota(jnp.int32, (1, LANES), 1)
    first = (lane % HEAD_DIM) < HEAD_DIM // 2
    return jnp.where(first, pltpu.roll(t, LANES - HEAD_DIM // 2, 1), pltpu.roll(t, HEAD_DIM // 2, 1))


def _rope_fwd(proj, cos_t, sin_t, *, q_off, k_off, name):
    S = proj.shape[0]
    tr = _pick(S, 256, 8)
    nqb = SWA_Q_W // LANES

    def body(q_ref, k_ref, c_ref, s_ref, qo_ref, ko_ref):
        cv, sv = c_ref[...], s_ref[...]
        for b in range(nqb):
            t = q_ref[:, b * LANES:(b + 1) * LANES].astype(F32)
            qo_ref[:, b * LANES:(b + 1) * LANES] = (t * cv + _swap_halves(t) * sv).astype(BF16)
        t = k_ref[...].astype(F32)
        ko_ref[...] = (t * cv + _swap_halves(t) * sv).astype(BF16)

    tab = pl.BlockSpec((tr, LANES), lambda i: (i, 0))
    return pl.pallas_call(
        body, name=name, grid=(S // tr,),
        in_specs=[pl.BlockSpec((tr, SWA_Q_W), lambda i: (i, q_off // SWA_Q_W)),
                  pl.BlockSpec((tr, LANES), lambda i: (i, k_off // LANES)), tab, tab],
        out_specs=[pl.BlockSpec((tr, SWA_Q_W), lambda i: (i, 0)), tab],
        out_shape=[jax.ShapeDtypeStruct((S, SWA_Q_W), BF16), jax.ShapeDtypeStruct((S, LANES), BF16)],
        compiler_params=_cparams("parallel"),
    )(proj, proj, cos_t, sin_t)


def _rope_bwd(dq, dk_cur, dk_prev, dv_cur, dv_prev, cos_t, sin_t, *, name):
    S = dq.shape[0]
    tr = WINDOW
    nb = S // tr
    nqb = SWA_Q_W // LANES

    def body(dq_ref, kc_ref, kp_ref, vc_ref, vp_ref, c_ref, s_ref, dqo_ref, dko_ref, dvo_ref):
        cv, sv = c_ref[...], s_ref[...]
        has_next = (pl.program_id(0) + 1 < nb).astype(F32)
        for b in range(nqb):
            d = dq_ref[:, b * LANES:(b + 1) * LANES]
            dqo_ref[:, b * LANES:(b + 1) * LANES] = (d * cv + _swap_halves(d * sv)).astype(BF16)
        d = kc_ref[0] + kc_ref[1] + has_next * (kp_ref[0] + kp_ref[1])
        dko_ref[...] = (d * cv + _swap_halves(d * sv)).astype(BF16)
        dvo_ref[...] = (vc_ref[0] + vc_ref[1] + has_next * (vp_ref[0] + vp_ref[1])).astype(BF16)

    tab = pl.BlockSpec((tr, LANES), lambda i: (i, 0))
    cur = pl.BlockSpec((2, tr, LANES), lambda i: (0, i, 0))
    nxt = pl.BlockSpec((2, tr, LANES), lambda i: (0, jnp.minimum(i + 1, nb - 1), 0))
    return pl.pallas_call(
        body, name=name, grid=(nb,),
        in_specs=[pl.BlockSpec((tr, SWA_Q_W), lambda i: (i, 0)), cur, nxt, cur, nxt, tab, tab],
        out_specs=[pl.BlockSpec((tr, SWA_Q_W), lambda i: (i, 0)), tab, tab],
        out_shape=[jax.ShapeDtypeStruct((S, SWA_Q_W), BF16), jax.ShapeDtypeStruct((S, LANES), BF16),
                   jax.ShapeDtypeStruct((S, LANES), BF16)],
        compiler_params=_cparams("parallel"),
    )(dq, dk_cur, dk_prev, dv_cur, dv_prev, cos_t, sin_t)


def _dot_nt(a, b):
    return lax.dot_general(a, b, (((1,), (1,)), ((), ())), preferred_element_type=F32)


def _dot_tn(a, b):
    return lax.dot_general(a, b, (((0,), (0,)), ((), ())), preferred_element_type=F32)


def _dot_nn(a, b):
    return lax.dot_general(a, b, (((1,), (0,)), ((), ())), preferred_element_type=F32)


def _roll_half(t):
    return pltpu.roll(t.astype(F32), HEAD_DIM, 1).astype(t.dtype)


SWA_STACK = SWA_GROUP // 2


def _swa_common(hk, n, kp_ref, kc_ref, vp_ref, vc_ref):
    k2 = jnp.concatenate([kp_ref[...], kc_ref[...]], axis=0)
    v2 = jnp.concatenate([vp_ref[...], vc_ref[...]], axis=0)
    k_sw, v_sw = _roll_half(k2), _roll_half(v2)
    rows = SWA_STACK * WINDOW
    row = lax.broadcasted_iota(jnp.int32, (rows, 2 * WINDOW), 0) % WINDOW
    col = lax.broadcasted_iota(jnp.int32, (rows, 2 * WINDOW), 1)
    diff = row + WINDOW - col
    allowed = (diff >= 0) & (diff < WINDOW) & ((col >= WINDOW) | (n > 0))
    lane = lax.broadcasted_iota(jnp.int32, (1, LANES), 1)
    half = [lane < HEAD_DIM, lane >= HEAD_DIM]
    kk = [jnp.where(hk == a, k2, k_sw) for a in range(2)]
    vv = [jnp.where(hk == a, v2, v_sw) for a in range(2)]
    return allowed, half, kk, vv


def _swa_stack(ref, mask, scale=None):
    parts = []
    for t in range(SWA_STACK):
        blk = ref[:, t * LANES:(t + 1) * LANES]
        if scale is not None:
            blk = blk * jnp.asarray(scale, blk.dtype)
        parts.append(jnp.where(mask, blk, jnp.zeros_like(blk)))
    return jnp.concatenate(parts, axis=0)


def _swa_sink_column(sink_ref, hk, a):
    blk = lax.broadcasted_iota(jnp.int32, (SWA_STACK * WINDOW, 1), 0) // WINDOW
    col = jnp.zeros((SWA_STACK * WINDOW, 1), F32)
    for t in range(SWA_STACK):
        col = jnp.where(blk == t, sink_ref[hk * SWA_GROUP + 2 * t + a], col)
    return col


def _swa_probs(qm, kk, allowed, sink):
    s = jnp.where(allowed, _dot_nt(qm, kk), NEG)
    m = jnp.maximum(jnp.max(s, axis=1, keepdims=True), sink)
    e = jnp.exp(s - m)
    es = jnp.exp(sink - m)
    inv = 1.0 / (jnp.sum(e, axis=1, keepdims=True) + es)
    return e * inv, es * inv


def _swa_fwd(q_rope, k_rope, proj, sinks, *, v_off, name):
    S = q_rope.shape[0]
    nb = S // WINDOW
    gw = SWA_GROUP * HEAD_DIM

    def body(sink_ref, q_ref, kp_ref, kc_ref, vp_ref, vc_ref, o_ref):
        hk, n = pl.program_id(0), pl.program_id(1)
        allowed, half, kk, vv = _swa_common(hk, n, kp_ref, kc_ref, vp_ref, vc_ref)
        outs = []
        for a in range(2):
            qm = _swa_stack(q_ref, half[a], ATT_SCALE)
            p, _ = _swa_probs(qm, kk[a], allowed, _swa_sink_column(sink_ref, hk, a))
            outs.append(_dot_nn(p.astype(BF16), vv[a]))
        for t in range(SWA_STACK):
            rows = slice(t * WINDOW, (t + 1) * WINDOW)
            o_ref[:, t * LANES:(t + 1) * LANES] = jnp.where(half[0], outs[0][rows], outs[1][rows]).astype(BF16)

    prev = lambda hk, n: (jnp.maximum(n - 1, 0), 0)
    cur = lambda hk, n: (n, 0)
    vprev = lambda hk, n: (jnp.maximum(n - 1, 0), v_off // LANES)
    vcur = lambda hk, n: (n, v_off // LANES)
    blk = lambda m: pl.BlockSpec((WINDOW, LANES), m)
    return pl.pallas_call(
        body, name=name, grid=(2, nb),
        in_specs=[pl.BlockSpec(memory_space=pltpu.SMEM),
                  pl.BlockSpec((WINDOW, gw), lambda hk, n: (n, hk)),
                  blk(prev), blk(cur), blk(vprev), blk(vcur)],
        out_specs=pl.BlockSpec((WINDOW, gw), lambda hk, n: (n, hk)),
        out_shape=jax.ShapeDtypeStruct((S, SWA_Q_W), BF16),
        compiler_params=_cparams("parallel", "parallel"),
    )(sinks, q_rope, k_rope, k_rope, proj, proj)


def _swa_bwd(q_rope, k_rope, proj, sinks, d_o, *, v_off, name):
    S = q_rope.shape[0]
    nb = S // WINDOW
    gw = SWA_GROUP * HEAD_DIM

    def body(sink_ref, q_ref, kp_ref, kc_ref, vp_ref, vc_ref, do_ref,
             dq_ref, dkc_ref, dkp_ref, dvc_ref, dvp_ref, dsink_ref):
        hk, n = pl.program_id(0), pl.program_id(1)
        allowed, half, kk, vv = _swa_common(hk, n, kp_ref, kc_ref, vp_ref, vc_ref)
        dk_acc = jnp.zeros((2 * WINDOW, LANES), F32)
        dv_acc = jnp.zeros((2 * WINDOW, LANES), F32)
        srow = lax.broadcasted_iota(jnp.int32, (SWA_GROUP, LANES), 0)
        dsink = jnp.zeros((SWA_GROUP, LANES), F32)
        dqs = []
        for a in range(2):
            qm = _swa_stack(q_ref, half[a], ATT_SCALE)
            dom = _swa_stack(do_ref, half[a])
            p, psink = _swa_probs(qm, kk[a], allowed, _swa_sink_column(sink_ref, hk, a))
            dp = _dot_nt(dom, vv[a])
            delta = jnp.sum(p * dp, axis=1, keepdims=True)
            ds = (p * (dp - delta)).astype(BF16)
            dsk = psink * delta
            for t in range(SWA_STACK):
                dsink = dsink + jnp.where(srow == 2 * t + a, -jnp.sum(dsk[t * WINDOW:(t + 1) * WINDOW]), 0.0)
            dqs.append(_dot_nn(ds, kk[a]) * ATT_SCALE)
            dk_acc = dk_acc + _dot_tn(ds, qm)
            dv_acc = dv_acc + _dot_tn(p.astype(BF16), dom)
        for t in range(SWA_STACK):
            rows = slice(t * WINDOW, (t + 1) * WINDOW)
            dq_ref[:, t * LANES:(t + 1) * LANES] = jnp.where(half[0], dqs[0][rows], dqs[1][rows])
        lane = lax.broadcasted_iota(jnp.int32, (1, LANES), 1)
        mine = (lane >= HEAD_DIM) == (hk == 1)
        dk_t = jnp.where(mine, dk_acc + pltpu.roll(dk_acc, HEAD_DIM, 1), 0.0)
        dv_t = jnp.where(mine, dv_acc + pltpu.roll(dv_acc, HEAD_DIM, 1), 0.0)
        dkp_ref[0] = dk_t[:WINDOW]
        dkc_ref[0] = dk_t[WINDOW:]
        dvp_ref[0] = dv_t[:WINDOW]
        dvc_ref[0] = dv_t[WINDOW:]

        @pl.when(n == 0)
        def _():
            dsink_ref[0] = dsink

        @pl.when(n > 0)
        def _():
            dsink_ref[0] += dsink

    prev = lambda hk, n: (jnp.maximum(n - 1, 0), 0)
    cur = lambda hk, n: (n, 0)
    vprev = lambda hk, n: (jnp.maximum(n - 1, 0), v_off // LANES)
    vcur = lambda hk, n: (n, v_off // LANES)
    blk = lambda m: pl.BlockSpec((WINDOW, LANES), m)
    qblk = pl.BlockSpec((WINDOW, gw), lambda hk, n: (n, hk))
    part = pl.BlockSpec((1, WINDOW, LANES), lambda hk, n: (hk, n, 0))
    part_shape = jax.ShapeDtypeStruct((2, S, LANES), F32)
    return pl.pallas_call(
        body, name=name, grid=(2, nb),
        in_specs=[pl.BlockSpec(memory_space=pltpu.SMEM), qblk, blk(prev), blk(cur), blk(vprev), blk(vcur), qblk],
        out_specs=[qblk, part, part, part, part,
                   pl.BlockSpec((1, SWA_GROUP, LANES), lambda hk, n: (hk, 0, 0))],
        out_shape=[jax.ShapeDtypeStruct((S, SWA_Q_W), F32), part_shape, part_shape, part_shape, part_shape,
                   jax.ShapeDtypeStruct((2, SWA_GROUP, LANES), F32)],
        compiler_params=_cparams("parallel", "arbitrary"),
    )(sinks, q_rope, k_rope, k_rope, proj, proj, d_o)


def _fox_prep(z_t, bias_col, *, name):
    H, S = z_t.shape
    tb = _pick(S, 512)

    def body(z_ref, b_ref, o_ref, carry_ref):
        @pl.when(pl.program_id(0) == 0)
        def _():
            carry_ref[...] = jnp.zeros_like(carry_ref)

        zz = z_ref[...] + b_ref[...]
        t = jnp.exp(-jnp.abs(zz))
        log1p = jnp.where(t < 1e-2, t * (1.0 - t * (0.5 - t * (1.0 / 3.0))), jnp.log(1.0 + t))
        logf = jnp.minimum(zz, 0.0) - log1p
        r = lax.broadcasted_iota(jnp.int32, (tb, tb), 0)
        c = lax.broadcasted_iota(jnp.int32, (tb, tb), 1)
        tri = (r <= c).astype(BF16)
        hi = logf.astype(BF16)
        r1 = logf - hi.astype(F32)
        mid = r1.astype(BF16)
        lo = (r1 - mid.astype(F32)).astype(BF16)
        cs = _dot_nn(hi, tri) + _dot_nn(mid, tri) + _dot_nn(lo, tri) + carry_ref[:, 0:1]
        o_ref[...] = -cs
        carry_ref[...] = jnp.zeros_like(carry_ref) + cs[:, tb - 1:tb]

    return pl.pallas_call(
        body, name=name, grid=(S // tb,),
        in_specs=[pl.BlockSpec((H, tb), lambda i: (0, i)), pl.BlockSpec((H, 1), lambda i: (0, 0))],
        out_specs=pl.BlockSpec((H, tb), lambda i: (0, i)),
        out_shape=jax.ShapeDtypeStruct((H, S), F32),
        scratch_shapes=[pltpu.VMEM((H, LANES), F32)],
        compiler_params=_cparams("arbitrary"),
    )(z_t, bias_col)


def _fox_post(drow, dcol, z_t, bias_col, *, name):
    H, S = z_t.shape
    tb = _pick(S, 512)
    nb = S // tb

    def body(dr_ref, d_ref, z_ref, b_ref, dz_ref, db_ref, carry_ref):
        @pl.when(pl.program_id(0) == 0)
        def _():
            carry_ref[...] = jnp.zeros_like(carry_ref)
            db_ref[...] = jnp.zeros_like(db_ref)

        dc = dr_ref[...] - d_ref[...]
        r = lax.broadcasted_iota(jnp.int32, (tb, tb), 0)
        c = lax.broadcasted_iota(jnp.int32, (tb, tb), 1)
        tri = (r >= c).astype(BF16)
        hi = dc.astype(BF16)
        r1 = dc - hi.astype(F32)
        mid = r1.astype(BF16)
        lo = (r1 - mid.astype(F32)).astype(BF16)
        dlogf = _dot_nn(hi, tri) + _dot_nn(mid, tri) + _dot_nn(lo, tri) + carry_ref[:, 0:1]
        carry_ref[...] = jnp.zeros_like(carry_ref) + dlogf[:, 0:1]
        dz = dlogf * _sigmoid(-(z_ref[...] + b_ref[...]))
        dz_ref[...] = dz
        db_ref[...] += jnp.sum(dz, axis=1, keepdims=True)

    rev = lambda i: (0, nb - 1 - i)
    return pl.pallas_call(
        body, name=name, grid=(nb,),
        in_specs=[pl.BlockSpec((H, tb), rev), pl.BlockSpec((H, tb), rev), pl.BlockSpec((H, tb), rev),
                  pl.BlockSpec((H, 1), lambda i: (0, 0))],
        out_specs=[pl.BlockSpec((H, tb), rev), pl.BlockSpec((H, LANES), lambda i: (0, 0))],
        out_shape=[jax.ShapeDtypeStruct((H, S), F32), jax.ShapeDtypeStruct((H, LANES), F32)],
        scratch_shapes=[pltpu.VMEM((H, LANES), F32)],
        compiler_params=_cparams("arbitrary"),
    )(drow, dcol, z_t, bias_col)


def _fox_blocks(S):
    cap = max(LANES, S // 4)
    return (min(FOX_FWD_BLOCKS[0], cap), min(FOX_FWD_BLOCKS[1], cap)), \
           (min(FOX_BWD_BLOCKS[0], cap), min(FOX_BWD_BLOCKS[1], cap))


def _key_bias_blocks(negc, bk):
    H, S = negc.shape
    return negc.reshape(H // 2, 2, S // bk, bk).transpose(0, 2, 1, 3)


def _fox_fwd(proj, negc4, *, q_off, k_off, v_off, bq, bk, name):
    S = proj.shape[0]
    nq, nk = S // bq, S // bk
    npair = FOX_HEADS // 2
    assert bq % bk == 0 or bk % bq == 0
    nmask = max(1, bq // bk)

    gp = FOX_FWD_PAIRS
    gw = gp * LANES
    assert q_off % gw == 0 and k_off % gw == 0 and v_off % gw == 0 and npair % gp == 0

    def body(q_ref, k_ref, v_ref, nc_ref, o_ref, lse_ref):
        i = pl.program_id(1)
        lane = lax.broadcasted_iota(jnp.int32, (1, LANES), 1)
        half = [lane < HEAD_DIM, lane >= HEAD_DIM]
        qh = []
        for g in range(gp):
            q2 = q_ref[:, g * LANES:(g + 1) * LANES] * jnp.asarray(ATT_SCALE, BF16)
            qh += [jnp.where(half[h], q2, jnp.zeros_like(q2)) for h in range(2)]
        row = lax.broadcasted_iota(jnp.int32, (bq, bk), 0)
        col = lax.broadcasted_iota(jnp.int32, (bq, bk), 1)
        rel = row - col
        nfull = (i * bq) // bk

        spare = [HEAD_DIM, 0]
        ones_lane = [lane == spare[h] for h in range(2)]

        def step(j, carry, masked):
            start = pl.multiple_of(j * bk, bk)
            new = []
            for g in range(gp):
                ks = k_ref[pl.ds(start, bk), g * LANES:(g + 1) * LANES]
                vs = v_ref[pl.ds(start, bk), g * LANES:(g + 1) * LANES]
                nb = nc_ref[g, j]
                for h in range(2):
                    m, acc = carry[4 * g + 2 * h:4 * g + 2 * h + 2]
                    vh = jnp.where(half[h], vs, jnp.where(ones_lane[h], jnp.ones_like(vs), jnp.zeros_like(vs)))
                    s = _dot_nt(qh[2 * g + h], ks) + nb[h:h + 1, :]
                    if masked:
                        s = jnp.where(rel >= j * bk - i * bq, s, NEG)
                    m_new = jnp.maximum(m, jnp.max(s, axis=1, keepdims=True))
                    p = jnp.exp(s - m_new).astype(BF16)
                    acc = jnp.exp(m - m_new) * acc + _dot_nn(p, vh)
                    new += [m_new, acc]
            return tuple(new)

        init = (jnp.full((bq, 1), NEG, F32), jnp.zeros((bq, LANES), F32)) * (2 * gp)
        carry = lax.fori_loop(0, nfull, lambda j, c: step(j, c, False), init)
        for t in range(nmask):
            carry = step(nfull + t, carry, True)
        for g in range(gp):
            outs, lses = [], []
            for h in range(2):
                m, acc = carry[4 * g + 2 * h:4 * g + 2 * h + 2]
                l = acc[:, spare[h]:spare[h] + 1]
                outs.append(acc * (1.0 / l))
                lses.append(m + jnp.log(l))
            o_ref[:, g * LANES:(g + 1) * LANES] = jnp.where(half[0], outs[0], outs[1]).astype(BF16)
            lse_ref[g] = jnp.where(half[0], lses[0], lses[1])

    seq = lambda off: pl.BlockSpec((S, gw), lambda hp, i: (0, off // gw + hp))
    return pl.pallas_call(
        body, name=name, grid=(npair // gp, nq),
        in_specs=[pl.BlockSpec((bq, gw), lambda hp, i: (i, q_off // gw + hp)), seq(k_off), seq(v_off),
                  pl.BlockSpec((gp, nk, 2, bk), lambda hp, i: (hp, 0, 0, 0))],
        out_specs=[pl.BlockSpec((bq, gw), lambda hp, i: (i, hp)),
                   pl.BlockSpec((gp, bq, LANES), lambda hp, i: (hp, i, 0))],
        out_shape=[jax.ShapeDtypeStruct((S, FOX_W), BF16), jax.ShapeDtypeStruct((npair, S, LANES), F32)],
        compiler_params=_cparams("parallel", "parallel"),
    )(proj, proj, proj, negc4)


def _fox_bwd(proj, negc4, o, lse, d_o, q_t, do_t, *, q_off, k_off, v_off, bq, bk, name, deps=()):
    S = proj.shape[0]
    nq, nk = S // bq, S // bk
    npair = FOX_HEADS // 2
    assert bq % bk == 0 or bk % bq == 0
    nmask = max(1, bk // bq)

    def body(q_ref, k_ref, v_ref, nc_ref, o_ref, lse_ref, do_ref, qt_ref, dot_ref, *rest):
        dqo_ref, dk_ref, dv_ref, dn_ref, dr_ref, delta_ref, rs_ref, dq_ref = rest[len(deps):]
        j = pl.program_id(1)
        lane = lax.broadcasted_iota(jnp.int32, (1, LANES), 1)
        half = [lane < HEAD_DIM, lane >= HEAD_DIM]
        spare = [HEAD_DIM, 0]
        ones_lane = [lane == spare[h] for h in range(2)]
        srow = lax.broadcasted_iota(jnp.int32, (LANES, 1), 0)
        rhalf = [srow < HEAD_DIM, srow >= HEAD_DIM]
        ones_row = [srow == spare[h] for h in range(2)]
        k2, v2 = k_ref[...], v_ref[...]
        one_k = jnp.ones_like(k2)
        kh = [jnp.where(half[h], k2, jnp.where(ones_lane[h], one_k, jnp.zeros_like(k2))) for h in range(2)]
        nb = nc_ref[0, 0]
        row = lax.broadcasted_iota(jnp.int32, (bq, bk), 0)
        col = lax.broadcasted_iota(jnp.int32, (bq, bk), 1)
        rel = row - col
        i_first = (j * bk) // bq

        @pl.when(j == 0)
        def _():
            dq_ref[...] = jnp.zeros_like(dq_ref)
            rs_ref[...] = jnp.zeros_like(rs_ref)
            for b in range(nq):
                prod = do_ref[b * bq:(b + 1) * bq, :].astype(F32) * o_ref[b * bq:(b + 1) * bq, :].astype(F32)
                d0 = jnp.sum(jnp.where(half[0], prod, 0.0), axis=1, keepdims=True)
                d1 = jnp.sum(jnp.where(half[1], prod, 0.0), axis=1, keepdims=True)
                delta_ref[b * bq:(b + 1) * bq, :] = jnp.where(half[0], d0, d1)

        def step(i, carry, masked):
            dkt_a, dkt_b, dvt = carry
            dkts = [dkt_a, dkt_b]
            start = pl.multiple_of(i * bq, bq)
            q2 = q_ref[pl.ds(start, bq), :] * jnp.asarray(ATT_SCALE, BF16)
            do2 = do_ref[pl.ds(start, bq), :]
            qt = qt_ref[i] * jnp.asarray(ATT_SCALE, BF16)
            dot = dot_ref[i]
            lse2 = lse_ref[0, pl.ds(start, bq), :]
            del2 = delta_ref[pl.ds(start, bq), :]
            dqf = []
            for h in range(2):
                qm = jnp.where(half[h], q2, jnp.zeros_like(q2))
                dom = jnp.where(half[h], do2, jnp.zeros_like(do2))
                qtm = jnp.where(rhalf[h], qt, jnp.where(ones_row[h], jnp.ones_like(qt), jnp.zeros_like(qt)))
                dotm = jnp.where(rhalf[h], dot, jnp.zeros_like(dot))
                c0 = h * HEAD_DIM
                p = jnp.exp(_dot_nt(qm, k2) + nb[h:h + 1, :] - lse2[:, c0:c0 + 1])
                if masked:
                    p = jnp.where(rel >= j * bk - i * bq, p, 0.0)
                dp = _dot_nt(dom, v2)
                dsb = (p * (dp - del2[:, c0:c0 + 1])).astype(BF16)
                dvt = dvt + _dot_nn(dotm, p.astype(BF16))
                dkts[h] = dkts[h] + _dot_nn(qtm, dsb)
                dqf.append(_dot_nn(dsb, kh[h]))
            dq_ref[pl.ds(start, bq), :] += jnp.where(half[0], dqf[0], dqf[1]) * ATT_SCALE
            rs_ref[pl.ds(start, bq), :] += jnp.where(ones_lane[0], dqf[0], jnp.where(ones_lane[1], dqf[1], 0.0))
            return dkts[0], dkts[1], dvt

        zero = jnp.zeros((LANES, bk), F32)
        carry = (zero, zero, zero)
        for t in range(nmask):
            carry = step(i_first + t, carry, True)
        dkt_a, dkt_b, dvt = lax.fori_loop(i_first + nmask, nq, lambda i, c: step(i, c, False), carry)
        dk_ref[...] = jnp.where(rhalf[0], dkt_a, dkt_b).T.astype(BF16)
        dv_ref[...] = dvt.T.astype(BF16)
        dn_ref[0, 0] = jnp.concatenate([dkt_a[spare[0]:spare[0] + 1], dkt_b[spare[1]:spare[1] + 1]], axis=0)

        @pl.when(j == nk - 1)
        def _():
            dqo_ref[...] = dq_ref[...].astype(BF16)
            for b in range(nq):
                t = rs_ref[b * bq:(b + 1) * bq, :].T
                dr_ref[0, b] = jnp.concatenate([t[spare[0]:spare[0] + 1], t[spare[1]:spare[1] + 1]], axis=0)

    once = pl.Buffered(1)
    seq = lambda off: pl.BlockSpec((S, LANES), lambda hp, j: (0, off // LANES + hp), pipeline_mode=once)
    blk = lambda off: pl.BlockSpec((bk, LANES), lambda hp, j: (j, off // LANES + hp))
    nc = pl.BlockSpec((1, 1, 2, bk), lambda hp, j: (hp, j, 0, 0))
    tsp = pl.BlockSpec((nq, LANES, bq), lambda hp, j: (0, hp, 0), pipeline_mode=once)
    return pl.pallas_call(
        body, name=name, grid=(npair, nk),
        in_specs=[seq(q_off), blk(k_off), blk(v_off), nc, seq(0),
                  pl.BlockSpec((1, S, LANES), lambda hp, j: (hp, 0, 0), pipeline_mode=once), seq(0),
                  tsp, tsp] + [_ANY] * len(deps),
        out_specs=[pl.BlockSpec((S, LANES), lambda hp, j: (0, hp)), blk(0), blk(0), nc,
                   pl.BlockSpec((1, nq, 2, bq), lambda hp, j: (hp, 0, 0, 0))],
        out_shape=[jax.ShapeDtypeStruct((S, FOX_W), BF16), jax.ShapeDtypeStruct((S, FOX_W), BF16),
                   jax.ShapeDtypeStruct((S, FOX_W), BF16), jax.ShapeDtypeStruct((npair, nk, 2, bk), F32),
                   jax.ShapeDtypeStruct((npair, nq, 2, bq), F32)],
        scratch_shapes=[pltpu.VMEM((S, LANES), F32), pltpu.VMEM((S, LANES), F32), pltpu.VMEM((S, LANES), F32)],
        compiler_params=_cparams("parallel", "arbitrary"),
    )(proj, proj, proj, negc4, o, lse, d_o, q_t, do_t, *deps)


def _exchange(arrs, *, gather, name):
    n = len(arrs)
    npeer = N_DEV - 1

    def body(*refs):
        ins, outs = refs[:n], refs[n:2 * n]
        send_sems, recv_sems, loc_sems = refs[2 * n:]
        x, y, c = lax.axis_index("x"), lax.axis_index("y"), lax.axis_index("c")
        me = 4 * x + 2 * y + c
        peers = []
        for k in range(1, N_DEV):
            px = 1 - x if k & 4 else x
            py = 1 - y if k & 2 else y
            pc = 1 - c if k & 1 else c
            peers.append(((px, py, pc), 4 * px + 2 * py + pc))

        def remote(w, k):
            dev, idx = peers[k]
            src = ins[w] if gather else ins[w].at[idx]
            return pltpu.make_async_remote_copy(
                src_ref=src, dst_ref=outs[w].at[me],
                send_sem=send_sems.at[w * npeer + k], recv_sem=recv_sems.at[w * npeer + k],
                device_id=dev, device_id_type=pl.DeviceIdType.MESH)

        def arrival(w, k):
            dev, idx = peers[k]
            src = ins[w] if gather else ins[w].at[idx]
            return pltpu.make_async_remote_copy(
                src_ref=src, dst_ref=outs[w].at[idx],
                send_sem=send_sems.at[w * npeer + k], recv_sem=recv_sems.at[w * npeer + k],
                device_id=dev, device_id_type=pl.DeviceIdType.MESH)

        local = []
        for w in range(n):
            for k in range(npeer):
                remote(w, k).start()
            cp = pltpu.make_async_copy(ins[w] if gather else ins[w].at[me], outs[w].at[me], loc_sems.at[w])
            cp.start()
            local.append(cp)
        for w in range(n):
            for k in range(npeer):
                arrival(w, k).wait_recv()
        for w in range(n):
            for k in range(npeer):
                remote(w, k).wait_send()
            local[w].wait()

    hbm = pl.BlockSpec(memory_space=pl.ANY)
    out_shape = [jax.ShapeDtypeStruct((N_DEV,) + (a.shape if gather else a.shape[1:]), a.dtype) for a in arrs]
    return pl.pallas_call(
        body, name=name,
        in_specs=[hbm] * n, out_specs=[hbm] * n, out_shape=out_shape,
        scratch_shapes=[pltpu.SemaphoreType.DMA((n * npeer,)), pltpu.SemaphoreType.DMA((n * npeer,)),
                        pltpu.SemaphoreType.DMA((n,))],
        compiler_params=pltpu.CompilerParams(has_side_effects=True),
    )(*arrs)


def _gather_two_level(shard, *, name):
    def body(x_ref, out_ref, send_sems, recv_sems, local_sem):
        x, y, c = lax.axis_index("x"), lax.axis_index("y"), lax.axis_index("c")
        me, sibling = (x, y, c), (x, y, 1 - c)
        chips = [(1 - x, y), (x, 1 - y), (1 - x, 1 - y)]

        def slot(px, py, pc):
            return out_ref.at[4 * px + 2 * py + pc]

        def copy(k, block, to, src=None):
            return pltpu.make_async_remote_copy(
                src_ref=slot(*block) if src is None else src, dst_ref=slot(*block),
                send_sem=send_sems.at[k], recv_sem=recv_sems.at[k],
                device_id=to, device_id_type=pl.DeviceIdType.MESH)

        mine = pltpu.make_async_copy(x_ref, slot(*me), local_sem)
        mine.start()
        first = [copy(0, me, sibling, src=x_ref)]
        first += [copy(1 + j, me, (*chip, c), src=x_ref) for j, chip in enumerate(chips)]
        for cp in first:
            cp.start()
        passed = [copy(4 + j, (*chip, c), sibling) for j, chip in enumerate(chips)]
        for j, chip in enumerate(chips):
            copy(1 + j, (*chip, c), me).wait_recv()
            passed[j].start()
        copy(0, sibling, me).wait_recv()
        for j, chip in enumerate(chips):
            copy(4 + j, (*chip, 1 - c), me).wait_recv()
        for cp in first + passed:
            cp.wait_send()
        mine.wait()

    return pl.pallas_call(
        body, name=name,
        in_specs=[_ANY], out_specs=_ANY,
        out_shape=jax.ShapeDtypeStruct((N_DEV,) + shard.shape, shard.dtype),
        scratch_shapes=[pltpu.SemaphoreType.DMA((N_DEV - 1,)), pltpu.SemaphoreType.DMA((N_DEV - 1,)),
                        pltpu.SemaphoreType.DMA],
        compiler_params=pltpu.CompilerParams(has_side_effects=True),
    )(shard)


_HBM = pl.BlockSpec(memory_space=pltpu.HBM)
_SEM = pl.BlockSpec(memory_space=pltpu.SEMAPHORE)
_EFFECT = pltpu.SideEffectType.DATAFLOW_SIDE_EFFECTING
NPEER = N_DEV - 1


def _peer_table():
    x, y, c = lax.axis_index("x"), lax.axis_index("y"), lax.axis_index("c")
    peers = []
    for k in range(1, N_DEV):
        px = 1 - x if k & 4 else x
        py = 1 - y if k & 2 else y
        pc = 1 - c if k & 1 else c
        peers.append(((px, py, pc), 4 * px + 2 * py + pc))
    return 4 * x + 2 * y + c, peers


def _split_copy(ins, lands, send_sems, recv_sems, gather, me, peers, w, k, arriving):
    dev, idx = peers[k]
    return pltpu.make_async_remote_copy(
        src_ref=ins[w] if gather else ins[w].at[idx],
        dst_ref=lands[w].at[idx if arriving else me],
        send_sem=send_sems.at[w * NPEER + k], recv_sem=recv_sems.at[w * NPEER + k],
        device_id=dev, device_id_type=pl.DeviceIdType.MESH)


def _exchange_start(arrs, *, gather, name, deps=()):
    n = len(arrs)
    land_shapes = [(N_DEV,) + (a.shape if gather else a.shape[1:]) for a in arrs]

    def body(*refs):
        ins, lands = refs[:n], refs[n:2 * n]
        send_sems, recv_sems = refs[2 * n + len(deps)], refs[2 * n + len(deps) + 1]
        token = refs[-1]
        me, peers = _peer_table()
        for w in range(n):
            for k in range(NPEER):
                _split_copy(ins, lands, send_sems, recv_sems, gather, me, peers, w, k, False).start()
        token[...] = jnp.zeros_like(token)

    out_shape = ([pltpu.SemaphoreType.DMA((n * NPEER,)), pltpu.SemaphoreType.DMA((n * NPEER,))]
                 + [pltpu.HBM(a.shape, a.dtype) for a in arrs]
                 + [pltpu.HBM(s, a.dtype) for s, a in zip(land_shapes, arrs)]
                 + [jax.ShapeDtypeStruct((8, LANES), F32)])
    res = pl.pallas_call(
        body, name=name,
        in_specs=[_HBM] * (2 * n) + [_ANY] * len(deps),
        out_specs=[_SEM, _SEM] + [_HBM] * (2 * n) + [pl.BlockSpec(memory_space=pltpu.VMEM)],
        out_shape=out_shape,
        input_output_aliases={i: 2 + i for i in range(2 * n)},
        compiler_params=pltpu.CompilerParams(has_side_effects=_EFFECT),
    )(*[pltpu.with_memory_space_constraint(a, pltpu.HBM) for a in arrs],
      *[pltpu.with_memory_space_constraint(lax.empty(s, a.dtype), pltpu.HBM) for s, a in zip(land_shapes, arrs)],
      *deps)
    return (n, gather, res[0], res[1], res[2:2 + n], res[2 + n:2 + 2 * n]), res[-1]


def _exchange_wait(handle, after, *, name):
    n, gather, send_sems, recv_sems, ins_thru, lands_thru = handle

    def body(*refs):
        ins, lands = refs[:n], refs[n:2 * n]
        send_s, recv_s = refs[2 * n], refs[2 * n + 1]
        me, peers = _peer_table()
        for w in range(n):
            for k in range(NPEER):
                _split_copy(ins, lands, send_s, recv_s, gather, me, peers, w, k, False).wait_send()
                _split_copy(ins, lands, send_s, recv_s, gather, me, peers, w, k, True).wait_recv()

    res = pl.pallas_call(
        body, name=name,
        in_specs=[_HBM] * (2 * n) + [_SEM, _SEM, pl.BlockSpec(memory_space=pl.ANY)],
        out_specs=[_HBM] * (2 * n),
        out_shape=[pltpu.HBM(a.shape, a.dtype) for a in list(ins_thru) + list(lands_thru)],
        input_output_aliases={i: i for i in range(2 * n)},
        compiler_params=pltpu.CompilerParams(has_side_effects=_EFFECT),
    )(*ins_thru, *lands_thru, send_sems, recv_sems, after)
    return res[:n], res[n:2 * n]


def _ordered_sum(s_ref, own_ref):
    if own_ref is None:
        blocks = [s_ref[q].astype(F32) for q in range(N_DEV)]
    else:
        me = 4 * lax.axis_index("x") + 2 * lax.axis_index("y") + lax.axis_index("c")
        own = own_ref[...]
        blocks = [jnp.where(me == q, own, s_ref[q]).astype(F32) for q in range(N_DEV)]
    acc = blocks[0]
    for b in blocks[1:]:
        acc = acc + b
    return acc


def _sum8(stack, own, *, name):
    _, R, C = stack.shape
    if R % 8 == 0:
        tr, tc = _pick(R, max(8, STEP_BYTES // (C * 4 * (N_DEV + 2))), 8), C
    else:
        tr, tc = R, _pick(C, max(LANES, STEP_BYTES // (R * 4 * (N_DEV + 2))))

    def body(s_ref, own_ref, o_ref):
        o_ref[...] = _ordered_sum(s_ref, own_ref)

    blk = pl.BlockSpec((tr, tc), lambda i, j: (i, j))
    return pl.pallas_call(
        body, name=name, grid=(R // tr, C // tc),
        in_specs=[pl.BlockSpec((N_DEV, tr, tc), lambda i, j: (0, i, j)), blk],
        out_specs=blk,
        out_shape=jax.ShapeDtypeStruct((R, C), F32),
        compiler_params=_cparams("parallel", "parallel"),
    )(stack, own)


def _adamw_math(w, g, m, v):
    m = ADAM_B1 * m + (1.0 - ADAM_B1) * g
    v = ADAM_B2 * v + (1.0 - ADAM_B2) * (g * g)
    m_hat = m / (1.0 - ADAM_B1 ** ADAM_STEP)
    v_hat = v / (1.0 - ADAM_B2 ** ADAM_STEP)
    delta = -ADAM_LR * (m_hat / (jnp.sqrt(v_hat) + ADAM_EPS) + ADAM_WD * w)
    return delta, m, v


def _adamw(w, g, m, v, *, name, stacked, own=None):
    R, C = w.shape
    tr = _pick(R, max(8, STEP_BYTES // (C * 4 * (8 + (N_DEV if stacked else 1)))), 8)
    has_own = own is not None

    def body(w_ref, g_ref, m_ref, v_ref, *rest):
        go_ref, d_ref, mo_ref, vo_ref = rest[-4:]
        g = _ordered_sum(g_ref, rest[0] if has_own else None) if stacked else g_ref[...]
        delta, m2, v2 = _adamw_math(w_ref[...], g, m_ref[...], v_ref[...])
        go_ref[...] = g
        d_ref[...] = delta
        mo_ref[...] = m2
        vo_ref[...] = v2

    row = pl.BlockSpec((tr, C), lambda i: (i, 0))
    g_spec = pl.BlockSpec((N_DEV, tr, C), lambda i: (0, i, 0)) if stacked else row
    return pl.pallas_call(
        body, name=name, grid=(R // tr,),
        in_specs=[row, g_spec, row, row] + [row] * has_own, out_specs=[row] * 4,
        out_shape=[jax.ShapeDtypeStruct((R, C), F32)] * 4,
        compiler_params=_cparams("parallel"),
    )(w, g, m, v, *([own] if has_own else []))


def kernel(x, positions, attn_norm, w_in, fox_f_bias, swa_sinks, w_branch_swa, w_branch_fox, w_out, mlp_norm, w_up, w_down, final_norm, loss_target, m_attn_norm, m_w_in, m_fox_f_bias, m_swa_sinks, m_w_branch_swa, m_w_branch_fox, m_w_out, m_mlp_norm, m_w_up, m_w_down, m_final_norm, v_attn_norm, v_w_in, v_fox_f_bias, v_swa_sinks, v_w_branch_swa, v_w_branch_fox, v_w_out, v_mlp_norm, v_w_up, v_w_down, v_final_norm):
    S, D = x.shape[1], x.shape[2]
    DFF = w_up.shape[2] * N_DEV
    d_in = w_in.shape[2] * N_DEV
    assert d_in == QKV_W + FOX_HEADS + 2 * D and (2 * D) % SWA_Q_W == 0 and S % (4 * LANES) == 0
    q_off = 2 * D
    k_off = q_off + SWA_Q_W
    v_off = k_off + SWA_KV_W
    fq_off = v_off + SWA_KV_W
    fk_off = fq_off + FOX_W
    fv_off = fk_off + FOX_W
    fl_off = fv_off + FOX_W
    NP = fl_off + FL_PAD
    x2d, tgt = x[0], loss_target[0]

    shards = [w_in[0].T.astype(BF16), w_branch_swa[0].T.astype(BF16), w_branch_fox[0].T.astype(BF16),
              w_out[0].astype(BF16), w_up[0].T.astype(BF16), w_down[0].astype(BF16)]
    me = 4 * lax.axis_index("x") + 2 * lax.axis_index("y") + lax.axis_index("c")

    def filled(stack, own):
        return lax.dynamic_update_slice(stack, own[None], (me,) + (0,) * own.ndim)

    g_in = _gather_two_level(shards[0], name="gather_w_in")
    h_rest, tok_rest = _exchange_start(shards[1:], gather=True, name="gather_rest_start", deps=[g_in])

    tm = _pick(S, 1024)
    td = _pick(D, 1024)
    tf = _pick(DFF, 1024)
    tnp = _pick(NP, 1024)

    h1 = _rms_fwd(x2d, attn_norm, name="rms1", deps=[tok_rest])
    w_in_t = g_in.reshape(d_in, D)
    w_in_p = jnp.concatenate([w_in_t[QKV_W + FOX_HEADS:], w_in_t[:QKV_W], w_in_t[QKV_W:QKV_W + FOX_HEADS],
                              jnp.zeros((FL_PAD - FOX_HEADS, D), BF16)], axis=0)
    w_fl_t = w_in_t[QKV_W:QKV_W + FOX_HEADS]
    proj, = _matmul(h1, w_in_p, mode="nt", name="mm_in", out_dtypes=[BF16], tm=_pick(S, 2048), tn=tnp, tk=D)
    z_t, = _matmul(w_fl_t, h1, mode="nt", name="mm_flogit", out_dtypes=[F32],
                   tm=FOX_HEADS, tn=_pick(S, 2048), tk=D)
    bias_col = fox_f_bias.reshape(FOX_HEADS, 1)
    negc = _fox_prep(z_t, bias_col, name="fox_prep")
    (fbq, fbk), (bbq, bbk) = _fox_blocks(S)
    inv_freq = ROPE_THETA ** (-jnp.arange(0, HEAD_DIM, 2, dtype=F32) / HEAD_DIM)
    invf = jnp.tile(inv_freq, LANES // (HEAD_DIM // 2)).reshape(1, LANES)
    cos_t, sin_t = _rope_tables(positions.reshape(S, 1), invf, name="rope_tables")
    q_rope, k_rope = _rope_fwd(proj, cos_t, sin_t, q_off=q_off, k_off=k_off, name="rope_fwd")
    sinks = swa_sinks.reshape(-1)
    o_a = _swa_fwd(q_rope, k_rope, proj, sinks, v_off=v_off, name="swa_fwd")
    o_b, lse = _fox_fwd(proj, _key_bias_blocks(negc, fbk), q_off=fq_off, k_off=fk_off, v_off=fv_off,
                        bq=fbq, bk=fbk, name="fox_fwd")
    s_rest, g_rest = _exchange_wait(h_rest, o_b, name="gather_rest_wait")
    g_bs, g_bf, g_o, g_up, g_dn = [filled(g, s) for g, s in zip(g_rest, s_rest)]
    w_bs_t = g_bs.reshape(D, SWA_Q_W)
    w_bf_t = g_bf.reshape(D, FOX_W)
    w_o = g_o.reshape(D, D)
    w_up_t = g_up.reshape(DFF, D)
    w_dn = g_dn.reshape(DFF, D)
    ya, = _matmul(o_a, w_bs_t, mode="nt", name="mm_branch_swa", out_dtypes=[BF16], tm=tm, tn=td, tk=SWA_Q_W)
    gate_maps = [lambda i, j, k: (i, j), lambda i, j, k: (i, j), lambda i, j, k: (i, j + D // td)]

    def merge_epi(acc, ya_t, ga_t, gb_t):
        merged = _sigmoid(ga_t.astype(F32)) * ya_t.astype(F32) + _sigmoid(gb_t.astype(F32)) * acc
        return acc, merged

    yb, merged = _matmul(o_b, w_bf_t, mode="nt", name="mm_branch_fox", out_dtypes=[BF16, BF16],
                         tm=tm, tn=td, tk=FOX_W, extras=[ya, proj, proj], extra_maps=gate_maps,
                         epilogue=merge_epi)
    x_mid, = _matmul(merged, w_o, mode="nn", name="mm_out", out_dtypes=[F32], tm=tm, tn=td, tk=D,
                     extras=[x2d], epilogue=lambda acc, r: (acc + r,))
    h2 = _rms_fwd(x_mid, mlp_norm, name="rms2")
    u, = _matmul(h2, w_up_t, mode="nt", name="mm_up", out_dtypes=[BF16], tm=_pick(S, 2048), tn=tf, tk=D,
                 epilogue=lambda acc: (jnp.maximum(acc, 0.0),))
    x_fin, = _matmul(u, w_dn, mode="nn", name="mm_down", out_dtypes=[F32], tm=tm, tn=td, tk=_pick(DFF, 2048),
                     a_fn=_square_bf16, extras=[x_mid], epilogue=lambda acc, r: (acc + r,))

    dx3, dx3b, dg3, loss_part = _loss_head(x_fin, tgt, final_norm.reshape(1, D), name="loss_head")
    d_up, = _matmul(dx3b, w_dn, mode="nt", name="mm_d_act", out_dtypes=[BF16], tm=_pick(S, 2048), tn=tf, tk=D,
                    extras=[u], epilogue=lambda acc, ut: (acc * (2.0 * ut.astype(F32)),))
    tks = _pick(S, 2048)
    dw_dn, = _matmul(u, dx3b, mode="tn", name="mm_dw_down", out_dtypes=[F32], tm=tf, tn=td, tk=tks,
                     a_fn=_square_bf16)
    dh2, = _matmul(d_up, w_up_t, mode="nn", name="mm_dh2", out_dtypes=[F32], tm=tm, tn=td, tk=_pick(DFF, 2048))
    dw_up_t, = _matmul(d_up, h2, mode="tn", name="mm_dw_up", out_dtypes=[F32], tm=tf, tn=td, tk=tks)
    h_s1, tok_s1 = _exchange_start([dw_up_t.reshape(N_DEV, DFF // N_DEV, D), dw_dn.reshape(N_DEV, DFF // N_DEV, D)],
                                   gather=False, name="scatter_mlp_start")
    dx2, dx2b, dg2 = _rms_bwd(dh2, x_mid, mlp_norm, dx3, name="rms2_bwd", want_bf16=True, deps=[tok_s1])

    def gate_bwd_epi(dm, ya_t, yb_t, ga_t, gb_t):
        sa, sb = _sigmoid(ga_t.astype(F32)), _sigmoid(gb_t.astype(F32))
        return (dm * sa, dm * sb, dm * ya_t.astype(F32) * sa * (1.0 - sa), dm * yb_t.astype(F32) * sb * (1.0 - sb))

    gmaps = [lambda i, j, k: (i, j), lambda i, j, k: (i, j), lambda i, j, k: (i, j),
             lambda i, j, k: (i, j + D // td)]
    d_ya, d_yb, d_ga, d_gb = _matmul(dx2b, w_o, mode="nt", name="mm_d_merged", out_dtypes=[BF16] * 4,
                                     tm=tm, tn=td, tk=D, extras=[ya, yb, proj, proj], extra_maps=gmaps,
                                     epilogue=gate_bwd_epi)
    dw_o, = _matmul(merged, dx2b, mode="tn", name="mm_dw_out", out_dtypes=[F32], tm=td, tn=td, tk=tks)
    d_oa, = _matmul(d_ya, w_bs_t, mode="nn", name="mm_d_oa", out_dtypes=[BF16], tm=tm, tn=SWA_Q_W, tk=D)
    d_ob, = _matmul(d_yb, w_bf_t, mode="nn", name="mm_d_ob", out_dtypes=[BF16], tm=tm, tn=FOX_W, tk=D)
    dw_bs_t, = _matmul(d_ya, o_a, mode="tn", name="mm_dw_bs", out_dtypes=[F32], tm=td, tn=SWA_Q_W, tk=tks)
    dw_bf_t, = _matmul(d_yb, o_b, mode="tn", name="mm_dw_bf", out_dtypes=[F32], tm=td, tn=FOX_W, tk=tks)
    h_s2, tok_s2 = _exchange_start([dw_bs_t.reshape(N_DEV, D // N_DEV, SWA_Q_W),
                                    dw_bf_t.reshape(N_DEV, D // N_DEV, FOX_W), dw_o.reshape(N_DEV, D // N_DEV, D)],
                                   gather=False, name="scatter_attn_start")
    def row_blocks_t(a):
        return a.reshape(S // bbq, bbq, FOX_W).transpose(0, 2, 1)

    d_fq, d_fk, d_fv, dcol4, drow4 = _fox_bwd(proj, _key_bias_blocks(negc, bbk), o_b, lse, d_ob,
                                              row_blocks_t(proj[:, fq_off:fq_off + FOX_W]), row_blocks_t(d_ob),
                                              q_off=fq_off, k_off=fk_off, v_off=fv_off, bq=bbq, bk=bbk,
                                              name="fox_bwd", deps=[tok_s2])
    dcol = dcol4.transpose(0, 2, 1, 3).reshape(FOX_HEADS, S)
    drow = drow4.transpose(0, 2, 1, 3).reshape(FOX_HEADS, S)
    dz_t, dbias_l = _fox_post(drow, dcol, z_t, bias_col, name="fox_post")
    dq_r, dk_c, dk_p, dv_c, dv_p, dsink_l = _swa_bwd(q_rope, k_rope, proj, sinks, d_oa, v_off=v_off, name="swa_bwd")
    d_aq, d_ak, d_av = _rope_bwd(dq_r, dk_c, dk_p, dv_c, dv_p, cos_t, sin_t, name="rope_bwd")
    dz_pad = jnp.pad(dz_t.T.astype(BF16), ((0, 0), (0, FL_PAD - FOX_HEADS)))
    d_proj = jnp.concatenate([d_ga, d_gb, d_aq, d_ak, d_av, d_fq, d_fk, d_fv, dz_pad], axis=1)
    tkp = _pick(NP, 2304)
    dw_in_p, = _matmul(d_proj, h1, mode="tn", name="mm_dw_in", out_dtypes=[BF16], tm=_pick(NP, 512), tn=D, tk=tks)
    dw_in_t = jnp.concatenate([dw_in_p[q_off:q_off + QKV_W], dw_in_p[fl_off:fl_off + FOX_HEADS], dw_in_p[:q_off]],
                              axis=0)
    h_s3, tok_s3 = _exchange_start([dw_in_t.reshape(N_DEV, d_in // N_DEV, D)], gather=False,
                                   name="scatter_in_start")
    dh1, = _matmul(d_proj, w_in_p, mode="nn", name="mm_dh1", out_dtypes=[F32], tm=tm, tn=td, tk=tkp, deps=[tok_s3])
    dx, dg1 = _rms_bwd(dh1, x2d, attn_norm, dx2, name="rms1_bwd", want_bf16=False)

    dbias = dbias_l[:, 0]
    dsinks = dsink_l[:, :, 0].reshape(-1)
    nsm = 3 * D + 2 * LANES
    tail = jnp.zeros((2 * LANES,), F32)
    small_g = jnp.concatenate([dg1[0], dg2[0], dg3[0],
                               tail.at[0:16].set(dbias).at[16:32].set(dsinks).at[32].set(loss_part[0, 0])])

    def pack(a_norm, b_norm, f_norm, bias, snk):
        return jnp.concatenate([a_norm[0], b_norm[0], f_norm,
                                tail.at[0:16].set(bias[0]).at[16:32].set(snk[0])]).reshape(1, nsm)

    small_stack, = _exchange([small_g.reshape(1, nsm)], gather=True, name="gather_small")
    u_sm = _adamw(pack(attn_norm, mlp_norm, final_norm, fox_f_bias, swa_sinks), small_stack,
                  pack(m_attn_norm, m_mlp_norm, m_final_norm, m_fox_f_bias, m_swa_sinks),
                  pack(v_attn_norm, v_mlp_norm, v_final_norm, v_fox_f_bias, v_swa_sinks),
                  name="adamw_small", stacked=True)
    loss = u_sm[0][0, 3 * D + 32]

    def own_of(src):
        return lax.dynamic_index_in_dim(src, me, 0, keepdims=False)

    def update_t(stack, src, w, m, v, nm):
        g = _sum8(stack, own_of(src), name="sum_" + nm).T
        return _adamw(w[0], g, m[0], v[0], name="adamw_" + nm, stacked=False)

    def update(stack, src, w, m, v, nm):
        return _adamw(w[0], stack, m[0], v[0], name="adamw_" + nm, stacked=True, own=own_of(src))

    (s_up, s_dn), (r_up, r_dn) = _exchange_wait(h_s1, u_sm[1], name="scatter_mlp_wait")
    u_up = update_t(r_up, s_up, w_up, m_w_up, v_w_up, "w_up")
    u_dn = update(r_dn, s_dn, w_down, m_w_down, v_w_down, "w_down")
    (s_bs, s_bf, s_o), (r_bs, r_bf, r_o) = _exchange_wait(h_s2, u_dn[1], name="scatter_attn_wait")
    u_bs = update_t(r_bs, s_bs, w_branch_swa, m_w_branch_swa, v_w_branch_swa, "w_bs")
    u_bf = update_t(r_bf, s_bf, w_branch_fox, m_w_branch_fox, v_w_branch_fox, "w_bf")
    u_o = update(r_o, s_o, w_out, m_w_out, v_w_out, "w_out")
    (s_w_in,), (r_in,) = _exchange_wait(h_s3, u_o[1], name="scatter_in_wait")
    u_in = update_t(r_in, s_w_in, w_in, m_w_in, v_w_in, "w_in")

    def small(kind):
        a = u_sm[kind][0]
        return dict(attn_norm=a[0:D][None], mlp_norm=a[D:2 * D][None], final_norm=a[2 * D:3 * D],
                    fox_f_bias=a[3 * D:3 * D + 16][None], swa_sinks=a[3 * D + 16:3 * D + 32][None])

    big = dict(w_in=u_in, w_branch_swa=u_bs, w_branch_fox=u_bf, w_out=u_o, w_up=u_up, w_down=u_dn)
    order = ["attn_norm", "w_in", "fox_f_bias", "swa_sinks", "w_branch_swa", "w_branch_fox", "w_out", "mlp_norm",
             "w_up", "w_down", "final_norm"]
    outs = [loss, dx[None]]
    for kind in range(4):
        sm = small(kind)
        for nm in order:
            outs.append(big[nm][kind][None] if nm in big else sm[nm])
    return tuple(outs)
```

```python
import functools

import jax
import jax.numpy as jnp
from jax import lax
from jax.experimental import pallas as pl
from jax.experimental.pallas import tpu as pltpu

F32 = jnp.float32
BF16 = jnp.bfloat16

N_DEV = 8
HEAD_DIM = 64
SWA_Q_W = 1024
SWA_KV_W = 128
SWA_GROUP = 8
WINDOW = 128
FOX_W = 1024
FOX_HEADS = 16
QKV_W = SWA_Q_W + 2 * SWA_KV_W + 3 * FOX_W
FL_PAD = 256
ROPE_THETA = 10000.0
RMS_EPS = 1e-6
ATT_SCALE = 0.125
NEG = -1e30

ADAM_LR = 0.001
ADAM_B1 = 0.9
ADAM_B2 = 0.999
ADAM_EPS = 1e-08
ADAM_WD = 0.01
ADAM_STEP = 10

FOX_FWD_BLOCKS = (1024, 1024)
FOX_BWD_BLOCKS = (1024, 512)
FOX_FWD_PAIRS = 2

LANES = 128
VMEM_LIMIT = 56 * 1024 * 1024
STEP_BYTES = 12 * 1024 * 1024


def _cparams(*sem):
    return pltpu.CompilerParams(dimension_semantics=sem, vmem_limit_bytes=VMEM_LIMIT)


def _pick(dim, pref, align=LANES):
    best = None
    t = align
    while t <= min(dim, pref):
        if dim % t == 0:
            best = t
        t += align
    return best if best is not None else dim


_DIMS = {"nn": ((1,), (0,)), "nt": ((1,), (1,)), "tn": ((0,), (0,))}


_ANY = pl.BlockSpec(memory_space=pl.ANY)


def _matmul(a, b, *, mode, name, out_dtypes, tm, tn, tk, extras=(), extra_maps=None,
            a_fn=None, epilogue=None, deps=()):
    if mode == "nn":
        (M, K), (K2, N) = a.shape, b.shape
    elif mode == "nt":
        (M, K), (N, K2) = a.shape, b.shape
    else:
        (K, M), (K2, N) = a.shape, b.shape
    assert K == K2, (name, a.shape, b.shape)
    assert M % tm == 0 and N % tn == 0 and K % tk == 0, (name, M, N, K, tm, tn, tk)
    nk = K // tk
    ne, no = len(extras), len(out_dtypes)
    dims = (_DIMS[mode], ((), ()))

    def body(*refs):
        a_ref, b_ref = refs[0], refs[1]
        ex_refs = refs[2:2 + ne]
        out_refs = refs[2 + ne + len(deps):2 + ne + len(deps) + no]

        def finish(acc):
            res = (acc,) if epilogue is None else epilogue(acc, *[e[...] for e in ex_refs])
            for o_ref, r in zip(out_refs, res):
                o_ref[...] = r.astype(o_ref.dtype)

        def product():
            av = a_ref[...]
            if a_fn is not None:
                av = a_fn(av)
            return lax.dot_general(av, b_ref[...], dims, preferred_element_type=F32)

        if nk == 1:
            finish(product())
        else:
            acc_ref = refs[-1]
            k = pl.program_id(2)

            @pl.when(k == 0)
            def _():
                acc_ref[...] = jnp.zeros_like(acc_ref)

            acc_ref[...] += product()

            @pl.when(k == nk - 1)
            def _():
                finish(acc_ref[...])

    if mode == "tn":
        a_spec = pl.BlockSpec((tk, tm), lambda i, j, k: (k, i))
    else:
        a_spec = pl.BlockSpec((tm, tk), lambda i, j, k: (i, k))
    if mode == "nt":
        b_spec = pl.BlockSpec((tn, tk), lambda i, j, k: (j, k))
    else:
        b_spec = pl.BlockSpec((tk, tn), lambda i, j, k: (k, j))
    if extra_maps is None:
        extra_maps = [lambda i, j, k: (i, j)] * ne
    ex_specs = [pl.BlockSpec((tm, tn), m) for m in extra_maps]
    out_spec = [pl.BlockSpec((tm, tn), lambda i, j, k: (i, j)) for _ in range(no)]
    res = pl.pallas_call(
        body,
        name=name,
        grid=(M // tm, N // tn, nk),
        in_specs=[a_spec, b_spec] + ex_specs + [_ANY] * len(deps),
        out_specs=out_spec,
        out_shape=[jax.ShapeDtypeStruct((M, N), d) for d in out_dtypes],
        scratch_shapes=[pltpu.VMEM((tm, tn), F32)] if nk > 1 else [],
        compiler_params=_cparams("parallel", "parallel", "arbitrary"),
    )(a, b, *extras, *deps)
    return res


def _square_bf16(t):
    tf = t.astype(F32)
    return (tf * tf).astype(BF16)


def _sigmoid(g):
    return 1.0 / (1.0 + jnp.exp(-g))


def _rms_fwd(x, gain, *, name, deps=()):
    S, D = x.shape
    tr = _pick(S, 512, 8)

    def body(x_ref, g_ref, *rest):
        h_ref = rest[-1]
        xv = x_ref[...]
        r = lax.rsqrt(jnp.mean(xv * xv, axis=-1, keepdims=True) + RMS_EPS)
        h_ref[...] = (xv * r * g_ref[...]).astype(BF16)

    return pl.pallas_call(
        body, name=name, grid=(S // tr,),
        in_specs=[pl.BlockSpec((tr, D), lambda i: (i, 0)), pl.BlockSpec((1, D), lambda i: (0, 0))] + [_ANY] * len(deps),
        out_specs=pl.BlockSpec((tr, D), lambda i: (i, 0)),
        out_shape=jax.ShapeDtypeStruct((S, D), BF16),
        compiler_params=_cparams("parallel"),
    )(x, gain, *deps)


def _rms_bwd(dh, x, gain, dres, *, name, want_bf16, deps=()):
    S, D = x.shape
    tr = _pick(S, 256, 8)

    def body(dh_ref, x_ref, g_ref, dres_ref, *rest):
        outs = rest[len(deps):]
        dx_ref, dg_ref = outs[0], outs[-1]
        xv = x_ref[...]
        r = lax.rsqrt(jnp.mean(xv * xv, axis=-1, keepdims=True) + RMS_EPS)
        xh = xv * r
        dhv = dh_ref[...]
        t = dhv * g_ref[...]
        dx = r * (t - xh * jnp.mean(t * xh, axis=-1, keepdims=True)) + dres_ref[...]
        dx_ref[...] = dx
        if want_bf16:
            outs[1][...] = dx.astype(BF16)
        part = jnp.sum(dhv * xh, axis=0, keepdims=True)

        @pl.when(pl.program_id(0) == 0)
        def _():
            dg_ref[...] = part

        @pl.when(pl.program_id(0) > 0)
        def _():
            dg_ref[...] += part

    row = pl.BlockSpec((tr, D), lambda i: (i, 0))
    vec = pl.BlockSpec((1, D), lambda i: (0, 0))
    out_shape = [jax.ShapeDtypeStruct((S, D), F32)]
    out_specs = [row]
    if want_bf16:
        out_shape.append(jax.ShapeDtypeStruct((S, D), BF16))
        out_specs.append(row)
    out_shape.append(jax.ShapeDtypeStruct((1, D), F32))
    out_specs.append(vec)
    return pl.pallas_call(
        body, name=name, grid=(S // tr,),
        in_specs=[row, row, vec, row] + [_ANY] * len(deps), out_specs=out_specs, out_shape=out_shape,
        compiler_params=_cparams("arbitrary"),
    )(dh, x, gain, dres, *deps)


def _loss_head(x3, target, gain, *, name):
    S, D = x3.shape
    tr = _pick(S, 256, 8)

    def body(x_ref, t_ref, g_ref, dx_ref, dxb_ref, dg_ref, loss_ref):
        xv = x_ref[...]
        r = lax.rsqrt(jnp.mean(xv * xv, axis=-1, keepdims=True) + RMS_EPS)
        xh = xv * r
        gv = g_ref[...]
        err = xh * gv - t_ref[...]
        lpart = jnp.zeros((1, LANES), F32) + (0.5 / D) * jnp.sum(err * err)
        dy = err * (1.0 / D)
        t = dy * gv
        dx = r * (t - xh * jnp.mean(t * xh, axis=-1, keepdims=True))
        dx_ref[...] = dx
        dxb_ref[...] = dx.astype(BF16)
        part = jnp.sum(dy * xh, axis=0, keepdims=True)

        @pl.when(pl.program_id(0) == 0)
        def _():
            dg_ref[...] = part
            loss_ref[...] = lpart

        @pl.when(pl.program_id(0) > 0)
        def _():
            dg_ref[...] += part
            loss_ref[...] += lpart

    row = pl.BlockSpec((tr, D), lambda i: (i, 0))
    vec = pl.BlockSpec((1, D), lambda i: (0, 0))
    return pl.pallas_call(
        body, name=name, grid=(S // tr,),
        in_specs=[row, row, vec],
        out_specs=[row, row, vec, pl.BlockSpec((1, LANES), lambda i: (0, 0))],
        out_shape=[jax.ShapeDtypeStruct((S, D), F32), jax.ShapeDtypeStruct((S, D), BF16),
                   jax.ShapeDtypeStruct((1, D), F32), jax.ShapeDtypeStruct((1, LANES), F32)],
        compiler_params=_cparams("arbitrary"),
    )(x3, target, gain)


def _rope_tables(pos_col, invf, *, name):
    S = pos_col.shape[0]
    tr = _pick(S, 512, 8)

    def body(p_ref, f_ref, cos_ref, sin_ref):
        ang = p_ref[...].astype(F32) * f_ref[...]
        lane = lax.broadcasted_iota(jnp.int32, (1, LANES), 1)
        first = (lane % HEAD_DIM) < HEAD_DIM // 2
        sn = jnp.sin(ang)
        cos_ref[...] = jnp.cos(ang)
        sin_ref[...] = jnp.where(first, -sn, sn)

    return pl.pallas_call(
        body, name=name, grid=(S // tr,),
        in_specs=[pl.BlockSpec((tr, 1), lambda i: (i, 0)), pl.BlockSpec((1, LANES), lambda i: (0, 0))],
        out_specs=[pl.BlockSpec((tr, LANES), lambda i: (i, 0))] * 2,
        out_shape=[jax.ShapeDtypeStruct((S, LANES), F32)] * 2,
        compiler_params=_cparams("parallel"),
    )(pos_col, invf)


def _swap_halves(t):
    lane = lax.broadcasted_iota(jnp.int32, (1, LANES), 1)
    first = (lane % HEAD_DIM) < HEAD_DIM // 2
    return jnp.where(first, pltpu.roll(t, LANES - HEAD_DIM // 2, 1), pltpu.roll(t, HEAD_DIM // 2, 1))


def _rope_fwd(proj, cos_t, sin_t, *, q_off, k_off, name):
    S = proj.shape[0]
    tr = _pick(S, 256, 8)
    nqb = SWA_Q_W // LANES

    def body(q_ref, k_ref, c_ref, s_ref, qo_ref, ko_ref):
        cv, sv = c_ref[...], s_ref[...]
        for b in range(nqb):
            t = q_ref[:, b * LANES:(b + 1) * LANES].astype(F32)
            qo_ref[:, b * LANES:(b + 1) * LANES] = (t * cv + _swap_halves(t) * sv).astype(BF16)
        t = k_ref[...].astype(F32)
        ko_ref[...] = (t * cv + _swap_halves(t) * sv).astype(BF16)

    tab = pl.BlockSpec((tr, LANES), lambda i: (i, 0))
    return pl.pallas_call(
        body, name=name, grid=(S // tr,),
        in_specs=[pl.BlockSpec((tr, SWA_Q_W), lambda i: (i, q_off // SWA_Q_W)),
                  pl.BlockSpec((tr, LANES), lambda i: (i, k_off // LANES)), tab, tab],
        out_specs=[pl.BlockSpec((tr, SWA_Q_W), lambda i: (i, 0)), tab],
        out_shape=[jax.ShapeDtypeStruct((S, SWA_Q_W), BF16), jax.ShapeDtypeStruct((S, LANES), BF16)],
        compiler_params=_cparams("parallel"),
    )(proj, proj, cos_t, sin_t)


def _rope_bwd(dq, dk_cur, dk_prev, dv_cur, dv_prev, cos_t, sin_t, *, name):
    S = dq.shape[0]
    tr = WINDOW
    nb = S // tr
    nqb = SWA_Q_W // LANES

    def body(dq_ref, kc_ref, kp_ref, vc_ref, vp_ref, c_ref, s_ref, dqo_ref, dko_ref, dvo_ref):
        cv, sv = c_ref[...], s_ref[...]
        has_next = (pl.program_id(0) + 1 < nb).astype(F32)
        for b in range(nqb):
            d = dq_ref[:, b * LANES:(b + 1) * LANES]
            dqo_ref[:, b * LANES:(b + 1) * LANES] = (d * cv + _swap_halves(d * sv)).astype(BF16)
        d = kc_ref[0] + kc_ref[1] + has_next * (kp_ref[0] + kp_ref[1])
        dko_ref[...] = (d * cv + _swap_halves(d * sv)).astype(BF16)
        dvo_ref[...] = (vc_ref[0] + vc_ref[1] + has_next * (vp_ref[0] + vp_ref[1])).astype(BF16)

    tab = pl.BlockSpec((tr, LANES), lambda i: (i, 0))
    cur = pl.BlockSpec((2, tr, LANES), lambda i: (0, i, 0))
    nxt = pl.BlockSpec((2, tr, LANES), lambda i: (0, jnp.minimum(i + 1, nb - 1), 0))
    return pl.pallas_call(
        body, name=name, grid=(nb,),
        in_specs=[pl.BlockSpec((tr, SWA_Q_W), lambda i: (i, 0)), cur, nxt, cur, nxt, tab, tab],
        out_specs=[pl.BlockSpec((tr, SWA_Q_W), lambda i: (i, 0)), tab, tab],
        out_shape=[jax.ShapeDtypeStruct((S, SWA_Q_W), BF16), jax.ShapeDtypeStruct((S, LANES), BF16),
                   jax.ShapeDtypeStruct((S, LANES), BF16)],
        compiler_params=_cparams("parallel"),
    )(dq, dk_cur, dk_prev, dv_cur, dv_prev, cos_t, sin_t)


def _dot_nt(a, b):
    return lax.dot_general(a, b, (((1,), (1,)), ((), ())), preferred_element_type=F32)


def _dot_tn(a, b):
    return lax.dot_general(a, b, (((0,), (0,)), ((), ())), preferred_element_type=F32)


def _dot_nn(a, b):
    return lax.dot_general(a, b, (((1,), (0,)), ((), ())), preferred_element_type=F32)


def _roll_half(t):
    return pltpu.roll(t.astype(F32), HEAD_DIM, 1).astype(t.dtype)


SWA_STACK = SWA_GROUP // 2


def _swa_common(hk, n, kp_ref, kc_ref, vp_ref, vc_ref):
    k2 = jnp.concatenate([kp_ref[...], kc_ref[...]], axis=0)
    v2 = jnp.concatenate([vp_ref[...], vc_ref[...]], axis=0)
    k_sw, v_sw = _roll_half(k2), _roll_half(v2)
    rows = SWA_STACK * WINDOW
    row = lax.broadcasted_iota(jnp.int32, (rows, 2 * WINDOW), 0) % WINDOW
    col = lax.broadcasted_iota(jnp.int32, (rows, 2 * WINDOW), 1)
    diff = row + WINDOW - col
    allowed = (diff >= 0) & (diff < WINDOW) & ((col >= WINDOW) | (n > 0))
    lane = lax.broadcasted_iota(jnp.int32, (1, LANES), 1)
    half = [lane < HEAD_DIM, lane >= HEAD_DIM]
    kk = [jnp.where(hk == a, k2, k_sw) for a in range(2)]
    vv = [jnp.where(hk == a, v2, v_sw) for a in range(2)]
    return allowed, half, kk, vv


def _swa_stack(ref, mask, scale=None):
    parts = []
    for t in range(SWA_STACK):
        blk = ref[:, t * LANES:(t + 1) * LANES]
        if scale is not None:
            blk = blk * jnp.asarray(scale, blk.dtype)
        parts.append(jnp.where(mask, blk, jnp.zeros_like(blk)))
    return jnp.concatenate(parts, axis=0)


def _swa_sink_column(sink_ref, hk, a):
    blk = lax.broadcasted_iota(jnp.int32, (SWA_STACK * WINDOW, 1), 0) // WINDOW
    col = jnp.zeros((SWA_STACK * WINDOW, 1), F32)
    for t in range(SWA_STACK):
        col = jnp.where(blk == t, sink_ref[hk * SWA_GROUP + 2 * t + a], col)
    return col


def _swa_probs(qm, kk, allowed, sink):
    s = jnp.where(allowed, _dot_nt(qm, kk), NEG)
    m = jnp.maximum(jnp.max(s, axis=1, keepdims=True), sink)
    e = jnp.exp(s - m)
    es = jnp.exp(sink - m)
    inv = 1.0 / (jnp.sum(e, axis=1, keepdims=True) + es)
    return e * inv, es * inv


def _swa_fwd(q_rope, k_rope, proj, sinks, *, v_off, name):
    S = q_rope.shape[0]
    nb = S // WINDOW
    gw = SWA_GROUP * HEAD_DIM

    def body(sink_ref, q_ref, kp_ref, kc_ref, vp_ref, vc_ref, o_ref):
        hk, n = pl.program_id(0), pl.program_id(1)
        allowed, half, kk, vv = _swa_common(hk, n, kp_ref, kc_ref, vp_ref, vc_ref)
        outs = []
        for a in range(2):
            qm = _swa_stack(q_ref, half[a], ATT_SCALE)
            p, _ = _swa_probs(qm, kk[a], allowed, _swa_sink_column(sink_ref, hk, a))
            outs.append(_dot_nn(p.astype(BF16), vv[a]))
        for t in range(SWA_STACK):
            rows = slice(t * WINDOW, (t + 1) * WINDOW)
            o_ref[:, t * LANES:(t + 1) * LANES] = jnp.where(half[0], outs[0][rows], outs[1][rows]).astype(BF16)

    prev = lambda hk, n: (jnp.maximum(n - 1, 0), 0)
    cur = lambda hk, n: (n, 0)
    vprev = lambda hk, n: (jnp.maximum(n - 1, 0), v_off // LANES)
    vcur = lambda hk, n: (n, v_off // LANES)
    blk = lambda m: pl.BlockSpec((WINDOW, LANES), m)
    return pl.pallas_call(
        body, name=name, grid=(2, nb),
        in_specs=[pl.BlockSpec(memory_space=pltpu.SMEM),
                  pl.BlockSpec((WINDOW, gw), lambda hk, n: (n, hk)),
                  blk(prev), blk(cur), blk(vprev), blk(vcur)],
        out_specs=pl.BlockSpec((WINDOW, gw), lambda hk, n: (n, hk)),
        out_shape=jax.ShapeDtypeStruct((S, SWA_Q_W), BF16),
        compiler_params=_cparams("parallel", "parallel"),
    )(sinks, q_rope, k_rope, k_rope, proj, proj)


def _swa_bwd(q_rope, k_rope, proj, sinks, d_o, *, v_off, name):
    S = q_rope.shape[0]
    nb = S // WINDOW
    gw = SWA_GROUP * HEAD_DIM

    def body(sink_ref, q_ref, kp_ref, kc_ref, vp_ref, vc_ref, do_ref,
             dq_ref, dkc_ref, dkp_ref, dvc_ref, dvp_ref, dsink_ref):
        hk, n = pl.program_id(0), pl.program_id(1)
        allowed, half, kk, vv = _swa_common(hk, n, kp_ref, kc_ref, vp_ref, vc_ref)
        dk_acc = jnp.zeros((2 * WINDOW, LANES), F32)
        dv_acc = jnp.zeros((2 * WINDOW, LANES), F32)
        srow = lax.broadcasted_iota(jnp.int32, (SWA_GROUP, LANES), 0)
        dsink = jnp.zeros((SWA_GROUP, LANES), F32)
        dqs = []
        for a in range(2):
            qm = _swa_stack(q_ref, half[a], ATT_SCALE)
            dom = _swa_stack(do_ref, half[a])
            p, psink = _swa_probs(qm, kk[a], allowed, _swa_sink_column(sink_ref, hk, a))
            dp = _dot_nt(dom, vv[a])
            delta = jnp.sum(p * dp, axis=1, keepdims=True)
            ds = (p * (dp - delta)).astype(BF16)
            dsk = psink * delta
            for t in range(SWA_STACK):
                dsink = dsink + jnp.where(srow == 2 * t + a, -jnp.sum(dsk[t * WINDOW:(t + 1) * WINDOW]), 0.0)
            dqs.append(_dot_nn(ds, kk[a]) * ATT_SCALE)
            dk_acc = dk_acc + _dot_tn(ds, qm)
            dv_acc = dv_acc + _dot_tn(p.astype(BF16), dom)
        for t in range(SWA_STACK):
            rows = slice(t * WINDOW, (t + 1) * WINDOW)
            dq_ref[:, t * LANES:(t + 1) * LANES] = jnp.where(half[0], dqs[0][rows], dqs[1][rows])
        lane = lax.broadcasted_iota(jnp.int32, (1, LANES), 1)
        mine = (lane >= HEAD_DIM) == (hk == 1)
        dk_t = jnp.where(mine, dk_acc + pltpu.roll(dk_acc, HEAD_DIM, 1), 0.0)
        dv_t = jnp.where(mine, dv_acc + pltpu.roll(dv_acc, HEAD_DIM, 1), 0.0)
        dkp_ref[0] = dk_t[:WINDOW]
        dkc_ref[0] = dk_t[WINDOW:]
        dvp_ref[0] = dv_t[:WINDOW]
        dvc_ref[0] = dv_t[WINDOW:]

        @pl.when(n == 0)
        def _():
            dsink_ref[0] = dsink

        @pl.when(n > 0)
        def _():
            dsink_ref[0] += dsink

    prev = lambda hk, n: (jnp.maximum(n - 1, 0), 0)
    cur = lambda hk, n: (n, 0)
    vprev = lambda hk, n: (jnp.maximum(n - 1, 0), v_off // LANES)
    vcur = lambda hk, n: (n, v_off // LANES)
    blk = lambda m: pl.BlockSpec((WINDOW, LANES), m)
    qblk = pl.BlockSpec((WINDOW, gw), lambda hk, n: (n, hk))
    part = pl.BlockSpec((1, WINDOW, LANES), lambda hk, n: (hk, n, 0))
    part_shape = jax.ShapeDtypeStruct((2, S, LANES), F32)
    return pl.pallas_call(
        body, name=name, grid=(2, nb),
        in_specs=[pl.BlockSpec(memory_space=pltpu.SMEM), qblk, blk(prev), blk(cur), blk(vprev), blk(vcur), qblk],
        out_specs=[qblk, part, part, part, part,
                   pl.BlockSpec((1, SWA_GROUP, LANES), lambda hk, n: (hk, 0, 0))],
        out_shape=[jax.ShapeDtypeStruct((S, SWA_Q_W), F32), part_shape, part_shape, part_shape, part_shape,
                   jax.ShapeDtypeStruct((2, SWA_GROUP, LANES), F32)],
        compiler_params=_cparams("parallel", "arbitrary"),
    )(sinks, q_rope, k_rope, k_rope, proj, proj, d_o)


def _fox_prep(z_t, bias_col, *, name):
    H, S = z_t.shape
    tb = _pick(S, 512)

    def body(z_ref, b_ref, o_ref, carry_ref):
        @pl.when(pl.program_id(0) == 0)
        def _():
            carry_ref[...] = jnp.zeros_like(carry_ref)

        zz = z_ref[...] + b_ref[...]
        t = jnp.exp(-jnp.abs(zz))
        log1p = jnp.where(t < 1e-2, t * (1.0 - t * (0.5 - t * (1.0 / 3.0))), jnp.log(1.0 + t))
        logf = jnp.minimum(zz, 0.0) - log1p
        r = lax.broadcasted_iota(jnp.int32, (tb, tb), 0)
        c = lax.broadcasted_iota(jnp.int32, (tb, tb), 1)
        tri = (r <= c).astype(BF16)
        hi = logf.astype(BF16)
        r1 = logf - hi.astype(F32)
        mid = r1.astype(BF16)
        lo = (r1 - mid.astype(F32)).astype(BF16)
        cs = _dot_nn(hi, tri) + _dot_nn(mid, tri) + _dot_nn(lo, tri) + carry_ref[:, 0:1]
        o_ref[...] = -cs
        carry_ref[...] = jnp.zeros_like(carry_ref) + cs[:, tb - 1:tb]

    return pl.pallas_call(
        body, name=name, grid=(S // tb,),
        in_specs=[pl.BlockSpec((H, tb), lambda i: (0, i)), pl.BlockSpec((H, 1), lambda i: (0, 0))],
        out_specs=pl.BlockSpec((H, tb), lambda i: (0, i)),
        out_shape=jax.ShapeDtypeStruct((H, S), F32),
        scratch_shapes=[pltpu.VMEM((H, LANES), F32)],
        compiler_params=_cparams("arbitrary"),
    )(z_t, bias_col)


def _fox_post(drow, dcol, z_t, bias_col, *, name):
    H, S = z_t.shape
    tb = _pick(S, 512)
    nb = S // tb

    def body(dr_ref, d_ref, z_ref, b_ref, dz_ref, db_ref, carry_ref):
        @pl.when(pl.program_id(0) == 0)
        def _():
            carry_ref[...] = jnp.zeros_like(carry_ref)
            db_ref[...] = jnp.zeros_like(db_ref)

        dc = dr_ref[...] - d_ref[...]
        r = lax.broadcasted_iota(jnp.int32, (tb, tb), 0)
        c = lax.broadcasted_iota(jnp.int32, (tb, tb), 1)
        tri = (r >= c).astype(BF16)
        hi = dc.astype(BF16)
        r1 = dc - hi.astype(F32)
        mid = r1.astype(BF16)
        lo = (r1 - mid.astype(F32)).astype(BF16)
        dlogf = _dot_nn(hi, tri) + _dot_nn(mid, tri) + _dot_nn(lo, tri) + carry_ref[:, 0:1]
        carry_ref[...] = jnp.zeros_like(carry_ref) + dlogf[:, 0:1]
        dz = dlogf * _sigmoid(-(z_ref[...] + b_ref[...]))
        dz_ref[...] = dz
        db_ref[...] += jnp.sum(dz, axis=1, keepdims=True)

    rev = lambda i: (0, nb - 1 - i)
    return pl.pallas_call(
        body, name=name, grid=(nb,),
        in_specs=[pl.BlockSpec((H, tb), rev), pl.BlockSpec((H, tb), rev), pl.BlockSpec((H, tb), rev),
                  pl.BlockSpec((H, 1), lambda i: (0, 0))],
        out_specs=[pl.BlockSpec((H, tb), rev), pl.BlockSpec((H, LANES), lambda i: (0, 0))],
        out_shape=[jax.ShapeDtypeStruct((H, S), F32), jax.ShapeDtypeStruct((H, LANES), F32)],
        scratch_shapes=[pltpu.VMEM((H, LANES), F32)],
        compiler_params=_cparams("arbitrary"),
    )(drow, dcol, z_t, bias_col)


def _fox_blocks(S):
    cap = max(LANES, S // 4)
    return (min(FOX_FWD_BLOCKS[0], cap), min(FOX_FWD_BLOCKS[1], cap)), \
           (min(FOX_BWD_BLOCKS[0], cap), min(FOX_BWD_BLOCKS[1], cap))


def _key_bias_blocks(negc, bk):
    H, S = negc.shape
    return negc.reshape(H // 2, 2, S // bk, bk).transpose(0, 2, 1, 3)


def _fox_fwd(proj, negc4, *, q_off, k_off, v_off, bq, bk, name):
    S = proj.shape[0]
    nq, nk = S // bq, S // bk
    npair = FOX_HEADS // 2
    assert bq % bk == 0 or bk % bq == 0
    nmask = max(1, bq // bk)

    gp = FOX_FWD_PAIRS
    gw = gp * LANES
    assert q_off % gw == 0 and k_off % gw == 0 and v_off % gw == 0 and npair % gp == 0

    def body(q_ref, k_ref, v_ref, nc_ref, o_ref, lse_ref):
        i = pl.program_id(1)
        lane = lax.broadcasted_iota(jnp.int32, (1, LANES), 1)
        half = [lane < HEAD_DIM, lane >= HEAD_DIM]
        qh = []
        for g in range(gp):
            q2 = q_ref[:, g * LANES:(g + 1) * LANES] * jnp.asarray(ATT_SCALE, BF16)
            qh += [jnp.where(half[h], q2, jnp.zeros_like(q2)) for h in range(2)]
        row = lax.broadcasted_iota(jnp.int32, (bq, bk), 0)
        col = lax.broadcasted_iota(jnp.int32, (bq, bk), 1)
        rel = row - col
        nfull = (i * bq) // bk

        spare = [HEAD_DIM, 0]
        ones_lane = [lane == spare[h] for h in range(2)]

        def step(j, carry, masked):
            start = pl.multiple_of(j * bk, bk)
            new = []
            for g in range(gp):
                ks = k_ref[pl.ds(start, bk), g * LANES:(g + 1) * LANES]
                vs = v_ref[pl.ds(start, bk), g * LANES:(g + 1) * LANES]
                nb = nc_ref[g, j]
                for h in range(2):
                    m, acc = carry[4 * g + 2 * h:4 * g + 2 * h + 2]
                    vh = jnp.where(half[h], vs, jnp.where(ones_lane[h], jnp.ones_like(vs), jnp.zeros_like(vs)))
                    s = _dot_nt(qh[2 * g + h], ks) + nb[h:h + 1, :]
                    if masked:
                        s = jnp.where(rel >= j * bk - i * bq, s, NEG)
                    m_new = jnp.maximum(m, jnp.max(s, axis=1, keepdims=True))
                    p = jnp.exp(s - m_new).astype(BF16)
                    acc = jnp.exp(m - m_new) * acc + _dot_nn(p, vh)
                    new += [m_new, acc]
            return tuple(new)

        init = (jnp.full((bq, 1), NEG, F32), jnp.zeros((bq, LANES), F32)) * (2 * gp)
        carry = lax.fori_loop(0, nfull, lambda j, c: step(j, c, False), init)
        for t in range(nmask):
            carry = step(nfull + t, carry, True)
        for g in range(gp):
            outs, lses = [], []
            for h in range(2):
                m, acc = carry[4 * g + 2 * h:4 * g + 2 * h + 2]
                l = acc[:, spare[h]:spare[h] + 1]
                outs.append(acc * (1.0 / l))
                lses.append(m + jnp.log(l))
            o_ref[:, g * LANES:(g + 1) * LANES] = jnp.where(half[0], outs[0], outs[1]).astype(BF16)
            lse_ref[g] = jnp.where(half[0], lses[0], lses[1])

    seq = lambda off: pl.BlockSpec((S, gw), lambda hp, i: (0, off // gw + hp))
    return pl.pallas_call(
        body, name=name, grid=(npair // gp, nq),
        in_specs=[pl.BlockSpec((bq, gw), lambda hp, i: (i, q_off // gw + hp)), seq(k_off), seq(v_off),
                  pl.BlockSpec((gp, nk, 2, bk), lambda hp, i: (hp, 0, 0, 0))],
        out_specs=[pl.BlockSpec((bq, gw), lambda hp, i: (i, hp)),
                   pl.BlockSpec((gp, bq, LANES), lambda hp, i: (hp, i, 0))],
        out_shape=[jax.ShapeDtypeStruct((S, FOX_W), BF16), jax.ShapeDtypeStruct((npair, S, LANES), F32)],
        compiler_params=_cparams("parallel", "parallel"),
    )(proj, proj, proj, negc4)


def _fox_bwd(proj, negc4, o, lse, d_o, q_t, do_t, *, q_off, k_off, v_off, bq, bk, name, deps=()):
    S = proj.shape[0]
    nq, nk = S // bq, S // bk
    npair = FOX_HEADS // 2
    assert bq % bk == 0 or bk % bq == 0
    nmask = max(1, bk // bq)

    def body(q_ref, k_ref, v_ref, nc_ref, o_ref, lse_ref, do_ref, qt_ref, dot_ref, *rest):
        dqo_ref, dk_ref, dv_ref, dn_ref, dr_ref, delta_ref, rs_ref, dq_ref = rest[len(deps):]
        j = pl.program_id(1)
        lane = lax.broadcasted_iota(jnp.int32, (1, LANES), 1)
        half = [lane < HEAD_DIM, lane >= HEAD_DIM]
        spare = [HEAD_DIM, 0]
        ones_lane = [lane == spare[h] for h in range(2)]
        srow = lax.broadcasted_iota(jnp.int32, (LANES, 1), 0)
        rhalf = [srow < HEAD_DIM, srow >= HEAD_DIM]
        ones_row = [srow == spare[h] for h in range(2)]
        k2, v2 = k_ref[...], v_ref[...]
        one_k = jnp.ones_like(k2)
        kh = [jnp.where(half[h], k2, jnp.where(ones_lane[h], one_k, jnp.zeros_like(k2))) for h in range(2)]
        nb = nc_ref[0, 0]
        row = lax.broadcasted_iota(jnp.int32, (bq, bk), 0)
        col = lax.broadcasted_iota(jnp.int32, (bq, bk), 1)
        rel = row - col
        i_first = (j * bk) // bq

        @pl.when(j == 0)
        def _():
            dq_ref[...] = jnp.zeros_like(dq_ref)
            rs_ref[...] = jnp.zeros_like(rs_ref)
            for b in range(nq):
                prod = do_ref[b * bq:(b + 1) * bq, :].astype(F32) * o_ref[b * bq:(b + 1) * bq, :].astype(F32)
                d0 = jnp.sum(jnp.where(half[0], prod, 0.0), axis=1, keepdims=True)
                d1 = jnp.sum(jnp.where(half[1], prod, 0.0), axis=1, keepdims=True)
                delta_ref[b * bq:(b + 1) * bq, :] = jnp.where(half[0], d0, d1)

        def step(i, carry, masked):
            dkt_a, dkt_b, dvt = carry
            dkts = [dkt_a, dkt_b]
            start = pl.multiple_of(i * bq, bq)
            q2 = q_ref[pl.ds(start, bq), :] * jnp.asarray(ATT_SCALE, BF16)
            do2 = do_ref[pl.ds(start, bq), :]
            qt = qt_ref[i] * jnp.asarray(ATT_SCALE, BF16)
            dot = dot_ref[i]
            lse2 = lse_ref[0, pl.ds(start, bq), :]
            del2 = delta_ref[pl.ds(start, bq), :]
            dqf = []
            for h in range(2):
                qm = jnp.where(half[h], q2, jnp.zeros_like(q2))
                dom = jnp.where(half[h], do2, jnp.zeros_like(do2))
                qtm = jnp.where(rhalf[h], qt, jnp.where(ones_row[h], jnp.ones_like(qt), jnp.zeros_like(qt)))
                dotm = jnp.where(rhalf[h], dot, jnp.zeros_like(dot))
                c0 = h * HEAD_DIM
                p = jnp.exp(_dot_nt(qm, k2) + nb[h:h + 1, :] - lse2[:, c0:c0 + 1])
                if masked:
                    p = jnp.where(rel >= j * bk - i * bq, p, 0.0)
                dp = _dot_nt(dom, v2)
                dsb = (p * (dp - del2[:, c0:c0 + 1])).astype(BF16)
                dvt = dvt + _dot_nn(dotm, p.astype(BF16))
                dkts[h] = dkts[h] + _dot_nn(qtm, dsb)
                dqf.append(_dot_nn(dsb, kh[h]))
            dq_ref[pl.ds(start, bq), :] += jnp.where(half[0], dqf[0], dqf[1]) * ATT_SCALE
            rs_ref[pl.ds(start, bq), :] += jnp.where(ones_lane[0], dqf[0], jnp.where(ones_lane[1], dqf[1], 0.0))
            return dkts[0], dkts[1], dvt

        zero = jnp.zeros((LANES, bk), F32)
        carry = (zero, zero, zero)
        for t in range(nmask):
            carry = step(i_first + t, carry, True)
        dkt_a, dkt_b, dvt = lax.fori_loop(i_first + nmask, nq, lambda i, c: step(i, c, False), carry)
        dk_ref[...] = jnp.where(rhalf[0], dkt_a, dkt_b).T.astype(BF16)
        dv_ref[...] = dvt.T.astype(BF16)
        dn_ref[0, 0] = jnp.concatenate([dkt_a[spare[0]:spare[0] + 1], dkt_b[spare[1]:spare[1] + 1]], axis=0)

        @pl.when(j == nk - 1)
        def _():
            dqo_ref[...] = dq_ref[...].astype(BF16)
            for b in range(nq):
                t = rs_ref[b * bq:(b + 1) * bq, :].T
                dr_ref[0, b] = jnp.concatenate([t[spare[0]:spare[0] + 1], t[spare[1]:spare[1] + 1]], axis=0)

    once = pl.Buffered(1)
    seq = lambda off: pl.BlockSpec((S, LANES), lambda hp, j: (0, off // LANES + hp), pipeline_mode=once)
    blk = lambda off: pl.BlockSpec((bk, LANES), lambda hp, j: (j, off // LANES + hp))
    nc = pl.BlockSpec((1, 1, 2, bk), lambda hp, j: (hp, j, 0, 0))
    tsp = pl.BlockSpec((nq, LANES, bq), lambda hp, j: (0, hp, 0), pipeline_mode=once)
    return pl.pallas_call(
        body, name=name, grid=(npair, nk),
        in_specs=[seq(q_off), blk(k_off), blk(v_off), nc, seq(0),
                  pl.BlockSpec((1, S, LANES), lambda hp, j: (hp, 0, 0), pipeline_mode=once), seq(0),
                  tsp, tsp] + [_ANY] * len(deps),
        out_specs=[pl.BlockSpec((S, LANES), lambda hp, j: (0, hp)), blk(0), blk(0), nc,
                   pl.BlockSpec((1, nq, 2, bq), lambda hp, j: (hp, 0, 0, 0))],
        out_shape=[jax.ShapeDtypeStruct((S, FOX_W), BF16), jax.ShapeDtypeStruct((S, FOX_W), BF16),
                   jax.ShapeDtypeStruct((S, FOX_W), BF16), jax.ShapeDtypeStruct((npair, nk, 2, bk), F32),
                   jax.ShapeDtypeStruct((npair, nq, 2, bq), F32)],
        scratch_shapes=[pltpu.VMEM((S, LANES), F32), pltpu.VMEM((S, LANES), F32), pltpu.VMEM((S, LANES), F32)],
        compiler_params=_cparams("parallel", "arbitrary"),
    )(proj, proj, proj, negc4, o, lse, d_o, q_t, do_t, *deps)


def _exchange(arrs, *, gather, name):
    n = len(arrs)
    npeer = N_DEV - 1

    def body(*refs):
        ins, outs = refs[:n], refs[n:2 * n]
        send_sems, recv_sems, loc_sems = refs[2 * n:]
        x, y, c = lax.axis_index("x"), lax.axis_index("y"), lax.axis_index("c")
        me = 4 * x + 2 * y + c
        peers = []
        for k in range(1, N_DEV):
            px = 1 - x if k & 4 else x
            py = 1 - y if k & 2 else y
            pc = 1 - c if k & 1 else c
            peers.append(((px, py, pc), 4 * px + 2 * py + pc))

        def remote(w, k):
            dev, idx = peers[k]
            src = ins[w] if gather else ins[w].at[idx]
            return pltpu.make_async_remote_copy(
                src_ref=src, dst_ref=outs[w].at[me],
                send_sem=send_sems.at[w * npeer + k], recv_sem=recv_sems.at[w * npeer + k],
                device_id=dev, device_id_type=pl.DeviceIdType.MESH)

        def arrival(w, k):
            dev, idx = peers[k]
            src = ins[w] if gather else ins[w].at[idx]
            return pltpu.make_async_remote_copy(
                src_ref=src, dst_ref=outs[w].at[idx],
                send_sem=send_sems.at[w * npeer + k], recv_sem=recv_sems.at[w * npeer + k],
                device_id=dev, device_id_type=pl.DeviceIdType.MESH)

        local = []
        for w in range(n):
            for k in range(npeer):
                remote(w, k).start()
            cp = pltpu.make_async_copy(ins[w] if gather else ins[w].at[me], outs[w].at[me], loc_sems.at[w])
            cp.start()
            local.append(cp)
        for w in range(n):
            for k in range(npeer):
                arrival(w, k).wait_recv()
        for w in range(n):
            for k in range(npeer):
                remote(w, k).wait_send()
            local[w].wait()

    hbm = pl.BlockSpec(memory_space=pl.ANY)
    out_shape = [jax.ShapeDtypeStruct((N_DEV,) + (a.shape if gather else a.shape[1:]), a.dtype) for a in arrs]
    return pl.pallas_call(
        body, name=name,
        in_specs=[hbm] * n, out_specs=[hbm] * n, out_shape=out_shape,
        scratch_shapes=[pltpu.SemaphoreType.DMA((n * npeer,)), pltpu.SemaphoreType.DMA((n * npeer,)),
                        pltpu.SemaphoreType.DMA((n,))],
        compiler_params=pltpu.CompilerParams(has_side_effects=True),
    )(*arrs)


def _gather_two_level(shard, *, name):
    def body(x_ref, out_ref, send_sems, recv_sems, local_sem):
        x, y, c = lax.axis_index("x"), lax.axis_index("y"), lax.axis_index("c")
        me, sibling = (x, y, c), (x, y, 1 - c)
        chips = [(1 - x, y), (x, 1 - y), (1 - x, 1 - y)]

        def slot(px, py, pc):
            return out_ref.at[4 * px + 2 * py + pc]

        def copy(k, block, to, src=None):
            return pltpu.make_async_remote_copy(
                src_ref=slot(*block) if src is None else src, dst_ref=slot(*block),
                send_sem=send_sems.at[k], recv_sem=recv_sems.at[k],
                device_id=to, device_id_type=pl.DeviceIdType.MESH)

        mine = pltpu.make_async_copy(x_ref, slot(*me), local_sem)
        mine.start()
        first = [copy(0, me, sibling, src=x_ref)]
        first += [copy(1 + j, me, (*chip, c), src=x_ref) for j, chip in enumerate(chips)]
        for cp in first:
            cp.start()
        passed = [copy(4 + j, (*chip, c), sibling) for j, chip in enumerate(chips)]
        for j, chip in enumerate(chips):
            copy(1 + j, (*chip, c), me).wait_recv()
            passed[j].start()
        copy(0, sibling, me).wait_recv()
        for j, chip in enumerate(chips):
            copy(4 + j, (*chip, 1 - c), me).wait_recv()
        for cp in first + passed:
            cp.wait_send()
        mine.wait()

    return pl.pallas_call(
        body, name=name,
        in_specs=[_ANY], out_specs=_ANY,
        out_shape=jax.ShapeDtypeStruct((N_DEV,) + shard.shape, shard.dtype),
        scratch_shapes=[pltpu.SemaphoreType.DMA((N_DEV - 1,)), pltpu.SemaphoreType.DMA((N_DEV - 1,)),
                        pltpu.SemaphoreType.DMA],
        compiler_params=pltpu.CompilerParams(has_side_effects=True),
    )(shard)


_HBM = pl.BlockSpec(memory_space=pltpu.HBM)
_SEM = pl.BlockSpec(memory_space=pltpu.SEMAPHORE)
_EFFECT = pltpu.SideEffectType.DATAFLOW_SIDE_EFFECTING
NPEER = N_DEV - 1


def _peer_table():
    x, y, c = lax.axis_index("x"), lax.axis_index("y"), lax.axis_index("c")
    peers = []
    for k in range(1, N_DEV):
        px = 1 - x if k & 4 else x
        py = 1 - y if k & 2 else y
        pc = 1 - c if k & 1 else c
        peers.append(((px, py, pc), 4 * px + 2 * py + pc))
    return 4 * x + 2 * y + c, peers


def _split_copy(ins, lands, send_sems, recv_sems, gather, me, peers, w, k, arriving):
    dev, idx = peers[k]
    return pltpu.make_async_remote_copy(
        src_ref=ins[w] if gather else ins[w].at[idx],
        dst_ref=lands[w].at[idx if arriving else me],
        send_sem=send_sems.at[w * NPEER + k], recv_sem=recv_sems.at[w * NPEER + k],
        device_id=dev, device_id_type=pl.DeviceIdType.MESH)


def _exchange_start(arrs, *, gather, name, deps=()):
    n = len(arrs)
    land_shapes = [(N_DEV,) + (a.shape if gather else a.shape[1:]) for a in arrs]

    def body(*refs):
        ins, lands = refs[:n], refs[n:2 * n]
        send_sems, recv_sems = refs[2 * n + len(deps)], refs[2 * n + len(deps) + 1]
        token = refs[-1]
        me, peers = _peer_table()
        for w in range(n):
            for k in range(NPEER):
                _split_copy(ins, lands, send_sems, recv_sems, gather, me, peers, w, k, False).start()
        token[...] = jnp.zeros_like(token)

    out_shape = ([pltpu.SemaphoreType.DMA((n * NPEER,)), pltpu.SemaphoreType.DMA((n * NPEER,))]
                 + [pltpu.HBM(a.shape, a.dtype) for a in arrs]
                 + [pltpu.HBM(s, a.dtype) for s, a in zip(land_shapes, arrs)]
                 + [jax.ShapeDtypeStruct((8, LANES), F32)])
    res = pl.pallas_call(
        body, name=name,
        in_specs=[_HBM] * (2 * n) + [_ANY] * len(deps),
        out_specs=[_SEM, _SEM] + [_HBM] * (2 * n) + [pl.BlockSpec(memory_space=pltpu.VMEM)],
        out_shape=out_shape,
        input_output_aliases={i: 2 + i for i in range(2 * n)},
        compiler_params=pltpu.CompilerParams(has_side_effects=_EFFECT),
    )(*[pltpu.with_memory_space_constraint(a, pltpu.HBM) for a in arrs],
      *[pltpu.with_memory_space_constraint(lax.empty(s, a.dtype), pltpu.HBM) for s, a in zip(land_shapes, arrs)],
      *deps)
    return (n, gather, res[0], res[1], res[2:2 + n], res[2 + n:2 + 2 * n]), res[-1]


def _exchange_wait(handle, after, *, name):
    n, gather, send_sems, recv_sems, ins_thru, lands_thru = handle

    def body(*refs):
        ins, lands = refs[:n], refs[n:2 * n]
        send_s, recv_s = refs[2 * n], refs[2 * n + 1]
        me, peers = _peer_table()
        for w in range(n):
            for k in range(NPEER):
                _split_copy(ins, lands, send_s, recv_s, gather, me, peers, w, k, False).wait_send()
                _split_copy(ins, lands, send_s, recv_s, gather, me, peers, w, k, True).wait_recv()

    res = pl.pallas_call(
        body, name=name,
        in_specs=[_HBM] * (2 * n) + [_SEM, _SEM, pl.BlockSpec(memory_space=pl.ANY)],
        out_specs=[_HBM] * (2 * n),
        out_shape=[pltpu.HBM(a.shape, a.dtype) for a in list(ins_thru) + list(lands_thru)],
        input_output_aliases={i: i for i in range(2 * n)},
        compiler_params=pltpu.CompilerParams(has_side_effects=_EFFECT),
    )(*ins_thru, *lands_thru, send_sems, recv_sems, after)
    return res[:n], res[n:2 * n]


def _ordered_sum(s_ref, own_ref):
    if own_ref is None:
        blocks = [s_ref[q].astype(F32) for q in range(N_DEV)]
    else:
        me = 4 * lax.axis_index("x") + 2 * lax.axis_index("y") + lax.axis_index("c")
        own = own_ref[...]
        blocks = [jnp.where(me == q, own, s_ref[q]).astype(F32) for q in range(N_DEV)]
    acc = blocks[0]
    for b in blocks[1:]:
        acc = acc + b
    return acc


def _sum8(stack, own, *, name):
    _, R, C = stack.shape
    if R % 8 == 0:
        tr, tc = _pick(R, max(8, STEP_BYTES // (C * 4 * (N_DEV + 2))), 8), C
    else:
        tr, tc = R, _pick(C, max(LANES, STEP_BYTES // (R * 4 * (N_DEV + 2))))

    def body(s_ref, own_ref, o_ref):
        o_ref[...] = _ordered_sum(s_ref, own_ref)

    blk = pl.BlockSpec((tr, tc), lambda i, j: (i, j))
    return pl.pallas_call(
        body, name=name, grid=(R // tr, C // tc),
        in_specs=[pl.BlockSpec((N_DEV, tr, tc), lambda i, j: (0, i, j)), blk],
        out_specs=blk,
        out_shape=jax.ShapeDtypeStruct((R, C), F32),
        compiler_params=_cparams("parallel", "parallel"),
    )(stack, own)


def _adamw_math(w, g, m, v):
    m = ADAM_B1 * m + (1.0 - ADAM_B1) * g
    v = ADAM_B2 * v + (1.0 - ADAM_B2) * (g * g)
    m_hat = m / (1.0 - ADAM_B1 ** ADAM_STEP)
    v_hat = v / (1.0 - ADAM_B2 ** ADAM_STEP)
    delta = -ADAM_LR * (m_hat / (jnp.sqrt(v_hat) + ADAM_EPS) + ADAM_WD * w)
    return delta, m, v


def _adamw(w, g, m, v, *, name, stacked, own=None):
    R, C = w.shape
    tr = _pick(R, max(8, STEP_BYTES // (C * 4 * (8 + (N_DEV if stacked else 1)))), 8)
    has_own = own is not None

    def body(w_ref, g_ref, m_ref, v_ref, *rest):
        go_ref, d_ref, mo_ref, vo_ref = rest[-4:]
        g = _ordered_sum(g_ref, rest[0] if has_own else None) if stacked else g_ref[...]
        delta, m2, v2 = _adamw_math(w_ref[...], g, m_ref[...], v_ref[...])
        go_ref[...] = g
        d_ref[...] = delta
        mo_ref[...] = m2
        vo_ref[...] = v2

    row = pl.BlockSpec((tr, C), lambda i: (i, 0))
    g_spec = pl.BlockSpec((N_DEV, tr, C), lambda i: (0, i, 0)) if stacked else row
    return pl.pallas_call(
        body, name=name, grid=(R // tr,),
        in_specs=[row, g_spec, row, row] + [row] * has_own, out_specs=[row] * 4,
        out_shape=[jax.ShapeDtypeStruct((R, C), F32)] * 4,
        compiler_params=_cparams("parallel"),
    )(w, g, m, v, *([own] if has_own else []))


def kernel(x, positions, attn_norm, w_in, fox_f_bias, swa_sinks, w_branch_swa, w_branch_fox, w_out, mlp_norm, w_up, w_down, final_norm, loss_target, m_attn_norm, m_w_in, m_fox_f_bias, m_swa_sinks, m_w_branch_swa, m_w_branch_fox, m_w_out, m_mlp_norm, m_w_up, m_w_down, m_final_norm, v_attn_norm, v_w_in, v_fox_f_bias, v_swa_sinks, v_w_branch_swa, v_w_branch_fox, v_w_out, v_mlp_norm, v_w_up, v_w_down, v_final_norm):
    S, D = x.shape[1], x.shape[2]
    DFF = w_up.shape[2] * N_DEV
    d_in = w_in.shape[2] * N_DEV
    assert d_in == QKV_W + FOX_HEADS + 2 * D and (2 * D) % SWA_Q_W == 0 and S % (4 * LANES) == 0
    q_off = 2 * D
    k_off = q_off + SWA_Q_W
    v_off = k_off + SWA_KV_W
    fq_off = v_off + SWA_KV_W
    fk_off = fq_off + FOX_W
    fv_off = fk_off + FOX_W
    fl_off = fv_off + FOX_W
    NP = fl_off + FL_PAD
    x2d, tgt = x[0], loss_target[0]

    shards = [w_in[0].T.astype(BF16), w_branch_swa[0].T.astype(BF16), w_branch_fox[0].T.astype(BF16),
              w_out[0].astype(BF16), w_up[0].T.astype(BF16), w_down[0].astype(BF16)]
    me = 4 * lax.axis_index("x") + 2 * lax.axis_index("y") + lax.axis_index("c")

    def filled(stack, own):
        return lax.dynamic_update_slice(stack, own[None], (me,) + (0,) * own.ndim)

    g_in = _gather_two_level(shards[0], name="gather_w_in")
    h_rest, tok_rest = _exchange_start(shards[1:], gather=True, name="gather_rest_start", deps=[g_in])

    tm = _pick(S, 1024)
    td = _pick(D, 1024)
    tf = _pick(DFF, 1024)
    tnp = _pick(NP, 1024)

    h1 = _rms_fwd(x2d, attn_norm, name="rms1", deps=[tok_rest])
    w_in_t = g_in.reshape(d_in, D)
    w_in_p = jnp.concatenate([w_in_t[QKV_W + FOX_HEADS:], w_in_t[:QKV_W], w_in_t[QKV_W:QKV_W + FOX_HEADS],
                              jnp.zeros((FL_PAD - FOX_HEADS, D), BF16)], axis=0)
    w_fl_t = w_in_t[QKV_W:QKV_W + FOX_HEADS]
    proj, = _matmul(h1, w_in_p, mode="nt", name="mm_in", out_dtypes=[BF16], tm=_pick(S, 2048), tn=tnp, tk=D)
    z_t, = _matmul(w_fl_t, h1, mode="nt", name="mm_flogit", out_dtypes=[F32],
                   tm=FOX_HEADS, tn=_pick(S, 2048), tk=D)
    bias_col = fox_f_bias.reshape(FOX_HEADS, 1)
    negc = _fox_prep(z_t, bias_col, name="fox_prep")
    (fbq, fbk), (bbq, bbk) = _fox_blocks(S)
    inv_freq = ROPE_THETA ** (-jnp.arange(0, HEAD_DIM, 2, dtype=F32) / HEAD_DIM)
    invf = jnp.tile(inv_freq, LANES // (HEAD_DIM // 2)).reshape(1, LANES)
    cos_t, sin_t = _rope_tables(positions.reshape(S, 1), invf, name="rope_tables")
    q_rope, k_rope = _rope_fwd(proj, cos_t, sin_t, q_off=q_off, k_off=k_off, name="rope_fwd")
    sinks = swa_sinks.reshape(-1)
    o_a = _swa_fwd(q_rope, k_rope, proj, sinks, v_off=v_off, name="swa_fwd")
    o_b, lse = _fox_fwd(proj, _key_bias_blocks(negc, fbk), q_off=fq_off, k_off=fk_off, v_off=fv_off,
                        bq=fbq, bk=fbk, name="fox_fwd")
    s_rest, g_rest = _exchange_wait(h_rest, o_b, name="gather_rest_wait")
    g_bs, g_bf, g_o, g_up, g_dn = [filled(g, s) for g, s in zip(g_rest, s_rest)]
    w_bs_t = g_bs.reshape(D, SWA_Q_W)
    w_bf_t = g_bf.reshape(D, FOX_W)
    w_o = g_o.reshape(D, D)
    w_up_t = g_up.reshape(DFF, D)
    w_dn = g_dn.reshape(DFF, D)
    ya, = _matmul(o_a, w_bs_t, mode="nt", name="mm_branch_swa", out_dtypes=[BF16], tm=tm, tn=td, tk=SWA_Q_W)
    gate_maps = [lambda i, j, k: (i, j), lambda i, j, k: (i, j), lambda i, j, k: (i, j + D // td)]

    def merge_epi(acc, ya_t, ga_t, gb_t):
        merged = _sigmoid(ga_t.astype(F32)) * ya_t.astype(F32) + _sigmoid(gb_t.astype(F32)) * acc
        return acc, merged

    yb, merged = _matmul(o_b, w_bf_t, mode="nt", name="mm_branch_fox", out_dtypes=[BF16, BF16],
                         tm=tm, tn=td, tk=FOX_W, extras=[ya, proj, proj], extra_maps=gate_maps,
                         epilogue=merge_epi)
    x_mid, = _matmul(merged, w_o, mode="nn", name="mm_out", out_dtypes=[F32], tm=tm, tn=td, tk=D,
                     extras=[x2d], epilogue=lambda acc, r: (acc + r,))
    h2 = _rms_fwd(x_mid, mlp_norm, name="rms2")
    u, = _matmul(h2, w_up_t, mode="nt", name="mm_up", out_dtypes=[BF16], tm=_pick(S, 2048), tn=tf, tk=D,
                 epilogue=lambda acc: (jnp.maximum(acc, 0.0),))
    x_fin, = _matmul(u, w_dn, mode="nn", name="mm_down", out_dtypes=[F32], tm=tm, tn=td, tk=_pick(DFF, 2048),
                     a_fn=_square_bf16, extras=[x_mid], epilogue=lambda acc, r: (acc + r,))

    dx3, dx3b, dg3, loss_part = _loss_head(x_fin, tgt, final_norm.reshape(1, D), name="loss_head")
    d_up, = _matmul(dx3b, w_dn, mode="nt", name="mm_d_act", out_dtypes=[BF16], tm=_pick(S, 2048), tn=tf, tk=D,
                    extras=[u], epilogue=lambda acc, ut: (acc * (2.0 * ut.astype(F32)),))
    tks = _pick(S, 2048)
    dw_dn, = _matmul(u, dx3b, mode="tn", name="mm_dw_down", out_dtypes=[F32], tm=tf, tn=td, tk=tks,
                     a_fn=_square_bf16)
    dh2, = _matmul(d_up, w_up_t, mode="nn", name="mm_dh2", out_dtypes=[F32], tm=tm, tn=td, tk=_pick(DFF, 2048))
    dw_up_t, = _matmul(d_up, h2, mode="tn", name="mm_dw_up", out_dtypes=[F32], tm=tf, tn=td, tk=tks)
    h_s1, tok_s1 = _exchange_start([dw_up_t.reshape(N_DEV, DFF // N_DEV, D), dw_dn.reshape(N_DEV, DFF // N_DEV, D)],
                                   gather=False, name="scatter_mlp_start")
    dx2, dx2b, dg2 = _rms_bwd(dh2, x_mid, mlp_norm, dx3, name="rms2_bwd", want_bf16=True, deps=[tok_s1])

    def gate_bwd_epi(dm, ya_t, yb_t, ga_t, gb_t):
        sa, sb = _sigmoid(ga_t.astype(F32)), _sigmoid(gb_t.astype(F32))
        return (dm * sa, dm * sb, dm * ya_t.astype(F32) * sa * (1.0 - sa), dm * yb_t.astype(F32) * sb * (1.0 - sb))

    gmaps = [lambda i, j, k: (i, j), lambda i, j, k: (i, j), lambda i, j, k: (i, j),
             lambda i, j, k: (i, j + D // td)]
    d_ya, d_yb, d_ga, d_gb = _matmul(dx2b, w_o, mode="nt", name="mm_d_merged", out_dtypes=[BF16] * 4,
                                     tm=tm, tn=td, tk=D, extras=[ya, yb, proj, proj], extra_maps=gmaps,
                                     epilogue=gate_bwd_epi)
    dw_o, = _matmul(merged, dx2b, mode="tn", name="mm_dw_out", out_dtypes=[F32], tm=td, tn=td, tk=tks)
    d_oa, = _matmul(d_ya, w_bs_t, mode="nn", name="mm_d_oa", out_dtypes=[BF16], tm=tm, tn=SWA_Q_W, tk=D)
    d_ob, = _matmul(d_yb, w_bf_t, mode="nn", name="mm_d_ob", out_dtypes=[BF16], tm=tm, tn=FOX_W, tk=D)
    dw_bs_t, = _matmul(d_ya, o_a, mode="tn", name="mm_dw_bs", out_dtypes=[F32], tm=td, tn=SWA_Q_W, tk=tks)
    dw_bf_t, = _matmul(d_yb, o_b, mode="tn", name="mm_dw_bf", out_dtypes=[F32], tm=td, tn=FOX_W, tk=tks)
    h_s2, tok_s2 = _exchange_start([dw_bs_t.reshape(N_DEV, D // N_DEV, SWA_Q_W),
                                    dw_bf_t.reshape(N_DEV, D // N_DEV, FOX_W), dw_o.reshape(N_DEV, D // N_DEV, D)],
                                   gather=False, name="scatter_attn_start")
    def row_blocks_t(a):
        return a.reshape(S // bbq, bbq, FOX_W).transpose(0, 2, 1)

    d_fq, d_fk, d_fv, dcol4, drow4 = _fox_bwd(proj, _key_bias_blocks(negc, bbk), o_b, lse, d_ob,
                                              row_blocks_t(proj[:, fq_off:fq_off + FOX_W]), row_blocks_t(d_ob),
                                              q_off=fq_off, k_off=fk_off, v_off=fv_off, bq=bbq, bk=bbk,
                                              name="fox_bwd", deps=[tok_s2])
    dcol = dcol4.transpose(0, 2, 1, 3).reshape(FOX_HEADS, S)
    drow = drow4.transpose(0, 2, 1, 3).reshape(FOX_HEADS, S)
    dz_t, dbias_l = _fox_post(drow, dcol, z_t, bias_col, name="fox_post")
    dq_r, dk_c, dk_p, dv_c, dv_p, dsink_l = _swa_bwd(q_rope, k_rope, proj, sinks, d_oa, v_off=v_off, name="swa_bwd")
    d_aq, d_ak, d_av = _rope_bwd(dq_r, dk_c, dk_p, dv_c, dv_p, cos_t, sin_t, name="rope_bwd")
    dz_pad = jnp.pad(dz_t.T.astype(BF16), ((0, 0), (0, FL_PAD - FOX_HEADS)))
    d_proj = jnp.concatenate([d_ga, d_gb, d_aq, d_ak, d_av, d_fq, d_fk, d_fv, dz_pad], axis=1)
    tkp = _pick(NP, 2304)
    dw_in_p, = _matmul(d_proj, h1, mode="tn", name="mm_dw_in", out_dtypes=[BF16], tm=_pick(NP, 512), tn=D, tk=tks)
    dw_in_t = jnp.concatenate([dw_in_p[q_off:q_off + QKV_W], dw_in_p[fl_off:fl_off + FOX_HEADS], dw_in_p[:q_off]],
                              axis=0)
    h_s3, tok_s3 = _exchange_start([dw_in_t.reshape(N_DEV, d_in // N_DEV, D)], gather=False,
                                   name="scatter_in_start")
    dh1, = _matmul(d_proj, w_in_p, mode="nn", name="mm_dh1", out_dtypes=[F32], tm=tm, tn=td, tk=tkp, deps=[tok_s3])
    dx, dg1 = _rms_bwd(dh1, x2d, attn_norm, dx2, name="rms1_bwd", want_bf16=False)

    dbias = dbias_l[:, 0]
    dsinks = dsink_l[:, :, 0].reshape(-1)
    nsm = 3 * D + 2 * LANES
    tail = jnp.zeros((2 * LANES,), F32)
    small_g = jnp.concatenate([dg1[0], dg2[0], dg3[0],
                               tail.at[0:16].set(dbias).at[16:32].set(dsinks).at[32].set(loss_part[0, 0])])

    def pack(a_norm, b_norm, f_norm, bias, snk):
        return jnp.concatenate([a_norm[0], b_norm[0], f_norm,
                                tail.at[0:16].set(bias[0]).at[16:32].set(snk[0])]).reshape(1, nsm)

    small_stack, = _exchange([small_g.reshape(1, nsm)], gather=True, name="gather_small")
    u_sm = _adamw(pack(attn_norm, mlp_norm, final_norm, fox_f_bias, swa_sinks), small_stack,
                  pack(m_attn_norm, m_mlp_norm, m_final_norm, m_fox_f_bias, m_swa_sinks),
                  pack(v_attn_norm, v_mlp_norm, v_final_norm, v_fox_f_bias, v_swa_sinks),
                  name="adamw_small", stacked=True)
    loss = u_sm[0][0, 3 * D + 32]

    def own_of(src):
        return lax.dynamic_index_in_dim(src, me, 0, keepdims=False)

    def update_t(stack, src, w, m, v, nm):
        g = _sum8(stack, own_of(src), name="sum_" + nm).T
        return _adamw(w[0], g, m[0], v[0], name="adamw_" + nm, stacked=False)

    def update(stack, src, w, m, v, nm):
        return _adamw(w[0], stack, m[0], v[0], name="adamw_" + nm, stacked=True, own=own_of(src))

    (s_up, s_dn), (r_up, r_dn) = _exchange_wait(h_s1, u_sm[1], name="scatter_mlp_wait")
    u_up = update_t(r_up, s_up, w_up, m_w_up, v_w_up, "w_up")
    u_dn = update(r_dn, s_dn, w_down, m_w_down, v_w_down, "w_down")
    (s_bs, s_bf, s_o), (r_bs, r_bf, r_o) = _exchange_wait(h_s2, u_dn[1], name="scatter_attn_wait")
    u_bs = update_t(r_bs, s_bs, w_branch_swa, m_w_branch_swa, v_w_branch_swa, "w_bs")
    u_bf = update_t(r_bf, s_bf, w_branch_fox, m_w_branch_fox, v_w_branch_fox, "w_bf")
    u_o = update(r_o, s_o, w_out, m_w_out, v_w_out, "w_out")
    (s_w_in,), (r_in,) = _exchange_wait(h_s3, u_o[1], name="scatter_in_wait")
    u_in = update_t(r_in, s_w_in, w_in, m_w_in, v_w_in, "w_in")

    def small(kind):
        a = u_sm[kind][0]
        return dict(attn_norm=a[0:D][None], mlp_norm=a[D:2 * D][None], final_norm=a[2 * D:3 * D],
                    fox_f_bias=a[3 * D:3 * D + 16][None], swa_sinks=a[3 * D + 16:3 * D + 32][None])

    big = dict(w_in=u_in, w_branch_swa=u_bs, w_branch_fox=u_bf, w_out=u_o, w_up=u_up, w_down=u_dn)
    order = ["attn_norm", "w_in", "fox_f_bias", "swa_sinks", "w_branch_swa", "w_branch_fox", "w_out", "mlp_norm",
             "w_up", "w_down", "final_norm"]
    outs = [loss, dx[None]]
    for kind in range(4):
        sm = small(kind)
        for nm in order:
            outs.append(big[nm][kind][None] if nm in big else sm[nm])
    return tuple(outs)
```

```python
import functools

import jax
import jax.numpy as jnp
from jax import lax
from jax.experimental import pallas as pl
from jax.experimental.pallas import tpu as pltpu

F32 = jnp.float32
BF16 = jnp.bfloat16

N_DEV = 8
HEAD_DIM = 64
SWA_Q_W = 1024
SWA_KV_W = 128
SWA_GROUP = 8
WINDOW = 128
FOX_W = 1024
FOX_HEADS = 16
QKV_W = SWA_Q_W + 2 * SWA_KV_W + 3 * FOX_W
FL_PAD = 256
ROPE_THETA = 10000.0
RMS_EPS = 1e-6
ATT_SCALE = 0.125
NEG = -1e30

ADAM_LR = 0.001
ADAM_B1 = 0.9
ADAM_B2 = 0.999
ADAM_EPS = 1e-08
ADAM_WD = 0.01
ADAM_STEP = 10

FOX_FWD_BLOCKS = (1024, 1024)
FOX_BWD_BLOCKS = (1024, 512)
FOX_FWD_PAIRS = 2

LANES = 128
VMEM_LIMIT = 56 * 1024 * 1024
STEP_BYTES = 12 * 1024 * 1024


def _cparams(*sem):
    return pltpu.CompilerParams(dimension_semantics=sem, vmem_limit_bytes=VMEM_LIMIT)


def _pick(dim, pref, align=LANES):
    best = None
    t = align
    while t <= min(dim, pref):
        if dim % t == 0:
            best = t
        t += align
    return best if best is not None else dim


_DIMS = {"nn": ((1,), (0,)), "nt": ((1,), (1,)), "tn": ((0,), (0,))}


_ANY = pl.BlockSpec(memory_space=pl.ANY)


def _matmul(a, b, *, mode, name, out_dtypes, tm, tn, tk, extras=(), extra_maps=None,
            a_fn=None, epilogue=None, deps=()):
    if mode == "nn":
        (M, K), (K2, N) = a.shape, b.shape
    elif mode == "nt":
        (M, K), (N, K2) = a.shape, b.shape
    else:
        (K, M), (K2, N) = a.shape, b.shape
    assert K == K2, (name, a.shape, b.shape)
    assert M % tm == 0 and N % tn == 0 and K % tk == 0, (name, M, N, K, tm, tn, tk)
    nk = K // tk
    ne, no = len(extras), len(out_dtypes)
    dims = (_DIMS[mode], ((), ()))

    def body(*refs):
        a_ref, b_ref = refs[0], refs[1]
        ex_refs = refs[2:2 + ne]
        out_refs = refs[2 + ne + len(deps):2 + ne + len(deps) + no]

        def finish(acc):
            res = (acc,) if epilogue is None else epilogue(acc, *[e[...] for e in ex_refs])
            for o_ref, r in zip(out_refs, res):
                o_ref[...] = r.astype(o_ref.dtype)

        def product():
            av = a_ref[...]
            if a_fn is not None:
                av = a_fn(av)
            return lax.dot_general(av, b_ref[...], dims, preferred_element_type=F32)

        if nk == 1:
            finish(product())
        else:
            acc_ref = refs[-1]
            k = pl.program_id(2)

            @pl.when(k == 0)
            def _():
                acc_ref[...] = jnp.zeros_like(acc_ref)

            acc_ref[...] += product()

            @pl.when(k == nk - 1)
            def _():
                finish(acc_ref[...])

    if mode == "tn":
        a_spec = pl.BlockSpec((tk, tm), lambda i, j, k: (k, i))
    else:
        a_spec = pl.BlockSpec((tm, tk), lambda i, j, k: (i, k))
    if mode == "nt":
        b_spec = pl.BlockSpec((tn, tk), lambda i, j, k: (j, k))
    else:
        b_spec = pl.BlockSpec((tk, tn), lambda i, j, k: (k, j))
    if extra_maps is None:
        extra_maps = [lambda i, j, k: (i, j)] * ne
    ex_specs = [pl.BlockSpec((tm, tn), m) for m in extra_maps]
    out_spec = [pl.BlockSpec((tm, tn), lambda i, j, k: (i, j)) for _ in range(no)]
    res = pl.pallas_call(
        body,
        name=name,
        grid=(M // tm, N // tn, nk),
        in_specs=[a_spec, b_spec] + ex_specs + [_ANY] * len(deps),
        out_specs=out_spec,
        out_shape=[jax.ShapeDtypeStruct((M, N), d) for d in out_dtypes],
        scratch_shapes=[pltpu.VMEM((tm, tn), F32)] if nk > 1 else [],
        compiler_params=_cparams("parallel", "parallel", "arbitrary"),
    )(a, b, *extras, *deps)
    return res


def _square_bf16(t):
    tf = t.astype(F32)
    return (tf * tf).astype(BF16)


def _sigmoid(g):
    return 1.0 / (1.0 + jnp.exp(-g))


def _rms_fwd(x, gain, *, name, deps=()):
    S, D = x.shape
    tr = _pick(S, 512, 8)

    def body(x_ref, g_ref, *rest):
        h_ref = rest[-1]
        xv = x_ref[...]
        r = lax.rsqrt(jnp.mean(xv * xv, axis=-1, keepdims=True) + RMS_EPS)
        h_ref[...] = (xv * r * g_ref[...]).astype(BF16)

    return pl.pallas_call(
        body, name=name, grid=(S // tr,),
        in_specs=[pl.BlockSpec((tr, D), lambda i: (i, 0)), pl.BlockSpec((1, D), lambda i: (0, 0))] + [_ANY] * len(deps),
        out_specs=pl.BlockSpec((tr, D), lambda i: (i, 0)),
        out_shape=jax.ShapeDtypeStruct((S, D), BF16),
        compiler_params=_cparams("parallel"),
    )(x, gain, *deps)


def _rms_bwd(dh, x, gain, dres, *, name, want_bf16, deps=()):
    S, D = x.shape
    tr = _pick(S, 256, 8)

    def body(dh_ref, x_ref, g_ref, dres_ref, *rest):
        outs = rest[len(deps):]
        dx_ref, dg_ref = outs[0], outs[-1]
        xv = x_ref[...]
        r = lax.rsqrt(jnp.mean(xv * xv, axis=-1, keepdims=True) + RMS_EPS)
        xh = xv * r
        dhv = dh_ref[...]
        t = dhv * g_ref[...]
        dx = r * (t - xh * jnp.mean(t * xh, axis=-1, keepdims=True)) + dres_ref[...]
        dx_ref[...] = dx
        if want_bf16:
            outs[1][...] = dx.astype(BF16)
        part = jnp.sum(dhv * xh, axis=0, keepdims=True)

        @pl.when(pl.program_id(0) == 0)
        def _():
            dg_ref[...] = part

        @pl.when(pl.program_id(0) > 0)
        def _():
            dg_ref[...] += part

    row = pl.BlockSpec((tr, D), lambda i: (i, 0))
    vec = pl.BlockSpec((1, D), lambda i: (0, 0))
    out_shape = [jax.ShapeDtypeStruct((S, D), F32)]
    out_specs = [row]
    if want_bf16:
        out_shape.append(jax.ShapeDtypeStruct((S, D), BF16))
        out_specs.append(row)
    out_shape.append(jax.ShapeDtypeStruct((1, D), F32))
    out_specs.append(vec)
    return pl.pallas_call(
        body, name=name, grid=(S // tr,),
        in_specs=[row, row, vec, row] + [_ANY] * len(deps), out_specs=out_specs, out_shape=out_shape,
        compiler_params=_cparams("arbitrary"),
    )(dh, x, gain, dres, *deps)


def _loss_head(x3, target, gain, *, name):
    S, D = x3.shape
    tr = _pick(S, 256, 8)

    def body(x_ref, t_ref, g_ref, dx_ref, dxb_ref, dg_ref, loss_ref):
        xv = x_ref[...]
        r = lax.rsqrt(jnp.mean(xv * xv, axis=-1, keepdims=True) + RMS_EPS)
        xh = xv * r
        gv = g_ref[...]
        err = xh * gv - t_ref[...]
        lpart = jnp.zeros((1, LANES), F32) + (0.5 / D) * jnp.sum(err * err)
        dy = err * (1.0 / D)
        t = dy * gv
        dx = r * (t - xh * jnp.mean(t * xh, axis=-1, keepdims=True))
        dx_ref[...] = dx
        dxb_ref[...] = dx.astype(BF16)
        part = jnp.sum(dy * xh, axis=0, keepdims=True)

        @pl.when(pl.program_id(0) == 0)
        def _():
            dg_ref[...] = part
            loss_ref[...] = lpart

        @pl.when(pl.program_id(0) > 0)
        def _():
            dg_ref[...] += part
            loss_ref[...] += lpart

    row = pl.BlockSpec((tr, D), lambda i: (i, 0))
    vec = pl.BlockSpec((1, D), lambda i: (0, 0))
    return pl.pallas_call(
        body, name=name, grid=(S // tr,),
        in_specs=[row, row, vec],
        out_specs=[row, row, vec, pl.BlockSpec((1, LANES), lambda i: (0, 0))],
        out_shape=[jax.ShapeDtypeStruct((S, D), F32), jax.ShapeDtypeStruct((S, D), BF16),
                   jax.ShapeDtypeStruct((1, D), F32), jax.ShapeDtypeStruct((1, LANES), F32)],
        compiler_params=_cparams("arbitrary"),
    )(x3, target, gain)


def _rope_tables(pos_col, invf, *, name):
    S = pos_col.shape[0]
    tr = _pick(S, 512, 8)

    def body(p_ref, f_ref, cos_ref, sin_ref):
        ang = p_ref[...].astype(F32) * f_ref[...]
        lane = lax.broadcasted_iota(jnp.int32, (1, LANES), 1)
        first = (lane % HEAD_DIM) < HEAD_DIM // 2
        sn = jnp.sin(ang)
        cos_ref[...] = jnp.cos(ang)
        sin_ref[...] = jnp.where(first, -sn, sn)

    return pl.pallas_call(
        body, name=name, grid=(S // tr,),
        in_specs=[pl.BlockSpec((tr, 1), lambda i: (i, 0)), pl.BlockSpec((1, LANES), lambda i: (0, 0))],
        out_specs=[pl.BlockSpec((tr, LANES), lambda i: (i, 0))] * 2,
        out_shape=[jax.ShapeDtypeStruct((S, LANES), F32)] * 2,
        compiler_params=_cparams("parallel"),
    )(pos_col, invf)


def _swap_halves(t):
    lane = lax.broadcasted_iota(jnp.int32, (1, LANES), 1)
    first = (lane % HEAD_DIM) < HEAD_DIM // 2
    return jnp.where(first, pltpu.roll(t, LANES - HEAD_DIM // 2, 1), pltpu.roll(t, HEAD_DIM // 2, 1))


def _rope_fwd(proj, cos_t, sin_t, *, q_off, k_off, name):
    S = proj.shape[0]
    tr = _pick(S, 256, 8)
    nqb = SWA_Q_W // LANES

    def body(q_ref, k_ref, c_ref, s_ref, qo_ref, ko_ref):
        cv, sv = c_ref[...], s_ref[...]
        for b in range(nqb):
            t = q_ref[:, b * LANES:(b + 1) * LANES].astype(F32)
            qo_ref[:, b * LANES:(b + 1) * LANES] = (t * cv + _swap_halves(t) * sv).astype(BF16)
        t = k_ref[...].astype(F32)
        ko_ref[...] = (t * cv + _swap_halves(t) * sv).astype(BF16)

    tab = pl.BlockSpec((tr, LANES), lambda i: (i, 0))
    return pl.pallas_call(
        body, name=name, grid=(S // tr,),
        in_specs=[pl.BlockSpec((tr, SWA_Q_W), lambda i: (i, q_off // SWA_Q_W)),
                  pl.BlockSpec((tr, LANES), lambda i: (i, k_off // LANES)), tab, tab],
        out_specs=[pl.BlockSpec((tr, SWA_Q_W), lambda i: (i, 0)), tab],
        out_shape=[jax.ShapeDtypeStruct((S, SWA_Q_W), BF16), jax.ShapeDtypeStruct((S, LANES), BF16)],
        compiler_params=_cparams("parallel"),
    )(proj, proj, cos_t, sin_t)


def _rope_bwd(dq, dk_cur, dk_prev, dv_cur, dv_prev, cos_t, sin_t, *, name):
    S = dq.shape[0]
    tr = WINDOW
    nb = S // tr
    nqb = SWA_Q_W // LANES

    def body(dq_ref, kc_ref, kp_ref, vc_ref, vp_ref, c_ref, s_ref, dqo_ref, dko_ref, dvo_ref):
        cv, sv = c_ref[...], s_ref[...]
        has_next = (pl.program_id(0) + 1 < nb).astype(F32)
        for b in range(nqb):
            d = dq_ref[:, b * LANES:(b + 1) * LANES]
            dqo_ref[:, b * LANES:(b + 1) * LANES] = (d * cv + _swap_halves(d * sv)).astype(BF16)
        d = kc_ref[0] + kc_ref[1] + has_next * (kp_ref[0] + kp_ref[1])
        dko_ref[...] = (d * cv + _swap_halves(d * sv)).astype(BF16)
        dvo_ref[...] = (vc_ref[0] + vc_ref[1] + has_next * (vp_ref[0] + vp_ref[1])).astype(BF16)

    tab = pl.BlockSpec((tr, LANES), lambda i: (i, 0))
    cur = pl.BlockSpec((2, tr, LANES), lambda i: (0, i, 0))
    nxt = pl.BlockSpec((2, tr, LANES), lambda i: (0, jnp.minimum(i + 1, nb - 1), 0))
    return pl.pallas_call(
        body, name=name, grid=(nb,),
        in_specs=[pl.BlockSpec((tr, SWA_Q_W), lambda i: (i, 0)), cur, nxt, cur, nxt, tab, tab],
        out_specs=[pl.BlockSpec((tr, SWA_Q_W), lambda i: (i, 0)), tab, tab],
        out_shape=[jax.ShapeDtypeStruct((S, SWA_Q_W), BF16), jax.ShapeDtypeStruct((S, LANES), BF16),
                   jax.ShapeDtypeStruct((S, LANES), BF16)],
        compiler_params=_cparams("parallel"),
    )(dq, dk_cur, dk_prev, dv_cur, dv_prev, cos_t, sin_t)


def _dot_nt(a, b):
    return lax.dot_general(a, b, (((1,), (1,)), ((), ())), preferred_element_type=F32)


def _dot_tn(a, b):
    return lax.dot_general(a, b, (((0,), (0,)), ((), ())), preferred_element_type=F32)


def _dot_nn(a, b):
    return lax.dot_general(a, b, (((1,), (0,)), ((), ())), preferred_element_type=F32)


def _roll_half(t):
    return pltpu.roll(t.astype(F32), HEAD_DIM, 1).astype(t.dtype)


SWA_STACK = SWA_GROUP // 2


def _swa_common(hk, n, kp_ref, kc_ref, vp_ref, vc_ref):
    k2 = jnp.concatenate([kp_ref[...], kc_ref[...]], axis=0)
    v2 = jnp.concatenate([vp_ref[...], vc_ref[...]], axis=0)
    k_sw, v_sw = _roll_half(k2), _roll_half(v2)
    rows = SWA_STACK * WINDOW
    row = lax.broadcasted_iota(jnp.int32, (rows, 2 * WINDOW), 0) % WINDOW
    col = lax.broadcasted_iota(jnp.int32, (rows, 2 * WINDOW), 1)
    diff = row + WINDOW - col
    allowed = (diff >= 0) & (diff < WINDOW) & ((col >= WINDOW) | (n > 0))
    lane = lax.broadcasted_iota(jnp.int32, (1, LANES), 1)
    half = [lane < HEAD_DIM, lane >= HEAD_DIM]
    kk = [jnp.where(hk == a, k2, k_sw) for a in range(2)]
    vv = [jnp.where(hk == a, v2, v_sw) for a in range(2)]
    return allowed, half, kk, vv


def _swa_stack(ref, mask, scale=None):
    parts = []
    for t in range(SWA_STACK):
        blk = ref[:, t * LANES:(t + 1) * LANES]
        if scale is not None:
            blk = blk * jnp.asarray(scale, blk.dtype)
        parts.append(jnp.where(mask, blk, jnp.zeros_like(blk)))
    return jnp.concatenate(parts, axis=0)


def _swa_sink_column(sink_ref, hk, a):
    blk = lax.broadcasted_iota(jnp.int32, (SWA_STACK * WINDOW, 1), 0) // WINDOW
    col = jnp.zeros((SWA_STACK * WINDOW, 1), F32)
    for t in range(SWA_STACK):
        col = jnp.where(blk == t, sink_ref[hk * SWA_GROUP + 2 * t + a], col)
    return col


def _swa_probs(qm, kk, allowed, sink):
    s = jnp.where(allowed, _dot_nt(qm, kk), NEG)
    m = jnp.maximum(jnp.max(s, axis=1, keepdims=True), sink)
    e = jnp.exp(s - m)
    es = jnp.exp(sink - m)
    inv = 1.0 / (jnp.sum(e, axis=1, keepdims=True) + es)
    return e * inv, es * inv


def _swa_fwd(q_rope, k_rope, proj, sinks, *, v_off, name):
    S = q_rope.shape[0]
    nb = S // WINDOW
    gw = SWA_GROUP * HEAD_DIM

    def body(sink_ref, q_ref, kp_ref, kc_ref, vp_ref, vc_ref, o_ref):
        hk, n = pl.program_id(0), pl.program_id(1)
        allowed, half, kk, vv = _swa_common(hk, n, kp_ref, kc_ref, vp_ref, vc_ref)
        outs = []
        for a in range(2):
            qm = _swa_stack(q_ref, half[a], ATT_SCALE)
            p, _ = _swa_probs(qm, kk[a], allowed, _swa_sink_column(sink_ref, hk, a))
            outs.append(_dot_nn(p.astype(BF16), vv[a]))
        for t in range(SWA_STACK):
            rows = slice(t * WINDOW, (t + 1) * WINDOW)
            o_ref[:, t * LANES:(t + 1) * LANES] = jnp.where(half[0], outs[0][rows], outs[1][rows]).astype(BF16)

    prev = lambda hk, n: (jnp.maximum(n - 1, 0), 0)
    cur = lambda hk, n: (n, 0)
    vprev = lambda hk, n: (jnp.maximum(n - 1, 0), v_off // LANES)
    vcur = lambda hk, n: (n, v_off // LANES)
    blk = lambda m: pl.BlockSpec((WINDOW, LANES), m)
    return pl.pallas_call(
        body, name=name, grid=(2, nb),
        in_specs=[pl.BlockSpec(memory_space=pltpu.SMEM),
                  pl.BlockSpec((WINDOW, gw), lambda hk, n: (n, hk)),
                  blk(prev), blk(cur), blk(vprev), blk(vcur)],
        out_specs=pl.BlockSpec((WINDOW, gw), lambda hk, n: (n, hk)),
        out_shape=jax.ShapeDtypeStruct((S, SWA_Q_W), BF16),
        compiler_params=_cparams("parallel", "parallel"),
    )(sinks, q_rope, k_rope, k_rope, proj, proj)


def _swa_bwd(q_rope, k_rope, proj, sinks, d_o, *, v_off, name):
    S = q_rope.shape[0]
    nb = S // WINDOW
    gw = SWA_GROUP * HEAD_DIM

    def body(sink_ref, q_ref, kp_ref, kc_ref, vp_ref, vc_ref, do_ref,
             dq_ref, dkc_ref, dkp_ref, dvc_ref, dvp_ref, dsink_ref):
        hk, n = pl.program_id(0), pl.program_id(1)
        allowed, half, kk, vv = _swa_common(hk, n, kp_ref, kc_ref, vp_ref, vc_ref)
        dk_acc = jnp.zeros((2 * WINDOW, LANES), F32)
        dv_acc = jnp.zeros((2 * WINDOW, LANES), F32)
        srow = lax.broadcasted_iota(jnp.int32, (SWA_GROUP, LANES), 0)
        dsink = jnp.zeros((SWA_GROUP, LANES), F32)
        dqs = []
        for a in range(2):
            qm = _swa_stack(q_ref, half[a], ATT_SCALE)
            dom = _swa_stack(do_ref, half[a])
            p, psink = _swa_probs(qm, kk[a], allowed, _swa_sink_column(sink_ref, hk, a))
            dp = _dot_nt(dom, vv[a])
            delta = jnp.sum(p * dp, axis=1, keepdims=True)
            ds = (p * (dp - delta)).astype(BF16)
            dsk = psink * delta
            for t in range(SWA_STACK):
                dsink = dsink + jnp.where(srow == 2 * t + a, -jnp.sum(dsk[t * WINDOW:(t + 1) * WINDOW]), 0.0)
            dqs.append(_dot_nn(ds, kk[a]) * ATT_SCALE)
            dk_acc = dk_acc + _dot_tn(ds, qm)
            dv_acc = dv_acc + _dot_tn(p.astype(BF16), dom)
        for t in range(SWA_STACK):
            rows = slice(t * WINDOW, (t + 1) * WINDOW)
            dq_ref[:, t * LANES:(t + 1) * LANES] = jnp.where(half[0], dqs[0][rows], dqs[1][rows])
        lane = lax.broadcasted_iota(jnp.int32, (1, LANES), 1)
        mine = (lane >= HEAD_DIM) == (hk == 1)
        dk_t = jnp.where(mine, dk_acc + pltpu.roll(dk_acc, HEAD_DIM, 1), 0.0)
        dv_t = jnp.where(mine, dv_acc + pltpu.roll(dv_acc, HEAD_DIM, 1), 0.0)
        dkp_ref[0] = dk_t[:WINDOW]
        dkc_ref[0] = dk_t[WINDOW:]
        dvp_ref[0] = dv_t[:WINDOW]
        dvc_ref[0] = dv_t[WINDOW:]

        @pl.when(n == 0)
        def _():
            dsink_ref[0] = dsink

        @pl.when(n > 0)
        def _():
            dsink_ref[0] += dsink

    prev = lambda hk, n: (jnp.maximum(n - 1, 0), 0)
    cur = lambda hk, n: (n, 0)
    vprev = lambda hk, n: (jnp.maximum(n - 1, 0), v_off // LANES)
    vcur = lambda hk, n: (n, v_off // LANES)
    blk = lambda m: pl.BlockSpec((WINDOW, LANES), m)
    qblk = pl.BlockSpec((WINDOW, gw), lambda hk, n: (n, hk))
    part = pl.BlockSpec((1, WINDOW, LANES), lambda hk, n: (hk, n, 0))
    part_shape = jax.ShapeDtypeStruct((2, S, LANES), F32)
    return pl.pallas_call(
        body, name=name, grid=(2, nb),
        in_specs=[pl.BlockSpec(memory_space=pltpu.SMEM), qblk, blk(prev), blk(cur), blk(vprev), blk(vcur), qblk],
        out_specs=[qblk, part, part, part, part,
                   pl.BlockSpec((1, SWA_GROUP, LANES), lambda hk, n: (hk, 0, 0))],
        out_shape=[jax.ShapeDtypeStruct((S, SWA_Q_W), F32), part_shape, part_shape, part_shape, part_shape,
                   jax.ShapeDtypeStruct((2, SWA_GROUP, LANES), F32)],
        compiler_params=_cparams("parallel", "arbitrary"),
    )(sinks, q_rope, k_rope, k_rope, proj, proj, d_o)


def _fox_prep(z_t, bias_col, *, name):
    H, S = z_t.shape
    tb = _pick(S, 512)

    def body(z_ref, b_ref, o_ref, carry_ref):
        @pl.when(pl.program_id(0) == 0)
        def _():
            carry_ref[...] = jnp.zeros_like(carry_ref)

        zz = z_ref[...] + b_ref[...]
        t = jnp.exp(-jnp.abs(zz))
        log1p = jnp.where(t < 1e-2, t * (1.0 - t * (0.5 - t * (1.0 / 3.0))), jnp.log(1.0 + t))
        logf = jnp.minimum(zz, 0.0) - log1p
        r = lax.broadcasted_iota(jnp.int32, (tb, tb), 0)
        c = lax.broadcasted_iota(jnp.int32, (tb, tb), 1)
        tri = (r <= c).astype(BF16)
        hi = logf.astype(BF16)
        r1 = logf - hi.astype(F32)
        mid = r1.astype(BF16)
        lo = (r1 - mid.astype(F32)).astype(BF16)
        cs = _dot_nn(hi, tri) + _dot_nn(mid, tri) + _dot_nn(lo, tri) + carry_ref[:, 0:1]
        o_ref[...] = -cs
        carry_ref[...] = jnp.zeros_like(carry_ref) + cs[:, tb - 1:tb]

    return pl.pallas_call(
        body, name=name, grid=(S // tb,),
        in_specs=[pl.BlockSpec((H, tb), lambda i: (0, i)), pl.BlockSpec((H, 1), lambda i: (0, 0))],
        out_specs=pl.BlockSpec((H, tb), lambda i: (0, i)),
        out_shape=jax.ShapeDtypeStruct((H, S), F32),
        scratch_shapes=[pltpu.VMEM((H, LANES), F32)],
        compiler_params=_cparams("arbitrary"),
    )(z_t, bias_col)


def _fox_post(drow, dcol, z_t, bias_col, *, name):
    H, S = z_t.shape
    tb = _pick(S, 512)
    nb = S // tb

    def body(dr_ref, d_ref, z_ref, b_ref, dz_ref, db_ref, carry_ref):
        @pl.when(pl.program_id(0) == 0)
        def _():
            carry_ref[...] = jnp.zeros_like(carry_ref)
            db_ref[...] = jnp.zeros_like(db_ref)

        dc = dr_ref[...] - d_ref[...]
        r = lax.broadcasted_iota(jnp.int32, (tb, tb), 0)
        c = lax.broadcasted_iota(jnp.int32, (tb, tb), 1)
        tri = (r >= c).astype(BF16)
        hi = dc.astype(BF16)
        r1 = dc - hi.astype(F32)
        mid = r1.astype(BF16)
        lo = (r1 - mid.astype(F32)).astype(BF16)
        dlogf = _dot_nn(hi, tri) + _dot_nn(mid, tri) + _dot_nn(lo, tri) + carry_ref[:, 0:1]
        carry_ref[...] = jnp.zeros_like(carry_ref) + dlogf[:, 0:1]
        dz = dlogf * _sigmoid(-(z_ref[...] + b_ref[...]))
        dz_ref[...] = dz
        db_ref[...] += jnp.sum(dz, axis=1, keepdims=True)

    rev = lambda i: (0, nb - 1 - i)
    return pl.pallas_call(
        body, name=name, grid=(nb,),
        in_specs=[pl.BlockSpec((H, tb), rev), pl.BlockSpec((H, tb), rev), pl.BlockSpec((H, tb), rev),
                  pl.BlockSpec((H, 1), lambda i: (0, 0))],
        out_specs=[pl.BlockSpec((H, tb), rev), pl.BlockSpec((H, LANES), lambda i: (0, 0))],
        out_shape=[jax.ShapeDtypeStruct((H, S), F32), jax.ShapeDtypeStruct((H, LANES), F32)],
        scratch_shapes=[pltpu.VMEM((H, LANES), F32)],
        compiler_params=_cparams("arbitrary"),
    )(drow, dcol, z_t, bias_col)


def _fox_blocks(S):
    cap = max(LANES, S // 4)
    return (min(FOX_FWD_BLOCKS[0], cap), min(FOX_FWD_BLOCKS[1], cap)), \
           (min(FOX_BWD_BLOCKS[0], cap), min(FOX_BWD_BLOCKS[1], cap))


def _key_bias_blocks(negc, bk):
    H, S = negc.shape
    return negc.reshape(H // 2, 2, S // bk, bk).transpose(0, 2, 1, 3)


def _fox_fwd(proj, negc4, *, q_off, k_off, v_off, bq, bk, name):
    S = proj.shape[0]
    nq, nk = S // bq, S // bk
    npair = FOX_HEADS // 2
    assert bq % bk == 0 or bk % bq == 0
    nmask = max(1, bq // bk)

    gp = FOX_FWD_PAIRS
    gw = gp * LANES
    assert q_off % gw == 0 and k_off % gw == 0 and v_off % gw == 0 and npair % gp == 0

    def body(q_ref, k_ref, v_ref, nc_ref, o_ref, lse_ref):
        i = pl.program_id(1)
        lane = lax.broadcasted_iota(jnp.int32, (1, LANES), 1)
        half = [lane < HEAD_DIM, lane >= HEAD_DIM]
        qh = []
        for g in range(gp):
            q2 = q_ref[:, g * LANES:(g + 1) * LANES] * jnp.asarray(ATT_SCALE, BF16)
            qh += [jnp.where(half[h], q2, jnp.zeros_like(q2)) for h in range(2)]
        row = lax.broadcasted_iota(jnp.int32, (bq, bk), 0)
        col = lax.broadcasted_iota(jnp.int32, (bq, bk), 1)
        rel = row - col
        nfull = (i * bq) // bk

        spare = [HEAD_DIM, 0]
        ones_lane = [lane == spare[h] for h in range(2)]

        def step(j, carry, masked):
            start = pl.multiple_of(j * bk, bk)
            new = []
            for g in range(gp):
                ks = k_ref[pl.ds(start, bk), g * LANES:(g + 1) * LANES]
                vs = v_ref[pl.ds(start, bk), g * LANES:(g + 1) * LANES]
                nb = nc_ref[g, j]
                for h in range(2):
                    m, acc = carry[4 * g + 2 * h:4 * g + 2 * h + 2]
                    vh = jnp.where(half[h], vs, jnp.where(ones_lane[h], jnp.ones_like(vs), jnp.zeros_like(vs)))
                    qs, bias = qh[2 * g + h], nb[h:h + 1, :]

                    def update(m, acc, rows, keys):
                        s = _dot_nt(qs[rows], ks[keys]) + bias[:, keys]
                        if masked:
                            s = jnp.where(rel[rows, keys] >= j * bk - i * bq, s, NEG)
                        m_new = jnp.maximum(m[rows], jnp.max(s, axis=1, keepdims=True))
                        p = jnp.exp(s - m_new).astype(BF16)
                        return m_new, jnp.exp(m[rows] - m_new) * acc[rows] + _dot_nn(p, vh[keys])

                    if masked and bq == bk:
                        top, bot, everything = slice(0, bq // 2), slice(bq // 2, bq), slice(0, bk)
                        m_t, acc_t = update(m, acc, top, top)
                        m_b, acc_b = update(m, acc, bot, everything)
                        new += [jnp.concatenate([m_t, m_b], axis=0), jnp.concatenate([acc_t, acc_b], axis=0)]
                    else:
                        new += list(update(m, acc, slice(0, bq), slice(0, bk)))
            return tuple(new)

        init = (jnp.full((bq, 1), NEG, F32), jnp.zeros((bq, LANES), F32)) * (2 * gp)
        carry = lax.fori_loop(0, nfull, lambda j, c: step(j, c, False), init)
        for t in range(nmask):
            carry = step(nfull + t, carry, True)
        for g in range(gp):
            outs, lses = [], []
            for h in range(2):
                m, acc = carry[4 * g + 2 * h:4 * g + 2 * h + 2]
                l = acc[:, spare[h]:spare[h] + 1]
                outs.append(acc * (1.0 / l))
                lses.append(m + jnp.log(l))
            o_ref[:, g * LANES:(g + 1) * LANES] = jnp.where(half[0], outs[0], outs[1]).astype(BF16)
            lse_ref[g] = jnp.where(half[0], lses[0], lses[1])

    seq = lambda off: pl.BlockSpec((S, gw), lambda hp, i: (0, off // gw + hp))
    return pl.pallas_call(
        body, name=name, grid=(npair // gp, nq),
        in_specs=[pl.BlockSpec((bq, gw), lambda hp, i: (i, q_off // gw + hp)), seq(k_off), seq(v_off),
                  pl.BlockSpec((gp, nk, 2, bk), lambda hp, i: (hp, 0, 0, 0))],
        out_specs=[pl.BlockSpec((bq, gw), lambda hp, i: (i, hp)),
                   pl.BlockSpec((gp, bq, LANES), lambda hp, i: (hp, i, 0))],
        out_shape=[jax.ShapeDtypeStruct((S, FOX_W), BF16), jax.ShapeDtypeStruct((npair, S, LANES), F32)],
        compiler_params=_cparams("parallel", "parallel"),
    )(proj, proj, proj, negc4)


def _fox_bwd(proj, negc4, o, lse, d_o, q_t, do_t, *, q_off, k_off, v_off, bq, bk, name, deps=()):
    S = proj.shape[0]
    nq, nk = S // bq, S // bk
    npair = FOX_HEADS // 2
    assert bq % bk == 0 or bk % bq == 0
    nmask = max(1, bk // bq)

    def body(q_ref, k_ref, v_ref, nc_ref, o_ref, lse_ref, do_ref, qt_ref, dot_ref, *rest):
        dqo_ref, dk_ref, dv_ref, dn_ref, dr_ref, delta_ref, rs_ref, dq_ref = rest[len(deps):]
        j = pl.program_id(1)
        lane = lax.broadcasted_iota(jnp.int32, (1, LANES), 1)
        half = [lane < HEAD_DIM, lane >= HEAD_DIM]
        spare = [HEAD_DIM, 0]
        ones_lane = [lane == spare[h] for h in range(2)]
        srow = lax.broadcasted_iota(jnp.int32, (LANES, 1), 0)
        rhalf = [srow < HEAD_DIM, srow >= HEAD_DIM]
        ones_row = [srow == spare[h] for h in range(2)]
        k2, v2 = k_ref[...], v_ref[...]
        one_k = jnp.ones_like(k2)
        kh = [jnp.where(half[h], k2, jnp.where(ones_lane[h], one_k, jnp.zeros_like(k2))) for h in range(2)]
        nb = nc_ref[0, 0]
        row = lax.broadcasted_iota(jnp.int32, (bq, bk), 0)
        col = lax.broadcasted_iota(jnp.int32, (bq, bk), 1)
        rel = row - col
        i_first = (j * bk) // bq

        @pl.when(j == 0)
        def _():
            dq_ref[...] = jnp.zeros_like(dq_ref)
            rs_ref[...] = jnp.zeros_like(rs_ref)
            for b in range(nq):
                prod = do_ref[b * bq:(b + 1) * bq, :].astype(F32) * o_ref[b * bq:(b + 1) * bq, :].astype(F32)
                d0 = jnp.sum(jnp.where(half[0], prod, 0.0), axis=1, keepdims=True)
                d1 = jnp.sum(jnp.where(half[1], prod, 0.0), axis=1, keepdims=True)
                delta_ref[b * bq:(b + 1) * bq, :] = jnp.where(half[0], d0, d1)

        def step(i, carry, masked, r0=0):
            dkt_a, dkt_b, dvt = carry
            dkts = [dkt_a, dkt_b]
            nr = bq - r0
            start = pl.multiple_of(i * bq + r0, LANES)
            q2 = q_ref[pl.ds(start, nr), :] * jnp.asarray(ATT_SCALE, BF16)
            do2 = do_ref[pl.ds(start, nr), :]
            qt = qt_ref[i][:, r0:] * jnp.asarray(ATT_SCALE, BF16)
            dot = dot_ref[i][:, r0:]
            lse2 = lse_ref[0, pl.ds(start, nr), :]
            del2 = delta_ref[pl.ds(start, nr), :]
            dqf = []
            for h in range(2):
                qm = jnp.where(half[h], q2, jnp.zeros_like(q2))
                dom = jnp.where(half[h], do2, jnp.zeros_like(do2))
                qtm = jnp.where(rhalf[h], qt, jnp.where(ones_row[h], jnp.ones_like(qt), jnp.zeros_like(qt)))
                dotm = jnp.where(rhalf[h], dot, jnp.zeros_like(dot))
                c0 = h * HEAD_DIM
                p = jnp.exp(_dot_nt(qm, k2) + nb[h:h + 1, :] - lse2[:, c0:c0 + 1])
                if masked:
                    p = jnp.where(rel[r0:] >= j * bk - i * bq, p, 0.0)
                dp = _dot_nt(dom, v2)
                dsb = (p * (dp - del2[:, c0:c0 + 1])).astype(BF16)
                dvt = dvt + _dot_nn(dotm, p.astype(BF16))
                dkts[h] = dkts[h] + _dot_nn(qtm, dsb)
                dqf.append(_dot_nn(dsb, kh[h]))
            dq_ref[pl.ds(start, nr), :] += jnp.where(half[0], dqf[0], dqf[1]) * ATT_SCALE
            rs_ref[pl.ds(start, nr), :] += jnp.where(ones_lane[0], dqf[0], jnp.where(ones_lane[1], dqf[1], 0.0))
            return dkts[0], dkts[1], dvt

        zero = jnp.zeros((LANES, bk), F32)
        carry = (zero, zero, zero)
        if bq > bk:
            sp = j % (bq // bk)
            carry = lax.switch(sp, [functools.partial(step, i_first, masked=True, r0=s * bk)
                                    for s in range(bq // bk)], carry)
        else:
            for t in range(nmask):
                carry = step(i_first + t, carry, True)
        dkt_a, dkt_b, dvt = lax.fori_loop(i_first + nmask, nq, lambda i, c: step(i, c, False), carry)
        dk_ref[...] = jnp.where(rhalf[0], dkt_a, dkt_b).T.astype(BF16)
        dv_ref[...] = dvt.T.astype(BF16)
        dn_ref[0, 0] = jnp.concatenate([dkt_a[spare[0]:spare[0] + 1], dkt_b[spare[1]:spare[1] + 1]], axis=0)

        @pl.when(j == nk - 1)
        def _():
            dqo_ref[...] = dq_ref[...].astype(BF16)
            for b in range(nq):
                t = rs_ref[b * bq:(b + 1) * bq, :].T
                dr_ref[0, b] = jnp.concatenate([t[spare[0]:spare[0] + 1], t[spare[1]:spare[1] + 1]], axis=0)

    once = pl.Buffered(1)
    seq = lambda off: pl.BlockSpec((S, LANES), lambda hp, j: (0, off // LANES + hp), pipeline_mode=once)
    blk = lambda off: pl.BlockSpec((bk, LANES), lambda hp, j: (j, off // LANES + hp))
    nc = pl.BlockSpec((1, 1, 2, bk), lambda hp, j: (hp, j, 0, 0))
    tsp = pl.BlockSpec((nq, LANES, bq), lambda hp, j: (0, hp, 0), pipeline_mode=once)
    return pl.pallas_call(
        body, name=name, grid=(npair, nk),
        in_specs=[seq(q_off), blk(k_off), blk(v_off), nc, seq(0),
                  pl.BlockSpec((1, S, LANES), lambda hp, j: (hp, 0, 0), pipeline_mode=once), seq(0),
                  tsp, tsp] + [_ANY] * len(deps),
        out_specs=[pl.BlockSpec((S, LANES), lambda hp, j: (0, hp)), blk(0), blk(0), nc,
                   pl.BlockSpec((1, nq, 2, bq), lambda hp, j: (hp, 0, 0, 0))],
        out_shape=[jax.ShapeDtypeStruct((S, FOX_W), BF16), jax.ShapeDtypeStruct((S, FOX_W), BF16),
                   jax.ShapeDtypeStruct((S, FOX_W), BF16), jax.ShapeDtypeStruct((npair, nk, 2, bk), F32),
                   jax.ShapeDtypeStruct((npair, nq, 2, bq), F32)],
        scratch_shapes=[pltpu.VMEM((S, LANES), F32), pltpu.VMEM((S, LANES), F32), pltpu.VMEM((S, LANES), F32)],
        compiler_params=_cparams("parallel", "arbitrary"),
    )(proj, proj, proj, negc4, o, lse, d_o, q_t, do_t, *deps)


def _exchange(arrs, *, gather, name):
    n = len(arrs)
    npeer = N_DEV - 1

    def body(*refs):
        ins, outs = refs[:n], refs[n:2 * n]
        send_sems, recv_sems, loc_sems = refs[2 * n:]
        x, y, c = lax.axis_index("x"), lax.axis_index("y"), lax.axis_index("c")
        me = 4 * x + 2 * y + c
        peers = []
        for k in range(1, N_DEV):
            px = 1 - x if k & 4 else x
            py = 1 - y if k & 2 else y
            pc = 1 - c if k & 1 else c
            peers.append(((px, py, pc), 4 * px + 2 * py + pc))

        def remote(w, k):
            dev, idx = peers[k]
            src = ins[w] if gather else ins[w].at[idx]
            return pltpu.make_async_remote_copy(
                src_ref=src, dst_ref=outs[w].at[me],
                send_sem=send_sems.at[w * npeer + k], recv_sem=recv_sems.at[w * npeer + k],
                device_id=dev, device_id_type=pl.DeviceIdType.MESH)

        def arrival(w, k):
            dev, idx = peers[k]
            src = ins[w] if gather else ins[w].at[idx]
            return pltpu.make_async_remote_copy(
                src_ref=src, dst_ref=outs[w].at[idx],
                send_sem=send_sems.at[w * npeer + k], recv_sem=recv_sems.at[w * npeer + k],
                device_id=dev, device_id_type=pl.DeviceIdType.MESH)

        local = []
        for w in range(n):
            for k in range(npeer):
                remote(w, k).start()
            cp = pltpu.make_async_copy(ins[w] if gather else ins[w].at[me], outs[w].at[me], loc_sems.at[w])
            cp.start()
            local.append(cp)
        for w in range(n):
            for k in range(npeer):
                arrival(w, k).wait_recv()
        for w in range(n):
            for k in range(npeer):
                remote(w, k).wait_send()
            local[w].wait()

    hbm = pl.BlockSpec(memory_space=pl.ANY)
    out_shape = [jax.ShapeDtypeStruct((N_DEV,) + (a.shape if gather else a.shape[1:]), a.dtype) for a in arrs]
    return pl.pallas_call(
        body, name=name,
        in_specs=[hbm] * n, out_specs=[hbm] * n, out_shape=out_shape,
        scratch_shapes=[pltpu.SemaphoreType.DMA((n * npeer,)), pltpu.SemaphoreType.DMA((n * npeer,)),
                        pltpu.SemaphoreType.DMA((n,))],
        compiler_params=pltpu.CompilerParams(has_side_effects=True),
    )(*arrs)


def _gather_two_level(shard, *, name):
    def body(x_ref, out_ref, send_sems, recv_sems, local_sem):
        x, y, c = lax.axis_index("x"), lax.axis_index("y"), lax.axis_index("c")
        me, sibling = (x, y, c), (x, y, 1 - c)
        chips = [(1 - x, y), (x, 1 - y), (1 - x, 1 - y)]

        def slot(px, py, pc):
            return out_ref.at[4 * px + 2 * py + pc]

        def copy(k, block, to, src=None):
            return pltpu.make_async_remote_copy(
                src_ref=slot(*block) if src is None else src, dst_ref=slot(*block),
                send_sem=send_sems.at[k], recv_sem=recv_sems.at[k],
                device_id=to, device_id_type=pl.DeviceIdType.MESH)

        mine = pltpu.make_async_copy(x_ref, slot(*me), local_sem)
        mine.start()
        first = [copy(0, me, sibling, src=x_ref)]
        first += [copy(1 + j, me, (*chip, c), src=x_ref) for j, chip in enumerate(chips)]
        for cp in first:
            cp.start()
        passed = [copy(4 + j, (*chip, c), sibling) for j, chip in enumerate(chips)]
        for j, chip in enumerate(chips):
            copy(1 + j, (*chip, c), me).wait_recv()
            passed[j].start()
        copy(0, sibling, me).wait_recv()
        for j, chip in enumerate(chips):
            copy(4 + j, (*chip, 1 - c), me).wait_recv()
        for cp in first + passed:
            cp.wait_send()
        mine.wait()

    return pl.pallas_call(
        body, name=name,
        in_specs=[_ANY], out_specs=_ANY,
        out_shape=jax.ShapeDtypeStruct((N_DEV,) + shard.shape, shard.dtype),
        scratch_shapes=[pltpu.SemaphoreType.DMA((N_DEV - 1,)), pltpu.SemaphoreType.DMA((N_DEV - 1,)),
                        pltpu.SemaphoreType.DMA],
        compiler_params=pltpu.CompilerParams(has_side_effects=True),
    )(shard)


_HBM = pl.BlockSpec(memory_space=pltpu.HBM)
_SEM = pl.BlockSpec(memory_space=pltpu.SEMAPHORE)
_EFFECT = pltpu.SideEffectType.DATAFLOW_SIDE_EFFECTING
NPEER = N_DEV - 1


def _peer_table():
    x, y, c = lax.axis_index("x"), lax.axis_index("y"), lax.axis_index("c")
    peers = []
    for k in range(1, N_DEV):
        px = 1 - x if k & 4 else x
        py = 1 - y if k & 2 else y
        pc = 1 - c if k & 1 else c
        peers.append(((px, py, pc), 4 * px + 2 * py + pc))
    return 4 * x + 2 * y + c, peers


def _split_copy(ins, lands, send_sems, recv_sems, gather, me, peers, w, k, arriving):
    dev, idx = peers[k]
    return pltpu.make_async_remote_copy(
        src_ref=ins[w] if gather else ins[w].at[idx],
        dst_ref=lands[w].at[idx if arriving else me],
        send_sem=send_sems.at[w * NPEER + k], recv_sem=recv_sems.at[w * NPEER + k],
        device_id=dev, device_id_type=pl.DeviceIdType.MESH)


def _exchange_start(arrs, *, gather, name, deps=()):
    n = len(arrs)
    land_shapes = [(N_DEV,) + (a.shape if gather else a.shape[1:]) for a in arrs]

    def body(*refs):
        ins, lands = refs[:n], refs[n:2 * n]
        send_sems, recv_sems = refs[2 * n + len(deps)], refs[2 * n + len(deps) + 1]
        token = refs[-1]
        me, peers = _peer_table()
        for w in range(n):
            for k in range(NPEER):
                _split_copy(ins, lands, send_sems, recv_sems, gather, me, peers, w, k, False).start()
        token[...] = jnp.zeros_like(token)

    out_shape = ([pltpu.SemaphoreType.DMA((n * NPEER,)), pltpu.SemaphoreType.DMA((n * NPEER,))]
                 + [pltpu.HBM(a.shape, a.dtype) for a in arrs]
                 + [pltpu.HBM(s, a.dtype) for s, a in zip(land_shapes, arrs)]
                 + [jax.ShapeDtypeStruct((8, LANES), F32)])
    res = pl.pallas_call(
        body, name=name,
        in_specs=[_HBM] * (2 * n) + [_ANY] * len(deps),
        out_specs=[_SEM, _SEM] + [_HBM] * (2 * n) + [pl.BlockSpec(memory_space=pltpu.VMEM)],
        out_shape=out_shape,
        input_output_aliases={i: 2 + i for i in range(2 * n)},
        compiler_params=pltpu.CompilerParams(has_side_effects=_EFFECT),
    )(*[pltpu.with_memory_space_constraint(a, pltpu.HBM) for a in arrs],
      *[pltpu.with_memory_space_constraint(lax.empty(s, a.dtype), pltpu.HBM) for s, a in zip(land_shapes, arrs)],
      *deps)
    return (n, gather, res[0], res[1], res[2:2 + n], res[2 + n:2 + 2 * n]), res[-1]


def _exchange_wait(handle, after, *, name):
    n, gather, send_sems, recv_sems, ins_thru, lands_thru = handle

    def body(*refs):
        ins, lands = refs[:n], refs[n:2 * n]
        send_s, recv_s = refs[2 * n], refs[2 * n + 1]
        me, peers = _peer_table()
        for w in range(n):
            for k in range(NPEER):
                _split_copy(ins, lands, send_s, recv_s, gather, me, peers, w, k, False).wait_send()
                _split_copy(ins, lands, send_s, recv_s, gather, me, peers, w, k, True).wait_recv()

    res = pl.pallas_call(
        body, name=name,
        in_specs=[_HBM] * (2 * n) + [_SEM, _SEM, pl.BlockSpec(memory_space=pl.ANY)],
        out_specs=[_HBM] * (2 * n),
        out_shape=[pltpu.HBM(a.shape, a.dtype) for a in list(ins_thru) + list(lands_thru)],
        input_output_aliases={i: i for i in range(2 * n)},
        compiler_params=pltpu.CompilerParams(has_side_effects=_EFFECT),
    )(*ins_thru, *lands_thru, send_sems, recv_sems, after)
    return res[:n], res[n:2 * n]


def _ordered_sum(s_ref, own_ref):
    if own_ref is None:
        blocks = [s_ref[q].astype(F32) for q in range(N_DEV)]
    else:
        me = 4 * lax.axis_index("x") + 2 * lax.axis_index("y") + lax.axis_index("c")
        own = own_ref[...]
        blocks = [jnp.where(me == q, own, s_ref[q]).astype(F32) for q in range(N_DEV)]
    acc = blocks[0]
    for b in blocks[1:]:
        acc = acc + b
    return acc


def _sum8(stack, own, *, name):
    _, R, C = stack.shape
    if R % 8 == 0:
        tr, tc = _pick(R, max(8, STEP_BYTES // (C * 4 * (N_DEV + 2))), 8), C
    else:
        tr, tc = R, _pick(C, max(LANES, STEP_BYTES // (R * 4 * (N_DEV + 2))))

    def body(s_ref, own_ref, o_ref):
        o_ref[...] = _ordered_sum(s_ref, own_ref)

    blk = pl.BlockSpec((tr, tc), lambda i, j: (i, j))
    return pl.pallas_call(
        body, name=name, grid=(R // tr, C // tc),
        in_specs=[pl.BlockSpec((N_DEV, tr, tc), lambda i, j: (0, i, j)), blk],
        out_specs=blk,
        out_shape=jax.ShapeDtypeStruct((R, C), F32),
        compiler_params=_cparams("parallel", "parallel"),
    )(stack, own)


def _adamw_math(w, g, m, v):
    m = ADAM_B1 * m + (1.0 - ADAM_B1) * g
    v = ADAM_B2 * v + (1.0 - ADAM_B2) * (g * g)
    m_hat = m / (1.0 - ADAM_B1 ** ADAM_STEP)
    v_hat = v / (1.0 - ADAM_B2 ** ADAM_STEP)
    delta = -ADAM_LR * (m_hat / (jnp.sqrt(v_hat) + ADAM_EPS) + ADAM_WD * w)
    return delta, m, v


def _adamw(w, g, m, v, *, name, stacked, own=None):
    R, C = w.shape
    tr = _pick(R, max(8, STEP_BYTES // (C * 4 * (8 + (N_DEV if stacked else 1)))), 8)
    has_own = own is not None

    def body(w_ref, g_ref, m_ref, v_ref, *rest):
        go_ref, d_ref, mo_ref, vo_ref = rest[-4:]
        g = _ordered_sum(g_ref, rest[0] if has_own else None) if stacked else g_ref[...]
        delta, m2, v2 = _adamw_math(w_ref[...], g, m_ref[...], v_ref[...])
        go_ref[...] = g
        d_ref[...] = delta
        mo_ref[...] = m2
        vo_ref[...] = v2

    row = pl.BlockSpec((tr, C), lambda i: (i, 0))
    g_spec = pl.BlockSpec((N_DEV, tr, C), lambda i: (0, i, 0)) if stacked else row
    return pl.pallas_call(
        body, name=name, grid=(R // tr,),
        in_specs=[row, g_spec, row, row] + [row] * has_own, out_specs=[row] * 4,
        out_shape=[jax.ShapeDtypeStruct((R, C), F32)] * 4,
        compiler_params=_cparams("parallel"),
    )(w, g, m, v, *([own] if has_own else []))


def kernel(x, positions, attn_norm, w_in, fox_f_bias, swa_sinks, w_branch_swa, w_branch_fox, w_out, mlp_norm, w_up, w_down, final_norm, loss_target, m_attn_norm, m_w_in, m_fox_f_bias, m_swa_sinks, m_w_branch_swa, m_w_branch_fox, m_w_out, m_mlp_norm, m_w_up, m_w_down, m_final_norm, v_attn_norm, v_w_in, v_fox_f_bias, v_swa_sinks, v_w_branch_swa, v_w_branch_fox, v_w_out, v_mlp_norm, v_w_up, v_w_down, v_final_norm):
    S, D = x.shape[1], x.shape[2]
    DFF = w_up.shape[2] * N_DEV
    d_in = w_in.shape[2] * N_DEV
    assert d_in == QKV_W + FOX_HEADS + 2 * D and (2 * D) % SWA_Q_W == 0 and S % (4 * LANES) == 0
    q_off = 2 * D
    k_off = q_off + SWA_Q_W
    v_off = k_off + SWA_KV_W
    fq_off = v_off + SWA_KV_W
    fk_off = fq_off + FOX_W
    fv_off = fk_off + FOX_W
    fl_off = fv_off + FOX_W
    NP = fl_off + FL_PAD
    x2d, tgt = x[0], loss_target[0]

    shards = [w_in[0].T.astype(BF16), w_branch_swa[0].T.astype(BF16), w_branch_fox[0].T.astype(BF16),
              w_out[0].astype(BF16), w_up[0].T.astype(BF16), w_down[0].astype(BF16)]
    me = 4 * lax.axis_index("x") + 2 * lax.axis_index("y") + lax.axis_index("c")

    def filled(stack, own):
        return lax.dynamic_update_slice(stack, own[None], (me,) + (0,) * own.ndim)

    g_in = _gather_two_level(shards[0], name="gather_w_in")
    h_rest, tok_rest = _exchange_start(shards[1:], gather=True, name="gather_rest_start", deps=[g_in])

    tm = _pick(S, 1024)
    td = _pick(D, 1024)
    tf = _pick(DFF, 1024)
    tnp = _pick(NP, 1024)

    h1 = _rms_fwd(x2d, attn_norm, name="rms1", deps=[tok_rest])
    w_in_t = g_in.reshape(d_in, D)
    w_in_p = jnp.concatenate([w_in_t[QKV_W + FOX_HEADS:], w_in_t[:QKV_W], w_in_t[QKV_W:QKV_W + FOX_HEADS],
                              jnp.zeros((FL_PAD - FOX_HEADS, D), BF16)], axis=0)
    w_fl_t = w_in_t[QKV_W:QKV_W + FOX_HEADS]
    proj, = _matmul(h1, w_in_p, mode="nt", name="mm_in", out_dtypes=[BF16], tm=_pick(S, 2048), tn=tnp, tk=D)
    z_t, = _matmul(w_fl_t, h1, mode="nt", name="mm_flogit", out_dtypes=[F32],
                   tm=FOX_HEADS, tn=_pick(S, 2048), tk=D)
    bias_col = fox_f_bias.reshape(FOX_HEADS, 1)
    negc = _fox_prep(z_t, bias_col, name="fox_prep")
    (fbq, fbk), (bbq, bbk) = _fox_blocks(S)
    inv_freq = ROPE_THETA ** (-jnp.arange(0, HEAD_DIM, 2, dtype=F32) / HEAD_DIM)
    invf = jnp.tile(inv_freq, LANES // (HEAD_DIM // 2)).reshape(1, LANES)
    cos_t, sin_t = _rope_tables(positions.reshape(S, 1), invf, name="rope_tables")
    q_rope, k_rope = _rope_fwd(proj, cos_t, sin_t, q_off=q_off, k_off=k_off, name="rope_fwd")
    sinks = swa_sinks.reshape(-1)
    o_a = _swa_fwd(q_rope, k_rope, proj, sinks, v_off=v_off, name="swa_fwd")
    o_b, lse = _fox_fwd(proj, _key_bias_blocks(negc, fbk), q_off=fq_off, k_off=fk_off, v_off=fv_off,
                        bq=fbq, bk=fbk, name="fox_fwd")
    s_rest, g_rest = _exchange_wait(h_rest, o_b, name="gather_rest_wait")
    g_bs, g_bf, g_o, g_up, g_dn = [filled(g, s) for g, s in zip(g_rest, s_rest)]
    w_bs_t = g_bs.reshape(D, SWA_Q_W)
    w_bf_t = g_bf.reshape(D, FOX_W)
    w_o = g_o.reshape(D, D)
    w_up_t = g_up.reshape(DFF, D)
    w_dn = g_dn.reshape(DFF, D)
    ya, = _matmul(o_a, w_bs_t, mode="nt", name="mm_branch_swa", out_dtypes=[BF16], tm=tm, tn=td, tk=SWA_Q_W)
    gate_maps = [lambda i, j, k: (i, j), lambda i, j, k: (i, j), lambda i, j, k: (i, j + D // td)]

    def merge_epi(acc, ya_t, ga_t, gb_t):
        merged = _sigmoid(ga_t.astype(F32)) * ya_t.astype(F32) + _sigmoid(gb_t.astype(F32)) * acc
        return acc, merged

    yb, merged = _matmul(o_b, w_bf_t, mode="nt", name="mm_branch_fox", out_dtypes=[BF16, BF16],
                         tm=tm, tn=td, tk=FOX_W, extras=[ya, proj, proj], extra_maps=gate_maps,
                         epilogue=merge_epi)
    x_mid, = _matmul(merged, w_o, mode="nn", name="mm_out", out_dtypes=[F32], tm=tm, tn=td, tk=D,
                     extras=[x2d], epilogue=lambda acc, r: (acc + r,))
    h2 = _rms_fwd(x_mid, mlp_norm, name="rms2")
    u, = _matmul(h2, w_up_t, mode="nt", name="mm_up", out_dtypes=[BF16], tm=_pick(S, 2048), tn=tf, tk=D,
                 epilogue=lambda acc: (jnp.maximum(acc, 0.0),))
    x_fin, = _matmul(u, w_dn, mode="nn", name="mm_down", out_dtypes=[F32], tm=tm, tn=td, tk=_pick(DFF, 2048),
                     a_fn=_square_bf16, extras=[x_mid], epilogue=lambda acc, r: (acc + r,))

    dx3, dx3b, dg3, loss_part = _loss_head(x_fin, tgt, final_norm.reshape(1, D), name="loss_head")
    d_up, = _matmul(dx3b, w_dn, mode="nt", name="mm_d_act", out_dtypes=[BF16], tm=_pick(S, 2048), tn=tf, tk=D,
                    extras=[u], epilogue=lambda acc, ut: (acc * (2.0 * ut.astype(F32)),))
    tks = _pick(S, 2048)
    dw_dn, = _matmul(u, dx3b, mode="tn", name="mm_dw_down", out_dtypes=[F32], tm=tf, tn=td, tk=tks,
                     a_fn=_square_bf16)
    dh2, = _matmul(d_up, w_up_t, mode="nn", name="mm_dh2", out_dtypes=[F32], tm=tm, tn=td, tk=_pick(DFF, 2048))
    dw_up_t, = _matmul(d_up, h2, mode="tn", name="mm_dw_up", out_dtypes=[F32], tm=tf, tn=td, tk=tks)
    h_s1, tok_s1 = _exchange_start([dw_up_t.reshape(N_DEV, DFF // N_DEV, D), dw_dn.reshape(N_DEV, DFF // N_DEV, D)],
                                   gather=False, name="scatter_mlp_start")
    dx2, dx2b, dg2 = _rms_bwd(dh2, x_mid, mlp_norm, dx3, name="rms2_bwd", want_bf16=True, deps=[tok_s1])

    def gate_bwd_epi(dm, ya_t, yb_t, ga_t, gb_t):
        sa, sb = _sigmoid(ga_t.astype(F32)), _sigmoid(gb_t.astype(F32))
        return (dm * sa, dm * sb, dm * ya_t.astype(F32) * sa * (1.0 - sa), dm * yb_t.astype(F32) * sb * (1.0 - sb))

    gmaps = [lambda i, j, k: (i, j), lambda i, j, k: (i, j), lambda i, j, k: (i, j),
             lambda i, j, k: (i, j + D // td)]
    d_ya, d_yb, d_ga, d_gb = _matmul(dx2b, w_o, mode="nt", name="mm_d_merged", out_dtypes=[BF16] * 4,
                                     tm=tm, tn=td, tk=D, extras=[ya, yb, proj, proj], extra_maps=gmaps,
                                     epilogue=gate_bwd_epi)
    dw_o, = _matmul(merged, dx2b, mode="tn", name="mm_dw_out", out_dtypes=[F32], tm=td, tn=td, tk=tks)
    d_oa, = _matmul(d_ya, w_bs_t, mode="nn", name="mm_d_oa", out_dtypes=[BF16], tm=tm, tn=SWA_Q_W, tk=D)
    d_ob, = _matmul(d_yb, w_bf_t, mode="nn", name="mm_d_ob", out_dtypes=[BF16], tm=tm, tn=FOX_W, tk=D)
    dw_bs_t, = _matmul(d_ya, o_a, mode="tn", name="mm_dw_bs", out_dtypes=[F32], tm=td, tn=SWA_Q_W, tk=tks)
    dw_bf_t, = _matmul(d_yb, o_b, mode="tn", name="mm_dw_bf", out_dtypes=[F32], tm=td, tn=FOX_W, tk=tks)
    h_s2, tok_s2 = _exchange_start([dw_bs_t.reshape(N_DEV, D // N_DEV, SWA_Q_W),
                                    dw_bf_t.reshape(N_DEV, D // N_DEV, FOX_W), dw_o.reshape(N_DEV, D // N_DEV, D)],
                                   gather=False, name="scatter_attn_start")
    def row_blocks_t(a):
        return a.reshape(S // bbq, bbq, FOX_W).transpose(0, 2, 1)

    d_fq, d_fk, d_fv, dcol4, drow4 = _fox_bwd(proj, _key_bias_blocks(negc, bbk), o_b, lse, d_ob,
                                              row_blocks_t(proj[:, fq_off:fq_off + FOX_W]), row_blocks_t(d_ob),
                                              q_off=fq_off, k_off=fk_off, v_off=fv_off, bq=bbq, bk=bbk,
                                              name="fox_bwd", deps=[tok_s2])
    dcol = dcol4.transpose(0, 2, 1, 3).reshape(FOX_HEADS, S)
    drow = drow4.transpose(0, 2, 1, 3).reshape(FOX_HEADS, S)
    dz_t, dbias_l = _fox_post(drow, dcol, z_t, bias_col, name="fox_post")
    dq_r, dk_c, dk_p, dv_c, dv_p, dsink_l = _swa_bwd(q_rope, k_rope, proj, sinks, d_oa, v_off=v_off, name="swa_bwd")
    d_aq, d_ak, d_av = _rope_bwd(dq_r, dk_c, dk_p, dv_c, dv_p, cos_t, sin_t, name="rope_bwd")
    dz_pad = jnp.pad(dz_t.T.astype(BF16), ((0, 0), (0, FL_PAD - FOX_HEADS)))
    d_proj = jnp.concatenate([d_ga, d_gb, d_aq, d_ak, d_av, d_fq, d_fk, d_fv, dz_pad], axis=1)
    tkp = _pick(NP, 2304)
    dw_in_p, = _matmul(d_proj, h1, mode="tn", name="mm_dw_in", out_dtypes=[BF16], tm=_pick(NP, 512), tn=D, tk=tks)
    dw_in_t = jnp.concatenate([dw_in_p[q_off:q_off + QKV_W], dw_in_p[fl_off:fl_off + FOX_HEADS], dw_in_p[:q_off]],
                              axis=0)
    h_s3, tok_s3 = _exchange_start([dw_in_t.reshape(N_DEV, d_in // N_DEV, D)], gather=False,
                                   name="scatter_in_start")
    dh1, = _matmul(d_proj, w_in_p, mode="nn", name="mm_dh1", out_dtypes=[F32], tm=tm, tn=td, tk=tkp, deps=[tok_s3])
    dx, dg1 = _rms_bwd(dh1, x2d, attn_norm, dx2, name="rms1_bwd", want_bf16=False)

    dbias = dbias_l[:, 0]
    dsinks = dsink_l[:, :, 0].reshape(-1)
    nsm = 3 * D + 2 * LANES
    tail = jnp.zeros((2 * LANES,), F32)
    small_g = jnp.concatenate([dg1[0], dg2[0], dg3[0],
                               tail.at[0:16].set(dbias).at[16:32].set(dsinks).at[32].set(loss_part[0, 0])])

    def pack(a_norm, b_norm, f_norm, bias, snk):
        return jnp.concatenate([a_norm[0], b_norm[0], f_norm,
                                tail.at[0:16].set(bias[0]).at[16:32].set(snk[0])]).reshape(1, nsm)

    small_stack, = _exchange([small_g.reshape(1, nsm)], gather=True, name="gather_small")
    u_sm = _adamw(pack(attn_norm, mlp_norm, final_norm, fox_f_bias, swa_sinks), small_stack,
                  pack(m_attn_norm, m_mlp_norm, m_final_norm, m_fox_f_bias, m_swa_sinks),
                  pack(v_attn_norm, v_mlp_norm, v_final_norm, v_fox_f_bias, v_swa_sinks),
                  name="adamw_small", stacked=True)
    loss = u_sm[0][0, 3 * D + 32]

    def own_of(src):
        return lax.dynamic_index_in_dim(src, me, 0, keepdims=False)

    def update_t(stack, src, w, m, v, nm):
        g = _sum8(stack, own_of(src), name="sum_" + nm).T
        return _adamw(w[0], g, m[0], v[0], name="adamw_" + nm, stacked=False)

    def update(stack, src, w, m, v, nm):
        return _adamw(w[0], stack, m[0], v[0], name="adamw_" + nm, stacked=True, own=own_of(src))

    (s_up, s_dn), (r_up, r_dn) = _exchange_wait(h_s1, u_sm[1], name="scatter_mlp_wait")
    u_up = update_t(r_up, s_up, w_up, m_w_up, v_w_up, "w_up")
    u_dn = update(r_dn, s_dn, w_down, m_w_down, v_w_down, "w_down")
    (s_bs, s_bf, s_o), (r_bs, r_bf, r_o) = _exchange_wait(h_s2, u_dn[1], name="scatter_attn_wait")
    u_bs = update_t(r_bs, s_bs, w_branch_swa, m_w_branch_swa, v_w_branch_swa, "w_bs")
    u_bf = update_t(r_bf, s_bf, w_branch_fox, m_w_branch_fox, v_w_branch_fox, "w_bf")
    u_o = update(r_o, s_o, w_out, m_w_out, v_w_out, "w_out")
    (s_w_in,), (r_in,) = _exchange_wait(h_s3, u_o[1], name="scatter_in_wait")
    u_in = update_t(r_in, s_w_in, w_in, m_w_in, v_w_in, "w_in")

    def small(kind):
        a = u_sm[kind][0]
        return dict(attn_norm=a[0:D][None], mlp_norm=a[D:2 * D][None], final_norm=a[2 * D:3 * D],
                    fox_f_bias=a[3 * D:3 * D + 16][None], swa_sinks=a[3 * D + 16:3 * D + 32][None])

    big = dict(w_in=u_in, w_branch_swa=u_bs, w_branch_fox=u_bf, w_out=u_o, w_up=u_up, w_down=u_dn)
    order = ["attn_norm", "w_in", "fox_f_bias", "swa_sinks", "w_branch_swa", "w_branch_fox", "w_out", "mlp_norm",
             "w_up", "w_down", "final_norm"]
    outs = [loss, dx[None]]
    for kind in range(4):
        sm = small(kind)
        for nm in order:
            outs.append(big[nm][kind][None] if nm in big else sm[nm])
    return tuple(outs)
```

```python
import functools

import jax
import jax.numpy as jnp
from jax import lax
from jax.experimental import pallas as pl
from jax.experimental.pallas import tpu as pltpu

F32 = jnp.float32
BF16 = jnp.bfloat16

N_DEV = 8
HEAD_DIM = 64
SWA_Q_W = 1024
SWA_KV_W = 128
SWA_GROUP = 8
WINDOW = 128
FOX_W = 1024
FOX_HEADS = 16
QKV_W = SWA_Q_W + 2 * SWA_KV_W + 3 * FOX_W
FL_PAD = 256
ROPE_THETA = 10000.0
RMS_EPS = 1e-6
ATT_SCALE = 0.125
NEG = -1e30

ADAM_LR = 0.001
ADAM_B1 = 0.9
ADAM_B2 = 0.999
ADAM_EPS = 1e-08
ADAM_WD = 0.01
ADAM_STEP = 10

FOX_FWD_BLOCKS = (1024, 1024)
FOX_BWD_BLOCKS = (1024, 512)
FOX_FWD_PAIRS = 2

LANES = 128
VMEM_LIMIT = 56 * 1024 * 1024
STEP_BYTES = 12 * 1024 * 1024


def _cparams(*sem):
    return pltpu.CompilerParams(dimension_semantics=sem, vmem_limit_bytes=VMEM_LIMIT)


def _pick(dim, pref, align=LANES):
    best = None
    t = align
    while t <= min(dim, pref):
        if dim % t == 0:
            best = t
        t += align
    return best if best is not None else dim


_DIMS = {"nn": ((1,), (0,)), "nt": ((1,), (1,)), "tn": ((0,), (0,))}


_ANY = pl.BlockSpec(memory_space=pl.ANY)


def _matmul(a, b, *, mode, name, out_dtypes, tm, tn, tk, extras=(), extra_maps=None,
            a_fn=None, epilogue=None, deps=()):
    if mode == "nn":
        (M, K), (K2, N) = a.shape, b.shape
    elif mode == "nt":
        (M, K), (N, K2) = a.shape, b.shape
    else:
        (K, M), (K2, N) = a.shape, b.shape
    assert K == K2, (name, a.shape, b.shape)
    assert M % tm == 0 and N % tn == 0 and K % tk == 0, (name, M, N, K, tm, tn, tk)
    nk = K // tk
    ne, no = len(extras), len(out_dtypes)
    dims = (_DIMS[mode], ((), ()))

    def body(*refs):
        a_ref, b_ref = refs[0], refs[1]
        ex_refs = refs[2:2 + ne]
        out_refs = refs[2 + ne + len(deps):2 + ne + len(deps) + no]

        def finish(acc):
            res = (acc,) if epilogue is None else epilogue(acc, *[e[...] for e in ex_refs])
            for o_ref, r in zip(out_refs, res):
                o_ref[...] = r.astype(o_ref.dtype)

        def product():
            av = a_ref[...]
            if a_fn is not None:
                av = a_fn(av)
            return lax.dot_general(av, b_ref[...], dims, preferred_element_type=F32)

        if nk == 1:
            finish(product())
        else:
            acc_ref = refs[-1]
            k = pl.program_id(2)

            @pl.when(k == 0)
            def _():
                acc_ref[...] = jnp.zeros_like(acc_ref)

            acc_ref[...] += product()

            @pl.when(k == nk - 1)
            def _():
                finish(acc_ref[...])

    if mode == "tn":
        a_spec = pl.BlockSpec((tk, tm), lambda i, j, k: (k, i))
    else:
        a_spec = pl.BlockSpec((tm, tk), lambda i, j, k: (i, k))
    if mode == "nt":
        b_spec = pl.BlockSpec((tn, tk), lambda i, j, k: (j, k))
    else:
        b_spec = pl.BlockSpec((tk, tn), lambda i, j, k: (k, j))
    if extra_maps is None:
        extra_maps = [lambda i, j, k: (i, j)] * ne
    ex_specs = [pl.BlockSpec((tm, tn), m) for m in extra_maps]
    out_spec = [pl.BlockSpec((tm, tn), lambda i, j, k: (i, j)) for _ in range(no)]
    res = pl.pallas_call(
        body,
        name=name,
        grid=(M // tm, N // tn, nk),
        in_specs=[a_spec, b_spec] + ex_specs + [_ANY] * len(deps),
        out_specs=out_spec,
        out_shape=[jax.ShapeDtypeStruct((M, N), d) for d in out_dtypes],
        scratch_shapes=[pltpu.VMEM((tm, tn), F32)] if nk > 1 else [],
        compiler_params=_cparams("parallel", "parallel", "arbitrary"),
    )(a, b, *extras, *deps)
    return res


def _square_bf16(t):
    tf = t.astype(F32)
    return (tf * tf).astype(BF16)


def _sigmoid(g):
    return 0.5 * jnp.tanh(0.5 * g) + 0.5


def _rms_fwd(x, gain, *, name, deps=()):
    S, D = x.shape
    tr = _pick(S, 512, 8)

    def body(x_ref, g_ref, *rest):
        h_ref = rest[-1]
        xv = x_ref[...]
        r = lax.rsqrt(jnp.mean(xv * xv, axis=-1, keepdims=True) + RMS_EPS)
        h_ref[...] = (xv * r * g_ref[...]).astype(BF16)

    return pl.pallas_call(
        body, name=name, grid=(S // tr,),
        in_specs=[pl.BlockSpec((tr, D), lambda i: (i, 0)), pl.BlockSpec((1, D), lambda i: (0, 0))] + [_ANY] * len(deps),
        out_specs=pl.BlockSpec((tr, D), lambda i: (i, 0)),
        out_shape=jax.ShapeDtypeStruct((S, D), BF16),
        compiler_params=_cparams("parallel"),
    )(x, gain, *deps)


def _rms_bwd(dh, x, gain, dres, *, name, want_bf16, deps=()):
    S, D = x.shape
    tr = _pick(S, 256, 8)

    def body(dh_ref, x_ref, g_ref, dres_ref, *rest):
        outs = rest[len(deps):]
        dx_ref, dg_ref = outs[0], outs[-1]
        xv = x_ref[...]
        r = lax.rsqrt(jnp.mean(xv * xv, axis=-1, keepdims=True) + RMS_EPS)
        xh = xv * r
        dhv = dh_ref[...]
        t = dhv * g_ref[...]
        dx = r * (t - xh * jnp.mean(t * xh, axis=-1, keepdims=True)) + dres_ref[...]
        dx_ref[...] = dx
        if want_bf16:
            outs[1][...] = dx.astype(BF16)
        part = jnp.sum(dhv * xh, axis=0, keepdims=True)

        @pl.when(pl.program_id(0) == 0)
        def _():
            dg_ref[...] = part

        @pl.when(pl.program_id(0) > 0)
        def _():
            dg_ref[...] += part

    row = pl.BlockSpec((tr, D), lambda i: (i, 0))
    vec = pl.BlockSpec((1, D), lambda i: (0, 0))
    out_shape = [jax.ShapeDtypeStruct((S, D), F32)]
    out_specs = [row]
    if want_bf16:
        out_shape.append(jax.ShapeDtypeStruct((S, D), BF16))
        out_specs.append(row)
    out_shape.append(jax.ShapeDtypeStruct((1, D), F32))
    out_specs.append(vec)
    return pl.pallas_call(
        body, name=name, grid=(S // tr,),
        in_specs=[row, row, vec, row] + [_ANY] * len(deps), out_specs=out_specs, out_shape=out_shape,
        compiler_params=_cparams("arbitrary"),
    )(dh, x, gain, dres, *deps)


def _loss_head(x3, target, gain, *, name):
    S, D = x3.shape
    tr = _pick(S, 256, 8)

    def body(x_ref, t_ref, g_ref, dx_ref, dxb_ref, dg_ref, loss_ref):
        xv = x_ref[...]
        r = lax.rsqrt(jnp.mean(xv * xv, axis=-1, keepdims=True) + RMS_EPS)
        xh = xv * r
        gv = g_ref[...]
        err = xh * gv - t_ref[...]
        lpart = jnp.zeros((1, LANES), F32) + (0.5 / D) * jnp.sum(err * err)
        dy = err * (1.0 / D)
        t = dy * gv
        dx = r * (t - xh * jnp.mean(t * xh, axis=-1, keepdims=True))
        dx_ref[...] = dx
        dxb_ref[...] = dx.astype(BF16)
        part = jnp.sum(dy * xh, axis=0, keepdims=True)

        @pl.when(pl.program_id(0) == 0)
        def _():
            dg_ref[...] = part
            loss_ref[...] = lpart

        @pl.when(pl.program_id(0) > 0)
        def _():
            dg_ref[...] += part
            loss_ref[...] += lpart

    row = pl.BlockSpec((tr, D), lambda i: (i, 0))
    vec = pl.BlockSpec((1, D), lambda i: (0, 0))
    return pl.pallas_call(
        body, name=name, grid=(S // tr,),
        in_specs=[row, row, vec],
        out_specs=[row, row, vec, pl.BlockSpec((1, LANES), lambda i: (0, 0))],
        out_shape=[jax.ShapeDtypeStruct((S, D), F32), jax.ShapeDtypeStruct((S, D), BF16),
                   jax.ShapeDtypeStruct((1, D), F32), jax.ShapeDtypeStruct((1, LANES), F32)],
        compiler_params=_cparams("arbitrary"),
    )(x3, target, gain)


def _rope_tables(pos_col, invf, *, name, deps=()):
    S = pos_col.shape[0]
    tr = _pick(S, 512, 8)

    def body(p_ref, f_ref, *rest):
        cos_ref, sin_ref = rest[len(deps):]
        ang = p_ref[...].astype(F32) * f_ref[...]
        lane = lax.broadcasted_iota(jnp.int32, (1, LANES), 1)
        first = (lane % HEAD_DIM) < HEAD_DIM // 2
        sn = jnp.sin(ang)
        cos_ref[...] = jnp.cos(ang)
        sin_ref[...] = jnp.where(first, -sn, sn)

    return pl.pallas_call(
        body, name=name, grid=(S // tr,),
        in_specs=[pl.BlockSpec((tr, 1), lambda i: (i, 0)), pl.BlockSpec((1, LANES), lambda i: (0, 0))]
        + [_ANY] * len(deps),
        out_specs=[pl.BlockSpec((tr, LANES), lambda i: (i, 0))] * 2,
        out_shape=[jax.ShapeDtypeStruct((S, LANES), F32)] * 2,
        compiler_params=_cparams("parallel"),
    )(pos_col, invf, *deps)


def _swap_halves(t):
    lane = lax.broadcasted_iota(jnp.int32, (1, LANES), 1)
    first = (lane % HEAD_DIM) < HEAD_DIM // 2
    return jnp.where(first, pltpu.roll(t, LANES - HEAD_DIM // 2, 1), pltpu.roll(t, HEAD_DIM // 2, 1))


def _rope_fwd(proj, cos_t, sin_t, *, q_off, k_off, name):
    S = proj.shape[0]
    tr = _pick(S, 256, 8)
    nqb = SWA_Q_W // LANES

    def body(q_ref, k_ref, c_ref, s_ref, qo_ref, ko_ref):
        cv, sv = c_ref[...], s_ref[...]
        for b in range(nqb):
            t = q_ref[:, b * LANES:(b + 1) * LANES].astype(F32)
            qo_ref[:, b * LANES:(b + 1) * LANES] = (t * cv + _swap_halves(t) * sv).astype(BF16)
        t = k_ref[...].astype(F32)
        ko_ref[...] = (t * cv + _swap_halves(t) * sv).astype(BF16)

    tab = pl.BlockSpec((tr, LANES), lambda i: (i, 0))
    return pl.pallas_call(
        body, name=name, grid=(S // tr,),
        in_specs=[pl.BlockSpec((tr, SWA_Q_W), lambda i: (i, q_off // SWA_Q_W)),
                  pl.BlockSpec((tr, LANES), lambda i: (i, k_off // LANES)), tab, tab],
        out_specs=[pl.BlockSpec((tr, SWA_Q_W), lambda i: (i, 0)), tab],
        out_shape=[jax.ShapeDtypeStruct((S, SWA_Q_W), BF16), jax.ShapeDtypeStruct((S, LANES), BF16)],
        compiler_params=_cparams("parallel"),
    )(proj, proj, cos_t, sin_t)


def _rope_bwd(dq, dk_cur, dk_prev, dv_cur, dv_prev, cos_t, sin_t, *, name):
    S = dq.shape[0]
    tr = WINDOW
    nb = S // tr
    nqb = SWA_Q_W // LANES

    def body(dq_ref, kc_ref, kp_ref, vc_ref, vp_ref, c_ref, s_ref, dqo_ref, dko_ref, dvo_ref):
        cv, sv = c_ref[...], s_ref[...]
        has_next = (pl.program_id(0) + 1 < nb).astype(F32)
        for b in range(nqb):
            d = dq_ref[:, b * LANES:(b + 1) * LANES]
            dqo_ref[:, b * LANES:(b + 1) * LANES] = (d * cv + _swap_halves(d * sv)).astype(BF16)
        d = kc_ref[0] + kc_ref[1] + has_next * (kp_ref[0] + kp_ref[1])
        dko_ref[...] = (d * cv + _swap_halves(d * sv)).astype(BF16)
        dvo_ref[...] = (vc_ref[0] + vc_ref[1] + has_next * (vp_ref[0] + vp_ref[1])).astype(BF16)

    tab = pl.BlockSpec((tr, LANES), lambda i: (i, 0))
    cur = pl.BlockSpec((2, tr, LANES), lambda i: (0, i, 0))
    nxt = pl.BlockSpec((2, tr, LANES), lambda i: (0, jnp.minimum(i + 1, nb - 1), 0))
    return pl.pallas_call(
        body, name=name, grid=(nb,),
        in_specs=[pl.BlockSpec((tr, SWA_Q_W), lambda i: (i, 0)), cur, nxt, cur, nxt, tab, tab],
        out_specs=[pl.BlockSpec((tr, SWA_Q_W), lambda i: (i, 0)), tab, tab],
        out_shape=[jax.ShapeDtypeStruct((S, SWA_Q_W), BF16), jax.ShapeDtypeStruct((S, LANES), BF16),
                   jax.ShapeDtypeStruct((S, LANES), BF16)],
        compiler_params=_cparams("parallel"),
    )(dq, dk_cur, dk_prev, dv_cur, dv_prev, cos_t, sin_t)


def _dot_nt(a, b):
    return lax.dot_general(a, b, (((1,), (1,)), ((), ())), preferred_element_type=F32)


def _dot_tn(a, b):
    return lax.dot_general(a, b, (((0,), (0,)), ((), ())), preferred_element_type=F32)


def _dot_nn(a, b):
    return lax.dot_general(a, b, (((1,), (0,)), ((), ())), preferred_element_type=F32)


def _roll_half(t):
    return pltpu.roll(t.astype(F32), HEAD_DIM, 1).astype(t.dtype)


SWA_STACK = SWA_GROUP // 2


def _swa_common(hk, n, kp_ref, kc_ref, vp_ref, vc_ref):
    k2 = jnp.concatenate([kp_ref[...], kc_ref[...]], axis=0)
    v2 = jnp.concatenate([vp_ref[...], vc_ref[...]], axis=0)
    k_sw, v_sw = _roll_half(k2), _roll_half(v2)
    rows = SWA_STACK * WINDOW
    row = lax.broadcasted_iota(jnp.int32, (rows, 2 * WINDOW), 0) % WINDOW
    col = lax.broadcasted_iota(jnp.int32, (rows, 2 * WINDOW), 1)
    diff = row + WINDOW - col
    allowed = (diff >= 0) & (diff < WINDOW) & ((col >= WINDOW) | (n > 0))
    lane = lax.broadcasted_iota(jnp.int32, (1, LANES), 1)
    half = [lane < HEAD_DIM, lane >= HEAD_DIM]
    kk = [jnp.where(hk == a, k2, k_sw) for a in range(2)]
    vv = [jnp.where(hk == a, v2, v_sw) for a in range(2)]
    return allowed, half, kk, vv


def _swa_stack(ref, mask, scale=None):
    parts = []
    for t in range(SWA_STACK):
        blk = ref[:, t * LANES:(t + 1) * LANES]
        if scale is not None:
            blk = blk * jnp.asarray(scale, blk.dtype)
        parts.append(jnp.where(mask, blk, jnp.zeros_like(blk)))
    return jnp.concatenate(parts, axis=0)


def _swa_sink_column(sink_ref, hk, a):
    blk = lax.broadcasted_iota(jnp.int32, (SWA_STACK * WINDOW, 1), 0) // WINDOW
    col = jnp.zeros((SWA_STACK * WINDOW, 1), F32)
    for t in range(SWA_STACK):
        col = jnp.where(blk == t, sink_ref[hk * SWA_GROUP + 2 * t + a], col)
    return col


def _swa_probs(qm, kk, allowed, sink):
    s = jnp.where(allowed, _dot_nt(qm, kk), NEG)
    m = jnp.maximum(jnp.max(s, axis=1, keepdims=True), sink)
    e = jnp.exp(s - m)
    es = jnp.exp(sink - m)
    inv = 1.0 / (jnp.sum(e, axis=1, keepdims=True) + es)
    return e * inv, es * inv


def _swa_fwd(q_rope, k_rope, proj, sinks, *, v_off, name):
    S = q_rope.shape[0]
    nb = S // WINDOW
    gw = SWA_GROUP * HEAD_DIM

    def body(sink_ref, q_ref, kp_ref, kc_ref, vp_ref, vc_ref, o_ref):
        hk, n = pl.program_id(0), pl.program_id(1)
        allowed, half, kk, vv = _swa_common(hk, n, kp_ref, kc_ref, vp_ref, vc_ref)
        outs = []
        for a in range(2):
            qm = _swa_stack(q_ref, half[a], ATT_SCALE)
            p, _ = _swa_probs(qm, kk[a], allowed, _swa_sink_column(sink_ref, hk, a))
            outs.append(_dot_nn(p.astype(BF16), vv[a]))
        for t in range(SWA_STACK):
            rows = slice(t * WINDOW, (t + 1) * WINDOW)
            o_ref[:, t * LANES:(t + 1) * LANES] = jnp.where(half[0], outs[0][rows], outs[1][rows]).astype(BF16)

    prev = lambda hk, n: (jnp.maximum(n - 1, 0), 0)
    cur = lambda hk, n: (n, 0)
    vprev = lambda hk, n: (jnp.maximum(n - 1, 0), v_off // LANES)
    vcur = lambda hk, n: (n, v_off // LANES)
    blk = lambda m: pl.BlockSpec((WINDOW, LANES), m)
    return pl.pallas_call(
        body, name=name, grid=(2, nb),
        in_specs=[pl.BlockSpec(memory_space=pltpu.SMEM),
                  pl.BlockSpec((WINDOW, gw), lambda hk, n: (n, hk)),
                  blk(prev), blk(cur), blk(vprev), blk(vcur)],
        out_specs=pl.BlockSpec((WINDOW, gw), lambda hk, n: (n, hk)),
        out_shape=jax.ShapeDtypeStruct((S, SWA_Q_W), BF16),
        compiler_params=_cparams("parallel", "parallel"),
    )(sinks, q_rope, k_rope, k_rope, proj, proj)


def _swa_bwd(q_rope, k_rope, proj, sinks, d_o, *, v_off, name):
    S = q_rope.shape[0]
    nb = S // WINDOW
    gw = SWA_GROUP * HEAD_DIM

    def body(sink_ref, q_ref, kp_ref, kc_ref, vp_ref, vc_ref, do_ref,
             dq_ref, dkc_ref, dkp_ref, dvc_ref, dvp_ref, dsink_ref):
        hk, n = pl.program_id(0), pl.program_id(1)
        allowed, half, kk, vv = _swa_common(hk, n, kp_ref, kc_ref, vp_ref, vc_ref)
        dk_acc = jnp.zeros((2 * WINDOW, LANES), F32)
        dv_acc = jnp.zeros((2 * WINDOW, LANES), F32)
        srow = lax.broadcasted_iota(jnp.int32, (SWA_GROUP, LANES), 0)
        dsink = jnp.zeros((SWA_GROUP, LANES), F32)
        dqs = []
        for a in range(2):
            qm = _swa_stack(q_ref, half[a], ATT_SCALE)
            dom = _swa_stack(do_ref, half[a])
            p, psink = _swa_probs(qm, kk[a], allowed, _swa_sink_column(sink_ref, hk, a))
            dp = _dot_nt(dom, vv[a])
            delta = jnp.sum(p * dp, axis=1, keepdims=True)
            ds = (p * (dp - delta)).astype(BF16)
            dsk = psink * delta
            for t in range(SWA_STACK):
                dsink = dsink + jnp.where(srow == 2 * t + a, -jnp.sum(dsk[t * WINDOW:(t + 1) * WINDOW]), 0.0)
            dqs.append(_dot_nn(ds, kk[a]) * ATT_SCALE)
            dk_acc = dk_acc + _dot_tn(ds, qm)
            dv_acc = dv_acc + _dot_tn(p.astype(BF16), dom)
        for t in range(SWA_STACK):
            rows = slice(t * WINDOW, (t + 1) * WINDOW)
            dq_ref[:, t * LANES:(t + 1) * LANES] = jnp.where(half[0], dqs[0][rows], dqs[1][rows])
        lane = lax.broadcasted_iota(jnp.int32, (1, LANES), 1)
        mine = (lane >= HEAD_DIM) == (hk == 1)
        dk_t = jnp.where(mine, dk_acc + pltpu.roll(dk_acc, HEAD_DIM, 1), 0.0)
        dv_t = jnp.where(mine, dv_acc + pltpu.roll(dv_acc, HEAD_DIM, 1), 0.0)
        dkp_ref[0] = dk_t[:WINDOW]
        dkc_ref[0] = dk_t[WINDOW:]
        dvp_ref[0] = dv_t[:WINDOW]
        dvc_ref[0] = dv_t[WINDOW:]

        @pl.when(n == 0)
        def _():
            dsink_ref[0] = dsink

        @pl.when(n > 0)
        def _():
            dsink_ref[0] += dsink

    prev = lambda hk, n: (jnp.maximum(n - 1, 0), 0)
    cur = lambda hk, n: (n, 0)
    vprev = lambda hk, n: (jnp.maximum(n - 1, 0), v_off // LANES)
    vcur = lambda hk, n: (n, v_off // LANES)
    blk = lambda m: pl.BlockSpec((WINDOW, LANES), m)
    qblk = pl.BlockSpec((WINDOW, gw), lambda hk, n: (n, hk))
    part = pl.BlockSpec((1, WINDOW, LANES), lambda hk, n: (hk, n, 0))
    part_shape = jax.ShapeDtypeStruct((2, S, LANES), F32)
    return pl.pallas_call(
        body, name=name, grid=(2, nb),
        in_specs=[pl.BlockSpec(memory_space=pltpu.SMEM), qblk, blk(prev), blk(cur), blk(vprev), blk(vcur), qblk],
        out_specs=[qblk, part, part, part, part,
                   pl.BlockSpec((1, SWA_GROUP, LANES), lambda hk, n: (hk, 0, 0))],
        out_shape=[jax.ShapeDtypeStruct((S, SWA_Q_W), F32), part_shape, part_shape, part_shape, part_shape,
                   jax.ShapeDtypeStruct((2, SWA_GROUP, LANES), F32)],
        compiler_params=_cparams("parallel", "arbitrary"),
    )(sinks, q_rope, k_rope, k_rope, proj, proj, d_o)


def _fox_prep(z_t, bias_col, *, name):
    H, S = z_t.shape
    tb = _pick(S, 512)

    def body(z_ref, b_ref, o_ref, carry_ref):
        @pl.when(pl.program_id(0) == 0)
        def _():
            carry_ref[...] = jnp.zeros_like(carry_ref)

        zz = z_ref[...] + b_ref[...]
        t = jnp.exp(-jnp.abs(zz))
        log1p = jnp.where(t < 1e-2, t * (1.0 - t * (0.5 - t * (1.0 / 3.0))), jnp.log(1.0 + t))
        logf = jnp.minimum(zz, 0.0) - log1p
        r = lax.broadcasted_iota(jnp.int32, (tb, tb), 0)
        c = lax.broadcasted_iota(jnp.int32, (tb, tb), 1)
        tri = (r <= c).astype(BF16)
        hi = logf.astype(BF16)
        r1 = logf - hi.astype(F32)
        mid = r1.astype(BF16)
        lo = (r1 - mid.astype(F32)).astype(BF16)
        cs = _dot_nn(hi, tri) + _dot_nn(mid, tri) + _dot_nn(lo, tri) + carry_ref[:, 0:1]
        o_ref[...] = -cs
        carry_ref[...] = jnp.zeros_like(carry_ref) + cs[:, tb - 1:tb]

    return pl.pallas_call(
        body, name=name, grid=(S // tb,),
        in_specs=[pl.BlockSpec((H, tb), lambda i: (0, i)), pl.BlockSpec((H, 1), lambda i: (0, 0))],
        out_specs=pl.BlockSpec((H, tb), lambda i: (0, i)),
        out_shape=jax.ShapeDtypeStruct((H, S), F32),
        scratch_shapes=[pltpu.VMEM((H, LANES), F32)],
        compiler_params=_cparams("arbitrary"),
    )(z_t, bias_col)


def _fox_post(drow, dcol, z_t, bias_col, *, name):
    H, S = z_t.shape
    tb = _pick(S, 512)
    nb = S // tb

    def body(dr_ref, d_ref, z_ref, b_ref, dz_ref, db_ref, carry_ref):
        @pl.when(pl.program_id(0) == 0)
        def _():
            carry_ref[...] = jnp.zeros_like(carry_ref)
            db_ref[...] = jnp.zeros_like(db_ref)

        dc = dr_ref[...] - d_ref[...]
        r = lax.broadcasted_iota(jnp.int32, (tb, tb), 0)
        c = lax.broadcasted_iota(jnp.int32, (tb, tb), 1)
        tri = (r >= c).astype(BF16)
        hi = dc.astype(BF16)
        r1 = dc - hi.astype(F32)
        mid = r1.astype(BF16)
        lo = (r1 - mid.astype(F32)).astype(BF16)
        dlogf = _dot_nn(hi, tri) + _dot_nn(mid, tri) + _dot_nn(lo, tri) + carry_ref[:, 0:1]
        carry_ref[...] = jnp.zeros_like(carry_ref) + dlogf[:, 0:1]
        dz = dlogf * _sigmoid(-(z_ref[...] + b_ref[...]))
        dz_ref[...] = dz
        db_ref[...] += jnp.sum(dz, axis=1, keepdims=True)

    rev = lambda i: (0, nb - 1 - i)
    return pl.pallas_call(
        body, name=name, grid=(nb,),
        in_specs=[pl.BlockSpec((H, tb), rev), pl.BlockSpec((H, tb), rev), pl.BlockSpec((H, tb), rev),
                  pl.BlockSpec((H, 1), lambda i: (0, 0))],
        out_specs=[pl.BlockSpec((H, tb), rev), pl.BlockSpec((H, LANES), lambda i: (0, 0))],
        out_shape=[jax.ShapeDtypeStruct((H, S), F32), jax.ShapeDtypeStruct((H, LANES), F32)],
        scratch_shapes=[pltpu.VMEM((H, LANES), F32)],
        compiler_params=_cparams("arbitrary"),
    )(drow, dcol, z_t, bias_col)


def _fox_blocks(S):
    cap = max(LANES, S // 4)
    return (min(FOX_FWD_BLOCKS[0], cap), min(FOX_FWD_BLOCKS[1], cap)), \
           (min(FOX_BWD_BLOCKS[0], cap), min(FOX_BWD_BLOCKS[1], cap))


def _key_bias_blocks(negc, bk):
    H, S = negc.shape
    return negc.reshape(H // 2, 2, S // bk, bk).transpose(0, 2, 1, 3)


def _fox_fwd(proj, negc4, *, q_off, k_off, v_off, bq, bk, name):
    S = proj.shape[0]
    nq, nk = S // bq, S // bk
    npair = FOX_HEADS // 2
    assert bq % bk == 0 or bk % bq == 0
    nmask = max(1, bq // bk)

    gp = FOX_FWD_PAIRS
    gw = gp * LANES
    assert q_off % gw == 0 and k_off % gw == 0 and v_off % gw == 0 and npair % gp == 0

    def body(q_ref, k_ref, v_ref, nc_ref, o_ref, lse_ref):
        i = pl.program_id(1)
        lane = lax.broadcasted_iota(jnp.int32, (1, LANES), 1)
        half = [lane < HEAD_DIM, lane >= HEAD_DIM]
        qh = []
        for g in range(gp):
            q2 = q_ref[:, g * LANES:(g + 1) * LANES] * jnp.asarray(ATT_SCALE, BF16)
            qh += [jnp.where(half[h], q2, jnp.zeros_like(q2)) for h in range(2)]
        row = lax.broadcasted_iota(jnp.int32, (bq, bk), 0)
        col = lax.broadcasted_iota(jnp.int32, (bq, bk), 1)
        rel = row - col
        nfull = (i * bq) // bk

        spare = [HEAD_DIM, 0]
        ones_lane = [lane == spare[h] for h in range(2)]

        def step(j, carry, masked):
            start = pl.multiple_of(j * bk, bk)
            new = []
            for g in range(gp):
                ks = k_ref[pl.ds(start, bk), g * LANES:(g + 1) * LANES]
                vs = v_ref[pl.ds(start, bk), g * LANES:(g + 1) * LANES]
                nb = nc_ref[g, j]
                for h in range(2):
                    m, acc = carry[4 * g + 2 * h:4 * g + 2 * h + 2]
                    vh = jnp.where(half[h], vs, jnp.where(ones_lane[h], jnp.ones_like(vs), jnp.zeros_like(vs)))
                    qs, bias = qh[2 * g + h], nb[h:h + 1, :]

                    def update(m, acc, rows, keys):
                        s = _dot_nt(qs[rows], ks[keys]) + bias[:, keys]
                        if masked:
                            s = jnp.where(rel[rows, keys] >= j * bk - i * bq, s, NEG)
                        m_new = jnp.maximum(m[rows], jnp.max(s, axis=1, keepdims=True))
                        p = jnp.exp(s - m_new).astype(BF16)
                        return m_new, jnp.exp(m[rows] - m_new) * acc[rows] + _dot_nn(p, vh[keys])

                    if masked and bq == bk:
                        top, bot, everything = slice(0, bq // 2), slice(bq // 2, bq), slice(0, bk)
                        m_t, acc_t = update(m, acc, top, top)
                        m_b, acc_b = update(m, acc, bot, everything)
                        new += [jnp.concatenate([m_t, m_b], axis=0), jnp.concatenate([acc_t, acc_b], axis=0)]
                    else:
                        new += list(update(m, acc, slice(0, bq), slice(0, bk)))
            return tuple(new)

        init = (jnp.full((bq, 1), NEG, F32), jnp.zeros((bq, LANES), F32)) * (2 * gp)
        carry = lax.fori_loop(0, nfull, lambda j, c: step(j, c, False), init)
        for t in range(nmask):
            carry = step(nfull + t, carry, True)
        for g in range(gp):
            outs, lses = [], []
            for h in range(2):
                m, acc = carry[4 * g + 2 * h:4 * g + 2 * h + 2]
                l = acc[:, spare[h]:spare[h] + 1]
                outs.append(acc * (1.0 / l))
                lses.append(m + jnp.log(l))
            o_ref[:, g * LANES:(g + 1) * LANES] = jnp.where(half[0], outs[0], outs[1]).astype(BF16)
            lse_ref[g] = jnp.where(half[0], lses[0], lses[1])

    seq = lambda off: pl.BlockSpec((S, gw), lambda hp, i: (0, off // gw + hp))
    return pl.pallas_call(
        body, name=name, grid=(npair // gp, nq),
        in_specs=[pl.BlockSpec((bq, gw), lambda hp, i: (i, q_off // gw + hp)), seq(k_off), seq(v_off),
                  pl.BlockSpec((gp, nk, 2, bk), lambda hp, i: (hp, 0, 0, 0))],
        out_specs=[pl.BlockSpec((bq, gw), lambda hp, i: (i, hp)),
                   pl.BlockSpec((gp, bq, LANES), lambda hp, i: (hp, i, 0))],
        out_shape=[jax.ShapeDtypeStruct((S, FOX_W), BF16), jax.ShapeDtypeStruct((npair, S, LANES), F32)],
        compiler_params=_cparams("parallel", "parallel"),
    )(proj, proj, proj, negc4)


def _fox_bwd(proj, negc4, o, lse, d_o, q_t, do_t, *, q_off, k_off, v_off, bq, bk, name, deps=()):
    S = proj.shape[0]
    nq, nk = S // bq, S // bk
    npair = FOX_HEADS // 2
    assert bq % bk == 0 or bk % bq == 0
    nmask = max(1, bk // bq)

    def body(q_ref, k_ref, v_ref, nc_ref, o_ref, lse_ref, do_ref, qt_ref, dot_ref, *rest):
        dqo_ref, dk_ref, dv_ref, dn_ref, dr_ref, delta_ref, rs_ref, dq_ref = rest[len(deps):]
        j = pl.program_id(1)
        lane = lax.broadcasted_iota(jnp.int32, (1, LANES), 1)
        half = [lane < HEAD_DIM, lane >= HEAD_DIM]
        spare = [HEAD_DIM, 0]
        ones_lane = [lane == spare[h] for h in range(2)]
        srow = lax.broadcasted_iota(jnp.int32, (LANES, 1), 0)
        rhalf = [srow < HEAD_DIM, srow >= HEAD_DIM]
        ones_row = [srow == spare[h] for h in range(2)]
        k2, v2 = k_ref[...], v_ref[...]
        one_k = jnp.ones_like(k2)
        kh = [jnp.where(half[h], k2, jnp.where(ones_lane[h], one_k, jnp.zeros_like(k2))) for h in range(2)]
        nb = nc_ref[0, 0]
        row = lax.broadcasted_iota(jnp.int32, (bq, bk), 0)
        col = lax.broadcasted_iota(jnp.int32, (bq, bk), 1)
        rel = row - col
        i_first = (j * bk) // bq

        @pl.when(j == 0)
        def _():
            dq_ref[...] = jnp.zeros_like(dq_ref)
            rs_ref[...] = jnp.zeros_like(rs_ref)
            for b in range(nq):
                prod = do_ref[b * bq:(b + 1) * bq, :].astype(F32) * o_ref[b * bq:(b + 1) * bq, :].astype(F32)
                d0 = jnp.sum(jnp.where(half[0], prod, 0.0), axis=1, keepdims=True)
                d1 = jnp.sum(jnp.where(half[1], prod, 0.0), axis=1, keepdims=True)
                delta_ref[b * bq:(b + 1) * bq, :] = jnp.where(half[0], d0, d1)

        def step(i, carry, masked, r0=0):
            dkt_a, dkt_b, dvt = carry
            dkts = [dkt_a, dkt_b]
            nr = bq - r0
            start = pl.multiple_of(i * bq + r0, LANES)
            q2 = q_ref[pl.ds(start, nr), :] * jnp.asarray(ATT_SCALE, BF16)
            do2 = do_ref[pl.ds(start, nr), :]
            qt = qt_ref[i][:, r0:] * jnp.asarray(ATT_SCALE, BF16)
            dot = dot_ref[i][:, r0:]
            lse2 = lse_ref[0, pl.ds(start, nr), :]
            del2 = delta_ref[pl.ds(start, nr), :]
            dqf = []
            for h in range(2):
                qm = jnp.where(half[h], q2, jnp.zeros_like(q2))
                dom = jnp.where(half[h], do2, jnp.zeros_like(do2))
                qtm = jnp.where(rhalf[h], qt, jnp.where(ones_row[h], jnp.ones_like(qt), jnp.zeros_like(qt)))
                dotm = jnp.where(rhalf[h], dot, jnp.zeros_like(dot))
                c0 = h * HEAD_DIM
                p = jnp.exp(_dot_nt(qm, k2) + nb[h:h + 1, :] - lse2[:, c0:c0 + 1])
                if masked:
                    p = jnp.where(rel[r0:] >= j * bk - i * bq, p, 0.0)
                dp = _dot_nt(dom, v2)
                dsb = (p * (dp - del2[:, c0:c0 + 1])).astype(BF16)
                dvt = dvt + _dot_nn(dotm, p.astype(BF16))
                dkts[h] = dkts[h] + _dot_nn(qtm, dsb)
                dqf.append(_dot_nn(dsb, kh[h]))
            dq_ref[pl.ds(start, nr), :] += jnp.where(half[0], dqf[0], dqf[1]) * ATT_SCALE
            rs_ref[pl.ds(start, nr), :] += jnp.where(ones_lane[0], dqf[0], jnp.where(ones_lane[1], dqf[1], 0.0))
            return dkts[0], dkts[1], dvt

        zero = jnp.zeros((LANES, bk), F32)
        carry = (zero, zero, zero)
        if bq > bk:
            sp = j % (bq // bk)
            carry = lax.switch(sp, [functools.partial(step, i_first, masked=True, r0=s * bk)
                                    for s in range(bq // bk)], carry)
        else:
            for t in range(nmask):
                carry = step(i_first + t, carry, True)
        dkt_a, dkt_b, dvt = lax.fori_loop(i_first + nmask, nq, lambda i, c: step(i, c, False), carry)
        dk_ref[...] = jnp.where(rhalf[0], dkt_a, dkt_b).T.astype(BF16)
        dv_ref[...] = dvt.T.astype(BF16)
        dn_ref[0, 0] = jnp.concatenate([dkt_a[spare[0]:spare[0] + 1], dkt_b[spare[1]:spare[1] + 1]], axis=0)

        @pl.when(j == nk - 1)
        def _():
            dqo_ref[...] = dq_ref[...].astype(BF16)
            for b in range(nq):
                t = rs_ref[b * bq:(b + 1) * bq, :].T
                dr_ref[0, b] = jnp.concatenate([t[spare[0]:spare[0] + 1], t[spare[1]:spare[1] + 1]], axis=0)

    once = pl.Buffered(1)
    seq = lambda off: pl.BlockSpec((S, LANES), lambda hp, j: (0, off // LANES + hp), pipeline_mode=once)
    blk = lambda off: pl.BlockSpec((bk, LANES), lambda hp, j: (j, off // LANES + hp))
    nc = pl.BlockSpec((1, 1, 2, bk), lambda hp, j: (hp, j, 0, 0))
    tsp = pl.BlockSpec((nq, LANES, bq), lambda hp, j: (0, hp, 0), pipeline_mode=once)
    return pl.pallas_call(
        body, name=name, grid=(npair, nk),
        in_specs=[seq(q_off), blk(k_off), blk(v_off), nc, seq(0),
                  pl.BlockSpec((1, S, LANES), lambda hp, j: (hp, 0, 0), pipeline_mode=once), seq(0),
                  tsp, tsp] + [_ANY] * len(deps),
        out_specs=[pl.BlockSpec((S, LANES), lambda hp, j: (0, hp)), blk(0), blk(0), nc,
                   pl.BlockSpec((1, nq, 2, bq), lambda hp, j: (hp, 0, 0, 0))],
        out_shape=[jax.ShapeDtypeStruct((S, FOX_W), BF16), jax.ShapeDtypeStruct((S, FOX_W), BF16),
                   jax.ShapeDtypeStruct((S, FOX_W), BF16), jax.ShapeDtypeStruct((npair, nk, 2, bk), F32),
                   jax.ShapeDtypeStruct((npair, nq, 2, bq), F32)],
        scratch_shapes=[pltpu.VMEM((S, LANES), F32), pltpu.VMEM((S, LANES), F32), pltpu.VMEM((S, LANES), F32)],
        compiler_params=_cparams("parallel", "arbitrary"),
    )(proj, proj, proj, negc4, o, lse, d_o, q_t, do_t, *deps)


def _exchange(arrs, *, gather, name):
    n = len(arrs)
    npeer = N_DEV - 1

    def body(*refs):
        ins, outs = refs[:n], refs[n:2 * n]
        send_sems, recv_sems, loc_sems = refs[2 * n:]
        x, y, c = lax.axis_index("x"), lax.axis_index("y"), lax.axis_index("c")
        me = 4 * x + 2 * y + c
        peers = []
        for k in range(1, N_DEV):
            px = 1 - x if k & 4 else x
            py = 1 - y if k & 2 else y
            pc = 1 - c if k & 1 else c
            peers.append(((px, py, pc), 4 * px + 2 * py + pc))

        def remote(w, k):
            dev, idx = peers[k]
            src = ins[w] if gather else ins[w].at[idx]
            return pltpu.make_async_remote_copy(
                src_ref=src, dst_ref=outs[w].at[me],
                send_sem=send_sems.at[w * npeer + k], recv_sem=recv_sems.at[w * npeer + k],
                device_id=dev, device_id_type=pl.DeviceIdType.MESH)

        def arrival(w, k):
            dev, idx = peers[k]
            src = ins[w] if gather else ins[w].at[idx]
            return pltpu.make_async_remote_copy(
                src_ref=src, dst_ref=outs[w].at[idx],
                send_sem=send_sems.at[w * npeer + k], recv_sem=recv_sems.at[w * npeer + k],
                device_id=dev, device_id_type=pl.DeviceIdType.MESH)

        local = []
        for w in range(n):
            for k in range(npeer):
                remote(w, k).start()
            cp = pltpu.make_async_copy(ins[w] if gather else ins[w].at[me], outs[w].at[me], loc_sems.at[w])
            cp.start()
            local.append(cp)
        for w in range(n):
            for k in range(npeer):
                arrival(w, k).wait_recv()
        for w in range(n):
            for k in range(npeer):
                remote(w, k).wait_send()
            local[w].wait()

    hbm = pl.BlockSpec(memory_space=pl.ANY)
    out_shape = [jax.ShapeDtypeStruct((N_DEV,) + (a.shape if gather else a.shape[1:]), a.dtype) for a in arrs]
    return pl.pallas_call(
        body, name=name,
        in_specs=[hbm] * n, out_specs=[hbm] * n, out_shape=out_shape,
        scratch_shapes=[pltpu.SemaphoreType.DMA((n * npeer,)), pltpu.SemaphoreType.DMA((n * npeer,)),
                        pltpu.SemaphoreType.DMA((n,))],
        compiler_params=pltpu.CompilerParams(has_side_effects=True),
    )(*arrs)


def _forward_to_sibling(stack, own, *, name):
    def body(s_ref, own_ref, out_ref, send_sems, recv_sems, local_sem):
        del s_ref
        x, y, c = lax.axis_index("x"), lax.axis_index("y"), lax.axis_index("c")
        sibling = (x, y, 1 - c)
        chips = [(1 - x, y), (x, 1 - y), (1 - x, 1 - y)]

        def slot(px, py, pc):
            return out_ref.at[4 * px + 2 * py + pc]

        def copy(k, block):
            return pltpu.make_async_remote_copy(
                src_ref=slot(*block), dst_ref=slot(*block),
                send_sem=send_sems.at[k], recv_sem=recv_sems.at[k],
                device_id=sibling, device_id_type=pl.DeviceIdType.MESH)

        mine = pltpu.make_async_copy(own_ref, slot(x, y, c), local_sem)
        mine.start()
        passed = [copy(k, (*chip, c)) for k, chip in enumerate(chips)]
        for cp in passed:
            cp.start()
        for k, chip in enumerate(chips):
            copy(k, (*chip, 1 - c)).wait_recv()
        for cp in passed:
            cp.wait_send()
        mine.wait()

    return pl.pallas_call(
        body, name=name,
        in_specs=[_ANY, _ANY], out_specs=_ANY,
        out_shape=jax.ShapeDtypeStruct(stack.shape, stack.dtype),
        scratch_shapes=[pltpu.SemaphoreType.DMA((3,)), pltpu.SemaphoreType.DMA((3,)), pltpu.SemaphoreType.DMA],
        input_output_aliases={0: 0},
        compiler_params=pltpu.CompilerParams(has_side_effects=True),
    )(stack, own)


_HBM = pl.BlockSpec(memory_space=pltpu.HBM)
_SEM = pl.BlockSpec(memory_space=pltpu.SEMAPHORE)
_EFFECT = pltpu.SideEffectType.DATAFLOW_SIDE_EFFECTING
NPEER = N_DEV - 1


def _peer_table():
    x, y, c = lax.axis_index("x"), lax.axis_index("y"), lax.axis_index("c")
    peers = []
    for k in range(1, N_DEV):
        px = 1 - x if k & 4 else x
        py = 1 - y if k & 2 else y
        pc = 1 - c if k & 1 else c
        peers.append(((px, py, pc), 4 * px + 2 * py + pc))
    return 4 * x + 2 * y + c, peers


ALL_PEERS = tuple(range(NPEER))
CHIP_PEERS = (0, 1, 3, 5)


def _split_copy(ins, lands, send_sems, recv_sems, gather, me, peers, ks, w, slot, arriving):
    dev, idx = peers[ks[slot]]
    return pltpu.make_async_remote_copy(
        src_ref=ins[w] if gather else ins[w].at[idx],
        dst_ref=lands[w].at[idx if arriving else me],
        send_sem=send_sems.at[w * len(ks) + slot], recv_sem=recv_sems.at[w * len(ks) + slot],
        device_id=dev, device_id_type=pl.DeviceIdType.MESH)


def _exchange_start(arrs, *, gather, name, deps=(), ks=ALL_PEERS):
    n = len(arrs)
    land_shapes = [(N_DEV,) + (a.shape if gather else a.shape[1:]) for a in arrs]

    def body(*refs):
        ins, lands = refs[:n], refs[n:2 * n]
        send_sems, recv_sems = refs[2 * n + len(deps)], refs[2 * n + len(deps) + 1]
        token = refs[-1]
        me, peers = _peer_table()
        for w in range(n):
            for slot in range(len(ks)):
                _split_copy(ins, lands, send_sems, recv_sems, gather, me, peers, ks, w, slot, False).start()
        token[...] = jnp.zeros_like(token)

    out_shape = ([pltpu.SemaphoreType.DMA((n * len(ks),)), pltpu.SemaphoreType.DMA((n * len(ks),))]
                 + [pltpu.HBM(a.shape, a.dtype) for a in arrs]
                 + [pltpu.HBM(s, a.dtype) for s, a in zip(land_shapes, arrs)]
                 + [jax.ShapeDtypeStruct((8, LANES), F32)])
    res = pl.pallas_call(
        body, name=name,
        in_specs=[_HBM] * (2 * n) + [_ANY] * len(deps),
        out_specs=[_SEM, _SEM] + [_HBM] * (2 * n) + [pl.BlockSpec(memory_space=pltpu.VMEM)],
        out_shape=out_shape,
        input_output_aliases={i: 2 + i for i in range(2 * n)},
        compiler_params=pltpu.CompilerParams(has_side_effects=_EFFECT),
    )(*[pltpu.with_memory_space_constraint(a, pltpu.HBM) for a in arrs],
      *[pltpu.with_memory_space_constraint(lax.empty(s, a.dtype), pltpu.HBM) for s, a in zip(land_shapes, arrs)],
      *deps)
    return (n, gather, ks, res[0], res[1], res[2:2 + n], res[2 + n:2 + 2 * n]), res[-1]


def _exchange_wait(handle, after, *, name):
    n, gather, ks, send_sems, recv_sems, ins_thru, lands_thru = handle

    def body(*refs):
        ins, lands = refs[:n], refs[n:2 * n]
        send_s, recv_s = refs[2 * n], refs[2 * n + 1]
        me, peers = _peer_table()
        for w in range(n):
            for slot in range(len(ks)):
                _split_copy(ins, lands, send_s, recv_s, gather, me, peers, ks, w, slot, False).wait_send()
                _split_copy(ins, lands, send_s, recv_s, gather, me, peers, ks, w, slot, True).wait_recv()

    res = pl.pallas_call(
        body, name=name,
        in_specs=[_HBM] * (2 * n) + [_SEM, _SEM, pl.BlockSpec(memory_space=pl.ANY)],
        out_specs=[_HBM] * (2 * n),
        out_shape=[pltpu.HBM(a.shape, a.dtype) for a in list(ins_thru) + list(lands_thru)],
        input_output_aliases={i: i for i in range(2 * n)},
        compiler_params=pltpu.CompilerParams(has_side_effects=_EFFECT),
    )(*ins_thru, *lands_thru, send_sems, recv_sems, after)
    return res[:n], res[n:2 * n]


def _ordered_sum(s_ref, own_ref):
    if own_ref is None:
        blocks = [s_ref[q].astype(F32) for q in range(N_DEV)]
    else:
        me = 4 * lax.axis_index("x") + 2 * lax.axis_index("y") + lax.axis_index("c")
        own = own_ref[...]
        blocks = [jnp.where(me == q, own, s_ref[q]).astype(F32) for q in range(N_DEV)]
    acc = blocks[0]
    for b in blocks[1:]:
        acc = acc + b
    return acc


def _sum8(stack, own, *, name):
    _, R, C = stack.shape
    if R % 8 == 0:
        tr, tc = _pick(R, max(8, STEP_BYTES // (C * 4 * (N_DEV + 2))), 8), C
    else:
        tr, tc = R, _pick(C, max(LANES, STEP_BYTES // (R * 4 * (N_DEV + 2))))

    def body(s_ref, own_ref, o_ref):
        o_ref[...] = _ordered_sum(s_ref, own_ref)

    blk = pl.BlockSpec((tr, tc), lambda i, j: (i, j))
    return pl.pallas_call(
        body, name=name, grid=(R // tr, C // tc),
        in_specs=[pl.BlockSpec((N_DEV, tr, tc), lambda i, j: (0, i, j)), blk],
        out_specs=blk,
        out_shape=jax.ShapeDtypeStruct((R, C), F32),
        compiler_params=_cparams("parallel", "parallel"),
    )(stack, own)


def _adamw_math(w, g, m, v):
    m = ADAM_B1 * m + (1.0 - ADAM_B1) * g
    v = ADAM_B2 * v + (1.0 - ADAM_B2) * (g * g)
    m_hat = m / (1.0 - ADAM_B1 ** ADAM_STEP)
    v_hat = v / (1.0 - ADAM_B2 ** ADAM_STEP)
    delta = -ADAM_LR * (m_hat / (jnp.sqrt(v_hat) + ADAM_EPS) + ADAM_WD * w)
    return delta, m, v


def _adamw(w, g, m, v, *, name, stacked, own=None):
    R, C = w.shape
    tr = _pick(R, max(8, STEP_BYTES // (C * 4 * (8 + (N_DEV if stacked else 1)))), 8)
    has_own = own is not None

    def body(w_ref, g_ref, m_ref, v_ref, *rest):
        go_ref, d_ref, mo_ref, vo_ref = rest[-4:]
        g = _ordered_sum(g_ref, rest[0] if has_own else None) if stacked else g_ref[...]
        delta, m2, v2 = _adamw_math(w_ref[...], g, m_ref[...], v_ref[...])
        go_ref[...] = g
        d_ref[...] = delta
        mo_ref[...] = m2
        vo_ref[...] = v2

    row = pl.BlockSpec((tr, C), lambda i: (i, 0))
    g_spec = pl.BlockSpec((N_DEV, tr, C), lambda i: (0, i, 0)) if stacked else row
    return pl.pallas_call(
        body, name=name, grid=(R // tr,),
        in_specs=[row, g_spec, row, row] + [row] * has_own, out_specs=[row] * 4,
        out_shape=[jax.ShapeDtypeStruct((R, C), F32)] * 4,
        compiler_params=_cparams("parallel"),
    )(w, g, m, v, *([own] if has_own else []))


def kernel(x, positions, attn_norm, w_in, fox_f_bias, swa_sinks, w_branch_swa, w_branch_fox, w_out, mlp_norm, w_up, w_down, final_norm, loss_target, m_attn_norm, m_w_in, m_fox_f_bias, m_swa_sinks, m_w_branch_swa, m_w_branch_fox, m_w_out, m_mlp_norm, m_w_up, m_w_down, m_final_norm, v_attn_norm, v_w_in, v_fox_f_bias, v_swa_sinks, v_w_branch_swa, v_w_branch_fox, v_w_out, v_mlp_norm, v_w_up, v_w_down, v_final_norm):
    S, D = x.shape[1], x.shape[2]
    DFF = w_up.shape[2] * N_DEV
    d_in = w_in.shape[2] * N_DEV
    assert d_in == QKV_W + FOX_HEADS + 2 * D and (2 * D) % SWA_Q_W == 0 and S % (4 * LANES) == 0
    q_off = 2 * D
    k_off = q_off + SWA_Q_W
    v_off = k_off + SWA_KV_W
    fq_off = v_off + SWA_KV_W
    fk_off = fq_off + FOX_W
    fv_off = fk_off + FOX_W
    fl_off = fv_off + FOX_W
    NP = fl_off + FL_PAD
    x2d, tgt = x[0], loss_target[0]

    shards = [w_in[0].T.astype(BF16), w_branch_swa[0].T.astype(BF16), w_branch_fox[0].T.astype(BF16),
              w_out[0].astype(BF16), w_up[0].T.astype(BF16), w_down[0].astype(BF16)]
    me = 4 * lax.axis_index("x") + 2 * lax.axis_index("y") + lax.axis_index("c")

    def filled(stack, own):
        return lax.dynamic_update_slice(stack, own[None], (me,) + (0,) * own.ndim)

    h_in, tok_in = _exchange_start(shards[:1], gather=True, name="gather_w_in_start", ks=CHIP_PEERS)
    h_rest, tok_rest = _exchange_start(shards[1:], gather=True, name="gather_rest_start", deps=[tok_in])

    tm = _pick(S, 1024)
    td = _pick(D, 1024)
    tf = _pick(DFF, 1024)
    tnp = _pick(NP, 1024)

    h1 = _rms_fwd(x2d, attn_norm, name="rms1", deps=[tok_rest])
    inv_freq = ROPE_THETA ** (-jnp.arange(0, HEAD_DIM, 2, dtype=F32) / HEAD_DIM)
    invf = jnp.tile(inv_freq, LANES // (HEAD_DIM // 2)).reshape(1, LANES)
    cos_t, sin_t = _rope_tables(positions.reshape(S, 1), invf, name="rope_tables", deps=[h1])
    (s_in,), (g_part,) = _exchange_wait(h_in, cos_t, name="gather_w_in_wait")
    g_in = _forward_to_sibling(g_part, s_in, name="gather_w_in_forward")
    w_in_t = g_in.reshape(d_in, D)
    w_in_p = jnp.concatenate([w_in_t[QKV_W + FOX_HEADS:], w_in_t[:QKV_W], w_in_t[QKV_W:QKV_W + FOX_HEADS],
                              jnp.zeros((FL_PAD - FOX_HEADS, D), BF16)], axis=0)
    w_fl_t = w_in_t[QKV_W:QKV_W + FOX_HEADS]
    proj, = _matmul(h1, w_in_p, mode="nt", name="mm_in", out_dtypes=[BF16], tm=_pick(S, 2048), tn=tnp, tk=D)
    z_t, = _matmul(w_fl_t, h1, mode="nt", name="mm_flogit", out_dtypes=[F32],
                   tm=FOX_HEADS, tn=_pick(S, 2048), tk=D)
    bias_col = fox_f_bias.reshape(FOX_HEADS, 1)
    negc = _fox_prep(z_t, bias_col, name="fox_prep")
    (fbq, fbk), (bbq, bbk) = _fox_blocks(S)
    q_rope, k_rope = _rope_fwd(proj, cos_t, sin_t, q_off=q_off, k_off=k_off, name="rope_fwd")
    sinks = swa_sinks.reshape(-1)
    o_a = _swa_fwd(q_rope, k_rope, proj, sinks, v_off=v_off, name="swa_fwd")
    o_b, lse = _fox_fwd(proj, _key_bias_blocks(negc, fbk), q_off=fq_off, k_off=fk_off, v_off=fv_off,
                        bq=fbq, bk=fbk, name="fox_fwd")
    s_rest, g_rest = _exchange_wait(h_rest, o_b, name="gather_rest_wait")
    g_bs, g_bf, g_o, g_up, g_dn = [filled(g, s) for g, s in zip(g_rest, s_rest)]
    w_bs_t = g_bs.reshape(D, SWA_Q_W)
    w_bf_t = g_bf.reshape(D, FOX_W)
    w_o = g_o.reshape(D, D)
    w_up_t = g_up.reshape(DFF, D)
    w_dn = g_dn.reshape(DFF, D)
    ya, = _matmul(o_a, w_bs_t, mode="nt", name="mm_branch_swa", out_dtypes=[BF16], tm=tm, tn=td, tk=SWA_Q_W)
    gate_maps = [lambda i, j, k: (i, j), lambda i, j, k: (i, j), lambda i, j, k: (i, j + D // td)]

    def merge_epi(acc, ya_t, ga_t, gb_t):
        merged = _sigmoid(ga_t.astype(F32)) * ya_t.astype(F32) + _sigmoid(gb_t.astype(F32)) * acc
        return acc, merged

    yb, merged = _matmul(o_b, w_bf_t, mode="nt", name="mm_branch_fox", out_dtypes=[BF16, BF16],
                         tm=tm, tn=td, tk=FOX_W, extras=[ya, proj, proj], extra_maps=gate_maps,
                         epilogue=merge_epi)
    x_mid, = _matmul(merged, w_o, mode="nn", name="mm_out", out_dtypes=[F32], tm=tm, tn=td, tk=D,
                     extras=[x2d], epilogue=lambda acc, r: (acc + r,))
    h2 = _rms_fwd(x_mid, mlp_norm, name="rms2")
    u, = _matmul(h2, w_up_t, mode="nt", name="mm_up", out_dtypes=[BF16], tm=_pick(S, 2048), tn=tf, tk=D,
                 epilogue=lambda acc: (jnp.maximum(acc, 0.0),))
    x_fin, = _matmul(u, w_dn, mode="nn", name="mm_down", out_dtypes=[F32], tm=tm, tn=td, tk=_pick(DFF, 2048),
                     a_fn=_square_bf16, extras=[x_mid], epilogue=lambda acc, r: (acc + r,))

    dx3, dx3b, dg3, loss_part = _loss_head(x_fin, tgt, final_norm.reshape(1, D), name="loss_head")
    d_up, = _matmul(dx3b, w_dn, mode="nt", name="mm_d_act", out_dtypes=[BF16], tm=_pick(S, 2048), tn=tf, tk=D,
                    extras=[u], epilogue=lambda acc, ut: (acc * (2.0 * ut.astype(F32)),))
    tks = _pick(S, 2048)
    dw_dn, = _matmul(u, dx3b, mode="tn", name="mm_dw_down", out_dtypes=[F32], tm=tf, tn=td, tk=tks,
                     a_fn=_square_bf16)
    dh2, = _matmul(d_up, w_up_t, mode="nn", name="mm_dh2", out_dtypes=[F32], tm=tm, tn=td, tk=_pick(DFF, 2048))
    dw_up_t, = _matmul(d_up, h2, mode="tn", name="mm_dw_up", out_dtypes=[F32], tm=tf, tn=td, tk=tks)
    h_s1, tok_s1 = _exchange_start([dw_up_t.reshape(N_DEV, DFF // N_DEV, D), dw_dn.reshape(N_DEV, DFF // N_DEV, D)],
                                   gather=False, name="scatter_mlp_start")
    dx2, dx2b, dg2 = _rms_bwd(dh2, x_mid, mlp_norm, dx3, name="rms2_bwd", want_bf16=True, deps=[tok_s1])

    def gate_bwd_epi(dm, ya_t, yb_t, ga_t, gb_t):
        sa, sb = _sigmoid(ga_t.astype(F32)), _sigmoid(gb_t.astype(F32))
        return (dm * sa, dm * sb, dm * ya_t.astype(F32) * sa * (1.0 - sa), dm * yb_t.astype(F32) * sb * (1.0 - sb))

    gmaps = [lambda i, j, k: (i, j), lambda i, j, k: (i, j), lambda i, j, k: (i, j),
             lambda i, j, k: (i, j + D // td)]
    d_ya, d_yb, d_ga, d_gb = _matmul(dx2b, w_o, mode="nt", name="mm_d_merged", out_dtypes=[BF16] * 4,
                                     tm=tm, tn=td, tk=D, extras=[ya, yb, proj, proj], extra_maps=gmaps,
                                     epilogue=gate_bwd_epi)
    dw_o, = _matmul(merged, dx2b, mode="tn", name="mm_dw_out", out_dtypes=[F32], tm=td, tn=td, tk=tks)
    d_oa, = _matmul(d_ya, w_bs_t, mode="nn", name="mm_d_oa", out_dtypes=[BF16], tm=tm, tn=SWA_Q_W, tk=D)
    d_ob, = _matmul(d_yb, w_bf_t, mode="nn", name="mm_d_ob", out_dtypes=[BF16], tm=tm, tn=FOX_W, tk=D)
    dw_bs_t, = _matmul(d_ya, o_a, mode="tn", name="mm_dw_bs", out_dtypes=[F32], tm=td, tn=SWA_Q_W, tk=tks)
    dw_bf_t, = _matmul(d_yb, o_b, mode="tn", name="mm_dw_bf", out_dtypes=[F32], tm=td, tn=FOX_W, tk=tks)
    h_s2, tok_s2 = _exchange_start([dw_bs_t.reshape(N_DEV, D // N_DEV, SWA_Q_W),
                                    dw_bf_t.reshape(N_DEV, D // N_DEV, FOX_W), dw_o.reshape(N_DEV, D // N_DEV, D)],
                                   gather=False, name="scatter_attn_start")
    def row_blocks_t(a):
        return a.reshape(S // bbq, bbq, FOX_W).transpose(0, 2, 1)

    d_fq, d_fk, d_fv, dcol4, drow4 = _fox_bwd(proj, _key_bias_blocks(negc, bbk), o_b, lse, d_ob,
                                              row_blocks_t(proj[:, fq_off:fq_off + FOX_W]), row_blocks_t(d_ob),
                                              q_off=fq_off, k_off=fk_off, v_off=fv_off, bq=bbq, bk=bbk,
                                              name="fox_bwd", deps=[tok_s2])
    dcol = dcol4.transpose(0, 2, 1, 3).reshape(FOX_HEADS, S)
    drow = drow4.transpose(0, 2, 1, 3).reshape(FOX_HEADS, S)
    dz_t, dbias_l = _fox_post(drow, dcol, z_t, bias_col, name="fox_post")
    dq_r, dk_c, dk_p, dv_c, dv_p, dsink_l = _swa_bwd(q_rope, k_rope, proj, sinks, d_oa, v_off=v_off, name="swa_bwd")
    d_aq, d_ak, d_av = _rope_bwd(dq_r, dk_c, dk_p, dv_c, dv_p, cos_t, sin_t, name="rope_bwd")
    dz_pad = jnp.pad(dz_t.T.astype(BF16), ((0, 0), (0, FL_PAD - FOX_HEADS)))
    d_proj = jnp.concatenate([d_ga, d_gb, d_aq, d_ak, d_av, d_fq, d_fk, d_fv, dz_pad], axis=1)
    tkp = _pick(NP, 2304)
    dw_in_p, = _matmul(d_proj, h1, mode="tn", name="mm_dw_in", out_dtypes=[BF16], tm=_pick(NP, 512), tn=D, tk=tks)
    dw_in_t = jnp.concatenate([dw_in_p[q_off:q_off + QKV_W], dw_in_p[fl_off:fl_off + FOX_HEADS], dw_in_p[:q_off]],
                              axis=0)
    h_s3, tok_s3 = _exchange_start([dw_in_t.reshape(N_DEV, d_in // N_DEV, D)], gather=False,
                                   name="scatter_in_start")
    dh1, = _matmul(d_proj, w_in_p, mode="nn", name="mm_dh1", out_dtypes=[F32], tm=tm, tn=td, tk=tkp, deps=[tok_s3])
    dx, dg1 = _rms_bwd(dh1, x2d, attn_norm, dx2, name="rms1_bwd", want_bf16=False)

    dbias = dbias_l[:, 0]
    dsinks = dsink_l[:, :, 0].reshape(-1)
    nsm = 3 * D + 2 * LANES
    tail = jnp.zeros((2 * LANES,), F32)
    small_g = jnp.concatenate([dg1[0], dg2[0], dg3[0],
                               tail.at[0:16].set(dbias).at[16:32].set(dsinks).at[32].set(loss_part[0, 0])])

    def pack(a_norm, b_norm, f_norm, bias, snk):
        return jnp.concatenate([a_norm[0], b_norm[0], f_norm,
                                tail.at[0:16].set(bias[0]).at[16:32].set(snk[0])]).reshape(1, nsm)

    small_stack, = _exchange([small_g.reshape(1, nsm)], gather=True, name="gather_small")
    u_sm = _adamw(pack(attn_norm, mlp_norm, final_norm, fox_f_bias, swa_sinks), small_stack,
                  pack(m_attn_norm, m_mlp_norm, m_final_norm, m_fox_f_bias, m_swa_sinks),
                  pack(v_attn_norm, v_mlp_norm, v_final_norm, v_fox_f_bias, v_swa_sinks),
                  name="adamw_small", stacked=True)
    loss = u_sm[0][0, 3 * D + 32]

    def own_of(src):
        return lax.dynamic_index_in_dim(src, me, 0, keepdims=False)

    def update_t(stack, src, w, m, v, nm):
        g = _sum8(stack, own_of(src), name="sum_" + nm).T
        return _adamw(w[0], g, m[0], v[0], name="adamw_" + nm, stacked=False)

    def update(stack, src, w, m, v, nm):
        return _adamw(w[0], stack, m[0], v[0], name="adamw_" + nm, stacked=True, own=own_of(src))

    (s_up, s_dn), (r_up, r_dn) = _exchange_wait(h_s1, u_sm[1], name="scatter_mlp_wait")
    u_up = update_t(r_up, s_up, w_up, m_w_up, v_w_up, "w_up")
    u_dn = update(r_dn, s_dn, w_down, m_w_down, v_w_down, "w_down")
    (s_bs, s_bf, s_o), (r_bs, r_bf, r_o) = _exchange_wait(h_s2, u_dn[1], name="scatter_attn_wait")
    u_bs = update_t(r_bs, s_bs, w_branch_swa, m_w_branch_swa, v_w_branch_swa, "w_bs")
    u_bf = update_t(r_bf, s_bf, w_branch_fox, m_w_branch_fox, v_w_branch_fox, "w_bf")
    u_o = update(r_o, s_o, w_out, m_w_out, v_w_out, "w_out")
    (s_w_in,), (r_in,) = _exchange_wait(h_s3, u_o[1], name="scatter_in_wait")
    u_in = update_t(r_in, s_w_in, w_in, m_w_in, v_w_in, "w_in")

    def small(kind):
        a = u_sm[kind][0]
        return dict(attn_norm=a[0:D][None], mlp_norm=a[D:2 * D][None], final_norm=a[2 * D:3 * D],
                    fox_f_bias=a[3 * D:3 * D + 16][None], swa_sinks=a[3 * D + 16:3 * D + 32][None])

    big = dict(w_in=u_in, w_branch_swa=u_bs, w_branch_fox=u_bf, w_out=u_o, w_up=u_up, w_down=u_dn)
    order = ["attn_norm", "w_in", "fox_f_bias", "swa_sinks", "w_branch_swa", "w_branch_fox", "w_out", "mlp_norm",
             "w_up", "w_down", "final_norm"]
    outs = [loss, dx[None]]
    for kind in range(4):
        sm = small(kind)
        for nm in order:
            outs.append(big[nm][kind][None] if nm in big else sm[nm])
    return tuple(outs)
```

```python
import functools

import jax
import jax.numpy as jnp
from jax import lax
from jax.experimental import pallas as pl
from jax.experimental.pallas import tpu as pltpu

F32 = jnp.float32
BF16 = jnp.bfloat16

N_DEV = 8
HEAD_DIM = 64
SWA_Q_W = 1024
SWA_KV_W = 128
SWA_GROUP = 8
WINDOW = 128
FOX_W = 1024
FOX_HEADS = 16
QKV_W = SWA_Q_W + 2 * SWA_KV_W + 3 * FOX_W
FL_PAD = 256
ROPE_THETA = 10000.0
RMS_EPS = 1e-6
ATT_SCALE = 0.125
NEG = -1e30

ADAM_LR = 0.001
ADAM_B1 = 0.9
ADAM_B2 = 0.999
ADAM_EPS = 1e-08
ADAM_WD = 0.01
ADAM_STEP = 10

FOX_FWD_BLOCKS = (1024, 1024)
FOX_BWD_BLOCKS = (1024, 512)
FOX_FWD_PAIRS = 2

LANES = 128
VMEM_LIMIT = 56 * 1024 * 1024
STEP_BYTES = 12 * 1024 * 1024


def _cparams(*sem):
    return pltpu.CompilerParams(dimension_semantics=sem, vmem_limit_bytes=VMEM_LIMIT)


def _pick(dim, pref, align=LANES):
    best = None
    t = align
    while t <= min(dim, pref):
        if dim % t == 0:
            best = t
        t += align
    return best if best is not None else dim


_DIMS = {"nn": ((1,), (0,)), "nt": ((1,), (1,)), "tn": ((0,), (0,))}


_ANY = pl.BlockSpec(memory_space=pl.ANY)


def _matmul(a, b, *, mode, name, out_dtypes, tm, tn, tk, extras=(), extra_maps=None,
            a_fn=None, epilogue=None, deps=()):
    if mode == "nn":
        (M, K), (K2, N) = a.shape, b.shape
    elif mode == "nt":
        (M, K), (N, K2) = a.shape, b.shape
    else:
        (K, M), (K2, N) = a.shape, b.shape
    assert K == K2, (name, a.shape, b.shape)
    assert M % tm == 0 and N % tn == 0 and K % tk == 0, (name, M, N, K, tm, tn, tk)
    nk = K // tk
    ne, no = len(extras), len(out_dtypes)
    dims = (_DIMS[mode], ((), ()))

    def body(*refs):
        a_ref, b_ref = refs[0], refs[1]
        ex_refs = refs[2:2 + ne]
        out_refs = refs[2 + ne + len(deps):2 + ne + len(deps) + no]

        def finish(acc):
            res = (acc,) if epilogue is None else epilogue(acc, *[e[...] for e in ex_refs])
            for o_ref, r in zip(out_refs, res):
                o_ref[...] = r.astype(o_ref.dtype)

        def product():
            av = a_ref[...]
            if a_fn is not None:
                av = a_fn(av)
            return lax.dot_general(av, b_ref[...], dims, preferred_element_type=F32)

        if nk == 1:
            finish(product())
        else:
            acc_ref = refs[-1]
            k = pl.program_id(2)

            @pl.when(k == 0)
            def _():
                acc_ref[...] = jnp.zeros_like(acc_ref)

            acc_ref[...] += product()

            @pl.when(k == nk - 1)
            def _():
                finish(acc_ref[...])

    if mode == "tn":
        a_spec = pl.BlockSpec((tk, tm), lambda i, j, k: (k, i))
    else:
        a_spec = pl.BlockSpec((tm, tk), lambda i, j, k: (i, k))
    if mode == "nt":
        b_spec = pl.BlockSpec((tn, tk), lambda i, j, k: (j, k))
    else:
        b_spec = pl.BlockSpec((tk, tn), lambda i, j, k: (k, j))
    if extra_maps is None:
        extra_maps = [lambda i, j, k: (i, j)] * ne
    ex_specs = [pl.BlockSpec((tm, tn), m) for m in extra_maps]
    out_spec = [pl.BlockSpec((tm, tn), lambda i, j, k: (i, j)) for _ in range(no)]
    res = pl.pallas_call(
        body,
        name=name,
        grid=(M // tm, N // tn, nk),
        in_specs=[a_spec, b_spec] + ex_specs + [_ANY] * len(deps),
        out_specs=out_spec,
        out_shape=[jax.ShapeDtypeStruct((M, N), d) for d in out_dtypes],
        scratch_shapes=[pltpu.VMEM((tm, tn), F32)] if nk > 1 else [],
        compiler_params=_cparams("parallel", "parallel", "arbitrary"),
    )(a, b, *extras, *deps)
    return res


def _square_bf16(t):
    tf = t.astype(F32)
    return (tf * tf).astype(BF16)


def _sigmoid(g):
    return 1.0 / (1.0 + jnp.exp(-g))


def _rms_fwd(x, gain, *, name, deps=()):
    S, D = x.shape
    tr = _pick(S, 512, 8)

    def body(x_ref, g_ref, *rest):
        h_ref = rest[-1]
        xv = x_ref[...]
        r = lax.rsqrt(jnp.mean(xv * xv, axis=-1, keepdims=True) + RMS_EPS)
        h_ref[...] = (xv * r * g_ref[...]).astype(BF16)

    return pl.pallas_call(
        body, name=name, grid=(S // tr,),
        in_specs=[pl.BlockSpec((tr, D), lambda i: (i, 0)), pl.BlockSpec((1, D), lambda i: (0, 0))] + [_ANY] * len(deps),
        out_specs=pl.BlockSpec((tr, D), lambda i: (i, 0)),
        out_shape=jax.ShapeDtypeStruct((S, D), BF16),
        compiler_params=_cparams("parallel"),
    )(x, gain, *deps)


def _rms_bwd(dh, x, gain, dres, *, name, out_dtype, deps=()):
    S, D = x.shape
    tr = _pick(S, 256, 8)

    def body(dh_ref, x_ref, g_ref, dres_ref, *rest):
        outs = rest[len(deps):]
        dx_ref, dg_ref = outs[0], outs[-1]
        xv = x_ref[...]
        r = lax.rsqrt(jnp.mean(xv * xv, axis=-1, keepdims=True) + RMS_EPS)
        xh = xv * r
        dhv = dh_ref[...].astype(F32)
        t = dhv * g_ref[...]
        dx = r * (t - xh * jnp.mean(t * xh, axis=-1, keepdims=True)) + dres_ref[...].astype(F32)
        dx_ref[...] = dx.astype(out_dtype)
        part = jnp.sum(dhv * xh, axis=0, keepdims=True)

        @pl.when(pl.program_id(0) == 0)
        def _():
            dg_ref[...] = part

        @pl.when(pl.program_id(0) > 0)
        def _():
            dg_ref[...] += part

    row = pl.BlockSpec((tr, D), lambda i: (i, 0))
    vec = pl.BlockSpec((1, D), lambda i: (0, 0))
    return pl.pallas_call(
        body, name=name, grid=(S // tr,),
        in_specs=[row, row, vec, row] + [_ANY] * len(deps), out_specs=[row, vec],
        out_shape=[jax.ShapeDtypeStruct((S, D), out_dtype), jax.ShapeDtypeStruct((1, D), F32)],
        compiler_params=_cparams("arbitrary"),
    )(dh, x, gain, dres, *deps)


def _loss_head(x3, target, gain, *, name):
    S, D = x3.shape
    tr = _pick(S, 256, 8)

    def body(x_ref, t_ref, g_ref, dxb_ref, dg_ref, loss_ref):
        xv = x_ref[...]
        r = lax.rsqrt(jnp.mean(xv * xv, axis=-1, keepdims=True) + RMS_EPS)
        xh = xv * r
        gv = g_ref[...]
        err = xh * gv - t_ref[...]
        lpart = jnp.zeros((1, LANES), F32) + (0.5 / D) * jnp.sum(err * err)
        dy = err * (1.0 / D)
        t = dy * gv
        dx = r * (t - xh * jnp.mean(t * xh, axis=-1, keepdims=True))
        dxb_ref[...] = dx.astype(BF16)
        part = jnp.sum(dy * xh, axis=0, keepdims=True)

        @pl.when(pl.program_id(0) == 0)
        def _():
            dg_ref[...] = part
            loss_ref[...] = lpart

        @pl.when(pl.program_id(0) > 0)
        def _():
            dg_ref[...] += part
            loss_ref[...] += lpart

    row = pl.BlockSpec((tr, D), lambda i: (i, 0))
    vec = pl.BlockSpec((1, D), lambda i: (0, 0))
    return pl.pallas_call(
        body, name=name, grid=(S // tr,),
        in_specs=[row, row, vec],
        out_specs=[row, vec, pl.BlockSpec((1, LANES), lambda i: (0, 0))],
        out_shape=[jax.ShapeDtypeStruct((S, D), BF16),
                   jax.ShapeDtypeStruct((1, D), F32), jax.ShapeDtypeStruct((1, LANES), F32)],
        compiler_params=_cparams("arbitrary"),
    )(x3, target, gain)


def _rope_tables(pos_col, invf, *, name):
    S = pos_col.shape[0]
    tr = _pick(S, 512, 8)

    def body(p_ref, f_ref, cos_ref, sin_ref):
        ang = p_ref[...].astype(F32) * f_ref[...]
        lane = lax.broadcasted_iota(jnp.int32, (1, LANES), 1)
        first = (lane % HEAD_DIM) < HEAD_DIM // 2
        sn = jnp.sin(ang)
        cos_ref[...] = jnp.cos(ang)
        sin_ref[...] = jnp.where(first, -sn, sn)

    return pl.pallas_call(
        body, name=name, grid=(S // tr,),
        in_specs=[pl.BlockSpec((tr, 1), lambda i: (i, 0)), pl.BlockSpec((1, LANES), lambda i: (0, 0))],
        out_specs=[pl.BlockSpec((tr, LANES), lambda i: (i, 0))] * 2,
        out_shape=[jax.ShapeDtypeStruct((S, LANES), F32)] * 2,
        compiler_params=_cparams("parallel"),
    )(pos_col, invf)


def _swap_halves(t):
    lane = lax.broadcasted_iota(jnp.int32, (1, LANES), 1)
    first = (lane % HEAD_DIM) < HEAD_DIM // 2
    return jnp.where(first, pltpu.roll(t, LANES - HEAD_DIM // 2, 1), pltpu.roll(t, HEAD_DIM // 2, 1))


def _rope_fwd(proj, cos_t, sin_t, *, q_off, k_off, name):
    S = proj.shape[0]
    tr = _pick(S, 256, 8)
    nqb = SWA_Q_W // LANES

    def body(q_ref, k_ref, c_ref, s_ref, qo_ref, ko_ref):
        cv, sv = c_ref[...], s_ref[...]
        for b in range(nqb):
            t = q_ref[:, b * LANES:(b + 1) * LANES].astype(F32)
            qo_ref[:, b * LANES:(b + 1) * LANES] = (t * cv + _swap_halves(t) * sv).astype(BF16)
        t = k_ref[...].astype(F32)
        ko_ref[...] = (t * cv + _swap_halves(t) * sv).astype(BF16)

    tab = pl.BlockSpec((tr, LANES), lambda i: (i, 0))
    return pl.pallas_call(
        body, name=name, grid=(S // tr,),
        in_specs=[pl.BlockSpec((tr, SWA_Q_W), lambda i: (i, q_off // SWA_Q_W)),
                  pl.BlockSpec((tr, LANES), lambda i: (i, k_off // LANES)), tab, tab],
        out_specs=[pl.BlockSpec((tr, SWA_Q_W), lambda i: (i, 0)), tab],
        out_shape=[jax.ShapeDtypeStruct((S, SWA_Q_W), BF16), jax.ShapeDtypeStruct((S, LANES), BF16)],
        compiler_params=_cparams("parallel"),
    )(proj, proj, cos_t, sin_t)


def _rope_bwd(dq, dk_cur, dk_prev, dv_cur, dv_prev, cos_t, sin_t, *, name):
    S = dq.shape[0]
    tr = WINDOW
    nb = S // tr
    nqb = SWA_Q_W // LANES

    def body(dq_ref, kc_ref, kp_ref, vc_ref, vp_ref, c_ref, s_ref, dqo_ref, dko_ref, dvo_ref):
        cv, sv = c_ref[...], s_ref[...]
        has_next = (pl.program_id(0) + 1 < nb).astype(F32)
        for b in range(nqb):
            d = dq_ref[:, b * LANES:(b + 1) * LANES]
            dqo_ref[:, b * LANES:(b + 1) * LANES] = (d * cv + _swap_halves(d * sv)).astype(BF16)
        d = kc_ref[0] + kc_ref[1] + has_next * (kp_ref[0] + kp_ref[1])
        dko_ref[...] = (d * cv + _swap_halves(d * sv)).astype(BF16)
        dvo_ref[...] = (vc_ref[0] + vc_ref[1] + has_next * (vp_ref[0] + vp_ref[1])).astype(BF16)

    tab = pl.BlockSpec((tr, LANES), lambda i: (i, 0))
    cur = pl.BlockSpec((2, tr, LANES), lambda i: (0, i, 0))
    nxt = pl.BlockSpec((2, tr, LANES), lambda i: (0, jnp.minimum(i + 1, nb - 1), 0))
    return pl.pallas_call(
        body, name=name, grid=(nb,),
        in_specs=[pl.BlockSpec((tr, SWA_Q_W), lambda i: (i, 0)), cur, nxt, cur, nxt, tab, tab],
        out_specs=[pl.BlockSpec((tr, SWA_Q_W), lambda i: (i, 0)), tab, tab],
        out_shape=[jax.ShapeDtypeStruct((S, SWA_Q_W), BF16), jax.ShapeDtypeStruct((S, LANES), BF16),
                   jax.ShapeDtypeStruct((S, LANES), BF16)],
        compiler_params=_cparams("parallel"),
    )(dq, dk_cur, dk_prev, dv_cur, dv_prev, cos_t, sin_t)


def _dot_nt(a, b):
    return lax.dot_general(a, b, (((1,), (1,)), ((), ())), preferred_element_type=F32)


def _dot_tn(a, b):
    return lax.dot_general(a, b, (((0,), (0,)), ((), ())), preferred_element_type=F32)


def _dot_nn(a, b):
    return lax.dot_general(a, b, (((1,), (0,)), ((), ())), preferred_element_type=F32)


def _roll_half(t):
    return pltpu.roll(t.astype(F32), HEAD_DIM, 1).astype(t.dtype)


SWA_STACK = SWA_GROUP // 2


def _swa_common(hk, n, kp_ref, kc_ref, vp_ref, vc_ref):
    k2 = jnp.concatenate([kp_ref[...], kc_ref[...]], axis=0)
    v2 = jnp.concatenate([vp_ref[...], vc_ref[...]], axis=0)
    k_sw, v_sw = _roll_half(k2), _roll_half(v2)
    rows = SWA_STACK * WINDOW
    row = lax.broadcasted_iota(jnp.int32, (rows, 2 * WINDOW), 0) % WINDOW
    col = lax.broadcasted_iota(jnp.int32, (rows, 2 * WINDOW), 1)
    diff = row + WINDOW - col
    allowed = (diff >= 0) & (diff < WINDOW) & ((col >= WINDOW) | (n > 0))
    lane = lax.broadcasted_iota(jnp.int32, (1, LANES), 1)
    half = [lane < HEAD_DIM, lane >= HEAD_DIM]
    kk = [jnp.where(hk == a, k2, k_sw) for a in range(2)]
    vv = [jnp.where(hk == a, v2, v_sw) for a in range(2)]
    return allowed, half, kk, vv


def _swa_stack(ref, mask, scale=None):
    parts = []
    for t in range(SWA_STACK):
        blk = ref[:, t * LANES:(t + 1) * LANES]
        if scale is not None:
            blk = blk * jnp.asarray(scale, blk.dtype)
        parts.append(jnp.where(mask, blk, jnp.zeros_like(blk)))
    return jnp.concatenate(parts, axis=0)


def _swa_sink_column(sink_ref, hk, a):
    blk = lax.broadcasted_iota(jnp.int32, (SWA_STACK * WINDOW, 1), 0) // WINDOW
    col = jnp.zeros((SWA_STACK * WINDOW, 1), F32)
    for t in range(SWA_STACK):
        col = jnp.where(blk == t, sink_ref[hk * SWA_GROUP + 2 * t + a], col)
    return col


def _swa_probs(qm, kk, allowed, sink):
    s = jnp.where(allowed, _dot_nt(qm, kk), NEG)
    m = jnp.maximum(jnp.max(s, axis=1, keepdims=True), sink)
    e = jnp.exp(s - m)
    es = jnp.exp(sink - m)
    inv = 1.0 / (jnp.sum(e, axis=1, keepdims=True) + es)
    return e * inv, es * inv


def _swa_fwd(q_rope, k_rope, proj, sinks, *, v_off, name):
    S = q_rope.shape[0]
    nb = S // WINDOW
    gw = SWA_GROUP * HEAD_DIM

    def body(sink_ref, q_ref, kp_ref, kc_ref, vp_ref, vc_ref, o_ref):
        hk, n = pl.program_id(0), pl.program_id(1)
        allowed, half, kk, vv = _swa_common(hk, n, kp_ref, kc_ref, vp_ref, vc_ref)
        outs = []
        for a in range(2):
            qm = _swa_stack(q_ref, half[a], ATT_SCALE)
            p, _ = _swa_probs(qm, kk[a], allowed, _swa_sink_column(sink_ref, hk, a))
            outs.append(_dot_nn(p.astype(BF16), vv[a]))
        for t in range(SWA_STACK):
            rows = slice(t * WINDOW, (t + 1) * WINDOW)
            o_ref[:, t * LANES:(t + 1) * LANES] = jnp.where(half[0], outs[0][rows], outs[1][rows]).astype(BF16)

    prev = lambda hk, n: (jnp.maximum(n - 1, 0), 0)
    cur = lambda hk, n: (n, 0)
    vprev = lambda hk, n: (jnp.maximum(n - 1, 0), v_off // LANES)
    vcur = lambda hk, n: (n, v_off // LANES)
    blk = lambda m: pl.BlockSpec((WINDOW, LANES), m)
    return pl.pallas_call(
        body, name=name, grid=(2, nb),
        in_specs=[pl.BlockSpec(memory_space=pltpu.SMEM),
                  pl.BlockSpec((WINDOW, gw), lambda hk, n: (n, hk)),
                  blk(prev), blk(cur), blk(vprev), blk(vcur)],
        out_specs=pl.BlockSpec((WINDOW, gw), lambda hk, n: (n, hk)),
        out_shape=jax.ShapeDtypeStruct((S, SWA_Q_W), BF16),
        compiler_params=_cparams("parallel", "parallel"),
    )(sinks, q_rope, k_rope, k_rope, proj, proj)


def _swa_bwd(q_rope, k_rope, proj, sinks, d_o, *, v_off, name):
    S = q_rope.shape[0]
    nb = S // WINDOW
    gw = SWA_GROUP * HEAD_DIM

    def body(sink_ref, q_ref, kp_ref, kc_ref, vp_ref, vc_ref, do_ref,
             dq_ref, dkc_ref, dkp_ref, dvc_ref, dvp_ref, dsink_ref):
        hk, n = pl.program_id(0), pl.program_id(1)
        allowed, half, kk, vv = _swa_common(hk, n, kp_ref, kc_ref, vp_ref, vc_ref)
        dk_acc = jnp.zeros((2 * WINDOW, LANES), F32)
        dv_acc = jnp.zeros((2 * WINDOW, LANES), F32)
        srow = lax.broadcasted_iota(jnp.int32, (SWA_GROUP, LANES), 0)
        dsink = jnp.zeros((SWA_GROUP, LANES), F32)
        dqs = []
        for a in range(2):
            qm = _swa_stack(q_ref, half[a], ATT_SCALE)
            dom = _swa_stack(do_ref, half[a])
            p, psink = _swa_probs(qm, kk[a], allowed, _swa_sink_column(sink_ref, hk, a))
            dp = _dot_nt(dom, vv[a])
            delta = jnp.sum(p * dp, axis=1, keepdims=True)
            ds = (p * (dp - delta)).astype(BF16)
            dsk = psink * delta
            for t in range(SWA_STACK):
                dsink = dsink + jnp.where(srow == 2 * t + a, -jnp.sum(dsk[t * WINDOW:(t + 1) * WINDOW]), 0.0)
            dqs.append(_dot_nn(ds, kk[a]) * ATT_SCALE)
            dk_acc = dk_acc + _dot_tn(ds, qm)
            dv_acc = dv_acc + _dot_tn(p.astype(BF16), dom)
        for t in range(SWA_STACK):
            rows = slice(t * WINDOW, (t + 1) * WINDOW)
            dq_ref[:, t * LANES:(t + 1) * LANES] = jnp.where(half[0], dqs[0][rows], dqs[1][rows])
        lane = lax.broadcasted_iota(jnp.int32, (1, LANES), 1)
        mine = (lane >= HEAD_DIM) == (hk == 1)
        dk_t = jnp.where(mine, dk_acc + pltpu.roll(dk_acc, HEAD_DIM, 1), 0.0)
        dv_t = jnp.where(mine, dv_acc + pltpu.roll(dv_acc, HEAD_DIM, 1), 0.0)
        dkp_ref[0] = dk_t[:WINDOW]
        dkc_ref[0] = dk_t[WINDOW:]
        dvp_ref[0] = dv_t[:WINDOW]
        dvc_ref[0] = dv_t[WINDOW:]

        @pl.when(n == 0)
        def _():
            dsink_ref[0] = dsink

        @pl.when(n > 0)
        def _():
            dsink_ref[0] += dsink

    prev = lambda hk, n: (jnp.maximum(n - 1, 0), 0)
    cur = lambda hk, n: (n, 0)
    vprev = lambda hk, n: (jnp.maximum(n - 1, 0), v_off // LANES)
    vcur = lambda hk, n: (n, v_off // LANES)
    blk = lambda m: pl.BlockSpec((WINDOW, LANES), m)
    qblk = pl.BlockSpec((WINDOW, gw), lambda hk, n: (n, hk))
    part = pl.BlockSpec((1, WINDOW, LANES), lambda hk, n: (hk, n, 0))
    part_shape = jax.ShapeDtypeStruct((2, S, LANES), F32)
    return pl.pallas_call(
        body, name=name, grid=(2, nb),
        in_specs=[pl.BlockSpec(memory_space=pltpu.SMEM), qblk, blk(prev), blk(cur), blk(vprev), blk(vcur), qblk],
        out_specs=[qblk, part, part, part, part,
                   pl.BlockSpec((1, SWA_GROUP, LANES), lambda hk, n: (hk, 0, 0))],
        out_shape=[jax.ShapeDtypeStruct((S, SWA_Q_W), F32), part_shape, part_shape, part_shape, part_shape,
                   jax.ShapeDtypeStruct((2, SWA_GROUP, LANES), F32)],
        compiler_params=_cparams("parallel", "arbitrary"),
    )(sinks, q_rope, k_rope, k_rope, proj, proj, d_o)


def _fox_prep(z_t, bias_col, *, name):
    H, S = z_t.shape
    tb = _pick(S, 512)

    def body(z_ref, b_ref, o_ref, carry_ref):
        @pl.when(pl.program_id(0) == 0)
        def _():
            carry_ref[...] = jnp.zeros_like(carry_ref)

        zz = z_ref[...] + b_ref[...]
        t = jnp.exp(-jnp.abs(zz))
        log1p = jnp.where(t < 1e-2, t * (1.0 - t * (0.5 - t * (1.0 / 3.0))), jnp.log(1.0 + t))
        logf = jnp.minimum(zz, 0.0) - log1p
        r = lax.broadcasted_iota(jnp.int32, (tb, tb), 0)
        c = lax.broadcasted_iota(jnp.int32, (tb, tb), 1)
        tri = (r <= c).astype(BF16)
        hi = logf.astype(BF16)
        r1 = logf - hi.astype(F32)
        mid = r1.astype(BF16)
        lo = (r1 - mid.astype(F32)).astype(BF16)
        cs = _dot_nn(hi, tri) + _dot_nn(mid, tri) + _dot_nn(lo, tri) + carry_ref[:, 0:1]
        o_ref[...] = -cs
        carry_ref[...] = jnp.zeros_like(carry_ref) + cs[:, tb - 1:tb]

    return pl.pallas_call(
        body, name=name, grid=(S // tb,),
        in_specs=[pl.BlockSpec((H, tb), lambda i: (0, i)), pl.BlockSpec((H, 1), lambda i: (0, 0))],
        out_specs=pl.BlockSpec((H, tb), lambda i: (0, i)),
        out_shape=jax.ShapeDtypeStruct((H, S), F32),
        scratch_shapes=[pltpu.VMEM((H, LANES), F32)],
        compiler_params=_cparams("arbitrary"),
    )(z_t, bias_col)


def _fox_post(drow, dcol, z_t, bias_col, *, name):
    H, S = z_t.shape
    tb = _pick(S, 512)
    nb = S // tb

    def body(dr_ref, d_ref, z_ref, b_ref, dz_ref, db_ref, carry_ref):
        @pl.when(pl.program_id(0) == 0)
        def _():
            carry_ref[...] = jnp.zeros_like(carry_ref)
            db_ref[...] = jnp.zeros_like(db_ref)

        dc = dr_ref[...] - d_ref[...]
        r = lax.broadcasted_iota(jnp.int32, (tb, tb), 0)
        c = lax.broadcasted_iota(jnp.int32, (tb, tb), 1)
        tri = (r >= c).astype(BF16)
        hi = dc.astype(BF16)
        r1 = dc - hi.astype(F32)
        mid = r1.astype(BF16)
        lo = (r1 - mid.astype(F32)).astype(BF16)
        dlogf = _dot_nn(hi, tri) + _dot_nn(mid, tri) + _dot_nn(lo, tri) + carry_ref[:, 0:1]
        carry_ref[...] = jnp.zeros_like(carry_ref) + dlogf[:, 0:1]
        dz = dlogf * _sigmoid(-(z_ref[...] + b_ref[...]))
        dz_ref[...] = dz
        db_ref[...] += jnp.sum(dz, axis=1, keepdims=True)

    rev = lambda i: (0, nb - 1 - i)
    return pl.pallas_call(
        body, name=name, grid=(nb,),
        in_specs=[pl.BlockSpec((H, tb), rev), pl.BlockSpec((H, tb), rev), pl.BlockSpec((H, tb), rev),
                  pl.BlockSpec((H, 1), lambda i: (0, 0))],
        out_specs=[pl.BlockSpec((H, tb), rev), pl.BlockSpec((H, LANES), lambda i: (0, 0))],
        out_shape=[jax.ShapeDtypeStruct((H, S), F32), jax.ShapeDtypeStruct((H, LANES), F32)],
        scratch_shapes=[pltpu.VMEM((H, LANES), F32)],
        compiler_params=_cparams("arbitrary"),
    )(drow, dcol, z_t, bias_col)


def _fox_blocks(S):
    cap = max(LANES, S // 4)
    return (min(FOX_FWD_BLOCKS[0], cap), min(FOX_FWD_BLOCKS[1], cap)), \
           (min(FOX_BWD_BLOCKS[0], cap), min(FOX_BWD_BLOCKS[1], cap))


def _key_bias_blocks(negc, bk):
    H, S = negc.shape
    return negc.reshape(H // 2, 2, S // bk, bk).transpose(0, 2, 1, 3)


def _fox_fwd(proj, negc4, *, q_off, k_off, v_off, bq, bk, name):
    S = proj.shape[0]
    nq, nk = S // bq, S // bk
    npair = FOX_HEADS // 2
    assert bq % bk == 0 or bk % bq == 0
    nmask = max(1, bq // bk)

    gp = FOX_FWD_PAIRS
    gw = gp * LANES
    assert q_off % gw == 0 and k_off % gw == 0 and v_off % gw == 0 and npair % gp == 0

    def body(q_ref, k_ref, v_ref, nc_ref, o_ref, lse_ref):
        i = pl.program_id(1)
        lane = lax.broadcasted_iota(jnp.int32, (1, LANES), 1)
        half = [lane < HEAD_DIM, lane >= HEAD_DIM]
        qh = []
        for g in range(gp):
            q2 = q_ref[:, g * LANES:(g + 1) * LANES] * jnp.asarray(ATT_SCALE, BF16)
            qh += [jnp.where(half[h], q2, jnp.zeros_like(q2)) for h in range(2)]
        row = lax.broadcasted_iota(jnp.int32, (bq, bk), 0)
        col = lax.broadcasted_iota(jnp.int32, (bq, bk), 1)
        rel = row - col
        nfull = (i * bq) // bk

        spare = [HEAD_DIM, 0]
        ones_lane = [lane == spare[h] for h in range(2)]

        def step(j, carry, masked):
            start = pl.multiple_of(j * bk, bk)
            new = []
            for g in range(gp):
                ks = k_ref[pl.ds(start, bk), g * LANES:(g + 1) * LANES]
                vs = v_ref[pl.ds(start, bk), g * LANES:(g + 1) * LANES]
                nb = nc_ref[g, j]
                for h in range(2):
                    m, acc = carry[4 * g + 2 * h:4 * g + 2 * h + 2]
                    vh = jnp.where(half[h], vs, jnp.where(ones_lane[h], jnp.ones_like(vs), jnp.zeros_like(vs)))
                    qs, bias = qh[2 * g + h], nb[h:h + 1, :]

                    def update(m, acc, rows, keys):
                        s = _dot_nt(qs[rows], ks[keys]) + bias[:, keys]
                        if masked:
                            s = jnp.where(rel[rows, keys] >= j * bk - i * bq, s, NEG)
                        m_new = jnp.maximum(m[rows], jnp.max(s, axis=1, keepdims=True))
                        p = jnp.exp(s - m_new).astype(BF16)
                        return m_new, jnp.exp(m[rows] - m_new) * acc[rows] + _dot_nn(p, vh[keys])

                    if masked and bq == bk:
                        top, bot, everything = slice(0, bq // 2), slice(bq // 2, bq), slice(0, bk)
                        m_t, acc_t = update(m, acc, top, top)
                        m_b, acc_b = update(m, acc, bot, everything)
                        new += [jnp.concatenate([m_t, m_b], axis=0), jnp.concatenate([acc_t, acc_b], axis=0)]
                    else:
                        new += list(update(m, acc, slice(0, bq), slice(0, bk)))
            return tuple(new)

        init = (jnp.full((bq, 1), NEG, F32), jnp.zeros((bq, LANES), F32)) * (2 * gp)
        carry = lax.fori_loop(0, nfull, lambda j, c: step(j, c, False), init)
        for t in range(nmask):
            carry = step(nfull + t, carry, True)
        for g in range(gp):
            outs, lses = [], []
            for h in range(2):
                m, acc = carry[4 * g + 2 * h:4 * g + 2 * h + 2]
                l = acc[:, spare[h]:spare[h] + 1]
                outs.append(acc * (1.0 / l))
                lses.append(m + jnp.log(l))
            o_ref[:, g * LANES:(g + 1) * LANES] = jnp.where(half[0], outs[0], outs[1]).astype(BF16)
            lse_ref[g] = jnp.where(half[0], lses[0], lses[1])

    seq = lambda off: pl.BlockSpec((S, gw), lambda hp, i: (0, off // gw + hp))
    return pl.pallas_call(
        body, name=name, grid=(npair // gp, nq),
        in_specs=[pl.BlockSpec((bq, gw), lambda hp, i: (i, q_off // gw + hp)), seq(k_off), seq(v_off),
                  pl.BlockSpec((gp, nk, 2, bk), lambda hp, i: (hp, 0, 0, 0))],
        out_specs=[pl.BlockSpec((bq, gw), lambda hp, i: (i, hp)),
                   pl.BlockSpec((gp, bq, LANES), lambda hp, i: (hp, i, 0))],
        out_shape=[jax.ShapeDtypeStruct((S, FOX_W), BF16), jax.ShapeDtypeStruct((npair, S, LANES), F32)],
        compiler_params=_cparams("parallel", "parallel"),
    )(proj, proj, proj, negc4)


def _fox_bwd(proj, negc4, o, lse, d_o, q_t, do_t, *, q_off, k_off, v_off, bq, bk, name, deps=()):
    S = proj.shape[0]
    nq, nk = S // bq, S // bk
    npair = FOX_HEADS // 2
    assert bq % bk == 0 or bk % bq == 0
    nmask = max(1, bk // bq)

    def body(q_ref, k_ref, v_ref, nc_ref, o_ref, lse_ref, do_ref, qt_ref, dot_ref, *rest):
        dqo_ref, dk_ref, dv_ref, dn_ref, dr_ref, delta_ref, rs_ref, dq_ref = rest[len(deps):]
        j = pl.program_id(1)
        lane = lax.broadcasted_iota(jnp.int32, (1, LANES), 1)
        half = [lane < HEAD_DIM, lane >= HEAD_DIM]
        spare = [HEAD_DIM, 0]
        ones_lane = [lane == spare[h] for h in range(2)]
        srow = lax.broadcasted_iota(jnp.int32, (LANES, 1), 0)
        rhalf = [srow < HEAD_DIM, srow >= HEAD_DIM]
        ones_row = [srow == spare[h] for h in range(2)]
        k2, v2 = k_ref[...], v_ref[...]
        one_k = jnp.ones_like(k2)
        kh = [jnp.where(half[h], k2, jnp.where(ones_lane[h], one_k, jnp.zeros_like(k2))) for h in range(2)]
        nb = nc_ref[0, 0]
        row = lax.broadcasted_iota(jnp.int32, (bq, bk), 0)
        col = lax.broadcasted_iota(jnp.int32, (bq, bk), 1)
        rel = row - col
        i_first = (j * bk) // bq

        @pl.when(j == 0)
        def _():
            dq_ref[...] = jnp.zeros_like(dq_ref)
            rs_ref[...] = jnp.zeros_like(rs_ref)
            for b in range(nq):
                prod = do_ref[b * bq:(b + 1) * bq, :].astype(F32) * o_ref[b * bq:(b + 1) * bq, :].astype(F32)
                d0 = jnp.sum(jnp.where(half[0], prod, 0.0), axis=1, keepdims=True)
                d1 = jnp.sum(jnp.where(half[1], prod, 0.0), axis=1, keepdims=True)
                delta_ref[b * bq:(b + 1) * bq, :] = jnp.where(half[0], d0, d1)

        def step(i, carry, masked, r0=0):
            dkt_a, dkt_b, dvt = carry
            dkts = [dkt_a, dkt_b]
            nr = bq - r0
            start = pl.multiple_of(i * bq + r0, LANES)
            q2 = q_ref[pl.ds(start, nr), :] * jnp.asarray(ATT_SCALE, BF16)
            do2 = do_ref[pl.ds(start, nr), :]
            qt = qt_ref[i][:, r0:] * jnp.asarray(ATT_SCALE, BF16)
            dot = dot_ref[i][:, r0:]
            lse2 = lse_ref[0, pl.ds(start, nr), :]
            del2 = delta_ref[pl.ds(start, nr), :]
            dqf = []
            for h in range(2):
                qm = jnp.where(half[h], q2, jnp.zeros_like(q2))
                dom = jnp.where(half[h], do2, jnp.zeros_like(do2))
                qtm = jnp.where(rhalf[h], qt, jnp.where(ones_row[h], jnp.ones_like(qt), jnp.zeros_like(qt)))
                dotm = jnp.where(rhalf[h], dot, jnp.zeros_like(dot))
                c0 = h * HEAD_DIM
                p = jnp.exp(_dot_nt(qm, k2) + nb[h:h + 1, :] - lse2[:, c0:c0 + 1])
                if masked:
                    p = jnp.where(rel[r0:] >= j * bk - i * bq, p, 0.0)
                dp = _dot_nt(dom, v2)
                dsb = (p * (dp - del2[:, c0:c0 + 1])).astype(BF16)
                dvt = dvt + _dot_nn(dotm, p.astype(BF16))
                dkts[h] = dkts[h] + _dot_nn(qtm, dsb)
                dqf.append(_dot_nn(dsb, kh[h]))
            dq_ref[pl.ds(start, nr), :] += jnp.where(half[0], dqf[0], dqf[1]) * ATT_SCALE
            rs_ref[pl.ds(start, nr), :] += jnp.where(ones_lane[0], dqf[0], jnp.where(ones_lane[1], dqf[1], 0.0))
            return dkts[0], dkts[1], dvt

        zero = jnp.zeros((LANES, bk), F32)
        carry = (zero, zero, zero)
        if bq > bk:
            sp = j % (bq // bk)
            carry = lax.switch(sp, [functools.partial(step, i_first, masked=True, r0=s * bk)
                                    for s in range(bq // bk)], carry)
        else:
            for t in range(nmask):
                carry = step(i_first + t, carry, True)
        dkt_a, dkt_b, dvt = lax.fori_loop(i_first + nmask, nq, lambda i, c: step(i, c, False), carry)
        dk_ref[...] = jnp.where(rhalf[0], dkt_a, dkt_b).T.astype(BF16)
        dv_ref[...] = dvt.T.astype(BF16)
        dn_ref[0, 0] = jnp.concatenate([dkt_a[spare[0]:spare[0] + 1], dkt_b[spare[1]:spare[1] + 1]], axis=0)

        @pl.when(j == nk - 1)
        def _():
            dqo_ref[...] = dq_ref[...].astype(BF16)
            for b in range(nq):
                t = rs_ref[b * bq:(b + 1) * bq, :].T
                dr_ref[0, b] = jnp.concatenate([t[spare[0]:spare[0] + 1], t[spare[1]:spare[1] + 1]], axis=0)

    once = pl.Buffered(1)
    seq = lambda off: pl.BlockSpec((S, LANES), lambda hp, j: (0, off // LANES + hp), pipeline_mode=once)
    blk = lambda off: pl.BlockSpec((bk, LANES), lambda hp, j: (j, off // LANES + hp))
    nc = pl.BlockSpec((1, 1, 2, bk), lambda hp, j: (hp, j, 0, 0))
    tsp = pl.BlockSpec((nq, LANES, bq), lambda hp, j: (0, hp, 0), pipeline_mode=once)
    return pl.pallas_call(
        body, name=name, grid=(npair, nk),
        in_specs=[seq(q_off), blk(k_off), blk(v_off), nc, seq(0),
                  pl.BlockSpec((1, S, LANES), lambda hp, j: (hp, 0, 0), pipeline_mode=once), seq(0),
                  tsp, tsp] + [_ANY] * len(deps),
        out_specs=[pl.BlockSpec((S, LANES), lambda hp, j: (0, hp)), blk(0), blk(0), nc,
                   pl.BlockSpec((1, nq, 2, bq), lambda hp, j: (hp, 0, 0, 0))],
        out_shape=[jax.ShapeDtypeStruct((S, FOX_W), BF16), jax.ShapeDtypeStruct((S, FOX_W), BF16),
                   jax.ShapeDtypeStruct((S, FOX_W), BF16), jax.ShapeDtypeStruct((npair, nk, 2, bk), F32),
                   jax.ShapeDtypeStruct((npair, nq, 2, bq), F32)],
        scratch_shapes=[pltpu.VMEM((S, LANES), F32), pltpu.VMEM((S, LANES), F32), pltpu.VMEM((S, LANES), F32)],
        compiler_params=_cparams("parallel", "arbitrary"),
    )(proj, proj, proj, negc4, o, lse, d_o, q_t, do_t, *deps)


def _exchange(arrs, *, gather, name):
    n = len(arrs)
    npeer = N_DEV - 1

    def body(*refs):
        ins, outs = refs[:n], refs[n:2 * n]
        send_sems, recv_sems, loc_sems = refs[2 * n:]
        x, y, c = lax.axis_index("x"), lax.axis_index("y"), lax.axis_index("c")
        me = 4 * x + 2 * y + c
        peers = []
        for k in range(1, N_DEV):
            px = 1 - x if k & 4 else x
            py = 1 - y if k & 2 else y
            pc = 1 - c if k & 1 else c
            peers.append(((px, py, pc), 4 * px + 2 * py + pc))

        def remote(w, k):
            dev, idx = peers[k]
            src = ins[w] if gather else ins[w].at[idx]
            return pltpu.make_async_remote_copy(
                src_ref=src, dst_ref=outs[w].at[me],
                send_sem=send_sems.at[w * npeer + k], recv_sem=recv_sems.at[w * npeer + k],
                device_id=dev, device_id_type=pl.DeviceIdType.MESH)

        def arrival(w, k):
            dev, idx = peers[k]
            src = ins[w] if gather else ins[w].at[idx]
            return pltpu.make_async_remote_copy(
                src_ref=src, dst_ref=outs[w].at[idx],
                send_sem=send_sems.at[w * npeer + k], recv_sem=recv_sems.at[w * npeer + k],
                device_id=dev, device_id_type=pl.DeviceIdType.MESH)

        local = []
        for w in range(n):
            for k in range(npeer):
                remote(w, k).start()
            cp = pltpu.make_async_copy(ins[w] if gather else ins[w].at[me], outs[w].at[me], loc_sems.at[w])
            cp.start()
            local.append(cp)
        for w in range(n):
            for k in range(npeer):
                arrival(w, k).wait_recv()
        for w in range(n):
            for k in range(npeer):
                remote(w, k).wait_send()
            local[w].wait()

    hbm = pl.BlockSpec(memory_space=pl.ANY)
    out_shape = [jax.ShapeDtypeStruct((N_DEV,) + (a.shape if gather else a.shape[1:]), a.dtype) for a in arrs]
    return pl.pallas_call(
        body, name=name,
        in_specs=[hbm] * n, out_specs=[hbm] * n, out_shape=out_shape,
        scratch_shapes=[pltpu.SemaphoreType.DMA((n * npeer,)), pltpu.SemaphoreType.DMA((n * npeer,)),
                        pltpu.SemaphoreType.DMA((n,))],
        compiler_params=pltpu.CompilerParams(has_side_effects=True),
    )(*arrs)


def _gather_two_level(shard, *, name):
    def body(x_ref, out_ref, send_sems, recv_sems, local_sem):
        x, y, c = lax.axis_index("x"), lax.axis_index("y"), lax.axis_index("c")
        me, sibling = (x, y, c), (x, y, 1 - c)
        chips = [(1 - x, y), (x, 1 - y), (1 - x, 1 - y)]

        def slot(px, py, pc):
            return out_ref.at[4 * px + 2 * py + pc]

        def copy(k, block, to, src=None):
            return pltpu.make_async_remote_copy(
                src_ref=slot(*block) if src is None else src, dst_ref=slot(*block),
                send_sem=send_sems.at[k], recv_sem=recv_sems.at[k],
                device_id=to, device_id_type=pl.DeviceIdType.MESH)

        mine = pltpu.make_async_copy(x_ref, slot(*me), local_sem)
        mine.start()
        first = [copy(0, me, sibling, src=x_ref)]
        first += [copy(1 + j, me, (*chip, c), src=x_ref) for j, chip in enumerate(chips)]
        for cp in first:
            cp.start()
        passed = [copy(4 + j, (*chip, c), sibling) for j, chip in enumerate(chips)]
        for j, chip in enumerate(chips):
            copy(1 + j, (*chip, c), me).wait_recv()
            passed[j].start()
        copy(0, sibling, me).wait_recv()
        for j, chip in enumerate(chips):
            copy(4 + j, (*chip, 1 - c), me).wait_recv()
        for cp in first + passed:
            cp.wait_send()
        mine.wait()

    return pl.pallas_call(
        body, name=name,
        in_specs=[_ANY], out_specs=_ANY,
        out_shape=jax.ShapeDtypeStruct((N_DEV,) + shard.shape, shard.dtype),
        scratch_shapes=[pltpu.SemaphoreType.DMA((N_DEV - 1,)), pltpu.SemaphoreType.DMA((N_DEV - 1,)),
                        pltpu.SemaphoreType.DMA],
        compiler_params=pltpu.CompilerParams(has_side_effects=True),
    )(shard)


_HBM = pl.BlockSpec(memory_space=pltpu.HBM)
_SEM = pl.BlockSpec(memory_space=pltpu.SEMAPHORE)
_EFFECT = pltpu.SideEffectType.DATAFLOW_SIDE_EFFECTING
NPEER = N_DEV - 1


def _peer_table():
    x, y, c = lax.axis_index("x"), lax.axis_index("y"), lax.axis_index("c")
    peers = []
    for k in range(1, N_DEV):
        px = 1 - x if k & 4 else x
        py = 1 - y if k & 2 else y
        pc = 1 - c if k & 1 else c
        peers.append(((px, py, pc), 4 * px + 2 * py + pc))
    return 4 * x + 2 * y + c, peers


def _split_copy(ins, lands, send_sems, recv_sems, gather, me, peers, w, k, arriving):
    dev, idx = peers[k]
    return pltpu.make_async_remote_copy(
        src_ref=ins[w] if gather else ins[w].at[idx],
        dst_ref=lands[w].at[idx if arriving else me],
        send_sem=send_sems.at[w * NPEER + k], recv_sem=recv_sems.at[w * NPEER + k],
        device_id=dev, device_id_type=pl.DeviceIdType.MESH)


def _exchange_start(arrs, *, gather, name, deps=()):
    n = len(arrs)
    land_shapes = [(N_DEV,) + (a.shape if gather else a.shape[1:]) for a in arrs]

    def body(*refs):
        ins, lands = refs[:n], refs[n:2 * n]
        send_sems, recv_sems = refs[2 * n + len(deps)], refs[2 * n + len(deps) + 1]
        token = refs[-1]
        me, peers = _peer_table()
        for w in range(n):
            for k in range(NPEER):
                _split_copy(ins, lands, send_sems, recv_sems, gather, me, peers, w, k, False).start()
        token[...] = jnp.zeros_like(token)

    out_shape = ([pltpu.SemaphoreType.DMA((n * NPEER,)), pltpu.SemaphoreType.DMA((n * NPEER,))]
                 + [pltpu.HBM(a.shape, a.dtype) for a in arrs]
                 + [pltpu.HBM(s, a.dtype) for s, a in zip(land_shapes, arrs)]
                 + [jax.ShapeDtypeStruct((8, LANES), F32)])
    res = pl.pallas_call(
        body, name=name,
        in_specs=[_HBM] * (2 * n) + [_ANY] * len(deps),
        out_specs=[_SEM, _SEM] + [_HBM] * (2 * n) + [pl.BlockSpec(memory_space=pltpu.VMEM)],
        out_shape=out_shape,
        input_output_aliases={i: 2 + i for i in range(2 * n)},
        compiler_params=pltpu.CompilerParams(has_side_effects=_EFFECT),
    )(*[pltpu.with_memory_space_constraint(a, pltpu.HBM) for a in arrs],
      *[pltpu.with_memory_space_constraint(lax.empty(s, a.dtype), pltpu.HBM) for s, a in zip(land_shapes, arrs)],
      *deps)
    return (n, gather, res[0], res[1], res[2:2 + n], res[2 + n:2 + 2 * n]), res[-1]


def _exchange_wait(handle, after, *, name):
    n, gather, send_sems, recv_sems, ins_thru, lands_thru = handle

    def body(*refs):
        ins, lands = refs[:n], refs[n:2 * n]
        send_s, recv_s = refs[2 * n], refs[2 * n + 1]
        me, peers = _peer_table()
        for w in range(n):
            for k in range(NPEER):
                _split_copy(ins, lands, send_s, recv_s, gather, me, peers, w, k, False).wait_send()
                _split_copy(ins, lands, send_s, recv_s, gather, me, peers, w, k, True).wait_recv()

    res = pl.pallas_call(
        body, name=name,
        in_specs=[_HBM] * (2 * n) + [_SEM, _SEM, pl.BlockSpec(memory_space=pl.ANY)],
        out_specs=[_HBM] * (2 * n),
        out_shape=[pltpu.HBM(a.shape, a.dtype) for a in list(ins_thru) + list(lands_thru)],
        input_output_aliases={i: i for i in range(2 * n)},
        compiler_params=pltpu.CompilerParams(has_side_effects=_EFFECT),
    )(*ins_thru, *lands_thru, send_sems, recv_sems, after)
    return res[:n], res[n:2 * n]


def _ordered_sum(s_ref, own_ref):
    if own_ref is None:
        blocks = [s_ref[q].astype(F32) for q in range(N_DEV)]
    else:
        me = 4 * lax.axis_index("x") + 2 * lax.axis_index("y") + lax.axis_index("c")
        own = own_ref[...]
        blocks = [jnp.where(me == q, own, s_ref[q]).astype(F32) for q in range(N_DEV)]
    acc = blocks[0]
    for b in blocks[1:]:
        acc = acc + b
    return acc


def _sum8(stack, own, *, name):
    _, R, C = stack.shape
    if R % 8 == 0:
        tr, tc = _pick(R, max(8, STEP_BYTES // (C * 4 * (N_DEV + 2))), 8), C
    else:
        tr, tc = R, _pick(C, max(LANES, STEP_BYTES // (R * 4 * (N_DEV + 2))))

    def body(s_ref, own_ref, o_ref):
        o_ref[...] = _ordered_sum(s_ref, own_ref)

    blk = pl.BlockSpec((tr, tc), lambda i, j: (i, j))
    return pl.pallas_call(
        body, name=name, grid=(R // tr, C // tc),
        in_specs=[pl.BlockSpec((N_DEV, tr, tc), lambda i, j: (0, i, j)), blk],
        out_specs=blk,
        out_shape=jax.ShapeDtypeStruct((R, C), F32),
        compiler_params=_cparams("parallel", "parallel"),
    )(stack, own)


def _adamw_math(w, g, m, v):
    m = ADAM_B1 * m + (1.0 - ADAM_B1) * g
    v = ADAM_B2 * v + (1.0 - ADAM_B2) * (g * g)
    m_hat = m / (1.0 - ADAM_B1 ** ADAM_STEP)
    v_hat = v / (1.0 - ADAM_B2 ** ADAM_STEP)
    delta = -ADAM_LR * (m_hat / (jnp.sqrt(v_hat) + ADAM_EPS) + ADAM_WD * w)
    return delta, m, v


def _adamw(w, g, m, v, *, name, stacked, own=None, transposed=False):
    R, C = w.shape
    if transposed:
        tr = _pick(R, max(LANES, STEP_BYTES // (C * 4 * (9 + N_DEV))))
    else:
        tr = _pick(R, max(8, STEP_BYTES // (C * 4 * (8 + (N_DEV if stacked else 1)))), 8)
    has_own = own is not None

    def body(w_ref, g_ref, m_ref, v_ref, *rest):
        go_ref, d_ref, mo_ref, vo_ref = rest[-4:]
        g = _ordered_sum(g_ref, rest[0] if has_own else None) if stacked else g_ref[...]
        if transposed:
            g = g.T
        delta, m2, v2 = _adamw_math(w_ref[...], g, m_ref[...], v_ref[...])
        go_ref[...] = g
        d_ref[...] = delta
        mo_ref[...] = m2
        vo_ref[...] = v2

    row = pl.BlockSpec((tr, C), lambda i: (i, 0))
    if transposed:
        g_spec, own_spec = pl.BlockSpec((N_DEV, C, tr), lambda i: (0, 0, i)), pl.BlockSpec((C, tr), lambda i: (0, i))
    else:
        g_spec, own_spec = (pl.BlockSpec((N_DEV, tr, C), lambda i: (0, i, 0)) if stacked else row), row
    return pl.pallas_call(
        body, name=name, grid=(R // tr,),
        in_specs=[row, g_spec, row, row] + [own_spec] * has_own, out_specs=[row] * 4,
        out_shape=[jax.ShapeDtypeStruct((R, C), F32)] * 4,
        compiler_params=_cparams("parallel"),
    )(w, g, m, v, *([own] if has_own else []))


def kernel(x, positions, attn_norm, w_in, fox_f_bias, swa_sinks, w_branch_swa, w_branch_fox, w_out, mlp_norm, w_up, w_down, final_norm, loss_target, m_attn_norm, m_w_in, m_fox_f_bias, m_swa_sinks, m_w_branch_swa, m_w_branch_fox, m_w_out, m_mlp_norm, m_w_up, m_w_down, m_final_norm, v_attn_norm, v_w_in, v_fox_f_bias, v_swa_sinks, v_w_branch_swa, v_w_branch_fox, v_w_out, v_mlp_norm, v_w_up, v_w_down, v_final_norm):
    S, D = x.shape[1], x.shape[2]
    DFF = w_up.shape[2] * N_DEV
    d_in = w_in.shape[2] * N_DEV
    assert d_in == QKV_W + FOX_HEADS + 2 * D and (2 * D) % SWA_Q_W == 0 and S % (4 * LANES) == 0
    q_off = 2 * D
    k_off = q_off + SWA_Q_W
    v_off = k_off + SWA_KV_W
    fq_off = v_off + SWA_KV_W
    fk_off = fq_off + FOX_W
    fv_off = fk_off + FOX_W
    fl_off = fv_off + FOX_W
    NP = fl_off + FL_PAD
    x2d, tgt = x[0], loss_target[0]

    shards = [w_in[0].T.astype(BF16), w_branch_swa[0].T.astype(BF16), w_branch_fox[0].T.astype(BF16),
              w_out[0].astype(BF16), w_up[0].T.astype(BF16), w_down[0].astype(BF16)]
    me = 4 * lax.axis_index("x") + 2 * lax.axis_index("y") + lax.axis_index("c")

    def filled(stack, own):
        return lax.dynamic_update_slice(stack, own[None], (me,) + (0,) * own.ndim)

    g_in = _gather_two_level(shards[0], name="gather_w_in")
    h_rest, tok_rest = _exchange_start(shards[1:], gather=True, name="gather_rest_start", deps=[g_in])

    tm = _pick(S, 1024)
    td = _pick(D, 1024)
    tf = _pick(DFF, 1024)
    tnp = _pick(NP, 1024)

    h1 = _rms_fwd(x2d, attn_norm, name="rms1", deps=[tok_rest])
    w_in_t = g_in.reshape(d_in, D)
    w_in_p = jnp.concatenate([w_in_t[QKV_W + FOX_HEADS:], w_in_t[:QKV_W], w_in_t[QKV_W:QKV_W + FOX_HEADS],
                              jnp.zeros((FL_PAD - FOX_HEADS, D), BF16)], axis=0)
    w_fl_t = w_in_t[QKV_W:QKV_W + FOX_HEADS]
    proj, = _matmul(h1, w_in_p, mode="nt", name="mm_in", out_dtypes=[BF16], tm=_pick(S, 2048), tn=tnp, tk=D)
    z_t, = _matmul(w_fl_t, h1, mode="nt", name="mm_flogit", out_dtypes=[F32],
                   tm=FOX_HEADS, tn=_pick(S, 2048), tk=D)
    bias_col = fox_f_bias.reshape(FOX_HEADS, 1)
    negc = _fox_prep(z_t, bias_col, name="fox_prep")
    (fbq, fbk), (bbq, bbk) = _fox_blocks(S)
    inv_freq = ROPE_THETA ** (-jnp.arange(0, HEAD_DIM, 2, dtype=F32) / HEAD_DIM)
    invf = jnp.tile(inv_freq, LANES // (HEAD_DIM // 2)).reshape(1, LANES)
    cos_t, sin_t = _rope_tables(positions.reshape(S, 1), invf, name="rope_tables")
    q_rope, k_rope = _rope_fwd(proj, cos_t, sin_t, q_off=q_off, k_off=k_off, name="rope_fwd")
    sinks = swa_sinks.reshape(-1)
    o_a = _swa_fwd(q_rope, k_rope, proj, sinks, v_off=v_off, name="swa_fwd")
    o_b, lse = _fox_fwd(proj, _key_bias_blocks(negc, fbk), q_off=fq_off, k_off=fk_off, v_off=fv_off,
                        bq=fbq, bk=fbk, name="fox_fwd")
    s_rest, g_rest = _exchange_wait(h_rest, o_b, name="gather_rest_wait")
    g_bs, g_bf, g_o, g_up, g_dn = [filled(g, s) for g, s in zip(g_rest, s_rest)]
    w_bs_t = g_bs.reshape(D, SWA_Q_W)
    w_bf_t = g_bf.reshape(D, FOX_W)
    w_o = g_o.reshape(D, D)
    w_up_t = g_up.reshape(DFF, D)
    w_dn = g_dn.reshape(DFF, D)
    ya, = _matmul(o_a, w_bs_t, mode="nt", name="mm_branch_swa", out_dtypes=[BF16], tm=tm, tn=td, tk=SWA_Q_W)
    gate_maps = [lambda i, j, k: (i, j), lambda i, j, k: (i, j), lambda i, j, k: (i, j + D // td)]

    def merge_epi(acc, ya_t, ga_t, gb_t):
        merged = _sigmoid(ga_t.astype(F32)) * ya_t.astype(F32) + _sigmoid(gb_t.astype(F32)) * acc
        return acc, merged

    yb, merged = _matmul(o_b, w_bf_t, mode="nt", name="mm_branch_fox", out_dtypes=[BF16, BF16],
                         tm=tm, tn=td, tk=FOX_W, extras=[ya, proj, proj], extra_maps=gate_maps,
                         epilogue=merge_epi)
    x_mid, = _matmul(merged, w_o, mode="nn", name="mm_out", out_dtypes=[F32], tm=tm, tn=td, tk=D,
                     extras=[x2d], epilogue=lambda acc, r: (acc + r,))
    h2 = _rms_fwd(x_mid, mlp_norm, name="rms2")
    u, = _matmul(h2, w_up_t, mode="nt", name="mm_up", out_dtypes=[BF16], tm=_pick(S, 2048), tn=tf, tk=D,
                 epilogue=lambda acc: (jnp.maximum(acc, 0.0),))
    x_fin, = _matmul(u, w_dn, mode="nn", name="mm_down", out_dtypes=[F32], tm=tm, tn=td, tk=_pick(DFF, 2048),
                     a_fn=_square_bf16, extras=[x_mid], epilogue=lambda acc, r: (acc + r,))

    dx3b, dg3, loss_part = _loss_head(x_fin, tgt, final_norm.reshape(1, D), name="loss_head")
    d_up, = _matmul(dx3b, w_dn, mode="nt", name="mm_d_act", out_dtypes=[BF16], tm=_pick(S, 2048), tn=tf, tk=D,
                    extras=[u], epilogue=lambda acc, ut: (acc * (2.0 * ut.astype(F32)),))
    tks = _pick(S, 2048)
    dw_dn, = _matmul(u, dx3b, mode="tn", name="mm_dw_down", out_dtypes=[F32], tm=tf, tn=td, tk=tks,
                     a_fn=_square_bf16)
    dh2, = _matmul(d_up, w_up_t, mode="nn", name="mm_dh2", out_dtypes=[BF16], tm=tm, tn=td, tk=_pick(DFF, 2048))
    dw_up_t, = _matmul(d_up, h2, mode="tn", name="mm_dw_up", out_dtypes=[F32], tm=tf, tn=td, tk=tks)
    h_s1, tok_s1 = _exchange_start([dw_up_t.reshape(N_DEV, DFF // N_DEV, D), dw_dn.reshape(N_DEV, DFF // N_DEV, D)],
                                   gather=False, name="scatter_mlp_start")
    dx2b, dg2 = _rms_bwd(dh2, x_mid, mlp_norm, dx3b, name="rms2_bwd", out_dtype=BF16, deps=[tok_s1])

    def gate_bwd_epi(dm, ya_t, yb_t, ga_t, gb_t):
        sa, sb = _sigmoid(ga_t.astype(F32)), _sigmoid(gb_t.astype(F32))
        return (dm * sa, dm * sb, dm * ya_t.astype(F32) * sa * (1.0 - sa), dm * yb_t.astype(F32) * sb * (1.0 - sb))

    gmaps = [lambda i, j, k: (i, j), lambda i, j, k: (i, j), lambda i, j, k: (i, j),
             lambda i, j, k: (i, j + D // td)]
    d_ya, d_yb, d_ga, d_gb = _matmul(dx2b, w_o, mode="nt", name="mm_d_merged", out_dtypes=[BF16] * 4,
                                     tm=tm, tn=td, tk=D, extras=[ya, yb, proj, proj], extra_maps=gmaps,
                                     epilogue=gate_bwd_epi)
    dw_o, = _matmul(merged, dx2b, mode="tn", name="mm_dw_out", out_dtypes=[F32], tm=td, tn=td, tk=tks)
    d_oa, = _matmul(d_ya, w_bs_t, mode="nn", name="mm_d_oa", out_dtypes=[BF16], tm=tm, tn=SWA_Q_W, tk=D)
    d_ob, = _matmul(d_yb, w_bf_t, mode="nn", name="mm_d_ob", out_dtypes=[BF16], tm=tm, tn=FOX_W, tk=D)
    dw_bs_t, = _matmul(d_ya, o_a, mode="tn", name="mm_dw_bs", out_dtypes=[F32], tm=td, tn=SWA_Q_W, tk=tks)
    dw_bf_t, = _matmul(d_yb, o_b, mode="tn", name="mm_dw_bf", out_dtypes=[F32], tm=td, tn=FOX_W, tk=tks)
    h_s2, tok_s2 = _exchange_start([dw_bs_t.reshape(N_DEV, D // N_DEV, SWA_Q_W),
                                    dw_bf_t.reshape(N_DEV, D // N_DEV, FOX_W), dw_o.reshape(N_DEV, D // N_DEV, D)],
                                   gather=False, name="scatter_attn_start")
    def row_blocks_t(a):
        return a.reshape(S // bbq, bbq, FOX_W).transpose(0, 2, 1)

    d_fq, d_fk, d_fv, dcol4, drow4 = _fox_bwd(proj, _key_bias_blocks(negc, bbk), o_b, lse, d_ob,
                                              row_blocks_t(proj[:, fq_off:fq_off + FOX_W]), row_blocks_t(d_ob),
                                              q_off=fq_off, k_off=fk_off, v_off=fv_off, bq=bbq, bk=bbk,
                                              name="fox_bwd", deps=[tok_s2])
    dcol = dcol4.transpose(0, 2, 1, 3).reshape(FOX_HEADS, S)
    drow = drow4.transpose(0, 2, 1, 3).reshape(FOX_HEADS, S)
    dz_t, dbias_l = _fox_post(drow, dcol, z_t, bias_col, name="fox_post")
    dq_r, dk_c, dk_p, dv_c, dv_p, dsink_l = _swa_bwd(q_rope, k_rope, proj, sinks, d_oa, v_off=v_off, name="swa_bwd")
    d_aq, d_ak, d_av = _rope_bwd(dq_r, dk_c, dk_p, dv_c, dv_p, cos_t, sin_t, name="rope_bwd")
    dz_pad = jnp.pad(dz_t.T.astype(BF16), ((0, 0), (0, FL_PAD - FOX_HEADS)))
    d_proj = jnp.concatenate([d_ga, d_gb, d_aq, d_ak, d_av, d_fq, d_fk, d_fv, dz_pad], axis=1)
    tkp = _pick(NP, 2304)
    dw_in_p, = _matmul(d_proj, h1, mode="tn", name="mm_dw_in", out_dtypes=[BF16], tm=_pick(NP, 512), tn=D, tk=tks)
    dw_in_t = jnp.concatenate([dw_in_p[q_off:q_off + QKV_W], dw_in_p[fl_off:fl_off + FOX_HEADS], dw_in_p[:q_off]],
                              axis=0)
    h_s3, tok_s3 = _exchange_start([dw_in_t.reshape(N_DEV, d_in // N_DEV, D)], gather=False,
                                   name="scatter_in_start")
    dh1, = _matmul(d_proj, w_in_p, mode="nn", name="mm_dh1", out_dtypes=[BF16], tm=tm, tn=td, tk=tkp, deps=[tok_s3])
    dx, dg1 = _rms_bwd(dh1, x2d, attn_norm, dx2b, name="rms1_bwd", out_dtype=F32)

    dbias = dbias_l[:, 0]
    dsinks = dsink_l[:, :, 0].reshape(-1)
    nsm = 3 * D + 2 * LANES
    tail = jnp.zeros((2 * LANES,), F32)
    small_g = jnp.concatenate([dg1[0], dg2[0], dg3[0],
                               tail.at[0:16].set(dbias).at[16:32].set(dsinks).at[32].set(loss_part[0, 0])])

    def pack(a_norm, b_norm, f_norm, bias, snk):
        return jnp.concatenate([a_norm[0], b_norm[0], f_norm,
                                tail.at[0:16].set(bias[0]).at[16:32].set(snk[0])]).reshape(1, nsm)

    small_stack, = _exchange([small_g.reshape(1, nsm)], gather=True, name="gather_small")
    u_sm = _adamw(pack(attn_norm, mlp_norm, final_norm, fox_f_bias, swa_sinks), small_stack,
                  pack(m_attn_norm, m_mlp_norm, m_final_norm, m_fox_f_bias, m_swa_sinks),
                  pack(v_attn_norm, v_mlp_norm, v_final_norm, v_fox_f_bias, v_swa_sinks),
                  name="adamw_small", stacked=True)
    loss = u_sm[0][0, 3 * D + 32]

    def own_of(src):
        return lax.dynamic_index_in_dim(src, me, 0, keepdims=False)

    def update_t(stack, src, w, m, v, nm):
        g = _sum8(stack, own_of(src), name="sum_" + nm).T
        return _adamw(w[0], g, m[0], v[0], name="adamw_" + nm, stacked=False)

    def update(stack, src, w, m, v, nm, transposed=False):
        return _adamw(w[0], stack, m[0], v[0], name="adamw_" + nm, stacked=True, own=own_of(src),
                      transposed=transposed)

    (s_up, s_dn), (r_up, r_dn) = _exchange_wait(h_s1, u_sm[1], name="scatter_mlp_wait")
    u_up = update(r_up, s_up, w_up, m_w_up, v_w_up, "w_up", transposed=True)
    u_dn = update(r_dn, s_dn, w_down, m_w_down, v_w_down, "w_down")
    (s_bs, s_bf, s_o), (r_bs, r_bf, r_o) = _exchange_wait(h_s2, u_dn[1], name="scatter_attn_wait")
    u_bs = update(r_bs, s_bs, w_branch_swa, m_w_branch_swa, v_w_branch_swa, "w_bs", transposed=True)
    u_bf = update(r_bf, s_bf, w_branch_fox, m_w_branch_fox, v_w_branch_fox, "w_bf", transposed=True)
    u_o = update(r_o, s_o, w_out, m_w_out, v_w_out, "w_out")
    (s_w_in,), (r_in,) = _exchange_wait(h_s3, u_o[1], name="scatter_in_wait")
    u_in = update_t(r_in, s_w_in, w_in, m_w_in, v_w_in, "w_in")

    def small(kind):
        a = u_sm[kind][0]
        return dict(attn_norm=a[0:D][None], mlp_norm=a[D:2 * D][None], final_norm=a[2 * D:3 * D],
                    fox_f_bias=a[3 * D:3 * D + 16][None], swa_sinks=a[3 * D + 16:3 * D + 32][None])

    big = dict(w_in=u_in, w_branch_swa=u_bs, w_branch_fox=u_bf, w_out=u_o, w_up=u_up, w_down=u_dn)
    order = ["attn_norm", "w_in", "fox_f_bias", "swa_sinks", "w_branch_swa", "w_branch_fox", "w_out", "mlp_norm",
             "w_up", "w_down", "final_norm"]
    outs = [loss, dx[None]]
    for kind in range(4):
        sm = small(kind)
        for nm in order:
            outs.append(big[nm][kind][None] if nm in big else sm[nm])
    return tuple(outs)
```

```python
import functools

import jax
import jax.numpy as jnp
from jax import lax
from jax.experimental import pallas as pl
from jax.experimental.pallas import tpu as pltpu

F32 = jnp.float32
BF16 = jnp.bfloat16

N_DEV = 8
HEAD_DIM = 64
SWA_Q_W = 1024
SWA_KV_W = 128
SWA_GROUP = 8
WINDOW = 128
FOX_W = 1024
FOX_HEADS = 16
QKV_W = SWA_Q_W + 2 * SWA_KV_W + 3 * FOX_W
FL_PAD = 256
ROPE_THETA = 10000.0
RMS_EPS = 1e-6
ATT_SCALE = 0.125
NEG = -1e30

ADAM_LR = 0.001
ADAM_B1 = 0.9
ADAM_B2 = 0.999
ADAM_EPS = 1e-08
ADAM_WD = 0.01
ADAM_STEP = 10

FOX_FWD_BLOCKS = (1024, 1024)
FOX_BWD_BLOCKS = (1024, 512)
FOX_FWD_PAIRS = 2

LANES = 128
VMEM_LIMIT = 56 * 1024 * 1024
STEP_BYTES = 12 * 1024 * 1024


def _cparams(*sem):
    return pltpu.CompilerParams(dimension_semantics=sem, vmem_limit_bytes=VMEM_LIMIT)


def _pick(dim, pref, align=LANES):
    best = None
    t = align
    while t <= min(dim, pref):
        if dim % t == 0:
            best = t
        t += align
    return best if best is not None else dim


_DIMS = {"nn": ((1,), (0,)), "nt": ((1,), (1,)), "tn": ((0,), (0,))}


_ANY = pl.BlockSpec(memory_space=pl.ANY)


def _matmul(a, b, *, mode, name, out_dtypes, tm, tn, tk, extras=(), extra_maps=None,
            a_fn=None, epilogue=None, deps=()):
    if mode == "nn":
        (M, K), (K2, N) = a.shape, b.shape
    elif mode == "nt":
        (M, K), (N, K2) = a.shape, b.shape
    else:
        (K, M), (K2, N) = a.shape, b.shape
    assert K == K2, (name, a.shape, b.shape)
    assert M % tm == 0 and N % tn == 0 and K % tk == 0, (name, M, N, K, tm, tn, tk)
    nk = K // tk
    ne, no = len(extras), len(out_dtypes)
    dims = (_DIMS[mode], ((), ()))

    def body(*refs):
        a_ref, b_ref = refs[0], refs[1]
        ex_refs = refs[2:2 + ne]
        out_refs = refs[2 + ne + len(deps):2 + ne + len(deps) + no]

        def finish(acc):
            res = (acc,) if epilogue is None else epilogue(acc, *[e[...] for e in ex_refs])
            for o_ref, r in zip(out_refs, res):
                o_ref[...] = r.astype(o_ref.dtype)

        def product():
            av = a_ref[...]
            if a_fn is not None:
                av = a_fn(av)
            return lax.dot_general(av, b_ref[...], dims, preferred_element_type=F32)

        if nk == 1:
            finish(product())
        else:
            acc_ref = refs[-1]
            k = pl.program_id(2)

            @pl.when(k == 0)
            def _():
                acc_ref[...] = jnp.zeros_like(acc_ref)

            acc_ref[...] += product()

            @pl.when(k == nk - 1)
            def _():
                finish(acc_ref[...])

    if mode == "tn":
        a_spec = pl.BlockSpec((tk, tm), lambda i, j, k: (k, i))
    else:
        a_spec = pl.BlockSpec((tm, tk), lambda i, j, k: (i, k))
    if mode == "nt":
        b_spec = pl.BlockSpec((tn, tk), lambda i, j, k: (j, k))
    else:
        b_spec = pl.BlockSpec((tk, tn), lambda i, j, k: (k, j))
    if extra_maps is None:
        extra_maps = [lambda i, j, k: (i, j)] * ne
    ex_specs = [pl.BlockSpec((tm, tn), m) for m in extra_maps]
    out_spec = [pl.BlockSpec((tm, tn), lambda i, j, k: (i, j)) for _ in range(no)]
    res = pl.pallas_call(
        body,
        name=name,
        grid=(M // tm, N // tn, nk),
        in_specs=[a_spec, b_spec] + ex_specs + [_ANY] * len(deps),
        out_specs=out_spec,
        out_shape=[jax.ShapeDtypeStruct((M, N), d) for d in out_dtypes],
        scratch_shapes=[pltpu.VMEM((tm, tn), F32)] if nk > 1 else [],
        compiler_params=_cparams("parallel", "parallel", "arbitrary"),
    )(a, b, *extras, *deps)
    return res


def _square_bf16(t):
    tf = t.astype(F32)
    return (tf * tf).astype(BF16)


def _sigmoid(g):
    return 1.0 / (1.0 + jnp.exp(-g))


def _rms_fwd(x, gain, *, name, deps=()):
    S, D = x.shape
    tr = _pick(S, 512, 8)

    def body(x_ref, g_ref, *rest):
        h_ref = rest[-1]
        xv = x_ref[...]
        r = lax.rsqrt(jnp.mean(xv * xv, axis=-1, keepdims=True) + RMS_EPS)
        h_ref[...] = (xv * r * g_ref[...]).astype(BF16)

    return pl.pallas_call(
        body, name=name, grid=(S // tr,),
        in_specs=[pl.BlockSpec((tr, D), lambda i: (i, 0)), pl.BlockSpec((1, D), lambda i: (0, 0))] + [_ANY] * len(deps),
        out_specs=pl.BlockSpec((tr, D), lambda i: (i, 0)),
        out_shape=jax.ShapeDtypeStruct((S, D), BF16),
        compiler_params=_cparams("parallel"),
    )(x, gain, *deps)


def _rms_bwd(dh, x, gain, dres, *, name, out_dtype, deps=()):
    S, D = x.shape
    tr = _pick(S, 256, 8)

    def body(dh_ref, x_ref, g_ref, dres_ref, *rest):
        outs = rest[len(deps):]
        dx_ref, dg_ref = outs[0], outs[-1]
        xv = x_ref[...]
        r = lax.rsqrt(jnp.mean(xv * xv, axis=-1, keepdims=True) + RMS_EPS)
        xh = xv * r
        dhv = dh_ref[...].astype(F32)
        t = dhv * g_ref[...]
        dx = r * (t - xh * jnp.mean(t * xh, axis=-1, keepdims=True)) + dres_ref[...].astype(F32)
        dx_ref[...] = dx.astype(out_dtype)
        part = jnp.sum(dhv * xh, axis=0, keepdims=True)

        @pl.when(pl.program_id(0) == 0)
        def _():
            dg_ref[...] = part

        @pl.when(pl.program_id(0) > 0)
        def _():
            dg_ref[...] += part

    row = pl.BlockSpec((tr, D), lambda i: (i, 0))
    vec = pl.BlockSpec((1, D), lambda i: (0, 0))
    return pl.pallas_call(
        body, name=name, grid=(S // tr,),
        in_specs=[row, row, vec, row] + [_ANY] * len(deps), out_specs=[row, vec],
        out_shape=[jax.ShapeDtypeStruct((S, D), out_dtype), jax.ShapeDtypeStruct((1, D), F32)],
        compiler_params=_cparams("arbitrary"),
    )(dh, x, gain, dres, *deps)


def _loss_head(x3, target, gain, *, name):
    S, D = x3.shape
    tr = _pick(S, 256, 8)

    def body(x_ref, t_ref, g_ref, dxb_ref, dg_ref, loss_ref):
        xv = x_ref[...]
        r = lax.rsqrt(jnp.mean(xv * xv, axis=-1, keepdims=True) + RMS_EPS)
        xh = xv * r
        gv = g_ref[...]
        err = xh * gv - t_ref[...]
        lpart = jnp.zeros((1, LANES), F32) + (0.5 / D) * jnp.sum(err * err)
        dy = err * (1.0 / D)
        t = dy * gv
        dx = r * (t - xh * jnp.mean(t * xh, axis=-1, keepdims=True))
        dxb_ref[...] = dx.astype(BF16)
        part = jnp.sum(dy * xh, axis=0, keepdims=True)

        @pl.when(pl.program_id(0) == 0)
        def _():
            dg_ref[...] = part
            loss_ref[...] = lpart

        @pl.when(pl.program_id(0) > 0)
        def _():
            dg_ref[...] += part
            loss_ref[...] += lpart

    row = pl.BlockSpec((tr, D), lambda i: (i, 0))
    vec = pl.BlockSpec((1, D), lambda i: (0, 0))
    return pl.pallas_call(
        body, name=name, grid=(S // tr,),
        in_specs=[row, row, vec],
        out_specs=[row, vec, pl.BlockSpec((1, LANES), lambda i: (0, 0))],
        out_shape=[jax.ShapeDtypeStruct((S, D), BF16),
                   jax.ShapeDtypeStruct((1, D), F32), jax.ShapeDtypeStruct((1, LANES), F32)],
        compiler_params=_cparams("arbitrary"),
    )(x3, target, gain)


def _rope_tables(pos_col, invf, *, name):
    S = pos_col.shape[0]
    tr = _pick(S, 512, 8)

    def body(p_ref, f_ref, cos_ref, sin_ref):
        ang = p_ref[...].astype(F32) * f_ref[...]
        lane = lax.broadcasted_iota(jnp.int32, (1, LANES), 1)
        first = (lane % HEAD_DIM) < HEAD_DIM // 2
        sn = jnp.sin(ang)
        cos_ref[...] = jnp.cos(ang)
        sin_ref[...] = jnp.where(first, -sn, sn)

    return pl.pallas_call(
        body, name=name, grid=(S // tr,),
        in_specs=[pl.BlockSpec((tr, 1), lambda i: (i, 0)), pl.BlockSpec((1, LANES), lambda i: (0, 0))],
        out_specs=[pl.BlockSpec((tr, LANES), lambda i: (i, 0))] * 2,
        out_shape=[jax.ShapeDtypeStruct((S, LANES), F32)] * 2,
        compiler_params=_cparams("parallel"),
    )(pos_col, invf)


def _swap_halves(t):
    lane = lax.broadcasted_iota(jnp.int32, (1, LANES), 1)
    first = (lane % HEAD_DIM) < HEAD_DIM // 2
    return jnp.where(first, pltpu.roll(t, LANES - HEAD_DIM // 2, 1), pltpu.roll(t, HEAD_DIM // 2, 1))


def _rope_fwd(proj, cos_t, sin_t, *, q_off, k_off, name):
    S = proj.shape[0]
    tr = _pick(S, 256, 8)
    nqb = SWA_Q_W // LANES

    def body(q_ref, k_ref, c_ref, s_ref, qo_ref, ko_ref):
        cv, sv = c_ref[...], s_ref[...]
        for b in range(nqb):
            t = q_ref[:, b * LANES:(b + 1) * LANES].astype(F32)
            qo_ref[:, b * LANES:(b + 1) * LANES] = (t * cv + _swap_halves(t) * sv).astype(BF16)
        t = k_ref[...].astype(F32)
        ko_ref[...] = (t * cv + _swap_halves(t) * sv).astype(BF16)

    tab = pl.BlockSpec((tr, LANES), lambda i: (i, 0))
    return pl.pallas_call(
        body, name=name, grid=(S // tr,),
        in_specs=[pl.BlockSpec((tr, SWA_Q_W), lambda i: (i, q_off // SWA_Q_W)),
                  pl.BlockSpec((tr, LANES), lambda i: (i, k_off // LANES)), tab, tab],
        out_specs=[pl.BlockSpec((tr, SWA_Q_W), lambda i: (i, 0)), tab],
        out_shape=[jax.ShapeDtypeStruct((S, SWA_Q_W), BF16), jax.ShapeDtypeStruct((S, LANES), BF16)],
        compiler_params=_cparams("parallel"),
    )(proj, proj, cos_t, sin_t)


def _rope_bwd(dq, dk_cur, dk_prev, dv_cur, dv_prev, cos_t, sin_t, *, name):
    S = dq.shape[0]
    tr = WINDOW
    nb = S // tr
    nqb = SWA_Q_W // LANES

    def body(dq_ref, kc_ref, kp_ref, vc_ref, vp_ref, c_ref, s_ref, dqo_ref, dko_ref, dvo_ref):
        cv, sv = c_ref[...], s_ref[...]
        has_next = (pl.program_id(0) + 1 < nb).astype(F32)
        for b in range(nqb):
            d = dq_ref[:, b * LANES:(b + 1) * LANES]
            dqo_ref[:, b * LANES:(b + 1) * LANES] = (d * cv + _swap_halves(d * sv)).astype(BF16)
        d = kc_ref[0] + kc_ref[1] + has_next * (kp_ref[0] + kp_ref[1])
        dko_ref[...] = (d * cv + _swap_halves(d * sv)).astype(BF16)
        dvo_ref[...] = (vc_ref[0] + vc_ref[1] + has_next * (vp_ref[0] + vp_ref[1])).astype(BF16)

    tab = pl.BlockSpec((tr, LANES), lambda i: (i, 0))
    cur = pl.BlockSpec((2, tr, LANES), lambda i: (0, i, 0))
    nxt = pl.BlockSpec((2, tr, LANES), lambda i: (0, jnp.minimum(i + 1, nb - 1), 0))
    return pl.pallas_call(
        body, name=name, grid=(nb,),
        in_specs=[pl.BlockSpec((tr, SWA_Q_W), lambda i: (i, 0)), cur, nxt, cur, nxt, tab, tab],
        out_specs=[pl.BlockSpec((tr, SWA_Q_W), lambda i: (i, 0)), tab, tab],
        out_shape=[jax.ShapeDtypeStruct((S, SWA_Q_W), BF16), jax.ShapeDtypeStruct((S, LANES), BF16),
                   jax.ShapeDtypeStruct((S, LANES), BF16)],
        compiler_params=_cparams("parallel"),
    )(dq, dk_cur, dk_prev, dv_cur, dv_prev, cos_t, sin_t)


def _dot_nt(a, b):
    return lax.dot_general(a, b, (((1,), (1,)), ((), ())), preferred_element_type=F32)


def _dot_tn(a, b):
    return lax.dot_general(a, b, (((0,), (0,)), ((), ())), preferred_element_type=F32)


def _dot_nn(a, b):
    return lax.dot_general(a, b, (((1,), (0,)), ((), ())), preferred_element_type=F32)


def _roll_half(t):
    return pltpu.roll(t.astype(F32), HEAD_DIM, 1).astype(t.dtype)


SWA_STACK = SWA_GROUP // 2


def _swa_common(hk, n, kp_ref, kc_ref, vp_ref, vc_ref):
    k2 = jnp.concatenate([kp_ref[...], kc_ref[...]], axis=0)
    v2 = jnp.concatenate([vp_ref[...], vc_ref[...]], axis=0)
    k_sw, v_sw = _roll_half(k2), _roll_half(v2)
    rows = SWA_STACK * WINDOW
    row = lax.broadcasted_iota(jnp.int32, (rows, 2 * WINDOW), 0) % WINDOW
    col = lax.broadcasted_iota(jnp.int32, (rows, 2 * WINDOW), 1)
    diff = row + WINDOW - col
    allowed = (diff >= 0) & (diff < WINDOW) & ((col >= WINDOW) | (n > 0))
    lane = lax.broadcasted_iota(jnp.int32, (1, LANES), 1)
    half = [lane < HEAD_DIM, lane >= HEAD_DIM]
    kk = [jnp.where(hk == a, k2, k_sw) for a in range(2)]
    vv = [jnp.where(hk == a, v2, v_sw) for a in range(2)]
    return allowed, half, kk, vv


def _swa_stack(ref, mask, scale=None):
    parts = []
    for t in range(SWA_STACK):
        blk = ref[:, t * LANES:(t + 1) * LANES]
        if scale is not None:
            blk = blk * jnp.asarray(scale, blk.dtype)
        parts.append(jnp.where(mask, blk, jnp.zeros_like(blk)))
    return jnp.concatenate(parts, axis=0)


def _swa_sink_column(sink_ref, hk, a):
    blk = lax.broadcasted_iota(jnp.int32, (SWA_STACK * WINDOW, 1), 0) // WINDOW
    col = jnp.zeros((SWA_STACK * WINDOW, 1), F32)
    for t in range(SWA_STACK):
        col = jnp.where(blk == t, sink_ref[hk * SWA_GROUP + 2 * t + a], col)
    return col


def _swa_probs(qm, kk, allowed, sink):
    s = jnp.where(allowed, _dot_nt(qm, kk), NEG)
    m = jnp.maximum(jnp.max(s, axis=1, keepdims=True), sink)
    e = jnp.exp(s - m)
    es = jnp.exp(sink - m)
    inv = 1.0 / (jnp.sum(e, axis=1, keepdims=True) + es)
    return e * inv, es * inv


def _swa_fwd(q_rope, k_rope, proj, sinks, *, v_off, name):
    S = q_rope.shape[0]
    nb = S // WINDOW
    gw = SWA_GROUP * HEAD_DIM

    def body(sink_ref, q_ref, kp_ref, kc_ref, vp_ref, vc_ref, o_ref):
        hk, n = pl.program_id(0), pl.program_id(1)
        allowed, half, kk, vv = _swa_common(hk, n, kp_ref, kc_ref, vp_ref, vc_ref)
        outs = []
        for a in range(2):
            qm = _swa_stack(q_ref, half[a], ATT_SCALE)
            p, _ = _swa_probs(qm, kk[a], allowed, _swa_sink_column(sink_ref, hk, a))
            outs.append(_dot_nn(p.astype(BF16), vv[a]))
        for t in range(SWA_STACK):
            rows = slice(t * WINDOW, (t + 1) * WINDOW)
            o_ref[:, t * LANES:(t + 1) * LANES] = jnp.where(half[0], outs[0][rows], outs[1][rows]).astype(BF16)

    prev = lambda hk, n: (jnp.maximum(n - 1, 0), 0)
    cur = lambda hk, n: (n, 0)
    vprev = lambda hk, n: (jnp.maximum(n - 1, 0), v_off // LANES)
    vcur = lambda hk, n: (n, v_off // LANES)
    blk = lambda m: pl.BlockSpec((WINDOW, LANES), m)
    return pl.pallas_call(
        body, name=name, grid=(2, nb),
        in_specs=[pl.BlockSpec(memory_space=pltpu.SMEM),
                  pl.BlockSpec((WINDOW, gw), lambda hk, n: (n, hk)),
                  blk(prev), blk(cur), blk(vprev), blk(vcur)],
        out_specs=pl.BlockSpec((WINDOW, gw), lambda hk, n: (n, hk)),
        out_shape=jax.ShapeDtypeStruct((S, SWA_Q_W), BF16),
        compiler_params=_cparams("parallel", "parallel"),
    )(sinks, q_rope, k_rope, k_rope, proj, proj)


def _swa_bwd(q_rope, k_rope, proj, sinks, d_o, *, v_off, name):
    S = q_rope.shape[0]
    nb = S // WINDOW
    gw = SWA_GROUP * HEAD_DIM

    def body(sink_ref, q_ref, kp_ref, kc_ref, vp_ref, vc_ref, do_ref,
             dq_ref, dkc_ref, dkp_ref, dvc_ref, dvp_ref, dsink_ref):
        hk, n = pl.program_id(0), pl.program_id(1)
        allowed, half, kk, vv = _swa_common(hk, n, kp_ref, kc_ref, vp_ref, vc_ref)
        dk_acc = jnp.zeros((2 * WINDOW, LANES), F32)
        dv_acc = jnp.zeros((2 * WINDOW, LANES), F32)
        srow = lax.broadcasted_iota(jnp.int32, (SWA_GROUP, LANES), 0)
        dsink = jnp.zeros((SWA_GROUP, LANES), F32)
        dqs = []
        for a in range(2):
            qm = _swa_stack(q_ref, half[a], ATT_SCALE)
            dom = _swa_stack(do_ref, half[a])
            p, psink = _swa_probs(qm, kk[a], allowed, _swa_sink_column(sink_ref, hk, a))
            dp = _dot_nt(dom, vv[a])
            delta = jnp.sum(p * dp, axis=1, keepdims=True)
            ds = (p * (dp - delta)).astype(BF16)
            dsk = psink * delta
            for t in range(SWA_STACK):
                dsink = dsink + jnp.where(srow == 2 * t + a, -jnp.sum(dsk[t * WINDOW:(t + 1) * WINDOW]), 0.0)
            dqs.append(_dot_nn(ds, kk[a]) * ATT_SCALE)
            dk_acc = dk_acc + _dot_tn(ds, qm)
            dv_acc = dv_acc + _dot_tn(p.astype(BF16), dom)
        for t in range(SWA_STACK):
            rows = slice(t * WINDOW, (t + 1) * WINDOW)
            dq_ref[:, t * LANES:(t + 1) * LANES] = jnp.where(half[0], dqs[0][rows], dqs[1][rows])
        lane = lax.broadcasted_iota(jnp.int32, (1, LANES), 1)
        mine = (lane >= HEAD_DIM) == (hk == 1)
        dk_t = jnp.where(mine, dk_acc + pltpu.roll(dk_acc, HEAD_DIM, 1), 0.0)
        dv_t = jnp.where(mine, dv_acc + pltpu.roll(dv_acc, HEAD_DIM, 1), 0.0)
        dkp_ref[0] = dk_t[:WINDOW]
        dkc_ref[0] = dk_t[WINDOW:]
        dvp_ref[0] = dv_t[:WINDOW]
        dvc_ref[0] = dv_t[WINDOW:]

        @pl.when(n == 0)
        def _():
            dsink_ref[0] = dsink

        @pl.when(n > 0)
        def _():
            dsink_ref[0] += dsink

    prev = lambda hk, n: (jnp.maximum(n - 1, 0), 0)
    cur = lambda hk, n: (n, 0)
    vprev = lambda hk, n: (jnp.maximum(n - 1, 0), v_off // LANES)
    vcur = lambda hk, n: (n, v_off // LANES)
    blk = lambda m: pl.BlockSpec((WINDOW, LANES), m)
    qblk = pl.BlockSpec((WINDOW, gw), lambda hk, n: (n, hk))
    part = pl.BlockSpec((1, WINDOW, LANES), lambda hk, n: (hk, n, 0))
    part_shape = jax.ShapeDtypeStruct((2, S, LANES), F32)
    return pl.pallas_call(
        body, name=name, grid=(2, nb),
        in_specs=[pl.BlockSpec(memory_space=pltpu.SMEM), qblk, blk(prev), blk(cur), blk(vprev), blk(vcur), qblk],
        out_specs=[qblk, part, part, part, part,
                   pl.BlockSpec((1, SWA_GROUP, LANES), lambda hk, n: (hk, 0, 0))],
        out_shape=[jax.ShapeDtypeStruct((S, SWA_Q_W), F32), part_shape, part_shape, part_shape, part_shape,
                   jax.ShapeDtypeStruct((2, SWA_GROUP, LANES), F32)],
        compiler_params=_cparams("parallel", "arbitrary"),
    )(sinks, q_rope, k_rope, k_rope, proj, proj, d_o)


def _fox_prep(z_t, bias_col, *, name):
    H, S = z_t.shape
    tb = _pick(S, 512)

    def body(z_ref, b_ref, o_ref, carry_ref):
        @pl.when(pl.program_id(0) == 0)
        def _():
            carry_ref[...] = jnp.zeros_like(carry_ref)

        zz = z_ref[...] + b_ref[...]
        t = jnp.exp(-jnp.abs(zz))
        log1p = jnp.where(t < 1e-2, t * (1.0 - t * (0.5 - t * (1.0 / 3.0))), jnp.log(1.0 + t))
        logf = jnp.minimum(zz, 0.0) - log1p
        r = lax.broadcasted_iota(jnp.int32, (tb, tb), 0)
        c = lax.broadcasted_iota(jnp.int32, (tb, tb), 1)
        tri = (r <= c).astype(BF16)
        hi = logf.astype(BF16)
        r1 = logf - hi.astype(F32)
        mid = r1.astype(BF16)
        lo = (r1 - mid.astype(F32)).astype(BF16)
        cs = _dot_nn(hi, tri) + _dot_nn(mid, tri) + _dot_nn(lo, tri) + carry_ref[:, 0:1]
        o_ref[...] = -cs
        carry_ref[...] = jnp.zeros_like(carry_ref) + cs[:, tb - 1:tb]

    return pl.pallas_call(
        body, name=name, grid=(S // tb,),
        in_specs=[pl.BlockSpec((H, tb), lambda i: (0, i)), pl.BlockSpec((H, 1), lambda i: (0, 0))],
        out_specs=pl.BlockSpec((H, tb), lambda i: (0, i)),
        out_shape=jax.ShapeDtypeStruct((H, S), F32),
        scratch_shapes=[pltpu.VMEM((H, LANES), F32)],
        compiler_params=_cparams("arbitrary"),
    )(z_t, bias_col)


def _fox_post(drow, dcol, z_t, bias_col, *, name):
    H, S = z_t.shape
    tb = _pick(S, 512)
    nb = S // tb

    def body(dr_ref, d_ref, z_ref, b_ref, dz_ref, db_ref, carry_ref):
        @pl.when(pl.program_id(0) == 0)
        def _():
            carry_ref[...] = jnp.zeros_like(carry_ref)
            db_ref[...] = jnp.zeros_like(db_ref)

        dc = dr_ref[...] - d_ref[...]
        r = lax.broadcasted_iota(jnp.int32, (tb, tb), 0)
        c = lax.broadcasted_iota(jnp.int32, (tb, tb), 1)
        tri = (r >= c).astype(BF16)
        hi = dc.astype(BF16)
        r1 = dc - hi.astype(F32)
        mid = r1.astype(BF16)
        lo = (r1 - mid.astype(F32)).astype(BF16)
        dlogf = _dot_nn(hi, tri) + _dot_nn(mid, tri) + _dot_nn(lo, tri) + carry_ref[:, 0:1]
        carry_ref[...] = jnp.zeros_like(carry_ref) + dlogf[:, 0:1]
        dz = dlogf * _sigmoid(-(z_ref[...] + b_ref[...]))
        dz_ref[...] = dz
        db_ref[...] += jnp.sum(dz, axis=1, keepdims=True)

    rev = lambda i: (0, nb - 1 - i)
    return pl.pallas_call(
        body, name=name, grid=(nb,),
        in_specs=[pl.BlockSpec((H, tb), rev), pl.BlockSpec((H, tb), rev), pl.BlockSpec((H, tb), rev),
                  pl.BlockSpec((H, 1), lambda i: (0, 0))],
        out_specs=[pl.BlockSpec((H, tb), rev), pl.BlockSpec((H, LANES), lambda i: (0, 0))],
        out_shape=[jax.ShapeDtypeStruct((H, S), F32), jax.ShapeDtypeStruct((H, LANES), F32)],
        scratch_shapes=[pltpu.VMEM((H, LANES), F32)],
        compiler_params=_cparams("arbitrary"),
    )(drow, dcol, z_t, bias_col)


def _fox_blocks(S):
    cap = max(LANES, S // 4)
    return (min(FOX_FWD_BLOCKS[0], cap), min(FOX_FWD_BLOCKS[1], cap)), \
           (min(FOX_BWD_BLOCKS[0], cap), min(FOX_BWD_BLOCKS[1], cap))


def _key_bias_blocks(negc, bk):
    H, S = negc.shape
    return negc.reshape(H // 2, 2, S // bk, bk).transpose(0, 2, 1, 3)


def _fox_fwd(proj, negc4, *, q_off, k_off, v_off, bq, bk, name):
    S = proj.shape[0]
    nq, nk = S // bq, S // bk
    npair = FOX_HEADS // 2
    assert bq % bk == 0 or bk % bq == 0
    nmask = max(1, bq // bk)

    gp = FOX_FWD_PAIRS
    gw = gp * LANES
    assert q_off % gw == 0 and k_off % gw == 0 and v_off % gw == 0 and npair % gp == 0

    def body(q_ref, k_ref, v_ref, nc_ref, o_ref, lse_ref):
        i = pl.program_id(1)
        lane = lax.broadcasted_iota(jnp.int32, (1, LANES), 1)
        half = [lane < HEAD_DIM, lane >= HEAD_DIM]
        qh = []
        for g in range(gp):
            q2 = q_ref[:, g * LANES:(g + 1) * LANES] * jnp.asarray(ATT_SCALE, BF16)
            qh += [jnp.where(half[h], q2, jnp.zeros_like(q2)) for h in range(2)]
        row = lax.broadcasted_iota(jnp.int32, (bq, bk), 0)
        col = lax.broadcasted_iota(jnp.int32, (bq, bk), 1)
        rel = row - col
        nfull = (i * bq) // bk

        spare = [HEAD_DIM, 0]
        ones_lane = [lane == spare[h] for h in range(2)]

        def step(j, carry, masked):
            start = pl.multiple_of(j * bk, bk)
            new = []
            for g in range(gp):
                ks = k_ref[pl.ds(start, bk), g * LANES:(g + 1) * LANES]
                vs = v_ref[pl.ds(start, bk), g * LANES:(g + 1) * LANES]
                nb = nc_ref[g, j]
                for h in range(2):
                    m, acc = carry[4 * g + 2 * h:4 * g + 2 * h + 2]
                    vh = jnp.where(half[h], vs, jnp.where(ones_lane[h], jnp.ones_like(vs), jnp.zeros_like(vs)))
                    qs, bias = qh[2 * g + h], nb[h:h + 1, :]

                    def update(m, acc, rows, keys):
                        s = _dot_nt(qs[rows], ks[keys]) + bias[:, keys]
                        if masked:
                            s = jnp.where(rel[rows, keys] >= j * bk - i * bq, s, NEG)
                        m_new = jnp.maximum(m[rows], jnp.max(s, axis=1, keepdims=True))
                        p = jnp.exp(s - m_new).astype(BF16)
                        return m_new, jnp.exp(m[rows] - m_new) * acc[rows] + _dot_nn(p, vh[keys])

                    if masked and bq == bk:
                        top, bot, everything = slice(0, bq // 2), slice(bq // 2, bq), slice(0, bk)
                        m_t, acc_t = update(m, acc, top, top)
                        m_b, acc_b = update(m, acc, bot, everything)
                        new += [jnp.concatenate([m_t, m_b], axis=0), jnp.concatenate([acc_t, acc_b], axis=0)]
                    else:
                        new += list(update(m, acc, slice(0, bq), slice(0, bk)))
            return tuple(new)

        init = (jnp.full((bq, 1), NEG, F32), jnp.zeros((bq, LANES), F32)) * (2 * gp)
        carry = lax.fori_loop(0, nfull, lambda j, c: step(j, c, False), init)
        for t in range(nmask):
            carry = step(nfull + t, carry, True)
        for g in range(gp):
            outs, lses = [], []
            for h in range(2):
                m, acc = carry[4 * g + 2 * h:4 * g + 2 * h + 2]
                l = acc[:, spare[h]:spare[h] + 1]
                outs.append(acc * (1.0 / l))
                lses.append(m + jnp.log(l))
            o_ref[:, g * LANES:(g + 1) * LANES] = jnp.where(half[0], outs[0], outs[1]).astype(BF16)
            lse_ref[g] = jnp.where(half[0], lses[0], lses[1])

    seq = lambda off: pl.BlockSpec((S, gw), lambda hp, i: (0, off // gw + hp))
    return pl.pallas_call(
        body, name=name, grid=(npair // gp, nq),
        in_specs=[pl.BlockSpec((bq, gw), lambda hp, i: (i, q_off // gw + hp)), seq(k_off), seq(v_off),
                  pl.BlockSpec((gp, nk, 2, bk), lambda hp, i: (hp, 0, 0, 0))],
        out_specs=[pl.BlockSpec((bq, gw), lambda hp, i: (i, hp)),
                   pl.BlockSpec((gp, bq, LANES), lambda hp, i: (hp, i, 0))],
        out_shape=[jax.ShapeDtypeStruct((S, FOX_W), BF16), jax.ShapeDtypeStruct((npair, S, LANES), F32)],
        compiler_params=_cparams("parallel", "parallel"),
    )(proj, proj, proj, negc4)


def _fox_bwd(proj, negc4, o, lse, d_o, q_t, do_t, *, q_off, k_off, v_off, bq, bk, name, deps=()):
    S = proj.shape[0]
    nq, nk = S // bq, S // bk
    npair = FOX_HEADS // 2
    assert bq % bk == 0 or bk % bq == 0
    nmask = max(1, bk // bq)

    def body(q_ref, k_ref, v_ref, nc_ref, o_ref, lse_ref, do_ref, qt_ref, dot_ref, *rest):
        dqo_ref, dk_ref, dv_ref, dn_ref, dr_ref, delta_ref, rs_ref, dq_ref = rest[len(deps):]
        j = pl.program_id(1)
        lane = lax.broadcasted_iota(jnp.int32, (1, LANES), 1)
        half = [lane < HEAD_DIM, lane >= HEAD_DIM]
        spare = [HEAD_DIM, 0]
        ones_lane = [lane == spare[h] for h in range(2)]
        srow = lax.broadcasted_iota(jnp.int32, (LANES, 1), 0)
        rhalf = [srow < HEAD_DIM, srow >= HEAD_DIM]
        ones_row = [srow == spare[h] for h in range(2)]
        k2, v2 = k_ref[...], v_ref[...]
        one_k = jnp.ones_like(k2)
        kh = [jnp.where(half[h], k2, jnp.where(ones_lane[h], one_k, jnp.zeros_like(k2))) for h in range(2)]
        nb = nc_ref[0, 0]
        row = lax.broadcasted_iota(jnp.int32, (bq, bk), 0)
        col = lax.broadcasted_iota(jnp.int32, (bq, bk), 1)
        rel = row - col
        i_first = (j * bk) // bq

        @pl.when(j == 0)
        def _():
            dq_ref[...] = jnp.zeros_like(dq_ref)
            rs_ref[...] = jnp.zeros_like(rs_ref)
            for b in range(nq):
                prod = do_ref[b * bq:(b + 1) * bq, :].astype(F32) * o_ref[b * bq:(b + 1) * bq, :].astype(F32)
                d0 = jnp.sum(jnp.where(half[0], prod, 0.0), axis=1, keepdims=True)
                d1 = jnp.sum(jnp.where(half[1], prod, 0.0), axis=1, keepdims=True)
                delta_ref[b * bq:(b + 1) * bq, :] = jnp.where(half[0], d0, d1)

        def step(i, carry, masked, r0=0):
            dkt_a, dkt_b, dvt = carry
            dkts = [dkt_a, dkt_b]
            nr = bq - r0
            start = pl.multiple_of(i * bq + r0, LANES)
            q2 = q_ref[pl.ds(start, nr), :] * jnp.asarray(ATT_SCALE, BF16)
            do2 = do_ref[pl.ds(start, nr), :]
            qt = qt_ref[i][:, r0:] * jnp.asarray(ATT_SCALE, BF16)
            dot = dot_ref[i][:, r0:]
            lse2 = lse_ref[0, pl.ds(start, nr), :]
            del2 = delta_ref[pl.ds(start, nr), :]
            dqf = []
            for h in range(2):
                qm = jnp.where(half[h], q2, jnp.zeros_like(q2))
                dom = jnp.where(half[h], do2, jnp.zeros_like(do2))
                qtm = jnp.where(rhalf[h], qt, jnp.where(ones_row[h], jnp.ones_like(qt), jnp.zeros_like(qt)))
                dotm = jnp.where(rhalf[h], dot, jnp.zeros_like(dot))
                c0 = h * HEAD_DIM
                p = jnp.exp(_dot_nt(qm, k2) + nb[h:h + 1, :] - lse2[:, c0:c0 + 1])
                if masked:
                    p = jnp.where(rel[r0:] >= j * bk - i * bq, p, 0.0)
                dp = _dot_nt(dom, v2)
                dsb = (p * (dp - del2[:, c0:c0 + 1])).astype(BF16)
                dvt = dvt + _dot_nn(dotm, p.astype(BF16))
                dkts[h] = dkts[h] + _dot_nn(qtm, dsb)
                dqf.append(_dot_nn(dsb, kh[h]))
            dq_ref[pl.ds(start, nr), :] += jnp.where(half[0], dqf[0], dqf[1]) * ATT_SCALE
            rs_ref[pl.ds(start, nr), :] += jnp.where(ones_lane[0], dqf[0], jnp.where(ones_lane[1], dqf[1], 0.0))
            return dkts[0], dkts[1], dvt

        zero = jnp.zeros((LANES, bk), F32)
        carry = (zero, zero, zero)
        if bq > bk:
            sp = j % (bq // bk)
            carry = lax.switch(sp, [functools.partial(step, i_first, masked=True, r0=s * bk)
                                    for s in range(bq // bk)], carry)
        else:
            for t in range(nmask):
                carry = step(i_first + t, carry, True)
        dkt_a, dkt_b, dvt = lax.fori_loop(i_first + nmask, nq, lambda i, c: step(i, c, False), carry)
        dk_ref[...] = jnp.where(rhalf[0], dkt_a, dkt_b).T.astype(BF16)
        dv_ref[...] = dvt.T.astype(BF16)
        dn_ref[0, 0] = jnp.concatenate([dkt_a[spare[0]:spare[0] + 1], dkt_b[spare[1]:spare[1] + 1]], axis=0)

        @pl.when(j == nk - 1)
        def _():
            dqo_ref[...] = dq_ref[...].astype(BF16)
            for b in range(nq):
                t = rs_ref[b * bq:(b + 1) * bq, :].T
                dr_ref[0, b] = jnp.concatenate([t[spare[0]:spare[0] + 1], t[spare[1]:spare[1] + 1]], axis=0)

    once = pl.Buffered(1)
    seq = lambda off: pl.BlockSpec((S, LANES), lambda hp, j: (0, off // LANES + hp), pipeline_mode=once)
    blk = lambda off: pl.BlockSpec((bk, LANES), lambda hp, j: (j, off // LANES + hp))
    nc = pl.BlockSpec((1, 1, 2, bk), lambda hp, j: (hp, j, 0, 0))
    tsp = pl.BlockSpec((nq, LANES, bq), lambda hp, j: (0, hp, 0), pipeline_mode=once)
    return pl.pallas_call(
        body, name=name, grid=(npair, nk),
        in_specs=[seq(q_off), blk(k_off), blk(v_off), nc, seq(0),
                  pl.BlockSpec((1, S, LANES), lambda hp, j: (hp, 0, 0), pipeline_mode=once), seq(0),
                  tsp, tsp] + [_ANY] * len(deps),
        out_specs=[pl.BlockSpec((S, LANES), lambda hp, j: (0, hp)), blk(0), blk(0), nc,
                   pl.BlockSpec((1, nq, 2, bq), lambda hp, j: (hp, 0, 0, 0))],
        out_shape=[jax.ShapeDtypeStruct((S, FOX_W), BF16), jax.ShapeDtypeStruct((S, FOX_W), BF16),
                   jax.ShapeDtypeStruct((S, FOX_W), BF16), jax.ShapeDtypeStruct((npair, nk, 2, bk), F32),
                   jax.ShapeDtypeStruct((npair, nq, 2, bq), F32)],
        scratch_shapes=[pltpu.VMEM((S, LANES), F32), pltpu.VMEM((S, LANES), F32), pltpu.VMEM((S, LANES), F32)],
        compiler_params=_cparams("parallel", "arbitrary"),
    )(proj, proj, proj, negc4, o, lse, d_o, q_t, do_t, *deps)


def _exchange(arrs, *, gather, name):
    n = len(arrs)
    npeer = N_DEV - 1

    def body(*refs):
        ins, outs = refs[:n], refs[n:2 * n]
        send_sems, recv_sems, loc_sems = refs[2 * n:]
        x, y, c = lax.axis_index("x"), lax.axis_index("y"), lax.axis_index("c")
        me = 4 * x + 2 * y + c
        peers = []
        for k in range(1, N_DEV):
            px = 1 - x if k & 4 else x
            py = 1 - y if k & 2 else y
            pc = 1 - c if k & 1 else c
            peers.append(((px, py, pc), 4 * px + 2 * py + pc))

        def remote(w, k):
            dev, idx = peers[k]
            src = ins[w] if gather else ins[w].at[idx]
            return pltpu.make_async_remote_copy(
                src_ref=src, dst_ref=outs[w].at[me],
                send_sem=send_sems.at[w * npeer + k], recv_sem=recv_sems.at[w * npeer + k],
                device_id=dev, device_id_type=pl.DeviceIdType.MESH)

        def arrival(w, k):
            dev, idx = peers[k]
            src = ins[w] if gather else ins[w].at[idx]
            return pltpu.make_async_remote_copy(
                src_ref=src, dst_ref=outs[w].at[idx],
                send_sem=send_sems.at[w * npeer + k], recv_sem=recv_sems.at[w * npeer + k],
                device_id=dev, device_id_type=pl.DeviceIdType.MESH)

        local = []
        for w in range(n):
            for k in range(npeer):
                remote(w, k).start()
            cp = pltpu.make_async_copy(ins[w] if gather else ins[w].at[me], outs[w].at[me], loc_sems.at[w])
            cp.start()
            local.append(cp)
        for w in range(n):
            for k in range(npeer):
                arrival(w, k).wait_recv()
        for w in range(n):
            for k in range(npeer):
                remote(w, k).wait_send()
            local[w].wait()

    hbm = pl.BlockSpec(memory_space=pl.ANY)
    out_shape = [jax.ShapeDtypeStruct((N_DEV,) + (a.shape if gather else a.shape[1:]), a.dtype) for a in arrs]
    return pl.pallas_call(
        body, name=name,
        in_specs=[hbm] * n, out_specs=[hbm] * n, out_shape=out_shape,
        scratch_shapes=[pltpu.SemaphoreType.DMA((n * npeer,)), pltpu.SemaphoreType.DMA((n * npeer,)),
                        pltpu.SemaphoreType.DMA((n,))],
        compiler_params=pltpu.CompilerParams(has_side_effects=True),
    )(*arrs)


def _gather_two_level(shard, *, name):
    def body(x_ref, out_ref, send_sems, recv_sems, local_sem):
        x, y, c = lax.axis_index("x"), lax.axis_index("y"), lax.axis_index("c")
        me, sibling = (x, y, c), (x, y, 1 - c)
        chips = [(1 - x, y), (x, 1 - y), (1 - x, 1 - y)]

        def slot(px, py, pc):
            return out_ref.at[4 * px + 2 * py + pc]

        def copy(k, block, to, src=None):
            return pltpu.make_async_remote_copy(
                src_ref=slot(*block) if src is None else src, dst_ref=slot(*block),
                send_sem=send_sems.at[k], recv_sem=recv_sems.at[k],
                device_id=to, device_id_type=pl.DeviceIdType.MESH)

        mine = pltpu.make_async_copy(x_ref, slot(*me), local_sem)
        mine.start()
        first = [copy(0, me, sibling, src=x_ref)]
        first += [copy(1 + j, me, (*chip, c), src=x_ref) for j, chip in enumerate(chips)]
        for cp in first:
            cp.start()
        passed = [copy(4 + j, (*chip, c), sibling) for j, chip in enumerate(chips)]
        for j, chip in enumerate(chips):
            copy(1 + j, (*chip, c), me).wait_recv()
            passed[j].start()
        copy(0, sibling, me).wait_recv()
        for j, chip in enumerate(chips):
            copy(4 + j, (*chip, 1 - c), me).wait_recv()
        for cp in first + passed:
            cp.wait_send()
        mine.wait()

    return pl.pallas_call(
        body, name=name,
        in_specs=[_ANY], out_specs=_ANY,
        out_shape=jax.ShapeDtypeStruct((N_DEV,) + shard.shape, shard.dtype),
        scratch_shapes=[pltpu.SemaphoreType.DMA((N_DEV - 1,)), pltpu.SemaphoreType.DMA((N_DEV - 1,)),
                        pltpu.SemaphoreType.DMA],
        compiler_params=pltpu.CompilerParams(has_side_effects=True),
    )(shard)


_HBM = pl.BlockSpec(memory_space=pltpu.HBM)
_SEM = pl.BlockSpec(memory_space=pltpu.SEMAPHORE)
_EFFECT = pltpu.SideEffectType.DATAFLOW_SIDE_EFFECTING
NPEER = N_DEV - 1


def _peer_table():
    x, y, c = lax.axis_index("x"), lax.axis_index("y"), lax.axis_index("c")
    peers = []
    for k in range(1, N_DEV):
        px = 1 - x if k & 4 else x
        py = 1 - y if k & 2 else y
        pc = 1 - c if k & 1 else c
        peers.append(((px, py, pc), 4 * px + 2 * py + pc))
    return 4 * x + 2 * y + c, peers


def _split_copy(ins, lands, send_sems, recv_sems, gather, me, peers, w, k, arriving):
    dev, idx = peers[k]
    return pltpu.make_async_remote_copy(
        src_ref=ins[w] if gather else ins[w].at[idx],
        dst_ref=lands[w].at[idx if arriving else me],
        send_sem=send_sems.at[w * NPEER + k], recv_sem=recv_sems.at[w * NPEER + k],
        device_id=dev, device_id_type=pl.DeviceIdType.MESH)


def _exchange_start(arrs, *, gather, name, deps=()):
    n = len(arrs)
    land_shapes = [(N_DEV,) + (a.shape if gather else a.shape[1:]) for a in arrs]

    def body(*refs):
        ins, lands = refs[:n], refs[n:2 * n]
        send_sems, recv_sems = refs[2 * n + len(deps)], refs[2 * n + len(deps) + 1]
        token = refs[-1]
        me, peers = _peer_table()
        for w in range(n):
            for k in range(NPEER):
                _split_copy(ins, lands, send_sems, recv_sems, gather, me, peers, w, k, False).start()
        token[...] = jnp.zeros_like(token)

    out_shape = ([pltpu.SemaphoreType.DMA((n * NPEER,)), pltpu.SemaphoreType.DMA((n * NPEER,))]
                 + [pltpu.HBM(a.shape, a.dtype) for a in arrs]
                 + [pltpu.HBM(s, a.dtype) for s, a in zip(land_shapes, arrs)]
                 + [jax.ShapeDtypeStruct((8, LANES), F32)])
    res = pl.pallas_call(
        body, name=name,
        in_specs=[_HBM] * (2 * n) + [_ANY] * len(deps),
        out_specs=[_SEM, _SEM] + [_HBM] * (2 * n) + [pl.BlockSpec(memory_space=pltpu.VMEM)],
        out_shape=out_shape,
        input_output_aliases={i: 2 + i for i in range(2 * n)},
        compiler_params=pltpu.CompilerParams(has_side_effects=_EFFECT),
    )(*[pltpu.with_memory_space_constraint(a, pltpu.HBM) for a in arrs],
      *[pltpu.with_memory_space_constraint(lax.empty(s, a.dtype), pltpu.HBM) for s, a in zip(land_shapes, arrs)],
      *deps)
    return (n, gather, res[0], res[1], res[2:2 + n], res[2 + n:2 + 2 * n]), res[-1]


def _exchange_wait(handle, after, *, name):
    n, gather, send_sems, recv_sems, ins_thru, lands_thru = handle

    def body(*refs):
        ins, lands = refs[:n], refs[n:2 * n]
        send_s, recv_s = refs[2 * n], refs[2 * n + 1]
        me, peers = _peer_table()
        for w in range(n):
            for k in range(NPEER):
                _split_copy(ins, lands, send_s, recv_s, gather, me, peers, w, k, False).wait_send()
                _split_copy(ins, lands, send_s, recv_s, gather, me, peers, w, k, True).wait_recv()

    res = pl.pallas_call(
        body, name=name,
        in_specs=[_HBM] * (2 * n) + [_SEM, _SEM, pl.BlockSpec(memory_space=pl.ANY)],
        out_specs=[_HBM] * (2 * n),
        out_shape=[pltpu.HBM(a.shape, a.dtype) for a in list(ins_thru) + list(lands_thru)],
        input_output_aliases={i: i for i in range(2 * n)},
        compiler_params=pltpu.CompilerParams(has_side_effects=_EFFECT),
    )(*ins_thru, *lands_thru, send_sems, recv_sems, after)
    return res[:n], res[n:2 * n]


def _ordered_sum(s_ref, own_ref):
    if own_ref is None:
        blocks = [s_ref[q].astype(F32) for q in range(N_DEV)]
    else:
        me = 4 * lax.axis_index("x") + 2 * lax.axis_index("y") + lax.axis_index("c")
        own = own_ref[...]
        blocks = [jnp.where(me == q, own, s_ref[q]).astype(F32) for q in range(N_DEV)]
    acc = blocks[0]
    for b in blocks[1:]:
        acc = acc + b
    return acc


def _sum8(stack, own, *, name):
    _, R, C = stack.shape
    if R % 8 == 0:
        tr, tc = _pick(R, max(8, STEP_BYTES // (C * 4 * (N_DEV + 2))), 8), C
    else:
        tr, tc = R, _pick(C, max(LANES, STEP_BYTES // (R * 4 * (N_DEV + 2))))

    def body(s_ref, own_ref, o_ref):
        o_ref[...] = _ordered_sum(s_ref, own_ref)

    blk = pl.BlockSpec((tr, tc), lambda i, j: (i, j))
    return pl.pallas_call(
        body, name=name, grid=(R // tr, C // tc),
        in_specs=[pl.BlockSpec((N_DEV, tr, tc), lambda i, j: (0, i, j)), blk],
        out_specs=blk,
        out_shape=jax.ShapeDtypeStruct((R, C), F32),
        compiler_params=_cparams("parallel", "parallel"),
    )(stack, own)


def _adamw_math(w, g, m, v):
    m = ADAM_B1 * m + (1.0 - ADAM_B1) * g
    v = ADAM_B2 * v + (1.0 - ADAM_B2) * (g * g)
    m_hat = m / (1.0 - ADAM_B1 ** ADAM_STEP)
    v_hat = v / (1.0 - ADAM_B2 ** ADAM_STEP)
    delta = -ADAM_LR * (m_hat / (jnp.sqrt(v_hat) + ADAM_EPS) + ADAM_WD * w)
    return delta, m, v


def _adamw(w, g, m, v, *, name, stacked, own=None, transposed=False):
    R, C = w.shape
    if transposed:
        tr = _pick(R, max(LANES, STEP_BYTES // (C * 4 * (9 + N_DEV))))
    else:
        tr = _pick(R, max(8, STEP_BYTES // (C * 4 * (8 + (N_DEV if stacked else 1)))), 8)
    has_own = own is not None

    def body(w_ref, g_ref, m_ref, v_ref, *rest):
        go_ref, d_ref, mo_ref, vo_ref = rest[-4:]
        g = _ordered_sum(g_ref, rest[0] if has_own else None) if stacked else g_ref[...]
        if transposed:
            g = g.T
        delta, m2, v2 = _adamw_math(w_ref[...], g, m_ref[...], v_ref[...])
        go_ref[...] = g
        d_ref[...] = delta
        mo_ref[...] = m2
        vo_ref[...] = v2

    row = pl.BlockSpec((tr, C), lambda i: (i, 0))
    if transposed:
        g_spec, own_spec = pl.BlockSpec((N_DEV, C, tr), lambda i: (0, 0, i)), pl.BlockSpec((C, tr), lambda i: (0, i))
    else:
        g_spec, own_spec = (pl.BlockSpec((N_DEV, tr, C), lambda i: (0, i, 0)) if stacked else row), row
    return pl.pallas_call(
        body, name=name, grid=(R // tr,),
        in_specs=[row, g_spec, row, row] + [own_spec] * has_own, out_specs=[row] * 4,
        out_shape=[jax.ShapeDtypeStruct((R, C), F32)] * 4,
        compiler_params=_cparams("parallel"),
    )(w, g, m, v, *([own] if has_own else []))


def kernel(x, positions, attn_norm, w_in, fox_f_bias, swa_sinks, w_branch_swa, w_branch_fox, w_out, mlp_norm, w_up, w_down, final_norm, loss_target, m_attn_norm, m_w_in, m_fox_f_bias, m_swa_sinks, m_w_branch_swa, m_w_branch_fox, m_w_out, m_mlp_norm, m_w_up, m_w_down, m_final_norm, v_attn_norm, v_w_in, v_fox_f_bias, v_swa_sinks, v_w_branch_swa, v_w_branch_fox, v_w_out, v_mlp_norm, v_w_up, v_w_down, v_final_norm):
    S, D = x.shape[1], x.shape[2]
    DFF = w_up.shape[2] * N_DEV
    d_in = w_in.shape[2] * N_DEV
    assert d_in == QKV_W + FOX_HEADS + 2 * D and (2 * D) % SWA_Q_W == 0 and S % (4 * LANES) == 0
    q_off = 2 * D
    k_off = q_off + SWA_Q_W
    v_off = k_off + SWA_KV_W
    fq_off = v_off + SWA_KV_W
    fk_off = fq_off + FOX_W
    fv_off = fk_off + FOX_W
    fl_off = fv_off + FOX_W
    NP = fl_off + FL_PAD
    x2d, tgt = x[0], loss_target[0]

    shards = [w_in[0].T.astype(BF16), w_branch_swa[0].T.astype(BF16), w_branch_fox[0].T.astype(BF16),
              w_out[0].astype(BF16), w_up[0].T.astype(BF16), w_down[0].astype(BF16)]
    me = 4 * lax.axis_index("x") + 2 * lax.axis_index("y") + lax.axis_index("c")

    def filled(stack, own):
        return lax.dynamic_update_slice(stack, own[None], (me,) + (0,) * own.ndim)

    g_in = _gather_two_level(shards[0], name="gather_w_in")
    h_rest, tok_rest = _exchange_start(shards[1:], gather=True, name="gather_rest_start", deps=[g_in])

    tm = _pick(S, 1024)
    td = _pick(D, 1024)
    tf = _pick(DFF, 1024)
    tnp = _pick(NP, 1024)

    h1 = _rms_fwd(x2d, attn_norm, name="rms1", deps=[tok_rest])
    w_in_t = g_in.reshape(d_in, D)
    w_in_p = jnp.concatenate([w_in_t[QKV_W + FOX_HEADS:], w_in_t[:QKV_W], w_in_t[QKV_W:QKV_W + FOX_HEADS],
                              jnp.zeros((FL_PAD - FOX_HEADS, D), BF16)], axis=0)
    w_fl_t = w_in_t[QKV_W:QKV_W + FOX_HEADS]
    proj, = _matmul(h1, w_in_p, mode="nt", name="mm_in", out_dtypes=[BF16], tm=_pick(S, 2048), tn=tnp, tk=D)
    z_t, = _matmul(w_fl_t, h1, mode="nt", name="mm_flogit", out_dtypes=[F32],
                   tm=FOX_HEADS, tn=_pick(S, 2048), tk=D)
    bias_col = fox_f_bias.reshape(FOX_HEADS, 1)
    negc = _fox_prep(z_t, bias_col, name="fox_prep")
    (fbq, fbk), (bbq, bbk) = _fox_blocks(S)
    inv_freq = ROPE_THETA ** (-jnp.arange(0, HEAD_DIM, 2, dtype=F32) / HEAD_DIM)
    invf = jnp.tile(inv_freq, LANES // (HEAD_DIM // 2)).reshape(1, LANES)
    cos_t, sin_t = _rope_tables(positions.reshape(S, 1), invf, name="rope_tables")
    q_rope, k_rope = _rope_fwd(proj, cos_t, sin_t, q_off=q_off, k_off=k_off, name="rope_fwd")
    sinks = swa_sinks.reshape(-1)
    o_a = _swa_fwd(q_rope, k_rope, proj, sinks, v_off=v_off, name="swa_fwd")
    o_b, lse = _fox_fwd(proj, _key_bias_blocks(negc, fbk), q_off=fq_off, k_off=fk_off, v_off=fv_off,
                        bq=fbq, bk=fbk, name="fox_fwd")
    s_rest, g_rest = _exchange_wait(h_rest, o_b, name="gather_rest_wait")
    g_bs, g_bf, g_o, g_up, g_dn = [filled(g, s) for g, s in zip(g_rest, s_rest)]
    w_bs_t = g_bs.reshape(D, SWA_Q_W)
    w_bf_t = g_bf.reshape(D, FOX_W)
    w_o = g_o.reshape(D, D)
    w_up_t = g_up.reshape(DFF, D)
    w_dn = g_dn.reshape(DFF, D)
    ya, = _matmul(o_a, w_bs_t, mode="nt", name="mm_branch_swa", out_dtypes=[BF16], tm=tm, tn=td, tk=SWA_Q_W)
    gate_maps = [lambda i, j, k: (i, j), lambda i, j, k: (i, j), lambda i, j, k: (i, j + D // td)]

    def merge_epi(acc, ya_t, ga_t, gb_t):
        merged = _sigmoid(ga_t.astype(F32)) * ya_t.astype(F32) + _sigmoid(gb_t.astype(F32)) * acc
        return acc, merged

    yb, merged = _matmul(o_b, w_bf_t, mode="nt", name="mm_branch_fox", out_dtypes=[BF16, BF16],
                         tm=tm, tn=td, tk=FOX_W, extras=[ya, proj, proj], extra_maps=gate_maps,
                         epilogue=merge_epi)
    x_mid, = _matmul(merged, w_o, mode="nn", name="mm_out", out_dtypes=[F32], tm=tm, tn=td, tk=D,
                     extras=[x2d], epilogue=lambda acc, r: (acc + r,))
    h2 = _rms_fwd(x_mid, mlp_norm, name="rms2")
    u, = _matmul(h2, w_up_t, mode="nt", name="mm_up", out_dtypes=[BF16], tm=_pick(S, 2048), tn=tf, tk=D,
                 epilogue=lambda acc: (jnp.maximum(acc, 0.0),))
    x_fin, = _matmul(u, w_dn, mode="nn", name="mm_down", out_dtypes=[F32], tm=tm, tn=td, tk=_pick(DFF, 2048),
                     a_fn=_square_bf16, extras=[x_mid], epilogue=lambda acc, r: (acc + r,))

    dx3b, dg3, loss_part = _loss_head(x_fin, tgt, final_norm.reshape(1, D), name="loss_head")
    d_up, = _matmul(dx3b, w_dn, mode="nt", name="mm_d_act", out_dtypes=[BF16], tm=_pick(S, 2048), tn=tf, tk=D,
                    extras=[u], epilogue=lambda acc, ut: (acc * (2.0 * ut.astype(F32)),))
    tks = _pick(S, 2048)
    dw_dn, = _matmul(u, dx3b, mode="tn", name="mm_dw_down", out_dtypes=[BF16], tm=tf, tn=td, tk=tks,
                     a_fn=_square_bf16)
    dh2, = _matmul(d_up, w_up_t, mode="nn", name="mm_dh2", out_dtypes=[BF16], tm=tm, tn=td, tk=_pick(DFF, 2048))
    dw_up_t, = _matmul(d_up, h2, mode="tn", name="mm_dw_up", out_dtypes=[BF16], tm=tf, tn=td, tk=tks)
    h_s1, tok_s1 = _exchange_start([dw_up_t.reshape(N_DEV, DFF // N_DEV, D), dw_dn.reshape(N_DEV, DFF // N_DEV, D)],
                                   gather=False, name="scatter_mlp_start")
    dx2b, dg2 = _rms_bwd(dh2, x_mid, mlp_norm, dx3b, name="rms2_bwd", out_dtype=BF16, deps=[tok_s1])

    def gate_bwd_epi(dm, ya_t, yb_t, ga_t, gb_t):
        sa, sb = _sigmoid(ga_t.astype(F32)), _sigmoid(gb_t.astype(F32))
        return (dm * sa, dm * sb, dm * ya_t.astype(F32) * sa * (1.0 - sa), dm * yb_t.astype(F32) * sb * (1.0 - sb))

    gmaps = [lambda i, j, k: (i, j), lambda i, j, k: (i, j), lambda i, j, k: (i, j),
             lambda i, j, k: (i, j + D // td)]
    d_ya, d_yb, d_ga, d_gb = _matmul(dx2b, w_o, mode="nt", name="mm_d_merged", out_dtypes=[BF16] * 4,
                                     tm=tm, tn=td, tk=D, extras=[ya, yb, proj, proj], extra_maps=gmaps,
                                     epilogue=gate_bwd_epi)
    dw_o, = _matmul(merged, dx2b, mode="tn", name="mm_dw_out", out_dtypes=[BF16], tm=td, tn=td, tk=tks)
    d_oa, = _matmul(d_ya, w_bs_t, mode="nn", name="mm_d_oa", out_dtypes=[BF16], tm=tm, tn=SWA_Q_W, tk=D)
    d_ob, = _matmul(d_yb, w_bf_t, mode="nn", name="mm_d_ob", out_dtypes=[BF16], tm=tm, tn=FOX_W, tk=D)
    dw_bs_t, = _matmul(d_ya, o_a, mode="tn", name="mm_dw_bs", out_dtypes=[BF16], tm=td, tn=SWA_Q_W, tk=tks)
    dw_bf_t, = _matmul(d_yb, o_b, mode="tn", name="mm_dw_bf", out_dtypes=[BF16], tm=td, tn=FOX_W, tk=tks)
    h_s2, tok_s2 = _exchange_start([dw_bs_t.reshape(N_DEV, D // N_DEV, SWA_Q_W),
                                    dw_bf_t.reshape(N_DEV, D // N_DEV, FOX_W), dw_o.reshape(N_DEV, D // N_DEV, D)],
                                   gather=False, name="scatter_attn_start")
    def row_blocks_t(a):
        return a.reshape(S // bbq, bbq, FOX_W).transpose(0, 2, 1)

    d_fq, d_fk, d_fv, dcol4, drow4 = _fox_bwd(proj, _key_bias_blocks(negc, bbk), o_b, lse, d_ob,
                                              row_blocks_t(proj[:, fq_off:fq_off + FOX_W]), row_blocks_t(d_ob),
                                              q_off=fq_off, k_off=fk_off, v_off=fv_off, bq=bbq, bk=bbk,
                                              name="fox_bwd", deps=[tok_s2])
    dcol = dcol4.transpose(0, 2, 1, 3).reshape(FOX_HEADS, S)
    drow = drow4.transpose(0, 2, 1, 3).reshape(FOX_HEADS, S)
    dz_t, dbias_l = _fox_post(drow, dcol, z_t, bias_col, name="fox_post")
    dq_r, dk_c, dk_p, dv_c, dv_p, dsink_l = _swa_bwd(q_rope, k_rope, proj, sinks, d_oa, v_off=v_off, name="swa_bwd")
    d_aq, d_ak, d_av = _rope_bwd(dq_r, dk_c, dk_p, dv_c, dv_p, cos_t, sin_t, name="rope_bwd")
    dz_pad = jnp.pad(dz_t.T.astype(BF16), ((0, 0), (0, FL_PAD - FOX_HEADS)))
    d_proj = jnp.concatenate([d_ga, d_gb, d_aq, d_ak, d_av, d_fq, d_fk, d_fv, dz_pad], axis=1)
    tkp = _pick(NP, 2304)
    dw_in_p, = _matmul(d_proj, h1, mode="tn", name="mm_dw_in", out_dtypes=[BF16], tm=_pick(NP, 512), tn=D, tk=tks)
    dw_in_t = jnp.concatenate([dw_in_p[q_off:q_off + QKV_W], dw_in_p[fl_off:fl_off + FOX_HEADS], dw_in_p[:q_off]],
                              axis=0)
    h_s3, tok_s3 = _exchange_start([dw_in_t.reshape(N_DEV, d_in // N_DEV, D)], gather=False,
                                   name="scatter_in_start")
    dh1, = _matmul(d_proj, w_in_p, mode="nn", name="mm_dh1", out_dtypes=[BF16], tm=tm, tn=td, tk=tkp, deps=[tok_s3])
    dx, dg1 = _rms_bwd(dh1, x2d, attn_norm, dx2b, name="rms1_bwd", out_dtype=F32)

    dbias = dbias_l[:, 0]
    dsinks = dsink_l[:, :, 0].reshape(-1)
    nsm = 3 * D + 2 * LANES
    tail = jnp.zeros((2 * LANES,), F32)
    small_g = jnp.concatenate([dg1[0], dg2[0], dg3[0],
                               tail.at[0:16].set(dbias).at[16:32].set(dsinks).at[32].set(loss_part[0, 0])])

    def pack(a_norm, b_norm, f_norm, bias, snk):
        return jnp.concatenate([a_norm[0], b_norm[0], f_norm,
                                tail.at[0:16].set(bias[0]).at[16:32].set(snk[0])]).reshape(1, nsm)

    small_stack, = _exchange([small_g.reshape(1, nsm)], gather=True, name="gather_small")
    u_sm = _adamw(pack(attn_norm, mlp_norm, final_norm, fox_f_bias, swa_sinks), small_stack,
                  pack(m_attn_norm, m_mlp_norm, m_final_norm, m_fox_f_bias, m_swa_sinks),
                  pack(v_attn_norm, v_mlp_norm, v_final_norm, v_fox_f_bias, v_swa_sinks),
                  name="adamw_small", stacked=True)
    loss = u_sm[0][0, 3 * D + 32]

    def own_of(src):
        return lax.dynamic_index_in_dim(src, me, 0, keepdims=False)

    def update_t(stack, src, w, m, v, nm):
        g = _sum8(stack, own_of(src), name="sum_" + nm).T
        return _adamw(w[0], g, m[0], v[0], name="adamw_" + nm, stacked=False)

    def update(stack, src, w, m, v, nm, transposed=False):
        return _adamw(w[0], stack, m[0], v[0], name="adamw_" + nm, stacked=True, own=own_of(src),
                      transposed=transposed)

    (s_up, s_dn), (r_up, r_dn) = _exchange_wait(h_s1, u_sm[1], name="scatter_mlp_wait")
    u_up = update(r_up, s_up, w_up, m_w_up, v_w_up, "w_up", transposed=True)
    u_dn = update(r_dn, s_dn, w_down, m_w_down, v_w_down, "w_down")
    (s_bs, s_bf, s_o), (r_bs, r_bf, r_o) = _exchange_wait(h_s2, u_dn[1], name="scatter_attn_wait")
    u_bs = update(r_bs, s_bs, w_branch_swa, m_w_branch_swa, v_w_branch_swa, "w_bs", transposed=True)
    u_bf = update(r_bf, s_bf, w_branch_fox, m_w_branch_fox, v_w_branch_fox, "w_bf", transposed=True)
    u_o = update(r_o, s_o, w_out, m_w_out, v_w_out, "w_out")
    (s_w_in,), (r_in,) = _exchange_wait(h_s3, u_o[1], name="scatter_in_wait")
    u_in = update_t(r_in, s_w_in, w_in, m_w_in, v_w_in, "w_in")

    def small(kind):
        a = u_sm[kind][0]
        return dict(attn_norm=a[0:D][None], mlp_norm=a[D:2 * D][None], final_norm=a[2 * D:3 * D],
                    fox_f_bias=a[3 * D:3 * D + 16][None], swa_sinks=a[3 * D + 16:3 * D + 32][None])

    big = dict(w_in=u_in, w_branch_swa=u_bs, w_branch_fox=u_bf, w_out=u_o, w_up=u_up, w_down=u_dn)
    order = ["attn_norm", "w_in", "fox_f_bias", "swa_sinks", "w_branch_swa", "w_branch_fox", "w_out", "mlp_norm",
             "w_up", "w_down", "final_norm"]
    outs = [loss, dx[None]]
    for kind in range(4):
        sm = small(kind)
        for nm in order:
            outs.append(big[nm][kind][None] if nm in big else sm[nm])
    return tuple(outs)
```

```python
import functools

import jax
import jax.numpy as jnp
from jax import lax
from jax.experimental import pallas as pl
from jax.experimental.pallas import tpu as pltpu

F32 = jnp.float32
BF16 = jnp.bfloat16

N_DEV = 8
HEAD_DIM = 64
SWA_Q_W = 1024
SWA_KV_W = 128
SWA_GROUP = 8
WINDOW = 128
FOX_W = 1024
FOX_HEADS = 16
QKV_W = SWA_Q_W + 2 * SWA_KV_W + 3 * FOX_W
FL_PAD = 256
ROPE_THETA = 10000.0
RMS_EPS = 1e-6
ATT_SCALE = 0.125
NEG = -1e30

ADAM_LR = 0.001
ADAM_B1 = 0.9
ADAM_B2 = 0.999
ADAM_EPS = 1e-08
ADAM_WD = 0.01
ADAM_STEP = 10

FOX_FWD_BLOCKS = (1024, 1024)
FOX_BWD_BLOCKS = (1024, 512)
FOX_FWD_PAIRS = 2

LANES = 128
VMEM_LIMIT = 56 * 1024 * 1024
STEP_BYTES = 12 * 1024 * 1024


def _cparams(*sem):
    return pltpu.CompilerParams(dimension_semantics=sem, vmem_limit_bytes=VMEM_LIMIT)


def _pick(dim, pref, align=LANES):
    best = None
    t = align
    while t <= min(dim, pref):
        if dim % t == 0:
            best = t
        t += align
    return best if best is not None else dim


_DIMS = {"nn": ((1,), (0,)), "nt": ((1,), (1,)), "tn": ((0,), (0,))}


_ANY = pl.BlockSpec(memory_space=pl.ANY)


def _matmul(a, b, *, mode, name, out_dtypes, tm, tn, tk, extras=(), extra_maps=None,
            a_fn=None, epilogue=None, deps=()):
    if mode == "nn":
        (M, K), (K2, N) = a.shape, b.shape
    elif mode == "nt":
        (M, K), (N, K2) = a.shape, b.shape
    else:
        (K, M), (K2, N) = a.shape, b.shape
    assert K == K2, (name, a.shape, b.shape)
    assert M % tm == 0 and N % tn == 0 and K % tk == 0, (name, M, N, K, tm, tn, tk)
    nk = K // tk
    ne, no = len(extras), len(out_dtypes)
    dims = (_DIMS[mode], ((), ()))

    def body(*refs):
        a_ref, b_ref = refs[0], refs[1]
        ex_refs = refs[2:2 + ne]
        out_refs = refs[2 + ne + len(deps):2 + ne + len(deps) + no]

        def finish(acc):
            res = (acc,) if epilogue is None else epilogue(acc, *[e[...] for e in ex_refs])
            for o_ref, r in zip(out_refs, res):
                o_ref[...] = r.astype(o_ref.dtype)

        def product():
            av = a_ref[...]
            if a_fn is not None:
                av = a_fn(av)
            return lax.dot_general(av, b_ref[...], dims, preferred_element_type=F32)

        if nk == 1:
            finish(product())
        else:
            acc_ref = refs[-1]
            k = pl.program_id(2)

            @pl.when(k == 0)
            def _():
                acc_ref[...] = jnp.zeros_like(acc_ref)

            acc_ref[...] += product()

            @pl.when(k == nk - 1)
            def _():
                finish(acc_ref[...])

    if mode == "tn":
        a_spec = pl.BlockSpec((tk, tm), lambda i, j, k: (k, i))
    else:
        a_spec = pl.BlockSpec((tm, tk), lambda i, j, k: (i, k))
    if mode == "nt":
        b_spec = pl.BlockSpec((tn, tk), lambda i, j, k: (j, k))
    else:
        b_spec = pl.BlockSpec((tk, tn), lambda i, j, k: (k, j))
    if extra_maps is None:
        extra_maps = [lambda i, j, k: (i, j)] * ne
    ex_specs = [pl.BlockSpec((tm, tn), m) for m in extra_maps]
    out_spec = [pl.BlockSpec((tm, tn), lambda i, j, k: (i, j)) for _ in range(no)]
    res = pl.pallas_call(
        body,
        name=name,
        grid=(M // tm, N // tn, nk),
        in_specs=[a_spec, b_spec] + ex_specs + [_ANY] * len(deps),
        out_specs=out_spec,
        out_shape=[jax.ShapeDtypeStruct((M, N), d) for d in out_dtypes],
        scratch_shapes=[pltpu.VMEM((tm, tn), F32)] if nk > 1 else [],
        compiler_params=_cparams("parallel", "parallel", "arbitrary"),
    )(a, b, *extras, *deps)
    return res


def _square_bf16(t):
    tf = t.astype(F32)
    return (tf * tf).astype(BF16)


def _sigmoid(g):
    return 1.0 / (1.0 + jnp.exp(-g))


def _rms_fwd(x, gain, *, name, deps=()):
    S, D = x.shape
    tr = _pick(S, 512, 8)

    def body(x_ref, g_ref, *rest):
        h_ref = rest[-1]
        xv = x_ref[...]
        r = lax.rsqrt(jnp.mean(xv * xv, axis=-1, keepdims=True) + RMS_EPS)
        h_ref[...] = (xv * r * g_ref[...]).astype(BF16)

    return pl.pallas_call(
        body, name=name, grid=(S // tr,),
        in_specs=[pl.BlockSpec((tr, D), lambda i: (i, 0)), pl.BlockSpec((1, D), lambda i: (0, 0))] + [_ANY] * len(deps),
        out_specs=pl.BlockSpec((tr, D), lambda i: (i, 0)),
        out_shape=jax.ShapeDtypeStruct((S, D), BF16),
        compiler_params=_cparams("parallel"),
    )(x, gain, *deps)


def _rms_bwd(dh, x, gain, dres, *, name, out_dtype, deps=()):
    S, D = x.shape
    tr = _pick(S, 256, 8)

    def body(dh_ref, x_ref, g_ref, dres_ref, *rest):
        outs = rest[len(deps):]
        dx_ref, dg_ref = outs[0], outs[-1]
        xv = x_ref[...]
        r = lax.rsqrt(jnp.mean(xv * xv, axis=-1, keepdims=True) + RMS_EPS)
        xh = xv * r
        dhv = dh_ref[...].astype(F32)
        t = dhv * g_ref[...]
        dx = r * (t - xh * jnp.mean(t * xh, axis=-1, keepdims=True)) + dres_ref[...].astype(F32)
        dx_ref[...] = dx.astype(out_dtype)
        part = jnp.sum(dhv * xh, axis=0, keepdims=True)

        @pl.when(pl.program_id(0) == 0)
        def _():
            dg_ref[...] = part

        @pl.when(pl.program_id(0) > 0)
        def _():
            dg_ref[...] += part

    row = pl.BlockSpec((tr, D), lambda i: (i, 0))
    vec = pl.BlockSpec((1, D), lambda i: (0, 0))
    return pl.pallas_call(
        body, name=name, grid=(S // tr,),
        in_specs=[row, row, vec, row] + [_ANY] * len(deps), out_specs=[row, vec],
        out_shape=[jax.ShapeDtypeStruct((S, D), out_dtype), jax.ShapeDtypeStruct((1, D), F32)],
        compiler_params=_cparams("arbitrary"),
    )(dh, x, gain, dres, *deps)


def _loss_head(x3, target, gain, *, name):
    S, D = x3.shape
    tr = _pick(S, 256, 8)

    def body(x_ref, t_ref, g_ref, dxb_ref, dg_ref, loss_ref):
        xv = x_ref[...]
        r = lax.rsqrt(jnp.mean(xv * xv, axis=-1, keepdims=True) + RMS_EPS)
        xh = xv * r
        gv = g_ref[...]
        err = xh * gv - t_ref[...]
        lpart = jnp.zeros((1, LANES), F32) + (0.5 / D) * jnp.sum(err * err)
        dy = err * (1.0 / D)
        t = dy * gv
        dx = r * (t - xh * jnp.mean(t * xh, axis=-1, keepdims=True))
        dxb_ref[...] = dx.astype(BF16)
        part = jnp.sum(dy * xh, axis=0, keepdims=True)

        @pl.when(pl.program_id(0) == 0)
        def _():
            dg_ref[...] = part
            loss_ref[...] = lpart

        @pl.when(pl.program_id(0) > 0)
        def _():
            dg_ref[...] += part
            loss_ref[...] += lpart

    row = pl.BlockSpec((tr, D), lambda i: (i, 0))
    vec = pl.BlockSpec((1, D), lambda i: (0, 0))
    return pl.pallas_call(
        body, name=name, grid=(S // tr,),
        in_specs=[row, row, vec],
        out_specs=[row, vec, pl.BlockSpec((1, LANES), lambda i: (0, 0))],
        out_shape=[jax.ShapeDtypeStruct((S, D), BF16),
                   jax.ShapeDtypeStruct((1, D), F32), jax.ShapeDtypeStruct((1, LANES), F32)],
        compiler_params=_cparams("arbitrary"),
    )(x3, target, gain)


def _rope_tables(pos_col, invf, *, name):
    S = pos_col.shape[0]
    tr = _pick(S, 512, 8)

    def body(p_ref, f_ref, cos_ref, sin_ref):
        ang = p_ref[...].astype(F32) * f_ref[...]
        lane = lax.broadcasted_iota(jnp.int32, (1, LANES), 1)
        first = (lane % HEAD_DIM) < HEAD_DIM // 2
        sn = jnp.sin(ang)
        cos_ref[...] = jnp.cos(ang)
        sin_ref[...] = jnp.where(first, -sn, sn)

    return pl.pallas_call(
        body, name=name, grid=(S // tr,),
        in_specs=[pl.BlockSpec((tr, 1), lambda i: (i, 0)), pl.BlockSpec((1, LANES), lambda i: (0, 0))],
        out_specs=[pl.BlockSpec((tr, LANES), lambda i: (i, 0))] * 2,
        out_shape=[jax.ShapeDtypeStruct((S, LANES), F32)] * 2,
        compiler_params=_cparams("parallel"),
    )(pos_col, invf)


def _swap_halves(t):
    lane = lax.broadcasted_iota(jnp.int32, (1, LANES), 1)
    first = (lane % HEAD_DIM) < HEAD_DIM // 2
    return jnp.where(first, pltpu.roll(t, LANES - HEAD_DIM // 2, 1), pltpu.roll(t, HEAD_DIM // 2, 1))


def _rope_fwd(proj, cos_t, sin_t, *, q_off, k_off, name):
    S = proj.shape[0]
    tr = _pick(S, 256, 8)
    nqb = SWA_Q_W // LANES

    def body(q_ref, k_ref, c_ref, s_ref, qo_ref, ko_ref):
        cv, sv = c_ref[...], s_ref[...]
        for b in range(nqb):
            t = q_ref[:, b * LANES:(b + 1) * LANES].astype(F32)
            qo_ref[:, b * LANES:(b + 1) * LANES] = (t * cv + _swap_halves(t) * sv).astype(BF16)
        t = k_ref[...].astype(F32)
        ko_ref[...] = (t * cv + _swap_halves(t) * sv).astype(BF16)

    tab = pl.BlockSpec((tr, LANES), lambda i: (i, 0))
    return pl.pallas_call(
        body, name=name, grid=(S // tr,),
        in_specs=[pl.BlockSpec((tr, SWA_Q_W), lambda i: (i, q_off // SWA_Q_W)),
                  pl.BlockSpec((tr, LANES), lambda i: (i, k_off // LANES)), tab, tab],
        out_specs=[pl.BlockSpec((tr, SWA_Q_W), lambda i: (i, 0)), tab],
        out_shape=[jax.ShapeDtypeStruct((S, SWA_Q_W), BF16), jax.ShapeDtypeStruct((S, LANES), BF16)],
        compiler_params=_cparams("parallel"),
    )(proj, proj, cos_t, sin_t)


def _rope_bwd(dk_cur, dk_prev, dv_cur, dv_prev, cos_t, sin_t, *, name):
    S = dk_cur.shape[1]
    tr = WINDOW
    nb = S // tr

    def body(kc_ref, kp_ref, vc_ref, vp_ref, c_ref, s_ref, dko_ref, dvo_ref):
        cv, sv = c_ref[...], s_ref[...]
        has_next = (pl.program_id(0) + 1 < nb).astype(F32)
        d = kc_ref[0] + kc_ref[1] + has_next * (kp_ref[0] + kp_ref[1])
        dko_ref[...] = (d * cv + _swap_halves(d * sv)).astype(BF16)
        dvo_ref[...] = (vc_ref[0] + vc_ref[1] + has_next * (vp_ref[0] + vp_ref[1])).astype(BF16)

    tab = pl.BlockSpec((tr, LANES), lambda i: (i, 0))
    cur = pl.BlockSpec((2, tr, LANES), lambda i: (0, i, 0))
    nxt = pl.BlockSpec((2, tr, LANES), lambda i: (0, jnp.minimum(i + 1, nb - 1), 0))
    return pl.pallas_call(
        body, name=name, grid=(nb,),
        in_specs=[cur, nxt, cur, nxt, tab, tab],
        out_specs=[tab, tab],
        out_shape=[jax.ShapeDtypeStruct((S, LANES), BF16), jax.ShapeDtypeStruct((S, LANES), BF16)],
        compiler_params=_cparams("parallel"),
    )(dk_cur, dk_prev, dv_cur, dv_prev, cos_t, sin_t)


def _dot_nt(a, b):
    return lax.dot_general(a, b, (((1,), (1,)), ((), ())), preferred_element_type=F32)


def _dot_tn(a, b):
    return lax.dot_general(a, b, (((0,), (0,)), ((), ())), preferred_element_type=F32)


def _dot_nn(a, b):
    return lax.dot_general(a, b, (((1,), (0,)), ((), ())), preferred_element_type=F32)


def _roll_half(t):
    return pltpu.roll(t.astype(F32), HEAD_DIM, 1).astype(t.dtype)


SWA_STACK = SWA_GROUP // 2


def _swa_common(hk, n, kp_ref, kc_ref, vp_ref, vc_ref):
    k2 = jnp.concatenate([kp_ref[...], kc_ref[...]], axis=0)
    v2 = jnp.concatenate([vp_ref[...], vc_ref[...]], axis=0)
    k_sw, v_sw = _roll_half(k2), _roll_half(v2)
    rows = SWA_STACK * WINDOW
    row = lax.broadcasted_iota(jnp.int32, (rows, 2 * WINDOW), 0) % WINDOW
    col = lax.broadcasted_iota(jnp.int32, (rows, 2 * WINDOW), 1)
    diff = row + WINDOW - col
    allowed = (diff >= 0) & (diff < WINDOW) & ((col >= WINDOW) | (n > 0))
    lane = lax.broadcasted_iota(jnp.int32, (1, LANES), 1)
    half = [lane < HEAD_DIM, lane >= HEAD_DIM]
    kk = [jnp.where(hk == a, k2, k_sw) for a in range(2)]
    vv = [jnp.where(hk == a, v2, v_sw) for a in range(2)]
    return allowed, half, kk, vv


def _swa_stack(ref, mask, scale=None):
    parts = []
    for t in range(SWA_STACK):
        blk = ref[:, t * LANES:(t + 1) * LANES]
        if scale is not None:
            blk = blk * jnp.asarray(scale, blk.dtype)
        parts.append(jnp.where(mask, blk, jnp.zeros_like(blk)))
    return jnp.concatenate(parts, axis=0)


def _swa_sink_column(sink_ref, hk, a):
    blk = lax.broadcasted_iota(jnp.int32, (SWA_STACK * WINDOW, 1), 0) // WINDOW
    col = jnp.zeros((SWA_STACK * WINDOW, 1), F32)
    for t in range(SWA_STACK):
        col = jnp.where(blk == t, sink_ref[hk * SWA_GROUP + 2 * t + a], col)
    return col


def _swa_probs(qm, kk, allowed, sink):
    s = jnp.where(allowed, _dot_nt(qm, kk), NEG)
    m = jnp.maximum(jnp.max(s, axis=1, keepdims=True), sink)
    e = jnp.exp(s - m)
    es = jnp.exp(sink - m)
    inv = 1.0 / (jnp.sum(e, axis=1, keepdims=True) + es)
    return e * inv, es * inv


def _swa_fwd(q_rope, k_rope, proj, sinks, *, v_off, name):
    S = q_rope.shape[0]
    nb = S // WINDOW
    gw = SWA_GROUP * HEAD_DIM

    def body(sink_ref, q_ref, kp_ref, kc_ref, vp_ref, vc_ref, o_ref):
        hk, n = pl.program_id(0), pl.program_id(1)
        allowed, half, kk, vv = _swa_common(hk, n, kp_ref, kc_ref, vp_ref, vc_ref)
        outs = []
        for a in range(2):
            qm = _swa_stack(q_ref, half[a], ATT_SCALE)
            p, _ = _swa_probs(qm, kk[a], allowed, _swa_sink_column(sink_ref, hk, a))
            outs.append(_dot_nn(p.astype(BF16), vv[a]))
        for t in range(SWA_STACK):
            rows = slice(t * WINDOW, (t + 1) * WINDOW)
            o_ref[:, t * LANES:(t + 1) * LANES] = jnp.where(half[0], outs[0][rows], outs[1][rows]).astype(BF16)

    prev = lambda hk, n: (jnp.maximum(n - 1, 0), 0)
    cur = lambda hk, n: (n, 0)
    vprev = lambda hk, n: (jnp.maximum(n - 1, 0), v_off // LANES)
    vcur = lambda hk, n: (n, v_off // LANES)
    blk = lambda m: pl.BlockSpec((WINDOW, LANES), m)
    return pl.pallas_call(
        body, name=name, grid=(2, nb),
        in_specs=[pl.BlockSpec(memory_space=pltpu.SMEM),
                  pl.BlockSpec((WINDOW, gw), lambda hk, n: (n, hk)),
                  blk(prev), blk(cur), blk(vprev), blk(vcur)],
        out_specs=pl.BlockSpec((WINDOW, gw), lambda hk, n: (n, hk)),
        out_shape=jax.ShapeDtypeStruct((S, SWA_Q_W), BF16),
        compiler_params=_cparams("parallel", "parallel"),
    )(sinks, q_rope, k_rope, k_rope, proj, proj)


def _swa_bwd(q_rope, k_rope, proj, sinks, d_o, cos_t, sin_t, *, v_off, name):
    S = q_rope.shape[0]
    nb = S // WINDOW
    gw = SWA_GROUP * HEAD_DIM

    def body(sink_ref, q_ref, kp_ref, kc_ref, vp_ref, vc_ref, do_ref, c_ref, s_ref,
             dq_ref, dkc_ref, dkp_ref, dvc_ref, dvp_ref, dsink_ref):
        hk, n = pl.program_id(0), pl.program_id(1)
        allowed, half, kk, vv = _swa_common(hk, n, kp_ref, kc_ref, vp_ref, vc_ref)
        dk_acc = jnp.zeros((2 * WINDOW, LANES), F32)
        dv_acc = jnp.zeros((2 * WINDOW, LANES), F32)
        srow = lax.broadcasted_iota(jnp.int32, (SWA_GROUP, LANES), 0)
        dsink = jnp.zeros((SWA_GROUP, LANES), F32)
        dqs = []
        for a in range(2):
            qm = _swa_stack(q_ref, half[a], ATT_SCALE)
            dom = _swa_stack(do_ref, half[a])
            p, psink = _swa_probs(qm, kk[a], allowed, _swa_sink_column(sink_ref, hk, a))
            dp = _dot_nt(dom, vv[a])
            delta = jnp.sum(p * dp, axis=1, keepdims=True)
            ds = (p * (dp - delta)).astype(BF16)
            dsk = psink * delta
            for t in range(SWA_STACK):
                dsink = dsink + jnp.where(srow == 2 * t + a, -jnp.sum(dsk[t * WINDOW:(t + 1) * WINDOW]), 0.0)
            dqs.append(_dot_nn(ds, kk[a]) * ATT_SCALE)
            dk_acc = dk_acc + _dot_tn(ds, qm)
            dv_acc = dv_acc + _dot_tn(p.astype(BF16), dom)
        cv, sv = c_ref[...], s_ref[...]
        for t in range(SWA_STACK):
            rows = slice(t * WINDOW, (t + 1) * WINDOW)
            d = jnp.where(half[0], dqs[0][rows], dqs[1][rows])
            dq_ref[:, t * LANES:(t + 1) * LANES] = (d * cv + _swap_halves(d * sv)).astype(BF16)
        lane = lax.broadcasted_iota(jnp.int32, (1, LANES), 1)
        mine = (lane >= HEAD_DIM) == (hk == 1)
        dk_t = jnp.where(mine, dk_acc + pltpu.roll(dk_acc, HEAD_DIM, 1), 0.0)
        dv_t = jnp.where(mine, dv_acc + pltpu.roll(dv_acc, HEAD_DIM, 1), 0.0)
        dkp_ref[0] = dk_t[:WINDOW]
        dkc_ref[0] = dk_t[WINDOW:]
        dvp_ref[0] = dv_t[:WINDOW]
        dvc_ref[0] = dv_t[WINDOW:]

        @pl.when(n == 0)
        def _():
            dsink_ref[0] = dsink

        @pl.when(n > 0)
        def _():
            dsink_ref[0] += dsink

    prev = lambda hk, n: (jnp.maximum(n - 1, 0), 0)
    cur = lambda hk, n: (n, 0)
    vprev = lambda hk, n: (jnp.maximum(n - 1, 0), v_off // LANES)
    vcur = lambda hk, n: (n, v_off // LANES)
    blk = lambda m: pl.BlockSpec((WINDOW, LANES), m)
    qblk = pl.BlockSpec((WINDOW, gw), lambda hk, n: (n, hk))
    part = pl.BlockSpec((1, WINDOW, LANES), lambda hk, n: (hk, n, 0))
    part_shape = jax.ShapeDtypeStruct((2, S, LANES), F32)
    return pl.pallas_call(
        body, name=name, grid=(2, nb),
        in_specs=[pl.BlockSpec(memory_space=pltpu.SMEM), qblk, blk(prev), blk(cur), blk(vprev), blk(vcur), qblk,
                  blk(cur), blk(cur)],
        out_specs=[qblk, part, part, part, part,
                   pl.BlockSpec((1, SWA_GROUP, LANES), lambda hk, n: (hk, 0, 0))],
        out_shape=[jax.ShapeDtypeStruct((S, SWA_Q_W), BF16), part_shape, part_shape, part_shape, part_shape,
                   jax.ShapeDtypeStruct((2, SWA_GROUP, LANES), F32)],
        compiler_params=_cparams("parallel", "arbitrary"),
    )(sinks, q_rope, k_rope, k_rope, proj, proj, d_o, cos_t, sin_t)


def _fox_prep(z_t, bias_col, *, name):
    H, S = z_t.shape
    tb = _pick(S, 512)

    def body(z_ref, b_ref, o_ref, carry_ref):
        @pl.when(pl.program_id(0) == 0)
        def _():
            carry_ref[...] = jnp.zeros_like(carry_ref)

        zz = z_ref[...] + b_ref[...]
        t = jnp.exp(-jnp.abs(zz))
        log1p = jnp.where(t < 1e-2, t * (1.0 - t * (0.5 - t * (1.0 / 3.0))), jnp.log(1.0 + t))
        logf = jnp.minimum(zz, 0.0) - log1p
        r = lax.broadcasted_iota(jnp.int32, (tb, tb), 0)
        c = lax.broadcasted_iota(jnp.int32, (tb, tb), 1)
        tri = (r <= c).astype(BF16)
        hi = logf.astype(BF16)
        r1 = logf - hi.astype(F32)
        mid = r1.astype(BF16)
        lo = (r1 - mid.astype(F32)).astype(BF16)
        cs = _dot_nn(hi, tri) + _dot_nn(mid, tri) + _dot_nn(lo, tri) + carry_ref[:, 0:1]
        o_ref[...] = -cs
        carry_ref[...] = jnp.zeros_like(carry_ref) + cs[:, tb - 1:tb]

    return pl.pallas_call(
        body, name=name, grid=(S // tb,),
        in_specs=[pl.BlockSpec((H, tb), lambda i: (0, i)), pl.BlockSpec((H, 1), lambda i: (0, 0))],
        out_specs=pl.BlockSpec((H, tb), lambda i: (0, i)),
        out_shape=jax.ShapeDtypeStruct((H, S), F32),
        scratch_shapes=[pltpu.VMEM((H, LANES), F32)],
        compiler_params=_cparams("arbitrary"),
    )(z_t, bias_col)


def _fox_post(drow, dcol, z_t, bias_col, *, name):
    H, S = z_t.shape
    tb = _pick(S, 512)
    nb = S // tb

    def body(dr_ref, d_ref, z_ref, b_ref, dz_ref, db_ref, carry_ref):
        @pl.when(pl.program_id(0) == 0)
        def _():
            carry_ref[...] = jnp.zeros_like(carry_ref)
            db_ref[...] = jnp.zeros_like(db_ref)

        dc = dr_ref[...] - d_ref[...]
        r = lax.broadcasted_iota(jnp.int32, (tb, tb), 0)
        c = lax.broadcasted_iota(jnp.int32, (tb, tb), 1)
        tri = (r >= c).astype(BF16)
        hi = dc.astype(BF16)
        r1 = dc - hi.astype(F32)
        mid = r1.astype(BF16)
        lo = (r1 - mid.astype(F32)).astype(BF16)
        dlogf = _dot_nn(hi, tri) + _dot_nn(mid, tri) + _dot_nn(lo, tri) + carry_ref[:, 0:1]
        carry_ref[...] = jnp.zeros_like(carry_ref) + dlogf[:, 0:1]
        dz = dlogf * _sigmoid(-(z_ref[...] + b_ref[...]))
        dz_ref[...] = dz
        db_ref[...] += jnp.sum(dz, axis=1, keepdims=True)

    rev = lambda i: (0, nb - 1 - i)
    return pl.pallas_call(
        body, name=name, grid=(nb,),
        in_specs=[pl.BlockSpec((H, tb), rev), pl.BlockSpec((H, tb), rev), pl.BlockSpec((H, tb), rev),
                  pl.BlockSpec((H, 1), lambda i: (0, 0))],
        out_specs=[pl.BlockSpec((H, tb), rev), pl.BlockSpec((H, LANES), lambda i: (0, 0))],
        out_shape=[jax.ShapeDtypeStruct((H, S), F32), jax.ShapeDtypeStruct((H, LANES), F32)],
        scratch_shapes=[pltpu.VMEM((H, LANES), F32)],
        compiler_params=_cparams("arbitrary"),
    )(drow, dcol, z_t, bias_col)


def _fox_blocks(S):
    cap = max(LANES, S // 4)
    return (min(FOX_FWD_BLOCKS[0], cap), min(FOX_FWD_BLOCKS[1], cap)), \
           (min(FOX_BWD_BLOCKS[0], cap), min(FOX_BWD_BLOCKS[1], cap))


def _key_bias_blocks(negc, bk):
    H, S = negc.shape
    return negc.reshape(H // 2, 2, S // bk, bk).transpose(0, 2, 1, 3)


def _fox_fwd(proj, negc4, *, q_off, k_off, v_off, bq, bk, name):
    S = proj.shape[0]
    nq, nk = S // bq, S // bk
    npair = FOX_HEADS // 2
    assert bq % bk == 0 or bk % bq == 0
    nmask = max(1, bq // bk)

    gp = FOX_FWD_PAIRS
    gw = gp * LANES
    assert q_off % gw == 0 and k_off % gw == 0 and v_off % gw == 0 and npair % gp == 0

    def body(q_ref, k_ref, v_ref, nc_ref, o_ref, lse_ref):
        i = pl.program_id(1)
        lane = lax.broadcasted_iota(jnp.int32, (1, LANES), 1)
        half = [lane < HEAD_DIM, lane >= HEAD_DIM]
        qh = []
        for g in range(gp):
            q2 = q_ref[:, g * LANES:(g + 1) * LANES] * jnp.asarray(ATT_SCALE, BF16)
            qh += [jnp.where(half[h], q2, jnp.zeros_like(q2)) for h in range(2)]
        row = lax.broadcasted_iota(jnp.int32, (bq, bk), 0)
        col = lax.broadcasted_iota(jnp.int32, (bq, bk), 1)
        rel = row - col
        nfull = (i * bq) // bk

        spare = [HEAD_DIM, 0]
        ones_lane = [lane == spare[h] for h in range(2)]

        def step(j, carry, masked):
            start = pl.multiple_of(j * bk, bk)
            new = []
            for g in range(gp):
                ks = k_ref[pl.ds(start, bk), g * LANES:(g + 1) * LANES]
                vs = v_ref[pl.ds(start, bk), g * LANES:(g + 1) * LANES]
                nb = nc_ref[g, j]
                for h in range(2):
                    m, acc = carry[4 * g + 2 * h:4 * g + 2 * h + 2]
                    vh = jnp.where(half[h], vs, jnp.where(ones_lane[h], jnp.ones_like(vs), jnp.zeros_like(vs)))
                    qs, bias = qh[2 * g + h], nb[h:h + 1, :]

                    def update(m, acc, rows, keys):
                        s = _dot_nt(qs[rows], ks[keys]) + bias[:, keys]
                        if masked:
                            s = jnp.where(rel[rows, keys] >= j * bk - i * bq, s, NEG)
                        m_new = jnp.maximum(m[rows], jnp.max(s, axis=1, keepdims=True))
                        p = jnp.exp(s - m_new).astype(BF16)
                        return m_new, jnp.exp(m[rows] - m_new) * acc[rows] + _dot_nn(p, vh[keys])

                    if masked and bq == bk:
                        top, bot, everything = slice(0, bq // 2), slice(bq // 2, bq), slice(0, bk)
                        m_t, acc_t = update(m, acc, top, top)
                        m_b, acc_b = update(m, acc, bot, everything)
                        new += [jnp.concatenate([m_t, m_b], axis=0), jnp.concatenate([acc_t, acc_b], axis=0)]
                    else:
                        new += list(update(m, acc, slice(0, bq), slice(0, bk)))
            return tuple(new)

        init = (jnp.full((bq, 1), NEG, F32), jnp.zeros((bq, LANES), F32)) * (2 * gp)
        carry = lax.fori_loop(0, nfull, lambda j, c: step(j, c, False), init)
        for t in range(nmask):
            carry = step(nfull + t, carry, True)
        for g in range(gp):
            outs, lses = [], []
            for h in range(2):
                m, acc = carry[4 * g + 2 * h:4 * g + 2 * h + 2]
                l = acc[:, spare[h]:spare[h] + 1]
                outs.append(acc * (1.0 / l))
                lses.append(m + jnp.log(l))
            o_ref[:, g * LANES:(g + 1) * LANES] = jnp.where(half[0], outs[0], outs[1]).astype(BF16)
            lse_ref[g] = jnp.where(half[0], lses[0], lses[1])

    seq = lambda off: pl.BlockSpec((S, gw), lambda hp, i: (0, off // gw + hp))
    return pl.pallas_call(
        body, name=name, grid=(npair // gp, nq),
        in_specs=[pl.BlockSpec((bq, gw), lambda hp, i: (i, q_off // gw + hp)), seq(k_off), seq(v_off),
                  pl.BlockSpec((gp, nk, 2, bk), lambda hp, i: (hp, 0, 0, 0))],
        out_specs=[pl.BlockSpec((bq, gw), lambda hp, i: (i, hp)),
                   pl.BlockSpec((gp, bq, LANES), lambda hp, i: (hp, i, 0))],
        out_shape=[jax.ShapeDtypeStruct((S, FOX_W), BF16), jax.ShapeDtypeStruct((npair, S, LANES), F32)],
        compiler_params=_cparams("parallel", "parallel"),
    )(proj, proj, proj, negc4)


def _fox_bwd(proj, negc4, o, lse, d_o, q_t, do_t, *, q_off, k_off, v_off, bq, bk, name, deps=()):
    S = proj.shape[0]
    nq, nk = S // bq, S // bk
    npair = FOX_HEADS // 2
    assert bq % bk == 0 or bk % bq == 0
    nmask = max(1, bk // bq)

    def body(q_ref, k_ref, v_ref, nc_ref, o_ref, lse_ref, do_ref, qt_ref, dot_ref, *rest):
        dqo_ref, dk_ref, dv_ref, dn_ref, dr_ref, delta_ref, rs_ref, dq_ref = rest[len(deps):]
        j = pl.program_id(1)
        lane = lax.broadcasted_iota(jnp.int32, (1, LANES), 1)
        half = [lane < HEAD_DIM, lane >= HEAD_DIM]
        spare = [HEAD_DIM, 0]
        ones_lane = [lane == spare[h] for h in range(2)]
        srow = lax.broadcasted_iota(jnp.int32, (LANES, 1), 0)
        rhalf = [srow < HEAD_DIM, srow >= HEAD_DIM]
        ones_row = [srow == spare[h] for h in range(2)]
        k2, v2 = k_ref[...], v_ref[...]
        one_k = jnp.ones_like(k2)
        kh = [jnp.where(half[h], k2, jnp.where(ones_lane[h], one_k, jnp.zeros_like(k2))) for h in range(2)]
        nb = nc_ref[0, 0]
        row = lax.broadcasted_iota(jnp.int32, (bq, bk), 0)
        col = lax.broadcasted_iota(jnp.int32, (bq, bk), 1)
        rel = row - col
        i_first = (j * bk) // bq

        @pl.when(j == 0)
        def _():
            dq_ref[...] = jnp.zeros_like(dq_ref)
            rs_ref[...] = jnp.zeros_like(rs_ref)
            for b in range(nq):
                prod = do_ref[b * bq:(b + 1) * bq, :].astype(F32) * o_ref[b * bq:(b + 1) * bq, :].astype(F32)
                d0 = jnp.sum(jnp.where(half[0], prod, 0.0), axis=1, keepdims=True)
                d1 = jnp.sum(jnp.where(half[1], prod, 0.0), axis=1, keepdims=True)
                delta_ref[b * bq:(b + 1) * bq, :] = jnp.where(half[0], d0, d1)

        def step(i, carry, masked, r0=0):
            dkt_a, dkt_b, dvt = carry
            dkts = [dkt_a, dkt_b]
            nr = bq - r0
            start = pl.multiple_of(i * bq + r0, LANES)
            q2 = q_ref[pl.ds(start, nr), :] * jnp.asarray(ATT_SCALE, BF16)
            do2 = do_ref[pl.ds(start, nr), :]
            qt = qt_ref[i][:, r0:] * jnp.asarray(ATT_SCALE, BF16)
            dot = dot_ref[i][:, r0:]
            lse2 = lse_ref[0, pl.ds(start, nr), :]
            del2 = delta_ref[pl.ds(start, nr), :]
            dqf = []
            for h in range(2):
                qm = jnp.where(half[h], q2, jnp.zeros_like(q2))
                dom = jnp.where(half[h], do2, jnp.zeros_like(do2))
                qtm = jnp.where(rhalf[h], qt, jnp.where(ones_row[h], jnp.ones_like(qt), jnp.zeros_like(qt)))
                dotm = jnp.where(rhalf[h], dot, jnp.zeros_like(dot))
                c0 = h * HEAD_DIM
                p = jnp.exp(_dot_nt(qm, k2) + nb[h:h + 1, :] - lse2[:, c0:c0 + 1])
                if masked:
                    p = jnp.where(rel[r0:] >= j * bk - i * bq, p, 0.0)
                dp = _dot_nt(dom, v2)
                dsb = (p * (dp - del2[:, c0:c0 + 1])).astype(BF16)
                dvt = dvt + _dot_nn(dotm, p.astype(BF16))
                dkts[h] = dkts[h] + _dot_nn(qtm, dsb)
                dqf.append(_dot_nn(dsb, kh[h]))
            dq_ref[pl.ds(start, nr), :] += jnp.where(half[0], dqf[0], dqf[1]) * ATT_SCALE
            rs_ref[pl.ds(start, nr), :] += jnp.where(ones_lane[0], dqf[0], jnp.where(ones_lane[1], dqf[1], 0.0))
            return dkts[0], dkts[1], dvt

        zero = jnp.zeros((LANES, bk), F32)
        carry = (zero, zero, zero)
        if bq > bk:
            sp = j % (bq // bk)
            carry = lax.switch(sp, [functools.partial(step, i_first, masked=True, r0=s * bk)
                                    for s in range(bq // bk)], carry)
        else:
            for t in range(nmask):
                carry = step(i_first + t, carry, True)
        dkt_a, dkt_b, dvt = lax.fori_loop(i_first + nmask, nq, lambda i, c: step(i, c, False), carry)
        dk_ref[...] = jnp.where(rhalf[0], dkt_a, dkt_b).T.astype(BF16)
        dv_ref[...] = dvt.T.astype(BF16)
        dn_ref[0, 0] = jnp.concatenate([dkt_a[spare[0]:spare[0] + 1], dkt_b[spare[1]:spare[1] + 1]], axis=0)

        @pl.when(j == nk - 1)
        def _():
            dqo_ref[...] = dq_ref[...].astype(BF16)
            for b in range(nq):
                t = rs_ref[b * bq:(b + 1) * bq, :].T
                dr_ref[0, b] = jnp.concatenate([t[spare[0]:spare[0] + 1], t[spare[1]:spare[1] + 1]], axis=0)

    once = pl.Buffered(1)
    seq = lambda off: pl.BlockSpec((S, LANES), lambda hp, j: (0, off // LANES + hp), pipeline_mode=once)
    blk = lambda off: pl.BlockSpec((bk, LANES), lambda hp, j: (j, off // LANES + hp))
    nc = pl.BlockSpec((1, 1, 2, bk), lambda hp, j: (hp, j, 0, 0))
    tsp = pl.BlockSpec((nq, LANES, bq), lambda hp, j: (0, hp, 0), pipeline_mode=once)
    return pl.pallas_call(
        body, name=name, grid=(npair, nk),
        in_specs=[seq(q_off), blk(k_off), blk(v_off), nc, seq(0),
                  pl.BlockSpec((1, S, LANES), lambda hp, j: (hp, 0, 0), pipeline_mode=once), seq(0),
                  tsp, tsp] + [_ANY] * len(deps),
        out_specs=[pl.BlockSpec((S, LANES), lambda hp, j: (0, hp)), blk(0), blk(0), nc,
                   pl.BlockSpec((1, nq, 2, bq), lambda hp, j: (hp, 0, 0, 0))],
        out_shape=[jax.ShapeDtypeStruct((S, FOX_W), BF16), jax.ShapeDtypeStruct((S, FOX_W), BF16),
                   jax.ShapeDtypeStruct((S, FOX_W), BF16), jax.ShapeDtypeStruct((npair, nk, 2, bk), F32),
                   jax.ShapeDtypeStruct((npair, nq, 2, bq), F32)],
        scratch_shapes=[pltpu.VMEM((S, LANES), F32), pltpu.VMEM((S, LANES), F32), pltpu.VMEM((S, LANES), F32)],
        compiler_params=_cparams("parallel", "arbitrary"),
    )(proj, proj, proj, negc4, o, lse, d_o, q_t, do_t, *deps)


def _exchange(arrs, *, gather, name):
    n = len(arrs)
    npeer = N_DEV - 1

    def body(*refs):
        ins, outs = refs[:n], refs[n:2 * n]
        send_sems, recv_sems, loc_sems = refs[2 * n:]
        x, y, c = lax.axis_index("x"), lax.axis_index("y"), lax.axis_index("c")
        me = 4 * x + 2 * y + c
        peers = []
        for k in range(1, N_DEV):
            px = 1 - x if k & 4 else x
            py = 1 - y if k & 2 else y
            pc = 1 - c if k & 1 else c
            peers.append(((px, py, pc), 4 * px + 2 * py + pc))

        def remote(w, k):
            dev, idx = peers[k]
            src = ins[w] if gather else ins[w].at[idx]
            return pltpu.make_async_remote_copy(
                src_ref=src, dst_ref=outs[w].at[me],
                send_sem=send_sems.at[w * npeer + k], recv_sem=recv_sems.at[w * npeer + k],
                device_id=dev, device_id_type=pl.DeviceIdType.MESH)

        def arrival(w, k):
            dev, idx = peers[k]
            src = ins[w] if gather else ins[w].at[idx]
            return pltpu.make_async_remote_copy(
                src_ref=src, dst_ref=outs[w].at[idx],
                send_sem=send_sems.at[w * npeer + k], recv_sem=recv_sems.at[w * npeer + k],
                device_id=dev, device_id_type=pl.DeviceIdType.MESH)

        local = []
        for w in range(n):
            for k in range(npeer):
                remote(w, k).start()
            cp = pltpu.make_async_copy(ins[w] if gather else ins[w].at[me], outs[w].at[me], loc_sems.at[w])
            cp.start()
            local.append(cp)
        for w in range(n):
            for k in range(npeer):
                arrival(w, k).wait_recv()
        for w in range(n):
            for k in range(npeer):
                remote(w, k).wait_send()
            local[w].wait()

    hbm = pl.BlockSpec(memory_space=pl.ANY)
    out_shape = [jax.ShapeDtypeStruct((N_DEV,) + (a.shape if gather else a.shape[1:]), a.dtype) for a in arrs]
    return pl.pallas_call(
        body, name=name,
        in_specs=[hbm] * n, out_specs=[hbm] * n, out_shape=out_shape,
        scratch_shapes=[pltpu.SemaphoreType.DMA((n * npeer,)), pltpu.SemaphoreType.DMA((n * npeer,)),
                        pltpu.SemaphoreType.DMA((n,))],
        compiler_params=pltpu.CompilerParams(has_side_effects=True),
    )(*arrs)


def _gather_two_level(shard, *, name):
    def body(x_ref, out_ref, send_sems, recv_sems, local_sem):
        x, y, c = lax.axis_index("x"), lax.axis_index("y"), lax.axis_index("c")
        me, sibling = (x, y, c), (x, y, 1 - c)
        chips = [(1 - x, y), (x, 1 - y), (1 - x, 1 - y)]

        def slot(px, py, pc):
            return out_ref.at[4 * px + 2 * py + pc]

        def copy(k, block, to, src=None):
            return pltpu.make_async_remote_copy(
                src_ref=slot(*block) if src is None else src, dst_ref=slot(*block),
                send_sem=send_sems.at[k], recv_sem=recv_sems.at[k],
                device_id=to, device_id_type=pl.DeviceIdType.MESH)

        mine = pltpu.make_async_copy(x_ref, slot(*me), local_sem)
        mine.start()
        first = [copy(0, me, sibling, src=x_ref)]
        first += [copy(1 + j, me, (*chip, c), src=x_ref) for j, chip in enumerate(chips)]
        for cp in first:
            cp.start()
        passed = [copy(4 + j, (*chip, c), sibling) for j, chip in enumerate(chips)]
        for j, chip in enumerate(chips):
            copy(1 + j, (*chip, c), me).wait_recv()
            passed[j].start()
        copy(0, sibling, me).wait_recv()
        for j, chip in enumerate(chips):
            copy(4 + j, (*chip, 1 - c), me).wait_recv()
        for cp in first + passed:
            cp.wait_send()
        mine.wait()

    return pl.pallas_call(
        body, name=name,
        in_specs=[_ANY], out_specs=_ANY,
        out_shape=jax.ShapeDtypeStruct((N_DEV,) + shard.shape, shard.dtype),
        scratch_shapes=[pltpu.SemaphoreType.DMA((N_DEV - 1,)), pltpu.SemaphoreType.DMA((N_DEV - 1,)),
                        pltpu.SemaphoreType.DMA],
        compiler_params=pltpu.CompilerParams(has_side_effects=True),
    )(shard)


_HBM = pl.BlockSpec(memory_space=pltpu.HBM)
_SEM = pl.BlockSpec(memory_space=pltpu.SEMAPHORE)
_EFFECT = pltpu.SideEffectType.DATAFLOW_SIDE_EFFECTING
NPEER = N_DEV - 1


def _peer_table():
    x, y, c = lax.axis_index("x"), lax.axis_index("y"), lax.axis_index("c")
    peers = []
    for k in range(1, N_DEV):
        px = 1 - x if k & 4 else x
        py = 1 - y if k & 2 else y
        pc = 1 - c if k & 1 else c
        peers.append(((px, py, pc), 4 * px + 2 * py + pc))
    return 4 * x + 2 * y + c, peers


def _split_copy(ins, lands, send_sems, recv_sems, gather, me, peers, w, k, arriving):
    dev, idx = peers[k]
    return pltpu.make_async_remote_copy(
        src_ref=ins[w] if gather else ins[w].at[idx],
        dst_ref=lands[w].at[idx if arriving else me],
        send_sem=send_sems.at[w * NPEER + k], recv_sem=recv_sems.at[w * NPEER + k],
        device_id=dev, device_id_type=pl.DeviceIdType.MESH)


def _exchange_start(arrs, *, gather, name, deps=()):
    n = len(arrs)
    land_shapes = [(N_DEV,) + (a.shape if gather else a.shape[1:]) for a in arrs]

    def body(*refs):
        ins, lands = refs[:n], refs[n:2 * n]
        send_sems, recv_sems = refs[2 * n + len(deps)], refs[2 * n + len(deps) + 1]
        token = refs[-1]
        me, peers = _peer_table()
        for w in range(n):
            for k in range(NPEER):
                _split_copy(ins, lands, send_sems, recv_sems, gather, me, peers, w, k, False).start()
        token[...] = jnp.zeros_like(token)

    out_shape = ([pltpu.SemaphoreType.DMA((n * NPEER,)), pltpu.SemaphoreType.DMA((n * NPEER,))]
                 + [pltpu.HBM(a.shape, a.dtype) for a in arrs]
                 + [pltpu.HBM(s, a.dtype) for s, a in zip(land_shapes, arrs)]
                 + [jax.ShapeDtypeStruct((8, LANES), F32)])
    res = pl.pallas_call(
        body, name=name,
        in_specs=[_HBM] * (2 * n) + [_ANY] * len(deps),
        out_specs=[_SEM, _SEM] + [_HBM] * (2 * n) + [pl.BlockSpec(memory_space=pltpu.VMEM)],
        out_shape=out_shape,
        input_output_aliases={i: 2 + i for i in range(2 * n)},
        compiler_params=pltpu.CompilerParams(has_side_effects=_EFFECT),
    )(*[pltpu.with_memory_space_constraint(a, pltpu.HBM) for a in arrs],
      *[pltpu.with_memory_space_constraint(lax.empty(s, a.dtype), pltpu.HBM) for s, a in zip(land_shapes, arrs)],
      *deps)
    return (n, gather, res[0], res[1], res[2:2 + n], res[2 + n:2 + 2 * n]), res[-1]


def _exchange_wait(handle, after, *, name):
    n, gather, send_sems, recv_sems, ins_thru, lands_thru = handle

    def body(*refs):
        ins, lands = refs[:n], refs[n:2 * n]
        send_s, recv_s = refs[2 * n], refs[2 * n + 1]
        me, peers = _peer_table()
        for w in range(n):
            for k in range(NPEER):
                _split_copy(ins, lands, send_s, recv_s, gather, me, peers, w, k, False).wait_send()
                _split_copy(ins, lands, send_s, recv_s, gather, me, peers, w, k, True).wait_recv()

    res = pl.pallas_call(
        body, name=name,
        in_specs=[_HBM] * (2 * n) + [_SEM, _SEM, pl.BlockSpec(memory_space=pl.ANY)],
        out_specs=[_HBM] * (2 * n),
        out_shape=[pltpu.HBM(a.shape, a.dtype) for a in list(ins_thru) + list(lands_thru)],
        input_output_aliases={i: i for i in range(2 * n)},
        compiler_params=pltpu.CompilerParams(has_side_effects=_EFFECT),
    )(*ins_thru, *lands_thru, send_sems, recv_sems, after)
    return res[:n], res[n:2 * n]


def _ordered_sum(s_ref, own_ref):
    if own_ref is None:
        blocks = [s_ref[q].astype(F32) for q in range(N_DEV)]
    else:
        me = 4 * lax.axis_index("x") + 2 * lax.axis_index("y") + lax.axis_index("c")
        own = own_ref[...]
        blocks = [jnp.where(me == q, own, s_ref[q]).astype(F32) for q in range(N_DEV)]
    acc = blocks[0]
    for b in blocks[1:]:
        acc = acc + b
    return acc


def _sum8(stack, own, *, name):
    _, R, C = stack.shape
    if R % 8 == 0:
        tr, tc = _pick(R, max(8, STEP_BYTES // (C * 4 * (N_DEV + 2))), 8), C
    else:
        tr, tc = R, _pick(C, max(LANES, STEP_BYTES // (R * 4 * (N_DEV + 2))))

    def body(s_ref, own_ref, o_ref):
        o_ref[...] = _ordered_sum(s_ref, own_ref)

    blk = pl.BlockSpec((tr, tc), lambda i, j: (i, j))
    return pl.pallas_call(
        body, name=name, grid=(R // tr, C // tc),
        in_specs=[pl.BlockSpec((N_DEV, tr, tc), lambda i, j: (0, i, j)), blk],
        out_specs=blk,
        out_shape=jax.ShapeDtypeStruct((R, C), F32),
        compiler_params=_cparams("parallel", "parallel"),
    )(stack, own)


def _adamw_math(w, g, m, v):
    m = ADAM_B1 * m + (1.0 - ADAM_B1) * g
    v = ADAM_B2 * v + (1.0 - ADAM_B2) * (g * g)
    m_hat = m / (1.0 - ADAM_B1 ** ADAM_STEP)
    v_hat = v / (1.0 - ADAM_B2 ** ADAM_STEP)
    delta = -ADAM_LR * (m_hat / (jnp.sqrt(v_hat) + ADAM_EPS) + ADAM_WD * w)
    return delta, m, v


def _adamw(w, g, m, v, *, name, stacked, own=None, transposed=False):
    R, C = w.shape
    if transposed:
        tr = _pick(R, max(LANES, STEP_BYTES // (C * 4 * (9 + N_DEV))))
    else:
        tr = _pick(R, max(8, STEP_BYTES // (C * 4 * (8 + (N_DEV if stacked else 1)))), 8)
    has_own = own is not None

    def body(w_ref, g_ref, m_ref, v_ref, *rest):
        go_ref, d_ref, mo_ref, vo_ref = rest[-4:]
        g = _ordered_sum(g_ref, rest[0] if has_own else None) if stacked else g_ref[...]
        if transposed:
            g = g.T
        delta, m2, v2 = _adamw_math(w_ref[...], g, m_ref[...], v_ref[...])
        go_ref[...] = g
        d_ref[...] = delta
        mo_ref[...] = m2
        vo_ref[...] = v2

    row = pl.BlockSpec((tr, C), lambda i: (i, 0))
    if transposed:
        g_spec, own_spec = pl.BlockSpec((N_DEV, C, tr), lambda i: (0, 0, i)), pl.BlockSpec((C, tr), lambda i: (0, i))
    else:
        g_spec, own_spec = (pl.BlockSpec((N_DEV, tr, C), lambda i: (0, i, 0)) if stacked else row), row
    return pl.pallas_call(
        body, name=name, grid=(R // tr,),
        in_specs=[row, g_spec, row, row] + [own_spec] * has_own, out_specs=[row] * 4,
        out_shape=[jax.ShapeDtypeStruct((R, C), F32)] * 4,
        compiler_params=_cparams("parallel"),
    )(w, g, m, v, *([own] if has_own else []))


def kernel(x, positions, attn_norm, w_in, fox_f_bias, swa_sinks, w_branch_swa, w_branch_fox, w_out, mlp_norm, w_up, w_down, final_norm, loss_target, m_attn_norm, m_w_in, m_fox_f_bias, m_swa_sinks, m_w_branch_swa, m_w_branch_fox, m_w_out, m_mlp_norm, m_w_up, m_w_down, m_final_norm, v_attn_norm, v_w_in, v_fox_f_bias, v_swa_sinks, v_w_branch_swa, v_w_branch_fox, v_w_out, v_mlp_norm, v_w_up, v_w_down, v_final_norm):
    S, D = x.shape[1], x.shape[2]
    DFF = w_up.shape[2] * N_DEV
    d_in = w_in.shape[2] * N_DEV
    assert d_in == QKV_W + FOX_HEADS + 2 * D and (2 * D) % SWA_Q_W == 0 and S % (4 * LANES) == 0
    q_off = 2 * D
    k_off = q_off + SWA_Q_W
    v_off = k_off + SWA_KV_W
    fq_off = v_off + SWA_KV_W
    fk_off = fq_off + FOX_W
    fv_off = fk_off + FOX_W
    fl_off = fv_off + FOX_W
    NP = fl_off + FL_PAD
    x2d, tgt = x[0], loss_target[0]

    shards = [w_in[0].T.astype(BF16), w_branch_swa[0].T.astype(BF16), w_branch_fox[0].T.astype(BF16),
              w_out[0].astype(BF16), w_up[0].T.astype(BF16), w_down[0].astype(BF16)]
    me = 4 * lax.axis_index("x") + 2 * lax.axis_index("y") + lax.axis_index("c")

    def filled(stack, own):
        return lax.dynamic_update_slice(stack, own[None], (me,) + (0,) * own.ndim)

    g_in = _gather_two_level(shards[0], name="gather_w_in")
    h_rest, tok_rest = _exchange_start(shards[1:], gather=True, name="gather_rest_start", deps=[g_in])

    tm = _pick(S, 1024)
    td = _pick(D, 1024)
    tf = _pick(DFF, 1024)
    tnp = _pick(NP, 1024)

    h1 = _rms_fwd(x2d, attn_norm, name="rms1", deps=[tok_rest])
    w_in_t = g_in.reshape(d_in, D)
    w_in_p = jnp.concatenate([w_in_t[QKV_W + FOX_HEADS:], w_in_t[:QKV_W], w_in_t[QKV_W:QKV_W + FOX_HEADS],
                              jnp.zeros((FL_PAD - FOX_HEADS, D), BF16)], axis=0)
    w_fl_t = w_in_t[QKV_W:QKV_W + FOX_HEADS]
    proj, = _matmul(h1, w_in_p, mode="nt", name="mm_in", out_dtypes=[BF16], tm=_pick(S, 2048), tn=tnp, tk=D)
    z_sd, = _matmul(h1, w_fl_t, mode="nt", name="mm_flogit", out_dtypes=[F32], tm=tm, tn=FOX_HEADS, tk=D)
    z_t = z_sd.T
    bias_col = fox_f_bias.reshape(FOX_HEADS, 1)
    negc = _fox_prep(z_t, bias_col, name="fox_prep")
    (fbq, fbk), (bbq, bbk) = _fox_blocks(S)
    inv_freq = ROPE_THETA ** (-jnp.arange(0, HEAD_DIM, 2, dtype=F32) / HEAD_DIM)
    invf = jnp.tile(inv_freq, LANES // (HEAD_DIM // 2)).reshape(1, LANES)
    cos_t, sin_t = _rope_tables(positions.reshape(S, 1), invf, name="rope_tables")
    q_rope, k_rope = _rope_fwd(proj, cos_t, sin_t, q_off=q_off, k_off=k_off, name="rope_fwd")
    sinks = swa_sinks.reshape(-1)
    o_a = _swa_fwd(q_rope, k_rope, proj, sinks, v_off=v_off, name="swa_fwd")
    o_b, lse = _fox_fwd(proj, _key_bias_blocks(negc, fbk), q_off=fq_off, k_off=fk_off, v_off=fv_off,
                        bq=fbq, bk=fbk, name="fox_fwd")
    s_rest, g_rest = _exchange_wait(h_rest, o_b, name="gather_rest_wait")
    g_bs, g_bf, g_o, g_up, g_dn = [filled(g, s) for g, s in zip(g_rest, s_rest)]
    w_bs_t = g_bs.reshape(D, SWA_Q_W)
    w_bf_t = g_bf.reshape(D, FOX_W)
    w_o = g_o.reshape(D, D)
    w_up_t = g_up.reshape(DFF, D)
    w_dn = g_dn.reshape(DFF, D)
    ya, = _matmul(o_a, w_bs_t, mode="nt", name="mm_branch_swa", out_dtypes=[BF16], tm=tm, tn=td, tk=SWA_Q_W)
    gate_maps = [lambda i, j, k: (i, j), lambda i, j, k: (i, j), lambda i, j, k: (i, j + D // td)]

    def merge_epi(acc, ya_t, ga_t, gb_t):
        merged = _sigmoid(ga_t.astype(F32)) * ya_t.astype(F32) + _sigmoid(gb_t.astype(F32)) * acc
        return acc, merged

    yb, merged = _matmul(o_b, w_bf_t, mode="nt", name="mm_branch_fox", out_dtypes=[BF16, BF16],
                         tm=tm, tn=td, tk=FOX_W, extras=[ya, proj, proj], extra_maps=gate_maps,
                         epilogue=merge_epi)
    x_mid, = _matmul(merged, w_o, mode="nn", name="mm_out", out_dtypes=[F32], tm=tm, tn=td, tk=D,
                     extras=[x2d], epilogue=lambda acc, r: (acc + r,))
    h2 = _rms_fwd(x_mid, mlp_norm, name="rms2")
    u, = _matmul(h2, w_up_t, mode="nt", name="mm_up", out_dtypes=[BF16], tm=_pick(S, 2048), tn=tf, tk=D,
                 epilogue=lambda acc: (jnp.maximum(acc, 0.0),))
    x_fin, = _matmul(u, w_dn, mode="nn", name="mm_down", out_dtypes=[F32], tm=tm, tn=td, tk=_pick(DFF, 2048),
                     a_fn=_square_bf16, extras=[x_mid], epilogue=lambda acc, r: (acc + r,))

    dx3b, dg3, loss_part = _loss_head(x_fin, tgt, final_norm.reshape(1, D), name="loss_head")
    d_up, = _matmul(dx3b, w_dn, mode="nt", name="mm_d_act", out_dtypes=[BF16], tm=_pick(S, 2048), tn=tf, tk=D,
                    extras=[u], epilogue=lambda acc, ut: (acc * (2.0 * ut.astype(F32)),))
    tks = _pick(S, 2048)
    dw_dn, = _matmul(u, dx3b, mode="tn", name="mm_dw_down", out_dtypes=[BF16], tm=tf, tn=td, tk=tks,
                     a_fn=_square_bf16)
    dh2, = _matmul(d_up, w_up_t, mode="nn", name="mm_dh2", out_dtypes=[BF16], tm=tm, tn=td, tk=_pick(DFF, 2048))
    dw_up_t, = _matmul(d_up, h2, mode="tn", name="mm_dw_up", out_dtypes=[BF16], tm=tf, tn=td, tk=tks)
    h_s1, tok_s1 = _exchange_start([dw_up_t.reshape(N_DEV, DFF // N_DEV, D), dw_dn.reshape(N_DEV, DFF // N_DEV, D)],
                                   gather=False, name="scatter_mlp_start")
    dx2b, dg2 = _rms_bwd(dh2, x_mid, mlp_norm, dx3b, name="rms2_bwd", out_dtype=BF16, deps=[tok_s1])

    def gate_bwd_epi(dm, ya_t, yb_t, ga_t, gb_t):
        sa, sb = _sigmoid(ga_t.astype(F32)), _sigmoid(gb_t.astype(F32))
        return (dm * sa, dm * sb, dm * ya_t.astype(F32) * sa * (1.0 - sa), dm * yb_t.astype(F32) * sb * (1.0 - sb))

    gmaps = [lambda i, j, k: (i, j), lambda i, j, k: (i, j), lambda i, j, k: (i, j),
             lambda i, j, k: (i, j + D // td)]
    d_ya, d_yb, d_ga, d_gb = _matmul(dx2b, w_o, mode="nt", name="mm_d_merged", out_dtypes=[BF16] * 4,
                                     tm=tm, tn=td, tk=D, extras=[ya, yb, proj, proj], extra_maps=gmaps,
                                     epilogue=gate_bwd_epi)
    dw_o, = _matmul(merged, dx2b, mode="tn", name="mm_dw_out", out_dtypes=[BF16], tm=td, tn=td, tk=tks)
    d_oa, = _matmul(d_ya, w_bs_t, mode="nn", name="mm_d_oa", out_dtypes=[BF16], tm=tm, tn=SWA_Q_W, tk=D)
    d_ob, = _matmul(d_yb, w_bf_t, mode="nn", name="mm_d_ob", out_dtypes=[BF16], tm=tm, tn=FOX_W, tk=D)
    dw_bs_t, = _matmul(d_ya, o_a, mode="tn", name="mm_dw_bs", out_dtypes=[BF16], tm=td, tn=SWA_Q_W, tk=tks)
    dw_bf_t, = _matmul(d_yb, o_b, mode="tn", name="mm_dw_bf", out_dtypes=[BF16], tm=td, tn=FOX_W, tk=tks)
    h_s2, tok_s2 = _exchange_start([dw_bs_t.reshape(N_DEV, D // N_DEV, SWA_Q_W),
                                    dw_bf_t.reshape(N_DEV, D // N_DEV, FOX_W), dw_o.reshape(N_DEV, D // N_DEV, D)],
                                   gather=False, name="scatter_attn_start")
    def row_blocks_t(a):
        return a.reshape(S // bbq, bbq, FOX_W).transpose(0, 2, 1)

    d_fq, d_fk, d_fv, dcol4, drow4 = _fox_bwd(proj, _key_bias_blocks(negc, bbk), o_b, lse, d_ob,
                                              row_blocks_t(proj[:, fq_off:fq_off + FOX_W]), row_blocks_t(d_ob),
                                              q_off=fq_off, k_off=fk_off, v_off=fv_off, bq=bbq, bk=bbk,
                                              name="fox_bwd", deps=[tok_s2])
    dcol = dcol4.transpose(0, 2, 1, 3).reshape(FOX_HEADS, S)
    drow = drow4.transpose(0, 2, 1, 3).reshape(FOX_HEADS, S)
    dz_t, dbias_l = _fox_post(drow, dcol, z_t, bias_col, name="fox_post")
    d_aq, dk_c, dk_p, dv_c, dv_p, dsink_l = _swa_bwd(q_rope, k_rope, proj, sinks, d_oa, cos_t, sin_t, v_off=v_off,
                                                     name="swa_bwd")
    d_ak, d_av = _rope_bwd(dk_c, dk_p, dv_c, dv_p, cos_t, sin_t, name="rope_bwd")
    dz_pad = jnp.pad(dz_t.T.astype(BF16), ((0, 0), (0, FL_PAD - FOX_HEADS)))
    d_proj = jnp.concatenate([d_ga, d_gb, d_aq, d_ak, d_av, d_fq, d_fk, d_fv, dz_pad], axis=1)
    tkp = _pick(NP, 2304)
    dw_in_p, = _matmul(d_proj, h1, mode="tn", name="mm_dw_in", out_dtypes=[BF16], tm=_pick(NP, 512), tn=D, tk=tks)
    dw_in_t = jnp.concatenate([dw_in_p[q_off:q_off + QKV_W], dw_in_p[fl_off:fl_off + FOX_HEADS], dw_in_p[:q_off]],
                              axis=0)
    h_s3, tok_s3 = _exchange_start([dw_in_t.reshape(N_DEV, d_in // N_DEV, D)], gather=False,
                                   name="scatter_in_start")
    dh1, = _matmul(d_proj, w_in_p, mode="nn", name="mm_dh1", out_dtypes=[BF16], tm=tm, tn=td, tk=tkp, deps=[tok_s3])
    dx, dg1 = _rms_bwd(dh1, x2d, attn_norm, dx2b, name="rms1_bwd", out_dtype=F32)

    dbias = dbias_l[:, 0]
    dsinks = dsink_l[:, :, 0].reshape(-1)
    nsm = 3 * D + 2 * LANES
    tail = jnp.zeros((2 * LANES,), F32)
    small_g = jnp.concatenate([dg1[0], dg2[0], dg3[0],
                               tail.at[0:16].set(dbias).at[16:32].set(dsinks).at[32].set(loss_part[0, 0])])

    def pack(a_norm, b_norm, f_norm, bias, snk):
        return jnp.concatenate([a_norm[0], b_norm[0], f_norm,
                                tail.at[0:16].set(bias[0]).at[16:32].set(snk[0])]).reshape(1, nsm)

    small_stack, = _exchange([small_g.reshape(1, nsm)], gather=True, name="gather_small")
    u_sm = _adamw(pack(attn_norm, mlp_norm, final_norm, fox_f_bias, swa_sinks), small_stack,
                  pack(m_attn_norm, m_mlp_norm, m_final_norm, m_fox_f_bias, m_swa_sinks),
                  pack(v_attn_norm, v_mlp_norm, v_final_norm, v_fox_f_bias, v_swa_sinks),
                  name="adamw_small", stacked=True)
    loss = u_sm[0][0, 3 * D + 32]

    def own_of(src):
        return lax.dynamic_index_in_dim(src, me, 0, keepdims=False)

    def update_t(stack, src, w, m, v, nm):
        g = _sum8(stack, own_of(src), name="sum_" + nm).T
        return _adamw(w[0], g, m[0], v[0], name="adamw_" + nm, stacked=False)

    def update(stack, src, w, m, v, nm, transposed=False):
        return _adamw(w[0], stack, m[0], v[0], name="adamw_" + nm, stacked=True, own=own_of(src),
                      transposed=transposed)

    (s_up, s_dn), (r_up, r_dn) = _exchange_wait(h_s1, u_sm[1], name="scatter_mlp_wait")
    u_up = update(r_up, s_up, w_up, m_w_up, v_w_up, "w_up", transposed=True)
    u_dn = update(r_dn, s_dn, w_down, m_w_down, v_w_down, "w_down")
    (s_bs, s_bf, s_o), (r_bs, r_bf, r_o) = _exchange_wait(h_s2, u_dn[1], name="scatter_attn_wait")
    u_bs = update(r_bs, s_bs, w_branch_swa, m_w_branch_swa, v_w_branch_swa, "w_bs", transposed=True)
    u_bf = update(r_bf, s_bf, w_branch_fox, m_w_branch_fox, v_w_branch_fox, "w_bf", transposed=True)
    u_o = update(r_o, s_o, w_out, m_w_out, v_w_out, "w_out")
    (s_w_in,), (r_in,) = _exchange_wait(h_s3, u_o[1], name="scatter_in_wait")
    u_in = update_t(r_in, s_w_in, w_in, m_w_in, v_w_in, "w_in")

    def small(kind):
        a = u_sm[kind][0]
        return dict(attn_norm=a[0:D][None], mlp_norm=a[D:2 * D][None], final_norm=a[2 * D:3 * D],
                    fox_f_bias=a[3 * D:3 * D + 16][None], swa_sinks=a[3 * D + 16:3 * D + 32][None])

    big = dict(w_in=u_in, w_branch_swa=u_bs, w_branch_fox=u_bf, w_out=u_o, w_up=u_up, w_down=u_dn)
    order = ["attn_norm", "w_in", "fox_f_bias", "swa_sinks", "w_branch_swa", "w_branch_fox", "w_out", "mlp_norm",
             "w_up", "w_down", "final_norm"]
    outs = [loss, dx[None]]
    for kind in range(4):
        sm = small(kind)
        for nm in order:
            outs.append(big[nm][kind][None] if nm in big else sm[nm])
    return tuple(outs)
```

```python
import functools

import jax
import jax.numpy as jnp
from jax import lax
from jax.experimental import pallas as pl
from jax.experimental.pallas import tpu as pltpu

F32 = jnp.float32
BF16 = jnp.bfloat16

N_DEV = 8
HEAD_DIM = 64
SWA_Q_W = 1024
SWA_KV_W = 128
SWA_GROUP = 8
WINDOW = 128
FOX_W = 1024
FOX_HEADS = 16
QKV_W = SWA_Q_W + 2 * SWA_KV_W + 3 * FOX_W
FL_PAD = 256
ROPE_THETA = 10000.0
RMS_EPS = 1e-6
ATT_SCALE = 0.125
NEG = -1e30

ADAM_LR = 0.001
ADAM_B1 = 0.9
ADAM_B2 = 0.999
ADAM_EPS = 1e-08
ADAM_WD = 0.01
ADAM_STEP = 10

FOX_FWD_BLOCKS = (1024, 1024)
FOX_BWD_BLOCKS = (1024, 512)
FOX_FWD_PAIRS = 2

LANES = 128
VMEM_LIMIT = 56 * 1024 * 1024
STEP_BYTES = 12 * 1024 * 1024


def _cparams(*sem):
    return pltpu.CompilerParams(dimension_semantics=sem, vmem_limit_bytes=VMEM_LIMIT)


def _pick(dim, pref, align=LANES):
    best = None
    t = align
    while t <= min(dim, pref):
        if dim % t == 0:
            best = t
        t += align
    return best if best is not None else dim


_DIMS = {"nn": ((1,), (0,)), "nt": ((1,), (1,)), "tn": ((0,), (0,))}


_ANY = pl.BlockSpec(memory_space=pl.ANY)


def _matmul(a, b, *, mode, name, out_dtypes, tm, tn, tk, extras=(), extra_maps=None,
            a_fn=None, epilogue=None, deps=()):
    if mode == "nn":
        (M, K), (K2, N) = a.shape, b.shape
    elif mode == "nt":
        (M, K), (N, K2) = a.shape, b.shape
    else:
        (K, M), (K2, N) = a.shape, b.shape
    assert K == K2, (name, a.shape, b.shape)
    assert M % tm == 0 and N % tn == 0 and K % tk == 0, (name, M, N, K, tm, tn, tk)
    nk = K // tk
    ne, no = len(extras), len(out_dtypes)
    dims = (_DIMS[mode], ((), ()))

    def body(*refs):
        a_ref, b_ref = refs[0], refs[1]
        ex_refs = refs[2:2 + ne]
        out_refs = refs[2 + ne + len(deps):2 + ne + len(deps) + no]

        def finish(acc):
            res = (acc,) if epilogue is None else epilogue(acc, *[e[...] for e in ex_refs])
            for o_ref, r in zip(out_refs, res):
                o_ref[...] = r.astype(o_ref.dtype)

        def product():
            av = a_ref[...]
            if a_fn is not None:
                av = a_fn(av)
            return lax.dot_general(av, b_ref[...], dims, preferred_element_type=F32)

        if nk == 1:
            finish(product())
        else:
            acc_ref = refs[-1]
            k = pl.program_id(2)

            @pl.when(k == 0)
            def _():
                acc_ref[...] = jnp.zeros_like(acc_ref)

            acc_ref[...] += product()

            @pl.when(k == nk - 1)
            def _():
                finish(acc_ref[...])

    if mode == "tn":
        a_spec = pl.BlockSpec((tk, tm), lambda i, j, k: (k, i))
    else:
        a_spec = pl.BlockSpec((tm, tk), lambda i, j, k: (i, k))
    if mode == "nt":
        b_spec = pl.BlockSpec((tn, tk), lambda i, j, k: (j, k))
    else:
        b_spec = pl.BlockSpec((tk, tn), lambda i, j, k: (k, j))
    if extra_maps is None:
        extra_maps = [lambda i, j, k: (i, j)] * ne
    ex_specs = [pl.BlockSpec((tm, tn), m) for m in extra_maps]
    out_spec = [pl.BlockSpec((tm, tn), lambda i, j, k: (i, j)) for _ in range(no)]
    res = pl.pallas_call(
        body,
        name=name,
        grid=(M // tm, N // tn, nk),
        in_specs=[a_spec, b_spec] + ex_specs + [_ANY] * len(deps),
        out_specs=out_spec,
        out_shape=[jax.ShapeDtypeStruct((M, N), d) for d in out_dtypes],
        scratch_shapes=[pltpu.VMEM((tm, tn), F32)] if nk > 1 else [],
        compiler_params=_cparams("parallel", "parallel", "arbitrary"),
    )(a, b, *extras, *deps)
    return res


def _square_bf16(t):
    tf = t.astype(F32)
    return (tf * tf).astype(BF16)


def _sigmoid(g):
    return 1.0 / (1.0 + jnp.exp(-g))


def _rms_fwd(x, gain, *, name, deps=()):
    S, D = x.shape
    tr = _pick(S, 512, 8)

    def body(x_ref, g_ref, *rest):
        h_ref = rest[-1]
        xv = x_ref[...]
        r = lax.rsqrt(jnp.mean(xv * xv, axis=-1, keepdims=True) + RMS_EPS)
        h_ref[...] = (xv * r * g_ref[...]).astype(BF16)

    return pl.pallas_call(
        body, name=name, grid=(S // tr,),
        in_specs=[pl.BlockSpec((tr, D), lambda i: (i, 0)), pl.BlockSpec((1, D), lambda i: (0, 0))] + [_ANY] * len(deps),
        out_specs=pl.BlockSpec((tr, D), lambda i: (i, 0)),
        out_shape=jax.ShapeDtypeStruct((S, D), BF16),
        compiler_params=_cparams("parallel"),
    )(x, gain, *deps)


def _rms_bwd(dh, x, gain, dres, *, name, out_dtype, deps=()):
    S, D = x.shape
    tr = _pick(S, 256, 8)

    def body(dh_ref, x_ref, g_ref, dres_ref, *rest):
        outs = rest[len(deps):]
        dx_ref, dg_ref = outs[0], outs[-1]
        xv = x_ref[...]
        r = lax.rsqrt(jnp.mean(xv * xv, axis=-1, keepdims=True) + RMS_EPS)
        xh = xv * r
        dhv = dh_ref[...].astype(F32)
        t = dhv * g_ref[...]
        dx = r * (t - xh * jnp.mean(t * xh, axis=-1, keepdims=True)) + dres_ref[...].astype(F32)
        dx_ref[...] = dx.astype(out_dtype)
        part = jnp.sum(dhv * xh, axis=0, keepdims=True)

        @pl.when(pl.program_id(0) == 0)
        def _():
            dg_ref[...] = part

        @pl.when(pl.program_id(0) > 0)
        def _():
            dg_ref[...] += part

    row = pl.BlockSpec((tr, D), lambda i: (i, 0))
    vec = pl.BlockSpec((1, D), lambda i: (0, 0))
    return pl.pallas_call(
        body, name=name, grid=(S // tr,),
        in_specs=[row, row, vec, row] + [_ANY] * len(deps), out_specs=[row, vec],
        out_shape=[jax.ShapeDtypeStruct((S, D), out_dtype), jax.ShapeDtypeStruct((1, D), F32)],
        compiler_params=_cparams("arbitrary"),
    )(dh, x, gain, dres, *deps)


def _loss_head(x3, target, gain, *, name):
    S, D = x3.shape
    tr = _pick(S, 256, 8)

    def body(x_ref, t_ref, g_ref, dxb_ref, dg_ref, loss_ref):
        xv = x_ref[...]
        r = lax.rsqrt(jnp.mean(xv * xv, axis=-1, keepdims=True) + RMS_EPS)
        xh = xv * r
        gv = g_ref[...]
        err = xh * gv - t_ref[...]
        lpart = jnp.zeros((1, LANES), F32) + (0.5 / D) * jnp.sum(err * err)
        dy = err * (1.0 / D)
        t = dy * gv
        dx = r * (t - xh * jnp.mean(t * xh, axis=-1, keepdims=True))
        dxb_ref[...] = dx.astype(BF16)
        part = jnp.sum(dy * xh, axis=0, keepdims=True)

        @pl.when(pl.program_id(0) == 0)
        def _():
            dg_ref[...] = part
            loss_ref[...] = lpart

        @pl.when(pl.program_id(0) > 0)
        def _():
            dg_ref[...] += part
            loss_ref[...] += lpart

    row = pl.BlockSpec((tr, D), lambda i: (i, 0))
    vec = pl.BlockSpec((1, D), lambda i: (0, 0))
    return pl.pallas_call(
        body, name=name, grid=(S // tr,),
        in_specs=[row, row, vec],
        out_specs=[row, vec, pl.BlockSpec((1, LANES), lambda i: (0, 0))],
        out_shape=[jax.ShapeDtypeStruct((S, D), BF16),
                   jax.ShapeDtypeStruct((1, D), F32), jax.ShapeDtypeStruct((1, LANES), F32)],
        compiler_params=_cparams("arbitrary"),
    )(x3, target, gain)


def _rope_tables(pos_col, invf, *, name, deps=()):
    S = pos_col.shape[0]
    tr = _pick(S, 512, 8)

    def body(p_ref, f_ref, *rest):
        cos_ref, sin_ref = rest[len(deps):]
        ang = p_ref[...].astype(F32) * f_ref[...]
        lane = lax.broadcasted_iota(jnp.int32, (1, LANES), 1)
        first = (lane % HEAD_DIM) < HEAD_DIM // 2
        sn = jnp.sin(ang)
        cos_ref[...] = jnp.cos(ang)
        sin_ref[...] = jnp.where(first, -sn, sn)

    return pl.pallas_call(
        body, name=name, grid=(S // tr,),
        in_specs=[pl.BlockSpec((tr, 1), lambda i: (i, 0)), pl.BlockSpec((1, LANES), lambda i: (0, 0))]
        + [_ANY] * len(deps),
        out_specs=[pl.BlockSpec((tr, LANES), lambda i: (i, 0))] * 2,
        out_shape=[jax.ShapeDtypeStruct((S, LANES), F32)] * 2,
        compiler_params=_cparams("parallel"),
    )(pos_col, invf, *deps)


def _swap_halves(t):
    lane = lax.broadcasted_iota(jnp.int32, (1, LANES), 1)
    first = (lane % HEAD_DIM) < HEAD_DIM // 2
    return jnp.where(first, pltpu.roll(t, LANES - HEAD_DIM // 2, 1), pltpu.roll(t, HEAD_DIM // 2, 1))


def _rope_fwd(proj, cos_t, sin_t, *, q_off, k_off, name):
    S = proj.shape[0]
    tr = _pick(S, 256, 8)
    nqb = SWA_Q_W // LANES

    def body(q_ref, k_ref, c_ref, s_ref, qo_ref, ko_ref):
        cv, sv = c_ref[...], s_ref[...]
        for b in range(nqb):
            t = q_ref[:, b * LANES:(b + 1) * LANES].astype(F32)
            qo_ref[:, b * LANES:(b + 1) * LANES] = (t * cv + _swap_halves(t) * sv).astype(BF16)
        t = k_ref[...].astype(F32)
        ko_ref[...] = (t * cv + _swap_halves(t) * sv).astype(BF16)

    tab = pl.BlockSpec((tr, LANES), lambda i: (i, 0))
    return pl.pallas_call(
        body, name=name, grid=(S // tr,),
        in_specs=[pl.BlockSpec((tr, SWA_Q_W), lambda i: (i, q_off // SWA_Q_W)),
                  pl.BlockSpec((tr, LANES), lambda i: (i, k_off // LANES)), tab, tab],
        out_specs=[pl.BlockSpec((tr, SWA_Q_W), lambda i: (i, 0)), tab],
        out_shape=[jax.ShapeDtypeStruct((S, SWA_Q_W), BF16), jax.ShapeDtypeStruct((S, LANES), BF16)],
        compiler_params=_cparams("parallel"),
    )(proj, proj, cos_t, sin_t)


def _rope_bwd(dk_cur, dk_prev, dv_cur, dv_prev, cos_t, sin_t, *, name):
    S = dk_cur.shape[1]
    tr = WINDOW
    nb = S // tr

    def body(kc_ref, kp_ref, vc_ref, vp_ref, c_ref, s_ref, dko_ref, dvo_ref):
        cv, sv = c_ref[...], s_ref[...]
        has_next = (pl.program_id(0) + 1 < nb).astype(F32)
        d = kc_ref[0] + kc_ref[1] + has_next * (kp_ref[0] + kp_ref[1])
        dko_ref[...] = (d * cv + _swap_halves(d * sv)).astype(BF16)
        dvo_ref[...] = (vc_ref[0] + vc_ref[1] + has_next * (vp_ref[0] + vp_ref[1])).astype(BF16)

    tab = pl.BlockSpec((tr, LANES), lambda i: (i, 0))
    cur = pl.BlockSpec((2, tr, LANES), lambda i: (0, i, 0))
    nxt = pl.BlockSpec((2, tr, LANES), lambda i: (0, jnp.minimum(i + 1, nb - 1), 0))
    return pl.pallas_call(
        body, name=name, grid=(nb,),
        in_specs=[cur, nxt, cur, nxt, tab, tab],
        out_specs=[tab, tab],
        out_shape=[jax.ShapeDtypeStruct((S, LANES), BF16), jax.ShapeDtypeStruct((S, LANES), BF16)],
        compiler_params=_cparams("parallel"),
    )(dk_cur, dk_prev, dv_cur, dv_prev, cos_t, sin_t)


def _dot_nt(a, b):
    return lax.dot_general(a, b, (((1,), (1,)), ((), ())), preferred_element_type=F32)


def _dot_tn(a, b):
    return lax.dot_general(a, b, (((0,), (0,)), ((), ())), preferred_element_type=F32)


def _dot_nn(a, b):
    return lax.dot_general(a, b, (((1,), (0,)), ((), ())), preferred_element_type=F32)


def _roll_half(t):
    return pltpu.roll(t.astype(F32), HEAD_DIM, 1).astype(t.dtype)


SWA_STACK = SWA_GROUP // 2


def _swa_common(hk, n, kp_ref, kc_ref, vp_ref, vc_ref):
    k2 = jnp.concatenate([kp_ref[...], kc_ref[...]], axis=0)
    v2 = jnp.concatenate([vp_ref[...], vc_ref[...]], axis=0)
    k_sw, v_sw = _roll_half(k2), _roll_half(v2)
    rows = SWA_STACK * WINDOW
    row = lax.broadcasted_iota(jnp.int32, (rows, 2 * WINDOW), 0) % WINDOW
    col = lax.broadcasted_iota(jnp.int32, (rows, 2 * WINDOW), 1)
    diff = row + WINDOW - col
    allowed = (diff >= 0) & (diff < WINDOW) & ((col >= WINDOW) | (n > 0))
    lane = lax.broadcasted_iota(jnp.int32, (1, LANES), 1)
    half = [lane < HEAD_DIM, lane >= HEAD_DIM]
    kk = [jnp.where(hk == a, k2, k_sw) for a in range(2)]
    vv = [jnp.where(hk == a, v2, v_sw) for a in range(2)]
    return allowed, half, kk, vv


def _swa_stack(ref, mask, scale=None):
    parts = []
    for t in range(SWA_STACK):
        blk = ref[:, t * LANES:(t + 1) * LANES]
        if scale is not None:
            blk = blk * jnp.asarray(scale, blk.dtype)
        parts.append(jnp.where(mask, blk, jnp.zeros_like(blk)))
    return jnp.concatenate(parts, axis=0)


def _swa_sink_column(sink_ref, hk, a):
    blk = lax.broadcasted_iota(jnp.int32, (SWA_STACK * WINDOW, 1), 0) // WINDOW
    col = jnp.zeros((SWA_STACK * WINDOW, 1), F32)
    for t in range(SWA_STACK):
        col = jnp.where(blk == t, sink_ref[hk * SWA_GROUP + 2 * t + a], col)
    return col


def _swa_probs(qm, kk, allowed, sink):
    s = jnp.where(allowed, _dot_nt(qm, kk), NEG)
    m = jnp.maximum(jnp.max(s, axis=1, keepdims=True), sink)
    e = jnp.exp(s - m)
    es = jnp.exp(sink - m)
    inv = 1.0 / (jnp.sum(e, axis=1, keepdims=True) + es)
    return e * inv, es * inv


def _swa_fwd(q_rope, k_rope, proj, sinks, *, v_off, name):
    S = q_rope.shape[0]
    nb = S // WINDOW
    gw = SWA_GROUP * HEAD_DIM

    def body(sink_ref, q_ref, kp_ref, kc_ref, vp_ref, vc_ref, o_ref):
        hk, n = pl.program_id(0), pl.program_id(1)
        allowed, half, kk, vv = _swa_common(hk, n, kp_ref, kc_ref, vp_ref, vc_ref)
        outs = []
        for a in range(2):
            qm = _swa_stack(q_ref, half[a], ATT_SCALE)
            p, _ = _swa_probs(qm, kk[a], allowed, _swa_sink_column(sink_ref, hk, a))
            outs.append(_dot_nn(p.astype(BF16), vv[a]))
        for t in range(SWA_STACK):
            rows = slice(t * WINDOW, (t + 1) * WINDOW)
            o_ref[:, t * LANES:(t + 1) * LANES] = jnp.where(half[0], outs[0][rows], outs[1][rows]).astype(BF16)

    prev = lambda hk, n: (jnp.maximum(n - 1, 0), 0)
    cur = lambda hk, n: (n, 0)
    vprev = lambda hk, n: (jnp.maximum(n - 1, 0), v_off // LANES)
    vcur = lambda hk, n: (n, v_off // LANES)
    blk = lambda m: pl.BlockSpec((WINDOW, LANES), m)
    return pl.pallas_call(
        body, name=name, grid=(2, nb),
        in_specs=[pl.BlockSpec(memory_space=pltpu.SMEM),
                  pl.BlockSpec((WINDOW, gw), lambda hk, n: (n, hk)),
                  blk(prev), blk(cur), blk(vprev), blk(vcur)],
        out_specs=pl.BlockSpec((WINDOW, gw), lambda hk, n: (n, hk)),
        out_shape=jax.ShapeDtypeStruct((S, SWA_Q_W), BF16),
        compiler_params=_cparams("parallel", "parallel"),
    )(sinks, q_rope, k_rope, k_rope, proj, proj)


def _swa_bwd(q_rope, k_rope, proj, sinks, d_o, cos_t, sin_t, *, v_off, name):
    S = q_rope.shape[0]
    nb = S // WINDOW
    gw = SWA_GROUP * HEAD_DIM

    def body(sink_ref, q_ref, kp_ref, kc_ref, vp_ref, vc_ref, do_ref, c_ref, s_ref,
             dq_ref, dkc_ref, dkp_ref, dvc_ref, dvp_ref, dsink_ref):
        hk, n = pl.program_id(0), pl.program_id(1)
        allowed, half, kk, vv = _swa_common(hk, n, kp_ref, kc_ref, vp_ref, vc_ref)
        dk_acc = jnp.zeros((2 * WINDOW, LANES), F32)
        dv_acc = jnp.zeros((2 * WINDOW, LANES), F32)
        srow = lax.broadcasted_iota(jnp.int32, (SWA_GROUP, LANES), 0)
        dsink = jnp.zeros((SWA_GROUP, LANES), F32)
        dqs = []
        for a in range(2):
            qm = _swa_stack(q_ref, half[a], ATT_SCALE)
            dom = _swa_stack(do_ref, half[a])
            p, psink = _swa_probs(qm, kk[a], allowed, _swa_sink_column(sink_ref, hk, a))
            dp = _dot_nt(dom, vv[a])
            delta = jnp.sum(p * dp, axis=1, keepdims=True)
            ds = (p * (dp - delta)).astype(BF16)
            dsk = psink * delta
            for t in range(SWA_STACK):
                dsink = dsink + jnp.where(srow == 2 * t + a, -jnp.sum(dsk[t * WINDOW:(t + 1) * WINDOW]), 0.0)
            dqs.append(_dot_nn(ds, kk[a]) * ATT_SCALE)
            dk_acc = dk_acc + _dot_tn(ds, qm)
            dv_acc = dv_acc + _dot_tn(p.astype(BF16), dom)
        cv, sv = c_ref[...], s_ref[...]
        for t in range(SWA_STACK):
            rows = slice(t * WINDOW, (t + 1) * WINDOW)
            d = jnp.where(half[0], dqs[0][rows], dqs[1][rows])
            dq_ref[:, t * LANES:(t + 1) * LANES] = (d * cv + _swap_halves(d * sv)).astype(BF16)
        lane = lax.broadcasted_iota(jnp.int32, (1, LANES), 1)
        mine = (lane >= HEAD_DIM) == (hk == 1)
        dk_t = jnp.where(mine, dk_acc + pltpu.roll(dk_acc, HEAD_DIM, 1), 0.0)
        dv_t = jnp.where(mine, dv_acc + pltpu.roll(dv_acc, HEAD_DIM, 1), 0.0)
        dkp_ref[0] = dk_t[:WINDOW]
        dkc_ref[0] = dk_t[WINDOW:]
        dvp_ref[0] = dv_t[:WINDOW]
        dvc_ref[0] = dv_t[WINDOW:]

        @pl.when(n == 0)
        def _():
            dsink_ref[0] = dsink

        @pl.when(n > 0)
        def _():
            dsink_ref[0] += dsink

    prev = lambda hk, n: (jnp.maximum(n - 1, 0), 0)
    cur = lambda hk, n: (n, 0)
    vprev = lambda hk, n: (jnp.maximum(n - 1, 0), v_off // LANES)
    vcur = lambda hk, n: (n, v_off // LANES)
    blk = lambda m: pl.BlockSpec((WINDOW, LANES), m)
    qblk = pl.BlockSpec((WINDOW, gw), lambda hk, n: (n, hk))
    part = pl.BlockSpec((1, WINDOW, LANES), lambda hk, n: (hk, n, 0))
    part_shape = jax.ShapeDtypeStruct((2, S, LANES), F32)
    return pl.pallas_call(
        body, name=name, grid=(2, nb),
        in_specs=[pl.BlockSpec(memory_space=pltpu.SMEM), qblk, blk(prev), blk(cur), blk(vprev), blk(vcur), qblk,
                  blk(cur), blk(cur)],
        out_specs=[qblk, part, part, part, part,
                   pl.BlockSpec((1, SWA_GROUP, LANES), lambda hk, n: (hk, 0, 0))],
        out_shape=[jax.ShapeDtypeStruct((S, SWA_Q_W), BF16), part_shape, part_shape, part_shape, part_shape,
                   jax.ShapeDtypeStruct((2, SWA_GROUP, LANES), F32)],
        compiler_params=_cparams("parallel", "arbitrary"),
    )(sinks, q_rope, k_rope, k_rope, proj, proj, d_o, cos_t, sin_t)


def _fox_prep(z_t, bias_col, *, name):
    H, S = z_t.shape
    tb = _pick(S, 512)

    def body(z_ref, b_ref, o_ref, carry_ref):
        @pl.when(pl.program_id(0) == 0)
        def _():
            carry_ref[...] = jnp.zeros_like(carry_ref)

        zz = z_ref[...] + b_ref[...]
        t = jnp.exp(-jnp.abs(zz))
        log1p = jnp.where(t < 1e-2, t * (1.0 - t * (0.5 - t * (1.0 / 3.0))), jnp.log(1.0 + t))
        logf = jnp.minimum(zz, 0.0) - log1p
        r = lax.broadcasted_iota(jnp.int32, (tb, tb), 0)
        c = lax.broadcasted_iota(jnp.int32, (tb, tb), 1)
        tri = (r <= c).astype(BF16)
        hi = logf.astype(BF16)
        r1 = logf - hi.astype(F32)
        mid = r1.astype(BF16)
        lo = (r1 - mid.astype(F32)).astype(BF16)
        cs = _dot_nn(hi, tri) + _dot_nn(mid, tri) + _dot_nn(lo, tri) + carry_ref[:, 0:1]
        o_ref[...] = -cs
        carry_ref[...] = jnp.zeros_like(carry_ref) + cs[:, tb - 1:tb]

    return pl.pallas_call(
        body, name=name, grid=(S // tb,),
        in_specs=[pl.BlockSpec((H, tb), lambda i: (0, i)), pl.BlockSpec((H, 1), lambda i: (0, 0))],
        out_specs=pl.BlockSpec((H, tb), lambda i: (0, i)),
        out_shape=jax.ShapeDtypeStruct((H, S), F32),
        scratch_shapes=[pltpu.VMEM((H, LANES), F32)],
        compiler_params=_cparams("arbitrary"),
    )(z_t, bias_col)


def _fox_post(drow, dcol, z_t, bias_col, *, name):
    H, S = z_t.shape
    tb = _pick(S, 512)
    nb = S // tb

    def body(dr_ref, d_ref, z_ref, b_ref, dz_ref, db_ref, carry_ref):
        @pl.when(pl.program_id(0) == 0)
        def _():
            carry_ref[...] = jnp.zeros_like(carry_ref)
            db_ref[...] = jnp.zeros_like(db_ref)

        dc = dr_ref[...] - d_ref[...]
        r = lax.broadcasted_iota(jnp.int32, (tb, tb), 0)
        c = lax.broadcasted_iota(jnp.int32, (tb, tb), 1)
        tri = (r >= c).astype(BF16)
        hi = dc.astype(BF16)
        r1 = dc - hi.astype(F32)
        mid = r1.astype(BF16)
        lo = (r1 - mid.astype(F32)).astype(BF16)
        dlogf = _dot_nn(hi, tri) + _dot_nn(mid, tri) + _dot_nn(lo, tri) + carry_ref[:, 0:1]
        carry_ref[...] = jnp.zeros_like(carry_ref) + dlogf[:, 0:1]
        dz = dlogf * _sigmoid(-(z_ref[...] + b_ref[...]))
        dz_ref[...] = dz
        db_ref[...] += jnp.sum(dz, axis=1, keepdims=True)

    rev = lambda i: (0, nb - 1 - i)
    return pl.pallas_call(
        body, name=name, grid=(nb,),
        in_specs=[pl.BlockSpec((H, tb), rev), pl.BlockSpec((H, tb), rev), pl.BlockSpec((H, tb), rev),
                  pl.BlockSpec((H, 1), lambda i: (0, 0))],
        out_specs=[pl.BlockSpec((H, tb), rev), pl.BlockSpec((H, LANES), lambda i: (0, 0))],
        out_shape=[jax.ShapeDtypeStruct((H, S), F32), jax.ShapeDtypeStruct((H, LANES), F32)],
        scratch_shapes=[pltpu.VMEM((H, LANES), F32)],
        compiler_params=_cparams("arbitrary"),
    )(drow, dcol, z_t, bias_col)


def _fox_blocks(S):
    cap = max(LANES, S // 4)
    return (min(FOX_FWD_BLOCKS[0], cap), min(FOX_FWD_BLOCKS[1], cap)), \
           (min(FOX_BWD_BLOCKS[0], cap), min(FOX_BWD_BLOCKS[1], cap))


def _key_bias_blocks(negc, bk):
    H, S = negc.shape
    return negc.reshape(H // 2, 2, S // bk, bk).transpose(0, 2, 1, 3)


def _fox_fwd(proj, negc4, *, q_off, k_off, v_off, bq, bk, name):
    S = proj.shape[0]
    nq, nk = S // bq, S // bk
    npair = FOX_HEADS // 2
    assert bq % bk == 0 or bk % bq == 0
    nmask = max(1, bq // bk)

    gp = FOX_FWD_PAIRS
    gw = gp * LANES
    assert q_off % gw == 0 and k_off % gw == 0 and v_off % gw == 0 and npair % gp == 0

    def body(q_ref, k_ref, v_ref, nc_ref, o_ref, lse_ref):
        i = pl.program_id(1)
        lane = lax.broadcasted_iota(jnp.int32, (1, LANES), 1)
        half = [lane < HEAD_DIM, lane >= HEAD_DIM]
        qh = []
        for g in range(gp):
            q2 = q_ref[:, g * LANES:(g + 1) * LANES] * jnp.asarray(ATT_SCALE, BF16)
            qh += [jnp.where(half[h], q2, jnp.zeros_like(q2)) for h in range(2)]
        row = lax.broadcasted_iota(jnp.int32, (bq, bk), 0)
        col = lax.broadcasted_iota(jnp.int32, (bq, bk), 1)
        rel = row - col
        nfull = (i * bq) // bk

        spare = [HEAD_DIM, 0]
        ones_lane = [lane == spare[h] for h in range(2)]

        def step(j, carry, masked):
            start = pl.multiple_of(j * bk, bk)
            new = []
            for g in range(gp):
                ks = k_ref[pl.ds(start, bk), g * LANES:(g + 1) * LANES]
                vs = v_ref[pl.ds(start, bk), g * LANES:(g + 1) * LANES]
                nb = nc_ref[g, j]
                for h in range(2):
                    m, acc = carry[4 * g + 2 * h:4 * g + 2 * h + 2]
                    vh = jnp.where(half[h], vs, jnp.where(ones_lane[h], jnp.ones_like(vs), jnp.zeros_like(vs)))
                    qs, bias = qh[2 * g + h], nb[h:h + 1, :]

                    def update(m, acc, rows, keys):
                        s = _dot_nt(qs[rows], ks[keys]) + bias[:, keys]
                        if masked:
                            s = jnp.where(rel[rows, keys] >= j * bk - i * bq, s, NEG)
                        m_new = jnp.maximum(m[rows], jnp.max(s, axis=1, keepdims=True))
                        p = jnp.exp(s - m_new).astype(BF16)
                        return m_new, jnp.exp(m[rows] - m_new) * acc[rows] + _dot_nn(p, vh[keys])

                    if masked and bq == bk:
                        top, bot, everything = slice(0, bq // 2), slice(bq // 2, bq), slice(0, bk)
                        m_t, acc_t = update(m, acc, top, top)
                        m_b, acc_b = update(m, acc, bot, everything)
                        new += [jnp.concatenate([m_t, m_b], axis=0), jnp.concatenate([acc_t, acc_b], axis=0)]
                    else:
                        new += list(update(m, acc, slice(0, bq), slice(0, bk)))
            return tuple(new)

        init = (jnp.full((bq, 1), NEG, F32), jnp.zeros((bq, LANES), F32)) * (2 * gp)
        carry = lax.fori_loop(0, nfull, lambda j, c: step(j, c, False), init)
        for t in range(nmask):
            carry = step(nfull + t, carry, True)
        for g in range(gp):
            outs, lses = [], []
            for h in range(2):
                m, acc = carry[4 * g + 2 * h:4 * g + 2 * h + 2]
                l = acc[:, spare[h]:spare[h] + 1]
                outs.append(acc * (1.0 / l))
                lses.append(m + jnp.log(l))
            o_ref[:, g * LANES:(g + 1) * LANES] = jnp.where(half[0], outs[0], outs[1]).astype(BF16)
            lse_ref[g] = jnp.where(half[0], lses[0], lses[1])

    seq = lambda off: pl.BlockSpec((S, gw), lambda hp, i: (0, off // gw + hp))
    return pl.pallas_call(
        body, name=name, grid=(npair // gp, nq),
        in_specs=[pl.BlockSpec((bq, gw), lambda hp, i: (i, q_off // gw + hp)), seq(k_off), seq(v_off),
                  pl.BlockSpec((gp, nk, 2, bk), lambda hp, i: (hp, 0, 0, 0))],
        out_specs=[pl.BlockSpec((bq, gw), lambda hp, i: (i, hp)),
                   pl.BlockSpec((gp, bq, LANES), lambda hp, i: (hp, i, 0))],
        out_shape=[jax.ShapeDtypeStruct((S, FOX_W), BF16), jax.ShapeDtypeStruct((npair, S, LANES), F32)],
        compiler_params=_cparams("parallel", "parallel"),
    )(proj, proj, proj, negc4)


def _fox_bwd(proj, negc4, o, lse, d_o, q_t, do_t, *, q_off, k_off, v_off, bq, bk, name, deps=()):
    S = proj.shape[0]
    nq, nk = S // bq, S // bk
    npair = FOX_HEADS // 2
    assert bq % bk == 0 or bk % bq == 0
    nmask = max(1, bk // bq)

    def body(q_ref, k_ref, v_ref, nc_ref, o_ref, lse_ref, do_ref, qt_ref, dot_ref, *rest):
        dqo_ref, dk_ref, dv_ref, dn_ref, dr_ref, delta_ref, rs_ref, dq_ref = rest[len(deps):]
        j = pl.program_id(1)
        lane = lax.broadcasted_iota(jnp.int32, (1, LANES), 1)
        half = [lane < HEAD_DIM, lane >= HEAD_DIM]
        spare = [HEAD_DIM, 0]
        ones_lane = [lane == spare[h] for h in range(2)]
        srow = lax.broadcasted_iota(jnp.int32, (LANES, 1), 0)
        rhalf = [srow < HEAD_DIM, srow >= HEAD_DIM]
        ones_row = [srow == spare[h] for h in range(2)]
        k2, v2 = k_ref[...], v_ref[...]
        one_k = jnp.ones_like(k2)
        kh = [jnp.where(half[h], k2, jnp.where(ones_lane[h], one_k, jnp.zeros_like(k2))) for h in range(2)]
        nb = nc_ref[0, 0]
        row = lax.broadcasted_iota(jnp.int32, (bq, bk), 0)
        col = lax.broadcasted_iota(jnp.int32, (bq, bk), 1)
        rel = row - col
        i_first = (j * bk) // bq

        @pl.when(j == 0)
        def _():
            dq_ref[...] = jnp.zeros_like(dq_ref)
            rs_ref[...] = jnp.zeros_like(rs_ref)
            for b in range(nq):
                prod = do_ref[b * bq:(b + 1) * bq, :].astype(F32) * o_ref[b * bq:(b + 1) * bq, :].astype(F32)
                d0 = jnp.sum(jnp.where(half[0], prod, 0.0), axis=1, keepdims=True)
                d1 = jnp.sum(jnp.where(half[1], prod, 0.0), axis=1, keepdims=True)
                delta_ref[b * bq:(b + 1) * bq, :] = jnp.where(half[0], d0, d1)

        def step(i, carry, masked, r0=0):
            dkt_a, dkt_b, dvt = carry
            dkts = [dkt_a, dkt_b]
            nr = bq - r0
            start = pl.multiple_of(i * bq + r0, LANES)
            q2 = q_ref[pl.ds(start, nr), :] * jnp.asarray(ATT_SCALE, BF16)
            do2 = do_ref[pl.ds(start, nr), :]
            qt = qt_ref[i][:, r0:] * jnp.asarray(ATT_SCALE, BF16)
            dot = dot_ref[i][:, r0:]
            lse2 = lse_ref[0, pl.ds(start, nr), :]
            del2 = delta_ref[pl.ds(start, nr), :]
            dqf = []
            for h in range(2):
                qm = jnp.where(half[h], q2, jnp.zeros_like(q2))
                dom = jnp.where(half[h], do2, jnp.zeros_like(do2))
                qtm = jnp.where(rhalf[h], qt, jnp.where(ones_row[h], jnp.ones_like(qt), jnp.zeros_like(qt)))
                dotm = jnp.where(rhalf[h], dot, jnp.zeros_like(dot))
                c0 = h * HEAD_DIM
                p = jnp.exp(_dot_nt(qm, k2) + nb[h:h + 1, :] - lse2[:, c0:c0 + 1])
                if masked:
                    p = jnp.where(rel[r0:] >= j * bk - i * bq, p, 0.0)
                dp = _dot_nt(dom, v2)
                dsb = (p * (dp - del2[:, c0:c0 + 1])).astype(BF16)
                dvt = dvt + _dot_nn(dotm, p.astype(BF16))
                dkts[h] = dkts[h] + _dot_nn(qtm, dsb)
                dqf.append(_dot_nn(dsb, kh[h]))
            dq_ref[pl.ds(start, nr), :] += jnp.where(half[0], dqf[0], dqf[1]) * ATT_SCALE
            rs_ref[pl.ds(start, nr), :] += jnp.where(ones_lane[0], dqf[0], jnp.where(ones_lane[1], dqf[1], 0.0))
            return dkts[0], dkts[1], dvt

        zero = jnp.zeros((LANES, bk), F32)
        carry = (zero, zero, zero)
        if bq > bk:
            sp = j % (bq // bk)
            carry = lax.switch(sp, [functools.partial(step, i_first, masked=True, r0=s * bk)
                                    for s in range(bq // bk)], carry)
        else:
            for t in range(nmask):
                carry = step(i_first + t, carry, True)
        dkt_a, dkt_b, dvt = lax.fori_loop(i_first + nmask, nq, lambda i, c: step(i, c, False), carry)
        dk_ref[...] = jnp.where(rhalf[0], dkt_a, dkt_b).T.astype(BF16)
        dv_ref[...] = dvt.T.astype(BF16)
        dn_ref[0, 0] = jnp.concatenate([dkt_a[spare[0]:spare[0] + 1], dkt_b[spare[1]:spare[1] + 1]], axis=0)

        @pl.when(j == nk - 1)
        def _():
            dqo_ref[...] = dq_ref[...].astype(BF16)
            for b in range(nq):
                t = rs_ref[b * bq:(b + 1) * bq, :].T
                dr_ref[0, b] = jnp.concatenate([t[spare[0]:spare[0] + 1], t[spare[1]:spare[1] + 1]], axis=0)

    once = pl.Buffered(1)
    seq = lambda off: pl.BlockSpec((S, LANES), lambda hp, j: (0, off // LANES + hp), pipeline_mode=once)
    blk = lambda off: pl.BlockSpec((bk, LANES), lambda hp, j: (j, off // LANES + hp))
    nc = pl.BlockSpec((1, 1, 2, bk), lambda hp, j: (hp, j, 0, 0))
    tsp = pl.BlockSpec((nq, LANES, bq), lambda hp, j: (0, hp, 0), pipeline_mode=once)
    return pl.pallas_call(
        body, name=name, grid=(npair, nk),
        in_specs=[seq(q_off), blk(k_off), blk(v_off), nc, seq(0),
                  pl.BlockSpec((1, S, LANES), lambda hp, j: (hp, 0, 0), pipeline_mode=once), seq(0),
                  tsp, tsp] + [_ANY] * len(deps),
        out_specs=[pl.BlockSpec((S, LANES), lambda hp, j: (0, hp)), blk(0), blk(0), nc,
                   pl.BlockSpec((1, nq, 2, bq), lambda hp, j: (hp, 0, 0, 0))],
        out_shape=[jax.ShapeDtypeStruct((S, FOX_W), BF16), jax.ShapeDtypeStruct((S, FOX_W), BF16),
                   jax.ShapeDtypeStruct((S, FOX_W), BF16), jax.ShapeDtypeStruct((npair, nk, 2, bk), F32),
                   jax.ShapeDtypeStruct((npair, nq, 2, bq), F32)],
        scratch_shapes=[pltpu.VMEM((S, LANES), F32), pltpu.VMEM((S, LANES), F32), pltpu.VMEM((S, LANES), F32)],
        compiler_params=_cparams("parallel", "arbitrary"),
    )(proj, proj, proj, negc4, o, lse, d_o, q_t, do_t, *deps)


def _exchange(arrs, *, gather, name):
    n = len(arrs)
    npeer = N_DEV - 1

    def body(*refs):
        ins, outs = refs[:n], refs[n:2 * n]
        send_sems, recv_sems, loc_sems = refs[2 * n:]
        x, y, c = lax.axis_index("x"), lax.axis_index("y"), lax.axis_index("c")
        me = 4 * x + 2 * y + c
        peers = []
        for k in range(1, N_DEV):
            px = 1 - x if k & 4 else x
            py = 1 - y if k & 2 else y
            pc = 1 - c if k & 1 else c
            peers.append(((px, py, pc), 4 * px + 2 * py + pc))

        def remote(w, k):
            dev, idx = peers[k]
            src = ins[w] if gather else ins[w].at[idx]
            return pltpu.make_async_remote_copy(
                src_ref=src, dst_ref=outs[w].at[me],
                send_sem=send_sems.at[w * npeer + k], recv_sem=recv_sems.at[w * npeer + k],
                device_id=dev, device_id_type=pl.DeviceIdType.MESH)

        def arrival(w, k):
            dev, idx = peers[k]
            src = ins[w] if gather else ins[w].at[idx]
            return pltpu.make_async_remote_copy(
                src_ref=src, dst_ref=outs[w].at[idx],
                send_sem=send_sems.at[w * npeer + k], recv_sem=recv_sems.at[w * npeer + k],
                device_id=dev, device_id_type=pl.DeviceIdType.MESH)

        local = []
        for w in range(n):
            for k in range(npeer):
                remote(w, k).start()
            cp = pltpu.make_async_copy(ins[w] if gather else ins[w].at[me], outs[w].at[me], loc_sems.at[w])
            cp.start()
            local.append(cp)
        for w in range(n):
            for k in range(npeer):
                arrival(w, k).wait_recv()
        for w in range(n):
            for k in range(npeer):
                remote(w, k).wait_send()
            local[w].wait()

    hbm = pl.BlockSpec(memory_space=pl.ANY)
    out_shape = [jax.ShapeDtypeStruct((N_DEV,) + (a.shape if gather else a.shape[1:]), a.dtype) for a in arrs]
    return pl.pallas_call(
        body, name=name,
        in_specs=[hbm] * n, out_specs=[hbm] * n, out_shape=out_shape,
        scratch_shapes=[pltpu.SemaphoreType.DMA((n * npeer,)), pltpu.SemaphoreType.DMA((n * npeer,)),
                        pltpu.SemaphoreType.DMA((n,))],
        compiler_params=pltpu.CompilerParams(has_side_effects=True),
    )(*arrs)


def _forward_to_sibling(stack, own, *, name):
    def body(s_ref, own_ref, out_ref, send_sems, recv_sems, local_sem):
        del s_ref
        x, y, c = lax.axis_index("x"), lax.axis_index("y"), lax.axis_index("c")
        sibling = (x, y, 1 - c)
        chips = [(1 - x, y), (x, 1 - y), (1 - x, 1 - y)]

        def slot(px, py, pc):
            return out_ref.at[4 * px + 2 * py + pc]

        def copy(k, block):
            return pltpu.make_async_remote_copy(
                src_ref=slot(*block), dst_ref=slot(*block),
                send_sem=send_sems.at[k], recv_sem=recv_sems.at[k],
                device_id=sibling, device_id_type=pl.DeviceIdType.MESH)

        mine = pltpu.make_async_copy(own_ref, slot(x, y, c), local_sem)
        mine.start()
        passed = [copy(k, (*chip, c)) for k, chip in enumerate(chips)]
        for cp in passed:
            cp.start()
        for k, chip in enumerate(chips):
            copy(k, (*chip, 1 - c)).wait_recv()
        for cp in passed:
            cp.wait_send()
        mine.wait()

    return pl.pallas_call(
        body, name=name,
        in_specs=[_ANY, _ANY], out_specs=_ANY,
        out_shape=jax.ShapeDtypeStruct(stack.shape, stack.dtype),
        scratch_shapes=[pltpu.SemaphoreType.DMA((3,)), pltpu.SemaphoreType.DMA((3,)), pltpu.SemaphoreType.DMA],
        input_output_aliases={0: 0},
        compiler_params=pltpu.CompilerParams(has_side_effects=True),
    )(stack, own)


_HBM = pl.BlockSpec(memory_space=pltpu.HBM)
_SEM = pl.BlockSpec(memory_space=pltpu.SEMAPHORE)
_EFFECT = pltpu.SideEffectType.DATAFLOW_SIDE_EFFECTING
NPEER = N_DEV - 1


def _peer_table():
    x, y, c = lax.axis_index("x"), lax.axis_index("y"), lax.axis_index("c")
    peers = []
    for k in range(1, N_DEV):
        px = 1 - x if k & 4 else x
        py = 1 - y if k & 2 else y
        pc = 1 - c if k & 1 else c
        peers.append(((px, py, pc), 4 * px + 2 * py + pc))
    return 4 * x + 2 * y + c, peers


ALL_PEERS = tuple(range(NPEER))
CHIP_PEERS = (0, 1, 3, 5)


def _split_copy(ins, lands, send_sems, recv_sems, gather, me, peers, ks, w, slot, arriving):
    dev, idx = peers[ks[slot]]
    return pltpu.make_async_remote_copy(
        src_ref=ins[w] if gather else ins[w].at[idx],
        dst_ref=lands[w].at[idx if arriving else me],
        send_sem=send_sems.at[w * len(ks) + slot], recv_sem=recv_sems.at[w * len(ks) + slot],
        device_id=dev, device_id_type=pl.DeviceIdType.MESH)


def _exchange_start(arrs, *, gather, name, deps=(), ks=ALL_PEERS):
    n = len(arrs)
    land_shapes = [(N_DEV,) + (a.shape if gather else a.shape[1:]) for a in arrs]

    def body(*refs):
        ins, lands = refs[:n], refs[n:2 * n]
        send_sems, recv_sems = refs[2 * n + len(deps)], refs[2 * n + len(deps) + 1]
        token = refs[-1]
        me, peers = _peer_table()
        for w in range(n):
            for slot in range(len(ks)):
                _split_copy(ins, lands, send_sems, recv_sems, gather, me, peers, ks, w, slot, False).start()
        token[...] = jnp.zeros_like(token)

    out_shape = ([pltpu.SemaphoreType.DMA((n * len(ks),)), pltpu.SemaphoreType.DMA((n * len(ks),))]
                 + [pltpu.HBM(a.shape, a.dtype) for a in arrs]
                 + [pltpu.HBM(s, a.dtype) for s, a in zip(land_shapes, arrs)]
                 + [jax.ShapeDtypeStruct((8, LANES), F32)])
    res = pl.pallas_call(
        body, name=name,
        in_specs=[_HBM] * (2 * n) + [_ANY] * len(deps),
        out_specs=[_SEM, _SEM] + [_HBM] * (2 * n) + [pl.BlockSpec(memory_space=pltpu.VMEM)],
        out_shape=out_shape,
        input_output_aliases={i: 2 + i for i in range(2 * n)},
        compiler_params=pltpu.CompilerParams(has_side_effects=_EFFECT),
    )(*[pltpu.with_memory_space_constraint(a, pltpu.HBM) for a in arrs],
      *[pltpu.with_memory_space_constraint(lax.empty(s, a.dtype), pltpu.HBM) for s, a in zip(land_shapes, arrs)],
      *deps)
    return (n, gather, ks, res[0], res[1], res[2:2 + n], res[2 + n:2 + 2 * n]), res[-1]


def _exchange_wait(handle, after, *, name):
    n, gather, ks, send_sems, recv_sems, ins_thru, lands_thru = handle

    def body(*refs):
        ins, lands = refs[:n], refs[n:2 * n]
        send_s, recv_s = refs[2 * n], refs[2 * n + 1]
        me, peers = _peer_table()
        for w in range(n):
            for slot in range(len(ks)):
                _split_copy(ins, lands, send_s, recv_s, gather, me, peers, ks, w, slot, False).wait_send()
                _split_copy(ins, lands, send_s, recv_s, gather, me, peers, ks, w, slot, True).wait_recv()

    res = pl.pallas_call(
        body, name=name,
        in_specs=[_HBM] * (2 * n) + [_SEM, _SEM, pl.BlockSpec(memory_space=pl.ANY)],
        out_specs=[_HBM] * (2 * n),
        out_shape=[pltpu.HBM(a.shape, a.dtype) for a in list(ins_thru) + list(lands_thru)],
        input_output_aliases={i: i for i in range(2 * n)},
        compiler_params=pltpu.CompilerParams(has_side_effects=_EFFECT),
    )(*ins_thru, *lands_thru, send_sems, recv_sems, after)
    return res[:n], res[n:2 * n]


def _ordered_sum(s_ref, own_ref):
    if own_ref is None:
        blocks = [s_ref[q].astype(F32) for q in range(N_DEV)]
    else:
        me = 4 * lax.axis_index("x") + 2 * lax.axis_index("y") + lax.axis_index("c")
        own = own_ref[...]
        blocks = [jnp.where(me == q, own, s_ref[q]).astype(F32) for q in range(N_DEV)]
    acc = blocks[0]
    for b in blocks[1:]:
        acc = acc + b
    return acc


def _sum8(stack, own, *, name):
    _, R, C = stack.shape
    if R % 8 == 0:
        tr, tc = _pick(R, max(8, STEP_BYTES // (C * 4 * (N_DEV + 2))), 8), C
    else:
        tr, tc = R, _pick(C, max(LANES, STEP_BYTES // (R * 4 * (N_DEV + 2))))

    def body(s_ref, own_ref, o_ref):
        o_ref[...] = _ordered_sum(s_ref, own_ref)

    blk = pl.BlockSpec((tr, tc), lambda i, j: (i, j))
    return pl.pallas_call(
        body, name=name, grid=(R // tr, C // tc),
        in_specs=[pl.BlockSpec((N_DEV, tr, tc), lambda i, j: (0, i, j)), blk],
        out_specs=blk,
        out_shape=jax.ShapeDtypeStruct((R, C), F32),
        compiler_params=_cparams("parallel", "parallel"),
    )(stack, own)


def _adamw_math(w, g, m, v):
    m = ADAM_B1 * m + (1.0 - ADAM_B1) * g
    v = ADAM_B2 * v + (1.0 - ADAM_B2) * (g * g)
    m_hat = m / (1.0 - ADAM_B1 ** ADAM_STEP)
    v_hat = v / (1.0 - ADAM_B2 ** ADAM_STEP)
    delta = -ADAM_LR * (m_hat / (jnp.sqrt(v_hat) + ADAM_EPS) + ADAM_WD * w)
    return delta, m, v


def _adamw(w, g, m, v, *, name, stacked, own=None, transposed=False):
    R, C = w.shape
    if transposed:
        tr = _pick(R, max(LANES, STEP_BYTES // (C * 4 * (9 + N_DEV))))
    else:
        tr = _pick(R, max(8, STEP_BYTES // (C * 4 * (8 + (N_DEV if stacked else 1)))), 8)
    has_own = own is not None

    def body(w_ref, g_ref, m_ref, v_ref, *rest):
        go_ref, d_ref, mo_ref, vo_ref = rest[-4:]
        g = _ordered_sum(g_ref, rest[0] if has_own else None) if stacked else g_ref[...]
        if transposed:
            g = g.T
        delta, m2, v2 = _adamw_math(w_ref[...], g, m_ref[...], v_ref[...])
        go_ref[...] = g
        d_ref[...] = delta
        mo_ref[...] = m2
        vo_ref[...] = v2

    row = pl.BlockSpec((tr, C), lambda i: (i, 0))
    if transposed:
        g_spec, own_spec = pl.BlockSpec((N_DEV, C, tr), lambda i: (0, 0, i)), pl.BlockSpec((C, tr), lambda i: (0, i))
    else:
        g_spec, own_spec = (pl.BlockSpec((N_DEV, tr, C), lambda i: (0, i, 0)) if stacked else row), row
    return pl.pallas_call(
        body, name=name, grid=(R // tr,),
        in_specs=[row, g_spec, row, row] + [own_spec] * has_own, out_specs=[row] * 4,
        out_shape=[jax.ShapeDtypeStruct((R, C), F32)] * 4,
        compiler_params=_cparams("parallel"),
    )(w, g, m, v, *([own] if has_own else []))


def kernel(x, positions, attn_norm, w_in, fox_f_bias, swa_sinks, w_branch_swa, w_branch_fox, w_out, mlp_norm, w_up, w_down, final_norm, loss_target, m_attn_norm, m_w_in, m_fox_f_bias, m_swa_sinks, m_w_branch_swa, m_w_branch_fox, m_w_out, m_mlp_norm, m_w_up, m_w_down, m_final_norm, v_attn_norm, v_w_in, v_fox_f_bias, v_swa_sinks, v_w_branch_swa, v_w_branch_fox, v_w_out, v_mlp_norm, v_w_up, v_w_down, v_final_norm):
    S, D = x.shape[1], x.shape[2]
    DFF = w_up.shape[2] * N_DEV
    d_in = w_in.shape[2] * N_DEV
    assert d_in == QKV_W + FOX_HEADS + 2 * D and (2 * D) % SWA_Q_W == 0 and S % (4 * LANES) == 0
    q_off = 2 * D
    k_off = q_off + SWA_Q_W
    v_off = k_off + SWA_KV_W
    fq_off = v_off + SWA_KV_W
    fk_off = fq_off + FOX_W
    fv_off = fk_off + FOX_W
    fl_off = fv_off + FOX_W
    NP = fl_off + FL_PAD
    x2d, tgt = x[0], loss_target[0]

    shards = [w_in[0].T.astype(BF16), w_branch_swa[0].T.astype(BF16), w_branch_fox[0].T.astype(BF16),
              w_out[0].astype(BF16), w_up[0].T.astype(BF16), w_down[0].astype(BF16)]
    me = 4 * lax.axis_index("x") + 2 * lax.axis_index("y") + lax.axis_index("c")

    def filled(stack, own):
        return lax.dynamic_update_slice(stack, own[None], (me,) + (0,) * own.ndim)

    h_in, tok_in = _exchange_start(shards[:1], gather=True, name="gather_w_in_start", ks=CHIP_PEERS)

    tm = _pick(S, 1024)
    td = _pick(D, 1024)
    tf = _pick(DFF, 1024)
    tnp = _pick(NP, 1024)

    h1 = _rms_fwd(x2d, attn_norm, name="rms1", deps=[tok_in])
    inv_freq = ROPE_THETA ** (-jnp.arange(0, HEAD_DIM, 2, dtype=F32) / HEAD_DIM)
    invf = jnp.tile(inv_freq, LANES // (HEAD_DIM // 2)).reshape(1, LANES)
    cos_t, sin_t = _rope_tables(positions.reshape(S, 1), invf, name="rope_tables", deps=[h1])
    (s_in,), (g_part,) = _exchange_wait(h_in, cos_t, name="gather_w_in_wait")
    g_in = _forward_to_sibling(g_part, s_in, name="gather_w_in_forward")
    h_rest, tok_rest = _exchange_start(shards[1:], gather=True, name="gather_rest_start", deps=[g_in])
    w_in_t = g_in.reshape(d_in, D)
    w_in_p = jnp.concatenate([w_in_t[QKV_W + FOX_HEADS:], w_in_t[:QKV_W], w_in_t[QKV_W:QKV_W + FOX_HEADS],
                              jnp.zeros((FL_PAD - FOX_HEADS, D), BF16)], axis=0)
    w_fl_t = w_in_t[QKV_W:QKV_W + FOX_HEADS]
    proj, = _matmul(h1, w_in_p, mode="nt", name="mm_in", out_dtypes=[BF16], tm=_pick(S, 2048), tn=tnp, tk=D,
                    deps=[tok_rest])
    z_sd, = _matmul(h1, w_fl_t, mode="nt", name="mm_flogit", out_dtypes=[F32], tm=tm, tn=FOX_HEADS, tk=D)
    z_t = z_sd.T
    bias_col = fox_f_bias.reshape(FOX_HEADS, 1)
    negc = _fox_prep(z_t, bias_col, name="fox_prep")
    (fbq, fbk), (bbq, bbk) = _fox_blocks(S)
    q_rope, k_rope = _rope_fwd(proj, cos_t, sin_t, q_off=q_off, k_off=k_off, name="rope_fwd")
    sinks = swa_sinks.reshape(-1)
    o_a = _swa_fwd(q_rope, k_rope, proj, sinks, v_off=v_off, name="swa_fwd")
    o_b, lse = _fox_fwd(proj, _key_bias_blocks(negc, fbk), q_off=fq_off, k_off=fk_off, v_off=fv_off,
                        bq=fbq, bk=fbk, name="fox_fwd")
    s_rest, g_rest = _exchange_wait(h_rest, o_b, name="gather_rest_wait")
    g_bs, g_bf, g_o, g_up, g_dn = [filled(g, s) for g, s in zip(g_rest, s_rest)]
    w_bs_t = g_bs.reshape(D, SWA_Q_W)
    w_bf_t = g_bf.reshape(D, FOX_W)
    w_o = g_o.reshape(D, D)
    w_up_t = g_up.reshape(DFF, D)
    w_dn = g_dn.reshape(DFF, D)
    ya, = _matmul(o_a, w_bs_t, mode="nt", name="mm_branch_swa", out_dtypes=[BF16], tm=tm, tn=td, tk=SWA_Q_W)
    gate_maps = [lambda i, j, k: (i, j), lambda i, j, k: (i, j), lambda i, j, k: (i, j + D // td)]

    def merge_epi(acc, ya_t, ga_t, gb_t):
        merged = _sigmoid(ga_t.astype(F32)) * ya_t.astype(F32) + _sigmoid(gb_t.astype(F32)) * acc
        return acc, merged

    yb, merged = _matmul(o_b, w_bf_t, mode="nt", name="mm_branch_fox", out_dtypes=[BF16, BF16],
                         tm=tm, tn=td, tk=FOX_W, extras=[ya, proj, proj], extra_maps=gate_maps,
                         epilogue=merge_epi)
    x_mid, = _matmul(merged, w_o, mode="nn", name="mm_out", out_dtypes=[F32], tm=tm, tn=td, tk=D,
                     extras=[x2d], epilogue=lambda acc, r: (acc + r,))
    h2 = _rms_fwd(x_mid, mlp_norm, name="rms2")
    u, = _matmul(h2, w_up_t, mode="nt", name="mm_up", out_dtypes=[BF16], tm=_pick(S, 2048), tn=tf, tk=D,
                 epilogue=lambda acc: (jnp.maximum(acc, 0.0),))
    x_fin, = _matmul(u, w_dn, mode="nn", name="mm_down", out_dtypes=[F32], tm=tm, tn=td, tk=_pick(DFF, 2048),
                     a_fn=_square_bf16, extras=[x_mid], epilogue=lambda acc, r: (acc + r,))

    dx3b, dg3, loss_part = _loss_head(x_fin, tgt, final_norm.reshape(1, D), name="loss_head")
    d_up, = _matmul(dx3b, w_dn, mode="nt", name="mm_d_act", out_dtypes=[BF16], tm=_pick(S, 2048), tn=tf, tk=D,
                    extras=[u], epilogue=lambda acc, ut: (acc * (2.0 * ut.astype(F32)),))
    tks = _pick(S, 2048)
    dw_dn, = _matmul(u, dx3b, mode="tn", name="mm_dw_down", out_dtypes=[BF16], tm=tf, tn=td, tk=tks,
                     a_fn=_square_bf16)
    dh2, = _matmul(d_up, w_up_t, mode="nn", name="mm_dh2", out_dtypes=[BF16], tm=tm, tn=td, tk=_pick(DFF, 2048))
    dw_up_t, = _matmul(d_up, h2, mode="tn", name="mm_dw_up", out_dtypes=[BF16], tm=tf, tn=td, tk=tks)
    h_s1, tok_s1 = _exchange_start([dw_up_t.reshape(N_DEV, DFF // N_DEV, D), dw_dn.reshape(N_DEV, DFF // N_DEV, D)],
                                   gather=False, name="scatter_mlp_start")
    dx2b, dg2 = _rms_bwd(dh2, x_mid, mlp_norm, dx3b, name="rms2_bwd", out_dtype=BF16, deps=[tok_s1])

    def gate_bwd_epi(dm, ya_t, yb_t, ga_t, gb_t):
        sa, sb = _sigmoid(ga_t.astype(F32)), _sigmoid(gb_t.astype(F32))
        return (dm * sa, dm * sb, dm * ya_t.astype(F32) * sa * (1.0 - sa), dm * yb_t.astype(F32) * sb * (1.0 - sb))

    gmaps = [lambda i, j, k: (i, j), lambda i, j, k: (i, j), lambda i, j, k: (i, j),
             lambda i, j, k: (i, j + D // td)]
    d_ya, d_yb, d_ga, d_gb = _matmul(dx2b, w_o, mode="nt", name="mm_d_merged", out_dtypes=[BF16] * 4,
                                     tm=tm, tn=td, tk=D, extras=[ya, yb, proj, proj], extra_maps=gmaps,
                                     epilogue=gate_bwd_epi)
    dw_o, = _matmul(merged, dx2b, mode="tn", name="mm_dw_out", out_dtypes=[BF16], tm=td, tn=td, tk=tks)
    d_oa, = _matmul(d_ya, w_bs_t, mode="nn", name="mm_d_oa", out_dtypes=[BF16], tm=tm, tn=SWA_Q_W, tk=D)
    d_ob, = _matmul(d_yb, w_bf_t, mode="nn", name="mm_d_ob", out_dtypes=[BF16], tm=tm, tn=FOX_W, tk=D)
    dw_bs_t, = _matmul(d_ya, o_a, mode="tn", name="mm_dw_bs", out_dtypes=[BF16], tm=td, tn=SWA_Q_W, tk=tks)
    dw_bf_t, = _matmul(d_yb, o_b, mode="tn", name="mm_dw_bf", out_dtypes=[BF16], tm=td, tn=FOX_W, tk=tks)
    h_s2, tok_s2 = _exchange_start([dw_bs_t.reshape(N_DEV, D // N_DEV, SWA_Q_W),
                                    dw_bf_t.reshape(N_DEV, D // N_DEV, FOX_W), dw_o.reshape(N_DEV, D // N_DEV, D)],
                                   gather=False, name="scatter_attn_start")
    def row_blocks_t(a):
        return a.reshape(S // bbq, bbq, FOX_W).transpose(0, 2, 1)

    d_fq, d_fk, d_fv, dcol4, drow4 = _fox_bwd(proj, _key_bias_blocks(negc, bbk), o_b, lse, d_ob,
                                              row_blocks_t(proj[:, fq_off:fq_off + FOX_W]), row_blocks_t(d_ob),
                                              q_off=fq_off, k_off=fk_off, v_off=fv_off, bq=bbq, bk=bbk,
                                              name="fox_bwd", deps=[tok_s2])
    dcol = dcol4.transpose(0, 2, 1, 3).reshape(FOX_HEADS, S)
    drow = drow4.transpose(0, 2, 1, 3).reshape(FOX_HEADS, S)
    dz_t, dbias_l = _fox_post(drow, dcol, z_t, bias_col, name="fox_post")
    d_aq, dk_c, dk_p, dv_c, dv_p, dsink_l = _swa_bwd(q_rope, k_rope, proj, sinks, d_oa, cos_t, sin_t, v_off=v_off,
                                                     name="swa_bwd")
    d_ak, d_av = _rope_bwd(dk_c, dk_p, dv_c, dv_p, cos_t, sin_t, name="rope_bwd")
    dz_pad = jnp.pad(dz_t.T.astype(BF16), ((0, 0), (0, FL_PAD - FOX_HEADS)))
    d_proj = jnp.concatenate([d_ga, d_gb, d_aq, d_ak, d_av, d_fq, d_fk, d_fv, dz_pad], axis=1)
    tkp = _pick(NP, 2304)
    dw_in_p, = _matmul(d_proj, h1, mode="tn", name="mm_dw_in", out_dtypes=[BF16], tm=_pick(NP, 512), tn=D, tk=tks)
    dw_in_t = jnp.concatenate([dw_in_p[q_off:q_off + QKV_W], dw_in_p[fl_off:fl_off + FOX_HEADS], dw_in_p[:q_off]],
                              axis=0)
    h_s3, tok_s3 = _exchange_start([dw_in_t.reshape(N_DEV, d_in // N_DEV, D)], gather=False,
                                   name="scatter_in_start")
    dh1, = _matmul(d_proj, w_in_p, mode="nn", name="mm_dh1", out_dtypes=[BF16], tm=tm, tn=td, tk=tkp, deps=[tok_s3])
    dx, dg1 = _rms_bwd(dh1, x2d, attn_norm, dx2b, name="rms1_bwd", out_dtype=F32)

    dbias = dbias_l[:, 0]
    dsinks = dsink_l[:, :, 0].reshape(-1)
    nsm = 3 * D + 2 * LANES
    tail = jnp.zeros((2 * LANES,), F32)
    small_g = jnp.concatenate([dg1[0], dg2[0], dg3[0],
                               tail.at[0:16].set(dbias).at[16:32].set(dsinks).at[32].set(loss_part[0, 0])])

    def pack(a_norm, b_norm, f_norm, bias, snk):
        return jnp.concatenate([a_norm[0], b_norm[0], f_norm,
                                tail.at[0:16].set(bias[0]).at[16:32].set(snk[0])]).reshape(1, nsm)

    small_stack, = _exchange([small_g.reshape(1, nsm)], gather=True, name="gather_small")
    u_sm = _adamw(pack(attn_norm, mlp_norm, final_norm, fox_f_bias, swa_sinks), small_stack,
                  pack(m_attn_norm, m_mlp_norm, m_final_norm, m_fox_f_bias, m_swa_sinks),
                  pack(v_attn_norm, v_mlp_norm, v_final_norm, v_fox_f_bias, v_swa_sinks),
                  name="adamw_small", stacked=True)
    loss = u_sm[0][0, 3 * D + 32]

    def own_of(src):
        return lax.dynamic_index_in_dim(src, me, 0, keepdims=False)

    def update_t(stack, src, w, m, v, nm):
        g = _sum8(stack, own_of(src), name="sum_" + nm).T
        return _adamw(w[0], g, m[0], v[0], name="adamw_" + nm, stacked=False)

    def update(stack, src, w, m, v, nm, transposed=False):
        return _adamw(w[0], stack, m[0], v[0], name="adamw_" + nm, stacked=True, own=own_of(src),
                      transposed=transposed)

    (s_up, s_dn), (r_up, r_dn) = _exchange_wait(h_s1, u_sm[1], name="scatter_mlp_wait")
    u_up = update(r_up, s_up, w_up, m_w_up, v_w_up, "w_up", transposed=True)
    u_dn = update(r_dn, s_dn, w_down, m_w_down, v_w_down, "w_down")
    (s_bs, s_bf, s_o), (r_bs, r_bf, r_o) = _exchange_wait(h_s2, u_dn[1], name="scatter_attn_wait")
    u_bs = update(r_bs, s_bs, w_branch_swa, m_w_branch_swa, v_w_branch_swa, "w_bs", transposed=True)
    u_bf = update(r_bf, s_bf, w_branch_fox, m_w_branch_fox, v_w_branch_fox, "w_bf", transposed=True)
    u_o = update(r_o, s_o, w_out, m_w_out, v_w_out, "w_out")
    (s_w_in,), (r_in,) = _exchange_wait(h_s3, u_o[1], name="scatter_in_wait")
    u_in = update_t(r_in, s_w_in, w_in, m_w_in, v_w_in, "w_in")

    def small(kind):
        a = u_sm[kind][0]
        return dict(attn_norm=a[0:D][None], mlp_norm=a[D:2 * D][None], final_norm=a[2 * D:3 * D],
                    fox_f_bias=a[3 * D:3 * D + 16][None], swa_sinks=a[3 * D + 16:3 * D + 32][None])

    big = dict(w_in=u_in, w_branch_swa=u_bs, w_branch_fox=u_bf, w_out=u_o, w_up=u_up, w_down=u_dn)
    order = ["attn_norm", "w_in", "fox_f_bias", "swa_sinks", "w_branch_swa", "w_branch_fox", "w_out", "mlp_norm",
             "w_up", "w_down", "final_norm"]
    outs = [loss, dx[None]]
    for kind in range(4):
        sm = small(kind)
        for nm in order:
            outs.append(big[nm][kind][None] if nm in big else sm[nm])
    return tuple(outs)
```

```python
import functools

import jax
import jax.numpy as jnp
from jax import lax
from jax.experimental import pallas as pl
from jax.experimental.pallas import tpu as pltpu

F32 = jnp.float32
BF16 = jnp.bfloat16

N_DEV = 8
HEAD_DIM = 64
SWA_Q_W = 1024
SWA_KV_W = 128
SWA_GROUP = 8
WINDOW = 128
FOX_W = 1024
FOX_HEADS = 16
QKV_W = SWA_Q_W + 2 * SWA_KV_W + 3 * FOX_W
FL_PAD = 256
ROPE_THETA = 10000.0
RMS_EPS = 1e-6
ATT_SCALE = 0.125
NEG = -1e30

ADAM_LR = 0.001
ADAM_B1 = 0.9
ADAM_B2 = 0.999
ADAM_EPS = 1e-08
ADAM_WD = 0.01
ADAM_STEP = 10

FOX_FWD_BLOCKS = (1024, 1024)
FOX_BWD_BLOCKS = (1024, 512)
FOX_FWD_PAIRS = 2

LANES = 128
VMEM_LIMIT = 56 * 1024 * 1024
STEP_BYTES = 12 * 1024 * 1024


def _cparams(*sem):
    return pltpu.CompilerParams(dimension_semantics=sem, vmem_limit_bytes=VMEM_LIMIT)


def _pick(dim, pref, align=LANES):
    best = None
    t = align
    while t <= min(dim, pref):
        if dim % t == 0:
            best = t
        t += align
    return best if best is not None else dim


_DIMS = {"nn": ((1,), (0,)), "nt": ((1,), (1,)), "tn": ((0,), (0,))}


_ANY = pl.BlockSpec(memory_space=pl.ANY)


def _matmul(a, b, *, mode, name, out_dtypes, tm, tn, tk, extras=(), extra_maps=None,
            a_fn=None, epilogue=None, deps=()):
    if mode == "nn":
        (M, K), (K2, N) = a.shape, b.shape
    elif mode == "nt":
        (M, K), (N, K2) = a.shape, b.shape
    else:
        (K, M), (K2, N) = a.shape, b.shape
    assert K == K2, (name, a.shape, b.shape)
    assert M % tm == 0 and N % tn == 0 and K % tk == 0, (name, M, N, K, tm, tn, tk)
    nk = K // tk
    ne, no = len(extras), len(out_dtypes)
    dims = (_DIMS[mode], ((), ()))

    def body(*refs):
        a_ref, b_ref = refs[0], refs[1]
        ex_refs = refs[2:2 + ne]
        out_refs = refs[2 + ne + len(deps):2 + ne + len(deps) + no]

        def finish(acc):
            res = (acc,) if epilogue is None else epilogue(acc, *[e[...] for e in ex_refs])
            for o_ref, r in zip(out_refs, res):
                o_ref[...] = r.astype(o_ref.dtype)

        def product():
            av = a_ref[...]
            if a_fn is not None:
                av = a_fn(av)
            return lax.dot_general(av, b_ref[...], dims, preferred_element_type=F32)

        if nk == 1:
            finish(product())
        else:
            acc_ref = refs[-1]
            k = pl.program_id(2)

            @pl.when(k == 0)
            def _():
                acc_ref[...] = jnp.zeros_like(acc_ref)

            acc_ref[...] += product()

            @pl.when(k == nk - 1)
            def _():
                finish(acc_ref[...])

    if mode == "tn":
        a_spec = pl.BlockSpec((tk, tm), lambda i, j, k: (k, i))
    else:
        a_spec = pl.BlockSpec((tm, tk), lambda i, j, k: (i, k))
    if mode == "nt":
        b_spec = pl.BlockSpec((tn, tk), lambda i, j, k: (j, k))
    else:
        b_spec = pl.BlockSpec((tk, tn), lambda i, j, k: (k, j))
    if extra_maps is None:
        extra_maps = [lambda i, j, k: (i, j)] * ne
    ex_specs = [pl.BlockSpec((tm, tn), m) for m in extra_maps]
    out_spec = [pl.BlockSpec((tm, tn), lambda i, j, k: (i, j)) for _ in range(no)]
    res = pl.pallas_call(
        body,
        name=name,
        grid=(M // tm, N // tn, nk),
        in_specs=[a_spec, b_spec] + ex_specs + [_ANY] * len(deps),
        out_specs=out_spec,
        out_shape=[jax.ShapeDtypeStruct((M, N), d) for d in out_dtypes],
        scratch_shapes=[pltpu.VMEM((tm, tn), F32)] if nk > 1 else [],
        compiler_params=_cparams("parallel", "parallel", "arbitrary"),
    )(a, b, *extras, *deps)
    return res


def _square_bf16(t):
    tf = t.astype(F32)
    return (tf * tf).astype(BF16)


def _sigmoid(g):
    return 1.0 / (1.0 + jnp.exp(-g))


def _rms_fwd(x, gain, *, name, deps=()):
    S, D = x.shape
    tr = _pick(S, 512, 8)

    def body(x_ref, g_ref, *rest):
        h_ref = rest[-1]
        xv = x_ref[...]
        r = lax.rsqrt(jnp.mean(xv * xv, axis=-1, keepdims=True) + RMS_EPS)
        h_ref[...] = (xv * r * g_ref[...]).astype(BF16)

    return pl.pallas_call(
        body, name=name, grid=(S // tr,),
        in_specs=[pl.BlockSpec((tr, D), lambda i: (i, 0)), pl.BlockSpec((1, D), lambda i: (0, 0))] + [_ANY] * len(deps),
        out_specs=pl.BlockSpec((tr, D), lambda i: (i, 0)),
        out_shape=jax.ShapeDtypeStruct((S, D), BF16),
        compiler_params=_cparams("parallel"),
    )(x, gain, *deps)


def _rms_bwd(dh, x, gain, dres, *, name, out_dtype, deps=()):
    S, D = x.shape
    tr = _pick(S, 256, 8)

    def body(dh_ref, x_ref, g_ref, dres_ref, *rest):
        outs = rest[len(deps):]
        dx_ref, dg_ref = outs[0], outs[-1]
        xv = x_ref[...]
        r = lax.rsqrt(jnp.mean(xv * xv, axis=-1, keepdims=True) + RMS_EPS)
        xh = xv * r
        dhv = dh_ref[...].astype(F32)
        t = dhv * g_ref[...]
        dx = r * (t - xh * jnp.mean(t * xh, axis=-1, keepdims=True)) + dres_ref[...].astype(F32)
        dx_ref[...] = dx.astype(out_dtype)
        part = jnp.sum(dhv * xh, axis=0, keepdims=True)

        @pl.when(pl.program_id(0) == 0)
        def _():
            dg_ref[...] = part

        @pl.when(pl.program_id(0) > 0)
        def _():
            dg_ref[...] += part

    row = pl.BlockSpec((tr, D), lambda i: (i, 0))
    vec = pl.BlockSpec((1, D), lambda i: (0, 0))
    return pl.pallas_call(
        body, name=name, grid=(S // tr,),
        in_specs=[row, row, vec, row] + [_ANY] * len(deps), out_specs=[row, vec],
        out_shape=[jax.ShapeDtypeStruct((S, D), out_dtype), jax.ShapeDtypeStruct((1, D), F32)],
        compiler_params=_cparams("arbitrary"),
    )(dh, x, gain, dres, *deps)


def _loss_head(x3, target, gain, *, name):
    S, D = x3.shape
    tr = _pick(S, 256, 8)

    def body(x_ref, t_ref, g_ref, dxb_ref, dg_ref, loss_ref):
        xv = x_ref[...]
        r = lax.rsqrt(jnp.mean(xv * xv, axis=-1, keepdims=True) + RMS_EPS)
        xh = xv * r
        gv = g_ref[...]
        err = xh * gv - t_ref[...]
        lpart = jnp.zeros((1, LANES), F32) + (0.5 / D) * jnp.sum(err * err)
        dy = err * (1.0 / D)
        t = dy * gv
        dx = r * (t - xh * jnp.mean(t * xh, axis=-1, keepdims=True))
        dxb_ref[...] = dx.astype(BF16)
        part = jnp.sum(dy * xh, axis=0, keepdims=True)

        @pl.when(pl.program_id(0) == 0)
        def _():
            dg_ref[...] = part
            loss_ref[...] = lpart

        @pl.when(pl.program_id(0) > 0)
        def _():
            dg_ref[...] += part
            loss_ref[...] += lpart

    row = pl.BlockSpec((tr, D), lambda i: (i, 0))
    vec = pl.BlockSpec((1, D), lambda i: (0, 0))
    return pl.pallas_call(
        body, name=name, grid=(S // tr,),
        in_specs=[row, row, vec],
        out_specs=[row, vec, pl.BlockSpec((1, LANES), lambda i: (0, 0))],
        out_shape=[jax.ShapeDtypeStruct((S, D), BF16),
                   jax.ShapeDtypeStruct((1, D), F32), jax.ShapeDtypeStruct((1, LANES), F32)],
        compiler_params=_cparams("arbitrary"),
    )(x3, target, gain)


def _rope_tables(pos_col, invf, *, name):
    S = pos_col.shape[0]
    tr = _pick(S, 512, 8)

    def body(p_ref, f_ref, cos_ref, sin_ref):
        ang = p_ref[...].astype(F32) * f_ref[...]
        lane = lax.broadcasted_iota(jnp.int32, (1, LANES), 1)
        first = (lane % HEAD_DIM) < HEAD_DIM // 2
        sn = jnp.sin(ang)
        cos_ref[...] = jnp.cos(ang)
        sin_ref[...] = jnp.where(first, -sn, sn)

    return pl.pallas_call(
        body, name=name, grid=(S // tr,),
        in_specs=[pl.BlockSpec((tr, 1), lambda i: (i, 0)), pl.BlockSpec((1, LANES), lambda i: (0, 0))],
        out_specs=[pl.BlockSpec((tr, LANES), lambda i: (i, 0))] * 2,
        out_shape=[jax.ShapeDtypeStruct((S, LANES), F32)] * 2,
        compiler_params=_cparams("parallel"),
    )(pos_col, invf)


def _swap_halves(t):
    lane = lax.broadcasted_iota(jnp.int32, (1, LANES), 1)
    first = (lane % HEAD_DIM) < HEAD_DIM // 2
    return jnp.where(first, pltpu.roll(t, LANES - HEAD_DIM // 2, 1), pltpu.roll(t, HEAD_DIM // 2, 1))


def _rope_fwd(proj, cos_t, sin_t, *, q_off, k_off, name):
    S = proj.shape[0]
    tr = _pick(S, 256, 8)
    nqb = SWA_Q_W // LANES

    def body(q_ref, k_ref, c_ref, s_ref, qo_ref, ko_ref):
        cv, sv = c_ref[...], s_ref[...]
        for b in range(nqb):
            t = q_ref[:, b * LANES:(b + 1) * LANES].astype(F32)
            qo_ref[:, b * LANES:(b + 1) * LANES] = (t * cv + _swap_halves(t) * sv).astype(BF16)
        t = k_ref[...].astype(F32)
        ko_ref[...] = (t * cv + _swap_halves(t) * sv).astype(BF16)

    tab = pl.BlockSpec((tr, LANES), lambda i: (i, 0))
    return pl.pallas_call(
        body, name=name, grid=(S // tr,),
        in_specs=[pl.BlockSpec((tr, SWA_Q_W), lambda i: (i, q_off // SWA_Q_W)),
                  pl.BlockSpec((tr, LANES), lambda i: (i, k_off // LANES)), tab, tab],
        out_specs=[pl.BlockSpec((tr, SWA_Q_W), lambda i: (i, 0)), tab],
        out_shape=[jax.ShapeDtypeStruct((S, SWA_Q_W), BF16), jax.ShapeDtypeStruct((S, LANES), BF16)],
        compiler_params=_cparams("parallel"),
    )(proj, proj, cos_t, sin_t)


def _rope_bwd(dk_cur, dk_prev, dv_cur, dv_prev, cos_t, sin_t, *, name):
    S = dk_cur.shape[1]
    tr = _pick(S, 512)
    nb = S // tr

    def body(kc_ref, kp_ref, vc_ref, vp_ref, c_ref, s_ref, dko_ref, dvo_ref):
        cv, sv = c_ref[...], s_ref[...]
        row = pl.program_id(0) * tr + lax.broadcasted_iota(jnp.int32, (tr, 1), 0)
        has_next = row < S - WINDOW
        d = kc_ref[0] + kc_ref[1] + jnp.where(has_next, kp_ref[0] + kp_ref[1], 0.0)
        dko_ref[...] = (d * cv + _swap_halves(d * sv)).astype(BF16)
        dvo_ref[...] = (vc_ref[0] + vc_ref[1] + jnp.where(has_next, vp_ref[0] + vp_ref[1], 0.0)).astype(BF16)

    tab = pl.BlockSpec((tr, LANES), lambda i: (i, 0))
    cur = pl.BlockSpec((2, tr, LANES), lambda i: (0, i, 0))
    return pl.pallas_call(
        body, name=name, grid=(nb,),
        in_specs=[cur, cur, cur, cur, tab, tab],
        out_specs=[tab, tab],
        out_shape=[jax.ShapeDtypeStruct((S, LANES), BF16), jax.ShapeDtypeStruct((S, LANES), BF16)],
        compiler_params=_cparams("parallel"),
    )(dk_cur, dk_prev, dv_cur, dv_prev, cos_t, sin_t)


def _dot_nt(a, b):
    return lax.dot_general(a, b, (((1,), (1,)), ((), ())), preferred_element_type=F32)


def _dot_tn(a, b):
    return lax.dot_general(a, b, (((0,), (0,)), ((), ())), preferred_element_type=F32)


def _dot_nn(a, b):
    return lax.dot_general(a, b, (((1,), (0,)), ((), ())), preferred_element_type=F32)


def _roll_half(t):
    return pltpu.roll(t.astype(F32), HEAD_DIM, 1).astype(t.dtype)


SWA_STACK = SWA_GROUP // 2


def _swa_mask_bias():
    rows = SWA_STACK * WINDOW
    row = lax.broadcasted_iota(jnp.int32, (rows, 2 * WINDOW), 0) % WINDOW
    col = lax.broadcasted_iota(jnp.int32, (rows, 2 * WINDOW), 1)
    diff = row + WINDOW - col
    window = (diff >= 0) & (diff < WINDOW)
    return jnp.stack([jnp.where(window & (col >= WINDOW), 0.0, NEG), jnp.where(window, 0.0, NEG)]).astype(F32)


def _swa_common(hk, kp_ref, kc_ref, vp_ref, vc_ref):
    k2 = jnp.concatenate([kp_ref[...], kc_ref[...]], axis=0)
    v2 = jnp.concatenate([vp_ref[...], vc_ref[...]], axis=0)
    k_sw, v_sw = _roll_half(k2), _roll_half(v2)
    lane = lax.broadcasted_iota(jnp.int32, (1, LANES), 1)
    half = [lane < HEAD_DIM, lane >= HEAD_DIM]
    kk = [jnp.where(hk == a, k2, k_sw) for a in range(2)]
    vv = [jnp.where(hk == a, v2, v_sw) for a in range(2)]
    return half, kk, vv


def _swa_stack(ref, mask, scale=None):
    parts = []
    for t in range(SWA_STACK):
        blk = ref[:, t * LANES:(t + 1) * LANES]
        if scale is not None:
            blk = blk * jnp.asarray(scale, blk.dtype)
        parts.append(jnp.where(mask, blk, jnp.zeros_like(blk)))
    return jnp.concatenate(parts, axis=0)


def _swa_sink_column(sink_ref, hk, a):
    blk = lax.broadcasted_iota(jnp.int32, (SWA_STACK * WINDOW, 1), 0) // WINDOW
    col = jnp.zeros((SWA_STACK * WINDOW, 1), F32)
    for t in range(SWA_STACK):
        col = jnp.where(blk == t, sink_ref[hk * SWA_GROUP + 2 * t + a], col)
    return col


def _swa_probs(qm, kk, mask_bias, sink):
    s = _dot_nt(qm, kk) + mask_bias
    m = jnp.maximum(jnp.max(s, axis=1, keepdims=True), sink)
    e = jnp.exp(s - m)
    es = jnp.exp(sink - m)
    inv = 1.0 / (jnp.sum(e, axis=1, keepdims=True) + es)
    return e * inv, es * inv


def _swa_mask_spec():
    return pl.BlockSpec((1, SWA_STACK * WINDOW, 2 * WINDOW), lambda hk, n: (jnp.minimum(n, 1), 0, 0))


def _swa_fwd(q_rope, k_rope, proj, sinks, mask_bias, *, v_off, name):
    S = q_rope.shape[0]
    nb = S // WINDOW
    gw = SWA_GROUP * HEAD_DIM

    def body(sink_ref, q_ref, kp_ref, kc_ref, vp_ref, vc_ref, mask_ref, o_ref):
        hk = pl.program_id(0)
        half, kk, vv = _swa_common(hk, kp_ref, kc_ref, vp_ref, vc_ref)
        outs = []
        for a in range(2):
            qm = _swa_stack(q_ref, half[a], ATT_SCALE)
            p, _ = _swa_probs(qm, kk[a], mask_ref[0], _swa_sink_column(sink_ref, hk, a))
            outs.append(_dot_nn(p.astype(BF16), vv[a]))
        for t in range(SWA_STACK):
            rows = slice(t * WINDOW, (t + 1) * WINDOW)
            o_ref[:, t * LANES:(t + 1) * LANES] = jnp.where(half[0], outs[0][rows], outs[1][rows]).astype(BF16)

    prev = lambda hk, n: (jnp.maximum(n - 1, 0), 0)
    cur = lambda hk, n: (n, 0)
    vprev = lambda hk, n: (jnp.maximum(n - 1, 0), v_off // LANES)
    vcur = lambda hk, n: (n, v_off // LANES)
    blk = lambda m: pl.BlockSpec((WINDOW, LANES), m)
    return pl.pallas_call(
        body, name=name, grid=(2, nb),
        in_specs=[pl.BlockSpec(memory_space=pltpu.SMEM),
                  pl.BlockSpec((WINDOW, gw), lambda hk, n: (n, hk)),
                  blk(prev), blk(cur), blk(vprev), blk(vcur), _swa_mask_spec()],
        out_specs=pl.BlockSpec((WINDOW, gw), lambda hk, n: (n, hk)),
        out_shape=jax.ShapeDtypeStruct((S, SWA_Q_W), BF16),
        compiler_params=_cparams("parallel", "parallel"),
    )(sinks, q_rope, k_rope, k_rope, proj, proj, mask_bias)


def _swa_bwd(q_rope, k_rope, proj, sinks, d_o, cos_t, sin_t, mask_bias, *, v_off, name):
    S = q_rope.shape[0]
    nb = S // WINDOW
    gw = SWA_GROUP * HEAD_DIM

    def body(sink_ref, q_ref, kp_ref, kc_ref, vp_ref, vc_ref, do_ref, c_ref, s_ref, mask_ref,
             dq_ref, dkc_ref, dkp_ref, dvc_ref, dvp_ref, dsink_ref):
        hk, n = pl.program_id(0), pl.program_id(1)
        half, kk, vv = _swa_common(hk, kp_ref, kc_ref, vp_ref, vc_ref)
        allowed = mask_ref[0]
        dk_acc = jnp.zeros((2 * WINDOW, LANES), F32)
        dv_acc = jnp.zeros((2 * WINDOW, LANES), F32)
        srow = lax.broadcasted_iota(jnp.int32, (SWA_GROUP, LANES), 0)
        dsink = jnp.zeros((SWA_GROUP, LANES), F32)
        dqs = []
        for a in range(2):
            qm = _swa_stack(q_ref, half[a], ATT_SCALE)
            dom = _swa_stack(do_ref, half[a])
            p, psink = _swa_probs(qm, kk[a], allowed, _swa_sink_column(sink_ref, hk, a))
            dp = _dot_nt(dom, vv[a])
            delta = jnp.sum(p * dp, axis=1, keepdims=True)
            ds = (p * (dp - delta)).astype(BF16)
            dsk = psink * delta
            for t in range(SWA_STACK):
                dsink = dsink + jnp.where(srow == 2 * t + a, -jnp.sum(dsk[t * WINDOW:(t + 1) * WINDOW]), 0.0)
            dqs.append(_dot_nn(ds, kk[a]) * ATT_SCALE)
            dk_acc = dk_acc + _dot_tn(ds, qm)
            dv_acc = dv_acc + _dot_tn(p.astype(BF16), dom)
        cv, sv = c_ref[...], s_ref[...]
        for t in range(SWA_STACK):
            rows = slice(t * WINDOW, (t + 1) * WINDOW)
            d = jnp.where(half[0], dqs[0][rows], dqs[1][rows])
            dq_ref[:, t * LANES:(t + 1) * LANES] = (d * cv + _swap_halves(d * sv)).astype(BF16)
        lane = lax.broadcasted_iota(jnp.int32, (1, LANES), 1)
        mine = (lane >= HEAD_DIM) == (hk == 1)
        dk_t = jnp.where(mine, dk_acc + pltpu.roll(dk_acc, HEAD_DIM, 1), 0.0)
        dv_t = jnp.where(mine, dv_acc + pltpu.roll(dv_acc, HEAD_DIM, 1), 0.0)
        dkp_ref[0] = dk_t[:WINDOW]
        dkc_ref[0] = dk_t[WINDOW:]
        dvp_ref[0] = dv_t[:WINDOW]
        dvc_ref[0] = dv_t[WINDOW:]

        @pl.when(n == 0)
        def _():
            dsink_ref[0] = dsink

        @pl.when(n > 0)
        def _():
            dsink_ref[0] += dsink

    prev = lambda hk, n: (jnp.maximum(n - 1, 0), 0)
    cur = lambda hk, n: (n, 0)
    vprev = lambda hk, n: (jnp.maximum(n - 1, 0), v_off // LANES)
    vcur = lambda hk, n: (n, v_off // LANES)
    blk = lambda m: pl.BlockSpec((WINDOW, LANES), m)
    qblk = pl.BlockSpec((WINDOW, gw), lambda hk, n: (n, hk))
    part = pl.BlockSpec((1, WINDOW, LANES), lambda hk, n: (hk, n, 0))
    part_prev = pl.BlockSpec((1, WINDOW, LANES), lambda hk, n: (hk, jnp.maximum(n - 1, 0), 0))
    part_shape = jax.ShapeDtypeStruct((2, S, LANES), F32)
    return pl.pallas_call(
        body, name=name, grid=(2, nb),
        in_specs=[pl.BlockSpec(memory_space=pltpu.SMEM), qblk, blk(prev), blk(cur), blk(vprev), blk(vcur), qblk,
                  blk(cur), blk(cur), _swa_mask_spec()],
        out_specs=[qblk, part, part_prev, part, part_prev,
                   pl.BlockSpec((1, SWA_GROUP, LANES), lambda hk, n: (hk, 0, 0))],
        out_shape=[jax.ShapeDtypeStruct((S, SWA_Q_W), BF16), part_shape, part_shape, part_shape, part_shape,
                   jax.ShapeDtypeStruct((2, SWA_GROUP, LANES), F32)],
        compiler_params=_cparams("parallel", "arbitrary"),
    )(sinks, q_rope, k_rope, k_rope, proj, proj, d_o, cos_t, sin_t, mask_bias)


def _fox_prep(z_t, bias_col, *, name):
    H, S = z_t.shape
    tb = _pick(S, 512)

    def body(z_ref, b_ref, o_ref, carry_ref):
        @pl.when(pl.program_id(0) == 0)
        def _():
            carry_ref[...] = jnp.zeros_like(carry_ref)

        zz = z_ref[...] + b_ref[...]
        t = jnp.exp(-jnp.abs(zz))
        log1p = jnp.where(t < 1e-2, t * (1.0 - t * (0.5 - t * (1.0 / 3.0))), jnp.log(1.0 + t))
        logf = jnp.minimum(zz, 0.0) - log1p
        r = lax.broadcasted_iota(jnp.int32, (tb, tb), 0)
        c = lax.broadcasted_iota(jnp.int32, (tb, tb), 1)
        tri = (r <= c).astype(BF16)
        hi = logf.astype(BF16)
        r1 = logf - hi.astype(F32)
        mid = r1.astype(BF16)
        lo = (r1 - mid.astype(F32)).astype(BF16)
        cs = _dot_nn(hi, tri) + _dot_nn(mid, tri) + _dot_nn(lo, tri) + carry_ref[:, 0:1]
        o_ref[...] = -cs
        carry_ref[...] = jnp.zeros_like(carry_ref) + cs[:, tb - 1:tb]

    return pl.pallas_call(
        body, name=name, grid=(S // tb,),
        in_specs=[pl.BlockSpec((H, tb), lambda i: (0, i)), pl.BlockSpec((H, 1), lambda i: (0, 0))],
        out_specs=pl.BlockSpec((H, tb), lambda i: (0, i)),
        out_shape=jax.ShapeDtypeStruct((H, S), F32),
        scratch_shapes=[pltpu.VMEM((H, LANES), F32)],
        compiler_params=_cparams("arbitrary"),
    )(z_t, bias_col)


def _fox_post(drow, dcol, z_t, bias_col, *, name):
    H, S = z_t.shape
    tb = _pick(S, 512)
    nb = S // tb

    def body(dr_ref, d_ref, z_ref, b_ref, dz_ref, db_ref, carry_ref):
        @pl.when(pl.program_id(0) == 0)
        def _():
            carry_ref[...] = jnp.zeros_like(carry_ref)
            db_ref[...] = jnp.zeros_like(db_ref)

        dc = dr_ref[...] - d_ref[...]
        r = lax.broadcasted_iota(jnp.int32, (tb, tb), 0)
        c = lax.broadcasted_iota(jnp.int32, (tb, tb), 1)
        tri = (r >= c).astype(BF16)
        hi = dc.astype(BF16)
        r1 = dc - hi.astype(F32)
        mid = r1.astype(BF16)
        lo = (r1 - mid.astype(F32)).astype(BF16)
        dlogf = _dot_nn(hi, tri) + _dot_nn(mid, tri) + _dot_nn(lo, tri) + carry_ref[:, 0:1]
        carry_ref[...] = jnp.zeros_like(carry_ref) + dlogf[:, 0:1]
        dz = dlogf * _sigmoid(-(z_ref[...] + b_ref[...]))
        dz_ref[...] = dz
        db_ref[...] += jnp.sum(dz, axis=1, keepdims=True)

    rev = lambda i: (0, nb - 1 - i)
    return pl.pallas_call(
        body, name=name, grid=(nb,),
        in_specs=[pl.BlockSpec((H, tb), rev), pl.BlockSpec((H, tb), rev), pl.BlockSpec((H, tb), rev),
                  pl.BlockSpec((H, 1), lambda i: (0, 0))],
        out_specs=[pl.BlockSpec((H, tb), rev), pl.BlockSpec((H, LANES), lambda i: (0, 0))],
        out_shape=[jax.ShapeDtypeStruct((H, S), F32), jax.ShapeDtypeStruct((H, LANES), F32)],
        scratch_shapes=[pltpu.VMEM((H, LANES), F32)],
        compiler_params=_cparams("arbitrary"),
    )(drow, dcol, z_t, bias_col)


def _fox_blocks(S):
    cap = max(LANES, S // 4)
    return (min(FOX_FWD_BLOCKS[0], cap), min(FOX_FWD_BLOCKS[1], cap)), \
           (min(FOX_BWD_BLOCKS[0], cap), min(FOX_BWD_BLOCKS[1], cap))


def _key_bias_blocks(negc, bk):
    H, S = negc.shape
    return negc.reshape(H // 2, 2, S // bk, bk).transpose(0, 2, 1, 3)


def _fox_fwd(proj, negc4, *, q_off, k_off, v_off, bq, bk, name):
    S = proj.shape[0]
    nq, nk = S // bq, S // bk
    npair = FOX_HEADS // 2
    assert bq % bk == 0 or bk % bq == 0
    nmask = max(1, bq // bk)

    gp = FOX_FWD_PAIRS
    gw = gp * LANES
    assert q_off % gw == 0 and k_off % gw == 0 and v_off % gw == 0 and npair % gp == 0

    def body(q_ref, k_ref, v_ref, nc_ref, o_ref, lse_ref):
        i = pl.program_id(1)
        lane = lax.broadcasted_iota(jnp.int32, (1, LANES), 1)
        half = [lane < HEAD_DIM, lane >= HEAD_DIM]
        qh = []
        for g in range(gp):
            q2 = q_ref[:, g * LANES:(g + 1) * LANES] * jnp.asarray(ATT_SCALE, BF16)
            qh += [jnp.where(half[h], q2, jnp.zeros_like(q2)) for h in range(2)]
        row = lax.broadcasted_iota(jnp.int32, (bq, bk), 0)
        col = lax.broadcasted_iota(jnp.int32, (bq, bk), 1)
        rel = row - col
        nfull = (i * bq) // bk

        spare = [HEAD_DIM, 0]
        ones_lane = [lane == spare[h] for h in range(2)]

        def step(j, carry, masked):
            start = pl.multiple_of(j * bk, bk)
            new = []
            for g in range(gp):
                ks = k_ref[pl.ds(start, bk), g * LANES:(g + 1) * LANES]
                vs = v_ref[pl.ds(start, bk), g * LANES:(g + 1) * LANES]
                nb = nc_ref[g, j]
                for h in range(2):
                    m, acc = carry[4 * g + 2 * h:4 * g + 2 * h + 2]
                    vh = jnp.where(half[h], vs, jnp.where(ones_lane[h], jnp.ones_like(vs), jnp.zeros_like(vs)))
                    qs, bias = qh[2 * g + h], nb[h:h + 1, :]

                    def update(m, acc, rows, keys):
                        s = _dot_nt(qs[rows], ks[keys]) + bias[:, keys]
                        if masked:
                            s = jnp.where(rel[rows, keys] >= j * bk - i * bq, s, NEG)
                        m_new = jnp.maximum(m[rows], jnp.max(s, axis=1, keepdims=True))
                        p = jnp.exp(s - m_new).astype(BF16)
                        return m_new, jnp.exp(m[rows] - m_new) * acc[rows] + _dot_nn(p, vh[keys])

                    if masked and bq == bk:
                        top, bot, everything = slice(0, bq // 2), slice(bq // 2, bq), slice(0, bk)
                        m_t, acc_t = update(m, acc, top, top)
                        m_b, acc_b = update(m, acc, bot, everything)
                        new += [jnp.concatenate([m_t, m_b], axis=0), jnp.concatenate([acc_t, acc_b], axis=0)]
                    else:
                        new += list(update(m, acc, slice(0, bq), slice(0, bk)))
            return tuple(new)

        init = (jnp.full((bq, 1), NEG, F32), jnp.zeros((bq, LANES), F32)) * (2 * gp)
        carry = lax.fori_loop(0, nfull, lambda j, c: step(j, c, False), init)
        for t in range(nmask):
            carry = step(nfull + t, carry, True)
        for g in range(gp):
            outs, lses = [], []
            for h in range(2):
                m, acc = carry[4 * g + 2 * h:4 * g + 2 * h + 2]
                l = acc[:, spare[h]:spare[h] + 1]
                outs.append(acc * (1.0 / l))
                lses.append(m + jnp.log(l))
            o_ref[:, g * LANES:(g + 1) * LANES] = jnp.where(half[0], outs[0], outs[1]).astype(BF16)
            lse_ref[g] = jnp.where(half[0], lses[0], lses[1])

    seq = lambda off: pl.BlockSpec((S, gw), lambda hp, i: (0, off // gw + hp))
    return pl.pallas_call(
        body, name=name, grid=(npair // gp, nq),
        in_specs=[pl.BlockSpec((bq, gw), lambda hp, i: (i, q_off // gw + hp)), seq(k_off), seq(v_off),
                  pl.BlockSpec((gp, nk, 2, bk), lambda hp, i: (hp, 0, 0, 0))],
        out_specs=[pl.BlockSpec((bq, gw), lambda hp, i: (i, hp)),
                   pl.BlockSpec((gp, bq, LANES), lambda hp, i: (hp, i, 0))],
        out_shape=[jax.ShapeDtypeStruct((S, FOX_W), BF16), jax.ShapeDtypeStruct((npair, S, LANES), F32)],
        compiler_params=_cparams("parallel", "parallel"),
    )(proj, proj, proj, negc4)


def _fox_bwd(proj, negc4, o, lse, d_o, q_t, do_t, *, q_off, k_off, v_off, bq, bk, name, deps=()):
    S = proj.shape[0]
    nq, nk = S // bq, S // bk
    npair = FOX_HEADS // 2
    assert bq % bk == 0 or bk % bq == 0
    nmask = max(1, bk // bq)

    def body(q_ref, k_ref, v_ref, nc_ref, o_ref, lse_ref, do_ref, qt_ref, dot_ref, *rest):
        dqo_ref, dk_ref, dv_ref, dn_ref, dr_ref, delta_ref, rs_ref, dq_ref = rest[len(deps):]
        j = pl.program_id(1)
        lane = lax.broadcasted_iota(jnp.int32, (1, LANES), 1)
        half = [lane < HEAD_DIM, lane >= HEAD_DIM]
        spare = [HEAD_DIM, 0]
        ones_lane = [lane == spare[h] for h in range(2)]
        srow = lax.broadcasted_iota(jnp.int32, (LANES, 1), 0)
        rhalf = [srow < HEAD_DIM, srow >= HEAD_DIM]
        ones_row = [srow == spare[h] for h in range(2)]
        k2, v2 = k_ref[...], v_ref[...]
        one_k = jnp.ones_like(k2)
        kh = [jnp.where(half[h], k2, jnp.where(ones_lane[h], one_k, jnp.zeros_like(k2))) for h in range(2)]
        nb = nc_ref[0, 0]
        row = lax.broadcasted_iota(jnp.int32, (bq, bk), 0)
        col = lax.broadcasted_iota(jnp.int32, (bq, bk), 1)
        rel = row - col
        i_first = (j * bk) // bq

        @pl.when(j == 0)
        def _():
            dq_ref[...] = jnp.zeros_like(dq_ref)
            rs_ref[...] = jnp.zeros_like(rs_ref)
            for b in range(nq):
                prod = do_ref[b * bq:(b + 1) * bq, :].astype(F32) * o_ref[b * bq:(b + 1) * bq, :].astype(F32)
                d0 = jnp.sum(jnp.where(half[0], prod, 0.0), axis=1, keepdims=True)
                d1 = jnp.sum(jnp.where(half[1], prod, 0.0), axis=1, keepdims=True)
                delta_ref[b * bq:(b + 1) * bq, :] = jnp.where(half[0], d0, d1)

        def step(i, carry, masked, r0=0):
            dkt_a, dkt_b, dvt = carry
            dkts = [dkt_a, dkt_b]
            nr = bq - r0
            start = pl.multiple_of(i * bq + r0, LANES)
            q2 = q_ref[pl.ds(start, nr), :] * jnp.asarray(ATT_SCALE, BF16)
            do2 = do_ref[pl.ds(start, nr), :]
            qt = qt_ref[i][:, r0:] * jnp.asarray(ATT_SCALE, BF16)
            dot = dot_ref[i][:, r0:]
            lse2 = lse_ref[0, pl.ds(start, nr), :]
            del2 = delta_ref[pl.ds(start, nr), :]
            dqf = []
            for h in range(2):
                qm = jnp.where(half[h], q2, jnp.zeros_like(q2))
                dom = jnp.where(half[h], do2, jnp.zeros_like(do2))
                qtm = jnp.where(rhalf[h], qt, jnp.where(ones_row[h], jnp.ones_like(qt), jnp.zeros_like(qt)))
                dotm = jnp.where(rhalf[h], dot, jnp.zeros_like(dot))
                c0 = h * HEAD_DIM
                p = jnp.exp(_dot_nt(qm, k2) + nb[h:h + 1, :] - lse2[:, c0:c0 + 1])
                if masked:
                    p = jnp.where(rel[r0:] >= j * bk - i * bq, p, 0.0)
                dp = _dot_nt(dom, v2)
                dsb = (p * (dp - del2[:, c0:c0 + 1])).astype(BF16)
                dvt = dvt + _dot_nn(dotm, p.astype(BF16))
                dkts[h] = dkts[h] + _dot_nn(qtm, dsb)
                dqf.append(_dot_nn(dsb, kh[h]))
            dq_ref[pl.ds(start, nr), :] += jnp.where(half[0], dqf[0], dqf[1]) * ATT_SCALE
            rs_ref[pl.ds(start, nr), :] += jnp.where(ones_lane[0], dqf[0], jnp.where(ones_lane[1], dqf[1], 0.0))
            return dkts[0], dkts[1], dvt

        zero = jnp.zeros((LANES, bk), F32)
        carry = (zero, zero, zero)
        if bq > bk:
            sp = j % (bq // bk)
            carry = lax.switch(sp, [functools.partial(step, i_first, masked=True, r0=s * bk)
                                    for s in range(bq // bk)], carry)
        else:
            for t in range(nmask):
                carry = step(i_first + t, carry, True)
        dkt_a, dkt_b, dvt = lax.fori_loop(i_first + nmask, nq, lambda i, c: step(i, c, False), carry)
        dk_ref[...] = jnp.where(rhalf[0], dkt_a, dkt_b).T.astype(BF16)
        dv_ref[...] = dvt.T.astype(BF16)
        dn_ref[0, 0] = jnp.concatenate([dkt_a[spare[0]:spare[0] + 1], dkt_b[spare[1]:spare[1] + 1]], axis=0)

        @pl.when(j == nk - 1)
        def _():
            dqo_ref[...] = dq_ref[...].astype(BF16)
            for b in range(nq):
                t = rs_ref[b * bq:(b + 1) * bq, :].T
                dr_ref[0, b] = jnp.concatenate([t[spare[0]:spare[0] + 1], t[spare[1]:spare[1] + 1]], axis=0)

    once = pl.Buffered(1)
    seq = lambda off: pl.BlockSpec((S, LANES), lambda hp, j: (0, off // LANES + hp), pipeline_mode=once)
    blk = lambda off: pl.BlockSpec((bk, LANES), lambda hp, j: (j, off // LANES + hp))
    nc = pl.BlockSpec((1, 1, 2, bk), lambda hp, j: (hp, j, 0, 0))
    tsp = pl.BlockSpec((nq, LANES, bq), lambda hp, j: (0, hp, 0), pipeline_mode=once)
    return pl.pallas_call(
        body, name=name, grid=(npair, nk),
        in_specs=[seq(q_off), blk(k_off), blk(v_off), nc, seq(0),
                  pl.BlockSpec((1, S, LANES), lambda hp, j: (hp, 0, 0), pipeline_mode=once), seq(0),
                  tsp, tsp] + [_ANY] * len(deps),
        out_specs=[pl.BlockSpec((S, LANES), lambda hp, j: (0, hp)), blk(0), blk(0), nc,
                   pl.BlockSpec((1, nq, 2, bq), lambda hp, j: (hp, 0, 0, 0))],
        out_shape=[jax.ShapeDtypeStruct((S, FOX_W), BF16), jax.ShapeDtypeStruct((S, FOX_W), BF16),
                   jax.ShapeDtypeStruct((S, FOX_W), BF16), jax.ShapeDtypeStruct((npair, nk, 2, bk), F32),
                   jax.ShapeDtypeStruct((npair, nq, 2, bq), F32)],
        scratch_shapes=[pltpu.VMEM((S, LANES), F32), pltpu.VMEM((S, LANES), F32), pltpu.VMEM((S, LANES), F32)],
        compiler_params=_cparams("parallel", "arbitrary"),
    )(proj, proj, proj, negc4, o, lse, d_o, q_t, do_t, *deps)


def _exchange(arrs, *, gather, name):
    n = len(arrs)
    npeer = N_DEV - 1

    def body(*refs):
        ins, outs = refs[:n], refs[n:2 * n]
        send_sems, recv_sems, loc_sems = refs[2 * n:]
        x, y, c = lax.axis_index("x"), lax.axis_index("y"), lax.axis_index("c")
        me = 4 * x + 2 * y + c
        peers = []
        for k in range(1, N_DEV):
            px = 1 - x if k & 4 else x
            py = 1 - y if k & 2 else y
            pc = 1 - c if k & 1 else c
            peers.append(((px, py, pc), 4 * px + 2 * py + pc))

        def remote(w, k):
            dev, idx = peers[k]
            src = ins[w] if gather else ins[w].at[idx]
            return pltpu.make_async_remote_copy(
                src_ref=src, dst_ref=outs[w].at[me],
                send_sem=send_sems.at[w * npeer + k], recv_sem=recv_sems.at[w * npeer + k],
                device_id=dev, device_id_type=pl.DeviceIdType.MESH)

        def arrival(w, k):
            dev, idx = peers[k]
            src = ins[w] if gather else ins[w].at[idx]
            return pltpu.make_async_remote_copy(
                src_ref=src, dst_ref=outs[w].at[idx],
                send_sem=send_sems.at[w * npeer + k], recv_sem=recv_sems.at[w * npeer + k],
                device_id=dev, device_id_type=pl.DeviceIdType.MESH)

        local = []
        for w in range(n):
            for k in range(npeer):
                remote(w, k).start()
            cp = pltpu.make_async_copy(ins[w] if gather else ins[w].at[me], outs[w].at[me], loc_sems.at[w])
            cp.start()
            local.append(cp)
        for w in range(n):
            for k in range(npeer):
                arrival(w, k).wait_recv()
        for w in range(n):
            for k in range(npeer):
                remote(w, k).wait_send()
            local[w].wait()

    hbm = pl.BlockSpec(memory_space=pl.ANY)
    out_shape = [jax.ShapeDtypeStruct((N_DEV,) + (a.shape if gather else a.shape[1:]), a.dtype) for a in arrs]
    return pl.pallas_call(
        body, name=name,
        in_specs=[hbm] * n, out_specs=[hbm] * n, out_shape=out_shape,
        scratch_shapes=[pltpu.SemaphoreType.DMA((n * npeer,)), pltpu.SemaphoreType.DMA((n * npeer,)),
                        pltpu.SemaphoreType.DMA((n,))],
        compiler_params=pltpu.CompilerParams(has_side_effects=True),
    )(*arrs)


def _gather_two_level(shard, *, name):
    def body(x_ref, out_ref, send_sems, recv_sems, local_sem):
        x, y, c = lax.axis_index("x"), lax.axis_index("y"), lax.axis_index("c")
        me, sibling = (x, y, c), (x, y, 1 - c)
        chips = [(1 - x, y), (x, 1 - y), (1 - x, 1 - y)]

        def slot(px, py, pc):
            return out_ref.at[4 * px + 2 * py + pc]

        def copy(k, block, to, src=None):
            return pltpu.make_async_remote_copy(
                src_ref=slot(*block) if src is None else src, dst_ref=slot(*block),
                send_sem=send_sems.at[k], recv_sem=recv_sems.at[k],
                device_id=to, device_id_type=pl.DeviceIdType.MESH)

        mine = pltpu.make_async_copy(x_ref, slot(*me), local_sem)
        mine.start()
        first = [copy(0, me, sibling, src=x_ref)]
        first += [copy(1 + j, me, (*chip, c), src=x_ref) for j, chip in enumerate(chips)]
        for cp in first:
            cp.start()
        passed = [copy(4 + j, (*chip, c), sibling) for j, chip in enumerate(chips)]
        for j, chip in enumerate(chips):
            copy(1 + j, (*chip, c), me).wait_recv()
            passed[j].start()
        copy(0, sibling, me).wait_recv()
        for j, chip in enumerate(chips):
            copy(4 + j, (*chip, 1 - c), me).wait_recv()
        for cp in first + passed:
            cp.wait_send()
        mine.wait()

    return pl.pallas_call(
        body, name=name,
        in_specs=[_ANY], out_specs=_ANY,
        out_shape=jax.ShapeDtypeStruct((N_DEV,) + shard.shape, shard.dtype),
        scratch_shapes=[pltpu.SemaphoreType.DMA((N_DEV - 1,)), pltpu.SemaphoreType.DMA((N_DEV - 1,)),
                        pltpu.SemaphoreType.DMA],
        compiler_params=pltpu.CompilerParams(has_side_effects=True),
    )(shard)


_HBM = pl.BlockSpec(memory_space=pltpu.HBM)
_SEM = pl.BlockSpec(memory_space=pltpu.SEMAPHORE)
_EFFECT = pltpu.SideEffectType.DATAFLOW_SIDE_EFFECTING
NPEER = N_DEV - 1


def _peer_table():
    x, y, c = lax.axis_index("x"), lax.axis_index("y"), lax.axis_index("c")
    peers = []
    for k in range(1, N_DEV):
        px = 1 - x if k & 4 else x
        py = 1 - y if k & 2 else y
        pc = 1 - c if k & 1 else c
        peers.append(((px, py, pc), 4 * px + 2 * py + pc))
    return 4 * x + 2 * y + c, peers


def _split_copy(ins, lands, send_sems, recv_sems, gather, me, peers, w, k, arriving):
    dev, idx = peers[k]
    return pltpu.make_async_remote_copy(
        src_ref=ins[w] if gather else ins[w].at[idx],
        dst_ref=lands[w].at[idx if arriving else me],
        send_sem=send_sems.at[w * NPEER + k], recv_sem=recv_sems.at[w * NPEER + k],
        device_id=dev, device_id_type=pl.DeviceIdType.MESH)


def _exchange_start(arrs, *, gather, name, deps=()):
    n = len(arrs)
    land_shapes = [(N_DEV,) + (a.shape if gather else a.shape[1:]) for a in arrs]

    def body(*refs):
        ins, lands = refs[:n], refs[n:2 * n]
        send_sems, recv_sems = refs[2 * n + len(deps)], refs[2 * n + len(deps) + 1]
        token = refs[-1]
        me, peers = _peer_table()
        for w in range(n):
            for k in range(NPEER):
                _split_copy(ins, lands, send_sems, recv_sems, gather, me, peers, w, k, False).start()
        token[...] = jnp.zeros_like(token)

    out_shape = ([pltpu.SemaphoreType.DMA((n * NPEER,)), pltpu.SemaphoreType.DMA((n * NPEER,))]
                 + [pltpu.HBM(a.shape, a.dtype) for a in arrs]
                 + [pltpu.HBM(s, a.dtype) for s, a in zip(land_shapes, arrs)]
                 + [jax.ShapeDtypeStruct((8, LANES), F32)])
    res = pl.pallas_call(
        body, name=name,
        in_specs=[_HBM] * (2 * n) + [_ANY] * len(deps),
        out_specs=[_SEM, _SEM] + [_HBM] * (2 * n) + [pl.BlockSpec(memory_space=pltpu.VMEM)],
        out_shape=out_shape,
        input_output_aliases={i: 2 + i for i in range(2 * n)},
        compiler_params=pltpu.CompilerParams(has_side_effects=_EFFECT),
    )(*[pltpu.with_memory_space_constraint(a, pltpu.HBM) for a in arrs],
      *[pltpu.with_memory_space_constraint(lax.empty(s, a.dtype), pltpu.HBM) for s, a in zip(land_shapes, arrs)],
      *deps)
    return (n, gather, res[0], res[1], res[2:2 + n], res[2 + n:2 + 2 * n]), res[-1]


def _exchange_wait(handle, after, *, name):
    n, gather, send_sems, recv_sems, ins_thru, lands_thru = handle

    def body(*refs):
        ins, lands = refs[:n], refs[n:2 * n]
        send_s, recv_s = refs[2 * n], refs[2 * n + 1]
        me, peers = _peer_table()
        for w in range(n):
            for k in range(NPEER):
                _split_copy(ins, lands, send_s, recv_s, gather, me, peers, w, k, False).wait_send()
                _split_copy(ins, lands, send_s, recv_s, gather, me, peers, w, k, True).wait_recv()

    res = pl.pallas_call(
        body, name=name,
        in_specs=[_HBM] * (2 * n) + [_SEM, _SEM, pl.BlockSpec(memory_space=pl.ANY)],
        out_specs=[_HBM] * (2 * n),
        out_shape=[pltpu.HBM(a.shape, a.dtype) for a in list(ins_thru) + list(lands_thru)],
        input_output_aliases={i: i for i in range(2 * n)},
        compiler_params=pltpu.CompilerParams(has_side_effects=_EFFECT),
    )(*ins_thru, *lands_thru, send_sems, recv_sems, after)
    return res[:n], res[n:2 * n]


def _ordered_sum(s_ref, own_ref):
    if own_ref is None:
        blocks = [s_ref[q].astype(F32) for q in range(N_DEV)]
    else:
        me = 4 * lax.axis_index("x") + 2 * lax.axis_index("y") + lax.axis_index("c")
        own = own_ref[...]
        blocks = [jnp.where(me == q, own, s_ref[q]).astype(F32) for q in range(N_DEV)]
    acc = blocks[0]
    for b in blocks[1:]:
        acc = acc + b
    return acc


def _sum8(stack, own, *, name):
    _, R, C = stack.shape
    if R % 8 == 0:
        tr, tc = _pick(R, max(8, STEP_BYTES // (C * 4 * (N_DEV + 2))), 8), C
    else:
        tr, tc = R, _pick(C, max(LANES, STEP_BYTES // (R * 4 * (N_DEV + 2))))

    def body(s_ref, own_ref, o_ref):
        o_ref[...] = _ordered_sum(s_ref, own_ref)

    blk = pl.BlockSpec((tr, tc), lambda i, j: (i, j))
    return pl.pallas_call(
        body, name=name, grid=(R // tr, C // tc),
        in_specs=[pl.BlockSpec((N_DEV, tr, tc), lambda i, j: (0, i, j)), blk],
        out_specs=blk,
        out_shape=jax.ShapeDtypeStruct((R, C), F32),
        compiler_params=_cparams("parallel", "parallel"),
    )(stack, own)


def _adamw_math(w, g, m, v):
    m = ADAM_B1 * m + (1.0 - ADAM_B1) * g
    v = ADAM_B2 * v + (1.0 - ADAM_B2) * (g * g)
    m_hat = m / (1.0 - ADAM_B1 ** ADAM_STEP)
    v_hat = v / (1.0 - ADAM_B2 ** ADAM_STEP)
    delta = -ADAM_LR * (m_hat / (jnp.sqrt(v_hat) + ADAM_EPS) + ADAM_WD * w)
    return delta, m, v


def _adamw(w, g, m, v, *, name, stacked, own=None, transposed=False):
    R, C = w.shape
    if transposed:
        tr = _pick(R, max(LANES, STEP_BYTES // (C * 4 * (9 + N_DEV))))
    else:
        tr = _pick(R, max(8, STEP_BYTES // (C * 4 * (8 + (N_DEV if stacked else 1)))), 8)
    has_own = own is not None

    def body(w_ref, g_ref, m_ref, v_ref, *rest):
        go_ref, d_ref, mo_ref, vo_ref = rest[-4:]
        g = _ordered_sum(g_ref, rest[0] if has_own else None) if stacked else g_ref[...]
        if transposed:
            g = g.T
        delta, m2, v2 = _adamw_math(w_ref[...], g, m_ref[...], v_ref[...])
        go_ref[...] = g
        d_ref[...] = delta
        mo_ref[...] = m2
        vo_ref[...] = v2

    row = pl.BlockSpec((tr, C), lambda i: (i, 0))
    if transposed:
        g_spec, own_spec = pl.BlockSpec((N_DEV, C, tr), lambda i: (0, 0, i)), pl.BlockSpec((C, tr), lambda i: (0, i))
    else:
        g_spec, own_spec = (pl.BlockSpec((N_DEV, tr, C), lambda i: (0, i, 0)) if stacked else row), row
    return pl.pallas_call(
        body, name=name, grid=(R // tr,),
        in_specs=[row, g_spec, row, row] + [own_spec] * has_own, out_specs=[row] * 4,
        out_shape=[jax.ShapeDtypeStruct((R, C), F32)] * 4,
        compiler_params=_cparams("parallel"),
    )(w, g, m, v, *([own] if has_own else []))


def kernel(x, positions, attn_norm, w_in, fox_f_bias, swa_sinks, w_branch_swa, w_branch_fox, w_out, mlp_norm, w_up, w_down, final_norm, loss_target, m_attn_norm, m_w_in, m_fox_f_bias, m_swa_sinks, m_w_branch_swa, m_w_branch_fox, m_w_out, m_mlp_norm, m_w_up, m_w_down, m_final_norm, v_attn_norm, v_w_in, v_fox_f_bias, v_swa_sinks, v_w_branch_swa, v_w_branch_fox, v_w_out, v_mlp_norm, v_w_up, v_w_down, v_final_norm):
    S, D = x.shape[1], x.shape[2]
    DFF = w_up.shape[2] * N_DEV
    d_in = w_in.shape[2] * N_DEV
    assert d_in == QKV_W + FOX_HEADS + 2 * D and (2 * D) % SWA_Q_W == 0 and S % (4 * LANES) == 0
    q_off = 2 * D
    k_off = q_off + SWA_Q_W
    v_off = k_off + SWA_KV_W
    fq_off = v_off + SWA_KV_W
    fk_off = fq_off + FOX_W
    fv_off = fk_off + FOX_W
    fl_off = fv_off + FOX_W
    NP = fl_off + FL_PAD
    x2d, tgt = x[0], loss_target[0]

    shards = [w_in[0].T.astype(BF16), w_branch_swa[0].T.astype(BF16), w_branch_fox[0].T.astype(BF16),
              w_out[0].astype(BF16), w_up[0].T.astype(BF16), w_down[0].astype(BF16)]
    me = 4 * lax.axis_index("x") + 2 * lax.axis_index("y") + lax.axis_index("c")

    def filled(stack, own):
        return lax.dynamic_update_slice(stack, own[None], (me,) + (0,) * own.ndim)

    g_in = _gather_two_level(shards[0], name="gather_w_in")
    h_rest, tok_rest = _exchange_start(shards[1:], gather=True, name="gather_rest_start", deps=[g_in])

    tm = _pick(S, 1024)
    td = _pick(D, 1024)
    tf = _pick(DFF, 1024)
    tnp = _pick(NP, 1024)

    h1 = _rms_fwd(x2d, attn_norm, name="rms1", deps=[tok_rest])
    w_in_t = g_in.reshape(d_in, D)
    w_in_p = jnp.concatenate([w_in_t[QKV_W + FOX_HEADS:], w_in_t[:QKV_W], w_in_t[QKV_W:QKV_W + FOX_HEADS],
                              jnp.zeros((FL_PAD - FOX_HEADS, D), BF16)], axis=0)
    w_fl_t = w_in_t[QKV_W:QKV_W + FOX_HEADS]
    proj, = _matmul(h1, w_in_p, mode="nt", name="mm_in", out_dtypes=[BF16], tm=_pick(S, 2048), tn=tnp, tk=D)
    z_sd, = _matmul(h1, w_fl_t, mode="nt", name="mm_flogit", out_dtypes=[F32], tm=tm, tn=FOX_HEADS, tk=D)
    z_t = z_sd.T
    bias_col = fox_f_bias.reshape(FOX_HEADS, 1)
    negc = _fox_prep(z_t, bias_col, name="fox_prep")
    (fbq, fbk), (bbq, bbk) = _fox_blocks(S)
    inv_freq = ROPE_THETA ** (-jnp.arange(0, HEAD_DIM, 2, dtype=F32) / HEAD_DIM)
    invf = jnp.tile(inv_freq, LANES // (HEAD_DIM // 2)).reshape(1, LANES)
    cos_t, sin_t = _rope_tables(positions.reshape(S, 1), invf, name="rope_tables")
    q_rope, k_rope = _rope_fwd(proj, cos_t, sin_t, q_off=q_off, k_off=k_off, name="rope_fwd")
    sinks = swa_sinks.reshape(-1)
    swa_mask = _swa_mask_bias()
    o_a = _swa_fwd(q_rope, k_rope, proj, sinks, swa_mask, v_off=v_off, name="swa_fwd")
    o_b, lse = _fox_fwd(proj, _key_bias_blocks(negc, fbk), q_off=fq_off, k_off=fk_off, v_off=fv_off,
                        bq=fbq, bk=fbk, name="fox_fwd")
    s_rest, g_rest = _exchange_wait(h_rest, o_b, name="gather_rest_wait")
    g_bs, g_bf, g_o, g_up, g_dn = [filled(g, s) for g, s in zip(g_rest, s_rest)]
    w_bs_t = g_bs.reshape(D, SWA_Q_W)
    w_bf_t = g_bf.reshape(D, FOX_W)
    w_o = g_o.reshape(D, D)
    w_up_t = g_up.reshape(DFF, D)
    w_dn = g_dn.reshape(DFF, D)
    ya, = _matmul(o_a, w_bs_t, mode="nt", name="mm_branch_swa", out_dtypes=[BF16], tm=tm, tn=td, tk=SWA_Q_W)
    gate_maps = [lambda i, j, k: (i, j), lambda i, j, k: (i, j), lambda i, j, k: (i, j + D // td)]

    def merge_epi(acc, ya_t, ga_t, gb_t):
        merged = _sigmoid(ga_t.astype(F32)) * ya_t.astype(F32) + _sigmoid(gb_t.astype(F32)) * acc
        return acc, merged

    yb, merged = _matmul(o_b, w_bf_t, mode="nt", name="mm_branch_fox", out_dtypes=[BF16, BF16],
                         tm=tm, tn=td, tk=FOX_W, extras=[ya, proj, proj], extra_maps=gate_maps,
                         epilogue=merge_epi)
    x_mid, = _matmul(merged, w_o, mode="nn", name="mm_out", out_dtypes=[F32], tm=tm, tn=td, tk=D,
                     extras=[x2d], epilogue=lambda acc, r: (acc + r,))
    h2 = _rms_fwd(x_mid, mlp_norm, name="rms2")
    u, = _matmul(h2, w_up_t, mode="nt", name="mm_up", out_dtypes=[BF16], tm=_pick(S, 2048), tn=tf, tk=D,
                 epilogue=lambda acc: (jnp.maximum(acc, 0.0),))
    x_fin, = _matmul(u, w_dn, mode="nn", name="mm_down", out_dtypes=[F32], tm=tm, tn=td, tk=_pick(DFF, 2048),
                     a_fn=_square_bf16, extras=[x_mid], epilogue=lambda acc, r: (acc + r,))

    dx3b, dg3, loss_part = _loss_head(x_fin, tgt, final_norm.reshape(1, D), name="loss_head")
    d_up, = _matmul(dx3b, w_dn, mode="nt", name="mm_d_act", out_dtypes=[BF16], tm=_pick(S, 2048), tn=tf, tk=D,
                    extras=[u], epilogue=lambda acc, ut: (acc * (2.0 * ut.astype(F32)),))
    tks = _pick(S, 2048)
    dw_dn, = _matmul(u, dx3b, mode="tn", name="mm_dw_down", out_dtypes=[BF16], tm=tf, tn=td, tk=tks,
                     a_fn=_square_bf16)
    dh2, = _matmul(d_up, w_up_t, mode="nn", name="mm_dh2", out_dtypes=[BF16], tm=tm, tn=td, tk=_pick(DFF, 2048))
    dw_up_t, = _matmul(d_up, h2, mode="tn", name="mm_dw_up", out_dtypes=[BF16], tm=tf, tn=td, tk=tks)
    h_s1, tok_s1 = _exchange_start([dw_up_t.reshape(N_DEV, DFF // N_DEV, D), dw_dn.reshape(N_DEV, DFF // N_DEV, D)],
                                   gather=False, name="scatter_mlp_start")
    dx2b, dg2 = _rms_bwd(dh2, x_mid, mlp_norm, dx3b, name="rms2_bwd", out_dtype=BF16, deps=[tok_s1])

    def gate_bwd_epi(dm, ya_t, yb_t, ga_t, gb_t):
        sa, sb = _sigmoid(ga_t.astype(F32)), _sigmoid(gb_t.astype(F32))
        return (dm * sa, dm * sb, dm * ya_t.astype(F32) * sa * (1.0 - sa), dm * yb_t.astype(F32) * sb * (1.0 - sb))

    gmaps = [lambda i, j, k: (i, j), lambda i, j, k: (i, j), lambda i, j, k: (i, j),
             lambda i, j, k: (i, j + D // td)]
    d_ya, d_yb, d_ga, d_gb = _matmul(dx2b, w_o, mode="nt", name="mm_d_merged", out_dtypes=[BF16] * 4,
                                     tm=tm, tn=td, tk=D, extras=[ya, yb, proj, proj], extra_maps=gmaps,
                                     epilogue=gate_bwd_epi)
    dw_o, = _matmul(merged, dx2b, mode="tn", name="mm_dw_out", out_dtypes=[BF16], tm=td, tn=td, tk=tks)
    d_oa, = _matmul(d_ya, w_bs_t, mode="nn", name="mm_d_oa", out_dtypes=[BF16], tm=tm, tn=SWA_Q_W, tk=D)
    d_ob, = _matmul(d_yb, w_bf_t, mode="nn", name="mm_d_ob", out_dtypes=[BF16], tm=tm, tn=FOX_W, tk=D)
    dw_bs_t, = _matmul(d_ya, o_a, mode="tn", name="mm_dw_bs", out_dtypes=[BF16], tm=td, tn=SWA_Q_W, tk=tks)
    dw_bf_t, = _matmul(d_yb, o_b, mode="tn", name="mm_dw_bf", out_dtypes=[BF16], tm=td, tn=FOX_W, tk=tks)
    h_s2, tok_s2 = _exchange_start([dw_bs_t.reshape(N_DEV, D // N_DEV, SWA_Q_W),
                                    dw_bf_t.reshape(N_DEV, D // N_DEV, FOX_W), dw_o.reshape(N_DEV, D // N_DEV, D)],
                                   gather=False, name="scatter_attn_start")
    def row_blocks_t(a):
        return a.reshape(S // bbq, bbq, FOX_W).transpose(0, 2, 1)

    d_fq, d_fk, d_fv, dcol4, drow4 = _fox_bwd(proj, _key_bias_blocks(negc, bbk), o_b, lse, d_ob,
                                              row_blocks_t(proj[:, fq_off:fq_off + FOX_W]), row_blocks_t(d_ob),
                                              q_off=fq_off, k_off=fk_off, v_off=fv_off, bq=bbq, bk=bbk,
                                              name="fox_bwd", deps=[tok_s2])
    dcol = dcol4.transpose(0, 2, 1, 3).reshape(FOX_HEADS, S)
    drow = drow4.transpose(0, 2, 1, 3).reshape(FOX_HEADS, S)
    dz_t, dbias_l = _fox_post(drow, dcol, z_t, bias_col, name="fox_post")
    d_aq, dk_c, dk_p, dv_c, dv_p, dsink_l = _swa_bwd(q_rope, k_rope, proj, sinks, d_oa, cos_t, sin_t, swa_mask,
                                                     v_off=v_off, name="swa_bwd")
    d_ak, d_av = _rope_bwd(dk_c, dk_p, dv_c, dv_p, cos_t, sin_t, name="rope_bwd")
    dz_pad = jnp.pad(dz_t.T.astype(BF16), ((0, 0), (0, FL_PAD - FOX_HEADS)))
    d_proj = jnp.concatenate([d_ga, d_gb, d_aq, d_ak, d_av, d_fq, d_fk, d_fv, dz_pad], axis=1)
    tkp = _pick(NP, 2304)
    dw_in_p, = _matmul(d_proj, h1, mode="tn", name="mm_dw_in", out_dtypes=[BF16], tm=_pick(NP, 512), tn=D, tk=tks)
    dw_in_t = jnp.concatenate([dw_in_p[q_off:q_off + QKV_W], dw_in_p[fl_off:fl_off + FOX_HEADS], dw_in_p[:q_off]],
                              axis=0)
    h_s3, tok_s3 = _exchange_start([dw_in_t.reshape(N_DEV, d_in // N_DEV, D)], gather=False,
                                   name="scatter_in_start")
    dh1, = _matmul(d_proj, w_in_p, mode="nn", name="mm_dh1", out_dtypes=[BF16], tm=tm, tn=td, tk=tkp, deps=[tok_s3])
    dx, dg1 = _rms_bwd(dh1, x2d, attn_norm, dx2b, name="rms1_bwd", out_dtype=F32)

    dbias = dbias_l[:, 0]
    dsinks = dsink_l[:, :, 0].reshape(-1)
    nsm = 3 * D + 2 * LANES
    tail = jnp.zeros((2 * LANES,), F32)
    small_g = jnp.concatenate([dg1[0], dg2[0], dg3[0],
                               tail.at[0:16].set(dbias).at[16:32].set(dsinks).at[32].set(loss_part[0, 0])])

    def pack(a_norm, b_norm, f_norm, bias, snk):
        return jnp.concatenate([a_norm[0], b_norm[0], f_norm,
                                tail.at[0:16].set(bias[0]).at[16:32].set(snk[0])]).reshape(1, nsm)

    small_stack, = _exchange([small_g.reshape(1, nsm)], gather=True, name="gather_small")
    u_sm = _adamw(pack(attn_norm, mlp_norm, final_norm, fox_f_bias, swa_sinks), small_stack,
                  pack(m_attn_norm, m_mlp_norm, m_final_norm, m_fox_f_bias, m_swa_sinks),
                  pack(v_attn_norm, v_mlp_norm, v_final_norm, v_fox_f_bias, v_swa_sinks),
                  name="adamw_small", stacked=True)
    loss = u_sm[0][0, 3 * D + 32]

    def own_of(src):
        return lax.dynamic_index_in_dim(src, me, 0, keepdims=False)

    def update_t(stack, src, w, m, v, nm):
        g = _sum8(stack, own_of(src), name="sum_" + nm).T
        return _adamw(w[0], g, m[0], v[0], name="adamw_" + nm, stacked=False)

    def update(stack, src, w, m, v, nm, transposed=False):
        return _adamw(w[0], stack, m[0], v[0], name="adamw_" + nm, stacked=True, own=own_of(src),
                      transposed=transposed)

    (s_up, s_dn), (r_up, r_dn) = _exchange_wait(h_s1, u_sm[1], name="scatter_mlp_wait")
    u_up = update(r_up, s_up, w_up, m_w_up, v_w_up, "w_up", transposed=True)
    u_dn = update(r_dn, s_dn, w_down, m_w_down, v_w_down, "w_down")
    (s_bs, s_bf, s_o), (r_bs, r_bf, r_o) = _exchange_wait(h_s2, u_dn[1], name="scatter_attn_wait")
    u_bs = update(r_bs, s_bs, w_branch_swa, m_w_branch_swa, v_w_branch_swa, "w_bs", transposed=True)
    u_bf = update(r_bf, s_bf, w_branch_fox, m_w_branch_fox, v_w_branch_fox, "w_bf", transposed=True)
    u_o = update(r_o, s_o, w_out, m_w_out, v_w_out, "w_out")
    (s_w_in,), (r_in,) = _exchange_wait(h_s3, u_o[1], name="scatter_in_wait")
    u_in = update_t(r_in, s_w_in, w_in, m_w_in, v_w_in, "w_in")

    def small(kind):
        a = u_sm[kind][0]
        return dict(attn_norm=a[0:D][None], mlp_norm=a[D:2 * D][None], final_norm=a[2 * D:3 * D],
                    fox_f_bias=a[3 * D:3 * D + 16][None], swa_sinks=a[3 * D + 16:3 * D + 32][None])

    big = dict(w_in=u_in, w_branch_swa=u_bs, w_branch_fox=u_bf, w_out=u_o, w_up=u_up, w_down=u_dn)
    order = ["attn_norm", "w_in", "fox_f_bias", "swa_sinks", "w_branch_swa", "w_branch_fox", "w_out", "mlp_norm",
             "w_up", "w_down", "final_norm"]
    outs = [loss, dx[None]]
    for kind in range(4):
        sm = small(kind)
        for nm in order:
            outs.append(big[nm][kind][None] if nm in big else sm[nm])
    return tuple(outs)
```

```python
import functools

import jax
import jax.numpy as jnp
from jax import lax
from jax.experimental import pallas as pl
from jax.experimental.pallas import tpu as pltpu

F32 = jnp.float32
BF16 = jnp.bfloat16

N_DEV = 8
HEAD_DIM = 64
SWA_Q_W = 1024
SWA_KV_W = 128
SWA_GROUP = 8
WINDOW = 128
FOX_W = 1024
FOX_HEADS = 16
QKV_W = SWA_Q_W + 2 * SWA_KV_W + 3 * FOX_W
FL_PAD = 256
ROPE_THETA = 10000.0
RMS_EPS = 1e-6
ATT_SCALE = 0.125
NEG = -1e30

ADAM_LR = 0.001
ADAM_B1 = 0.9
ADAM_B2 = 0.999
ADAM_EPS = 1e-08
ADAM_WD = 0.01
ADAM_STEP = 10

FOX_FWD_BLOCKS = (1024, 1024)
FOX_BWD_BLOCKS = (1024, 512)
FOX_FWD_PAIRS = 2

LANES = 128
VMEM_LIMIT = 56 * 1024 * 1024
STEP_BYTES = 12 * 1024 * 1024


def _cparams(*sem):
    return pltpu.CompilerParams(dimension_semantics=sem, vmem_limit_bytes=VMEM_LIMIT)


def _pick(dim, pref, align=LANES):
    best = None
    t = align
    while t <= min(dim, pref):
        if dim % t == 0:
            best = t
        t += align
    return best if best is not None else dim


_DIMS = {"nn": ((1,), (0,)), "nt": ((1,), (1,)), "tn": ((0,), (0,))}


_ANY = pl.BlockSpec(memory_space=pl.ANY)


def _matmul(a, b, *, mode, name, out_dtypes, tm, tn, tk, extras=(), extra_maps=None,
            a_fn=None, epilogue=None, deps=()):
    if mode == "nn":
        (M, K), (K2, N) = a.shape, b.shape
    elif mode == "nt":
        (M, K), (N, K2) = a.shape, b.shape
    else:
        (K, M), (K2, N) = a.shape, b.shape
    assert K == K2, (name, a.shape, b.shape)
    assert M % tm == 0 and N % tn == 0 and K % tk == 0, (name, M, N, K, tm, tn, tk)
    nk = K // tk
    ne, no = len(extras), len(out_dtypes)
    dims = (_DIMS[mode], ((), ()))

    def body(*refs):
        a_ref, b_ref = refs[0], refs[1]
        ex_refs = refs[2:2 + ne]
        out_refs = refs[2 + ne + len(deps):2 + ne + len(deps) + no]

        def finish(acc):
            res = (acc,) if epilogue is None else epilogue(acc, *[e[...] for e in ex_refs])
            for o_ref, r in zip(out_refs, res):
                o_ref[...] = r.astype(o_ref.dtype)

        def product():
            av = a_ref[...]
            if a_fn is not None:
                av = a_fn(av)
            return lax.dot_general(av, b_ref[...], dims, preferred_element_type=F32)

        if nk == 1:
            finish(product())
        else:
            acc_ref = refs[-1]
            k = pl.program_id(2)

            @pl.when(k == 0)
            def _():
                acc_ref[...] = jnp.zeros_like(acc_ref)

            acc_ref[...] += product()

            @pl.when(k == nk - 1)
            def _():
                finish(acc_ref[...])

    if mode == "tn":
        a_spec = pl.BlockSpec((tk, tm), lambda i, j, k: (k, i))
    else:
        a_spec = pl.BlockSpec((tm, tk), lambda i, j, k: (i, k))
    if mode == "nt":
        b_spec = pl.BlockSpec((tn, tk), lambda i, j, k: (j, k))
    else:
        b_spec = pl.BlockSpec((tk, tn), lambda i, j, k: (k, j))
    if extra_maps is None:
        extra_maps = [lambda i, j, k: (i, j)] * ne
    ex_specs = [pl.BlockSpec((tm, tn), m) for m in extra_maps]
    out_spec = [pl.BlockSpec((tm, tn), lambda i, j, k: (i, j)) for _ in range(no)]
    res = pl.pallas_call(
        body,
        name=name,
        grid=(M // tm, N // tn, nk),
        in_specs=[a_spec, b_spec] + ex_specs + [_ANY] * len(deps),
        out_specs=out_spec,
        out_shape=[jax.ShapeDtypeStruct((M, N), d) for d in out_dtypes],
        scratch_shapes=[pltpu.VMEM((tm, tn), F32)] if nk > 1 else [],
        compiler_params=_cparams("parallel", "parallel", "arbitrary"),
    )(a, b, *extras, *deps)
    return res


def _square_bf16(t):
    tf = t.astype(F32)
    return (tf * tf).astype(BF16)


def _sigmoid(g):
    return 1.0 / (1.0 + jnp.exp(-g))


def _rms_fwd(x, gain, *, name, deps=()):
    S, D = x.shape
    tr = _pick(S, 512, 8)

    def body(x_ref, g_ref, *rest):
        h_ref = rest[-1]
        xv = x_ref[...]
        r = lax.rsqrt(jnp.mean(xv * xv, axis=-1, keepdims=True) + RMS_EPS)
        h_ref[...] = (xv * r * g_ref[...]).astype(BF16)

    return pl.pallas_call(
        body, name=name, grid=(S // tr,),
        in_specs=[pl.BlockSpec((tr, D), lambda i: (i, 0)), pl.BlockSpec((1, D), lambda i: (0, 0))] + [_ANY] * len(deps),
        out_specs=pl.BlockSpec((tr, D), lambda i: (i, 0)),
        out_shape=jax.ShapeDtypeStruct((S, D), BF16),
        compiler_params=_cparams("parallel"),
    )(x, gain, *deps)


def _rms_bwd(dh, x, gain, dres, *, name, out_dtype, deps=()):
    S, D = x.shape
    tr = _pick(S, 256, 8)

    def body(dh_ref, x_ref, g_ref, dres_ref, *rest):
        outs = rest[len(deps):]
        dx_ref, dg_ref = outs[0], outs[-1]
        xv = x_ref[...]
        r = lax.rsqrt(jnp.mean(xv * xv, axis=-1, keepdims=True) + RMS_EPS)
        xh = xv * r
        dhv = dh_ref[...].astype(F32)
        t = dhv * g_ref[...]
        dx = r * (t - xh * jnp.mean(t * xh, axis=-1, keepdims=True)) + dres_ref[...].astype(F32)
        dx_ref[...] = dx.astype(out_dtype)
        part = jnp.sum(dhv * xh, axis=0, keepdims=True)

        @pl.when(pl.program_id(0) == 0)
        def _():
            dg_ref[...] = part

        @pl.when(pl.program_id(0) > 0)
        def _():
            dg_ref[...] += part

    row = pl.BlockSpec((tr, D), lambda i: (i, 0))
    vec = pl.BlockSpec((1, D), lambda i: (0, 0))
    return pl.pallas_call(
        body, name=name, grid=(S // tr,),
        in_specs=[row, row, vec, row] + [_ANY] * len(deps), out_specs=[row, vec],
        out_shape=[jax.ShapeDtypeStruct((S, D), out_dtype), jax.ShapeDtypeStruct((1, D), F32)],
        compiler_params=_cparams("arbitrary"),
    )(dh, x, gain, dres, *deps)


def _loss_head(x3, target, gain, *, name):
    S, D = x3.shape
    tr = _pick(S, 256, 8)

    def body(x_ref, t_ref, g_ref, dxb_ref, dg_ref, loss_ref):
        xv = x_ref[...]
        r = lax.rsqrt(jnp.mean(xv * xv, axis=-1, keepdims=True) + RMS_EPS)
        xh = xv * r
        gv = g_ref[...]
        err = xh * gv - t_ref[...]
        lpart = jnp.zeros((1, LANES), F32) + (0.5 / D) * jnp.sum(err * err)
        dy = err * (1.0 / D)
        t = dy * gv
        dx = r * (t - xh * jnp.mean(t * xh, axis=-1, keepdims=True))
        dxb_ref[...] = dx.astype(BF16)
        part = jnp.sum(dy * xh, axis=0, keepdims=True)

        @pl.when(pl.program_id(0) == 0)
        def _():
            dg_ref[...] = part
            loss_ref[...] = lpart

        @pl.when(pl.program_id(0) > 0)
        def _():
            dg_ref[...] += part
            loss_ref[...] += lpart

    row = pl.BlockSpec((tr, D), lambda i: (i, 0))
    vec = pl.BlockSpec((1, D), lambda i: (0, 0))
    return pl.pallas_call(
        body, name=name, grid=(S // tr,),
        in_specs=[row, row, vec],
        out_specs=[row, vec, pl.BlockSpec((1, LANES), lambda i: (0, 0))],
        out_shape=[jax.ShapeDtypeStruct((S, D), BF16),
                   jax.ShapeDtypeStruct((1, D), F32), jax.ShapeDtypeStruct((1, LANES), F32)],
        compiler_params=_cparams("arbitrary"),
    )(x3, target, gain)


def _rope_tables(pos_col, invf, *, name):
    S = pos_col.shape[0]
    tr = _pick(S, 512, 8)

    def body(p_ref, f_ref, cos_ref, sin_ref):
        ang = p_ref[...].astype(F32) * f_ref[...]
        lane = lax.broadcasted_iota(jnp.int32, (1, LANES), 1)
        first = (lane % HEAD_DIM) < HEAD_DIM // 2
        sn = jnp.sin(ang)
        cos_ref[...] = jnp.cos(ang)
        sin_ref[...] = jnp.where(first, -sn, sn)

    return pl.pallas_call(
        body, name=name, grid=(S // tr,),
        in_specs=[pl.BlockSpec((tr, 1), lambda i: (i, 0)), pl.BlockSpec((1, LANES), lambda i: (0, 0))],
        out_specs=[pl.BlockSpec((tr, LANES), lambda i: (i, 0))] * 2,
        out_shape=[jax.ShapeDtypeStruct((S, LANES), F32)] * 2,
        compiler_params=_cparams("parallel"),
    )(pos_col, invf)


def _swap_halves(t):
    lane = lax.broadcasted_iota(jnp.int32, (1, LANES), 1)
    first = (lane % HEAD_DIM) < HEAD_DIM // 2
    return jnp.where(first, pltpu.roll(t, LANES - HEAD_DIM // 2, 1), pltpu.roll(t, HEAD_DIM // 2, 1))


def _rope_fwd(proj, cos_t, sin_t, *, q_off, k_off, name):
    S = proj.shape[0]
    tr = _pick(S, 256, 8)
    nqb = SWA_Q_W // LANES

    def body(q_ref, k_ref, c_ref, s_ref, qo_ref, ko_ref):
        cv, sv = c_ref[...], s_ref[...]
        for b in range(nqb):
            t = q_ref[:, b * LANES:(b + 1) * LANES].astype(F32)
            qo_ref[:, b * LANES:(b + 1) * LANES] = (t * cv + _swap_halves(t) * sv).astype(BF16)
        t = k_ref[...].astype(F32)
        ko_ref[...] = (t * cv + _swap_halves(t) * sv).astype(BF16)

    tab = pl.BlockSpec((tr, LANES), lambda i: (i, 0))
    return pl.pallas_call(
        body, name=name, grid=(S // tr,),
        in_specs=[pl.BlockSpec((tr, SWA_Q_W), lambda i: (i, q_off // SWA_Q_W)),
                  pl.BlockSpec((tr, LANES), lambda i: (i, k_off // LANES)), tab, tab],
        out_specs=[pl.BlockSpec((tr, SWA_Q_W), lambda i: (i, 0)), tab],
        out_shape=[jax.ShapeDtypeStruct((S, SWA_Q_W), BF16), jax.ShapeDtypeStruct((S, LANES), BF16)],
        compiler_params=_cparams("parallel"),
    )(proj, proj, cos_t, sin_t)


def _rope_bwd(dk_cur, dk_prev, dv_cur, dv_prev, cos_t, sin_t, *, name):
    S = dk_cur.shape[1]
    tr = _pick(S, 512)
    nb = S // tr

    def body(kc_ref, kp_ref, vc_ref, vp_ref, c_ref, s_ref, dko_ref, dvo_ref):
        cv, sv = c_ref[...], s_ref[...]
        row = pl.program_id(0) * tr + lax.broadcasted_iota(jnp.int32, (tr, 1), 0)
        has_next = row < S - WINDOW
        d = kc_ref[0] + kc_ref[1] + jnp.where(has_next, kp_ref[0] + kp_ref[1], 0.0)
        dko_ref[...] = (d * cv + _swap_halves(d * sv)).astype(BF16)
        dvo_ref[...] = (vc_ref[0] + vc_ref[1] + jnp.where(has_next, vp_ref[0] + vp_ref[1], 0.0)).astype(BF16)

    tab = pl.BlockSpec((tr, LANES), lambda i: (i, 0))
    cur = pl.BlockSpec((2, tr, LANES), lambda i: (0, i, 0))
    return pl.pallas_call(
        body, name=name, grid=(nb,),
        in_specs=[cur, cur, cur, cur, tab, tab],
        out_specs=[tab, tab],
        out_shape=[jax.ShapeDtypeStruct((S, LANES), BF16), jax.ShapeDtypeStruct((S, LANES), BF16)],
        compiler_params=_cparams("parallel"),
    )(dk_cur, dk_prev, dv_cur, dv_prev, cos_t, sin_t)


def _dot_nt(a, b):
    return lax.dot_general(a, b, (((1,), (1,)), ((), ())), preferred_element_type=F32)


def _dot_tn(a, b):
    return lax.dot_general(a, b, (((0,), (0,)), ((), ())), preferred_element_type=F32)


def _dot_nn(a, b):
    return lax.dot_general(a, b, (((1,), (0,)), ((), ())), preferred_element_type=F32)


def _roll_half(t):
    return pltpu.roll(t.astype(F32), HEAD_DIM, 1).astype(t.dtype)


SWA_STACK = SWA_GROUP // 2


def _swa_mask_bias():
    rows = SWA_STACK * WINDOW
    row = lax.broadcasted_iota(jnp.int32, (rows, 2 * WINDOW), 0) % WINDOW
    col = lax.broadcasted_iota(jnp.int32, (rows, 2 * WINDOW), 1)
    diff = row + WINDOW - col
    window = (diff >= 0) & (diff < WINDOW)
    return jnp.stack([jnp.where(window & (col >= WINDOW), 0.0, NEG), jnp.where(window, 0.0, NEG)]).astype(F32)


def _swa_common(kp_ref, kc_ref, vp_ref, vc_ref):
    k2 = jnp.concatenate([kp_ref[...], kc_ref[...]], axis=0)
    v2 = jnp.concatenate([vp_ref[...], vc_ref[...]], axis=0)
    k_sw, v_sw = _roll_half(k2), _roll_half(v2)
    lane = lax.broadcasted_iota(jnp.int32, (1, LANES), 1)
    half = [lane < HEAD_DIM, lane >= HEAD_DIM]
    kk = [[k2 if hk == a else k_sw for a in range(2)] for hk in range(2)]
    vv = [[v2 if hk == a else v_sw for a in range(2)] for hk in range(2)]
    return half, kk, vv


def _swa_stack(ref, hk, mask, scale=None):
    parts = []
    for t in range(SWA_STACK):
        blk = ref[:, (hk * SWA_STACK + t) * LANES:(hk * SWA_STACK + t + 1) * LANES]
        if scale is not None:
            blk = blk * jnp.asarray(scale, blk.dtype)
        parts.append(jnp.where(mask, blk, jnp.zeros_like(blk)))
    return jnp.concatenate(parts, axis=0)


def _swa_sink_column(sink_ref, hk, a):
    blk = lax.broadcasted_iota(jnp.int32, (SWA_STACK * WINDOW, 1), 0) // WINDOW
    col = jnp.zeros((SWA_STACK * WINDOW, 1), F32)
    for t in range(SWA_STACK):
        col = jnp.where(blk == t, sink_ref[hk * SWA_GROUP + 2 * t + a], col)
    return col


def _swa_probs(qm, kk, mask_bias, sink):
    s = _dot_nt(qm, kk) + mask_bias
    m = jnp.maximum(jnp.max(s, axis=1, keepdims=True), sink)
    e = jnp.exp(s - m)
    es = jnp.exp(sink - m)
    inv = 1.0 / (jnp.sum(e, axis=1, keepdims=True) + es)
    return e * inv, es * inv


def _swa_mask_spec():
    return pl.BlockSpec((1, SWA_STACK * WINDOW, 2 * WINDOW), lambda n: (jnp.minimum(n, 1), 0, 0))


def _swa_fwd(q_rope, k_rope, proj, sinks, mask_bias, *, v_off, name):
    S = q_rope.shape[0]
    nb = S // WINDOW

    def body(sink_ref, q_ref, kp_ref, kc_ref, vp_ref, vc_ref, mask_ref, o_ref):
        half, kk, vv = _swa_common(kp_ref, kc_ref, vp_ref, vc_ref)
        for hk in range(2):
            outs = []
            for a in range(2):
                qm = _swa_stack(q_ref, hk, half[a], ATT_SCALE)
                p, _ = _swa_probs(qm, kk[hk][a], mask_ref[0], _swa_sink_column(sink_ref, hk, a))
                outs.append(_dot_nn(p.astype(BF16), vv[hk][a]))
            for t in range(SWA_STACK):
                rows = slice(t * WINDOW, (t + 1) * WINDOW)
                c0 = (hk * SWA_STACK + t) * LANES
                o_ref[:, c0:c0 + LANES] = jnp.where(half[0], outs[0][rows], outs[1][rows]).astype(BF16)

    prev = lambda n: (jnp.maximum(n - 1, 0), 0)
    cur = lambda n: (n, 0)
    vprev = lambda n: (jnp.maximum(n - 1, 0), v_off // LANES)
    vcur = lambda n: (n, v_off // LANES)
    blk = lambda m: pl.BlockSpec((WINDOW, LANES), m)
    return pl.pallas_call(
        body, name=name, grid=(nb,),
        in_specs=[pl.BlockSpec(memory_space=pltpu.SMEM),
                  pl.BlockSpec((WINDOW, SWA_Q_W), lambda n: (n, 0)),
                  blk(prev), blk(cur), blk(vprev), blk(vcur), _swa_mask_spec()],
        out_specs=pl.BlockSpec((WINDOW, SWA_Q_W), lambda n: (n, 0)),
        out_shape=jax.ShapeDtypeStruct((S, SWA_Q_W), BF16),
        compiler_params=_cparams("parallel"),
    )(sinks, q_rope, k_rope, k_rope, proj, proj, mask_bias)


def _swa_bwd(q_rope, k_rope, proj, sinks, d_o, cos_t, sin_t, mask_bias, *, v_off, name):
    S = q_rope.shape[0]
    nb = S // WINDOW

    def body(sink_ref, q_ref, kp_ref, kc_ref, vp_ref, vc_ref, do_ref, c_ref, s_ref, mask_ref,
             dq_ref, dkc_ref, dkp_ref, dvc_ref, dvp_ref, dsink_ref):
        n = pl.program_id(0)
        half, kk, vv = _swa_common(kp_ref, kc_ref, vp_ref, vc_ref)
        allowed = mask_ref[0]
        cv, sv = c_ref[...], s_ref[...]
        srow = lax.broadcasted_iota(jnp.int32, (SWA_GROUP, LANES), 0)
        for hk in range(2):
            dk_acc = jnp.zeros((2 * WINDOW, LANES), F32)
            dv_acc = jnp.zeros((2 * WINDOW, LANES), F32)
            dsink = jnp.zeros((SWA_GROUP, LANES), F32)
            dqs = []
            for a in range(2):
                qm = _swa_stack(q_ref, hk, half[a], ATT_SCALE)
                dom = _swa_stack(do_ref, hk, half[a])
                p, psink = _swa_probs(qm, kk[hk][a], allowed, _swa_sink_column(sink_ref, hk, a))
                dp = _dot_nt(dom, vv[hk][a])
                delta = jnp.sum(p * dp, axis=1, keepdims=True)
                ds = (p * (dp - delta)).astype(BF16)
                dsk = psink * delta
                for t in range(SWA_STACK):
                    dsink = dsink + jnp.where(srow == 2 * t + a, -jnp.sum(dsk[t * WINDOW:(t + 1) * WINDOW]), 0.0)
                dqs.append(_dot_nn(ds, kk[hk][a]) * ATT_SCALE)
                dk_acc = dk_acc + _dot_tn(ds, qm)
                dv_acc = dv_acc + _dot_tn(p.astype(BF16), dom)
            for t in range(SWA_STACK):
                rows = slice(t * WINDOW, (t + 1) * WINDOW)
                d = jnp.where(half[0], dqs[0][rows], dqs[1][rows])
                c0 = (hk * SWA_STACK + t) * LANES
                dq_ref[:, c0:c0 + LANES] = (d * cv + _swap_halves(d * sv)).astype(BF16)
            dk_t = jnp.where(half[hk], dk_acc + pltpu.roll(dk_acc, HEAD_DIM, 1), 0.0)
            dv_t = jnp.where(half[hk], dv_acc + pltpu.roll(dv_acc, HEAD_DIM, 1), 0.0)
            dkp_ref[hk] = dk_t[:WINDOW]
            dkc_ref[hk] = dk_t[WINDOW:]
            dvp_ref[hk] = dv_t[:WINDOW]
            dvc_ref[hk] = dv_t[WINDOW:]

            @pl.when(n == 0)
            def _():
                dsink_ref[hk] = dsink

            @pl.when(n > 0)
            def _():
                dsink_ref[hk] += dsink

    prev = lambda n: (jnp.maximum(n - 1, 0), 0)
    cur = lambda n: (n, 0)
    vprev = lambda n: (jnp.maximum(n - 1, 0), v_off // LANES)
    vcur = lambda n: (n, v_off // LANES)
    blk = lambda m: pl.BlockSpec((WINDOW, LANES), m)
    qblk = pl.BlockSpec((WINDOW, SWA_Q_W), lambda n: (n, 0))
    part = pl.BlockSpec((2, WINDOW, LANES), lambda n: (0, n, 0))
    part_prev = pl.BlockSpec((2, WINDOW, LANES), lambda n: (0, jnp.maximum(n - 1, 0), 0))
    part_shape = jax.ShapeDtypeStruct((2, S, LANES), F32)
    return pl.pallas_call(
        body, name=name, grid=(nb,),
        in_specs=[pl.BlockSpec(memory_space=pltpu.SMEM), qblk, blk(prev), blk(cur), blk(vprev), blk(vcur), qblk,
                  blk(cur), blk(cur), _swa_mask_spec()],
        out_specs=[qblk, part, part_prev, part, part_prev,
                   pl.BlockSpec((2, SWA_GROUP, LANES), lambda n: (0, 0, 0))],
        out_shape=[jax.ShapeDtypeStruct((S, SWA_Q_W), BF16), part_shape, part_shape, part_shape, part_shape,
                   jax.ShapeDtypeStruct((2, SWA_GROUP, LANES), F32)],
        compiler_params=_cparams("arbitrary"),
    )(sinks, q_rope, k_rope, k_rope, proj, proj, d_o, cos_t, sin_t, mask_bias)


def _fox_prep(z_t, bias_col, *, name):
    H, S = z_t.shape
    tb = _pick(S, 512)

    def body(z_ref, b_ref, o_ref, carry_ref):
        @pl.when(pl.program_id(0) == 0)
        def _():
            carry_ref[...] = jnp.zeros_like(carry_ref)

        zz = z_ref[...] + b_ref[...]
        t = jnp.exp(-jnp.abs(zz))
        log1p = jnp.where(t < 1e-2, t * (1.0 - t * (0.5 - t * (1.0 / 3.0))), jnp.log(1.0 + t))
        logf = jnp.minimum(zz, 0.0) - log1p
        r = lax.broadcasted_iota(jnp.int32, (tb, tb), 0)
        c = lax.broadcasted_iota(jnp.int32, (tb, tb), 1)
        tri = (r <= c).astype(BF16)
        hi = logf.astype(BF16)
        r1 = logf - hi.astype(F32)
        mid = r1.astype(BF16)
        lo = (r1 - mid.astype(F32)).astype(BF16)
        cs = _dot_nn(hi, tri) + _dot_nn(mid, tri) + _dot_nn(lo, tri) + carry_ref[:, 0:1]
        o_ref[...] = -cs
        carry_ref[...] = jnp.zeros_like(carry_ref) + cs[:, tb - 1:tb]

    return pl.pallas_call(
        body, name=name, grid=(S // tb,),
        in_specs=[pl.BlockSpec((H, tb), lambda i: (0, i)), pl.BlockSpec((H, 1), lambda i: (0, 0))],
        out_specs=pl.BlockSpec((H, tb), lambda i: (0, i)),
        out_shape=jax.ShapeDtypeStruct((H, S), F32),
        scratch_shapes=[pltpu.VMEM((H, LANES), F32)],
        compiler_params=_cparams("arbitrary"),
    )(z_t, bias_col)


def _fox_post(drow, dcol, z_t, bias_col, *, name):
    H, S = z_t.shape
    tb = _pick(S, 512)
    nb = S // tb

    def body(dr_ref, d_ref, z_ref, b_ref, dz_ref, db_ref, carry_ref):
        @pl.when(pl.program_id(0) == 0)
        def _():
            carry_ref[...] = jnp.zeros_like(carry_ref)
            db_ref[...] = jnp.zeros_like(db_ref)

        dc = dr_ref[...] - d_ref[...]
        r = lax.broadcasted_iota(jnp.int32, (tb, tb), 0)
        c = lax.broadcasted_iota(jnp.int32, (tb, tb), 1)
        tri = (r >= c).astype(BF16)
        hi = dc.astype(BF16)
        r1 = dc - hi.astype(F32)
        mid = r1.astype(BF16)
        lo = (r1 - mid.astype(F32)).astype(BF16)
        dlogf = _dot_nn(hi, tri) + _dot_nn(mid, tri) + _dot_nn(lo, tri) + carry_ref[:, 0:1]
        carry_ref[...] = jnp.zeros_like(carry_ref) + dlogf[:, 0:1]
        dz = dlogf * _sigmoid(-(z_ref[...] + b_ref[...]))
        dz_ref[...] = dz
        db_ref[...] += jnp.sum(dz, axis=1, keepdims=True)

    rev = lambda i: (0, nb - 1 - i)
    return pl.pallas_call(
        body, name=name, grid=(nb,),
        in_specs=[pl.BlockSpec((H, tb), rev), pl.BlockSpec((H, tb), rev), pl.BlockSpec((H, tb), rev),
                  pl.BlockSpec((H, 1), lambda i: (0, 0))],
        out_specs=[pl.BlockSpec((H, tb), rev), pl.BlockSpec((H, LANES), lambda i: (0, 0))],
        out_shape=[jax.ShapeDtypeStruct((H, S), F32), jax.ShapeDtypeStruct((H, LANES), F32)],
        scratch_shapes=[pltpu.VMEM((H, LANES), F32)],
        compiler_params=_cparams("arbitrary"),
    )(drow, dcol, z_t, bias_col)


def _fox_blocks(S):
    cap = max(LANES, S // 4)
    return (min(FOX_FWD_BLOCKS[0], cap), min(FOX_FWD_BLOCKS[1], cap)), \
           (min(FOX_BWD_BLOCKS[0], cap), min(FOX_BWD_BLOCKS[1], cap))


def _key_bias_blocks(negc, bk):
    H, S = negc.shape
    return negc.reshape(H // 2, 2, S // bk, bk).transpose(0, 2, 1, 3)


def _fox_fwd(proj, negc4, *, q_off, k_off, v_off, bq, bk, name):
    S = proj.shape[0]
    nq, nk = S // bq, S // bk
    npair = FOX_HEADS // 2
    assert bq % bk == 0 or bk % bq == 0
    nmask = max(1, bq // bk)

    gp = FOX_FWD_PAIRS
    gw = gp * LANES
    assert q_off % gw == 0 and k_off % gw == 0 and v_off % gw == 0 and npair % gp == 0

    def body(q_ref, k_ref, v_ref, nc_ref, o_ref, lse_ref):
        i = pl.program_id(1)
        lane = lax.broadcasted_iota(jnp.int32, (1, LANES), 1)
        half = [lane < HEAD_DIM, lane >= HEAD_DIM]
        qh = []
        for g in range(gp):
            q2 = q_ref[:, g * LANES:(g + 1) * LANES] * jnp.asarray(ATT_SCALE, BF16)
            qh += [jnp.where(half[h], q2, jnp.zeros_like(q2)) for h in range(2)]
        row = lax.broadcasted_iota(jnp.int32, (bq, bk), 0)
        col = lax.broadcasted_iota(jnp.int32, (bq, bk), 1)
        rel = row - col
        nfull = (i * bq) // bk

        spare = [HEAD_DIM, 0]
        ones_lane = [lane == spare[h] for h in range(2)]

        def step(j, carry, masked):
            start = pl.multiple_of(j * bk, bk)
            new = []
            for g in range(gp):
                ks = k_ref[pl.ds(start, bk), g * LANES:(g + 1) * LANES]
                vs = v_ref[pl.ds(start, bk), g * LANES:(g + 1) * LANES]
                nb = nc_ref[g, j]
                for h in range(2):
                    m, acc = carry[4 * g + 2 * h:4 * g + 2 * h + 2]
                    vh = jnp.where(half[h], vs, jnp.where(ones_lane[h], jnp.ones_like(vs), jnp.zeros_like(vs)))
                    qs, bias = qh[2 * g + h], nb[h:h + 1, :]

                    def update(m, acc, rows, keys):
                        s = _dot_nt(qs[rows], ks[keys]) + bias[:, keys]
                        if masked:
                            s = jnp.where(rel[rows, keys] >= j * bk - i * bq, s, NEG)
                        m_new = jnp.maximum(m[rows], jnp.max(s, axis=1, keepdims=True))
                        p = jnp.exp(s - m_new).astype(BF16)
                        return m_new, jnp.exp(m[rows] - m_new) * acc[rows] + _dot_nn(p, vh[keys])

                    if masked and bq == bk:
                        top, bot, everything = slice(0, bq // 2), slice(bq // 2, bq), slice(0, bk)
                        m_t, acc_t = update(m, acc, top, top)
                        m_b, acc_b = update(m, acc, bot, everything)
                        new += [jnp.concatenate([m_t, m_b], axis=0), jnp.concatenate([acc_t, acc_b], axis=0)]
                    else:
                        new += list(update(m, acc, slice(0, bq), slice(0, bk)))
            return tuple(new)

        init = (jnp.full((bq, 1), NEG, F32), jnp.zeros((bq, LANES), F32)) * (2 * gp)
        carry = lax.fori_loop(0, nfull, lambda j, c: step(j, c, False), init)
        for t in range(nmask):
            carry = step(nfull + t, carry, True)
        for g in range(gp):
            outs, lses = [], []
            for h in range(2):
                m, acc = carry[4 * g + 2 * h:4 * g + 2 * h + 2]
                l = acc[:, spare[h]:spare[h] + 1]
                outs.append(acc * (1.0 / l))
                lses.append(m + jnp.log(l))
            o_ref[:, g * LANES:(g + 1) * LANES] = jnp.where(half[0], outs[0], outs[1]).astype(BF16)
            lse_ref[g] = jnp.where(half[0], lses[0], lses[1])

    seq = lambda off: pl.BlockSpec((S, gw), lambda hp, i: (0, off // gw + hp))
    return pl.pallas_call(
        body, name=name, grid=(npair // gp, nq),
        in_specs=[pl.BlockSpec((bq, gw), lambda hp, i: (i, q_off // gw + hp)), seq(k_off), seq(v_off),
                  pl.BlockSpec((gp, nk, 2, bk), lambda hp, i: (hp, 0, 0, 0))],
        out_specs=[pl.BlockSpec((bq, gw), lambda hp, i: (i, hp)),
                   pl.BlockSpec((gp, bq, LANES), lambda hp, i: (hp, i, 0))],
        out_shape=[jax.ShapeDtypeStruct((S, FOX_W), BF16), jax.ShapeDtypeStruct((npair, S, LANES), F32)],
        compiler_params=_cparams("parallel", "parallel"),
    )(proj, proj, proj, negc4)


def _fox_bwd(proj, negc4, o, lse, d_o, q_t, do_t, *, q_off, k_off, v_off, bq, bk, name, deps=()):
    S = proj.shape[0]
    nq, nk = S // bq, S // bk
    npair = FOX_HEADS // 2
    assert bq % bk == 0 or bk % bq == 0
    nmask = max(1, bk // bq)

    def body(q_ref, k_ref, v_ref, nc_ref, o_ref, lse_ref, do_ref, qt_ref, dot_ref, *rest):
        dqo_ref, dk_ref, dv_ref, dn_ref, dr_ref, delta_ref, rs_ref, dq_ref = rest[len(deps):]
        j = pl.program_id(1)
        lane = lax.broadcasted_iota(jnp.int32, (1, LANES), 1)
        half = [lane < HEAD_DIM, lane >= HEAD_DIM]
        spare = [HEAD_DIM, 0]
        ones_lane = [lane == spare[h] for h in range(2)]
        srow = lax.broadcasted_iota(jnp.int32, (LANES, 1), 0)
        rhalf = [srow < HEAD_DIM, srow >= HEAD_DIM]
        ones_row = [srow == spare[h] for h in range(2)]
        k2, v2 = k_ref[...], v_ref[...]
        one_k = jnp.ones_like(k2)
        kh = [jnp.where(half[h], k2, jnp.where(ones_lane[h], one_k, jnp.zeros_like(k2))) for h in range(2)]
        nb = nc_ref[0, 0]
        row = lax.broadcasted_iota(jnp.int32, (bq, bk), 0)
        col = lax.broadcasted_iota(jnp.int32, (bq, bk), 1)
        rel = row - col
        i_first = (j * bk) // bq

        @pl.when(j == 0)
        def _():
            dq_ref[...] = jnp.zeros_like(dq_ref)
            rs_ref[...] = jnp.zeros_like(rs_ref)
            for b in range(nq):
                prod = do_ref[b * bq:(b + 1) * bq, :].astype(F32) * o_ref[b * bq:(b + 1) * bq, :].astype(F32)
                d0 = jnp.sum(jnp.where(half[0], prod, 0.0), axis=1, keepdims=True)
                d1 = jnp.sum(jnp.where(half[1], prod, 0.0), axis=1, keepdims=True)
                delta_ref[b * bq:(b + 1) * bq, :] = jnp.where(half[0], d0, d1)

        def step(i, carry, masked, r0=0):
            dkt_a, dkt_b, dvt = carry
            dkts = [dkt_a, dkt_b]
            nr = bq - r0
            start = pl.multiple_of(i * bq + r0, LANES)
            q2 = q_ref[pl.ds(start, nr), :] * jnp.asarray(ATT_SCALE, BF16)
            do2 = do_ref[pl.ds(start, nr), :]
            qt = qt_ref[i][:, r0:] * jnp.asarray(ATT_SCALE, BF16)
            dot = dot_ref[i][:, r0:]
            lse2 = lse_ref[0, pl.ds(start, nr), :]
            del2 = delta_ref[pl.ds(start, nr), :]
            dqf = []
            for h in range(2):
                qm = jnp.where(half[h], q2, jnp.zeros_like(q2))
                dom = jnp.where(half[h], do2, jnp.zeros_like(do2))
                qtm = jnp.where(rhalf[h], qt, jnp.where(ones_row[h], jnp.ones_like(qt), jnp.zeros_like(qt)))
                dotm = jnp.where(rhalf[h], dot, jnp.zeros_like(dot))
                c0 = h * HEAD_DIM
                p = jnp.exp(_dot_nt(qm, k2) + nb[h:h + 1, :] - lse2[:, c0:c0 + 1])
                if masked:
                    p = jnp.where(rel[r0:] >= j * bk - i * bq, p, 0.0)
                dp = _dot_nt(dom, v2)
                dsb = (p * (dp - del2[:, c0:c0 + 1])).astype(BF16)
                dvt = dvt + _dot_nn(dotm, p.astype(BF16))
                dkts[h] = dkts[h] + _dot_nn(qtm, dsb)
                dqf.append(_dot_nn(dsb, kh[h]))
            dq_ref[pl.ds(start, nr), :] += jnp.where(half[0], dqf[0], dqf[1]) * ATT_SCALE
            rs_ref[pl.ds(start, nr), :] += jnp.where(ones_lane[0], dqf[0], jnp.where(ones_lane[1], dqf[1], 0.0))
            return dkts[0], dkts[1], dvt

        zero = jnp.zeros((LANES, bk), F32)
        carry = (zero, zero, zero)
        if bq > bk:
            sp = j % (bq // bk)
            carry = lax.switch(sp, [functools.partial(step, i_first, masked=True, r0=s * bk)
                                    for s in range(bq // bk)], carry)
        else:
            for t in range(nmask):
                carry = step(i_first + t, carry, True)
        dkt_a, dkt_b, dvt = lax.fori_loop(i_first + nmask, nq, lambda i, c: step(i, c, False), carry)
        dk_ref[...] = jnp.where(rhalf[0], dkt_a, dkt_b).T.astype(BF16)
        dv_ref[...] = dvt.T.astype(BF16)
        dn_ref[0, 0] = jnp.concatenate([dkt_a[spare[0]:spare[0] + 1], dkt_b[spare[1]:spare[1] + 1]], axis=0)

        @pl.when(j == nk - 1)
        def _():
            dqo_ref[...] = dq_ref[...].astype(BF16)
            for b in range(nq):
                t = rs_ref[b * bq:(b + 1) * bq, :].T
                dr_ref[0, b] = jnp.concatenate([t[spare[0]:spare[0] + 1], t[spare[1]:spare[1] + 1]], axis=0)

    once = pl.Buffered(1)
    seq = lambda off: pl.BlockSpec((S, LANES), lambda hp, j: (0, off // LANES + hp), pipeline_mode=once)
    blk = lambda off: pl.BlockSpec((bk, LANES), lambda hp, j: (j, off // LANES + hp))
    nc = pl.BlockSpec((1, 1, 2, bk), lambda hp, j: (hp, j, 0, 0))
    tsp = pl.BlockSpec((nq, LANES, bq), lambda hp, j: (0, hp, 0), pipeline_mode=once)
    return pl.pallas_call(
        body, name=name, grid=(npair, nk),
        in_specs=[seq(q_off), blk(k_off), blk(v_off), nc, seq(0),
                  pl.BlockSpec((1, S, LANES), lambda hp, j: (hp, 0, 0), pipeline_mode=once), seq(0),
                  tsp, tsp] + [_ANY] * len(deps),
        out_specs=[pl.BlockSpec((S, LANES), lambda hp, j: (0, hp)), blk(0), blk(0), nc,
                   pl.BlockSpec((1, nq, 2, bq), lambda hp, j: (hp, 0, 0, 0))],
        out_shape=[jax.ShapeDtypeStruct((S, FOX_W), BF16), jax.ShapeDtypeStruct((S, FOX_W), BF16),
                   jax.ShapeDtypeStruct((S, FOX_W), BF16), jax.ShapeDtypeStruct((npair, nk, 2, bk), F32),
                   jax.ShapeDtypeStruct((npair, nq, 2, bq), F32)],
        scratch_shapes=[pltpu.VMEM((S, LANES), F32), pltpu.VMEM((S, LANES), F32), pltpu.VMEM((S, LANES), F32)],
        compiler_params=_cparams("parallel", "arbitrary"),
    )(proj, proj, proj, negc4, o, lse, d_o, q_t, do_t, *deps)


def _exchange(arrs, *, gather, name):
    n = len(arrs)
    npeer = N_DEV - 1

    def body(*refs):
        ins, outs = refs[:n], refs[n:2 * n]
        send_sems, recv_sems, loc_sems = refs[2 * n:]
        x, y, c = lax.axis_index("x"), lax.axis_index("y"), lax.axis_index("c")
        me = 4 * x + 2 * y + c
        peers = []
        for k in range(1, N_DEV):
            px = 1 - x if k & 4 else x
            py = 1 - y if k & 2 else y
            pc = 1 - c if k & 1 else c
            peers.append(((px, py, pc), 4 * px + 2 * py + pc))

        def remote(w, k):
            dev, idx = peers[k]
            src = ins[w] if gather else ins[w].at[idx]
            return pltpu.make_async_remote_copy(
                src_ref=src, dst_ref=outs[w].at[me],
                send_sem=send_sems.at[w * npeer + k], recv_sem=recv_sems.at[w * npeer + k],
                device_id=dev, device_id_type=pl.DeviceIdType.MESH)

        def arrival(w, k):
            dev, idx = peers[k]
            src = ins[w] if gather else ins[w].at[idx]
            return pltpu.make_async_remote_copy(
                src_ref=src, dst_ref=outs[w].at[idx],
                send_sem=send_sems.at[w * npeer + k], recv_sem=recv_sems.at[w * npeer + k],
                device_id=dev, device_id_type=pl.DeviceIdType.MESH)

        local = []
        for w in range(n):
            for k in range(npeer):
                remote(w, k).start()
            cp = pltpu.make_async_copy(ins[w] if gather else ins[w].at[me], outs[w].at[me], loc_sems.at[w])
            cp.start()
            local.append(cp)
        for w in range(n):
            for k in range(npeer):
                arrival(w, k).wait_recv()
        for w in range(n):
            for k in range(npeer):
                remote(w, k).wait_send()
            local[w].wait()

    hbm = pl.BlockSpec(memory_space=pl.ANY)
    out_shape = [jax.ShapeDtypeStruct((N_DEV,) + (a.shape if gather else a.shape[1:]), a.dtype) for a in arrs]
    return pl.pallas_call(
        body, name=name,
        in_specs=[hbm] * n, out_specs=[hbm] * n, out_shape=out_shape,
        scratch_shapes=[pltpu.SemaphoreType.DMA((n * npeer,)), pltpu.SemaphoreType.DMA((n * npeer,)),
                        pltpu.SemaphoreType.DMA((n,))],
        compiler_params=pltpu.CompilerParams(has_side_effects=True),
    )(*arrs)


def _gather_two_level(shard, *, name):
    def body(x_ref, out_ref, send_sems, recv_sems, local_sem):
        x, y, c = lax.axis_index("x"), lax.axis_index("y"), lax.axis_index("c")
        me, sibling = (x, y, c), (x, y, 1 - c)
        chips = [(1 - x, y), (x, 1 - y), (1 - x, 1 - y)]

        def slot(px, py, pc):
            return out_ref.at[4 * px + 2 * py + pc]

        def copy(k, block, to, src=None):
            return pltpu.make_async_remote_copy(
                src_ref=slot(*block) if src is None else src, dst_ref=slot(*block),
                send_sem=send_sems.at[k], recv_sem=recv_sems.at[k],
                device_id=to, device_id_type=pl.DeviceIdType.MESH)

        mine = pltpu.make_async_copy(x_ref, slot(*me), local_sem)
        mine.start()
        first = [copy(0, me, sibling, src=x_ref)]
        first += [copy(1 + j, me, (*chip, c), src=x_ref) for j, chip in enumerate(chips)]
        for cp in first:
            cp.start()
        passed = [copy(4 + j, (*chip, c), sibling) for j, chip in enumerate(chips)]
        for j, chip in enumerate(chips):
            copy(1 + j, (*chip, c), me).wait_recv()
            passed[j].start()
        copy(0, sibling, me).wait_recv()
        for j, chip in enumerate(chips):
            copy(4 + j, (*chip, 1 - c), me).wait_recv()
        for cp in first + passed:
            cp.wait_send()
        mine.wait()

    return pl.pallas_call(
        body, name=name,
        in_specs=[_ANY], out_specs=_ANY,
        out_shape=jax.ShapeDtypeStruct((N_DEV,) + shard.shape, shard.dtype),
        scratch_shapes=[pltpu.SemaphoreType.DMA((N_DEV - 1,)), pltpu.SemaphoreType.DMA((N_DEV - 1,)),
                        pltpu.SemaphoreType.DMA],
        compiler_params=pltpu.CompilerParams(has_side_effects=True),
    )(shard)


_HBM = pl.BlockSpec(memory_space=pltpu.HBM)
_SEM = pl.BlockSpec(memory_space=pltpu.SEMAPHORE)
_EFFECT = pltpu.SideEffectType.DATAFLOW_SIDE_EFFECTING
NPEER = N_DEV - 1


def _peer_table():
    x, y, c = lax.axis_index("x"), lax.axis_index("y"), lax.axis_index("c")
    peers = []
    for k in range(1, N_DEV):
        px = 1 - x if k & 4 else x
        py = 1 - y if k & 2 else y
        pc = 1 - c if k & 1 else c
        peers.append(((px, py, pc), 4 * px + 2 * py + pc))
    return 4 * x + 2 * y + c, peers


def _split_copy(ins, lands, send_sems, recv_sems, gather, me, peers, w, k, arriving):
    dev, idx = peers[k]
    return pltpu.make_async_remote_copy(
        src_ref=ins[w] if gather else ins[w].at[idx],
        dst_ref=lands[w].at[idx if arriving else me],
        send_sem=send_sems.at[w * NPEER + k], recv_sem=recv_sems.at[w * NPEER + k],
        device_id=dev, device_id_type=pl.DeviceIdType.MESH)


def _exchange_start(arrs, *, gather, name, deps=()):
    n = len(arrs)
    land_shapes = [(N_DEV,) + (a.shape if gather else a.shape[1:]) for a in arrs]

    def body(*refs):
        ins, lands = refs[:n], refs[n:2 * n]
        send_sems, recv_sems = refs[2 * n + len(deps)], refs[2 * n + len(deps) + 1]
        token = refs[-1]
        me, peers = _peer_table()
        for w in range(n):
            for k in range(NPEER):
                _split_copy(ins, lands, send_sems, recv_sems, gather, me, peers, w, k, False).start()
        token[...] = jnp.zeros_like(token)

    out_shape = ([pltpu.SemaphoreType.DMA((n * NPEER,)), pltpu.SemaphoreType.DMA((n * NPEER,))]
                 + [pltpu.HBM(a.shape, a.dtype) for a in arrs]
                 + [pltpu.HBM(s, a.dtype) for s, a in zip(land_shapes, arrs)]
                 + [jax.ShapeDtypeStruct((8, LANES), F32)])
    res = pl.pallas_call(
        body, name=name,
        in_specs=[_HBM] * (2 * n) + [_ANY] * len(deps),
        out_specs=[_SEM, _SEM] + [_HBM] * (2 * n) + [pl.BlockSpec(memory_space=pltpu.VMEM)],
        out_shape=out_shape,
        input_output_aliases={i: 2 + i for i in range(2 * n)},
        compiler_params=pltpu.CompilerParams(has_side_effects=_EFFECT),
    )(*[pltpu.with_memory_space_constraint(a, pltpu.HBM) for a in arrs],
      *[pltpu.with_memory_space_constraint(lax.empty(s, a.dtype), pltpu.HBM) for s, a in zip(land_shapes, arrs)],
      *deps)
    return (n, gather, res[0], res[1], res[2:2 + n], res[2 + n:2 + 2 * n]), res[-1]


def _exchange_wait(handle, after, *, name):
    n, gather, send_sems, recv_sems, ins_thru, lands_thru = handle

    def body(*refs):
        ins, lands = refs[:n], refs[n:2 * n]
        send_s, recv_s = refs[2 * n], refs[2 * n + 1]
        me, peers = _peer_table()
        for w in range(n):
            for k in range(NPEER):
                _split_copy(ins, lands, send_s, recv_s, gather, me, peers, w, k, False).wait_send()
                _split_copy(ins, lands, send_s, recv_s, gather, me, peers, w, k, True).wait_recv()

    res = pl.pallas_call(
        body, name=name,
        in_specs=[_HBM] * (2 * n) + [_SEM, _SEM, pl.BlockSpec(memory_space=pl.ANY)],
        out_specs=[_HBM] * (2 * n),
        out_shape=[pltpu.HBM(a.shape, a.dtype) for a in list(ins_thru) + list(lands_thru)],
        input_output_aliases={i: i for i in range(2 * n)},
        compiler_params=pltpu.CompilerParams(has_side_effects=_EFFECT),
    )(*ins_thru, *lands_thru, send_sems, recv_sems, after)
    return res[:n], res[n:2 * n]


def _ordered_sum(s_ref, own_ref):
    if own_ref is None:
        blocks = [s_ref[q].astype(F32) for q in range(N_DEV)]
    else:
        me = 4 * lax.axis_index("x") + 2 * lax.axis_index("y") + lax.axis_index("c")
        own = own_ref[...]
        blocks = [jnp.where(me == q, own, s_ref[q]).astype(F32) for q in range(N_DEV)]
    acc = blocks[0]
    for b in blocks[1:]:
        acc = acc + b
    return acc


def _sum8(stack, own, *, name):
    _, R, C = stack.shape
    if R % 8 == 0:
        tr, tc = _pick(R, max(8, STEP_BYTES // (C * 4 * (N_DEV + 2))), 8), C
    else:
        tr, tc = R, _pick(C, max(LANES, STEP_BYTES // (R * 4 * (N_DEV + 2))))

    def body(s_ref, own_ref, o_ref):
        o_ref[...] = _ordered_sum(s_ref, own_ref)

    blk = pl.BlockSpec((tr, tc), lambda i, j: (i, j))
    return pl.pallas_call(
        body, name=name, grid=(R // tr, C // tc),
        in_specs=[pl.BlockSpec((N_DEV, tr, tc), lambda i, j: (0, i, j)), blk],
        out_specs=blk,
        out_shape=jax.ShapeDtypeStruct((R, C), F32),
        compiler_params=_cparams("parallel", "parallel"),
    )(stack, own)


def _adamw_math(w, g, m, v):
    m = ADAM_B1 * m + (1.0 - ADAM_B1) * g
    v = ADAM_B2 * v + (1.0 - ADAM_B2) * (g * g)
    m_hat = m / (1.0 - ADAM_B1 ** ADAM_STEP)
    v_hat = v / (1.0 - ADAM_B2 ** ADAM_STEP)
    delta = -ADAM_LR * (m_hat / (jnp.sqrt(v_hat) + ADAM_EPS) + ADAM_WD * w)
    return delta, m, v


def _adamw(w, g, m, v, *, name, stacked, own=None, transposed=False):
    R, C = w.shape
    if transposed:
        tr = _pick(R, max(LANES, STEP_BYTES // (C * 4 * (9 + N_DEV))))
    else:
        tr = _pick(R, max(8, STEP_BYTES // (C * 4 * (8 + (N_DEV if stacked else 1)))), 8)
    has_own = own is not None

    def body(w_ref, g_ref, m_ref, v_ref, *rest):
        go_ref, d_ref, mo_ref, vo_ref = rest[-4:]
        g = _ordered_sum(g_ref, rest[0] if has_own else None) if stacked else g_ref[...]
        if transposed:
            g = g.T
        delta, m2, v2 = _adamw_math(w_ref[...], g, m_ref[...], v_ref[...])
        go_ref[...] = g
        d_ref[...] = delta
        mo_ref[...] = m2
        vo_ref[...] = v2

    row = pl.BlockSpec((tr, C), lambda i: (i, 0))
    if transposed:
        g_spec, own_spec = pl.BlockSpec((N_DEV, C, tr), lambda i: (0, 0, i)), pl.BlockSpec((C, tr), lambda i: (0, i))
    else:
        g_spec, own_spec = (pl.BlockSpec((N_DEV, tr, C), lambda i: (0, i, 0)) if stacked else row), row
    return pl.pallas_call(
        body, name=name, grid=(R // tr,),
        in_specs=[row, g_spec, row, row] + [own_spec] * has_own, out_specs=[row] * 4,
        out_shape=[jax.ShapeDtypeStruct((R, C), F32)] * 4,
        compiler_params=_cparams("parallel"),
    )(w, g, m, v, *([own] if has_own else []))


def kernel(x, positions, attn_norm, w_in, fox_f_bias, swa_sinks, w_branch_swa, w_branch_fox, w_out, mlp_norm, w_up, w_down, final_norm, loss_target, m_attn_norm, m_w_in, m_fox_f_bias, m_swa_sinks, m_w_branch_swa, m_w_branch_fox, m_w_out, m_mlp_norm, m_w_up, m_w_down, m_final_norm, v_attn_norm, v_w_in, v_fox_f_bias, v_swa_sinks, v_w_branch_swa, v_w_branch_fox, v_w_out, v_mlp_norm, v_w_up, v_w_down, v_final_norm):
    S, D = x.shape[1], x.shape[2]
    DFF = w_up.shape[2] * N_DEV
    d_in = w_in.shape[2] * N_DEV
    assert d_in == QKV_W + FOX_HEADS + 2 * D and (2 * D) % SWA_Q_W == 0 and S % (4 * LANES) == 0
    q_off = 2 * D
    k_off = q_off + SWA_Q_W
    v_off = k_off + SWA_KV_W
    fq_off = v_off + SWA_KV_W
    fk_off = fq_off + FOX_W
    fv_off = fk_off + FOX_W
    fl_off = fv_off + FOX_W
    NP = fl_off + FL_PAD
    x2d, tgt = x[0], loss_target[0]

    shards = [w_in[0].T.astype(BF16), w_branch_swa[0].T.astype(BF16), w_branch_fox[0].T.astype(BF16),
              w_out[0].astype(BF16), w_up[0].T.astype(BF16), w_down[0].astype(BF16)]
    me = 4 * lax.axis_index("x") + 2 * lax.axis_index("y") + lax.axis_index("c")

    def filled(stack, own):
        return lax.dynamic_update_slice(stack, own[None], (me,) + (0,) * own.ndim)

    g_in = _gather_two_level(shards[0], name="gather_w_in")
    h_rest, tok_rest = _exchange_start(shards[1:], gather=True, name="gather_rest_start", deps=[g_in])

    tm = _pick(S, 1024)
    td = _pick(D, 1024)
    tf = _pick(DFF, 1024)
    tnp = _pick(NP, 1024)

    h1 = _rms_fwd(x2d, attn_norm, name="rms1", deps=[tok_rest])
    w_in_t = g_in.reshape(d_in, D)
    w_in_p = jnp.concatenate([w_in_t[QKV_W + FOX_HEADS:], w_in_t[:QKV_W], w_in_t[QKV_W:QKV_W + FOX_HEADS],
                              jnp.zeros((FL_PAD - FOX_HEADS, D), BF16)], axis=0)
    w_fl_t = w_in_t[QKV_W:QKV_W + FOX_HEADS]
    proj, = _matmul(h1, w_in_p, mode="nt", name="mm_in", out_dtypes=[BF16], tm=_pick(S, 2048), tn=tnp, tk=D)
    z_sd, = _matmul(h1, w_fl_t, mode="nt", name="mm_flogit", out_dtypes=[F32], tm=tm, tn=FOX_HEADS, tk=D)
    z_t = z_sd.T
    bias_col = fox_f_bias.reshape(FOX_HEADS, 1)
    negc = _fox_prep(z_t, bias_col, name="fox_prep")
    (fbq, fbk), (bbq, bbk) = _fox_blocks(S)
    inv_freq = ROPE_THETA ** (-jnp.arange(0, HEAD_DIM, 2, dtype=F32) / HEAD_DIM)
    invf = jnp.tile(inv_freq, LANES // (HEAD_DIM // 2)).reshape(1, LANES)
    cos_t, sin_t = _rope_tables(positions.reshape(S, 1), invf, name="rope_tables")
    q_rope, k_rope = _rope_fwd(proj, cos_t, sin_t, q_off=q_off, k_off=k_off, name="rope_fwd")
    sinks = swa_sinks.reshape(-1)
    swa_mask = _swa_mask_bias()
    o_a = _swa_fwd(q_rope, k_rope, proj, sinks, swa_mask, v_off=v_off, name="swa_fwd")
    o_b, lse = _fox_fwd(proj, _key_bias_blocks(negc, fbk), q_off=fq_off, k_off=fk_off, v_off=fv_off,
                        bq=fbq, bk=fbk, name="fox_fwd")
    s_rest, g_rest = _exchange_wait(h_rest, o_b, name="gather_rest_wait")
    g_bs, g_bf, g_o, g_up, g_dn = [filled(g, s) for g, s in zip(g_rest, s_rest)]
    w_bs_t = g_bs.reshape(D, SWA_Q_W)
    w_bf_t = g_bf.reshape(D, FOX_W)
    w_o = g_o.reshape(D, D)
    w_up_t = g_up.reshape(DFF, D)
    w_dn = g_dn.reshape(DFF, D)
    ya, = _matmul(o_a, w_bs_t, mode="nt", name="mm_branch_swa", out_dtypes=[BF16], tm=tm, tn=td, tk=SWA_Q_W)
    gate_maps = [lambda i, j, k: (i, j), lambda i, j, k: (i, j), lambda i, j, k: (i, j + D // td)]

    def merge_epi(acc, ya_t, ga_t, gb_t):
        merged = _sigmoid(ga_t.astype(F32)) * ya_t.astype(F32) + _sigmoid(gb_t.astype(F32)) * acc
        return acc, merged

    yb, merged = _matmul(o_b, w_bf_t, mode="nt", name="mm_branch_fox", out_dtypes=[BF16, BF16],
                         tm=tm, tn=td, tk=FOX_W, extras=[ya, proj, proj], extra_maps=gate_maps,
                         epilogue=merge_epi)
    x_mid, = _matmul(merged, w_o, mode="nn", name="mm_out", out_dtypes=[F32], tm=tm, tn=td, tk=D,
                     extras=[x2d], epilogue=lambda acc, r: (acc + r,))
    h2 = _rms_fwd(x_mid, mlp_norm, name="rms2")
    u, = _matmul(h2, w_up_t, mode="nt", name="mm_up", out_dtypes=[BF16], tm=_pick(S, 2048), tn=tf, tk=D,
                 epilogue=lambda acc: (jnp.maximum(acc, 0.0),))
    x_fin, = _matmul(u, w_dn, mode="nn", name="mm_down", out_dtypes=[F32], tm=tm, tn=td, tk=_pick(DFF, 2048),
                     a_fn=_square_bf16, extras=[x_mid], epilogue=lambda acc, r: (acc + r,))

    dx3b, dg3, loss_part = _loss_head(x_fin, tgt, final_norm.reshape(1, D), name="loss_head")
    d_up, = _matmul(dx3b, w_dn, mode="nt", name="mm_d_act", out_dtypes=[BF16], tm=_pick(S, 2048), tn=tf, tk=D,
                    extras=[u], epilogue=lambda acc, ut: (acc * (2.0 * ut.astype(F32)),))
    tks = _pick(S, 2048)
    dw_dn, = _matmul(u, dx3b, mode="tn", name="mm_dw_down", out_dtypes=[BF16], tm=tf, tn=td, tk=tks,
                     a_fn=_square_bf16)
    dh2, = _matmul(d_up, w_up_t, mode="nn", name="mm_dh2", out_dtypes=[BF16], tm=tm, tn=td, tk=_pick(DFF, 2048))
    dw_up_t, = _matmul(d_up, h2, mode="tn", name="mm_dw_up", out_dtypes=[BF16], tm=tf, tn=td, tk=tks)
    h_s1, tok_s1 = _exchange_start([dw_up_t.reshape(N_DEV, DFF // N_DEV, D), dw_dn.reshape(N_DEV, DFF // N_DEV, D)],
                                   gather=False, name="scatter_mlp_start")
    dx2b, dg2 = _rms_bwd(dh2, x_mid, mlp_norm, dx3b, name="rms2_bwd", out_dtype=BF16, deps=[tok_s1])

    def gate_bwd_epi(dm, ya_t, yb_t, ga_t, gb_t):
        sa, sb = _sigmoid(ga_t.astype(F32)), _sigmoid(gb_t.astype(F32))
        return (dm * sa, dm * sb, dm * ya_t.astype(F32) * sa * (1.0 - sa), dm * yb_t.astype(F32) * sb * (1.0 - sb))

    gmaps = [lambda i, j, k: (i, j), lambda i, j, k: (i, j), lambda i, j, k: (i, j),
             lambda i, j, k: (i, j + D // td)]
    d_ya, d_yb, d_ga, d_gb = _matmul(dx2b, w_o, mode="nt", name="mm_d_merged", out_dtypes=[BF16] * 4,
                                     tm=tm, tn=td, tk=D, extras=[ya, yb, proj, proj], extra_maps=gmaps,
                                     epilogue=gate_bwd_epi)
    dw_o, = _matmul(merged, dx2b, mode="tn", name="mm_dw_out", out_dtypes=[BF16], tm=td, tn=td, tk=tks)
    d_oa, = _matmul(d_ya, w_bs_t, mode="nn", name="mm_d_oa", out_dtypes=[BF16], tm=tm, tn=SWA_Q_W, tk=D)
    d_ob, = _matmul(d_yb, w_bf_t, mode="nn", name="mm_d_ob", out_dtypes=[BF16], tm=tm, tn=FOX_W, tk=D)
    dw_bs_t, = _matmul(d_ya, o_a, mode="tn", name="mm_dw_bs", out_dtypes=[BF16], tm=td, tn=SWA_Q_W, tk=tks)
    dw_bf_t, = _matmul(d_yb, o_b, mode="tn", name="mm_dw_bf", out_dtypes=[BF16], tm=td, tn=FOX_W, tk=tks)
    h_s2, tok_s2 = _exchange_start([dw_bs_t.reshape(N_DEV, D // N_DEV, SWA_Q_W),
                                    dw_bf_t.reshape(N_DEV, D // N_DEV, FOX_W), dw_o.reshape(N_DEV, D // N_DEV, D)],
                                   gather=False, name="scatter_attn_start")
    def row_blocks_t(a):
        return a.reshape(S // bbq, bbq, FOX_W).transpose(0, 2, 1)

    d_fq, d_fk, d_fv, dcol4, drow4 = _fox_bwd(proj, _key_bias_blocks(negc, bbk), o_b, lse, d_ob,
                                              row_blocks_t(proj[:, fq_off:fq_off + FOX_W]), row_blocks_t(d_ob),
                                              q_off=fq_off, k_off=fk_off, v_off=fv_off, bq=bbq, bk=bbk,
                                              name="fox_bwd", deps=[tok_s2])
    dcol = dcol4.transpose(0, 2, 1, 3).reshape(FOX_HEADS, S)
    drow = drow4.transpose(0, 2, 1, 3).reshape(FOX_HEADS, S)
    dz_t, dbias_l = _fox_post(drow, dcol, z_t, bias_col, name="fox_post")
    d_aq, dk_c, dk_p, dv_c, dv_p, dsink_l = _swa_bwd(q_rope, k_rope, proj, sinks, d_oa, cos_t, sin_t, swa_mask,
                                                     v_off=v_off, name="swa_bwd")
    d_ak, d_av = _rope_bwd(dk_c, dk_p, dv_c, dv_p, cos_t, sin_t, name="rope_bwd")
    dz_pad = jnp.pad(dz_t.T.astype(BF16), ((0, 0), (0, FL_PAD - FOX_HEADS)))
    d_proj = jnp.concatenate([d_ga, d_gb, d_aq, d_ak, d_av, d_fq, d_fk, d_fv, dz_pad], axis=1)
    tkp = _pick(NP, 2304)
    dw_in_p, = _matmul(d_proj, h1, mode="tn", name="mm_dw_in", out_dtypes=[BF16], tm=_pick(NP, 512), tn=D, tk=tks)
    dw_in_t = jnp.concatenate([dw_in_p[q_off:q_off + QKV_W], dw_in_p[fl_off:fl_off + FOX_HEADS], dw_in_p[:q_off]],
                              axis=0)
    h_s3, tok_s3 = _exchange_start([dw_in_t.reshape(N_DEV, d_in // N_DEV, D)], gather=False,
                                   name="scatter_in_start")
    dh1, = _matmul(d_proj, w_in_p, mode="nn", name="mm_dh1", out_dtypes=[BF16], tm=tm, tn=td, tk=tkp, deps=[tok_s3])
    dx, dg1 = _rms_bwd(dh1, x2d, attn_norm, dx2b, name="rms1_bwd", out_dtype=F32)

    dbias = dbias_l[:, 0]
    dsinks = dsink_l[:, :, 0].reshape(-1)
    nsm = 3 * D + 2 * LANES
    tail = jnp.zeros((2 * LANES,), F32)
    small_g = jnp.concatenate([dg1[0], dg2[0], dg3[0],
                               tail.at[0:16].set(dbias).at[16:32].set(dsinks).at[32].set(loss_part[0, 0])])

    def pack(a_norm, b_norm, f_norm, bias, snk):
        return jnp.concatenate([a_norm[0], b_norm[0], f_norm,
                                tail.at[0:16].set(bias[0]).at[16:32].set(snk[0])]).reshape(1, nsm)

    small_stack, = _exchange([small_g.reshape(1, nsm)], gather=True, name="gather_small")
    u_sm = _adamw(pack(attn_norm, mlp_norm, final_norm, fox_f_bias, swa_sinks), small_stack,
                  pack(m_attn_norm, m_mlp_norm, m_final_norm, m_fox_f_bias, m_swa_sinks),
                  pack(v_attn_norm, v_mlp_norm, v_final_norm, v_fox_f_bias, v_swa_sinks),
                  name="adamw_small", stacked=True)
    loss = u_sm[0][0, 3 * D + 32]

    def own_of(src):
        return lax.dynamic_index_in_dim(src, me, 0, keepdims=False)

    def update_t(stack, src, w, m, v, nm):
        g = _sum8(stack, own_of(src), name="sum_" + nm).T
        return _adamw(w[0], g, m[0], v[0], name="adamw_" + nm, stacked=False)

    def update(stack, src, w, m, v, nm, transposed=False):
        return _adamw(w[0], stack, m[0], v[0], name="adamw_" + nm, stacked=True, own=own_of(src),
                      transposed=transposed)

    (s_up, s_dn), (r_up, r_dn) = _exchange_wait(h_s1, u_sm[1], name="scatter_mlp_wait")
    u_up = update(r_up, s_up, w_up, m_w_up, v_w_up, "w_up", transposed=True)
    u_dn = update(r_dn, s_dn, w_down, m_w_down, v_w_down, "w_down")
    (s_bs, s_bf, s_o), (r_bs, r_bf, r_o) = _exchange_wait(h_s2, u_dn[1], name="scatter_attn_wait")
    u_bs = update(r_bs, s_bs, w_branch_swa, m_w_branch_swa, v_w_branch_swa, "w_bs", transposed=True)
    u_bf = update(r_bf, s_bf, w_branch_fox, m_w_branch_fox, v_w_branch_fox, "w_bf", transposed=True)
    u_o = update(r_o, s_o, w_out, m_w_out, v_w_out, "w_out")
    (s_w_in,), (r_in,) = _exchange_wait(h_s3, u_o[1], name="scatter_in_wait")
    u_in = update_t(r_in, s_w_in, w_in, m_w_in, v_w_in, "w_in")

    def small(kind):
        a = u_sm[kind][0]
        return dict(attn_norm=a[0:D][None], mlp_norm=a[D:2 * D][None], final_norm=a[2 * D:3 * D],
                    fox_f_bias=a[3 * D:3 * D + 16][None], swa_sinks=a[3 * D + 16:3 * D + 32][None])

    big = dict(w_in=u_in, w_branch_swa=u_bs, w_branch_fox=u_bf, w_out=u_o, w_up=u_up, w_down=u_dn)
    order = ["attn_norm", "w_in", "fox_f_bias", "swa_sinks", "w_branch_swa", "w_branch_fox", "w_out", "mlp_norm",
             "w_up", "w_down", "final_norm"]
    outs = [loss, dx[None]]
    for kind in range(4):
        sm = small(kind)
        for nm in order:
            outs.append(big[nm][kind][None] if nm in big else sm[nm])
    return tuple(outs)
```

```python
import functools

import jax
import jax.numpy as jnp
from jax import lax
from jax.experimental import pallas as pl
from jax.experimental.pallas import tpu as pltpu

F32 = jnp.float32
BF16 = jnp.bfloat16

N_DEV = 8
HEAD_DIM = 64
SWA_Q_W = 1024
SWA_KV_W = 128
SWA_GROUP = 8
WINDOW = 128
FOX_W = 1024
FOX_HEADS = 16
QKV_W = SWA_Q_W + 2 * SWA_KV_W + 3 * FOX_W
FL_PAD = 256
ROPE_THETA = 10000.0
RMS_EPS = 1e-6
ATT_SCALE = 0.125
NEG = -1e30

ADAM_LR = 0.001
ADAM_B1 = 0.9
ADAM_B2 = 0.999
ADAM_EPS = 1e-08
ADAM_WD = 0.01
ADAM_STEP = 10

FOX_FWD_BLOCKS = (1024, 1024)
FOX_BWD_BLOCKS = (1024, 512)
FOX_FWD_PAIRS = 2

LANES = 128
VMEM_LIMIT = 56 * 1024 * 1024
STEP_BYTES = 12 * 1024 * 1024


def _cparams(*sem):
    return pltpu.CompilerParams(dimension_semantics=sem, vmem_limit_bytes=VMEM_LIMIT)


def _pick(dim, pref, align=LANES):
    best = None
    t = align
    while t <= min(dim, pref):
        if dim % t == 0:
            best = t
        t += align
    return best if best is not None else dim


_DIMS = {"nn": ((1,), (0,)), "nt": ((1,), (1,)), "tn": ((0,), (0,))}


_ANY = pl.BlockSpec(memory_space=pl.ANY)


def _matmul(a, b, *, mode, name, out_dtypes, tm, tn, tk, extras=(), extra_maps=None,
            a_fn=None, epilogue=None, deps=()):
    if mode == "nn":
        (M, K), (K2, N) = a.shape, b.shape
    elif mode == "nt":
        (M, K), (N, K2) = a.shape, b.shape
    else:
        (K, M), (K2, N) = a.shape, b.shape
    assert K == K2, (name, a.shape, b.shape)
    assert M % tm == 0 and N % tn == 0 and K % tk == 0, (name, M, N, K, tm, tn, tk)
    nk = K // tk
    ne, no = len(extras), len(out_dtypes)
    dims = (_DIMS[mode], ((), ()))

    def body(*refs):
        a_ref, b_ref = refs[0], refs[1]
        ex_refs = refs[2:2 + ne]
        out_refs = refs[2 + ne + len(deps):2 + ne + len(deps) + no]

        def finish(acc):
            res = (acc,) if epilogue is None else epilogue(acc, *[e[...] for e in ex_refs])
            for o_ref, r in zip(out_refs, res):
                o_ref[...] = r.astype(o_ref.dtype)

        def product():
            av = a_ref[...]
            if a_fn is not None:
                av = a_fn(av)
            return lax.dot_general(av, b_ref[...], dims, preferred_element_type=F32)

        if nk == 1:
            finish(product())
        else:
            acc_ref = refs[-1]
            k = pl.program_id(2)

            @pl.when(k == 0)
            def _():
                acc_ref[...] = jnp.zeros_like(acc_ref)

            acc_ref[...] += product()

            @pl.when(k == nk - 1)
            def _():
                finish(acc_ref[...])

    if mode == "tn":
        a_spec = pl.BlockSpec((tk, tm), lambda i, j, k: (k, i))
    else:
        a_spec = pl.BlockSpec((tm, tk), lambda i, j, k: (i, k))
    if mode == "nt":
        b_spec = pl.BlockSpec((tn, tk), lambda i, j, k: (j, k))
    else:
        b_spec = pl.BlockSpec((tk, tn), lambda i, j, k: (k, j))
    if extra_maps is None:
        extra_maps = [lambda i, j, k: (i, j)] * ne
    ex_specs = [pl.BlockSpec((tm, tn), m) for m in extra_maps]
    out_spec = [pl.BlockSpec((tm, tn), lambda i, j, k: (i, j)) for _ in range(no)]
    res = pl.pallas_call(
        body,
        name=name,
        grid=(M // tm, N // tn, nk),
        in_specs=[a_spec, b_spec] + ex_specs + [_ANY] * len(deps),
        out_specs=out_spec,
        out_shape=[jax.ShapeDtypeStruct((M, N), d) for d in out_dtypes],
        scratch_shapes=[pltpu.VMEM((tm, tn), F32)] if nk > 1 else [],
        compiler_params=_cparams("parallel", "parallel", "arbitrary"),
    )(a, b, *extras, *deps)
    return res


def _square_bf16(t):
    tf = t.astype(F32)
    return (tf * tf).astype(BF16)


def _sigmoid(g):
    return 1.0 / (1.0 + jnp.exp(-g))


def _rms_fwd(x, gain, *, name, deps=()):
    S, D = x.shape
    tr = _pick(S, 512, 8)

    def body(x_ref, g_ref, *rest):
        h_ref = rest[-1]
        xv = x_ref[...]
        r = lax.rsqrt(jnp.mean(xv * xv, axis=-1, keepdims=True) + RMS_EPS)
        h_ref[...] = (xv * r * g_ref[...]).astype(BF16)

    return pl.pallas_call(
        body, name=name, grid=(S // tr,),
        in_specs=[pl.BlockSpec((tr, D), lambda i: (i, 0)), pl.BlockSpec((1, D), lambda i: (0, 0))] + [_ANY] * len(deps),
        out_specs=pl.BlockSpec((tr, D), lambda i: (i, 0)),
        out_shape=jax.ShapeDtypeStruct((S, D), BF16),
        compiler_params=_cparams("parallel"),
    )(x, gain, *deps)


def _rms_bwd(dh, x, gain, dres, *, name, out_dtype, deps=()):
    S, D = x.shape
    tr = _pick(S, 512, 8)

    def body(dh_ref, x_ref, g_ref, dres_ref, *rest):
        outs = rest[len(deps):]
        dx_ref, dg_ref = outs[0], outs[-1]
        xv = x_ref[...]
        r = lax.rsqrt(jnp.mean(xv * xv, axis=-1, keepdims=True) + RMS_EPS)
        xh = xv * r
        dhv = dh_ref[...].astype(F32)
        t = dhv * g_ref[...]
        dx = r * (t - xh * jnp.mean(t * xh, axis=-1, keepdims=True)) + dres_ref[...].astype(F32)
        dx_ref[...] = dx.astype(out_dtype)
        part = jnp.sum(dhv * xh, axis=0, keepdims=True)

        @pl.when(pl.program_id(0) == 0)
        def _():
            dg_ref[...] = part

        @pl.when(pl.program_id(0) > 0)
        def _():
            dg_ref[...] += part

    row = pl.BlockSpec((tr, D), lambda i: (i, 0))
    vec = pl.BlockSpec((1, D), lambda i: (0, 0))
    return pl.pallas_call(
        body, name=name, grid=(S // tr,),
        in_specs=[row, row, vec, row] + [_ANY] * len(deps), out_specs=[row, vec],
        out_shape=[jax.ShapeDtypeStruct((S, D), out_dtype), jax.ShapeDtypeStruct((1, D), F32)],
        compiler_params=_cparams("arbitrary"),
    )(dh, x, gain, dres, *deps)


def _loss_head(x3, target, gain, *, name):
    S, D = x3.shape
    tr = _pick(S, 512, 8)

    def body(x_ref, t_ref, g_ref, dxb_ref, dg_ref, loss_ref):
        xv = x_ref[...]
        r = lax.rsqrt(jnp.mean(xv * xv, axis=-1, keepdims=True) + RMS_EPS)
        xh = xv * r
        gv = g_ref[...]
        err = xh * gv - t_ref[...]
        lpart = jnp.zeros((1, LANES), F32) + (0.5 / D) * jnp.sum(err * err)
        dy = err * (1.0 / D)
        t = dy * gv
        dx = r * (t - xh * jnp.mean(t * xh, axis=-1, keepdims=True))
        dxb_ref[...] = dx.astype(BF16)
        part = jnp.sum(dy * xh, axis=0, keepdims=True)

        @pl.when(pl.program_id(0) == 0)
        def _():
            dg_ref[...] = part
            loss_ref[...] = lpart

        @pl.when(pl.program_id(0) > 0)
        def _():
            dg_ref[...] += part
            loss_ref[...] += lpart

    row = pl.BlockSpec((tr, D), lambda i: (i, 0))
    vec = pl.BlockSpec((1, D), lambda i: (0, 0))
    return pl.pallas_call(
        body, name=name, grid=(S // tr,),
        in_specs=[row, row, vec],
        out_specs=[row, vec, pl.BlockSpec((1, LANES), lambda i: (0, 0))],
        out_shape=[jax.ShapeDtypeStruct((S, D), BF16),
                   jax.ShapeDtypeStruct((1, D), F32), jax.ShapeDtypeStruct((1, LANES), F32)],
        compiler_params=_cparams("arbitrary"),
    )(x3, target, gain)


def _rope_tables(pos_col, invf, *, name):
    S = pos_col.shape[0]
    tr = _pick(S, 512, 8)

    def body(p_ref, f_ref, cos_ref, sin_ref):
        ang = p_ref[...].astype(F32) * f_ref[...]
        lane = lax.broadcasted_iota(jnp.int32, (1, LANES), 1)
        first = (lane % HEAD_DIM) < HEAD_DIM // 2
        sn = jnp.sin(ang)
        cos_ref[...] = jnp.cos(ang)
        sin_ref[...] = jnp.where(first, -sn, sn)

    return pl.pallas_call(
        body, name=name, grid=(S // tr,),
        in_specs=[pl.BlockSpec((tr, 1), lambda i: (i, 0)), pl.BlockSpec((1, LANES), lambda i: (0, 0))],
        out_specs=[pl.BlockSpec((tr, LANES), lambda i: (i, 0))] * 2,
        out_shape=[jax.ShapeDtypeStruct((S, LANES), F32)] * 2,
        compiler_params=_cparams("parallel"),
    )(pos_col, invf)


def _swap_halves(t):
    lane = lax.broadcasted_iota(jnp.int32, (1, LANES), 1)
    first = (lane % HEAD_DIM) < HEAD_DIM // 2
    return jnp.where(first, pltpu.roll(t, LANES - HEAD_DIM // 2, 1), pltpu.roll(t, HEAD_DIM // 2, 1))


def _rope_fwd(proj, cos_t, sin_t, *, q_off, k_off, name):
    S = proj.shape[0]
    tr = _pick(S, 512, 8)
    nqb = SWA_Q_W // LANES

    def body(q_ref, k_ref, c_ref, s_ref, qo_ref, ko_ref):
        cv, sv = c_ref[...], s_ref[...]
        for b in range(nqb):
            t = q_ref[:, b * LANES:(b + 1) * LANES].astype(F32)
            qo_ref[:, b * LANES:(b + 1) * LANES] = (t * cv + _swap_halves(t) * sv).astype(BF16)
        t = k_ref[...].astype(F32)
        ko_ref[...] = (t * cv + _swap_halves(t) * sv).astype(BF16)

    tab = pl.BlockSpec((tr, LANES), lambda i: (i, 0))
    return pl.pallas_call(
        body, name=name, grid=(S // tr,),
        in_specs=[pl.BlockSpec((tr, SWA_Q_W), lambda i: (i, q_off // SWA_Q_W)),
                  pl.BlockSpec((tr, LANES), lambda i: (i, k_off // LANES)), tab, tab],
        out_specs=[pl.BlockSpec((tr, SWA_Q_W), lambda i: (i, 0)), tab],
        out_shape=[jax.ShapeDtypeStruct((S, SWA_Q_W), BF16), jax.ShapeDtypeStruct((S, LANES), BF16)],
        compiler_params=_cparams("parallel"),
    )(proj, proj, cos_t, sin_t)


def _rope_bwd(dk_cur, dk_prev, dv_cur, dv_prev, cos_t, sin_t, *, name):
    S = dk_cur.shape[1]
    tr = _pick(S, 512)
    nb = S // tr

    def body(kc_ref, kp_ref, vc_ref, vp_ref, c_ref, s_ref, dko_ref, dvo_ref):
        cv, sv = c_ref[...], s_ref[...]
        row = pl.program_id(0) * tr + lax.broadcasted_iota(jnp.int32, (tr, 1), 0)
        has_next = row < S - WINDOW
        d = kc_ref[0] + kc_ref[1] + jnp.where(has_next, kp_ref[0] + kp_ref[1], 0.0)
        dko_ref[...] = (d * cv + _swap_halves(d * sv)).astype(BF16)
        dvo_ref[...] = (vc_ref[0] + vc_ref[1] + jnp.where(has_next, vp_ref[0] + vp_ref[1], 0.0)).astype(BF16)

    tab = pl.BlockSpec((tr, LANES), lambda i: (i, 0))
    cur = pl.BlockSpec((2, tr, LANES), lambda i: (0, i, 0))
    return pl.pallas_call(
        body, name=name, grid=(nb,),
        in_specs=[cur, cur, cur, cur, tab, tab],
        out_specs=[tab, tab],
        out_shape=[jax.ShapeDtypeStruct((S, LANES), BF16), jax.ShapeDtypeStruct((S, LANES), BF16)],
        compiler_params=_cparams("parallel"),
    )(dk_cur, dk_prev, dv_cur, dv_prev, cos_t, sin_t)


def _dot_nt(a, b):
    return lax.dot_general(a, b, (((1,), (1,)), ((), ())), preferred_element_type=F32)


def _dot_tn(a, b):
    return lax.dot_general(a, b, (((0,), (0,)), ((), ())), preferred_element_type=F32)


def _dot_nn(a, b):
    return lax.dot_general(a, b, (((1,), (0,)), ((), ())), preferred_element_type=F32)


def _roll_half(t):
    return pltpu.roll(t.astype(F32), HEAD_DIM, 1).astype(t.dtype)


SWA_STACK = SWA_GROUP // 2


def _swa_mask_bias():
    rows = SWA_STACK * WINDOW
    row = lax.broadcasted_iota(jnp.int32, (rows, 2 * WINDOW), 0) % WINDOW
    col = lax.broadcasted_iota(jnp.int32, (rows, 2 * WINDOW), 1)
    diff = row + WINDOW - col
    window = (diff >= 0) & (diff < WINDOW)
    return jnp.stack([jnp.where(window & (col >= WINDOW), 0.0, NEG), jnp.where(window, 0.0, NEG)]).astype(F32)


def _swa_common(kp_ref, kc_ref, vp_ref, vc_ref):
    k2 = jnp.concatenate([kp_ref[...], kc_ref[...]], axis=0)
    v2 = jnp.concatenate([vp_ref[...], vc_ref[...]], axis=0)
    k_sw, v_sw = _roll_half(k2), _roll_half(v2)
    lane = lax.broadcasted_iota(jnp.int32, (1, LANES), 1)
    half = [lane < HEAD_DIM, lane >= HEAD_DIM]
    kk = [[k2 if hk == a else k_sw for a in range(2)] for hk in range(2)]
    vv = [[v2 if hk == a else v_sw for a in range(2)] for hk in range(2)]
    return half, kk, vv


def _swa_stack(ref, hk, mask, scale=None):
    parts = []
    for t in range(SWA_STACK):
        blk = ref[:, (hk * SWA_STACK + t) * LANES:(hk * SWA_STACK + t + 1) * LANES]
        if scale is not None:
            blk = blk * jnp.asarray(scale, blk.dtype)
        parts.append(jnp.where(mask, blk, jnp.zeros_like(blk)))
    return jnp.concatenate(parts, axis=0)


def _swa_sink_column(sink_ref, hk, a):
    blk = lax.broadcasted_iota(jnp.int32, (SWA_STACK * WINDOW, 1), 0) // WINDOW
    col = jnp.zeros((SWA_STACK * WINDOW, 1), F32)
    for t in range(SWA_STACK):
        col = jnp.where(blk == t, sink_ref[hk * SWA_GROUP + 2 * t + a], col)
    return col


def _swa_probs(qm, kk, mask_bias, sink):
    s = _dot_nt(qm, kk) + mask_bias
    m = jnp.maximum(jnp.max(s, axis=1, keepdims=True), sink)
    e = jnp.exp(s - m)
    es = jnp.exp(sink - m)
    inv = 1.0 / (jnp.sum(e, axis=1, keepdims=True) + es)
    return e * inv, es * inv


def _swa_mask_spec():
    return pl.BlockSpec((1, SWA_STACK * WINDOW, 2 * WINDOW), lambda n: (jnp.minimum(n, 1), 0, 0))


def _swa_fwd(q_rope, k_rope, proj, sinks, mask_bias, *, v_off, name):
    S = q_rope.shape[0]
    nb = S // WINDOW

    def body(sink_ref, q_ref, kp_ref, kc_ref, vp_ref, vc_ref, mask_ref, o_ref):
        half, kk, vv = _swa_common(kp_ref, kc_ref, vp_ref, vc_ref)
        for hk in range(2):
            outs = []
            for a in range(2):
                qm = _swa_stack(q_ref, hk, half[a], ATT_SCALE)
                p, _ = _swa_probs(qm, kk[hk][a], mask_ref[0], _swa_sink_column(sink_ref, hk, a))
                outs.append(_dot_nn(p.astype(BF16), vv[hk][a]))
            for t in range(SWA_STACK):
                rows = slice(t * WINDOW, (t + 1) * WINDOW)
                c0 = (hk * SWA_STACK + t) * LANES
                o_ref[:, c0:c0 + LANES] = jnp.where(half[0], outs[0][rows], outs[1][rows]).astype(BF16)

    prev = lambda n: (jnp.maximum(n - 1, 0), 0)
    cur = lambda n: (n, 0)
    vprev = lambda n: (jnp.maximum(n - 1, 0), v_off // LANES)
    vcur = lambda n: (n, v_off // LANES)
    blk = lambda m: pl.BlockSpec((WINDOW, LANES), m)
    return pl.pallas_call(
        body, name=name, grid=(nb,),
        in_specs=[pl.BlockSpec(memory_space=pltpu.SMEM),
                  pl.BlockSpec((WINDOW, SWA_Q_W), lambda n: (n, 0)),
                  blk(prev), blk(cur), blk(vprev), blk(vcur), _swa_mask_spec()],
        out_specs=pl.BlockSpec((WINDOW, SWA_Q_W), lambda n: (n, 0)),
        out_shape=jax.ShapeDtypeStruct((S, SWA_Q_W), BF16),
        compiler_params=_cparams("parallel"),
    )(sinks, q_rope, k_rope, k_rope, proj, proj, mask_bias)


def _swa_bwd(q_rope, k_rope, proj, sinks, d_o, cos_t, sin_t, mask_bias, *, v_off, name):
    S = q_rope.shape[0]
    nb = S // WINDOW

    def body(sink_ref, q_ref, kp_ref, kc_ref, vp_ref, vc_ref, do_ref, c_ref, s_ref, mask_ref,
             dq_ref, dkc_ref, dkp_ref, dvc_ref, dvp_ref, dsink_ref):
        n = pl.program_id(0)
        half, kk, vv = _swa_common(kp_ref, kc_ref, vp_ref, vc_ref)
        allowed = mask_ref[0]
        cv, sv = c_ref[...], s_ref[...]
        srow = lax.broadcasted_iota(jnp.int32, (SWA_GROUP, LANES), 0)
        for hk in range(2):
            dk_acc = jnp.zeros((2 * WINDOW, LANES), F32)
            dv_acc = jnp.zeros((2 * WINDOW, LANES), F32)
            dsink = jnp.zeros((SWA_GROUP, LANES), F32)
            dqs = []
            for a in range(2):
                qm = _swa_stack(q_ref, hk, half[a], ATT_SCALE)
                dom = _swa_stack(do_ref, hk, half[a])
                p, psink = _swa_probs(qm, kk[hk][a], allowed, _swa_sink_column(sink_ref, hk, a))
                dp = _dot_nt(dom, vv[hk][a])
                delta = jnp.sum(p * dp, axis=1, keepdims=True)
                ds = (p * (dp - delta)).astype(BF16)
                dsk = psink * delta
                for t in range(SWA_STACK):
                    dsink = dsink + jnp.where(srow == 2 * t + a, -jnp.sum(dsk[t * WINDOW:(t + 1) * WINDOW]), 0.0)
                dqs.append(_dot_nn(ds, kk[hk][a]) * ATT_SCALE)
                dk_acc = dk_acc + _dot_tn(ds, qm)
                dv_acc = dv_acc + _dot_tn(p.astype(BF16), dom)
            for t in range(SWA_STACK):
                rows = slice(t * WINDOW, (t + 1) * WINDOW)
                d = jnp.where(half[0], dqs[0][rows], dqs[1][rows])
                c0 = (hk * SWA_STACK + t) * LANES
                dq_ref[:, c0:c0 + LANES] = (d * cv + _swap_halves(d * sv)).astype(BF16)
            dk_t = jnp.where(half[hk], dk_acc + pltpu.roll(dk_acc, HEAD_DIM, 1), 0.0)
            dv_t = jnp.where(half[hk], dv_acc + pltpu.roll(dv_acc, HEAD_DIM, 1), 0.0)
            dkp_ref[hk] = dk_t[:WINDOW]
            dkc_ref[hk] = dk_t[WINDOW:]
            dvp_ref[hk] = dv_t[:WINDOW]
            dvc_ref[hk] = dv_t[WINDOW:]

            @pl.when(n == 0)
            def _():
                dsink_ref[hk] = dsink

            @pl.when(n > 0)
            def _():
                dsink_ref[hk] += dsink

    prev = lambda n: (jnp.maximum(n - 1, 0), 0)
    cur = lambda n: (n, 0)
    vprev = lambda n: (jnp.maximum(n - 1, 0), v_off // LANES)
    vcur = lambda n: (n, v_off // LANES)
    blk = lambda m: pl.BlockSpec((WINDOW, LANES), m)
    qblk = pl.BlockSpec((WINDOW, SWA_Q_W), lambda n: (n, 0))
    part = pl.BlockSpec((2, WINDOW, LANES), lambda n: (0, n, 0))
    part_prev = pl.BlockSpec((2, WINDOW, LANES), lambda n: (0, jnp.maximum(n - 1, 0), 0))
    part_shape = jax.ShapeDtypeStruct((2, S, LANES), F32)
    return pl.pallas_call(
        body, name=name, grid=(nb,),
        in_specs=[pl.BlockSpec(memory_space=pltpu.SMEM), qblk, blk(prev), blk(cur), blk(vprev), blk(vcur), qblk,
                  blk(cur), blk(cur), _swa_mask_spec()],
        out_specs=[qblk, part, part_prev, part, part_prev,
                   pl.BlockSpec((2, SWA_GROUP, LANES), lambda n: (0, 0, 0))],
        out_shape=[jax.ShapeDtypeStruct((S, SWA_Q_W), BF16), part_shape, part_shape, part_shape, part_shape,
                   jax.ShapeDtypeStruct((2, SWA_GROUP, LANES), F32)],
        compiler_params=_cparams("arbitrary"),
    )(sinks, q_rope, k_rope, k_rope, proj, proj, d_o, cos_t, sin_t, mask_bias)


def _fox_prep(z_t, bias_col, *, name):
    H, S = z_t.shape
    tb = _pick(S, 512)

    def body(z_ref, b_ref, o_ref, carry_ref):
        @pl.when(pl.program_id(0) == 0)
        def _():
            carry_ref[...] = jnp.zeros_like(carry_ref)

        zz = z_ref[...] + b_ref[...]
        t = jnp.exp(-jnp.abs(zz))
        log1p = jnp.where(t < 1e-2, t * (1.0 - t * (0.5 - t * (1.0 / 3.0))), jnp.log(1.0 + t))
        logf = jnp.minimum(zz, 0.0) - log1p
        r = lax.broadcasted_iota(jnp.int32, (tb, tb), 0)
        c = lax.broadcasted_iota(jnp.int32, (tb, tb), 1)
        tri = (r <= c).astype(BF16)
        hi = logf.astype(BF16)
        r1 = logf - hi.astype(F32)
        mid = r1.astype(BF16)
        lo = (r1 - mid.astype(F32)).astype(BF16)
        cs = _dot_nn(hi, tri) + _dot_nn(mid, tri) + _dot_nn(lo, tri) + carry_ref[:, 0:1]
        o_ref[...] = -cs
        carry_ref[...] = jnp.zeros_like(carry_ref) + cs[:, tb - 1:tb]

    return pl.pallas_call(
        body, name=name, grid=(S // tb,),
        in_specs=[pl.BlockSpec((H, tb), lambda i: (0, i)), pl.BlockSpec((H, 1), lambda i: (0, 0))],
        out_specs=pl.BlockSpec((H, tb), lambda i: (0, i)),
        out_shape=jax.ShapeDtypeStruct((H, S), F32),
        scratch_shapes=[pltpu.VMEM((H, LANES), F32)],
        compiler_params=_cparams("arbitrary"),
    )(z_t, bias_col)


def _fox_post(drow, dcol, z_t, bias_col, *, name):
    H, S = z_t.shape
    tb = _pick(S, 512)
    nb = S // tb

    def body(dr_ref, d_ref, z_ref, b_ref, dz_ref, db_ref, carry_ref):
        @pl.when(pl.program_id(0) == 0)
        def _():
            carry_ref[...] = jnp.zeros_like(carry_ref)
            db_ref[...] = jnp.zeros_like(db_ref)

        dc = dr_ref[...] - d_ref[...]
        r = lax.broadcasted_iota(jnp.int32, (tb, tb), 0)
        c = lax.broadcasted_iota(jnp.int32, (tb, tb), 1)
        tri = (r >= c).astype(BF16)
        hi = dc.astype(BF16)
        r1 = dc - hi.astype(F32)
        mid = r1.astype(BF16)
        lo = (r1 - mid.astype(F32)).astype(BF16)
        dlogf = _dot_nn(hi, tri) + _dot_nn(mid, tri) + _dot_nn(lo, tri) + carry_ref[:, 0:1]
        carry_ref[...] = jnp.zeros_like(carry_ref) + dlogf[:, 0:1]
        dz = dlogf * _sigmoid(-(z_ref[...] + b_ref[...]))
        dz_ref[...] = dz
        db_ref[...] += jnp.sum(dz, axis=1, keepdims=True)

    rev = lambda i: (0, nb - 1 - i)
    return pl.pallas_call(
        body, name=name, grid=(nb,),
        in_specs=[pl.BlockSpec((H, tb), rev), pl.BlockSpec((H, tb), rev), pl.BlockSpec((H, tb), rev),
                  pl.BlockSpec((H, 1), lambda i: (0, 0))],
        out_specs=[pl.BlockSpec((H, tb), rev), pl.BlockSpec((H, LANES), lambda i: (0, 0))],
        out_shape=[jax.ShapeDtypeStruct((H, S), F32), jax.ShapeDtypeStruct((H, LANES), F32)],
        scratch_shapes=[pltpu.VMEM((H, LANES), F32)],
        compiler_params=_cparams("arbitrary"),
    )(drow, dcol, z_t, bias_col)


def _fox_blocks(S):
    cap = max(LANES, S // 4)
    return (min(FOX_FWD_BLOCKS[0], cap), min(FOX_FWD_BLOCKS[1], cap)), \
           (min(FOX_BWD_BLOCKS[0], cap), min(FOX_BWD_BLOCKS[1], cap))


def _key_bias_blocks(negc, bk):
    H, S = negc.shape
    return negc.reshape(H // 2, 2, S // bk, bk).transpose(0, 2, 1, 3)


def _fox_fwd(proj, negc4, *, q_off, k_off, v_off, bq, bk, name):
    S = proj.shape[0]
    nq, nk = S // bq, S // bk
    npair = FOX_HEADS // 2
    assert bq % bk == 0 or bk % bq == 0
    nmask = max(1, bq // bk)

    gp = FOX_FWD_PAIRS
    gw = gp * LANES
    assert q_off % gw == 0 and k_off % gw == 0 and v_off % gw == 0 and npair % gp == 0

    def body(q_ref, k_ref, v_ref, nc_ref, o_ref, lse_ref):
        i = pl.program_id(1)
        lane = lax.broadcasted_iota(jnp.int32, (1, LANES), 1)
        half = [lane < HEAD_DIM, lane >= HEAD_DIM]
        qh = []
        for g in range(gp):
            q2 = q_ref[:, g * LANES:(g + 1) * LANES] * jnp.asarray(ATT_SCALE, BF16)
            qh += [jnp.where(half[h], q2, jnp.zeros_like(q2)) for h in range(2)]
        row = lax.broadcasted_iota(jnp.int32, (bq, bk), 0)
        col = lax.broadcasted_iota(jnp.int32, (bq, bk), 1)
        rel = row - col
        nfull = (i * bq) // bk

        spare = [HEAD_DIM, 0]
        ones_lane = [lane == spare[h] for h in range(2)]

        def step(j, carry, masked):
            start = pl.multiple_of(j * bk, bk)
            new = []
            for g in range(gp):
                ks = k_ref[pl.ds(start, bk), g * LANES:(g + 1) * LANES]
                vs = v_ref[pl.ds(start, bk), g * LANES:(g + 1) * LANES]
                nb = nc_ref[g, j]
                for h in range(2):
                    m, acc = carry[4 * g + 2 * h:4 * g + 2 * h + 2]
                    vh = jnp.where(half[h], vs, jnp.where(ones_lane[h], jnp.ones_like(vs), jnp.zeros_like(vs)))
                    qs, bias = qh[2 * g + h], nb[h:h + 1, :]

                    def update(m, acc, rows, keys):
                        s = _dot_nt(qs[rows], ks[keys]) + bias[:, keys]
                        if masked:
                            s = jnp.where(rel[rows, keys] >= j * bk - i * bq, s, NEG)
                        m_new = jnp.maximum(m[rows], jnp.max(s, axis=1, keepdims=True))
                        p = jnp.exp(s - m_new).astype(BF16)
                        return m_new, jnp.exp(m[rows] - m_new) * acc[rows] + _dot_nn(p, vh[keys])

                    if masked and bq == bk:
                        top, bot, everything = slice(0, bq // 2), slice(bq // 2, bq), slice(0, bk)
                        m_t, acc_t = update(m, acc, top, top)
                        m_b, acc_b = update(m, acc, bot, everything)
                        new += [jnp.concatenate([m_t, m_b], axis=0), jnp.concatenate([acc_t, acc_b], axis=0)]
                    else:
                        new += list(update(m, acc, slice(0, bq), slice(0, bk)))
            return tuple(new)

        init = (jnp.full((bq, 1), NEG, F32), jnp.zeros((bq, LANES), F32)) * (2 * gp)
        carry = lax.fori_loop(0, nfull, lambda j, c: step(j, c, False), init)
        for t in range(nmask):
            carry = step(nfull + t, carry, True)
        for g in range(gp):
            outs, lses = [], []
            for h in range(2):
                m, acc = carry[4 * g + 2 * h:4 * g + 2 * h + 2]
                l = acc[:, spare[h]:spare[h] + 1]
                outs.append(acc * (1.0 / l))
                lses.append(m + jnp.log(l))
            o_ref[:, g * LANES:(g + 1) * LANES] = jnp.where(half[0], outs[0], outs[1]).astype(BF16)
            lse_ref[g] = jnp.where(half[0], lses[0], lses[1])

    seq = lambda off: pl.BlockSpec((S, gw), lambda hp, i: (0, off // gw + hp))
    return pl.pallas_call(
        body, name=name, grid=(npair // gp, nq),
        in_specs=[pl.BlockSpec((bq, gw), lambda hp, i: (i, q_off // gw + hp)), seq(k_off), seq(v_off),
                  pl.BlockSpec((gp, nk, 2, bk), lambda hp, i: (hp, 0, 0, 0))],
        out_specs=[pl.BlockSpec((bq, gw), lambda hp, i: (i, hp)),
                   pl.BlockSpec((gp, bq, LANES), lambda hp, i: (hp, i, 0))],
        out_shape=[jax.ShapeDtypeStruct((S, FOX_W), BF16), jax.ShapeDtypeStruct((npair, S, LANES), F32)],
        compiler_params=_cparams("parallel", "parallel"),
    )(proj, proj, proj, negc4)


def _fox_bwd(proj, negc4, o, lse, d_o, q_t, do_t, *, q_off, k_off, v_off, bq, bk, name, deps=()):
    S = proj.shape[0]
    nq, nk = S // bq, S // bk
    npair = FOX_HEADS // 2
    assert bq % bk == 0 or bk % bq == 0
    nmask = max(1, bk // bq)

    def body(q_ref, k_ref, v_ref, nc_ref, o_ref, lse_ref, do_ref, qt_ref, dot_ref, *rest):
        dqo_ref, dk_ref, dv_ref, dn_ref, dr_ref, delta_ref, rs_ref, dq_ref = rest[len(deps):]
        j = pl.program_id(1)
        lane = lax.broadcasted_iota(jnp.int32, (1, LANES), 1)
        half = [lane < HEAD_DIM, lane >= HEAD_DIM]
        spare = [HEAD_DIM, 0]
        ones_lane = [lane == spare[h] for h in range(2)]
        srow = lax.broadcasted_iota(jnp.int32, (LANES, 1), 0)
        rhalf = [srow < HEAD_DIM, srow >= HEAD_DIM]
        ones_row = [srow == spare[h] for h in range(2)]
        k2, v2 = k_ref[...], v_ref[...]
        one_k = jnp.ones_like(k2)
        kh = [jnp.where(half[h], k2, jnp.where(ones_lane[h], one_k, jnp.zeros_like(k2))) for h in range(2)]
        nb = nc_ref[0, 0]
        row = lax.broadcasted_iota(jnp.int32, (bq, bk), 0)
        col = lax.broadcasted_iota(jnp.int32, (bq, bk), 1)
        rel = row - col
        i_first = (j * bk) // bq

        @pl.when(j == 0)
        def _():
            dq_ref[...] = jnp.zeros_like(dq_ref)
            rs_ref[...] = jnp.zeros_like(rs_ref)
            for b in range(nq):
                prod = do_ref[b * bq:(b + 1) * bq, :].astype(F32) * o_ref[b * bq:(b + 1) * bq, :].astype(F32)
                d0 = jnp.sum(jnp.where(half[0], prod, 0.0), axis=1, keepdims=True)
                d1 = jnp.sum(jnp.where(half[1], prod, 0.0), axis=1, keepdims=True)
                delta_ref[b * bq:(b + 1) * bq, :] = jnp.where(half[0], d0, d1)

        def step(i, carry, masked, r0=0):
            dkt_a, dkt_b, dvt = carry
            dkts = [dkt_a, dkt_b]
            nr = bq - r0
            start = pl.multiple_of(i * bq + r0, LANES)
            q2 = q_ref[pl.ds(start, nr), :] * jnp.asarray(ATT_SCALE, BF16)
            do2 = do_ref[pl.ds(start, nr), :]
            qt = qt_ref[i][:, r0:] * jnp.asarray(ATT_SCALE, BF16)
            dot = dot_ref[i][:, r0:]
            lse2 = lse_ref[0, pl.ds(start, nr), :]
            del2 = delta_ref[pl.ds(start, nr), :]
            dqf = []
            for h in range(2):
                qm = jnp.where(half[h], q2, jnp.zeros_like(q2))
                dom = jnp.where(half[h], do2, jnp.zeros_like(do2))
                qtm = jnp.where(rhalf[h], qt, jnp.where(ones_row[h], jnp.ones_like(qt), jnp.zeros_like(qt)))
                dotm = jnp.where(rhalf[h], dot, jnp.zeros_like(dot))
                c0 = h * HEAD_DIM
                p = jnp.exp(_dot_nt(qm, k2) + nb[h:h + 1, :] - lse2[:, c0:c0 + 1])
                if masked:
                    p = jnp.where(rel[r0:] >= j * bk - i * bq, p, 0.0)
                dp = _dot_nt(dom, v2)
                dsb = (p * (dp - del2[:, c0:c0 + 1])).astype(BF16)
                dvt = dvt + _dot_nn(dotm, p.astype(BF16))
                dkts[h] = dkts[h] + _dot_nn(qtm, dsb)
                dqf.append(_dot_nn(dsb, kh[h]))
            dq_ref[pl.ds(start, nr), :] += jnp.where(half[0], dqf[0], dqf[1]) * ATT_SCALE
            rs_ref[pl.ds(start, nr), :] += jnp.where(ones_lane[0], dqf[0], jnp.where(ones_lane[1], dqf[1], 0.0))
            return dkts[0], dkts[1], dvt

        zero = jnp.zeros((LANES, bk), F32)
        carry = (zero, zero, zero)
        if bq > bk:
            sp = j % (bq // bk)
            carry = lax.switch(sp, [functools.partial(step, i_first, masked=True, r0=s * bk)
                                    for s in range(bq // bk)], carry)
        else:
            for t in range(nmask):
                carry = step(i_first + t, carry, True)
        dkt_a, dkt_b, dvt = lax.fori_loop(i_first + nmask, nq, lambda i, c: step(i, c, False), carry)
        dk_ref[...] = jnp.where(rhalf[0], dkt_a, dkt_b).T.astype(BF16)
        dv_ref[...] = dvt.T.astype(BF16)
        dn_ref[0, 0] = jnp.concatenate([dkt_a[spare[0]:spare[0] + 1], dkt_b[spare[1]:spare[1] + 1]], axis=0)

        @pl.when(j == nk - 1)
        def _():
            dqo_ref[...] = dq_ref[...].astype(BF16)
            for b in range(nq):
                t = rs_ref[b * bq:(b + 1) * bq, :].T
                dr_ref[0, b] = jnp.concatenate([t[spare[0]:spare[0] + 1], t[spare[1]:spare[1] + 1]], axis=0)

    once = pl.Buffered(1)
    seq = lambda off: pl.BlockSpec((S, LANES), lambda hp, j: (0, off // LANES + hp), pipeline_mode=once)
    blk = lambda off: pl.BlockSpec((bk, LANES), lambda hp, j: (j, off // LANES + hp))
    nc = pl.BlockSpec((1, 1, 2, bk), lambda hp, j: (hp, j, 0, 0))
    tsp = pl.BlockSpec((nq, LANES, bq), lambda hp, j: (0, hp, 0), pipeline_mode=once)
    return pl.pallas_call(
        body, name=name, grid=(npair, nk),
        in_specs=[seq(q_off), blk(k_off), blk(v_off), nc, seq(0),
                  pl.BlockSpec((1, S, LANES), lambda hp, j: (hp, 0, 0), pipeline_mode=once), seq(0),
                  tsp, tsp] + [_ANY] * len(deps),
        out_specs=[pl.BlockSpec((S, LANES), lambda hp, j: (0, hp)), blk(0), blk(0), nc,
                   pl.BlockSpec((1, nq, 2, bq), lambda hp, j: (hp, 0, 0, 0))],
        out_shape=[jax.ShapeDtypeStruct((S, FOX_W), BF16), jax.ShapeDtypeStruct((S, FOX_W), BF16),
                   jax.ShapeDtypeStruct((S, FOX_W), BF16), jax.ShapeDtypeStruct((npair, nk, 2, bk), F32),
                   jax.ShapeDtypeStruct((npair, nq, 2, bq), F32)],
        scratch_shapes=[pltpu.VMEM((S, LANES), F32), pltpu.VMEM((S, LANES), F32), pltpu.VMEM((S, LANES), F32)],
        compiler_params=_cparams("parallel", "arbitrary"),
    )(proj, proj, proj, negc4, o, lse, d_o, q_t, do_t, *deps)


def _exchange(arrs, *, gather, name):
    n = len(arrs)
    npeer = N_DEV - 1

    def body(*refs):
        ins, outs = refs[:n], refs[n:2 * n]
        send_sems, recv_sems, loc_sems = refs[2 * n:]
        x, y, c = lax.axis_index("x"), lax.axis_index("y"), lax.axis_index("c")
        me = 4 * x + 2 * y + c
        peers = []
        for k in range(1, N_DEV):
            px = 1 - x if k & 4 else x
            py = 1 - y if k & 2 else y
            pc = 1 - c if k & 1 else c
            peers.append(((px, py, pc), 4 * px + 2 * py + pc))

        def remote(w, k):
            dev, idx = peers[k]
            src = ins[w] if gather else ins[w].at[idx]
            return pltpu.make_async_remote_copy(
                src_ref=src, dst_ref=outs[w].at[me],
                send_sem=send_sems.at[w * npeer + k], recv_sem=recv_sems.at[w * npeer + k],
                device_id=dev, device_id_type=pl.DeviceIdType.MESH)

        def arrival(w, k):
            dev, idx = peers[k]
            src = ins[w] if gather else ins[w].at[idx]
            return pltpu.make_async_remote_copy(
                src_ref=src, dst_ref=outs[w].at[idx],
                send_sem=send_sems.at[w * npeer + k], recv_sem=recv_sems.at[w * npeer + k],
                device_id=dev, device_id_type=pl.DeviceIdType.MESH)

        local = []
        for w in range(n):
            for k in range(npeer):
                remote(w, k).start()
            cp = pltpu.make_async_copy(ins[w] if gather else ins[w].at[me], outs[w].at[me], loc_sems.at[w])
            cp.start()
            local.append(cp)
        for w in range(n):
            for k in range(npeer):
                arrival(w, k).wait_recv()
        for w in range(n):
            for k in range(npeer):
                remote(w, k).wait_send()
            local[w].wait()

    hbm = pl.BlockSpec(memory_space=pl.ANY)
    out_shape = [jax.ShapeDtypeStruct((N_DEV,) + (a.shape if gather else a.shape[1:]), a.dtype) for a in arrs]
    return pl.pallas_call(
        body, name=name,
        in_specs=[hbm] * n, out_specs=[hbm] * n, out_shape=out_shape,
        scratch_shapes=[pltpu.SemaphoreType.DMA((n * npeer,)), pltpu.SemaphoreType.DMA((n * npeer,)),
                        pltpu.SemaphoreType.DMA((n,))],
        compiler_params=pltpu.CompilerParams(has_side_effects=True),
    )(*arrs)


def _gather_two_level(shard, *, name):
    def body(x_ref, out_ref, send_sems, recv_sems, local_sem):
        x, y, c = lax.axis_index("x"), lax.axis_index("y"), lax.axis_index("c")
        me, sibling = (x, y, c), (x, y, 1 - c)
        chips = [(1 - x, y), (x, 1 - y), (1 - x, 1 - y)]

        def slot(px, py, pc):
            return out_ref.at[4 * px + 2 * py + pc]

        def copy(k, block, to, src=None):
            return pltpu.make_async_remote_copy(
                src_ref=slot(*block) if src is None else src, dst_ref=slot(*block),
                send_sem=send_sems.at[k], recv_sem=recv_sems.at[k],
                device_id=to, device_id_type=pl.DeviceIdType.MESH)

        mine = pltpu.make_async_copy(x_ref, slot(*me), local_sem)
        mine.start()
        first = [copy(0, me, sibling, src=x_ref)]
        first += [copy(1 + j, me, (*chip, c), src=x_ref) for j, chip in enumerate(chips)]
        for cp in first:
            cp.start()
        passed = [copy(4 + j, (*chip, c), sibling) for j, chip in enumerate(chips)]
        for j, chip in enumerate(chips):
            copy(1 + j, (*chip, c), me).wait_recv()
            passed[j].start()
        copy(0, sibling, me).wait_recv()
        for j, chip in enumerate(chips):
            copy(4 + j, (*chip, 1 - c), me).wait_recv()
        for cp in first + passed:
            cp.wait_send()
        mine.wait()

    return pl.pallas_call(
        body, name=name,
        in_specs=[_ANY], out_specs=_ANY,
        out_shape=jax.ShapeDtypeStruct((N_DEV,) + shard.shape, shard.dtype),
        scratch_shapes=[pltpu.SemaphoreType.DMA((N_DEV - 1,)), pltpu.SemaphoreType.DMA((N_DEV - 1,)),
                        pltpu.SemaphoreType.DMA],
        compiler_params=pltpu.CompilerParams(has_side_effects=True),
    )(shard)


_HBM = pl.BlockSpec(memory_space=pltpu.HBM)
_SEM = pl.BlockSpec(memory_space=pltpu.SEMAPHORE)
_EFFECT = pltpu.SideEffectType.DATAFLOW_SIDE_EFFECTING
NPEER = N_DEV - 1


def _peer_table():
    x, y, c = lax.axis_index("x"), lax.axis_index("y"), lax.axis_index("c")
    peers = []
    for k in range(1, N_DEV):
        px = 1 - x if k & 4 else x
        py = 1 - y if k & 2 else y
        pc = 1 - c if k & 1 else c
        peers.append(((px, py, pc), 4 * px + 2 * py + pc))
    return 4 * x + 2 * y + c, peers


def _split_copy(ins, lands, send_sems, recv_sems, gather, me, peers, w, k, arriving):
    dev, idx = peers[k]
    return pltpu.make_async_remote_copy(
        src_ref=ins[w] if gather else ins[w].at[idx],
        dst_ref=lands[w].at[idx if arriving else me],
        send_sem=send_sems.at[w * NPEER + k], recv_sem=recv_sems.at[w * NPEER + k],
        device_id=dev, device_id_type=pl.DeviceIdType.MESH)


def _exchange_start(arrs, *, gather, name, deps=()):
    n = len(arrs)
    land_shapes = [(N_DEV,) + (a.shape if gather else a.shape[1:]) for a in arrs]

    def body(*refs):
        ins, lands = refs[:n], refs[n:2 * n]
        send_sems, recv_sems = refs[2 * n + len(deps)], refs[2 * n + len(deps) + 1]
        token = refs[-1]
        me, peers = _peer_table()
        for w in range(n):
            for k in range(NPEER):
                _split_copy(ins, lands, send_sems, recv_sems, gather, me, peers, w, k, False).start()
        token[...] = jnp.zeros_like(token)

    out_shape = ([pltpu.SemaphoreType.DMA((n * NPEER,)), pltpu.SemaphoreType.DMA((n * NPEER,))]
                 + [pltpu.HBM(a.shape, a.dtype) for a in arrs]
                 + [pltpu.HBM(s, a.dtype) for s, a in zip(land_shapes, arrs)]
                 + [jax.ShapeDtypeStruct((8, LANES), F32)])
    res = pl.pallas_call(
        body, name=name,
        in_specs=[_HBM] * (2 * n) + [_ANY] * len(deps),
        out_specs=[_SEM, _SEM] + [_HBM] * (2 * n) + [pl.BlockSpec(memory_space=pltpu.VMEM)],
        out_shape=out_shape,
        input_output_aliases={i: 2 + i for i in range(2 * n)},
        compiler_params=pltpu.CompilerParams(has_side_effects=_EFFECT),
    )(*[pltpu.with_memory_space_constraint(a, pltpu.HBM) for a in arrs],
      *[pltpu.with_memory_space_constraint(lax.empty(s, a.dtype), pltpu.HBM) for s, a in zip(land_shapes, arrs)],
      *deps)
    return (n, gather, res[0], res[1], res[2:2 + n], res[2 + n:2 + 2 * n]), res[-1]


def _exchange_wait(handle, after, *, name):
    n, gather, send_sems, recv_sems, ins_thru, lands_thru = handle

    def body(*refs):
        ins, lands = refs[:n], refs[n:2 * n]
        send_s, recv_s = refs[2 * n], refs[2 * n + 1]
        me, peers = _peer_table()
        for w in range(n):
            for k in range(NPEER):
                _split_copy(ins, lands, send_s, recv_s, gather, me, peers, w, k, False).wait_send()
                _split_copy(ins, lands, send_s, recv_s, gather, me, peers, w, k, True).wait_recv()

    res = pl.pallas_call(
        body, name=name,
        in_specs=[_HBM] * (2 * n) + [_SEM, _SEM, pl.BlockSpec(memory_space=pl.ANY)],
        out_specs=[_HBM] * (2 * n),
        out_shape=[pltpu.HBM(a.shape, a.dtype) for a in list(ins_thru) + list(lands_thru)],
        input_output_aliases={i: i for i in range(2 * n)},
        compiler_params=pltpu.CompilerParams(has_side_effects=_EFFECT),
    )(*ins_thru, *lands_thru, send_sems, recv_sems, after)
    return res[:n], res[n:2 * n]


def _ordered_sum(s_ref, own_ref):
    if own_ref is None:
        blocks = [s_ref[q].astype(F32) for q in range(N_DEV)]
    else:
        me = 4 * lax.axis_index("x") + 2 * lax.axis_index("y") + lax.axis_index("c")
        own = own_ref[...]
        blocks = [jnp.where(me == q, own, s_ref[q]).astype(F32) for q in range(N_DEV)]
    acc = blocks[0]
    for b in blocks[1:]:
        acc = acc + b
    return acc


def _sum8(stack, own, *, name):
    _, R, C = stack.shape
    if R % 8 == 0:
        tr, tc = _pick(R, max(8, STEP_BYTES // (C * 4 * (N_DEV + 2))), 8), C
    else:
        tr, tc = R, _pick(C, max(LANES, STEP_BYTES // (R * 4 * (N_DEV + 2))))

    def body(s_ref, own_ref, o_ref):
        o_ref[...] = _ordered_sum(s_ref, own_ref)

    blk = pl.BlockSpec((tr, tc), lambda i, j: (i, j))
    return pl.pallas_call(
        body, name=name, grid=(R // tr, C // tc),
        in_specs=[pl.BlockSpec((N_DEV, tr, tc), lambda i, j: (0, i, j)), blk],
        out_specs=blk,
        out_shape=jax.ShapeDtypeStruct((R, C), F32),
        compiler_params=_cparams("parallel", "parallel"),
    )(stack, own)


def _adamw_math(w, g, m, v):
    m = ADAM_B1 * m + (1.0 - ADAM_B1) * g
    v = ADAM_B2 * v + (1.0 - ADAM_B2) * (g * g)
    m_hat = m / (1.0 - ADAM_B1 ** ADAM_STEP)
    v_hat = v / (1.0 - ADAM_B2 ** ADAM_STEP)
    delta = -ADAM_LR * (m_hat / (jnp.sqrt(v_hat) + ADAM_EPS) + ADAM_WD * w)
    return delta, m, v


def _adamw(w, g, m, v, *, name, stacked, own=None, transposed=False):
    R, C = w.shape
    if transposed:
        tr = _pick(R, max(LANES, STEP_BYTES // (C * 4 * (9 + N_DEV))))
    else:
        tr = _pick(R, max(8, STEP_BYTES // (C * 4 * (8 + (N_DEV if stacked else 1)))), 8)
    has_own = own is not None

    def body(w_ref, g_ref, m_ref, v_ref, *rest):
        go_ref, d_ref, mo_ref, vo_ref = rest[-4:]
        g = _ordered_sum(g_ref, rest[0] if has_own else None) if stacked else g_ref[...]
        if transposed:
            g = g.T
        delta, m2, v2 = _adamw_math(w_ref[...], g, m_ref[...], v_ref[...])
        go_ref[...] = g
        d_ref[...] = delta
        mo_ref[...] = m2
        vo_ref[...] = v2

    row = pl.BlockSpec((tr, C), lambda i: (i, 0))
    if transposed:
        g_spec, own_spec = pl.BlockSpec((N_DEV, C, tr), lambda i: (0, 0, i)), pl.BlockSpec((C, tr), lambda i: (0, i))
    else:
        g_spec, own_spec = (pl.BlockSpec((N_DEV, tr, C), lambda i: (0, i, 0)) if stacked else row), row
    return pl.pallas_call(
        body, name=name, grid=(R // tr,),
        in_specs=[row, g_spec, row, row] + [own_spec] * has_own, out_specs=[row] * 4,
        out_shape=[jax.ShapeDtypeStruct((R, C), F32)] * 4,
        compiler_params=_cparams("parallel"),
    )(w, g, m, v, *([own] if has_own else []))


def kernel(x, positions, attn_norm, w_in, fox_f_bias, swa_sinks, w_branch_swa, w_branch_fox, w_out, mlp_norm, w_up, w_down, final_norm, loss_target, m_attn_norm, m_w_in, m_fox_f_bias, m_swa_sinks, m_w_branch_swa, m_w_branch_fox, m_w_out, m_mlp_norm, m_w_up, m_w_down, m_final_norm, v_attn_norm, v_w_in, v_fox_f_bias, v_swa_sinks, v_w_branch_swa, v_w_branch_fox, v_w_out, v_mlp_norm, v_w_up, v_w_down, v_final_norm):
    S, D = x.shape[1], x.shape[2]
    DFF = w_up.shape[2] * N_DEV
    d_in = w_in.shape[2] * N_DEV
    assert d_in == QKV_W + FOX_HEADS + 2 * D and (2 * D) % SWA_Q_W == 0 and S % (4 * LANES) == 0
    q_off = 2 * D
    k_off = q_off + SWA_Q_W
    v_off = k_off + SWA_KV_W
    fq_off = v_off + SWA_KV_W
    fk_off = fq_off + FOX_W
    fv_off = fk_off + FOX_W
    fl_off = fv_off + FOX_W
    NP = fl_off + FL_PAD
    x2d, tgt = x[0], loss_target[0]

    shards = [w_in[0].T.astype(BF16), w_branch_swa[0].T.astype(BF16), w_branch_fox[0].T.astype(BF16),
              w_out[0].astype(BF16), w_up[0].T.astype(BF16), w_down[0].astype(BF16)]
    me = 4 * lax.axis_index("x") + 2 * lax.axis_index("y") + lax.axis_index("c")

    def filled(stack, own):
        return lax.dynamic_update_slice(stack, own[None], (me,) + (0,) * own.ndim)

    g_in = _gather_two_level(shards[0], name="gather_w_in")
    h_rest, tok_rest = _exchange_start(shards[1:], gather=True, name="gather_rest_start", deps=[g_in])

    tm = _pick(S, 1024)
    td = _pick(D, 1024)
    tf = _pick(DFF, 1024)
    tnp = _pick(NP, 1024)

    h1 = _rms_fwd(x2d, attn_norm, name="rms1", deps=[tok_rest])
    w_in_t = g_in.reshape(d_in, D)
    w_in_p = jnp.concatenate([w_in_t[QKV_W + FOX_HEADS:], w_in_t[:QKV_W], w_in_t[QKV_W:QKV_W + FOX_HEADS],
                              jnp.zeros((FL_PAD - FOX_HEADS, D), BF16)], axis=0)
    w_fl_t = w_in_t[QKV_W:QKV_W + FOX_HEADS]
    proj, = _matmul(h1, w_in_p, mode="nt", name="mm_in", out_dtypes=[BF16], tm=_pick(S, 2048), tn=tnp, tk=D)
    z_sd, = _matmul(h1, w_fl_t, mode="nt", name="mm_flogit", out_dtypes=[F32], tm=tm, tn=FOX_HEADS, tk=D)
    z_t = z_sd.T
    bias_col = fox_f_bias.reshape(FOX_HEADS, 1)
    negc = _fox_prep(z_t, bias_col, name="fox_prep")
    (fbq, fbk), (bbq, bbk) = _fox_blocks(S)
    inv_freq = ROPE_THETA ** (-jnp.arange(0, HEAD_DIM, 2, dtype=F32) / HEAD_DIM)
    invf = jnp.tile(inv_freq, LANES // (HEAD_DIM // 2)).reshape(1, LANES)
    cos_t, sin_t = _rope_tables(positions.reshape(S, 1), invf, name="rope_tables")
    q_rope, k_rope = _rope_fwd(proj, cos_t, sin_t, q_off=q_off, k_off=k_off, name="rope_fwd")
    sinks = swa_sinks.reshape(-1)
    swa_mask = _swa_mask_bias()
    o_a = _swa_fwd(q_rope, k_rope, proj, sinks, swa_mask, v_off=v_off, name="swa_fwd")
    o_b, lse = _fox_fwd(proj, _key_bias_blocks(negc, fbk), q_off=fq_off, k_off=fk_off, v_off=fv_off,
                        bq=fbq, bk=fbk, name="fox_fwd")
    s_rest, g_rest = _exchange_wait(h_rest, o_b, name="gather_rest_wait")
    g_bs, g_bf, g_o, g_up, g_dn = [filled(g, s) for g, s in zip(g_rest, s_rest)]
    w_bs_t = g_bs.reshape(D, SWA_Q_W)
    w_bf_t = g_bf.reshape(D, FOX_W)
    w_o = g_o.reshape(D, D)
    w_up_t = g_up.reshape(DFF, D)
    w_dn = g_dn.reshape(DFF, D)
    ya, = _matmul(o_a, w_bs_t, mode="nt", name="mm_branch_swa", out_dtypes=[BF16], tm=tm, tn=td, tk=SWA_Q_W)
    gate_maps = [lambda i, j, k: (i, j), lambda i, j, k: (i, j), lambda i, j, k: (i, j + D // td)]

    def merge_epi(acc, ya_t, ga_t, gb_t):
        merged = _sigmoid(ga_t.astype(F32)) * ya_t.astype(F32) + _sigmoid(gb_t.astype(F32)) * acc
        return acc, merged

    yb, merged = _matmul(o_b, w_bf_t, mode="nt", name="mm_branch_fox", out_dtypes=[BF16, BF16],
                         tm=tm, tn=td, tk=FOX_W, extras=[ya, proj, proj], extra_maps=gate_maps,
                         epilogue=merge_epi)
    x_mid, = _matmul(merged, w_o, mode="nn", name="mm_out", out_dtypes=[F32], tm=tm, tn=td, tk=D,
                     extras=[x2d], epilogue=lambda acc, r: (acc + r,))
    h2 = _rms_fwd(x_mid, mlp_norm, name="rms2")
    u, = _matmul(h2, w_up_t, mode="nt", name="mm_up", out_dtypes=[BF16], tm=_pick(S, 2048), tn=tf, tk=D,
                 epilogue=lambda acc: (jnp.maximum(acc, 0.0),))
    x_fin, = _matmul(u, w_dn, mode="nn", name="mm_down", out_dtypes=[F32], tm=tm, tn=td, tk=_pick(DFF, 2048),
                     a_fn=_square_bf16, extras=[x_mid], epilogue=lambda acc, r: (acc + r,))

    dx3b, dg3, loss_part = _loss_head(x_fin, tgt, final_norm.reshape(1, D), name="loss_head")
    d_up, = _matmul(dx3b, w_dn, mode="nt", name="mm_d_act", out_dtypes=[BF16], tm=_pick(S, 2048), tn=tf, tk=D,
                    extras=[u], epilogue=lambda acc, ut: (acc * (2.0 * ut.astype(F32)),))
    tks = _pick(S, 2048)
    dw_dn, = _matmul(u, dx3b, mode="tn", name="mm_dw_down", out_dtypes=[BF16], tm=tf, tn=td, tk=tks,
                     a_fn=_square_bf16)
    dh2, = _matmul(d_up, w_up_t, mode="nn", name="mm_dh2", out_dtypes=[BF16], tm=tm, tn=td, tk=_pick(DFF, 2048))
    dw_up_t, = _matmul(d_up, h2, mode="tn", name="mm_dw_up", out_dtypes=[BF16], tm=tf, tn=td, tk=tks)
    h_s1, tok_s1 = _exchange_start([dw_up_t.reshape(N_DEV, DFF // N_DEV, D), dw_dn.reshape(N_DEV, DFF // N_DEV, D)],
                                   gather=False, name="scatter_mlp_start")
    dx2b, dg2 = _rms_bwd(dh2, x_mid, mlp_norm, dx3b, name="rms2_bwd", out_dtype=BF16, deps=[tok_s1])

    def gate_bwd_epi(dm, ya_t, yb_t, ga_t, gb_t):
        sa, sb = _sigmoid(ga_t.astype(F32)), _sigmoid(gb_t.astype(F32))
        return (dm * sa, dm * sb, dm * ya_t.astype(F32) * sa * (1.0 - sa), dm * yb_t.astype(F32) * sb * (1.0 - sb))

    gmaps = [lambda i, j, k: (i, j), lambda i, j, k: (i, j), lambda i, j, k: (i, j),
             lambda i, j, k: (i, j + D // td)]
    d_ya, d_yb, d_ga, d_gb = _matmul(dx2b, w_o, mode="nt", name="mm_d_merged", out_dtypes=[BF16] * 4,
                                     tm=tm, tn=td, tk=D, extras=[ya, yb, proj, proj], extra_maps=gmaps,
                                     epilogue=gate_bwd_epi)
    dw_o, = _matmul(merged, dx2b, mode="tn", name="mm_dw_out", out_dtypes=[BF16], tm=td, tn=td, tk=tks)
    d_oa, = _matmul(d_ya, w_bs_t, mode="nn", name="mm_d_oa", out_dtypes=[BF16], tm=tm, tn=SWA_Q_W, tk=D)
    d_ob, = _matmul(d_yb, w_bf_t, mode="nn", name="mm_d_ob", out_dtypes=[BF16], tm=tm, tn=FOX_W, tk=D)
    dw_bs_t, = _matmul(d_ya, o_a, mode="tn", name="mm_dw_bs", out_dtypes=[BF16], tm=td, tn=SWA_Q_W, tk=tks)
    dw_bf_t, = _matmul(d_yb, o_b, mode="tn", name="mm_dw_bf", out_dtypes=[BF16], tm=td, tn=FOX_W, tk=tks)
    h_s2, tok_s2 = _exchange_start([dw_bs_t.reshape(N_DEV, D // N_DEV, SWA_Q_W),
                                    dw_bf_t.reshape(N_DEV, D // N_DEV, FOX_W), dw_o.reshape(N_DEV, D // N_DEV, D)],
                                   gather=False, name="scatter_attn_start")
    def row_blocks_t(a):
        return a.reshape(S // bbq, bbq, FOX_W).transpose(0, 2, 1)

    d_fq, d_fk, d_fv, dcol4, drow4 = _fox_bwd(proj, _key_bias_blocks(negc, bbk), o_b, lse, d_ob,
                                              row_blocks_t(proj[:, fq_off:fq_off + FOX_W]), row_blocks_t(d_ob),
                                              q_off=fq_off, k_off=fk_off, v_off=fv_off, bq=bbq, bk=bbk,
                                              name="fox_bwd", deps=[tok_s2])
    dcol = dcol4.transpose(0, 2, 1, 3).reshape(FOX_HEADS, S)
    drow = drow4.transpose(0, 2, 1, 3).reshape(FOX_HEADS, S)
    dz_t, dbias_l = _fox_post(drow, dcol, z_t, bias_col, name="fox_post")
    d_aq, dk_c, dk_p, dv_c, dv_p, dsink_l = _swa_bwd(q_rope, k_rope, proj, sinks, d_oa, cos_t, sin_t, swa_mask,
                                                     v_off=v_off, name="swa_bwd")
    d_ak, d_av = _rope_bwd(dk_c, dk_p, dv_c, dv_p, cos_t, sin_t, name="rope_bwd")
    dz_pad = jnp.pad(dz_t.T.astype(BF16), ((0, 0), (0, FL_PAD - FOX_HEADS)))
    d_proj = jnp.concatenate([d_ga, d_gb, d_aq, d_ak, d_av, d_fq, d_fk, d_fv, dz_pad], axis=1)
    tkp = _pick(NP, 2304)
    dw_in_p, = _matmul(d_proj, h1, mode="tn", name="mm_dw_in", out_dtypes=[BF16], tm=_pick(NP, 512), tn=D, tk=tks)
    dw_in_t = jnp.concatenate([dw_in_p[q_off:q_off + QKV_W], dw_in_p[fl_off:fl_off + FOX_HEADS], dw_in_p[:q_off]],
                              axis=0)
    h_s3, tok_s3 = _exchange_start([dw_in_t.reshape(N_DEV, d_in // N_DEV, D)], gather=False,
                                   name="scatter_in_start")
    dh1, = _matmul(d_proj, w_in_p, mode="nn", name="mm_dh1", out_dtypes=[BF16], tm=tm, tn=td, tk=tkp, deps=[tok_s3])
    dx, dg1 = _rms_bwd(dh1, x2d, attn_norm, dx2b, name="rms1_bwd", out_dtype=F32)

    dbias = dbias_l[:, 0]
    dsinks = dsink_l[:, :, 0].reshape(-1)
    nsm = 3 * D + 2 * LANES
    tail = jnp.zeros((2 * LANES,), F32)
    small_g = jnp.concatenate([dg1[0], dg2[0], dg3[0],
                               tail.at[0:16].set(dbias).at[16:32].set(dsinks).at[32].set(loss_part[0, 0])])

    def pack(a_norm, b_norm, f_norm, bias, snk):
        return jnp.concatenate([a_norm[0], b_norm[0], f_norm,
                                tail.at[0:16].set(bias[0]).at[16:32].set(snk[0])]).reshape(1, nsm)

    small_stack, = _exchange([small_g.reshape(1, nsm)], gather=True, name="gather_small")
    u_sm = _adamw(pack(attn_norm, mlp_norm, final_norm, fox_f_bias, swa_sinks), small_stack,
                  pack(m_attn_norm, m_mlp_norm, m_final_norm, m_fox_f_bias, m_swa_sinks),
                  pack(v_attn_norm, v_mlp_norm, v_final_norm, v_fox_f_bias, v_swa_sinks),
                  name="adamw_small", stacked=True)
    loss = u_sm[0][0, 3 * D + 32]

    def own_of(src):
        return lax.dynamic_index_in_dim(src, me, 0, keepdims=False)

    def update_t(stack, src, w, m, v, nm):
        g = _sum8(stack, own_of(src), name="sum_" + nm).T
        return _adamw(w[0], g, m[0], v[0], name="adamw_" + nm, stacked=False)

    def update(stack, src, w, m, v, nm, transposed=False):
        return _adamw(w[0], stack, m[0], v[0], name="adamw_" + nm, stacked=True, own=own_of(src),
                      transposed=transposed)

    (s_up, s_dn), (r_up, r_dn) = _exchange_wait(h_s1, u_sm[1], name="scatter_mlp_wait")
    u_up = update(r_up, s_up, w_up, m_w_up, v_w_up, "w_up", transposed=True)
    u_dn = update(r_dn, s_dn, w_down, m_w_down, v_w_down, "w_down")
    (s_bs, s_bf, s_o), (r_bs, r_bf, r_o) = _exchange_wait(h_s2, u_dn[1], name="scatter_attn_wait")
    u_bs = update(r_bs, s_bs, w_branch_swa, m_w_branch_swa, v_w_branch_swa, "w_bs", transposed=True)
    u_bf = update(r_bf, s_bf, w_branch_fox, m_w_branch_fox, v_w_branch_fox, "w_bf", transposed=True)
    u_o = update(r_o, s_o, w_out, m_w_out, v_w_out, "w_out")
    (s_w_in,), (r_in,) = _exchange_wait(h_s3, u_o[1], name="scatter_in_wait")
    u_in = update_t(r_in, s_w_in, w_in, m_w_in, v_w_in, "w_in")

    def small(kind):
        a = u_sm[kind][0]
        return dict(attn_norm=a[0:D][None], mlp_norm=a[D:2 * D][None], final_norm=a[2 * D:3 * D],
                    fox_f_bias=a[3 * D:3 * D + 16][None], swa_sinks=a[3 * D + 16:3 * D + 32][None])

    big = dict(w_in=u_in, w_branch_swa=u_bs, w_branch_fox=u_bf, w_out=u_o, w_up=u_up, w_down=u_dn)
    order = ["attn_norm", "w_in", "fox_f_bias", "swa_sinks", "w_branch_swa", "w_branch_fox", "w_out", "mlp_norm",
             "w_up", "w_down", "final_norm"]
    outs = [loss, dx[None]]
    for kind in range(4):
        sm = small(kind)
        for nm in order:
            outs.append(big[nm][kind][None] if nm in big else sm[nm])
    return tuple(outs)
```

```python
import functools

import jax
import jax.numpy as jnp
from jax import lax
from jax.experimental import pallas as pl
from jax.experimental.pallas import tpu as pltpu

F32 = jnp.float32
BF16 = jnp.bfloat16

N_DEV = 8
HEAD_DIM = 64
SWA_Q_W = 1024
SWA_KV_W = 128
SWA_GROUP = 8
WINDOW = 128
FOX_W = 1024
FOX_HEADS = 16
QKV_W = SWA_Q_W + 2 * SWA_KV_W + 3 * FOX_W
FL_PAD = 256
ROPE_THETA = 10000.0
RMS_EPS = 1e-6
ATT_SCALE = 0.125
NEG = -1e30

ADAM_LR = 0.001
ADAM_B1 = 0.9
ADAM_B2 = 0.999
ADAM_EPS = 1e-08
ADAM_WD = 0.01
ADAM_STEP = 10

FOX_FWD_BLOCKS = (1024, 1024)
FOX_BWD_BLOCKS = (1024, 512)
FOX_FWD_PAIRS = 2

LANES = 128
VMEM_LIMIT = 56 * 1024 * 1024
STEP_BYTES = 12 * 1024 * 1024


def _cparams(*sem):
    return pltpu.CompilerParams(dimension_semantics=sem, vmem_limit_bytes=VMEM_LIMIT)


def _pick(dim, pref, align=LANES):
    best = None
    t = align
    while t <= min(dim, pref):
        if dim % t == 0:
            best = t
        t += align
    return best if best is not None else dim


_DIMS = {"nn": ((1,), (0,)), "nt": ((1,), (1,)), "tn": ((0,), (0,))}


_ANY = pl.BlockSpec(memory_space=pl.ANY)


def _matmul(a, b, *, mode, name, out_dtypes, tm, tn, tk, extras=(), extra_maps=None, extra_shapes=None,
            a_fn=None, epilogue=None, deps=()):
    if mode == "nn":
        (M, K), (K2, N) = a.shape, b.shape
    elif mode == "nt":
        (M, K), (N, K2) = a.shape, b.shape
    else:
        (K, M), (K2, N) = a.shape, b.shape
    assert K == K2, (name, a.shape, b.shape)
    assert M % tm == 0 and N % tn == 0 and K % tk == 0, (name, M, N, K, tm, tn, tk)
    nk = K // tk
    ne, no = len(extras), len(out_dtypes)
    dims = (_DIMS[mode], ((), ()))

    def body(*refs):
        a_ref, b_ref = refs[0], refs[1]
        ex_refs = refs[2:2 + ne]
        out_refs = refs[2 + ne + len(deps):2 + ne + len(deps) + no]

        def finish(acc):
            res = (acc,) if epilogue is None else epilogue(acc, *[e[...] for e in ex_refs])
            for o_ref, r in zip(out_refs, res):
                o_ref[...] = r.astype(o_ref.dtype)

        def product():
            av = a_ref[...]
            if a_fn is not None:
                av = a_fn(av)
            return lax.dot_general(av, b_ref[...], dims, preferred_element_type=F32)

        if nk == 1:
            finish(product())
        else:
            acc_ref = refs[-1]
            k = pl.program_id(2)

            @pl.when(k == 0)
            def _():
                acc_ref[...] = jnp.zeros_like(acc_ref)

            acc_ref[...] += product()

            @pl.when(k == nk - 1)
            def _():
                finish(acc_ref[...])

    if mode == "tn":
        a_spec = pl.BlockSpec((tk, tm), lambda i, j, k: (k, i))
    else:
        a_spec = pl.BlockSpec((tm, tk), lambda i, j, k: (i, k))
    if mode == "nt":
        b_spec = pl.BlockSpec((tn, tk), lambda i, j, k: (j, k))
    else:
        b_spec = pl.BlockSpec((tk, tn), lambda i, j, k: (k, j))
    if extra_maps is None:
        extra_maps = [lambda i, j, k: (i, j)] * ne
    if extra_shapes is None:
        extra_shapes = [None] * ne
    ex_specs = [pl.BlockSpec(s or (tm, tn), m) for s, m in zip(extra_shapes, extra_maps)]
    out_spec = [pl.BlockSpec((tm, tn), lambda i, j, k: (i, j)) for _ in range(no)]
    res = pl.pallas_call(
        body,
        name=name,
        grid=(M // tm, N // tn, nk),
        in_specs=[a_spec, b_spec] + ex_specs + [_ANY] * len(deps),
        out_specs=out_spec,
        out_shape=[jax.ShapeDtypeStruct((M, N), d) for d in out_dtypes],
        scratch_shapes=[pltpu.VMEM((tm, tn), F32)] if nk > 1 else [],
        compiler_params=_cparams("parallel", "parallel", "arbitrary"),
    )(a, b, *extras, *deps)
    return res


def _square_bf16(t):
    tf = t.astype(F32)
    return (tf * tf).astype(BF16)


def _sigmoid(g):
    return 1.0 / (1.0 + jnp.exp(-g))


def _rms_fwd(x, gain, *, name, deps=()):
    S, D = x.shape
    tr = _pick(S, 512, 8)

    def body(x_ref, g_ref, *rest):
        h_ref = rest[-1]
        xv = x_ref[...]
        r = lax.rsqrt(jnp.mean(xv * xv, axis=-1, keepdims=True) + RMS_EPS)
        h_ref[...] = (xv * r * g_ref[...]).astype(BF16)

    return pl.pallas_call(
        body, name=name, grid=(S // tr,),
        in_specs=[pl.BlockSpec((tr, D), lambda i: (i, 0)), pl.BlockSpec((1, D), lambda i: (0, 0))] + [_ANY] * len(deps),
        out_specs=pl.BlockSpec((tr, D), lambda i: (i, 0)),
        out_shape=jax.ShapeDtypeStruct((S, D), BF16),
        compiler_params=_cparams("parallel"),
    )(x, gain, *deps)


def _rms_bwd(dh, x, gain, dres, *, name, out_dtype, deps=()):
    S, D = x.shape
    tr = _pick(S, 512, 8)

    def body(dh_ref, x_ref, g_ref, dres_ref, *rest):
        outs = rest[len(deps):]
        dx_ref, dg_ref = outs[0], outs[-1]
        xv = x_ref[...]
        r = lax.rsqrt(jnp.mean(xv * xv, axis=-1, keepdims=True) + RMS_EPS)
        xh = xv * r
        dhv = dh_ref[...].astype(F32)
        t = dhv * g_ref[...]
        dx = r * (t - xh * jnp.mean(t * xh, axis=-1, keepdims=True)) + dres_ref[...].astype(F32)
        dx_ref[...] = dx.astype(out_dtype)
        part = jnp.sum(dhv * xh, axis=0, keepdims=True)

        @pl.when(pl.program_id(0) == 0)
        def _():
            dg_ref[...] = part

        @pl.when(pl.program_id(0) > 0)
        def _():
            dg_ref[...] += part

    row = pl.BlockSpec((tr, D), lambda i: (i, 0))
    vec = pl.BlockSpec((1, D), lambda i: (0, 0))
    return pl.pallas_call(
        body, name=name, grid=(S // tr,),
        in_specs=[row, row, vec, row] + [_ANY] * len(deps), out_specs=[row, vec],
        out_shape=[jax.ShapeDtypeStruct((S, D), out_dtype), jax.ShapeDtypeStruct((1, D), F32)],
        compiler_params=_cparams("arbitrary"),
    )(dh, x, gain, dres, *deps)


def _loss_head(x3, target, gain, *, name):
    S, D = x3.shape
    tr = _pick(S, 512, 8)

    def body(x_ref, t_ref, g_ref, dxb_ref, dg_ref, loss_ref):
        xv = x_ref[...]
        r = lax.rsqrt(jnp.mean(xv * xv, axis=-1, keepdims=True) + RMS_EPS)
        xh = xv * r
        gv = g_ref[...]
        err = xh * gv - t_ref[...]
        lpart = jnp.zeros((1, LANES), F32) + (0.5 / D) * jnp.sum(err * err)
        dy = err * (1.0 / D)
        t = dy * gv
        dx = r * (t - xh * jnp.mean(t * xh, axis=-1, keepdims=True))
        dxb_ref[...] = dx.astype(BF16)
        part = jnp.sum(dy * xh, axis=0, keepdims=True)

        @pl.when(pl.program_id(0) == 0)
        def _():
            dg_ref[...] = part
            loss_ref[...] = lpart

        @pl.when(pl.program_id(0) > 0)
        def _():
            dg_ref[...] += part
            loss_ref[...] += lpart

    row = pl.BlockSpec((tr, D), lambda i: (i, 0))
    vec = pl.BlockSpec((1, D), lambda i: (0, 0))
    return pl.pallas_call(
        body, name=name, grid=(S // tr,),
        in_specs=[row, row, vec],
        out_specs=[row, vec, pl.BlockSpec((1, LANES), lambda i: (0, 0))],
        out_shape=[jax.ShapeDtypeStruct((S, D), BF16),
                   jax.ShapeDtypeStruct((1, D), F32), jax.ShapeDtypeStruct((1, LANES), F32)],
        compiler_params=_cparams("arbitrary"),
    )(x3, target, gain)


def _rope_tables(pos_col, invf, *, name):
    S = pos_col.shape[0]
    tr = _pick(S, 512, 8)

    def body(p_ref, f_ref, cos_ref, sin_ref):
        ang = p_ref[...].astype(F32) * f_ref[...]
        lane = lax.broadcasted_iota(jnp.int32, (1, LANES), 1)
        first = (lane % HEAD_DIM) < HEAD_DIM // 2
        sn = jnp.sin(ang)
        cos_ref[...] = jnp.cos(ang)
        sin_ref[...] = jnp.where(first, -sn, sn)

    return pl.pallas_call(
        body, name=name, grid=(S // tr,),
        in_specs=[pl.BlockSpec((tr, 1), lambda i: (i, 0)), pl.BlockSpec((1, LANES), lambda i: (0, 0))],
        out_specs=[pl.BlockSpec((tr, LANES), lambda i: (i, 0))] * 2,
        out_shape=[jax.ShapeDtypeStruct((S, LANES), F32)] * 2,
        compiler_params=_cparams("parallel"),
    )(pos_col, invf)


def _swap_halves(t):
    lane = lax.broadcasted_iota(jnp.int32, (1, LANES), 1)
    first = (lane % HEAD_DIM) < HEAD_DIM // 2
    return jnp.where(first, pltpu.roll(t, LANES - HEAD_DIM // 2, 1), pltpu.roll(t, HEAD_DIM // 2, 1))


def _rope_fwd(proj, cos_t, sin_t, *, q_off, k_off, name):
    S = proj.shape[0]
    tr = _pick(S, 512, 8)
    nqb = SWA_Q_W // LANES

    def body(q_ref, k_ref, c_ref, s_ref, qo_ref, ko_ref):
        cv, sv = c_ref[...], s_ref[...]
        for b in range(nqb):
            t = q_ref[:, b * LANES:(b + 1) * LANES].astype(F32)
            qo_ref[:, b * LANES:(b + 1) * LANES] = (t * cv + _swap_halves(t) * sv).astype(BF16)
        t = k_ref[...].astype(F32)
        ko_ref[...] = (t * cv + _swap_halves(t) * sv).astype(BF16)

    tab = pl.BlockSpec((tr, LANES), lambda i: (i, 0))
    return pl.pallas_call(
        body, name=name, grid=(S // tr,),
        in_specs=[pl.BlockSpec((tr, SWA_Q_W), lambda i: (i, q_off // SWA_Q_W)),
                  pl.BlockSpec((tr, LANES), lambda i: (i, k_off // LANES)), tab, tab],
        out_specs=[pl.BlockSpec((tr, SWA_Q_W), lambda i: (i, 0)), tab],
        out_shape=[jax.ShapeDtypeStruct((S, SWA_Q_W), BF16), jax.ShapeDtypeStruct((S, LANES), BF16)],
        compiler_params=_cparams("parallel"),
    )(proj, proj, cos_t, sin_t)


def _rope_bwd(dk_cur, dk_prev, dv_cur, dv_prev, cos_t, sin_t, *, name):
    S = dk_cur.shape[1]
    tr = _pick(S, 512)
    nb = S // tr

    def body(kc_ref, kp_ref, vc_ref, vp_ref, c_ref, s_ref, dko_ref, dvo_ref):
        cv, sv = c_ref[...], s_ref[...]
        row = pl.program_id(0) * tr + lax.broadcasted_iota(jnp.int32, (tr, 1), 0)
        has_next = row < S - WINDOW
        d = kc_ref[0] + kc_ref[1] + jnp.where(has_next, kp_ref[0] + kp_ref[1], 0.0)
        dko_ref[...] = (d * cv + _swap_halves(d * sv)).astype(BF16)
        dvo_ref[...] = (vc_ref[0] + vc_ref[1] + jnp.where(has_next, vp_ref[0] + vp_ref[1], 0.0)).astype(BF16)

    tab = pl.BlockSpec((tr, LANES), lambda i: (i, 0))
    cur = pl.BlockSpec((2, tr, LANES), lambda i: (0, i, 0))
    return pl.pallas_call(
        body, name=name, grid=(nb,),
        in_specs=[cur, cur, cur, cur, tab, tab],
        out_specs=[tab, tab],
        out_shape=[jax.ShapeDtypeStruct((S, LANES), BF16), jax.ShapeDtypeStruct((S, LANES), BF16)],
        compiler_params=_cparams("parallel"),
    )(dk_cur, dk_prev, dv_cur, dv_prev, cos_t, sin_t)


def _dot_nt(a, b):
    return lax.dot_general(a, b, (((1,), (1,)), ((), ())), preferred_element_type=F32)


def _dot_tn(a, b):
    return lax.dot_general(a, b, (((0,), (0,)), ((), ())), preferred_element_type=F32)


def _dot_nn(a, b):
    return lax.dot_general(a, b, (((1,), (0,)), ((), ())), preferred_element_type=F32)


def _roll_half(t):
    return pltpu.roll(t.astype(F32), HEAD_DIM, 1).astype(t.dtype)


SWA_STACK = SWA_GROUP // 2


def _swa_mask_bias():
    rows = SWA_STACK * WINDOW
    row = lax.broadcasted_iota(jnp.int32, (rows, 2 * WINDOW), 0) % WINDOW
    col = lax.broadcasted_iota(jnp.int32, (rows, 2 * WINDOW), 1)
    diff = row + WINDOW - col
    window = (diff >= 0) & (diff < WINDOW)
    return jnp.stack([jnp.where(window & (col >= WINDOW), 0.0, NEG), jnp.where(window, 0.0, NEG)]).astype(F32)


def _swa_common(kp_ref, kc_ref, vp_ref, vc_ref):
    k2 = jnp.concatenate([kp_ref[...], kc_ref[...]], axis=0)
    v2 = jnp.concatenate([vp_ref[...], vc_ref[...]], axis=0)
    k_sw, v_sw = _roll_half(k2), _roll_half(v2)
    lane = lax.broadcasted_iota(jnp.int32, (1, LANES), 1)
    half = [lane < HEAD_DIM, lane >= HEAD_DIM]
    kk = [[k2 if hk == a else k_sw for a in range(2)] for hk in range(2)]
    vv = [[v2 if hk == a else v_sw for a in range(2)] for hk in range(2)]
    return half, kk, vv


def _swa_stack(ref, hk, mask, scale=None):
    parts = []
    for t in range(SWA_STACK):
        blk = ref[:, (hk * SWA_STACK + t) * LANES:(hk * SWA_STACK + t + 1) * LANES]
        if scale is not None:
            blk = blk * jnp.asarray(scale, blk.dtype)
        parts.append(jnp.where(mask, blk, jnp.zeros_like(blk)))
    return jnp.concatenate(parts, axis=0)


def _swa_sink_column(sink_ref, hk, a):
    blk = lax.broadcasted_iota(jnp.int32, (SWA_STACK * WINDOW, 1), 0) // WINDOW
    col = jnp.zeros((SWA_STACK * WINDOW, 1), F32)
    for t in range(SWA_STACK):
        col = jnp.where(blk == t, sink_ref[hk * SWA_GROUP + 2 * t + a], col)
    return col


def _swa_probs(qm, kk, mask_bias, sink):
    s = _dot_nt(qm, kk) + mask_bias
    m = jnp.maximum(jnp.max(s, axis=1, keepdims=True), sink)
    e = jnp.exp(s - m)
    es = jnp.exp(sink - m)
    inv = 1.0 / (jnp.sum(e, axis=1, keepdims=True) + es)
    return e * inv, es * inv


def _swa_mask_spec():
    return pl.BlockSpec((1, SWA_STACK * WINDOW, 2 * WINDOW), lambda n: (jnp.minimum(n, 1), 0, 0))


def _swa_fwd(q_rope, k_rope, proj, sinks, mask_bias, *, v_off, name):
    S = q_rope.shape[0]
    nb = S // WINDOW

    def body(sink_ref, q_ref, kp_ref, kc_ref, vp_ref, vc_ref, mask_ref, o_ref):
        half, kk, vv = _swa_common(kp_ref, kc_ref, vp_ref, vc_ref)
        for hk in range(2):
            outs = []
            for a in range(2):
                qm = _swa_stack(q_ref, hk, half[a], ATT_SCALE)
                p, _ = _swa_probs(qm, kk[hk][a], mask_ref[0], _swa_sink_column(sink_ref, hk, a))
                outs.append(_dot_nn(p.astype(BF16), vv[hk][a]))
            for t in range(SWA_STACK):
                rows = slice(t * WINDOW, (t + 1) * WINDOW)
                c0 = (hk * SWA_STACK + t) * LANES
                o_ref[:, c0:c0 + LANES] = jnp.where(half[0], outs[0][rows], outs[1][rows]).astype(BF16)

    prev = lambda n: (jnp.maximum(n - 1, 0), 0)
    cur = lambda n: (n, 0)
    vprev = lambda n: (jnp.maximum(n - 1, 0), v_off // LANES)
    vcur = lambda n: (n, v_off // LANES)
    blk = lambda m: pl.BlockSpec((WINDOW, LANES), m)
    return pl.pallas_call(
        body, name=name, grid=(nb,),
        in_specs=[pl.BlockSpec(memory_space=pltpu.SMEM),
                  pl.BlockSpec((WINDOW, SWA_Q_W), lambda n: (n, 0)),
                  blk(prev), blk(cur), blk(vprev), blk(vcur), _swa_mask_spec()],
        out_specs=pl.BlockSpec((WINDOW, SWA_Q_W), lambda n: (n, 0)),
        out_shape=jax.ShapeDtypeStruct((S, SWA_Q_W), BF16),
        compiler_params=_cparams("parallel"),
    )(sinks, q_rope, k_rope, k_rope, proj, proj, mask_bias)


def _swa_bwd(q_rope, k_rope, proj, sinks, d_o, cos_t, sin_t, mask_bias, *, v_off, name):
    S = q_rope.shape[0]
    nb = S // WINDOW

    def body(sink_ref, q_ref, kp_ref, kc_ref, vp_ref, vc_ref, do_ref, c_ref, s_ref, mask_ref,
             dq_ref, dkc_ref, dkp_ref, dvc_ref, dvp_ref, dsink_ref):
        n = pl.program_id(0)
        half, kk, vv = _swa_common(kp_ref, kc_ref, vp_ref, vc_ref)
        allowed = mask_ref[0]
        cv, sv = c_ref[...], s_ref[...]
        srow = lax.broadcasted_iota(jnp.int32, (SWA_GROUP, LANES), 0)
        for hk in range(2):
            dk_acc = jnp.zeros((2 * WINDOW, LANES), F32)
            dv_acc = jnp.zeros((2 * WINDOW, LANES), F32)
            dsink = jnp.zeros((SWA_GROUP, LANES), F32)
            dqs = []
            for a in range(2):
                qm = _swa_stack(q_ref, hk, half[a], ATT_SCALE)
                dom = _swa_stack(do_ref, hk, half[a])
                p, psink = _swa_probs(qm, kk[hk][a], allowed, _swa_sink_column(sink_ref, hk, a))
                dp = _dot_nt(dom, vv[hk][a])
                delta = jnp.sum(p * dp, axis=1, keepdims=True)
                ds = (p * (dp - delta)).astype(BF16)
                dsk = psink * delta
                for t in range(SWA_STACK):
                    dsink = dsink + jnp.where(srow == 2 * t + a, -jnp.sum(dsk[t * WINDOW:(t + 1) * WINDOW]), 0.0)
                dqs.append(_dot_nn(ds, kk[hk][a]) * ATT_SCALE)
                dk_acc = dk_acc + _dot_tn(ds, qm)
                dv_acc = dv_acc + _dot_tn(p.astype(BF16), dom)
            for t in range(SWA_STACK):
                rows = slice(t * WINDOW, (t + 1) * WINDOW)
                d = jnp.where(half[0], dqs[0][rows], dqs[1][rows])
                c0 = (hk * SWA_STACK + t) * LANES
                dq_ref[:, c0:c0 + LANES] = (d * cv + _swap_halves(d * sv)).astype(BF16)
            dk_t = jnp.where(half[hk], dk_acc + pltpu.roll(dk_acc, HEAD_DIM, 1), 0.0)
            dv_t = jnp.where(half[hk], dv_acc + pltpu.roll(dv_acc, HEAD_DIM, 1), 0.0)
            dkp_ref[hk] = dk_t[:WINDOW]
            dkc_ref[hk] = dk_t[WINDOW:]
            dvp_ref[hk] = dv_t[:WINDOW]
            dvc_ref[hk] = dv_t[WINDOW:]

            @pl.when(n == 0)
            def _():
                dsink_ref[hk] = dsink

            @pl.when(n > 0)
            def _():
                dsink_ref[hk] += dsink

    prev = lambda n: (jnp.maximum(n - 1, 0), 0)
    cur = lambda n: (n, 0)
    vprev = lambda n: (jnp.maximum(n - 1, 0), v_off // LANES)
    vcur = lambda n: (n, v_off // LANES)
    blk = lambda m: pl.BlockSpec((WINDOW, LANES), m)
    qblk = pl.BlockSpec((WINDOW, SWA_Q_W), lambda n: (n, 0))
    part = pl.BlockSpec((2, WINDOW, LANES), lambda n: (0, n, 0))
    part_prev = pl.BlockSpec((2, WINDOW, LANES), lambda n: (0, jnp.maximum(n - 1, 0), 0))
    part_shape = jax.ShapeDtypeStruct((2, S, LANES), F32)
    return pl.pallas_call(
        body, name=name, grid=(nb,),
        in_specs=[pl.BlockSpec(memory_space=pltpu.SMEM), qblk, blk(prev), blk(cur), blk(vprev), blk(vcur), qblk,
                  blk(cur), blk(cur), _swa_mask_spec()],
        out_specs=[qblk, part, part_prev, part, part_prev,
                   pl.BlockSpec((2, SWA_GROUP, LANES), lambda n: (0, 0, 0))],
        out_shape=[jax.ShapeDtypeStruct((S, SWA_Q_W), BF16), part_shape, part_shape, part_shape, part_shape,
                   jax.ShapeDtypeStruct((2, SWA_GROUP, LANES), F32)],
        compiler_params=_cparams("arbitrary"),
    )(sinks, q_rope, k_rope, k_rope, proj, proj, d_o, cos_t, sin_t, mask_bias)


def _fox_prep(z_t, bias_col, *, name):
    H, S = z_t.shape
    tb = _pick(S, 512)

    def body(z_ref, b_ref, o_ref, carry_ref):
        @pl.when(pl.program_id(0) == 0)
        def _():
            carry_ref[...] = jnp.zeros_like(carry_ref)

        zz = z_ref[...] + b_ref[...]
        t = jnp.exp(-jnp.abs(zz))
        log1p = jnp.where(t < 1e-2, t * (1.0 - t * (0.5 - t * (1.0 / 3.0))), jnp.log(1.0 + t))
        logf = jnp.minimum(zz, 0.0) - log1p
        r = lax.broadcasted_iota(jnp.int32, (tb, tb), 0)
        c = lax.broadcasted_iota(jnp.int32, (tb, tb), 1)
        tri = (r <= c).astype(BF16)
        hi = logf.astype(BF16)
        r1 = logf - hi.astype(F32)
        mid = r1.astype(BF16)
        lo = (r1 - mid.astype(F32)).astype(BF16)
        cs = _dot_nn(hi, tri) + _dot_nn(mid, tri) + _dot_nn(lo, tri) + carry_ref[:, 0:1]
        o_ref[...] = -cs
        carry_ref[...] = jnp.zeros_like(carry_ref) + cs[:, tb - 1:tb]

    return pl.pallas_call(
        body, name=name, grid=(S // tb,),
        in_specs=[pl.BlockSpec((H, tb), lambda i: (0, i)), pl.BlockSpec((H, 1), lambda i: (0, 0))],
        out_specs=pl.BlockSpec((H, tb), lambda i: (0, i)),
        out_shape=jax.ShapeDtypeStruct((H, S), F32),
        scratch_shapes=[pltpu.VMEM((H, LANES), F32)],
        compiler_params=_cparams("arbitrary"),
    )(z_t, bias_col)


def _fox_post(drow, dcol, z_t, bias_col, *, name):
    H, S = z_t.shape
    tb = _pick(S, 512)
    nb = S // tb

    def body(dr_ref, d_ref, z_ref, b_ref, dz_ref, db_ref, carry_ref):
        @pl.when(pl.program_id(0) == 0)
        def _():
            carry_ref[...] = jnp.zeros_like(carry_ref)
            db_ref[...] = jnp.zeros_like(db_ref)

        dc = dr_ref[...] - d_ref[...]
        r = lax.broadcasted_iota(jnp.int32, (tb, tb), 0)
        c = lax.broadcasted_iota(jnp.int32, (tb, tb), 1)
        tri = (r >= c).astype(BF16)
        hi = dc.astype(BF16)
        r1 = dc - hi.astype(F32)
        mid = r1.astype(BF16)
        lo = (r1 - mid.astype(F32)).astype(BF16)
        dlogf = _dot_nn(hi, tri) + _dot_nn(mid, tri) + _dot_nn(lo, tri) + carry_ref[:, 0:1]
        carry_ref[...] = jnp.zeros_like(carry_ref) + dlogf[:, 0:1]
        dz = dlogf * _sigmoid(-(z_ref[...] + b_ref[...]))
        dz_ref[...] = dz
        db_ref[...] += jnp.sum(dz, axis=1, keepdims=True)

    rev = lambda i: (0, nb - 1 - i)
    return pl.pallas_call(
        body, name=name, grid=(nb,),
        in_specs=[pl.BlockSpec((H, tb), rev), pl.BlockSpec((H, tb), rev), pl.BlockSpec((H, tb), rev),
                  pl.BlockSpec((H, 1), lambda i: (0, 0))],
        out_specs=[pl.BlockSpec((H, tb), rev), pl.BlockSpec((H, LANES), lambda i: (0, 0))],
        out_shape=[jax.ShapeDtypeStruct((H, S), F32), jax.ShapeDtypeStruct((H, LANES), F32)],
        scratch_shapes=[pltpu.VMEM((H, LANES), F32)],
        compiler_params=_cparams("arbitrary"),
    )(drow, dcol, z_t, bias_col)


def _fox_blocks(S):
    cap = max(LANES, S // 4)
    return (min(FOX_FWD_BLOCKS[0], cap), min(FOX_FWD_BLOCKS[1], cap)), \
           (min(FOX_BWD_BLOCKS[0], cap), min(FOX_BWD_BLOCKS[1], cap))


def _key_bias_blocks(negc, bk):
    H, S = negc.shape
    return negc.reshape(H // 2, 2, S // bk, bk).transpose(0, 2, 1, 3)


def _fox_fwd(proj, negc4, *, q_off, k_off, v_off, bq, bk, name):
    S = proj.shape[0]
    nq, nk = S // bq, S // bk
    npair = FOX_HEADS // 2
    assert bq % bk == 0 or bk % bq == 0
    nmask = max(1, bq // bk)

    gp = FOX_FWD_PAIRS
    gw = gp * LANES
    assert q_off % gw == 0 and k_off % gw == 0 and v_off % gw == 0 and npair % gp == 0

    def body(q_ref, k_ref, v_ref, nc_ref, o_ref, lse_ref):
        i = pl.program_id(1)
        lane = lax.broadcasted_iota(jnp.int32, (1, LANES), 1)
        half = [lane < HEAD_DIM, lane >= HEAD_DIM]
        qh = []
        for g in range(gp):
            q2 = q_ref[:, g * LANES:(g + 1) * LANES] * jnp.asarray(ATT_SCALE, BF16)
            qh += [jnp.where(half[h], q2, jnp.zeros_like(q2)) for h in range(2)]
        row = lax.broadcasted_iota(jnp.int32, (bq, bk), 0)
        col = lax.broadcasted_iota(jnp.int32, (bq, bk), 1)
        rel = row - col
        nfull = (i * bq) // bk

        spare = [HEAD_DIM, 0]
        ones_lane = [lane == spare[h] for h in range(2)]

        def step(j, carry, masked):
            start = pl.multiple_of(j * bk, bk)
            new = []
            for g in range(gp):
                ks = k_ref[pl.ds(start, bk), g * LANES:(g + 1) * LANES]
                vs = v_ref[pl.ds(start, bk), g * LANES:(g + 1) * LANES]
                nb = nc_ref[g, j]
                for h in range(2):
                    m, acc = carry[4 * g + 2 * h:4 * g + 2 * h + 2]
                    vh = jnp.where(half[h], vs, jnp.where(ones_lane[h], jnp.ones_like(vs), jnp.zeros_like(vs)))
                    qs, bias = qh[2 * g + h], nb[h:h + 1, :]

                    def update(m, acc, rows, keys):
                        s = _dot_nt(qs[rows], ks[keys]) + bias[:, keys]
                        if masked:
                            s = jnp.where(rel[rows, keys] >= j * bk - i * bq, s, NEG)
                        m_new = jnp.maximum(m[rows], jnp.max(s, axis=1, keepdims=True))
                        p = jnp.exp(s - m_new).astype(BF16)
                        return m_new, jnp.exp(m[rows] - m_new) * acc[rows] + _dot_nn(p, vh[keys])

                    if masked and bq == bk:
                        top, bot, everything = slice(0, bq // 2), slice(bq // 2, bq), slice(0, bk)
                        m_t, acc_t = update(m, acc, top, top)
                        m_b, acc_b = update(m, acc, bot, everything)
                        new += [jnp.concatenate([m_t, m_b], axis=0), jnp.concatenate([acc_t, acc_b], axis=0)]
                    else:
                        new += list(update(m, acc, slice(0, bq), slice(0, bk)))
            return tuple(new)

        init = (jnp.full((bq, 1), NEG, F32), jnp.zeros((bq, LANES), F32)) * (2 * gp)
        carry = lax.fori_loop(0, nfull, lambda j, c: step(j, c, False), init)
        for t in range(nmask):
            carry = step(nfull + t, carry, True)
        for g in range(gp):
            outs, lses = [], []
            for h in range(2):
                m, acc = carry[4 * g + 2 * h:4 * g + 2 * h + 2]
                l = acc[:, spare[h]:spare[h] + 1]
                outs.append(acc * (1.0 / l))
                lses.append(m + jnp.log(l))
            o_ref[:, g * LANES:(g + 1) * LANES] = jnp.where(half[0], outs[0], outs[1]).astype(BF16)
            lse_ref[g] = jnp.where(half[0], lses[0], lses[1])

    seq = lambda off: pl.BlockSpec((S, gw), lambda hp, i: (0, off // gw + hp))
    return pl.pallas_call(
        body, name=name, grid=(npair // gp, nq),
        in_specs=[pl.BlockSpec((bq, gw), lambda hp, i: (i, q_off // gw + hp)), seq(k_off), seq(v_off),
                  pl.BlockSpec((gp, nk, 2, bk), lambda hp, i: (hp, 0, 0, 0))],
        out_specs=[pl.BlockSpec((bq, gw), lambda hp, i: (i, hp)),
                   pl.BlockSpec((gp, bq, LANES), lambda hp, i: (hp, i, 0))],
        out_shape=[jax.ShapeDtypeStruct((S, FOX_W), BF16), jax.ShapeDtypeStruct((npair, S, LANES), F32)],
        compiler_params=_cparams("parallel", "parallel"),
    )(proj, proj, proj, negc4)


def _fox_bwd(proj, negc4, o, lse, d_o, q_t, do_t, *, q_off, k_off, v_off, bq, bk, name, deps=()):
    S = proj.shape[0]
    nq, nk = S // bq, S // bk
    npair = FOX_HEADS // 2
    assert bq % bk == 0 or bk % bq == 0
    nmask = max(1, bk // bq)

    def body(q_ref, k_ref, v_ref, nc_ref, o_ref, lse_ref, do_ref, qt_ref, dot_ref, *rest):
        dqo_ref, dk_ref, dv_ref, dn_ref, dr_ref, delta_ref, rs_ref, dq_ref = rest[len(deps):]
        j = pl.program_id(1)
        lane = lax.broadcasted_iota(jnp.int32, (1, LANES), 1)
        half = [lane < HEAD_DIM, lane >= HEAD_DIM]
        spare = [HEAD_DIM, 0]
        ones_lane = [lane == spare[h] for h in range(2)]
        srow = lax.broadcasted_iota(jnp.int32, (LANES, 1), 0)
        rhalf = [srow < HEAD_DIM, srow >= HEAD_DIM]
        ones_row = [srow == spare[h] for h in range(2)]
        k2, v2 = k_ref[...], v_ref[...]
        one_k = jnp.ones_like(k2)
        kh = [jnp.where(half[h], k2, jnp.where(ones_lane[h], one_k, jnp.zeros_like(k2))) for h in range(2)]
        nb = nc_ref[0, 0]
        row = lax.broadcasted_iota(jnp.int32, (bq, bk), 0)
        col = lax.broadcasted_iota(jnp.int32, (bq, bk), 1)
        rel = row - col
        i_first = (j * bk) // bq

        @pl.when(j == 0)
        def _():
            dq_ref[...] = jnp.zeros_like(dq_ref)
            rs_ref[...] = jnp.zeros_like(rs_ref)
            for b in range(nq):
                prod = do_ref[b * bq:(b + 1) * bq, :].astype(F32) * o_ref[b * bq:(b + 1) * bq, :].astype(F32)
                d0 = jnp.sum(jnp.where(half[0], prod, 0.0), axis=1, keepdims=True)
                d1 = jnp.sum(jnp.where(half[1], prod, 0.0), axis=1, keepdims=True)
                delta_ref[b * bq:(b + 1) * bq, :] = jnp.where(half[0], d0, d1)

        def step(i, carry, masked, r0=0):
            dkt_a, dkt_b, dvt = carry
            dkts = [dkt_a, dkt_b]
            nr = bq - r0
            start = pl.multiple_of(i * bq + r0, LANES)
            q2 = q_ref[pl.ds(start, nr), :] * jnp.asarray(ATT_SCALE, BF16)
            do2 = do_ref[pl.ds(start, nr), :]
            qt = qt_ref[i][:, r0:] * jnp.asarray(ATT_SCALE, BF16)
            dot = dot_ref[i][:, r0:]
            lse2 = lse_ref[0, pl.ds(start, nr), :]
            del2 = delta_ref[pl.ds(start, nr), :]
            dqf = []
            for h in range(2):
                qm = jnp.where(half[h], q2, jnp.zeros_like(q2))
                dom = jnp.where(half[h], do2, jnp.zeros_like(do2))
                qtm = jnp.where(rhalf[h], qt, jnp.where(ones_row[h], jnp.ones_like(qt), jnp.zeros_like(qt)))
                dotm = jnp.where(rhalf[h], dot, jnp.zeros_like(dot))
                c0 = h * HEAD_DIM
                p = jnp.exp(_dot_nt(qm, k2) + nb[h:h + 1, :] - lse2[:, c0:c0 + 1])
                if masked:
                    p = jnp.where(rel[r0:] >= j * bk - i * bq, p, 0.0)
                dp = _dot_nt(dom, v2)
                dsb = (p * (dp - del2[:, c0:c0 + 1])).astype(BF16)
                dvt = dvt + _dot_nn(dotm, p.astype(BF16))
                dkts[h] = dkts[h] + _dot_nn(qtm, dsb)
                dqf.append(_dot_nn(dsb, kh[h]))
            dq_ref[pl.ds(start, nr), :] += jnp.where(half[0], dqf[0], dqf[1]) * ATT_SCALE
            rs_ref[pl.ds(start, nr), :] += jnp.where(ones_lane[0], dqf[0], jnp.where(ones_lane[1], dqf[1], 0.0))
            return dkts[0], dkts[1], dvt

        zero = jnp.zeros((LANES, bk), F32)
        carry = (zero, zero, zero)
        if bq > bk:
            sp = j % (bq // bk)
            carry = lax.switch(sp, [functools.partial(step, i_first, masked=True, r0=s * bk)
                                    for s in range(bq // bk)], carry)
        else:
            for t in range(nmask):
                carry = step(i_first + t, carry, True)
        dkt_a, dkt_b, dvt = lax.fori_loop(i_first + nmask, nq, lambda i, c: step(i, c, False), carry)
        dk_ref[...] = jnp.where(rhalf[0], dkt_a, dkt_b).T.astype(BF16)
        dv_ref[...] = dvt.T.astype(BF16)
        dn_ref[0, 0] = jnp.concatenate([dkt_a[spare[0]:spare[0] + 1], dkt_b[spare[1]:spare[1] + 1]], axis=0)

        @pl.when(j == nk - 1)
        def _():
            dqo_ref[...] = dq_ref[...].astype(BF16)
            for b in range(nq):
                t = rs_ref[b * bq:(b + 1) * bq, :].T
                dr_ref[0, b] = jnp.concatenate([t[spare[0]:spare[0] + 1], t[spare[1]:spare[1] + 1]], axis=0)

    once = pl.Buffered(1)
    seq = lambda off: pl.BlockSpec((S, LANES), lambda hp, j: (0, off // LANES + hp), pipeline_mode=once)
    blk = lambda off: pl.BlockSpec((bk, LANES), lambda hp, j: (j, off // LANES + hp))
    nc = pl.BlockSpec((1, 1, 2, bk), lambda hp, j: (hp, j, 0, 0))
    tsp = pl.BlockSpec((nq, LANES, bq), lambda hp, j: (0, hp, 0), pipeline_mode=once)
    return pl.pallas_call(
        body, name=name, grid=(npair, nk),
        in_specs=[seq(q_off), blk(k_off), blk(v_off), nc, seq(0),
                  pl.BlockSpec((1, S, LANES), lambda hp, j: (hp, 0, 0), pipeline_mode=once), seq(0),
                  tsp, tsp] + [_ANY] * len(deps),
        out_specs=[pl.BlockSpec((S, LANES), lambda hp, j: (0, hp)), blk(0), blk(0), nc,
                   pl.BlockSpec((1, nq, 2, bq), lambda hp, j: (hp, 0, 0, 0))],
        out_shape=[jax.ShapeDtypeStruct((S, FOX_W), BF16), jax.ShapeDtypeStruct((S, FOX_W), BF16),
                   jax.ShapeDtypeStruct((S, FOX_W), BF16), jax.ShapeDtypeStruct((npair, nk, 2, bk), F32),
                   jax.ShapeDtypeStruct((npair, nq, 2, bq), F32)],
        scratch_shapes=[pltpu.VMEM((S, LANES), F32), pltpu.VMEM((S, LANES), F32), pltpu.VMEM((S, LANES), F32)],
        compiler_params=_cparams("parallel", "arbitrary"),
    )(proj, proj, proj, negc4, o, lse, d_o, q_t, do_t, *deps)


def _exchange(arrs, *, gather, name):
    n = len(arrs)
    npeer = N_DEV - 1

    def body(*refs):
        ins, outs = refs[:n], refs[n:2 * n]
        send_sems, recv_sems, loc_sems = refs[2 * n:]
        x, y, c = lax.axis_index("x"), lax.axis_index("y"), lax.axis_index("c")
        me = 4 * x + 2 * y + c
        peers = []
        for k in range(1, N_DEV):
            px = 1 - x if k & 4 else x
            py = 1 - y if k & 2 else y
            pc = 1 - c if k & 1 else c
            peers.append(((px, py, pc), 4 * px + 2 * py + pc))

        def remote(w, k):
            dev, idx = peers[k]
            src = ins[w] if gather else ins[w].at[idx]
            return pltpu.make_async_remote_copy(
                src_ref=src, dst_ref=outs[w].at[me],
                send_sem=send_sems.at[w * npeer + k], recv_sem=recv_sems.at[w * npeer + k],
                device_id=dev, device_id_type=pl.DeviceIdType.MESH)

        def arrival(w, k):
            dev, idx = peers[k]
            src = ins[w] if gather else ins[w].at[idx]
            return pltpu.make_async_remote_copy(
                src_ref=src, dst_ref=outs[w].at[idx],
                send_sem=send_sems.at[w * npeer + k], recv_sem=recv_sems.at[w * npeer + k],
                device_id=dev, device_id_type=pl.DeviceIdType.MESH)

        local = []
        for w in range(n):
            for k in range(npeer):
                remote(w, k).start()
            cp = pltpu.make_async_copy(ins[w] if gather else ins[w].at[me], outs[w].at[me], loc_sems.at[w])
            cp.start()
            local.append(cp)
        for w in range(n):
            for k in range(npeer):
                arrival(w, k).wait_recv()
        for w in range(n):
            for k in range(npeer):
                remote(w, k).wait_send()
            local[w].wait()

    hbm = pl.BlockSpec(memory_space=pl.ANY)
    out_shape = [jax.ShapeDtypeStruct((N_DEV,) + (a.shape if gather else a.shape[1:]), a.dtype) for a in arrs]
    return pl.pallas_call(
        body, name=name,
        in_specs=[hbm] * n, out_specs=[hbm] * n, out_shape=out_shape,
        scratch_shapes=[pltpu.SemaphoreType.DMA((n * npeer,)), pltpu.SemaphoreType.DMA((n * npeer,)),
                        pltpu.SemaphoreType.DMA((n,))],
        compiler_params=pltpu.CompilerParams(has_side_effects=True),
    )(*arrs)


def _gather_two_level(shard, *, name):
    def body(x_ref, out_ref, send_sems, recv_sems, local_sem):
        x, y, c = lax.axis_index("x"), lax.axis_index("y"), lax.axis_index("c")
        me, sibling = (x, y, c), (x, y, 1 - c)
        chips = [(1 - x, y), (x, 1 - y), (1 - x, 1 - y)]

        def slot(px, py, pc):
            return out_ref.at[4 * px + 2 * py + pc]

        def copy(k, block, to, src=None):
            return pltpu.make_async_remote_copy(
                src_ref=slot(*block) if src is None else src, dst_ref=slot(*block),
                send_sem=send_sems.at[k], recv_sem=recv_sems.at[k],
                device_id=to, device_id_type=pl.DeviceIdType.MESH)

        mine = pltpu.make_async_copy(x_ref, slot(*me), local_sem)
        mine.start()
        first = [copy(0, me, sibling, src=x_ref)]
        first += [copy(1 + j, me, (*chip, c), src=x_ref) for j, chip in enumerate(chips)]
        for cp in first:
            cp.start()
        passed = [copy(4 + j, (*chip, c), sibling) for j, chip in enumerate(chips)]
        for j, chip in enumerate(chips):
            copy(1 + j, (*chip, c), me).wait_recv()
            passed[j].start()
        copy(0, sibling, me).wait_recv()
        for j, chip in enumerate(chips):
            copy(4 + j, (*chip, 1 - c), me).wait_recv()
        for cp in first + passed:
            cp.wait_send()
        mine.wait()

    return pl.pallas_call(
        body, name=name,
        in_specs=[_ANY], out_specs=_ANY,
        out_shape=jax.ShapeDtypeStruct((N_DEV,) + shard.shape, shard.dtype),
        scratch_shapes=[pltpu.SemaphoreType.DMA((N_DEV - 1,)), pltpu.SemaphoreType.DMA((N_DEV - 1,)),
                        pltpu.SemaphoreType.DMA],
        compiler_params=pltpu.CompilerParams(has_side_effects=True),
    )(shard)


_HBM = pl.BlockSpec(memory_space=pltpu.HBM)
_SEM = pl.BlockSpec(memory_space=pltpu.SEMAPHORE)
_EFFECT = pltpu.SideEffectType.DATAFLOW_SIDE_EFFECTING
NPEER = N_DEV - 1


def _peer_table():
    x, y, c = lax.axis_index("x"), lax.axis_index("y"), lax.axis_index("c")
    peers = []
    for k in range(1, N_DEV):
        px = 1 - x if k & 4 else x
        py = 1 - y if k & 2 else y
        pc = 1 - c if k & 1 else c
        peers.append(((px, py, pc), 4 * px + 2 * py + pc))
    return 4 * x + 2 * y + c, peers


def _split_copy(ins, lands, send_sems, recv_sems, gather, me, peers, w, k, arriving):
    dev, idx = peers[k]
    return pltpu.make_async_remote_copy(
        src_ref=ins[w] if gather else ins[w].at[idx],
        dst_ref=lands[w].at[idx if arriving else me],
        send_sem=send_sems.at[w * NPEER + k], recv_sem=recv_sems.at[w * NPEER + k],
        device_id=dev, device_id_type=pl.DeviceIdType.MESH)


def _exchange_start(arrs, *, gather, name, deps=()):
    n = len(arrs)
    land_shapes = [(N_DEV,) + (a.shape if gather else a.shape[1:]) for a in arrs]

    def body(*refs):
        ins, lands = refs[:n], refs[n:2 * n]
        send_sems, recv_sems = refs[2 * n + len(deps)], refs[2 * n + len(deps) + 1]
        token = refs[-1]
        me, peers = _peer_table()
        for w in range(n):
            for k in range(NPEER):
                _split_copy(ins, lands, send_sems, recv_sems, gather, me, peers, w, k, False).start()
        token[...] = jnp.zeros_like(token)

    out_shape = ([pltpu.SemaphoreType.DMA((n * NPEER,)), pltpu.SemaphoreType.DMA((n * NPEER,))]
                 + [pltpu.HBM(a.shape, a.dtype) for a in arrs]
                 + [pltpu.HBM(s, a.dtype) for s, a in zip(land_shapes, arrs)]
                 + [jax.ShapeDtypeStruct((8, LANES), F32)])
    res = pl.pallas_call(
        body, name=name,
        in_specs=[_HBM] * (2 * n) + [_ANY] * len(deps),
        out_specs=[_SEM, _SEM] + [_HBM] * (2 * n) + [pl.BlockSpec(memory_space=pltpu.VMEM)],
        out_shape=out_shape,
        input_output_aliases={i: 2 + i for i in range(2 * n)},
        compiler_params=pltpu.CompilerParams(has_side_effects=_EFFECT),
    )(*[pltpu.with_memory_space_constraint(a, pltpu.HBM) for a in arrs],
      *[pltpu.with_memory_space_constraint(lax.empty(s, a.dtype), pltpu.HBM) for s, a in zip(land_shapes, arrs)],
      *deps)
    return (n, gather, res[0], res[1], res[2:2 + n], res[2 + n:2 + 2 * n]), res[-1]


def _exchange_wait(handle, after, *, name):
    n, gather, send_sems, recv_sems, ins_thru, lands_thru = handle

    def body(*refs):
        ins, lands = refs[:n], refs[n:2 * n]
        send_s, recv_s = refs[2 * n], refs[2 * n + 1]
        me, peers = _peer_table()
        for w in range(n):
            for k in range(NPEER):
                _split_copy(ins, lands, send_s, recv_s, gather, me, peers, w, k, False).wait_send()
                _split_copy(ins, lands, send_s, recv_s, gather, me, peers, w, k, True).wait_recv()

    res = pl.pallas_call(
        body, name=name,
        in_specs=[_HBM] * (2 * n) + [_SEM, _SEM, pl.BlockSpec(memory_space=pl.ANY)],
        out_specs=[_HBM] * (2 * n),
        out_shape=[pltpu.HBM(a.shape, a.dtype) for a in list(ins_thru) + list(lands_thru)],
        input_output_aliases={i: i for i in range(2 * n)},
        compiler_params=pltpu.CompilerParams(has_side_effects=_EFFECT),
    )(*ins_thru, *lands_thru, send_sems, recv_sems, after)
    return res[:n], res[n:2 * n]


def _ordered_sum(s_ref, own_ref):
    if own_ref is None:
        blocks = [s_ref[q].astype(F32) for q in range(N_DEV)]
    else:
        me = 4 * lax.axis_index("x") + 2 * lax.axis_index("y") + lax.axis_index("c")
        own = own_ref[...]
        blocks = [jnp.where(me == q, own, s_ref[q]).astype(F32) for q in range(N_DEV)]
    acc = blocks[0]
    for b in blocks[1:]:
        acc = acc + b
    return acc


def _sum8(stack, own, *, name):
    _, R, C = stack.shape
    if R % 8 == 0:
        tr, tc = _pick(R, max(8, STEP_BYTES // (C * 4 * (N_DEV + 2))), 8), C
    else:
        tr, tc = R, _pick(C, max(LANES, STEP_BYTES // (R * 4 * (N_DEV + 2))))

    def body(s_ref, own_ref, o_ref):
        o_ref[...] = _ordered_sum(s_ref, own_ref)

    blk = pl.BlockSpec((tr, tc), lambda i, j: (i, j))
    return pl.pallas_call(
        body, name=name, grid=(R // tr, C // tc),
        in_specs=[pl.BlockSpec((N_DEV, tr, tc), lambda i, j: (0, i, j)), blk],
        out_specs=blk,
        out_shape=jax.ShapeDtypeStruct((R, C), F32),
        compiler_params=_cparams("parallel", "parallel"),
    )(stack, own)


def _adamw_math(w, g, m, v):
    m = ADAM_B1 * m + (1.0 - ADAM_B1) * g
    v = ADAM_B2 * v + (1.0 - ADAM_B2) * (g * g)
    m_hat = m / (1.0 - ADAM_B1 ** ADAM_STEP)
    v_hat = v / (1.0 - ADAM_B2 ** ADAM_STEP)
    delta = -ADAM_LR * (m_hat / (jnp.sqrt(v_hat) + ADAM_EPS) + ADAM_WD * w)
    return delta, m, v


def _adamw(w, g, m, v, *, name, stacked, own=None, transposed=False):
    R, C = w.shape
    if transposed:
        tr = _pick(R, max(LANES, STEP_BYTES // (C * 4 * (9 + N_DEV))))
    else:
        tr = _pick(R, max(8, STEP_BYTES // (C * 4 * (8 + (N_DEV if stacked else 1)))), 8)
    has_own = own is not None

    def body(w_ref, g_ref, m_ref, v_ref, *rest):
        go_ref, d_ref, mo_ref, vo_ref = rest[-4:]
        g = _ordered_sum(g_ref, rest[0] if has_own else None) if stacked else g_ref[...]
        if transposed:
            g = g.T
        delta, m2, v2 = _adamw_math(w_ref[...], g, m_ref[...], v_ref[...])
        go_ref[...] = g
        d_ref[...] = delta
        mo_ref[...] = m2
        vo_ref[...] = v2

    row = pl.BlockSpec((tr, C), lambda i: (i, 0))
    if transposed:
        g_spec, own_spec = pl.BlockSpec((N_DEV, C, tr), lambda i: (0, 0, i)), pl.BlockSpec((C, tr), lambda i: (0, i))
    else:
        g_spec, own_spec = (pl.BlockSpec((N_DEV, tr, C), lambda i: (0, i, 0)) if stacked else row), row
    return pl.pallas_call(
        body, name=name, grid=(R // tr,),
        in_specs=[row, g_spec, row, row] + [own_spec] * has_own, out_specs=[row] * 4,
        out_shape=[jax.ShapeDtypeStruct((R, C), F32)] * 4,
        compiler_params=_cparams("parallel"),
    )(w, g, m, v, *([own] if has_own else []))


def kernel(x, positions, attn_norm, w_in, fox_f_bias, swa_sinks, w_branch_swa, w_branch_fox, w_out, mlp_norm, w_up, w_down, final_norm, loss_target, m_attn_norm, m_w_in, m_fox_f_bias, m_swa_sinks, m_w_branch_swa, m_w_branch_fox, m_w_out, m_mlp_norm, m_w_up, m_w_down, m_final_norm, v_attn_norm, v_w_in, v_fox_f_bias, v_swa_sinks, v_w_branch_swa, v_w_branch_fox, v_w_out, v_mlp_norm, v_w_up, v_w_down, v_final_norm):
    S, D = x.shape[1], x.shape[2]
    DFF = w_up.shape[2] * N_DEV
    d_in = w_in.shape[2] * N_DEV
    assert d_in == QKV_W + FOX_HEADS + 2 * D and (2 * D) % SWA_Q_W == 0 and S % (4 * LANES) == 0
    q_off = 2 * D
    k_off = q_off + SWA_Q_W
    v_off = k_off + SWA_KV_W
    fq_off = v_off + SWA_KV_W
    fk_off = fq_off + FOX_W
    fv_off = fk_off + FOX_W
    fl_off = fv_off + FOX_W
    NP = fl_off + FL_PAD
    x2d, tgt = x[0], loss_target[0]

    shards = [w_in[0].T.astype(BF16), w_branch_swa[0].T.astype(BF16), w_branch_fox[0].T.astype(BF16),
              w_out[0].astype(BF16), w_up[0].T.astype(BF16), w_down[0].astype(BF16)]
    me = 4 * lax.axis_index("x") + 2 * lax.axis_index("y") + lax.axis_index("c")

    def filled(stack, own):
        return lax.dynamic_update_slice(stack, own[None], (me,) + (0,) * own.ndim)

    g_in = _gather_two_level(shards[0], name="gather_w_in")
    h_rest, tok_rest = _exchange_start(shards[1:], gather=True, name="gather_rest_start", deps=[g_in])

    tm = _pick(S, 1024)
    td = _pick(D, 1024)
    tf = _pick(DFF, 1024)
    tnp = _pick(NP, 1024)

    h1 = _rms_fwd(x2d, attn_norm, name="rms1", deps=[tok_rest])
    w_in_t = g_in.reshape(d_in, D)
    w_in_p = jnp.concatenate([w_in_t[QKV_W + FOX_HEADS:], w_in_t[:QKV_W], w_in_t[QKV_W:QKV_W + FOX_HEADS],
                              jnp.zeros((FL_PAD - FOX_HEADS, D), BF16)], axis=0)
    w_fl_t = w_in_t[QKV_W:QKV_W + FOX_HEADS]
    proj, = _matmul(h1, w_in_p, mode="nt", name="mm_in", out_dtypes=[BF16], tm=_pick(S, 2048), tn=tnp, tk=D)
    z_sd, = _matmul(h1, w_fl_t, mode="nt", name="mm_flogit", out_dtypes=[F32], tm=tm, tn=FOX_HEADS, tk=D)
    z_t = z_sd.T
    bias_col = fox_f_bias.reshape(FOX_HEADS, 1)
    negc = _fox_prep(z_t, bias_col, name="fox_prep")
    (fbq, fbk), (bbq, bbk) = _fox_blocks(S)
    inv_freq = ROPE_THETA ** (-jnp.arange(0, HEAD_DIM, 2, dtype=F32) / HEAD_DIM)
    invf = jnp.tile(inv_freq, LANES // (HEAD_DIM // 2)).reshape(1, LANES)
    cos_t, sin_t = _rope_tables(positions.reshape(S, 1), invf, name="rope_tables")
    q_rope, k_rope = _rope_fwd(proj, cos_t, sin_t, q_off=q_off, k_off=k_off, name="rope_fwd")
    sinks = swa_sinks.reshape(-1)
    swa_mask = _swa_mask_bias()
    o_a = _swa_fwd(q_rope, k_rope, proj, sinks, swa_mask, v_off=v_off, name="swa_fwd")
    o_b, lse = _fox_fwd(proj, _key_bias_blocks(negc, fbk), q_off=fq_off, k_off=fk_off, v_off=fv_off,
                        bq=fbq, bk=fbk, name="fox_fwd")
    s_rest, g_rest = _exchange_wait(h_rest, o_b, name="gather_rest_wait")
    g_bs, g_bf, g_o, g_up, g_dn = [filled(g, s) for g, s in zip(g_rest, s_rest)]
    w_bs_t = g_bs.reshape(D, SWA_Q_W)
    w_bf_t = g_bf.reshape(D, FOX_W)
    w_o = g_o.reshape(D, D)
    w_up_t = g_up.reshape(DFF, D)
    w_dn = g_dn.reshape(DFF, D)
    ya, = _matmul(o_a, w_bs_t, mode="nt", name="mm_branch_swa", out_dtypes=[BF16], tm=tm, tn=td, tk=SWA_Q_W)
    gate_maps = [lambda i, j, k: (i, j), lambda i, j, k: (i, j), lambda i, j, k: (i, j + D // td)]

    def merge_epi(acc, ya_t, ga_t, gb_t):
        merged = _sigmoid(ga_t.astype(F32)) * ya_t.astype(F32) + _sigmoid(gb_t.astype(F32)) * acc
        return acc, merged

    yb, merged = _matmul(o_b, w_bf_t, mode="nt", name="mm_branch_fox", out_dtypes=[BF16, BF16],
                         tm=tm, tn=td, tk=FOX_W, extras=[ya, proj, proj], extra_maps=gate_maps,
                         epilogue=merge_epi)
    def out_epi(acc, r, g):
        xm = acc + r
        rr = lax.rsqrt(jnp.mean(xm * xm, axis=-1, keepdims=True) + RMS_EPS)
        return xm, xm * rr * g

    x_mid, h2 = _matmul(merged, w_o, mode="nn", name="mm_out", out_dtypes=[F32, BF16], tm=_pick(S, 512), tn=D, tk=D,
                        extras=[x2d, mlp_norm], extra_maps=[lambda i, j, k: (i, j), lambda i, j, k: (0, 0)],
                        extra_shapes=[None, (1, D)], epilogue=out_epi)
    u, = _matmul(h2, w_up_t, mode="nt", name="mm_up", out_dtypes=[BF16], tm=_pick(S, 2048), tn=tf, tk=D,
                 epilogue=lambda acc: (jnp.maximum(acc, 0.0),))
    x_fin, = _matmul(u, w_dn, mode="nn", name="mm_down", out_dtypes=[F32], tm=tm, tn=td, tk=_pick(DFF, 2048),
                     a_fn=_square_bf16, extras=[x_mid], epilogue=lambda acc, r: (acc + r,))

    dx3b, dg3, loss_part = _loss_head(x_fin, tgt, final_norm.reshape(1, D), name="loss_head")
    d_up, = _matmul(dx3b, w_dn, mode="nt", name="mm_d_act", out_dtypes=[BF16], tm=_pick(S, 2048), tn=tf, tk=D,
                    extras=[u], epilogue=lambda acc, ut: (acc * (2.0 * ut.astype(F32)),))
    tks = _pick(S, 2048)
    dw_dn, = _matmul(u, dx3b, mode="tn", name="mm_dw_down", out_dtypes=[BF16], tm=tf, tn=td, tk=tks,
                     a_fn=_square_bf16)
    dh2, = _matmul(d_up, w_up_t, mode="nn", name="mm_dh2", out_dtypes=[BF16], tm=tm, tn=td, tk=_pick(DFF, 2048))
    dw_up_t, = _matmul(d_up, h2, mode="tn", name="mm_dw_up", out_dtypes=[BF16], tm=tf, tn=td, tk=tks)
    h_s1, tok_s1 = _exchange_start([dw_up_t.reshape(N_DEV, DFF // N_DEV, D), dw_dn.reshape(N_DEV, DFF // N_DEV, D)],
                                   gather=False, name="scatter_mlp_start")
    dx2b, dg2 = _rms_bwd(dh2, x_mid, mlp_norm, dx3b, name="rms2_bwd", out_dtype=BF16, deps=[tok_s1])

    def gate_bwd_epi(dm, ya_t, yb_t, ga_t, gb_t):
        sa, sb = _sigmoid(ga_t.astype(F32)), _sigmoid(gb_t.astype(F32))
        return (dm * sa, dm * sb, dm * ya_t.astype(F32) * sa * (1.0 - sa), dm * yb_t.astype(F32) * sb * (1.0 - sb))

    gmaps = [lambda i, j, k: (i, j), lambda i, j, k: (i, j), lambda i, j, k: (i, j),
             lambda i, j, k: (i, j + D // td)]
    d_ya, d_yb, d_ga, d_gb = _matmul(dx2b, w_o, mode="nt", name="mm_d_merged", out_dtypes=[BF16] * 4,
                                     tm=tm, tn=td, tk=D, extras=[ya, yb, proj, proj], extra_maps=gmaps,
                                     epilogue=gate_bwd_epi)
    dw_o, = _matmul(merged, dx2b, mode="tn", name="mm_dw_out", out_dtypes=[BF16], tm=td, tn=td, tk=tks)
    d_oa, = _matmul(d_ya, w_bs_t, mode="nn", name="mm_d_oa", out_dtypes=[BF16], tm=tm, tn=SWA_Q_W, tk=D)
    d_ob, = _matmul(d_yb, w_bf_t, mode="nn", name="mm_d_ob", out_dtypes=[BF16], tm=tm, tn=FOX_W, tk=D)
    dw_bs_t, = _matmul(d_ya, o_a, mode="tn", name="mm_dw_bs", out_dtypes=[BF16], tm=td, tn=SWA_Q_W, tk=tks)
    dw_bf_t, = _matmul(d_yb, o_b, mode="tn", name="mm_dw_bf", out_dtypes=[BF16], tm=td, tn=FOX_W, tk=tks)
    h_s2, tok_s2 = _exchange_start([dw_bs_t.reshape(N_DEV, D // N_DEV, SWA_Q_W),
                                    dw_bf_t.reshape(N_DEV, D // N_DEV, FOX_W), dw_o.reshape(N_DEV, D // N_DEV, D)],
                                   gather=False, name="scatter_attn_start")
    def row_blocks_t(a):
        return a.reshape(S // bbq, bbq, FOX_W).transpose(0, 2, 1)

    d_fq, d_fk, d_fv, dcol4, drow4 = _fox_bwd(proj, _key_bias_blocks(negc, bbk), o_b, lse, d_ob,
                                              row_blocks_t(proj[:, fq_off:fq_off + FOX_W]), row_blocks_t(d_ob),
                                              q_off=fq_off, k_off=fk_off, v_off=fv_off, bq=bbq, bk=bbk,
                                              name="fox_bwd", deps=[tok_s2])
    dcol = dcol4.transpose(0, 2, 1, 3).reshape(FOX_HEADS, S)
    drow = drow4.transpose(0, 2, 1, 3).reshape(FOX_HEADS, S)
    dz_t, dbias_l = _fox_post(drow, dcol, z_t, bias_col, name="fox_post")
    d_aq, dk_c, dk_p, dv_c, dv_p, dsink_l = _swa_bwd(q_rope, k_rope, proj, sinks, d_oa, cos_t, sin_t, swa_mask,
                                                     v_off=v_off, name="swa_bwd")
    d_ak, d_av = _rope_bwd(dk_c, dk_p, dv_c, dv_p, cos_t, sin_t, name="rope_bwd")
    dz_pad = jnp.pad(dz_t.T.astype(BF16), ((0, 0), (0, FL_PAD - FOX_HEADS)))
    d_proj = jnp.concatenate([d_ga, d_gb, d_aq, d_ak, d_av, d_fq, d_fk, d_fv, dz_pad], axis=1)
    tkp = _pick(NP, 2304)
    dw_in_p, = _matmul(d_proj, h1, mode="tn", name="mm_dw_in", out_dtypes=[BF16], tm=_pick(NP, 512), tn=D, tk=tks)
    dw_in_t = jnp.concatenate([dw_in_p[q_off:q_off + QKV_W], dw_in_p[fl_off:fl_off + FOX_HEADS], dw_in_p[:q_off]],
                              axis=0)
    h_s3, tok_s3 = _exchange_start([dw_in_t.reshape(N_DEV, d_in // N_DEV, D)], gather=False,
                                   name="scatter_in_start")
    dh1, = _matmul(d_proj, w_in_p, mode="nn", name="mm_dh1", out_dtypes=[BF16], tm=tm, tn=td, tk=tkp, deps=[tok_s3])
    dx, dg1 = _rms_bwd(dh1, x2d, attn_norm, dx2b, name="rms1_bwd", out_dtype=F32)

    dbias = dbias_l[:, 0]
    dsinks = dsink_l[:, :, 0].reshape(-1)
    nsm = 3 * D + 2 * LANES
    tail = jnp.zeros((2 * LANES,), F32)
    small_g = jnp.concatenate([dg1[0], dg2[0], dg3[0],
                               tail.at[0:16].set(dbias).at[16:32].set(dsinks).at[32].set(loss_part[0, 0])])

    def pack(a_norm, b_norm, f_norm, bias, snk):
        return jnp.concatenate([a_norm[0], b_norm[0], f_norm,
                                tail.at[0:16].set(bias[0]).at[16:32].set(snk[0])]).reshape(1, nsm)

    small_stack, = _exchange([small_g.reshape(1, nsm)], gather=True, name="gather_small")
    u_sm = _adamw(pack(attn_norm, mlp_norm, final_norm, fox_f_bias, swa_sinks), small_stack,
                  pack(m_attn_norm, m_mlp_norm, m_final_norm, m_fox_f_bias, m_swa_sinks),
                  pack(v_attn_norm, v_mlp_norm, v_final_norm, v_fox_f_bias, v_swa_sinks),
                  name="adamw_small", stacked=True)
    loss = u_sm[0][0, 3 * D + 32]

    def own_of(src):
        return lax.dynamic_index_in_dim(src, me, 0, keepdims=False)

    def update_t(stack, src, w, m, v, nm):
        g = _sum8(stack, own_of(src), name="sum_" + nm).T
        return _adamw(w[0], g, m[0], v[0], name="adamw_" + nm, stacked=False)

    def update(stack, src, w, m, v, nm, transposed=False):
        return _adamw(w[0], stack, m[0], v[0], name="adamw_" + nm, stacked=True, own=own_of(src),
                      transposed=transposed)

    (s_up, s_dn), (r_up, r_dn) = _exchange_wait(h_s1, u_sm[1], name="scatter_mlp_wait")
    u_up = update(r_up, s_up, w_up, m_w_up, v_w_up, "w_up", transposed=True)
    u_dn = update(r_dn, s_dn, w_down, m_w_down, v_w_down, "w_down")
    (s_bs, s_bf, s_o), (r_bs, r_bf, r_o) = _exchange_wait(h_s2, u_dn[1], name="scatter_attn_wait")
    u_bs = update(r_bs, s_bs, w_branch_swa, m_w_branch_swa, v_w_branch_swa, "w_bs", transposed=True)
    u_bf = update(r_bf, s_bf, w_branch_fox, m_w_branch_fox, v_w_branch_fox, "w_bf", transposed=True)
    u_o = update(r_o, s_o, w_out, m_w_out, v_w_out, "w_out")
    (s_w_in,), (r_in,) = _exchange_wait(h_s3, u_o[1], name="scatter_in_wait")
    u_in = update_t(r_in, s_w_in, w_in, m_w_in, v_w_in, "w_in")

    def small(kind):
        a = u_sm[kind][0]
        return dict(attn_norm=a[0:D][None], mlp_norm=a[D:2 * D][None], final_norm=a[2 * D:3 * D],
                    fox_f_bias=a[3 * D:3 * D + 16][None], swa_sinks=a[3 * D + 16:3 * D + 32][None])

    big = dict(w_in=u_in, w_branch_swa=u_bs, w_branch_fox=u_bf, w_out=u_o, w_up=u_up, w_down=u_dn)
    order = ["attn_norm", "w_in", "fox_f_bias", "swa_sinks", "w_branch_swa", "w_branch_fox", "w_out", "mlp_norm",
             "w_up", "w_down", "final_norm"]
    outs = [loss, dx[None]]
    for kind in range(4):
        sm = small(kind)
        for nm in order:
            outs.append(big[nm][kind][None] if nm in big else sm[nm])
    return tuple(outs)
```

```python
import functools

import jax
import jax.numpy as jnp
from jax import lax
from jax.experimental import pallas as pl
from jax.experimental.pallas import tpu as pltpu

F32 = jnp.float32
BF16 = jnp.bfloat16

N_DEV = 8
HEAD_DIM = 64
SWA_Q_W = 1024
SWA_KV_W = 128
SWA_GROUP = 8
WINDOW = 128
FOX_W = 1024
FOX_HEADS = 16
QKV_W = SWA_Q_W + 2 * SWA_KV_W + 3 * FOX_W
FL_PAD = 256
ROPE_THETA = 10000.0
RMS_EPS = 1e-6
ATT_SCALE = 0.125
NEG = -1e30

ADAM_LR = 0.001
ADAM_B1 = 0.9
ADAM_B2 = 0.999
ADAM_EPS = 1e-08
ADAM_WD = 0.01
ADAM_STEP = 10

FOX_FWD_BLOCKS = (1024, 1024)
FOX_BWD_BLOCKS = (1024, 512)
FOX_FWD_PAIRS = 2

LANES = 128
VMEM_LIMIT = 56 * 1024 * 1024
STEP_BYTES = 12 * 1024 * 1024


def _cparams(*sem):
    return pltpu.CompilerParams(dimension_semantics=sem, vmem_limit_bytes=VMEM_LIMIT)


def _pick(dim, pref, align=LANES):
    best = None
    t = align
    while t <= min(dim, pref):
        if dim % t == 0:
            best = t
        t += align
    return best if best is not None else dim


_DIMS = {"nn": ((1,), (0,)), "nt": ((1,), (1,)), "tn": ((0,), (0,))}


_ANY = pl.BlockSpec(memory_space=pl.ANY)


def _matmul(a, b, *, mode, name, out_dtypes, tm, tn, tk, extras=(), extra_maps=None, extra_shapes=None,
            a_fn=None, epilogue=None, deps=()):
    if mode == "nn":
        (M, K), (K2, N) = a.shape, b.shape
    elif mode == "nt":
        (M, K), (N, K2) = a.shape, b.shape
    else:
        (K, M), (K2, N) = a.shape, b.shape
    assert K == K2, (name, a.shape, b.shape)
    assert M % tm == 0 and N % tn == 0 and K % tk == 0, (name, M, N, K, tm, tn, tk)
    nk = K // tk
    ne, no = len(extras), len(out_dtypes)
    dims = (_DIMS[mode], ((), ()))

    def body(*refs):
        a_ref, b_ref = refs[0], refs[1]
        ex_refs = refs[2:2 + ne]
        out_refs = refs[2 + ne + len(deps):2 + ne + len(deps) + no]

        def finish(acc):
            res = (acc,) if epilogue is None else epilogue(acc, *[e[...] for e in ex_refs])
            for o_ref, r in zip(out_refs, res):
                o_ref[...] = r.astype(o_ref.dtype)

        def product():
            av = a_ref[...]
            if a_fn is not None:
                av = a_fn(av)
            return lax.dot_general(av, b_ref[...], dims, preferred_element_type=F32)

        if nk == 1:
            finish(product())
        else:
            acc_ref = refs[-1]
            k = pl.program_id(2)

            @pl.when(k == 0)
            def _():
                acc_ref[...] = jnp.zeros_like(acc_ref)

            acc_ref[...] += product()

            @pl.when(k == nk - 1)
            def _():
                finish(acc_ref[...])

    if mode == "tn":
        a_spec = pl.BlockSpec((tk, tm), lambda i, j, k: (k, i))
    else:
        a_spec = pl.BlockSpec((tm, tk), lambda i, j, k: (i, k))
    if mode == "nt":
        b_spec = pl.BlockSpec((tn, tk), lambda i, j, k: (j, k))
    else:
        b_spec = pl.BlockSpec((tk, tn), lambda i, j, k: (k, j))
    if extra_maps is None:
        extra_maps = [lambda i, j, k: (i, j)] * ne
    if extra_shapes is None:
        extra_shapes = [None] * ne
    ex_specs = [pl.BlockSpec(s or (tm, tn), m) for s, m in zip(extra_shapes, extra_maps)]
    out_spec = [pl.BlockSpec((tm, tn), lambda i, j, k: (i, j)) for _ in range(no)]
    res = pl.pallas_call(
        body,
        name=name,
        grid=(M // tm, N // tn, nk),
        in_specs=[a_spec, b_spec] + ex_specs + [_ANY] * len(deps),
        out_specs=out_spec,
        out_shape=[jax.ShapeDtypeStruct((M, N), d) for d in out_dtypes],
        scratch_shapes=[pltpu.VMEM((tm, tn), F32)] if nk > 1 else [],
        compiler_params=_cparams("parallel", "parallel", "arbitrary"),
    )(a, b, *extras, *deps)
    return res


def _square_bf16(t):
    tf = t.astype(F32)
    return (tf * tf).astype(BF16)


def _sigmoid(g):
    return 1.0 / (1.0 + jnp.exp(-g))


def _rms_fwd(x, gain, *, name, deps=()):
    S, D = x.shape
    tr = _pick(S, 512, 8)

    def body(x_ref, g_ref, *rest):
        h_ref = rest[-1]
        xv = x_ref[...]
        r = lax.rsqrt(jnp.mean(xv * xv, axis=-1, keepdims=True) + RMS_EPS)
        h_ref[...] = (xv * r * g_ref[...]).astype(BF16)

    return pl.pallas_call(
        body, name=name, grid=(S // tr,),
        in_specs=[pl.BlockSpec((tr, D), lambda i: (i, 0)), pl.BlockSpec((1, D), lambda i: (0, 0))] + [_ANY] * len(deps),
        out_specs=pl.BlockSpec((tr, D), lambda i: (i, 0)),
        out_shape=jax.ShapeDtypeStruct((S, D), BF16),
        compiler_params=_cparams("parallel"),
    )(x, gain, *deps)


def _rms_bwd(dh, x, gain, dres, *, name, out_dtype, deps=()):
    S, D = x.shape
    tr = _pick(S, 512, 8)

    def body(dh_ref, x_ref, g_ref, dres_ref, *rest):
        outs = rest[len(deps):]
        dx_ref, dg_ref = outs[0], outs[-1]
        xv = x_ref[...]
        r = lax.rsqrt(jnp.mean(xv * xv, axis=-1, keepdims=True) + RMS_EPS)
        xh = xv * r
        dhv = dh_ref[...].astype(F32)
        t = dhv * g_ref[...]
        dx = r * (t - xh * jnp.mean(t * xh, axis=-1, keepdims=True)) + dres_ref[...].astype(F32)
        dx_ref[...] = dx.astype(out_dtype)
        part = jnp.sum(dhv * xh, axis=0, keepdims=True)

        @pl.when(pl.program_id(0) == 0)
        def _():
            dg_ref[...] = part

        @pl.when(pl.program_id(0) > 0)
        def _():
            dg_ref[...] += part

    row = pl.BlockSpec((tr, D), lambda i: (i, 0))
    vec = pl.BlockSpec((1, D), lambda i: (0, 0))
    return pl.pallas_call(
        body, name=name, grid=(S // tr,),
        in_specs=[row, row, vec, row] + [_ANY] * len(deps), out_specs=[row, vec],
        out_shape=[jax.ShapeDtypeStruct((S, D), out_dtype), jax.ShapeDtypeStruct((1, D), F32)],
        compiler_params=_cparams("arbitrary"),
    )(dh, x, gain, dres, *deps)


def _loss_head(x3, target, gain, *, name):
    S, D = x3.shape
    tr = _pick(S, 512, 8)

    def body(x_ref, t_ref, g_ref, dxb_ref, dg_ref, loss_ref):
        xv = x_ref[...]
        r = lax.rsqrt(jnp.mean(xv * xv, axis=-1, keepdims=True) + RMS_EPS)
        xh = xv * r
        gv = g_ref[...]
        err = xh * gv - t_ref[...]
        lpart = jnp.zeros((1, LANES), F32) + (0.5 / D) * jnp.sum(err * err)
        dy = err * (1.0 / D)
        t = dy * gv
        dx = r * (t - xh * jnp.mean(t * xh, axis=-1, keepdims=True))
        dxb_ref[...] = dx.astype(BF16)
        part = jnp.sum(dy * xh, axis=0, keepdims=True)

        @pl.when(pl.program_id(0) == 0)
        def _():
            dg_ref[...] = part
            loss_ref[...] = lpart

        @pl.when(pl.program_id(0) > 0)
        def _():
            dg_ref[...] += part
            loss_ref[...] += lpart

    row = pl.BlockSpec((tr, D), lambda i: (i, 0))
    vec = pl.BlockSpec((1, D), lambda i: (0, 0))
    return pl.pallas_call(
        body, name=name, grid=(S // tr,),
        in_specs=[row, row, vec],
        out_specs=[row, vec, pl.BlockSpec((1, LANES), lambda i: (0, 0))],
        out_shape=[jax.ShapeDtypeStruct((S, D), BF16),
                   jax.ShapeDtypeStruct((1, D), F32), jax.ShapeDtypeStruct((1, LANES), F32)],
        compiler_params=_cparams("arbitrary"),
    )(x3, target, gain)


def _rope_tables(pos_col, invf, *, name):
    S = pos_col.shape[0]
    tr = _pick(S, 512, 8)

    def body(p_ref, f_ref, cos_ref, sin_ref):
        ang = p_ref[...].astype(F32) * f_ref[...]
        lane = lax.broadcasted_iota(jnp.int32, (1, LANES), 1)
        first = (lane % HEAD_DIM) < HEAD_DIM // 2
        sn = jnp.sin(ang)
        cos_ref[...] = jnp.cos(ang)
        sin_ref[...] = jnp.where(first, -sn, sn)

    return pl.pallas_call(
        body, name=name, grid=(S // tr,),
        in_specs=[pl.BlockSpec((tr, 1), lambda i: (i, 0)), pl.BlockSpec((1, LANES), lambda i: (0, 0))],
        out_specs=[pl.BlockSpec((tr, LANES), lambda i: (i, 0))] * 2,
        out_shape=[jax.ShapeDtypeStruct((S, LANES), F32)] * 2,
        compiler_params=_cparams("parallel"),
    )(pos_col, invf)


def _swap_halves(t):
    lane = lax.broadcasted_iota(jnp.int32, (1, LANES), 1)
    first = (lane % HEAD_DIM) < HEAD_DIM // 2
    return jnp.where(first, pltpu.roll(t, LANES - HEAD_DIM // 2, 1), pltpu.roll(t, HEAD_DIM // 2, 1))


def _rope_fwd(proj, cos_t, sin_t, *, q_off, k_off, name):
    S = proj.shape[0]
    tr = _pick(S, 512, 8)
    nqb = SWA_Q_W // LANES

    def body(q_ref, k_ref, c_ref, s_ref, qo_ref, ko_ref):
        cv, sv = c_ref[...], s_ref[...]
        for b in range(nqb):
            t = q_ref[:, b * LANES:(b + 1) * LANES].astype(F32)
            qo_ref[:, b * LANES:(b + 1) * LANES] = (t * cv + _swap_halves(t) * sv).astype(BF16)
        t = k_ref[...].astype(F32)
        ko_ref[...] = (t * cv + _swap_halves(t) * sv).astype(BF16)

    tab = pl.BlockSpec((tr, LANES), lambda i: (i, 0))
    return pl.pallas_call(
        body, name=name, grid=(S // tr,),
        in_specs=[pl.BlockSpec((tr, SWA_Q_W), lambda i: (i, q_off // SWA_Q_W)),
                  pl.BlockSpec((tr, LANES), lambda i: (i, k_off // LANES)), tab, tab],
        out_specs=[pl.BlockSpec((tr, SWA_Q_W), lambda i: (i, 0)), tab],
        out_shape=[jax.ShapeDtypeStruct((S, SWA_Q_W), BF16), jax.ShapeDtypeStruct((S, LANES), BF16)],
        compiler_params=_cparams("parallel"),
    )(proj, proj, cos_t, sin_t)


def _rope_bwd(dk_cur, dk_prev, dv_cur, dv_prev, cos_t, sin_t, *, name):
    S = dk_cur.shape[1]
    tr = _pick(S, 512)
    nb = S // tr

    def body(kc_ref, kp_ref, vc_ref, vp_ref, c_ref, s_ref, dko_ref, dvo_ref):
        cv, sv = c_ref[...], s_ref[...]
        row = pl.program_id(0) * tr + lax.broadcasted_iota(jnp.int32, (tr, 1), 0)
        has_next = row < S - WINDOW
        d = kc_ref[0] + kc_ref[1] + jnp.where(has_next, kp_ref[0] + kp_ref[1], 0.0)
        dko_ref[...] = (d * cv + _swap_halves(d * sv)).astype(BF16)
        dvo_ref[...] = (vc_ref[0] + vc_ref[1] + jnp.where(has_next, vp_ref[0] + vp_ref[1], 0.0)).astype(BF16)

    tab = pl.BlockSpec((tr, LANES), lambda i: (i, 0))
    cur = pl.BlockSpec((2, tr, LANES), lambda i: (0, i, 0))
    return pl.pallas_call(
        body, name=name, grid=(nb,),
        in_specs=[cur, cur, cur, cur, tab, tab],
        out_specs=[tab, tab],
        out_shape=[jax.ShapeDtypeStruct((S, LANES), BF16), jax.ShapeDtypeStruct((S, LANES), BF16)],
        compiler_params=_cparams("parallel"),
    )(dk_cur, dk_prev, dv_cur, dv_prev, cos_t, sin_t)


def _dot_nt(a, b):
    return lax.dot_general(a, b, (((1,), (1,)), ((), ())), preferred_element_type=F32)


def _dot_tn(a, b):
    return lax.dot_general(a, b, (((0,), (0,)), ((), ())), preferred_element_type=F32)


def _dot_nn(a, b):
    return lax.dot_general(a, b, (((1,), (0,)), ((), ())), preferred_element_type=F32)


def _roll_half(t):
    return pltpu.roll(t.astype(F32), HEAD_DIM, 1).astype(t.dtype)


SWA_STACK = SWA_GROUP // 2


def _swa_mask_bias():
    rows = SWA_STACK * WINDOW
    row = lax.broadcasted_iota(jnp.int32, (rows, 2 * WINDOW), 0) % WINDOW
    col = lax.broadcasted_iota(jnp.int32, (rows, 2 * WINDOW), 1)
    diff = row + WINDOW - col
    window = (diff >= 0) & (diff < WINDOW)
    return jnp.stack([jnp.where(window & (col >= WINDOW), 0.0, NEG), jnp.where(window, 0.0, NEG)]).astype(F32)


def _swa_common(kp_ref, kc_ref, vp_ref, vc_ref):
    k2 = jnp.concatenate([kp_ref[...], kc_ref[...]], axis=0)
    v2 = jnp.concatenate([vp_ref[...], vc_ref[...]], axis=0)
    k_sw, v_sw = _roll_half(k2), _roll_half(v2)
    lane = lax.broadcasted_iota(jnp.int32, (1, LANES), 1)
    half = [lane < HEAD_DIM, lane >= HEAD_DIM]
    kk = [[k2 if hk == a else k_sw for a in range(2)] for hk in range(2)]
    vv = [[v2 if hk == a else v_sw for a in range(2)] for hk in range(2)]
    return half, kk, vv


def _swa_stack(ref, hk, mask, scale=None):
    parts = []
    for t in range(SWA_STACK):
        blk = ref[:, (hk * SWA_STACK + t) * LANES:(hk * SWA_STACK + t + 1) * LANES]
        if scale is not None:
            blk = blk * jnp.asarray(scale, blk.dtype)
        parts.append(jnp.where(mask, blk, jnp.zeros_like(blk)))
    return jnp.concatenate(parts, axis=0)


def _swa_sink_column(sink_ref, hk, a):
    blk = lax.broadcasted_iota(jnp.int32, (SWA_STACK * WINDOW, 1), 0) // WINDOW
    col = jnp.zeros((SWA_STACK * WINDOW, 1), F32)
    for t in range(SWA_STACK):
        col = jnp.where(blk == t, sink_ref[hk * SWA_GROUP + 2 * t + a], col)
    return col


def _swa_probs(qm, kk, mask_bias, sink):
    s = _dot_nt(qm, kk) + mask_bias
    m = jnp.maximum(jnp.max(s, axis=1, keepdims=True), sink)
    e = jnp.exp(s - m)
    es = jnp.exp(sink - m)
    inv = 1.0 / (jnp.sum(e, axis=1, keepdims=True) + es)
    return e * inv, es * inv


def _swa_mask_spec():
    return pl.BlockSpec((1, SWA_STACK * WINDOW, 2 * WINDOW), lambda n: (jnp.minimum(n, 1), 0, 0))


def _swa_fwd(q_rope, k_rope, proj, sinks, mask_bias, *, v_off, name):
    S = q_rope.shape[0]
    nb = S // WINDOW

    def body(sink_ref, q_ref, kp_ref, kc_ref, vp_ref, vc_ref, mask_ref, o_ref):
        half, kk, vv = _swa_common(kp_ref, kc_ref, vp_ref, vc_ref)
        for hk in range(2):
            outs = []
            for a in range(2):
                qm = _swa_stack(q_ref, hk, half[a], ATT_SCALE)
                p, _ = _swa_probs(qm, kk[hk][a], mask_ref[0], _swa_sink_column(sink_ref, hk, a))
                outs.append(_dot_nn(p.astype(BF16), vv[hk][a]))
            for t in range(SWA_STACK):
                rows = slice(t * WINDOW, (t + 1) * WINDOW)
                c0 = (hk * SWA_STACK + t) * LANES
                o_ref[:, c0:c0 + LANES] = jnp.where(half[0], outs[0][rows], outs[1][rows]).astype(BF16)

    prev = lambda n: (jnp.maximum(n - 1, 0), 0)
    cur = lambda n: (n, 0)
    vprev = lambda n: (jnp.maximum(n - 1, 0), v_off // LANES)
    vcur = lambda n: (n, v_off // LANES)
    blk = lambda m: pl.BlockSpec((WINDOW, LANES), m)
    return pl.pallas_call(
        body, name=name, grid=(nb,),
        in_specs=[pl.BlockSpec(memory_space=pltpu.SMEM),
                  pl.BlockSpec((WINDOW, SWA_Q_W), lambda n: (n, 0)),
                  blk(prev), blk(cur), blk(vprev), blk(vcur), _swa_mask_spec()],
        out_specs=pl.BlockSpec((WINDOW, SWA_Q_W), lambda n: (n, 0)),
        out_shape=jax.ShapeDtypeStruct((S, SWA_Q_W), BF16),
        compiler_params=_cparams("parallel"),
    )(sinks, q_rope, k_rope, k_rope, proj, proj, mask_bias)


def _swa_bwd(q_rope, k_rope, proj, sinks, d_o, cos_t, sin_t, mask_bias, *, v_off, name):
    S = q_rope.shape[0]
    nb = S // WINDOW

    def body(sink_ref, q_ref, kp_ref, kc_ref, vp_ref, vc_ref, do_ref, c_ref, s_ref, mask_ref,
             dq_ref, dkc_ref, dkp_ref, dvc_ref, dvp_ref, dsink_ref):
        n = pl.program_id(0)
        half, kk, vv = _swa_common(kp_ref, kc_ref, vp_ref, vc_ref)
        allowed = mask_ref[0]
        cv, sv = c_ref[...], s_ref[...]
        srow = lax.broadcasted_iota(jnp.int32, (SWA_GROUP, LANES), 0)
        for hk in range(2):
            dk_acc = jnp.zeros((2 * WINDOW, LANES), F32)
            dv_acc = jnp.zeros((2 * WINDOW, LANES), F32)
            dsink = jnp.zeros((SWA_GROUP, LANES), F32)
            dqs = []
            for a in range(2):
                qm = _swa_stack(q_ref, hk, half[a], ATT_SCALE)
                dom = _swa_stack(do_ref, hk, half[a])
                p, psink = _swa_probs(qm, kk[hk][a], allowed, _swa_sink_column(sink_ref, hk, a))
                dp = _dot_nt(dom, vv[hk][a])
                delta = jnp.sum(p * dp, axis=1, keepdims=True)
                ds = (p * (dp - delta)).astype(BF16)
                dsk = psink * delta
                for t in range(SWA_STACK):
                    dsink = dsink + jnp.where(srow == 2 * t + a, -jnp.sum(dsk[t * WINDOW:(t + 1) * WINDOW]), 0.0)
                dqs.append(_dot_nn(ds, kk[hk][a]) * ATT_SCALE)
                dk_acc = dk_acc + _dot_tn(ds, qm)
                dv_acc = dv_acc + _dot_tn(p.astype(BF16), dom)
            for t in range(SWA_STACK):
                rows = slice(t * WINDOW, (t + 1) * WINDOW)
                d = jnp.where(half[0], dqs[0][rows], dqs[1][rows])
                c0 = (hk * SWA_STACK + t) * LANES
                dq_ref[:, c0:c0 + LANES] = (d * cv + _swap_halves(d * sv)).astype(BF16)
            dk_t = jnp.where(half[hk], dk_acc + pltpu.roll(dk_acc, HEAD_DIM, 1), 0.0)
            dv_t = jnp.where(half[hk], dv_acc + pltpu.roll(dv_acc, HEAD_DIM, 1), 0.0)
            dkp_ref[hk] = dk_t[:WINDOW]
            dkc_ref[hk] = dk_t[WINDOW:]
            dvp_ref[hk] = dv_t[:WINDOW]
            dvc_ref[hk] = dv_t[WINDOW:]

            @pl.when(n == 0)
            def _():
                dsink_ref[hk] = dsink

            @pl.when(n > 0)
            def _():
                dsink_ref[hk] += dsink

    prev = lambda n: (jnp.maximum(n - 1, 0), 0)
    cur = lambda n: (n, 0)
    vprev = lambda n: (jnp.maximum(n - 1, 0), v_off // LANES)
    vcur = lambda n: (n, v_off // LANES)
    blk = lambda m: pl.BlockSpec((WINDOW, LANES), m)
    qblk = pl.BlockSpec((WINDOW, SWA_Q_W), lambda n: (n, 0))
    part = pl.BlockSpec((2, WINDOW, LANES), lambda n: (0, n, 0))
    part_prev = pl.BlockSpec((2, WINDOW, LANES), lambda n: (0, jnp.maximum(n - 1, 0), 0))
    part_shape = jax.ShapeDtypeStruct((2, S, LANES), F32)
    return pl.pallas_call(
        body, name=name, grid=(nb,),
        in_specs=[pl.BlockSpec(memory_space=pltpu.SMEM), qblk, blk(prev), blk(cur), blk(vprev), blk(vcur), qblk,
                  blk(cur), blk(cur), _swa_mask_spec()],
        out_specs=[qblk, part, part_prev, part, part_prev,
                   pl.BlockSpec((2, SWA_GROUP, LANES), lambda n: (0, 0, 0))],
        out_shape=[jax.ShapeDtypeStruct((S, SWA_Q_W), BF16), part_shape, part_shape, part_shape, part_shape,
                   jax.ShapeDtypeStruct((2, SWA_GROUP, LANES), F32)],
        compiler_params=_cparams("arbitrary"),
    )(sinks, q_rope, k_rope, k_rope, proj, proj, d_o, cos_t, sin_t, mask_bias)


def _fox_prep(z_t, bias_col, *, name):
    H, S = z_t.shape
    tb = _pick(S, 512)

    def body(z_ref, b_ref, o_ref, carry_ref):
        @pl.when(pl.program_id(0) == 0)
        def _():
            carry_ref[...] = jnp.zeros_like(carry_ref)

        zz = z_ref[...] + b_ref[...]
        t = jnp.exp(-jnp.abs(zz))
        log1p = jnp.where(t < 1e-2, t * (1.0 - t * (0.5 - t * (1.0 / 3.0))), jnp.log(1.0 + t))
        logf = jnp.minimum(zz, 0.0) - log1p
        r = lax.broadcasted_iota(jnp.int32, (tb, tb), 0)
        c = lax.broadcasted_iota(jnp.int32, (tb, tb), 1)
        tri = (r <= c).astype(BF16)
        hi = logf.astype(BF16)
        r1 = logf - hi.astype(F32)
        mid = r1.astype(BF16)
        lo = (r1 - mid.astype(F32)).astype(BF16)
        cs = _dot_nn(hi, tri) + _dot_nn(mid, tri) + _dot_nn(lo, tri) + carry_ref[:, 0:1]
        o_ref[...] = -cs
        carry_ref[...] = jnp.zeros_like(carry_ref) + cs[:, tb - 1:tb]

    return pl.pallas_call(
        body, name=name, grid=(S // tb,),
        in_specs=[pl.BlockSpec((H, tb), lambda i: (0, i)), pl.BlockSpec((H, 1), lambda i: (0, 0))],
        out_specs=pl.BlockSpec((H, tb), lambda i: (0, i)),
        out_shape=jax.ShapeDtypeStruct((H, S), F32),
        scratch_shapes=[pltpu.VMEM((H, LANES), F32)],
        compiler_params=_cparams("arbitrary"),
    )(z_t, bias_col)


def _fox_post(drow, dcol, z_t, bias_col, *, name):
    H, S = z_t.shape
    tb = _pick(S, 512)
    nb = S // tb

    def body(dr_ref, d_ref, z_ref, b_ref, dz_ref, db_ref, carry_ref):
        @pl.when(pl.program_id(0) == 0)
        def _():
            carry_ref[...] = jnp.zeros_like(carry_ref)
            db_ref[...] = jnp.zeros_like(db_ref)

        dc = dr_ref[...] - d_ref[...]
        r = lax.broadcasted_iota(jnp.int32, (tb, tb), 0)
        c = lax.broadcasted_iota(jnp.int32, (tb, tb), 1)
        tri = (r >= c).astype(BF16)
        hi = dc.astype(BF16)
        r1 = dc - hi.astype(F32)
        mid = r1.astype(BF16)
        lo = (r1 - mid.astype(F32)).astype(BF16)
        dlogf = _dot_nn(hi, tri) + _dot_nn(mid, tri) + _dot_nn(lo, tri) + carry_ref[:, 0:1]
        carry_ref[...] = jnp.zeros_like(carry_ref) + dlogf[:, 0:1]
        dz = dlogf * _sigmoid(-(z_ref[...] + b_ref[...]))
        dz_ref[...] = dz
        db_ref[...] += jnp.sum(dz, axis=1, keepdims=True)

    rev = lambda i: (0, nb - 1 - i)
    return pl.pallas_call(
        body, name=name, grid=(nb,),
        in_specs=[pl.BlockSpec((H, tb), rev), pl.BlockSpec((H, tb), rev), pl.BlockSpec((H, tb), rev),
                  pl.BlockSpec((H, 1), lambda i: (0, 0))],
        out_specs=[pl.BlockSpec((H, tb), rev), pl.BlockSpec((H, LANES), lambda i: (0, 0))],
        out_shape=[jax.ShapeDtypeStruct((H, S), F32), jax.ShapeDtypeStruct((H, LANES), F32)],
        scratch_shapes=[pltpu.VMEM((H, LANES), F32)],
        compiler_params=_cparams("arbitrary"),
    )(drow, dcol, z_t, bias_col)


def _fox_blocks(S):
    cap = max(LANES, S // 4)
    return (min(FOX_FWD_BLOCKS[0], cap), min(FOX_FWD_BLOCKS[1], cap)), \
           (min(FOX_BWD_BLOCKS[0], cap), min(FOX_BWD_BLOCKS[1], cap))


def _key_bias_blocks(negc, bk):
    H, S = negc.shape
    return negc.reshape(H // 2, 2, S // bk, bk).transpose(0, 2, 1, 3)


def _fox_fwd(proj, negc4, *, q_off, k_off, v_off, bq, bk, name):
    S = proj.shape[0]
    nq, nk = S // bq, S // bk
    npair = FOX_HEADS // 2
    assert bq % bk == 0 or bk % bq == 0
    nmask = max(1, bq // bk)

    gp = FOX_FWD_PAIRS
    gw = gp * LANES
    assert q_off % gw == 0 and k_off % gw == 0 and v_off % gw == 0 and npair % gp == 0

    def body(q_ref, k_ref, v_ref, nc_ref, o_ref, lse_ref):
        i = pl.program_id(1)
        lane = lax.broadcasted_iota(jnp.int32, (1, LANES), 1)
        half = [lane < HEAD_DIM, lane >= HEAD_DIM]
        qh = []
        for g in range(gp):
            q2 = q_ref[:, g * LANES:(g + 1) * LANES] * jnp.asarray(ATT_SCALE, BF16)
            qh += [jnp.where(half[h], q2, jnp.zeros_like(q2)) for h in range(2)]
        row = lax.broadcasted_iota(jnp.int32, (bq, bk), 0)
        col = lax.broadcasted_iota(jnp.int32, (bq, bk), 1)
        rel = row - col
        nfull = (i * bq) // bk

        spare = [HEAD_DIM, 0]
        ones_lane = [lane == spare[h] for h in range(2)]

        def step(j, carry, masked):
            start = pl.multiple_of(j * bk, bk)
            new = []
            for g in range(gp):
                ks = k_ref[pl.ds(start, bk), g * LANES:(g + 1) * LANES]
                vs = v_ref[pl.ds(start, bk), g * LANES:(g + 1) * LANES]
                nb = nc_ref[g, j]
                for h in range(2):
                    m, acc = carry[4 * g + 2 * h:4 * g + 2 * h + 2]
                    vh = jnp.where(half[h], vs, jnp.where(ones_lane[h], jnp.ones_like(vs), jnp.zeros_like(vs)))
                    qs, bias = qh[2 * g + h], nb[h:h + 1, :]

                    def update(m, acc, rows, keys):
                        s = _dot_nt(qs[rows], ks[keys]) + bias[:, keys]
                        if masked:
                            s = jnp.where(rel[rows, keys] >= j * bk - i * bq, s, NEG)
                        m_new = jnp.maximum(m[rows], jnp.max(s, axis=1, keepdims=True))
                        p = jnp.exp(s - m_new).astype(BF16)
                        return m_new, jnp.exp(m[rows] - m_new) * acc[rows] + _dot_nn(p, vh[keys])

                    if masked and bq == bk:
                        top, bot, everything = slice(0, bq // 2), slice(bq // 2, bq), slice(0, bk)
                        m_t, acc_t = update(m, acc, top, top)
                        m_b, acc_b = update(m, acc, bot, everything)
                        new += [jnp.concatenate([m_t, m_b], axis=0), jnp.concatenate([acc_t, acc_b], axis=0)]
                    else:
                        new += list(update(m, acc, slice(0, bq), slice(0, bk)))
            return tuple(new)

        init = (jnp.full((bq, 1), NEG, F32), jnp.zeros((bq, LANES), F32)) * (2 * gp)
        carry = lax.fori_loop(0, nfull, lambda j, c: step(j, c, False), init)
        for t in range(nmask):
            carry = step(nfull + t, carry, True)
        for g in range(gp):
            outs, lses = [], []
            for h in range(2):
                m, acc = carry[4 * g + 2 * h:4 * g + 2 * h + 2]
                l = acc[:, spare[h]:spare[h] + 1]
                outs.append(acc * (1.0 / l))
                lses.append(m + jnp.log(l))
            o_ref[:, g * LANES:(g + 1) * LANES] = jnp.where(half[0], outs[0], outs[1]).astype(BF16)
            lse_ref[g] = jnp.where(half[0], lses[0], lses[1])

    seq = lambda off: pl.BlockSpec((S, gw), lambda hp, i: (0, off // gw + hp))
    return pl.pallas_call(
        body, name=name, grid=(npair // gp, nq),
        in_specs=[pl.BlockSpec((bq, gw), lambda hp, i: (i, q_off // gw + hp)), seq(k_off), seq(v_off),
                  pl.BlockSpec((gp, nk, 2, bk), lambda hp, i: (hp, 0, 0, 0))],
        out_specs=[pl.BlockSpec((bq, gw), lambda hp, i: (i, hp)),
                   pl.BlockSpec((gp, bq, LANES), lambda hp, i: (hp, i, 0))],
        out_shape=[jax.ShapeDtypeStruct((S, FOX_W), BF16), jax.ShapeDtypeStruct((npair, S, LANES), F32)],
        compiler_params=_cparams("parallel", "parallel"),
    )(proj, proj, proj, negc4)


def _fox_bwd(proj, negc4, o, lse, d_o, q_t, do_t, *, q_off, k_off, v_off, bq, bk, name, deps=()):
    S = proj.shape[0]
    nq, nk = S // bq, S // bk
    npair = FOX_HEADS // 2
    assert bq % bk == 0 or bk % bq == 0
    nmask = max(1, bk // bq)

    def body(q_ref, k_ref, v_ref, nc_ref, o_ref, lse_ref, do_ref, qt_ref, dot_ref, *rest):
        dqo_ref, dk_ref, dv_ref, dn_ref, dr_ref, delta_ref, rs_ref, dq_ref = rest[len(deps):]
        j = pl.program_id(1)
        lane = lax.broadcasted_iota(jnp.int32, (1, LANES), 1)
        half = [lane < HEAD_DIM, lane >= HEAD_DIM]
        spare = [HEAD_DIM, 0]
        ones_lane = [lane == spare[h] for h in range(2)]
        srow = lax.broadcasted_iota(jnp.int32, (LANES, 1), 0)
        rhalf = [srow < HEAD_DIM, srow >= HEAD_DIM]
        ones_row = [srow == spare[h] for h in range(2)]
        k2, v2 = k_ref[...], v_ref[...]
        one_k = jnp.ones_like(k2)
        kh = [jnp.where(half[h], k2, jnp.where(ones_lane[h], one_k, jnp.zeros_like(k2))) for h in range(2)]
        nb = nc_ref[0, 0]
        row = lax.broadcasted_iota(jnp.int32, (bq, bk), 0)
        col = lax.broadcasted_iota(jnp.int32, (bq, bk), 1)
        rel = row - col
        i_first = (j * bk) // bq

        @pl.when(j == 0)
        def _():
            dq_ref[...] = jnp.zeros_like(dq_ref)
            rs_ref[...] = jnp.zeros_like(rs_ref)
            for b in range(nq):
                prod = do_ref[b * bq:(b + 1) * bq, :].astype(F32) * o_ref[b * bq:(b + 1) * bq, :].astype(F32)
                d0 = jnp.sum(jnp.where(half[0], prod, 0.0), axis=1, keepdims=True)
                d1 = jnp.sum(jnp.where(half[1], prod, 0.0), axis=1, keepdims=True)
                delta_ref[b * bq:(b + 1) * bq, :] = jnp.where(half[0], d0, d1)

        def step(i, carry, masked, r0=0):
            dkt_a, dkt_b, dvt = carry
            dkts = [dkt_a, dkt_b]
            nr = bq - r0
            start = pl.multiple_of(i * bq + r0, LANES)
            q2 = q_ref[pl.ds(start, nr), :] * jnp.asarray(ATT_SCALE, BF16)
            do2 = do_ref[pl.ds(start, nr), :]
            qt = qt_ref[i][:, r0:] * jnp.asarray(ATT_SCALE, BF16)
            dot = dot_ref[i][:, r0:]
            lse2 = lse_ref[0, pl.ds(start, nr), :]
            del2 = delta_ref[pl.ds(start, nr), :]
            dqf = []
            for h in range(2):
                qm = jnp.where(half[h], q2, jnp.zeros_like(q2))
                dom = jnp.where(half[h], do2, jnp.zeros_like(do2))
                qtm = jnp.where(rhalf[h], qt, jnp.where(ones_row[h], jnp.ones_like(qt), jnp.zeros_like(qt)))
                dotm = jnp.where(rhalf[h], dot, jnp.zeros_like(dot))
                c0 = h * HEAD_DIM
                p = jnp.exp(_dot_nt(qm, k2) + nb[h:h + 1, :] - lse2[:, c0:c0 + 1])
                if masked:
                    p = jnp.where(rel[r0:] >= j * bk - i * bq, p, 0.0)
                dp = _dot_nt(dom, v2)
                dsb = (p * (dp - del2[:, c0:c0 + 1])).astype(BF16)
                dvt = dvt + _dot_nn(dotm, p.astype(BF16))
                dkts[h] = dkts[h] + _dot_nn(qtm, dsb)
                dqf.append(_dot_nn(dsb, kh[h]))
            dq_ref[pl.ds(start, nr), :] += jnp.where(half[0], dqf[0], dqf[1]) * ATT_SCALE
            rs_ref[pl.ds(start, nr), :] += jnp.where(ones_lane[0], dqf[0], jnp.where(ones_lane[1], dqf[1], 0.0))
            return dkts[0], dkts[1], dvt

        zero = jnp.zeros((LANES, bk), F32)
        carry = (zero, zero, zero)
        if bq > bk:
            sp = j % (bq // bk)
            carry = lax.switch(sp, [functools.partial(step, i_first, masked=True, r0=s * bk)
                                    for s in range(bq // bk)], carry)
        else:
            for t in range(nmask):
                carry = step(i_first + t, carry, True)
        dkt_a, dkt_b, dvt = lax.fori_loop(i_first + nmask, nq, lambda i, c: step(i, c, False), carry)
        dk_ref[...] = jnp.where(rhalf[0], dkt_a, dkt_b).T.astype(BF16)
        dv_ref[...] = dvt.T.astype(BF16)
        dn_ref[0, 0] = jnp.concatenate([dkt_a[spare[0]:spare[0] + 1], dkt_b[spare[1]:spare[1] + 1]], axis=0)

        @pl.when(j == nk - 1)
        def _():
            dqo_ref[...] = dq_ref[...].astype(BF16)
            for b in range(nq):
                t = rs_ref[b * bq:(b + 1) * bq, :].T
                dr_ref[0, b] = jnp.concatenate([t[spare[0]:spare[0] + 1], t[spare[1]:spare[1] + 1]], axis=0)

    once = pl.Buffered(1)
    seq = lambda off: pl.BlockSpec((S, LANES), lambda hp, j: (0, off // LANES + hp), pipeline_mode=once)
    blk = lambda off: pl.BlockSpec((bk, LANES), lambda hp, j: (j, off // LANES + hp))
    nc = pl.BlockSpec((1, 1, 2, bk), lambda hp, j: (hp, j, 0, 0))
    tsp = pl.BlockSpec((nq, LANES, bq), lambda hp, j: (0, hp, 0), pipeline_mode=once)
    return pl.pallas_call(
        body, name=name, grid=(npair, nk),
        in_specs=[seq(q_off), blk(k_off), blk(v_off), nc, seq(0),
                  pl.BlockSpec((1, S, LANES), lambda hp, j: (hp, 0, 0), pipeline_mode=once), seq(0),
                  tsp, tsp] + [_ANY] * len(deps),
        out_specs=[pl.BlockSpec((S, LANES), lambda hp, j: (0, hp)), blk(0), blk(0), nc,
                   pl.BlockSpec((1, nq, 2, bq), lambda hp, j: (hp, 0, 0, 0))],
        out_shape=[jax.ShapeDtypeStruct((S, FOX_W), BF16), jax.ShapeDtypeStruct((S, FOX_W), BF16),
                   jax.ShapeDtypeStruct((S, FOX_W), BF16), jax.ShapeDtypeStruct((npair, nk, 2, bk), F32),
                   jax.ShapeDtypeStruct((npair, nq, 2, bq), F32)],
        scratch_shapes=[pltpu.VMEM((S, LANES), F32), pltpu.VMEM((S, LANES), F32), pltpu.VMEM((S, LANES), F32)],
        compiler_params=_cparams("parallel", "arbitrary"),
    )(proj, proj, proj, negc4, o, lse, d_o, q_t, do_t, *deps)


def _exchange(arrs, *, gather, name):
    n = len(arrs)
    npeer = N_DEV - 1

    def body(*refs):
        ins, outs = refs[:n], refs[n:2 * n]
        send_sems, recv_sems, loc_sems = refs[2 * n:]
        x, y, c = lax.axis_index("x"), lax.axis_index("y"), lax.axis_index("c")
        me = 4 * x + 2 * y + c
        peers = []
        for k in range(1, N_DEV):
            px = 1 - x if k & 4 else x
            py = 1 - y if k & 2 else y
            pc = 1 - c if k & 1 else c
            peers.append(((px, py, pc), 4 * px + 2 * py + pc))

        def remote(w, k):
            dev, idx = peers[k]
            src = ins[w] if gather else ins[w].at[idx]
            return pltpu.make_async_remote_copy(
                src_ref=src, dst_ref=outs[w].at[me],
                send_sem=send_sems.at[w * npeer + k], recv_sem=recv_sems.at[w * npeer + k],
                device_id=dev, device_id_type=pl.DeviceIdType.MESH)

        def arrival(w, k):
            dev, idx = peers[k]
            src = ins[w] if gather else ins[w].at[idx]
            return pltpu.make_async_remote_copy(
                src_ref=src, dst_ref=outs[w].at[idx],
                send_sem=send_sems.at[w * npeer + k], recv_sem=recv_sems.at[w * npeer + k],
                device_id=dev, device_id_type=pl.DeviceIdType.MESH)

        local = []
        for w in range(n):
            for k in range(npeer):
                remote(w, k).start()
            cp = pltpu.make_async_copy(ins[w] if gather else ins[w].at[me], outs[w].at[me], loc_sems.at[w])
            cp.start()
            local.append(cp)
        for w in range(n):
            for k in range(npeer):
                arrival(w, k).wait_recv()
        for w in range(n):
            for k in range(npeer):
                remote(w, k).wait_send()
            local[w].wait()

    hbm = pl.BlockSpec(memory_space=pl.ANY)
    out_shape = [jax.ShapeDtypeStruct((N_DEV,) + (a.shape if gather else a.shape[1:]), a.dtype) for a in arrs]
    return pl.pallas_call(
        body, name=name,
        in_specs=[hbm] * n, out_specs=[hbm] * n, out_shape=out_shape,
        scratch_shapes=[pltpu.SemaphoreType.DMA((n * npeer,)), pltpu.SemaphoreType.DMA((n * npeer,)),
                        pltpu.SemaphoreType.DMA((n,))],
        compiler_params=pltpu.CompilerParams(has_side_effects=True),
    )(*arrs)


def _gather_two_level(shard, *, name):
    def body(x_ref, out_ref, send_sems, recv_sems, local_sem):
        x, y, c = lax.axis_index("x"), lax.axis_index("y"), lax.axis_index("c")
        me, sibling = (x, y, c), (x, y, 1 - c)
        chips = [(1 - x, y), (x, 1 - y), (1 - x, 1 - y)]

        def slot(px, py, pc):
            return out_ref.at[4 * px + 2 * py + pc]

        def copy(k, block, to, src=None):
            return pltpu.make_async_remote_copy(
                src_ref=slot(*block) if src is None else src, dst_ref=slot(*block),
                send_sem=send_sems.at[k], recv_sem=recv_sems.at[k],
                device_id=to, device_id_type=pl.DeviceIdType.MESH)

        mine = pltpu.make_async_copy(x_ref, slot(*me), local_sem)
        mine.start()
        first = [copy(0, me, sibling, src=x_ref)]
        first += [copy(1 + j, me, (*chip, c), src=x_ref) for j, chip in enumerate(chips)]
        for cp in first:
            cp.start()
        passed = [copy(4 + j, (*chip, c), sibling) for j, chip in enumerate(chips)]
        for j, chip in enumerate(chips):
            copy(1 + j, (*chip, c), me).wait_recv()
            passed[j].start()
        copy(0, sibling, me).wait_recv()
        for j, chip in enumerate(chips):
            copy(4 + j, (*chip, 1 - c), me).wait_recv()
        for cp in first + passed:
            cp.wait_send()
        mine.wait()

    return pl.pallas_call(
        body, name=name,
        in_specs=[_ANY], out_specs=_ANY,
        out_shape=jax.ShapeDtypeStruct((N_DEV,) + shard.shape, shard.dtype),
        scratch_shapes=[pltpu.SemaphoreType.DMA((N_DEV - 1,)), pltpu.SemaphoreType.DMA((N_DEV - 1,)),
                        pltpu.SemaphoreType.DMA],
        compiler_params=pltpu.CompilerParams(has_side_effects=True),
    )(shard)


_HBM = pl.BlockSpec(memory_space=pltpu.HBM)
_SEM = pl.BlockSpec(memory_space=pltpu.SEMAPHORE)
_EFFECT = pltpu.SideEffectType.DATAFLOW_SIDE_EFFECTING
NPEER = N_DEV - 1


def _peer_table():
    x, y, c = lax.axis_index("x"), lax.axis_index("y"), lax.axis_index("c")
    peers = []
    for k in range(1, N_DEV):
        px = 1 - x if k & 4 else x
        py = 1 - y if k & 2 else y
        pc = 1 - c if k & 1 else c
        peers.append(((px, py, pc), 4 * px + 2 * py + pc))
    return 4 * x + 2 * y + c, peers


def _split_copy(ins, lands, send_sems, recv_sems, gather, me, peers, w, k, arriving):
    dev, idx = peers[k]
    return pltpu.make_async_remote_copy(
        src_ref=ins[w] if gather else ins[w].at[idx],
        dst_ref=lands[w].at[idx if arriving else me],
        send_sem=send_sems.at[w * NPEER + k], recv_sem=recv_sems.at[w * NPEER + k],
        device_id=dev, device_id_type=pl.DeviceIdType.MESH)


def _exchange_start(arrs, *, gather, name, deps=()):
    n = len(arrs)
    land_shapes = [(N_DEV,) + (a.shape if gather else a.shape[1:]) for a in arrs]

    def body(*refs):
        ins, lands = refs[:n], refs[n:2 * n]
        send_sems, recv_sems = refs[2 * n + len(deps)], refs[2 * n + len(deps) + 1]
        token = refs[-1]
        me, peers = _peer_table()
        for w in range(n):
            for k in range(NPEER):
                _split_copy(ins, lands, send_sems, recv_sems, gather, me, peers, w, k, False).start()
        token[...] = jnp.zeros_like(token)

    out_shape = ([pltpu.SemaphoreType.DMA((n * NPEER,)), pltpu.SemaphoreType.DMA((n * NPEER,))]
                 + [pltpu.HBM(a.shape, a.dtype) for a in arrs]
                 + [pltpu.HBM(s, a.dtype) for s, a in zip(land_shapes, arrs)]
                 + [jax.ShapeDtypeStruct((8, LANES), F32)])
    res = pl.pallas_call(
        body, name=name,
        in_specs=[_HBM] * (2 * n) + [_ANY] * len(deps),
        out_specs=[_SEM, _SEM] + [_HBM] * (2 * n) + [pl.BlockSpec(memory_space=pltpu.VMEM)],
        out_shape=out_shape,
        input_output_aliases={i: 2 + i for i in range(2 * n)},
        compiler_params=pltpu.CompilerParams(has_side_effects=_EFFECT),
    )(*[pltpu.with_memory_space_constraint(a, pltpu.HBM) for a in arrs],
      *[pltpu.with_memory_space_constraint(lax.empty(s, a.dtype), pltpu.HBM) for s, a in zip(land_shapes, arrs)],
      *deps)
    return (n, gather, res[0], res[1], res[2:2 + n], res[2 + n:2 + 2 * n]), res[-1]


def _exchange_wait(handle, after, *, name):
    n, gather, send_sems, recv_sems, ins_thru, lands_thru = handle

    def body(*refs):
        ins, lands = refs[:n], refs[n:2 * n]
        send_s, recv_s = refs[2 * n], refs[2 * n + 1]
        me, peers = _peer_table()
        for w in range(n):
            for k in range(NPEER):
                _split_copy(ins, lands, send_s, recv_s, gather, me, peers, w, k, False).wait_send()
                _split_copy(ins, lands, send_s, recv_s, gather, me, peers, w, k, True).wait_recv()

    res = pl.pallas_call(
        body, name=name,
        in_specs=[_HBM] * (2 * n) + [_SEM, _SEM, pl.BlockSpec(memory_space=pl.ANY)],
        out_specs=[_HBM] * (2 * n),
        out_shape=[pltpu.HBM(a.shape, a.dtype) for a in list(ins_thru) + list(lands_thru)],
        input_output_aliases={i: i for i in range(2 * n)},
        compiler_params=pltpu.CompilerParams(has_side_effects=_EFFECT),
    )(*ins_thru, *lands_thru, send_sems, recv_sems, after)
    return res[:n], res[n:2 * n]


def _ordered_sum(s_ref, own_ref):
    if own_ref is None:
        blocks = [s_ref[q].astype(F32) for q in range(N_DEV)]
    else:
        me = 4 * lax.axis_index("x") + 2 * lax.axis_index("y") + lax.axis_index("c")
        own = own_ref[...]
        blocks = [jnp.where(me == q, own, s_ref[q]).astype(F32) for q in range(N_DEV)]
    acc = blocks[0]
    for b in blocks[1:]:
        acc = acc + b
    return acc


def _sum8(stack, own, *, name):
    _, R, C = stack.shape
    if R % 8 == 0:
        tr, tc = _pick(R, max(8, STEP_BYTES // (C * 4 * (N_DEV + 2))), 8), C
    else:
        tr, tc = R, _pick(C, max(LANES, STEP_BYTES // (R * 4 * (N_DEV + 2))))

    def body(s_ref, own_ref, o_ref):
        o_ref[...] = _ordered_sum(s_ref, own_ref)

    blk = pl.BlockSpec((tr, tc), lambda i, j: (i, j))
    return pl.pallas_call(
        body, name=name, grid=(R // tr, C // tc),
        in_specs=[pl.BlockSpec((N_DEV, tr, tc), lambda i, j: (0, i, j)), blk],
        out_specs=blk,
        out_shape=jax.ShapeDtypeStruct((R, C), F32),
        compiler_params=_cparams("parallel", "parallel"),
    )(stack, own)


def _adamw_math(w, g, m, v):
    m = ADAM_B1 * m + (1.0 - ADAM_B1) * g
    v = ADAM_B2 * v + (1.0 - ADAM_B2) * (g * g)
    m_hat = m / (1.0 - ADAM_B1 ** ADAM_STEP)
    v_hat = v / (1.0 - ADAM_B2 ** ADAM_STEP)
    delta = -ADAM_LR * (m_hat / (jnp.sqrt(v_hat) + ADAM_EPS) + ADAM_WD * w)
    return delta, m, v


def _adamw(w, g, m, v, *, name, stacked, own=None, transposed=False):
    R, C = w.shape
    if transposed:
        tr = _pick(R, max(LANES, STEP_BYTES // (C * 4 * (9 + N_DEV))))
    else:
        tr = _pick(R, max(8, STEP_BYTES // (C * 4 * (8 + (N_DEV if stacked else 1)))), 8)
    has_own = own is not None

    def body(w_ref, g_ref, m_ref, v_ref, *rest):
        go_ref, d_ref, mo_ref, vo_ref = rest[-4:]
        g = _ordered_sum(g_ref, rest[0] if has_own else None) if stacked else g_ref[...]
        if transposed:
            g = g.T
        delta, m2, v2 = _adamw_math(w_ref[...], g, m_ref[...], v_ref[...])
        go_ref[...] = g
        d_ref[...] = delta
        mo_ref[...] = m2
        vo_ref[...] = v2

    row = pl.BlockSpec((tr, C), lambda i: (i, 0))
    if transposed:
        g_spec, own_spec = pl.BlockSpec((N_DEV, C, tr), lambda i: (0, 0, i)), pl.BlockSpec((C, tr), lambda i: (0, i))
    else:
        g_spec, own_spec = (pl.BlockSpec((N_DEV, tr, C), lambda i: (0, i, 0)) if stacked else row), row
    return pl.pallas_call(
        body, name=name, grid=(R // tr,),
        in_specs=[row, g_spec, row, row] + [own_spec] * has_own, out_specs=[row] * 4,
        out_shape=[jax.ShapeDtypeStruct((R, C), F32)] * 4,
        compiler_params=_cparams("parallel"),
    )(w, g, m, v, *([own] if has_own else []))


def kernel(x, positions, attn_norm, w_in, fox_f_bias, swa_sinks, w_branch_swa, w_branch_fox, w_out, mlp_norm, w_up, w_down, final_norm, loss_target, m_attn_norm, m_w_in, m_fox_f_bias, m_swa_sinks, m_w_branch_swa, m_w_branch_fox, m_w_out, m_mlp_norm, m_w_up, m_w_down, m_final_norm, v_attn_norm, v_w_in, v_fox_f_bias, v_swa_sinks, v_w_branch_swa, v_w_branch_fox, v_w_out, v_mlp_norm, v_w_up, v_w_down, v_final_norm):
    S, D = x.shape[1], x.shape[2]
    DFF = w_up.shape[2] * N_DEV
    d_in = w_in.shape[2] * N_DEV
    assert d_in == QKV_W + FOX_HEADS + 2 * D and (2 * D) % SWA_Q_W == 0 and S % (4 * LANES) == 0
    q_off = 2 * D
    k_off = q_off + SWA_Q_W
    v_off = k_off + SWA_KV_W
    fq_off = v_off + SWA_KV_W
    fk_off = fq_off + FOX_W
    fv_off = fk_off + FOX_W
    fl_off = fv_off + FOX_W
    NP = fl_off + FL_PAD
    x2d, tgt = x[0], loss_target[0]

    shards = [w_in[0].T.astype(BF16), w_branch_swa[0].T.astype(BF16), w_branch_fox[0].T.astype(BF16),
              w_out[0].astype(BF16), w_up[0].T.astype(BF16), w_down[0].astype(BF16)]
    me = 4 * lax.axis_index("x") + 2 * lax.axis_index("y") + lax.axis_index("c")

    def filled(stack, own):
        return lax.dynamic_update_slice(stack, own[None], (me,) + (0,) * own.ndim)

    g_in = _gather_two_level(shards[0], name="gather_w_in")
    h_rest, tok_rest = _exchange_start(shards[1:], gather=True, name="gather_rest_start", deps=[g_in])

    tm = _pick(S, 1024)
    td = _pick(D, 1024)
    tf = _pick(DFF, 1024)
    tnp = _pick(NP, 1024)

    h1 = _rms_fwd(x2d, attn_norm, name="rms1", deps=[tok_rest])
    w_in_t = g_in.reshape(d_in, D)
    w_in_p = jnp.concatenate([w_in_t[QKV_W + FOX_HEADS:], w_in_t[:QKV_W], w_in_t[QKV_W:QKV_W + FOX_HEADS],
                              jnp.zeros((FL_PAD - FOX_HEADS, D), BF16)], axis=0)
    w_fl_t = w_in_t[QKV_W:QKV_W + FOX_HEADS]
    proj, = _matmul(h1, w_in_p, mode="nt", name="mm_in", out_dtypes=[BF16], tm=_pick(S, 2048), tn=tnp, tk=D)
    z_sd, = _matmul(h1, w_fl_t, mode="nt", name="mm_flogit", out_dtypes=[F32], tm=tm, tn=FOX_HEADS, tk=D)
    z_t = z_sd.T
    bias_col = fox_f_bias.reshape(FOX_HEADS, 1)
    negc = _fox_prep(z_t, bias_col, name="fox_prep")
    (fbq, fbk), (bbq, bbk) = _fox_blocks(S)
    inv_freq = ROPE_THETA ** (-jnp.arange(0, HEAD_DIM, 2, dtype=F32) / HEAD_DIM)
    invf = jnp.tile(inv_freq, LANES // (HEAD_DIM // 2)).reshape(1, LANES)
    cos_t, sin_t = _rope_tables(positions.reshape(S, 1), invf, name="rope_tables")
    q_rope, k_rope = _rope_fwd(proj, cos_t, sin_t, q_off=q_off, k_off=k_off, name="rope_fwd")
    sinks = swa_sinks.reshape(-1)
    swa_mask = _swa_mask_bias()
    o_a = _swa_fwd(q_rope, k_rope, proj, sinks, swa_mask, v_off=v_off, name="swa_fwd")
    o_b, lse = _fox_fwd(proj, _key_bias_blocks(negc, fbk), q_off=fq_off, k_off=fk_off, v_off=fv_off,
                        bq=fbq, bk=fbk, name="fox_fwd")
    s_rest, g_rest = _exchange_wait(h_rest, o_b, name="gather_rest_wait")
    g_bs, g_bf, g_o, g_up, g_dn = [filled(g, s) for g, s in zip(g_rest, s_rest)]
    w_bs_t = g_bs.reshape(D, SWA_Q_W)
    w_bf_t = g_bf.reshape(D, FOX_W)
    w_o = g_o.reshape(D, D)
    w_up_t = g_up.reshape(DFF, D)
    w_dn = g_dn.reshape(DFF, D)
    ya, = _matmul(o_a, w_bs_t, mode="nt", name="mm_branch_swa", out_dtypes=[BF16], tm=tm, tn=td, tk=SWA_Q_W)
    gate_maps = [lambda i, j, k: (i, j), lambda i, j, k: (i, j), lambda i, j, k: (i, j + D // td)]

    def merge_epi(acc, ya_t, ga_t, gb_t):
        merged = _sigmoid(ga_t.astype(F32)) * ya_t.astype(F32) + _sigmoid(gb_t.astype(F32)) * acc
        return acc, merged

    yb, merged = _matmul(o_b, w_bf_t, mode="nt", name="mm_branch_fox", out_dtypes=[BF16, BF16],
                         tm=tm, tn=td, tk=FOX_W, extras=[ya, proj, proj], extra_maps=gate_maps,
                         epilogue=merge_epi)
    def out_epi(acc, r, g):
        xm = acc + r
        rr = lax.rsqrt(jnp.mean(xm * xm, axis=-1, keepdims=True) + RMS_EPS)
        return xm, xm * rr * g

    x_mid, h2 = _matmul(merged, w_o, mode="nn", name="mm_out", out_dtypes=[F32, BF16], tm=_pick(S, 512), tn=D, tk=D,
                        extras=[x2d, mlp_norm], extra_maps=[lambda i, j, k: (i, j), lambda i, j, k: (0, 0)],
                        extra_shapes=[None, (1, D)], epilogue=out_epi)
    u, = _matmul(h2, w_up_t, mode="nt", name="mm_up", out_dtypes=[BF16], tm=_pick(S, 2048), tn=tf, tk=D,
                 epilogue=lambda acc: (jnp.maximum(acc, 0.0),))
    x_fin, = _matmul(u, w_dn, mode="nn", name="mm_down", out_dtypes=[F32], tm=_pick(S, 512), tn=D, tk=_pick(DFF, 2048),
                     a_fn=_square_bf16, extras=[x_mid], epilogue=lambda acc, r: (acc + r,))

    dx3b, dg3, loss_part = _loss_head(x_fin, tgt, final_norm.reshape(1, D), name="loss_head")
    d_up, = _matmul(dx3b, w_dn, mode="nt", name="mm_d_act", out_dtypes=[BF16], tm=_pick(S, 2048), tn=tf, tk=D,
                    extras=[u], epilogue=lambda acc, ut: (acc * (2.0 * ut.astype(F32)),))
    tks = _pick(S, 2048)
    dw_dn, = _matmul(u, dx3b, mode="tn", name="mm_dw_down", out_dtypes=[BF16], tm=tf, tn=td, tk=tks,
                     a_fn=_square_bf16)
    dh2, = _matmul(d_up, w_up_t, mode="nn", name="mm_dh2", out_dtypes=[BF16], tm=_pick(S, 512), tn=D,
                   tk=_pick(DFF, 2048))
    dw_up_t, = _matmul(d_up, h2, mode="tn", name="mm_dw_up", out_dtypes=[BF16], tm=tf, tn=td, tk=tks)
    h_s1, tok_s1 = _exchange_start([dw_up_t.reshape(N_DEV, DFF // N_DEV, D), dw_dn.reshape(N_DEV, DFF // N_DEV, D)],
                                   gather=False, name="scatter_mlp_start")
    dx2b, dg2 = _rms_bwd(dh2, x_mid, mlp_norm, dx3b, name="rms2_bwd", out_dtype=BF16, deps=[tok_s1])

    def gate_bwd_epi(dm, ya_t, yb_t, ga_t, gb_t):
        sa, sb = _sigmoid(ga_t.astype(F32)), _sigmoid(gb_t.astype(F32))
        return (dm * sa, dm * sb, dm * ya_t.astype(F32) * sa * (1.0 - sa), dm * yb_t.astype(F32) * sb * (1.0 - sb))

    gmaps = [lambda i, j, k: (i, j), lambda i, j, k: (i, j), lambda i, j, k: (i, j),
             lambda i, j, k: (i, j + D // td)]
    d_ya, d_yb, d_ga, d_gb = _matmul(dx2b, w_o, mode="nt", name="mm_d_merged", out_dtypes=[BF16] * 4,
                                     tm=tm, tn=td, tk=D, extras=[ya, yb, proj, proj], extra_maps=gmaps,
                                     epilogue=gate_bwd_epi)
    dw_o, = _matmul(merged, dx2b, mode="tn", name="mm_dw_out", out_dtypes=[BF16], tm=td, tn=td, tk=tks)
    d_oa, = _matmul(d_ya, w_bs_t, mode="nn", name="mm_d_oa", out_dtypes=[BF16], tm=tm, tn=SWA_Q_W, tk=D)
    d_ob, = _matmul(d_yb, w_bf_t, mode="nn", name="mm_d_ob", out_dtypes=[BF16], tm=tm, tn=FOX_W, tk=D)
    dw_bs_t, = _matmul(d_ya, o_a, mode="tn", name="mm_dw_bs", out_dtypes=[BF16], tm=td, tn=SWA_Q_W, tk=tks)
    dw_bf_t, = _matmul(d_yb, o_b, mode="tn", name="mm_dw_bf", out_dtypes=[BF16], tm=td, tn=FOX_W, tk=tks)
    h_s2, tok_s2 = _exchange_start([dw_bs_t.reshape(N_DEV, D // N_DEV, SWA_Q_W),
                                    dw_bf_t.reshape(N_DEV, D // N_DEV, FOX_W), dw_o.reshape(N_DEV, D // N_DEV, D)],
                                   gather=False, name="scatter_attn_start")
    def row_blocks_t(a):
        return a.reshape(S // bbq, bbq, FOX_W).transpose(0, 2, 1)

    d_fq, d_fk, d_fv, dcol4, drow4 = _fox_bwd(proj, _key_bias_blocks(negc, bbk), o_b, lse, d_ob,
                                              row_blocks_t(proj[:, fq_off:fq_off + FOX_W]), row_blocks_t(d_ob),
                                              q_off=fq_off, k_off=fk_off, v_off=fv_off, bq=bbq, bk=bbk,
                                              name="fox_bwd", deps=[tok_s2])
    dcol = dcol4.transpose(0, 2, 1, 3).reshape(FOX_HEADS, S)
    drow = drow4.transpose(0, 2, 1, 3).reshape(FOX_HEADS, S)
    dz_t, dbias_l = _fox_post(drow, dcol, z_t, bias_col, name="fox_post")
    d_aq, dk_c, dk_p, dv_c, dv_p, dsink_l = _swa_bwd(q_rope, k_rope, proj, sinks, d_oa, cos_t, sin_t, swa_mask,
                                                     v_off=v_off, name="swa_bwd")
    d_ak, d_av = _rope_bwd(dk_c, dk_p, dv_c, dv_p, cos_t, sin_t, name="rope_bwd")
    dz_pad = jnp.pad(dz_t.T.astype(BF16), ((0, 0), (0, FL_PAD - FOX_HEADS)))
    d_proj = jnp.concatenate([d_ga, d_gb, d_aq, d_ak, d_av, d_fq, d_fk, d_fv, dz_pad], axis=1)
    tkp = _pick(NP, 2304)
    dw_in_p, = _matmul(d_proj, h1, mode="tn", name="mm_dw_in", out_dtypes=[BF16], tm=_pick(NP, 512), tn=D, tk=tks)
    dw_in_t = jnp.concatenate([dw_in_p[q_off:q_off + QKV_W], dw_in_p[fl_off:fl_off + FOX_HEADS], dw_in_p[:q_off]],
                              axis=0)
    h_s3, tok_s3 = _exchange_start([dw_in_t.reshape(N_DEV, d_in // N_DEV, D)], gather=False,
                                   name="scatter_in_start")
    dh1, = _matmul(d_proj, w_in_p, mode="nn", name="mm_dh1", out_dtypes=[BF16], tm=_pick(S, 512), tn=D, tk=tkp,
                   deps=[tok_s3])
    dx, dg1 = _rms_bwd(dh1, x2d, attn_norm, dx2b, name="rms1_bwd", out_dtype=F32)

    dbias = dbias_l[:, 0]
    dsinks = dsink_l[:, :, 0].reshape(-1)
    nsm = 3 * D + 2 * LANES
    tail = jnp.zeros((2 * LANES,), F32)
    small_g = jnp.concatenate([dg1[0], dg2[0], dg3[0],
                               tail.at[0:16].set(dbias).at[16:32].set(dsinks).at[32].set(loss_part[0, 0])])

    def pack(a_norm, b_norm, f_norm, bias, snk):
        return jnp.concatenate([a_norm[0], b_norm[0], f_norm,
                                tail.at[0:16].set(bias[0]).at[16:32].set(snk[0])]).reshape(1, nsm)

    small_stack, = _exchange([small_g.reshape(1, nsm)], gather=True, name="gather_small")
    u_sm = _adamw(pack(attn_norm, mlp_norm, final_norm, fox_f_bias, swa_sinks), small_stack,
                  pack(m_attn_norm, m_mlp_norm, m_final_norm, m_fox_f_bias, m_swa_sinks),
                  pack(v_attn_norm, v_mlp_norm, v_final_norm, v_fox_f_bias, v_swa_sinks),
                  name="adamw_small", stacked=True)
    loss = u_sm[0][0, 3 * D + 32]

    def own_of(src):
        return lax.dynamic_index_in_dim(src, me, 0, keepdims=False)

    def update_t(stack, src, w, m, v, nm):
        g = _sum8(stack, own_of(src), name="sum_" + nm).T
        return _adamw(w[0], g, m[0], v[0], name="adamw_" + nm, stacked=False)

    def update(stack, src, w, m, v, nm, transposed=False):
        return _adamw(w[0], stack, m[0], v[0], name="adamw_" + nm, stacked=True, own=own_of(src),
                      transposed=transposed)

    (s_up, s_dn), (r_up, r_dn) = _exchange_wait(h_s1, u_sm[1], name="scatter_mlp_wait")
    u_up = update(r_up, s_up, w_up, m_w_up, v_w_up, "w_up", transposed=True)
    u_dn = update(r_dn, s_dn, w_down, m_w_down, v_w_down, "w_down")
    (s_bs, s_bf, s_o), (r_bs, r_bf, r_o) = _exchange_wait(h_s2, u_dn[1], name="scatter_attn_wait")
    u_bs = update(r_bs, s_bs, w_branch_swa, m_w_branch_swa, v_w_branch_swa, "w_bs", transposed=True)
    u_bf = update(r_bf, s_bf, w_branch_fox, m_w_branch_fox, v_w_branch_fox, "w_bf", transposed=True)
    u_o = update(r_o, s_o, w_out, m_w_out, v_w_out, "w_out")
    (s_w_in,), (r_in,) = _exchange_wait(h_s3, u_o[1], name="scatter_in_wait")
    u_in = update_t(r_in, s_w_in, w_in, m_w_in, v_w_in, "w_in")

    def small(kind):
        a = u_sm[kind][0]
        return dict(attn_norm=a[0:D][None], mlp_norm=a[D:2 * D][None], final_norm=a[2 * D:3 * D],
                    fox_f_bias=a[3 * D:3 * D + 16][None], swa_sinks=a[3 * D + 16:3 * D + 32][None])

    big = dict(w_in=u_in, w_branch_swa=u_bs, w_branch_fox=u_bf, w_out=u_o, w_up=u_up, w_down=u_dn)
    order = ["attn_norm", "w_in", "fox_f_bias", "swa_sinks", "w_branch_swa", "w_branch_fox", "w_out", "mlp_norm",
             "w_up", "w_down", "final_norm"]
    outs = [loss, dx[None]]
    for kind in range(4):
        sm = small(kind)
        for nm in order:
            outs.append(big[nm][kind][None] if nm in big else sm[nm])
    return tuple(outs)
```

```python
import functools

import jax
import jax.numpy as jnp
from jax import lax
from jax.experimental import pallas as pl
from jax.experimental.pallas import tpu as pltpu

F32 = jnp.float32
BF16 = jnp.bfloat16

N_DEV = 8
HEAD_DIM = 64
SWA_Q_W = 1024
SWA_KV_W = 128
SWA_GROUP = 8
WINDOW = 128
FOX_W = 1024
FOX_HEADS = 16
QKV_W = SWA_Q_W + 2 * SWA_KV_W + 3 * FOX_W
FL_PAD = 256
ROPE_THETA = 10000.0
RMS_EPS = 1e-6
ATT_SCALE = 0.125
NEG = -1e30

ADAM_LR = 0.001
ADAM_B1 = 0.9
ADAM_B2 = 0.999
ADAM_EPS = 1e-08
ADAM_WD = 0.01
ADAM_STEP = 10

FOX_FWD_BLOCKS = (1024, 1024)
FOX_BWD_BLOCKS = (1024, 512)
FOX_FWD_PAIRS = 2

LANES = 128
VMEM_LIMIT = 56 * 1024 * 1024
STEP_BYTES = 12 * 1024 * 1024


def _cparams(*sem):
    return pltpu.CompilerParams(dimension_semantics=sem, vmem_limit_bytes=VMEM_LIMIT)


def _pick(dim, pref, align=LANES):
    best = None
    t = align
    while t <= min(dim, pref):
        if dim % t == 0:
            best = t
        t += align
    return best if best is not None else dim


_DIMS = {"nn": ((1,), (0,)), "nt": ((1,), (1,)), "tn": ((0,), (0,))}


_ANY = pl.BlockSpec(memory_space=pl.ANY)


def _matmul(a, b, *, mode, name, out_dtypes, tm, tn, tk, extras=(), extra_maps=None, extra_shapes=None,
            a_fn=None, epilogue=None, deps=()):
    if mode == "nn":
        (M, K), (K2, N) = a.shape, b.shape
    elif mode == "nt":
        (M, K), (N, K2) = a.shape, b.shape
    else:
        (K, M), (K2, N) = a.shape, b.shape
    assert K == K2, (name, a.shape, b.shape)
    assert M % tm == 0 and N % tn == 0 and K % tk == 0, (name, M, N, K, tm, tn, tk)
    nk = K // tk
    ne, no = len(extras), len(out_dtypes)
    dims = (_DIMS[mode], ((), ()))

    def body(*refs):
        a_ref, b_ref = refs[0], refs[1]
        ex_refs = refs[2:2 + ne]
        out_refs = refs[2 + ne + len(deps):2 + ne + len(deps) + no]

        def finish(acc):
            res = (acc,) if epilogue is None else epilogue(acc, *[e[...] for e in ex_refs])
            for o_ref, r in zip(out_refs, res):
                o_ref[...] = r.astype(o_ref.dtype)

        def product():
            av = a_ref[...]
            if a_fn is not None:
                av = a_fn(av)
            return lax.dot_general(av, b_ref[...], dims, preferred_element_type=F32)

        if nk == 1:
            finish(product())
        else:
            acc_ref = refs[-1]
            k = pl.program_id(2)

            @pl.when(k == 0)
            def _():
                acc_ref[...] = jnp.zeros_like(acc_ref)

            acc_ref[...] += product()

            @pl.when(k == nk - 1)
            def _():
                finish(acc_ref[...])

    if mode == "tn":
        a_spec = pl.BlockSpec((tk, tm), lambda i, j, k: (k, i))
    else:
        a_spec = pl.BlockSpec((tm, tk), lambda i, j, k: (i, k))
    if mode == "nt":
        b_spec = pl.BlockSpec((tn, tk), lambda i, j, k: (j, k))
    else:
        b_spec = pl.BlockSpec((tk, tn), lambda i, j, k: (k, j))
    if extra_maps is None:
        extra_maps = [lambda i, j, k: (i, j)] * ne
    if extra_shapes is None:
        extra_shapes = [None] * ne
    ex_specs = [pl.BlockSpec(s or (tm, tn), m) for s, m in zip(extra_shapes, extra_maps)]
    out_spec = [pl.BlockSpec((tm, tn), lambda i, j, k: (i, j)) for _ in range(no)]
    res = pl.pallas_call(
        body,
        name=name,
        grid=(M // tm, N // tn, nk),
        in_specs=[a_spec, b_spec] + ex_specs + [_ANY] * len(deps),
        out_specs=out_spec,
        out_shape=[jax.ShapeDtypeStruct((M, N), d) for d in out_dtypes],
        scratch_shapes=[pltpu.VMEM((tm, tn), F32)] if nk > 1 else [],
        compiler_params=_cparams("parallel", "parallel", "arbitrary"),
    )(a, b, *extras, *deps)
    return res


def _square_bf16(t):
    tf = t.astype(F32)
    return (tf * tf).astype(BF16)


def _sigmoid(g):
    return 1.0 / (1.0 + jnp.exp(-g))


def _rms_fwd(x, gain, *, name, deps=()):
    S, D = x.shape
    tr = _pick(S, 512, 8)

    def body(x_ref, g_ref, *rest):
        h_ref = rest[-1]
        xv = x_ref[...]
        r = lax.rsqrt(jnp.mean(xv * xv, axis=-1, keepdims=True) + RMS_EPS)
        h_ref[...] = (xv * r * g_ref[...]).astype(BF16)

    return pl.pallas_call(
        body, name=name, grid=(S // tr,),
        in_specs=[pl.BlockSpec((tr, D), lambda i: (i, 0)), pl.BlockSpec((1, D), lambda i: (0, 0))] + [_ANY] * len(deps),
        out_specs=pl.BlockSpec((tr, D), lambda i: (i, 0)),
        out_shape=jax.ShapeDtypeStruct((S, D), BF16),
        compiler_params=_cparams("parallel"),
    )(x, gain, *deps)


def _rms_bwd(dh, x, gain, dres, *, name, out_dtype, deps=()):
    S, D = x.shape
    tr = _pick(S, 512, 8)

    def body(dh_ref, x_ref, g_ref, dres_ref, *rest):
        outs = rest[len(deps):]
        dx_ref, dg_ref = outs[0], outs[-1]
        xv = x_ref[...]
        r = lax.rsqrt(jnp.mean(xv * xv, axis=-1, keepdims=True) + RMS_EPS)
        xh = xv * r
        dhv = dh_ref[...].astype(F32)
        t = dhv * g_ref[...]
        dx = r * (t - xh * jnp.mean(t * xh, axis=-1, keepdims=True)) + dres_ref[...].astype(F32)
        dx_ref[...] = dx.astype(out_dtype)
        part = jnp.sum(dhv * xh, axis=0, keepdims=True)

        @pl.when(pl.program_id(0) == 0)
        def _():
            dg_ref[...] = part

        @pl.when(pl.program_id(0) > 0)
        def _():
            dg_ref[...] += part

    row = pl.BlockSpec((tr, D), lambda i: (i, 0))
    vec = pl.BlockSpec((1, D), lambda i: (0, 0))
    return pl.pallas_call(
        body, name=name, grid=(S // tr,),
        in_specs=[row, row, vec, row] + [_ANY] * len(deps), out_specs=[row, vec],
        out_shape=[jax.ShapeDtypeStruct((S, D), out_dtype), jax.ShapeDtypeStruct((1, D), F32)],
        compiler_params=_cparams("arbitrary"),
    )(dh, x, gain, dres, *deps)


def _loss_head(x3, target, gain, *, name):
    S, D = x3.shape
    tr = _pick(S, 512, 8)

    def body(x_ref, t_ref, g_ref, dxb_ref, dg_ref, loss_ref):
        xv = x_ref[...]
        r = lax.rsqrt(jnp.mean(xv * xv, axis=-1, keepdims=True) + RMS_EPS)
        xh = xv * r
        gv = g_ref[...]
        err = xh * gv - t_ref[...]
        lpart = jnp.zeros((1, LANES), F32) + (0.5 / D) * jnp.sum(err * err)
        dy = err * (1.0 / D)
        t = dy * gv
        dx = r * (t - xh * jnp.mean(t * xh, axis=-1, keepdims=True))
        dxb_ref[...] = dx.astype(BF16)
        part = jnp.sum(dy * xh, axis=0, keepdims=True)

        @pl.when(pl.program_id(0) == 0)
        def _():
            dg_ref[...] = part
            loss_ref[...] = lpart

        @pl.when(pl.program_id(0) > 0)
        def _():
            dg_ref[...] += part
            loss_ref[...] += lpart

    row = pl.BlockSpec((tr, D), lambda i: (i, 0))
    vec = pl.BlockSpec((1, D), lambda i: (0, 0))
    return pl.pallas_call(
        body, name=name, grid=(S // tr,),
        in_specs=[row, row, vec],
        out_specs=[row, vec, pl.BlockSpec((1, LANES), lambda i: (0, 0))],
        out_shape=[jax.ShapeDtypeStruct((S, D), BF16),
                   jax.ShapeDtypeStruct((1, D), F32), jax.ShapeDtypeStruct((1, LANES), F32)],
        compiler_params=_cparams("arbitrary"),
    )(x3, target, gain)


def _rope_tables(pos_col, invf, *, name):
    S = pos_col.shape[0]
    tr = _pick(S, 512, 8)

    def body(p_ref, f_ref, cos_ref, sin_ref):
        ang = p_ref[...].astype(F32) * f_ref[...]
        lane = lax.broadcasted_iota(jnp.int32, (1, LANES), 1)
        first = (lane % HEAD_DIM) < HEAD_DIM // 2
        sn = jnp.sin(ang)
        cos_ref[...] = jnp.cos(ang)
        sin_ref[...] = jnp.where(first, -sn, sn)

    return pl.pallas_call(
        body, name=name, grid=(S // tr,),
        in_specs=[pl.BlockSpec((tr, 1), lambda i: (i, 0)), pl.BlockSpec((1, LANES), lambda i: (0, 0))],
        out_specs=[pl.BlockSpec((tr, LANES), lambda i: (i, 0))] * 2,
        out_shape=[jax.ShapeDtypeStruct((S, LANES), F32)] * 2,
        compiler_params=_cparams("parallel"),
    )(pos_col, invf)


def _swap_halves(t):
    lane = lax.broadcasted_iota(jnp.int32, (1, LANES), 1)
    first = (lane % HEAD_DIM) < HEAD_DIM // 2
    return jnp.where(first, pltpu.roll(t, LANES - HEAD_DIM // 2, 1), pltpu.roll(t, HEAD_DIM // 2, 1))


def _rope_fwd(proj, cos_t, sin_t, *, q_off, k_off, name):
    S = proj.shape[0]
    tr = _pick(S, 512, 8)
    nqb = SWA_Q_W // LANES

    def body(q_ref, k_ref, c_ref, s_ref, qo_ref, ko_ref):
        cv, sv = c_ref[...], s_ref[...]
        for b in range(nqb):
            t = q_ref[:, b * LANES:(b + 1) * LANES].astype(F32)
            qo_ref[:, b * LANES:(b + 1) * LANES] = (t * cv + _swap_halves(t) * sv).astype(BF16)
        t = k_ref[...].astype(F32)
        ko_ref[...] = (t * cv + _swap_halves(t) * sv).astype(BF16)

    tab = pl.BlockSpec((tr, LANES), lambda i: (i, 0))
    return pl.pallas_call(
        body, name=name, grid=(S // tr,),
        in_specs=[pl.BlockSpec((tr, SWA_Q_W), lambda i: (i, q_off // SWA_Q_W)),
                  pl.BlockSpec((tr, LANES), lambda i: (i, k_off // LANES)), tab, tab],
        out_specs=[pl.BlockSpec((tr, SWA_Q_W), lambda i: (i, 0)), tab],
        out_shape=[jax.ShapeDtypeStruct((S, SWA_Q_W), BF16), jax.ShapeDtypeStruct((S, LANES), BF16)],
        compiler_params=_cparams("parallel"),
    )(proj, proj, cos_t, sin_t)


def _rope_bwd(dk_cur, dk_prev, dv_cur, dv_prev, cos_t, sin_t, *, name):
    S = dk_cur.shape[1]
    tr = _pick(S, 512)
    nb = S // tr

    def body(kc_ref, kp_ref, vc_ref, vp_ref, c_ref, s_ref, dko_ref, dvo_ref):
        cv, sv = c_ref[...], s_ref[...]
        row = pl.program_id(0) * tr + lax.broadcasted_iota(jnp.int32, (tr, 1), 0)
        has_next = row < S - WINDOW
        d = kc_ref[0] + kc_ref[1] + jnp.where(has_next, kp_ref[0] + kp_ref[1], 0.0)
        dko_ref[...] = (d * cv + _swap_halves(d * sv)).astype(BF16)
        dvo_ref[...] = (vc_ref[0] + vc_ref[1] + jnp.where(has_next, vp_ref[0] + vp_ref[1], 0.0)).astype(BF16)

    tab = pl.BlockSpec((tr, LANES), lambda i: (i, 0))
    cur = pl.BlockSpec((2, tr, LANES), lambda i: (0, i, 0))
    return pl.pallas_call(
        body, name=name, grid=(nb,),
        in_specs=[cur, cur, cur, cur, tab, tab],
        out_specs=[tab, tab],
        out_shape=[jax.ShapeDtypeStruct((S, LANES), BF16), jax.ShapeDtypeStruct((S, LANES), BF16)],
        compiler_params=_cparams("parallel"),
    )(dk_cur, dk_prev, dv_cur, dv_prev, cos_t, sin_t)


def _dot_nt(a, b):
    return lax.dot_general(a, b, (((1,), (1,)), ((), ())), preferred_element_type=F32)


def _dot_tn(a, b):
    return lax.dot_general(a, b, (((0,), (0,)), ((), ())), preferred_element_type=F32)


def _dot_nn(a, b):
    return lax.dot_general(a, b, (((1,), (0,)), ((), ())), preferred_element_type=F32)


def _roll_half(t):
    return pltpu.roll(t.astype(F32), HEAD_DIM, 1).astype(t.dtype)


SWA_STACK = SWA_GROUP // 2


def _swa_mask_bias():
    rows = SWA_STACK * WINDOW
    row = lax.broadcasted_iota(jnp.int32, (rows, 2 * WINDOW), 0) % WINDOW
    col = lax.broadcasted_iota(jnp.int32, (rows, 2 * WINDOW), 1)
    diff = row + WINDOW - col
    window = (diff >= 0) & (diff < WINDOW)
    return jnp.stack([jnp.where(window & (col >= WINDOW), 0.0, NEG), jnp.where(window, 0.0, NEG)]).astype(F32)


def _swa_common(kp_ref, kc_ref, vp_ref, vc_ref):
    k2 = jnp.concatenate([kp_ref[...], kc_ref[...]], axis=0)
    v2 = jnp.concatenate([vp_ref[...], vc_ref[...]], axis=0)
    k_sw, v_sw = _roll_half(k2), _roll_half(v2)
    lane = lax.broadcasted_iota(jnp.int32, (1, LANES), 1)
    half = [lane < HEAD_DIM, lane >= HEAD_DIM]
    kk = [[k2 if hk == a else k_sw for a in range(2)] for hk in range(2)]
    vv = [[v2 if hk == a else v_sw for a in range(2)] for hk in range(2)]
    return half, kk, vv


def _swa_stack(ref, hk, mask, scale=None):
    parts = []
    for t in range(SWA_STACK):
        blk = ref[:, (hk * SWA_STACK + t) * LANES:(hk * SWA_STACK + t + 1) * LANES]
        if scale is not None:
            blk = blk * jnp.asarray(scale, blk.dtype)
        parts.append(jnp.where(mask, blk, jnp.zeros_like(blk)))
    return jnp.concatenate(parts, axis=0)


def _swa_sink_column(sink_ref, hk, a):
    blk = lax.broadcasted_iota(jnp.int32, (SWA_STACK * WINDOW, 1), 0) // WINDOW
    col = jnp.zeros((SWA_STACK * WINDOW, 1), F32)
    for t in range(SWA_STACK):
        col = jnp.where(blk == t, sink_ref[hk * SWA_GROUP + 2 * t + a], col)
    return col


def _swa_probs(qm, kk, mask_bias, sink):
    s = _dot_nt(qm, kk) + mask_bias
    m = jnp.maximum(jnp.max(s, axis=1, keepdims=True), sink)
    e = jnp.exp(s - m)
    es = jnp.exp(sink - m)
    inv = 1.0 / (jnp.sum(e, axis=1, keepdims=True) + es)
    return e * inv, es * inv


def _swa_mask_spec():
    return pl.BlockSpec((1, SWA_STACK * WINDOW, 2 * WINDOW), lambda n: (jnp.minimum(n, 1), 0, 0))


def _swa_fwd(q_rope, k_rope, proj, sinks, mask_bias, *, v_off, name):
    S = q_rope.shape[0]
    nb = S // WINDOW

    def body(sink_ref, q_ref, kp_ref, kc_ref, vp_ref, vc_ref, mask_ref, o_ref):
        half, kk, vv = _swa_common(kp_ref, kc_ref, vp_ref, vc_ref)
        for hk in range(2):
            outs = []
            for a in range(2):
                qm = _swa_stack(q_ref, hk, half[a], ATT_SCALE)
                p, _ = _swa_probs(qm, kk[hk][a], mask_ref[0], _swa_sink_column(sink_ref, hk, a))
                outs.append(_dot_nn(p.astype(BF16), vv[hk][a]))
            for t in range(SWA_STACK):
                rows = slice(t * WINDOW, (t + 1) * WINDOW)
                c0 = (hk * SWA_STACK + t) * LANES
                o_ref[:, c0:c0 + LANES] = jnp.where(half[0], outs[0][rows], outs[1][rows]).astype(BF16)

    prev = lambda n: (jnp.maximum(n - 1, 0), 0)
    cur = lambda n: (n, 0)
    vprev = lambda n: (jnp.maximum(n - 1, 0), v_off // LANES)
    vcur = lambda n: (n, v_off // LANES)
    blk = lambda m: pl.BlockSpec((WINDOW, LANES), m)
    return pl.pallas_call(
        body, name=name, grid=(nb,),
        in_specs=[pl.BlockSpec(memory_space=pltpu.SMEM),
                  pl.BlockSpec((WINDOW, SWA_Q_W), lambda n: (n, 0)),
                  blk(prev), blk(cur), blk(vprev), blk(vcur), _swa_mask_spec()],
        out_specs=pl.BlockSpec((WINDOW, SWA_Q_W), lambda n: (n, 0)),
        out_shape=jax.ShapeDtypeStruct((S, SWA_Q_W), BF16),
        compiler_params=_cparams("parallel"),
    )(sinks, q_rope, k_rope, k_rope, proj, proj, mask_bias)


def _swa_bwd(q_rope, k_rope, proj, sinks, d_o, cos_t, sin_t, mask_bias, *, v_off, name):
    S = q_rope.shape[0]
    nb = S // WINDOW

    def body(sink_ref, q_ref, kp_ref, kc_ref, vp_ref, vc_ref, do_ref, c_ref, s_ref, mask_ref,
             dq_ref, dkc_ref, dkp_ref, dvc_ref, dvp_ref, dsink_ref):
        n = pl.program_id(0)
        half, kk, vv = _swa_common(kp_ref, kc_ref, vp_ref, vc_ref)
        allowed = mask_ref[0]
        cv, sv = c_ref[...], s_ref[...]
        srow = lax.broadcasted_iota(jnp.int32, (SWA_GROUP, LANES), 0)
        for hk in range(2):
            dk_acc = jnp.zeros((2 * WINDOW, LANES), F32)
            dv_acc = jnp.zeros((2 * WINDOW, LANES), F32)
            dsink = jnp.zeros((SWA_GROUP, LANES), F32)
            dqs = []
            for a in range(2):
                qm = _swa_stack(q_ref, hk, half[a], ATT_SCALE)
                dom = _swa_stack(do_ref, hk, half[a])
                p, psink = _swa_probs(qm, kk[hk][a], allowed, _swa_sink_column(sink_ref, hk, a))
                dp = _dot_nt(dom, vv[hk][a])
                delta = jnp.sum(p * dp, axis=1, keepdims=True)
                ds = (p * (dp - delta)).astype(BF16)
                dsk = psink * delta
                for t in range(SWA_STACK):
                    dsink = dsink + jnp.where(srow == 2 * t + a, -jnp.sum(dsk[t * WINDOW:(t + 1) * WINDOW]), 0.0)
                dqs.append(_dot_nn(ds, kk[hk][a]) * ATT_SCALE)
                dk_acc = dk_acc + _dot_tn(ds, qm)
                dv_acc = dv_acc + _dot_tn(p.astype(BF16), dom)
            for t in range(SWA_STACK):
                rows = slice(t * WINDOW, (t + 1) * WINDOW)
                d = jnp.where(half[0], dqs[0][rows], dqs[1][rows])
                c0 = (hk * SWA_STACK + t) * LANES
                dq_ref[:, c0:c0 + LANES] = (d * cv + _swap_halves(d * sv)).astype(BF16)
            dk_t = jnp.where(half[hk], dk_acc + pltpu.roll(dk_acc, HEAD_DIM, 1), 0.0)
            dv_t = jnp.where(half[hk], dv_acc + pltpu.roll(dv_acc, HEAD_DIM, 1), 0.0)
            dkp_ref[hk] = dk_t[:WINDOW]
            dkc_ref[hk] = dk_t[WINDOW:]
            dvp_ref[hk] = dv_t[:WINDOW]
            dvc_ref[hk] = dv_t[WINDOW:]

            @pl.when(n == 0)
            def _():
                dsink_ref[hk] = dsink

            @pl.when(n > 0)
            def _():
                dsink_ref[hk] += dsink

    prev = lambda n: (jnp.maximum(n - 1, 0), 0)
    cur = lambda n: (n, 0)
    vprev = lambda n: (jnp.maximum(n - 1, 0), v_off // LANES)
    vcur = lambda n: (n, v_off // LANES)
    blk = lambda m: pl.BlockSpec((WINDOW, LANES), m)
    qblk = pl.BlockSpec((WINDOW, SWA_Q_W), lambda n: (n, 0))
    part = pl.BlockSpec((2, WINDOW, LANES), lambda n: (0, n, 0))
    part_prev = pl.BlockSpec((2, WINDOW, LANES), lambda n: (0, jnp.maximum(n - 1, 0), 0))
    part_shape = jax.ShapeDtypeStruct((2, S, LANES), F32)
    return pl.pallas_call(
        body, name=name, grid=(nb,),
        in_specs=[pl.BlockSpec(memory_space=pltpu.SMEM), qblk, blk(prev), blk(cur), blk(vprev), blk(vcur), qblk,
                  blk(cur), blk(cur), _swa_mask_spec()],
        out_specs=[qblk, part, part_prev, part, part_prev,
                   pl.BlockSpec((2, SWA_GROUP, LANES), lambda n: (0, 0, 0))],
        out_shape=[jax.ShapeDtypeStruct((S, SWA_Q_W), BF16), part_shape, part_shape, part_shape, part_shape,
                   jax.ShapeDtypeStruct((2, SWA_GROUP, LANES), F32)],
        compiler_params=_cparams("arbitrary"),
    )(sinks, q_rope, k_rope, k_rope, proj, proj, d_o, cos_t, sin_t, mask_bias)


def _fox_prep(z_t, bias_col, *, name):
    H, S = z_t.shape
    tb = _pick(S, 512)

    def body(z_ref, b_ref, o_ref, carry_ref):
        @pl.when(pl.program_id(0) == 0)
        def _():
            carry_ref[...] = jnp.zeros_like(carry_ref)

        zz = z_ref[...] + b_ref[...]
        t = jnp.exp(-jnp.abs(zz))
        log1p = jnp.where(t < 1e-2, t * (1.0 - t * (0.5 - t * (1.0 / 3.0))), jnp.log(1.0 + t))
        logf = jnp.minimum(zz, 0.0) - log1p
        r = lax.broadcasted_iota(jnp.int32, (tb, tb), 0)
        c = lax.broadcasted_iota(jnp.int32, (tb, tb), 1)
        tri = (r <= c).astype(BF16)
        hi = logf.astype(BF16)
        r1 = logf - hi.astype(F32)
        mid = r1.astype(BF16)
        lo = (r1 - mid.astype(F32)).astype(BF16)
        cs = _dot_nn(hi, tri) + _dot_nn(mid, tri) + _dot_nn(lo, tri) + carry_ref[:, 0:1]
        o_ref[...] = -cs
        carry_ref[...] = jnp.zeros_like(carry_ref) + cs[:, tb - 1:tb]

    return pl.pallas_call(
        body, name=name, grid=(S // tb,),
        in_specs=[pl.BlockSpec((H, tb), lambda i: (0, i)), pl.BlockSpec((H, 1), lambda i: (0, 0))],
        out_specs=pl.BlockSpec((H, tb), lambda i: (0, i)),
        out_shape=jax.ShapeDtypeStruct((H, S), F32),
        scratch_shapes=[pltpu.VMEM((H, LANES), F32)],
        compiler_params=_cparams("arbitrary"),
    )(z_t, bias_col)


def _fox_post(drow, dcol, z_t, bias_col, *, name):
    H, S = z_t.shape
    tb = _pick(S, 512)
    nb = S // tb

    def body(dr_ref, d_ref, z_ref, b_ref, dz_ref, db_ref, carry_ref):
        @pl.when(pl.program_id(0) == 0)
        def _():
            carry_ref[...] = jnp.zeros_like(carry_ref)
            db_ref[...] = jnp.zeros_like(db_ref)

        dc = dr_ref[...] - d_ref[...]
        r = lax.broadcasted_iota(jnp.int32, (tb, tb), 0)
        c = lax.broadcasted_iota(jnp.int32, (tb, tb), 1)
        tri = (r >= c).astype(BF16)
        hi = dc.astype(BF16)
        r1 = dc - hi.astype(F32)
        mid = r1.astype(BF16)
        lo = (r1 - mid.astype(F32)).astype(BF16)
        dlogf = _dot_nn(hi, tri) + _dot_nn(mid, tri) + _dot_nn(lo, tri) + carry_ref[:, 0:1]
        carry_ref[...] = jnp.zeros_like(carry_ref) + dlogf[:, 0:1]
        dz = dlogf * _sigmoid(-(z_ref[...] + b_ref[...]))
        dz_ref[...] = dz
        db_ref[...] += jnp.sum(dz, axis=1, keepdims=True)

    rev = lambda i: (0, nb - 1 - i)
    return pl.pallas_call(
        body, name=name, grid=(nb,),
        in_specs=[pl.BlockSpec((H, tb), rev), pl.BlockSpec((H, tb), rev), pl.BlockSpec((H, tb), rev),
                  pl.BlockSpec((H, 1), lambda i: (0, 0))],
        out_specs=[pl.BlockSpec((H, tb), rev), pl.BlockSpec((H, LANES), lambda i: (0, 0))],
        out_shape=[jax.ShapeDtypeStruct((H, S), F32), jax.ShapeDtypeStruct((H, LANES), F32)],
        scratch_shapes=[pltpu.VMEM((H, LANES), F32)],
        compiler_params=_cparams("arbitrary"),
    )(drow, dcol, z_t, bias_col)


def _fox_blocks(S):
    cap = max(LANES, S // 4)
    return (min(FOX_FWD_BLOCKS[0], cap), min(FOX_FWD_BLOCKS[1], cap)), \
           (min(FOX_BWD_BLOCKS[0], cap), min(FOX_BWD_BLOCKS[1], cap))


def _key_bias_blocks(negc, bk):
    H, S = negc.shape
    return negc.reshape(H // 2, 2, S // bk, bk).transpose(0, 2, 1, 3)


def _fox_fwd(proj, negc4, *, q_off, k_off, v_off, bq, bk, name):
    S = proj.shape[0]
    nq, nk = S // bq, S // bk
    npair = FOX_HEADS // 2
    assert bq % bk == 0 or bk % bq == 0
    nmask = max(1, bq // bk)

    gp = FOX_FWD_PAIRS
    gw = gp * LANES
    assert q_off % gw == 0 and k_off % gw == 0 and v_off % gw == 0 and npair % gp == 0

    def body(q_ref, k_ref, v_ref, nc_ref, o_ref, lse_ref):
        i = pl.program_id(1)
        lane = lax.broadcasted_iota(jnp.int32, (1, LANES), 1)
        half = [lane < HEAD_DIM, lane >= HEAD_DIM]
        qh = []
        for g in range(gp):
            q2 = q_ref[:, g * LANES:(g + 1) * LANES] * jnp.asarray(ATT_SCALE, BF16)
            qh += [jnp.where(half[h], q2, jnp.zeros_like(q2)) for h in range(2)]
        row = lax.broadcasted_iota(jnp.int32, (bq, bk), 0)
        col = lax.broadcasted_iota(jnp.int32, (bq, bk), 1)
        rel = row - col
        nfull = (i * bq) // bk

        spare = [HEAD_DIM, 0]
        ones_lane = [lane == spare[h] for h in range(2)]

        def step(j, carry, masked):
            start = pl.multiple_of(j * bk, bk)
            new = []
            for g in range(gp):
                ks = k_ref[pl.ds(start, bk), g * LANES:(g + 1) * LANES]
                vs = v_ref[pl.ds(start, bk), g * LANES:(g + 1) * LANES]
                nb = nc_ref[g, j]
                for h in range(2):
                    m, acc = carry[4 * g + 2 * h:4 * g + 2 * h + 2]
                    vh = jnp.where(half[h], vs, jnp.where(ones_lane[h], jnp.ones_like(vs), jnp.zeros_like(vs)))
                    qs, bias = qh[2 * g + h], nb[h:h + 1, :]

                    def update(m, acc, rows, keys):
                        s = _dot_nt(qs[rows], ks[keys]) + bias[:, keys]
                        if masked:
                            s = jnp.where(rel[rows, keys] >= j * bk - i * bq, s, NEG)
                        m_new = jnp.maximum(m[rows], jnp.max(s, axis=1, keepdims=True))
                        p = jnp.exp(s - m_new).astype(BF16)
                        return m_new, jnp.exp(m[rows] - m_new) * acc[rows] + _dot_nn(p, vh[keys])

                    if masked and bq == bk:
                        top, bot, everything = slice(0, bq // 2), slice(bq // 2, bq), slice(0, bk)
                        m_t, acc_t = update(m, acc, top, top)
                        m_b, acc_b = update(m, acc, bot, everything)
                        new += [jnp.concatenate([m_t, m_b], axis=0), jnp.concatenate([acc_t, acc_b], axis=0)]
                    else:
                        new += list(update(m, acc, slice(0, bq), slice(0, bk)))
            return tuple(new)

        init = (jnp.full((bq, 1), NEG, F32), jnp.zeros((bq, LANES), F32)) * (2 * gp)
        carry = lax.fori_loop(0, nfull, lambda j, c: step(j, c, False), init)
        for t in range(nmask):
            carry = step(nfull + t, carry, True)
        for g in range(gp):
            outs, lses = [], []
            for h in range(2):
                m, acc = carry[4 * g + 2 * h:4 * g + 2 * h + 2]
                l = acc[:, spare[h]:spare[h] + 1]
                outs.append(acc * (1.0 / l))
                lses.append(m + jnp.log(l))
            o_ref[:, g * LANES:(g + 1) * LANES] = jnp.where(half[0], outs[0], outs[1]).astype(BF16)
            lse_ref[g] = jnp.where(half[0], lses[0], lses[1])

    seq = lambda off: pl.BlockSpec((S, gw), lambda hp, i: (0, off // gw + hp))
    return pl.pallas_call(
        body, name=name, grid=(npair // gp, nq),
        in_specs=[pl.BlockSpec((bq, gw), lambda hp, i: (i, q_off // gw + hp)), seq(k_off), seq(v_off),
                  pl.BlockSpec((gp, nk, 2, bk), lambda hp, i: (hp, 0, 0, 0))],
        out_specs=[pl.BlockSpec((bq, gw), lambda hp, i: (i, hp)),
                   pl.BlockSpec((gp, bq, LANES), lambda hp, i: (hp, i, 0))],
        out_shape=[jax.ShapeDtypeStruct((S, FOX_W), BF16), jax.ShapeDtypeStruct((npair, S, LANES), F32)],
        compiler_params=_cparams("parallel", "parallel"),
    )(proj, proj, proj, negc4)


def _fox_bwd(proj, negc4, o, lse, d_o, q_t, do_t, *, q_off, k_off, v_off, bq, bk, name, deps=()):
    S = proj.shape[0]
    nq, nk = S // bq, S // bk
    npair = FOX_HEADS // 2
    assert bq % bk == 0 or bk % bq == 0
    nmask = max(1, bk // bq)

    def body(q_ref, k_ref, v_ref, nc_ref, o_ref, lse_ref, do_ref, qt_ref, dot_ref, *rest):
        dqo_ref, dk_ref, dv_ref, dn_ref, dr_ref, delta_ref, rs_ref, dq_ref = rest[len(deps):]
        j = pl.program_id(1)
        lane = lax.broadcasted_iota(jnp.int32, (1, LANES), 1)
        half = [lane < HEAD_DIM, lane >= HEAD_DIM]
        spare = [HEAD_DIM, 0]
        ones_lane = [lane == spare[h] for h in range(2)]
        srow = lax.broadcasted_iota(jnp.int32, (LANES, 1), 0)
        rhalf = [srow < HEAD_DIM, srow >= HEAD_DIM]
        ones_row = [srow == spare[h] for h in range(2)]
        k2, v2 = k_ref[...], v_ref[...]
        one_k = jnp.ones_like(k2)
        kh = [jnp.where(half[h], k2, jnp.where(ones_lane[h], one_k, jnp.zeros_like(k2))) for h in range(2)]
        nb = nc_ref[0, 0]
        row = lax.broadcasted_iota(jnp.int32, (bq, bk), 0)
        col = lax.broadcasted_iota(jnp.int32, (bq, bk), 1)
        rel = row - col
        i_first = (j * bk) // bq

        @pl.when(j == 0)
        def _():
            dq_ref[...] = jnp.zeros_like(dq_ref)
            rs_ref[...] = jnp.zeros_like(rs_ref)
            for b in range(nq):
                prod = do_ref[b * bq:(b + 1) * bq, :].astype(F32) * o_ref[b * bq:(b + 1) * bq, :].astype(F32)
                d0 = jnp.sum(jnp.where(half[0], prod, 0.0), axis=1, keepdims=True)
                d1 = jnp.sum(jnp.where(half[1], prod, 0.0), axis=1, keepdims=True)
                delta_ref[b * bq:(b + 1) * bq, :] = jnp.where(half[0], d0, d1)

        def step(i, carry, masked, r0=0):
            dkt_a, dkt_b, dvt = carry
            dkts = [dkt_a, dkt_b]
            nr = bq - r0
            start = pl.multiple_of(i * bq + r0, LANES)
            q2 = q_ref[pl.ds(start, nr), :] * jnp.asarray(ATT_SCALE, BF16)
            do2 = do_ref[pl.ds(start, nr), :]
            qt = qt_ref[i][:, r0:] * jnp.asarray(ATT_SCALE, BF16)
            dot = dot_ref[i][:, r0:]
            lse2 = lse_ref[0, pl.ds(start, nr), :]
            del2 = delta_ref[pl.ds(start, nr), :]
            dqf = []
            for h in range(2):
                qm = jnp.where(half[h], q2, jnp.zeros_like(q2))
                dom = jnp.where(half[h], do2, jnp.zeros_like(do2))
                qtm = jnp.where(rhalf[h], qt, jnp.where(ones_row[h], jnp.ones_like(qt), jnp.zeros_like(qt)))
                dotm = jnp.where(rhalf[h], dot, jnp.zeros_like(dot))
                c0 = h * HEAD_DIM
                p = jnp.exp(_dot_nt(qm, k2) + nb[h:h + 1, :] - lse2[:, c0:c0 + 1])
                if masked:
                    p = jnp.where(rel[r0:] >= j * bk - i * bq, p, 0.0)
                dp = _dot_nt(dom, v2)
                dsb = (p * (dp - del2[:, c0:c0 + 1])).astype(BF16)
                dvt = dvt + _dot_nn(dotm, p.astype(BF16))
                dkts[h] = dkts[h] + _dot_nn(qtm, dsb)
                dqf.append(_dot_nn(dsb, kh[h]))
            dq_ref[pl.ds(start, nr), :] += jnp.where(half[0], dqf[0], dqf[1]) * ATT_SCALE
            rs_ref[pl.ds(start, nr), :] += jnp.where(ones_lane[0], dqf[0], jnp.where(ones_lane[1], dqf[1], 0.0))
            return dkts[0], dkts[1], dvt

        zero = jnp.zeros((LANES, bk), F32)
        carry = (zero, zero, zero)
        if bq > bk:
            sp = j % (bq // bk)
            carry = lax.switch(sp, [functools.partial(step, i_first, masked=True, r0=s * bk)
                                    for s in range(bq // bk)], carry)
        else:
            for t in range(nmask):
                carry = step(i_first + t, carry, True)
        dkt_a, dkt_b, dvt = lax.fori_loop(i_first + nmask, nq, lambda i, c: step(i, c, False), carry)
        dk_ref[...] = jnp.where(rhalf[0], dkt_a, dkt_b).T.astype(BF16)
        dv_ref[...] = dvt.T.astype(BF16)
        dn_ref[0, 0] = jnp.concatenate([dkt_a[spare[0]:spare[0] + 1], dkt_b[spare[1]:spare[1] + 1]], axis=0)

        @pl.when(j == nk - 1)
        def _():
            dqo_ref[...] = dq_ref[...].astype(BF16)
            for b in range(nq):
                t = rs_ref[b * bq:(b + 1) * bq, :].T
                dr_ref[0, b] = jnp.concatenate([t[spare[0]:spare[0] + 1], t[spare[1]:spare[1] + 1]], axis=0)

    once = pl.Buffered(1)
    seq = lambda off: pl.BlockSpec((S, LANES), lambda hp, j: (0, off // LANES + hp), pipeline_mode=once)
    blk = lambda off: pl.BlockSpec((bk, LANES), lambda hp, j: (j, off // LANES + hp))
    nc = pl.BlockSpec((1, 1, 2, bk), lambda hp, j: (hp, j, 0, 0))
    tsp = pl.BlockSpec((nq, LANES, bq), lambda hp, j: (0, hp, 0), pipeline_mode=once)
    return pl.pallas_call(
        body, name=name, grid=(npair, nk),
        in_specs=[seq(q_off), blk(k_off), blk(v_off), nc, seq(0),
                  pl.BlockSpec((1, S, LANES), lambda hp, j: (hp, 0, 0), pipeline_mode=once), seq(0),
                  tsp, tsp] + [_ANY] * len(deps),
        out_specs=[pl.BlockSpec((S, LANES), lambda hp, j: (0, hp)), blk(0), blk(0), nc,
                   pl.BlockSpec((1, nq, 2, bq), lambda hp, j: (hp, 0, 0, 0))],
        out_shape=[jax.ShapeDtypeStruct((S, FOX_W), BF16), jax.ShapeDtypeStruct((S, FOX_W), BF16),
                   jax.ShapeDtypeStruct((S, FOX_W), BF16), jax.ShapeDtypeStruct((npair, nk, 2, bk), F32),
                   jax.ShapeDtypeStruct((npair, nq, 2, bq), F32)],
        scratch_shapes=[pltpu.VMEM((S, LANES), F32), pltpu.VMEM((S, LANES), F32), pltpu.VMEM((S, LANES), F32)],
        compiler_params=_cparams("parallel", "arbitrary"),
    )(proj, proj, proj, negc4, o, lse, d_o, q_t, do_t, *deps)


def _exchange(arrs, *, gather, name):
    n = len(arrs)
    npeer = N_DEV - 1

    def body(*refs):
        ins, outs = refs[:n], refs[n:2 * n]
        send_sems, recv_sems, loc_sems = refs[2 * n:]
        x, y, c = lax.axis_index("x"), lax.axis_index("y"), lax.axis_index("c")
        me = 4 * x + 2 * y + c
        peers = []
        for k in range(1, N_DEV):
            px = 1 - x if k & 4 else x
            py = 1 - y if k & 2 else y
            pc = 1 - c if k & 1 else c
            peers.append(((px, py, pc), 4 * px + 2 * py + pc))

        def remote(w, k):
            dev, idx = peers[k]
            src = ins[w] if gather else ins[w].at[idx]
            return pltpu.make_async_remote_copy(
                src_ref=src, dst_ref=outs[w].at[me],
                send_sem=send_sems.at[w * npeer + k], recv_sem=recv_sems.at[w * npeer + k],
                device_id=dev, device_id_type=pl.DeviceIdType.MESH)

        def arrival(w, k):
            dev, idx = peers[k]
            src = ins[w] if gather else ins[w].at[idx]
            return pltpu.make_async_remote_copy(
                src_ref=src, dst_ref=outs[w].at[idx],
                send_sem=send_sems.at[w * npeer + k], recv_sem=recv_sems.at[w * npeer + k],
                device_id=dev, device_id_type=pl.DeviceIdType.MESH)

        local = []
        for w in range(n):
            for k in range(npeer):
                remote(w, k).start()
            cp = pltpu.make_async_copy(ins[w] if gather else ins[w].at[me], outs[w].at[me], loc_sems.at[w])
            cp.start()
            local.append(cp)
        for w in range(n):
            for k in range(npeer):
                arrival(w, k).wait_recv()
        for w in range(n):
            for k in range(npeer):
                remote(w, k).wait_send()
            local[w].wait()

    hbm = pl.BlockSpec(memory_space=pl.ANY)
    out_shape = [jax.ShapeDtypeStruct((N_DEV,) + (a.shape if gather else a.shape[1:]), a.dtype) for a in arrs]
    return pl.pallas_call(
        body, name=name,
        in_specs=[hbm] * n, out_specs=[hbm] * n, out_shape=out_shape,
        scratch_shapes=[pltpu.SemaphoreType.DMA((n * npeer,)), pltpu.SemaphoreType.DMA((n * npeer,)),
                        pltpu.SemaphoreType.DMA((n,))],
        compiler_params=pltpu.CompilerParams(has_side_effects=True),
    )(*arrs)


def _gather_two_level(shard, *, name):
    def body(x_ref, out_ref, send_sems, recv_sems, local_sem):
        x, y, c = lax.axis_index("x"), lax.axis_index("y"), lax.axis_index("c")
        me, sibling = (x, y, c), (x, y, 1 - c)
        chips = [(1 - x, y), (x, 1 - y), (1 - x, 1 - y)]

        def slot(px, py, pc):
            return out_ref.at[4 * px + 2 * py + pc]

        def copy(k, block, to, src=None):
            return pltpu.make_async_remote_copy(
                src_ref=slot(*block) if src is None else src, dst_ref=slot(*block),
                send_sem=send_sems.at[k], recv_sem=recv_sems.at[k],
                device_id=to, device_id_type=pl.DeviceIdType.MESH)

        mine = pltpu.make_async_copy(x_ref, slot(*me), local_sem)
        mine.start()
        first = [copy(0, me, sibling, src=x_ref)]
        first += [copy(1 + j, me, (*chip, c), src=x_ref) for j, chip in enumerate(chips)]
        for cp in first:
            cp.start()
        passed = [copy(4 + j, (*chip, c), sibling) for j, chip in enumerate(chips)]
        for j, chip in enumerate(chips):
            copy(1 + j, (*chip, c), me).wait_recv()
            passed[j].start()
        copy(0, sibling, me).wait_recv()
        for j, chip in enumerate(chips):
            copy(4 + j, (*chip, 1 - c), me).wait_recv()
        for cp in first + passed:
            cp.wait_send()
        mine.wait()

    return pl.pallas_call(
        body, name=name,
        in_specs=[_ANY], out_specs=_ANY,
        out_shape=jax.ShapeDtypeStruct((N_DEV,) + shard.shape, shard.dtype),
        scratch_shapes=[pltpu.SemaphoreType.DMA((N_DEV - 1,)), pltpu.SemaphoreType.DMA((N_DEV - 1,)),
                        pltpu.SemaphoreType.DMA],
        compiler_params=pltpu.CompilerParams(has_side_effects=True),
    )(shard)


_HBM = pl.BlockSpec(memory_space=pltpu.HBM)
_SEM = pl.BlockSpec(memory_space=pltpu.SEMAPHORE)
_EFFECT = pltpu.SideEffectType.DATAFLOW_SIDE_EFFECTING
NPEER = N_DEV - 1


def _peer_table():
    x, y, c = lax.axis_index("x"), lax.axis_index("y"), lax.axis_index("c")
    peers = []
    for k in range(1, N_DEV):
        px = 1 - x if k & 4 else x
        py = 1 - y if k & 2 else y
        pc = 1 - c if k & 1 else c
        peers.append(((px, py, pc), 4 * px + 2 * py + pc))
    return 4 * x + 2 * y + c, peers


def _split_copy(ins, lands, send_sems, recv_sems, gather, me, peers, w, k, arriving):
    dev, idx = peers[k]
    return pltpu.make_async_remote_copy(
        src_ref=ins[w] if gather else ins[w].at[idx],
        dst_ref=lands[w].at[idx if arriving else me],
        send_sem=send_sems.at[w * NPEER + k], recv_sem=recv_sems.at[w * NPEER + k],
        device_id=dev, device_id_type=pl.DeviceIdType.MESH)


def _exchange_start(arrs, *, gather, name, deps=()):
    n = len(arrs)
    land_shapes = [(N_DEV,) + (a.shape if gather else a.shape[1:]) for a in arrs]

    def body(*refs):
        ins, lands = refs[:n], refs[n:2 * n]
        send_sems, recv_sems = refs[2 * n + len(deps)], refs[2 * n + len(deps) + 1]
        token = refs[-1]
        me, peers = _peer_table()
        for w in range(n):
            for k in range(NPEER):
                _split_copy(ins, lands, send_sems, recv_sems, gather, me, peers, w, k, False).start()
        token[...] = jnp.zeros_like(token)

    out_shape = ([pltpu.SemaphoreType.DMA((n * NPEER,)), pltpu.SemaphoreType.DMA((n * NPEER,))]
                 + [pltpu.HBM(a.shape, a.dtype) for a in arrs]
                 + [pltpu.HBM(s, a.dtype) for s, a in zip(land_shapes, arrs)]
                 + [jax.ShapeDtypeStruct((8, LANES), F32)])
    res = pl.pallas_call(
        body, name=name,
        in_specs=[_HBM] * (2 * n) + [_ANY] * len(deps),
        out_specs=[_SEM, _SEM] + [_HBM] * (2 * n) + [pl.BlockSpec(memory_space=pltpu.VMEM)],
        out_shape=out_shape,
        input_output_aliases={i: 2 + i for i in range(2 * n)},
        compiler_params=pltpu.CompilerParams(has_side_effects=_EFFECT),
    )(*[pltpu.with_memory_space_constraint(a, pltpu.HBM) for a in arrs],
      *[pltpu.with_memory_space_constraint(lax.empty(s, a.dtype), pltpu.HBM) for s, a in zip(land_shapes, arrs)],
      *deps)
    return (n, gather, res[0], res[1], res[2:2 + n], res[2 + n:2 + 2 * n]), res[-1]


def _exchange_wait(handle, after, *, name):
    n, gather, send_sems, recv_sems, ins_thru, lands_thru = handle

    def body(*refs):
        ins, lands = refs[:n], refs[n:2 * n]
        send_s, recv_s = refs[2 * n], refs[2 * n + 1]
        me, peers = _peer_table()
        for w in range(n):
            for k in range(NPEER):
                _split_copy(ins, lands, send_s, recv_s, gather, me, peers, w, k, False).wait_send()
                _split_copy(ins, lands, send_s, recv_s, gather, me, peers, w, k, True).wait_recv()

    res = pl.pallas_call(
        body, name=name,
        in_specs=[_HBM] * (2 * n) + [_SEM, _SEM, pl.BlockSpec(memory_space=pl.ANY)],
        out_specs=[_HBM] * (2 * n),
        out_shape=[pltpu.HBM(a.shape, a.dtype) for a in list(ins_thru) + list(lands_thru)],
        input_output_aliases={i: i for i in range(2 * n)},
        compiler_params=pltpu.CompilerParams(has_side_effects=_EFFECT),
    )(*ins_thru, *lands_thru, send_sems, recv_sems, after)
    return res[:n], res[n:2 * n]


def _ordered_sum(s_ref, own_ref):
    if own_ref is None:
        blocks = [s_ref[q].astype(F32) for q in range(N_DEV)]
    else:
        me = 4 * lax.axis_index("x") + 2 * lax.axis_index("y") + lax.axis_index("c")
        own = own_ref[...]
        blocks = [jnp.where(me == q, own, s_ref[q]).astype(F32) for q in range(N_DEV)]
    acc = blocks[0]
    for b in blocks[1:]:
        acc = acc + b
    return acc


def _sum8(stack, own, *, name):
    _, R, C = stack.shape
    if R % 8 == 0:
        tr, tc = _pick(R, max(8, STEP_BYTES // (C * 4 * (N_DEV + 2))), 8), C
    else:
        tr, tc = R, _pick(C, max(LANES, STEP_BYTES // (R * 4 * (N_DEV + 2))))

    def body(s_ref, own_ref, o_ref):
        o_ref[...] = _ordered_sum(s_ref, own_ref)

    blk = pl.BlockSpec((tr, tc), lambda i, j: (i, j))
    return pl.pallas_call(
        body, name=name, grid=(R // tr, C // tc),
        in_specs=[pl.BlockSpec((N_DEV, tr, tc), lambda i, j: (0, i, j)), blk],
        out_specs=blk,
        out_shape=jax.ShapeDtypeStruct((R, C), F32),
        compiler_params=_cparams("parallel", "parallel"),
    )(stack, own)


def _adamw_math(w, g, m, v):
    m = ADAM_B1 * m + (1.0 - ADAM_B1) * g
    v = ADAM_B2 * v + (1.0 - ADAM_B2) * (g * g)
    m_hat = m / (1.0 - ADAM_B1 ** ADAM_STEP)
    v_hat = v / (1.0 - ADAM_B2 ** ADAM_STEP)
    delta = -ADAM_LR * (m_hat / (jnp.sqrt(v_hat) + ADAM_EPS) + ADAM_WD * w)
    return delta, m, v


def _adamw(w, g, m, v, *, name, stacked, own=None, transposed=False):
    R, C = w.shape
    if transposed:
        tr = _pick(R, max(LANES, STEP_BYTES // (C * 4 * (9 + N_DEV))))
    else:
        tr = _pick(R, max(8, STEP_BYTES // (C * 4 * (8 + (N_DEV if stacked else 1)))), 8)
    has_own = own is not None

    def body(w_ref, g_ref, m_ref, v_ref, *rest):
        go_ref, d_ref, mo_ref, vo_ref = rest[-4:]
        g = _ordered_sum(g_ref, rest[0] if has_own else None) if stacked else g_ref[...]
        if transposed:
            g = g.T
        delta, m2, v2 = _adamw_math(w_ref[...], g, m_ref[...], v_ref[...])
        go_ref[...] = g
        d_ref[...] = delta
        mo_ref[...] = m2
        vo_ref[...] = v2

    row = pl.BlockSpec((tr, C), lambda i: (i, 0))
    if transposed:
        g_spec, own_spec = pl.BlockSpec((N_DEV, C, tr), lambda i: (0, 0, i)), pl.BlockSpec((C, tr), lambda i: (0, i))
    else:
        g_spec, own_spec = (pl.BlockSpec((N_DEV, tr, C), lambda i: (0, i, 0)) if stacked else row), row
    return pl.pallas_call(
        body, name=name, grid=(R // tr,),
        in_specs=[row, g_spec, row, row] + [own_spec] * has_own, out_specs=[row] * 4,
        out_shape=[jax.ShapeDtypeStruct((R, C), F32)] * 4,
        compiler_params=_cparams("parallel"),
    )(w, g, m, v, *([own] if has_own else []))


def kernel(x, positions, attn_norm, w_in, fox_f_bias, swa_sinks, w_branch_swa, w_branch_fox, w_out, mlp_norm, w_up, w_down, final_norm, loss_target, m_attn_norm, m_w_in, m_fox_f_bias, m_swa_sinks, m_w_branch_swa, m_w_branch_fox, m_w_out, m_mlp_norm, m_w_up, m_w_down, m_final_norm, v_attn_norm, v_w_in, v_fox_f_bias, v_swa_sinks, v_w_branch_swa, v_w_branch_fox, v_w_out, v_mlp_norm, v_w_up, v_w_down, v_final_norm):
    S, D = x.shape[1], x.shape[2]
    DFF = w_up.shape[2] * N_DEV
    d_in = w_in.shape[2] * N_DEV
    assert d_in == QKV_W + FOX_HEADS + 2 * D and (2 * D) % SWA_Q_W == 0 and S % (4 * LANES) == 0
    q_off = 2 * D
    k_off = q_off + SWA_Q_W
    v_off = k_off + SWA_KV_W
    fq_off = v_off + SWA_KV_W
    fk_off = fq_off + FOX_W
    fv_off = fk_off + FOX_W
    fl_off = fv_off + FOX_W
    NP = fl_off + FL_PAD
    x2d, tgt = x[0], loss_target[0]

    shards = [w_in[0].T.astype(BF16), w_branch_swa[0].T.astype(BF16), w_branch_fox[0].T.astype(BF16),
              w_out[0].astype(BF16), w_up[0].T.astype(BF16), w_down[0].astype(BF16)]
    me = 4 * lax.axis_index("x") + 2 * lax.axis_index("y") + lax.axis_index("c")

    def filled(stack, own):
        return lax.dynamic_update_slice(stack, own[None], (me,) + (0,) * own.ndim)

    g_in = _gather_two_level(shards[0], name="gather_w_in")
    h_rest, tok_rest = _exchange_start(shards[1:], gather=True, name="gather_rest_start", deps=[g_in])

    tm = _pick(S, 1024)
    td = _pick(D, 1024)
    tf = _pick(DFF, 1024)
    tnp = _pick(NP, 1024)

    h1 = _rms_fwd(x2d, attn_norm, name="rms1", deps=[tok_rest])
    w_in_t = g_in.reshape(d_in, D)
    w_in_p = jnp.concatenate([w_in_t[QKV_W + FOX_HEADS:], w_in_t[:QKV_W], w_in_t[QKV_W:QKV_W + FOX_HEADS],
                              jnp.zeros((FL_PAD - FOX_HEADS, D), BF16)], axis=0)
    w_fl_t = w_in_t[QKV_W:QKV_W + FOX_HEADS]
    proj, = _matmul(h1, w_in_p, mode="nt", name="mm_in", out_dtypes=[BF16], tm=_pick(S, 2048), tn=tnp, tk=D)
    z_sd, = _matmul(h1, w_fl_t, mode="nt", name="mm_flogit", out_dtypes=[F32], tm=tm, tn=FOX_HEADS, tk=D)
    z_t = z_sd.T
    bias_col = fox_f_bias.reshape(FOX_HEADS, 1)
    negc = _fox_prep(z_t, bias_col, name="fox_prep")
    (fbq, fbk), (bbq, bbk) = _fox_blocks(S)
    inv_freq = ROPE_THETA ** (-jnp.arange(0, HEAD_DIM, 2, dtype=F32) / HEAD_DIM)
    invf = jnp.tile(inv_freq, LANES // (HEAD_DIM // 2)).reshape(1, LANES)
    cos_t, sin_t = _rope_tables(positions.reshape(S, 1), invf, name="rope_tables")
    q_rope, k_rope = _rope_fwd(proj, cos_t, sin_t, q_off=q_off, k_off=k_off, name="rope_fwd")
    sinks = swa_sinks.reshape(-1)
    swa_mask = _swa_mask_bias()
    o_a = _swa_fwd(q_rope, k_rope, proj, sinks, swa_mask, v_off=v_off, name="swa_fwd")
    o_b, lse = _fox_fwd(proj, _key_bias_blocks(negc, fbk), q_off=fq_off, k_off=fk_off, v_off=fv_off,
                        bq=fbq, bk=fbk, name="fox_fwd")
    s_rest, g_rest = _exchange_wait(h_rest, o_b, name="gather_rest_wait")
    g_bs, g_bf, g_o, g_up, g_dn = [filled(g, s) for g, s in zip(g_rest, s_rest)]
    w_bs_t = g_bs.reshape(D, SWA_Q_W)
    w_bf_t = g_bf.reshape(D, FOX_W)
    w_o = g_o.reshape(D, D)
    w_up_t = g_up.reshape(DFF, D)
    w_dn = g_dn.reshape(DFF, D)
    ya, = _matmul(o_a, w_bs_t, mode="nt", name="mm_branch_swa", out_dtypes=[BF16], tm=tm, tn=td, tk=SWA_Q_W)
    gate_maps = [lambda i, j, k: (i, j), lambda i, j, k: (i, j), lambda i, j, k: (i, j + D // td)]

    def merge_epi(acc, ya_t, ga_t, gb_t):
        merged = _sigmoid(ga_t.astype(F32)) * ya_t.astype(F32) + _sigmoid(gb_t.astype(F32)) * acc
        return acc, merged

    yb, merged = _matmul(o_b, w_bf_t, mode="nt", name="mm_branch_fox", out_dtypes=[BF16, BF16],
                         tm=tm, tn=td, tk=FOX_W, extras=[ya, proj, proj], extra_maps=gate_maps,
                         epilogue=merge_epi)
    def out_epi(acc, r, g):
        xm = acc + r
        rr = lax.rsqrt(jnp.mean(xm * xm, axis=-1, keepdims=True) + RMS_EPS)
        return xm, xm * rr * g

    x_mid, h2 = _matmul(merged, w_o, mode="nn", name="mm_out", out_dtypes=[F32, BF16], tm=_pick(S, 512), tn=D, tk=D,
                        extras=[x2d, mlp_norm], extra_maps=[lambda i, j, k: (i, j), lambda i, j, k: (0, 0)],
                        extra_shapes=[None, (1, D)], epilogue=out_epi)
    u, = _matmul(h2, w_up_t, mode="nt", name="mm_up", out_dtypes=[BF16], tm=_pick(S, 2048), tn=tf, tk=D,
                 epilogue=lambda acc: (jnp.maximum(acc, 0.0),))
    x_fin, = _matmul(u, w_dn, mode="nn", name="mm_down", out_dtypes=[F32], tm=tm, tn=td, tk=_pick(DFF, 2048),
                     a_fn=_square_bf16, extras=[x_mid], epilogue=lambda acc, r: (acc + r,))

    dx3b, dg3, loss_part = _loss_head(x_fin, tgt, final_norm.reshape(1, D), name="loss_head")
    d_up, = _matmul(dx3b, w_dn, mode="nt", name="mm_d_act", out_dtypes=[BF16], tm=_pick(S, 2048), tn=tf, tk=D,
                    extras=[u], epilogue=lambda acc, ut: (acc * (2.0 * ut.astype(F32)),))
    tks = _pick(S, 2048)
    dw_dn, = _matmul(u, dx3b, mode="tn", name="mm_dw_down", out_dtypes=[BF16], tm=tf, tn=td, tk=tks,
                     a_fn=_square_bf16)
    dh2, = _matmul(d_up, w_up_t, mode="nn", name="mm_dh2", out_dtypes=[BF16], tm=_pick(S, 512), tn=D,
                   tk=_pick(DFF, 2048))
    dw_up_t, = _matmul(d_up, h2, mode="tn", name="mm_dw_up", out_dtypes=[BF16], tm=tf, tn=td, tk=tks)
    h_s1, tok_s1 = _exchange_start([dw_up_t.reshape(N_DEV, DFF // N_DEV, D), dw_dn.reshape(N_DEV, DFF // N_DEV, D)],
                                   gather=False, name="scatter_mlp_start")
    dx2b, dg2 = _rms_bwd(dh2, x_mid, mlp_norm, dx3b, name="rms2_bwd", out_dtype=BF16, deps=[tok_s1])

    def gate_bwd_epi(dm, ya_t, yb_t, ga_t, gb_t):
        sa, sb = _sigmoid(ga_t.astype(F32)), _sigmoid(gb_t.astype(F32))
        return (dm * sa, dm * sb, dm * ya_t.astype(F32) * sa * (1.0 - sa), dm * yb_t.astype(F32) * sb * (1.0 - sb))

    gmaps = [lambda i, j, k: (i, j), lambda i, j, k: (i, j), lambda i, j, k: (i, j),
             lambda i, j, k: (i, j + D // td)]
    d_ya, d_yb, d_ga, d_gb = _matmul(dx2b, w_o, mode="nt", name="mm_d_merged", out_dtypes=[BF16] * 4,
                                     tm=tm, tn=td, tk=D, extras=[ya, yb, proj, proj], extra_maps=gmaps,
                                     epilogue=gate_bwd_epi)
    dw_o, = _matmul(merged, dx2b, mode="tn", name="mm_dw_out", out_dtypes=[BF16], tm=td, tn=td, tk=tks)
    d_oa, = _matmul(d_ya, w_bs_t, mode="nn", name="mm_d_oa", out_dtypes=[BF16], tm=tm, tn=SWA_Q_W, tk=D)
    d_ob, = _matmul(d_yb, w_bf_t, mode="nn", name="mm_d_ob", out_dtypes=[BF16], tm=tm, tn=FOX_W, tk=D)
    dw_bs_t, = _matmul(d_ya, o_a, mode="tn", name="mm_dw_bs", out_dtypes=[BF16], tm=td, tn=SWA_Q_W, tk=tks)
    dw_bf_t, = _matmul(d_yb, o_b, mode="tn", name="mm_dw_bf", out_dtypes=[BF16], tm=td, tn=FOX_W, tk=tks)
    h_s2, tok_s2 = _exchange_start([dw_bs_t.reshape(N_DEV, D // N_DEV, SWA_Q_W),
                                    dw_bf_t.reshape(N_DEV, D // N_DEV, FOX_W), dw_o.reshape(N_DEV, D // N_DEV, D)],
                                   gather=False, name="scatter_attn_start")
    def row_blocks_t(a):
        return a.reshape(S // bbq, bbq, FOX_W).transpose(0, 2, 1)

    d_fq, d_fk, d_fv, dcol4, drow4 = _fox_bwd(proj, _key_bias_blocks(negc, bbk), o_b, lse, d_ob,
                                              row_blocks_t(proj[:, fq_off:fq_off + FOX_W]), row_blocks_t(d_ob),
                                              q_off=fq_off, k_off=fk_off, v_off=fv_off, bq=bbq, bk=bbk,
                                              name="fox_bwd", deps=[tok_s2])
    dcol = dcol4.transpose(0, 2, 1, 3).reshape(FOX_HEADS, S)
    drow = drow4.transpose(0, 2, 1, 3).reshape(FOX_HEADS, S)
    dz_t, dbias_l = _fox_post(drow, dcol, z_t, bias_col, name="fox_post")
    d_aq, dk_c, dk_p, dv_c, dv_p, dsink_l = _swa_bwd(q_rope, k_rope, proj, sinks, d_oa, cos_t, sin_t, swa_mask,
                                                     v_off=v_off, name="swa_bwd")
    d_ak, d_av = _rope_bwd(dk_c, dk_p, dv_c, dv_p, cos_t, sin_t, name="rope_bwd")
    dz_pad = jnp.pad(dz_t.T.astype(BF16), ((0, 0), (0, FL_PAD - FOX_HEADS)))
    d_proj = jnp.concatenate([d_ga, d_gb, d_aq, d_ak, d_av, d_fq, d_fk, d_fv, dz_pad], axis=1)
    tkp = _pick(NP, 2304)
    dw_in_p, = _matmul(d_proj, h1, mode="tn", name="mm_dw_in", out_dtypes=[BF16], tm=_pick(NP, 512), tn=D, tk=tks)
    dw_in_t = jnp.concatenate([dw_in_p[q_off:q_off + QKV_W], dw_in_p[fl_off:fl_off + FOX_HEADS], dw_in_p[:q_off]],
                              axis=0)
    h_s3, tok_s3 = _exchange_start([dw_in_t.reshape(N_DEV, d_in // N_DEV, D)], gather=False,
                                   name="scatter_in_start")
    dh1, = _matmul(d_proj, w_in_p, mode="nn", name="mm_dh1", out_dtypes=[BF16], tm=tm, tn=td, tk=tkp, deps=[tok_s3])
    dx, dg1 = _rms_bwd(dh1, x2d, attn_norm, dx2b, name="rms1_bwd", out_dtype=F32)

    dbias = dbias_l[:, 0]
    dsinks = dsink_l[:, :, 0].reshape(-1)
    nsm = 3 * D + 2 * LANES
    tail = jnp.zeros((2 * LANES,), F32)
    small_g = jnp.concatenate([dg1[0], dg2[0], dg3[0],
                               tail.at[0:16].set(dbias).at[16:32].set(dsinks).at[32].set(loss_part[0, 0])])

    def pack(a_norm, b_norm, f_norm, bias, snk):
        return jnp.concatenate([a_norm[0], b_norm[0], f_norm,
                                tail.at[0:16].set(bias[0]).at[16:32].set(snk[0])]).reshape(1, nsm)

    small_stack, = _exchange([small_g.reshape(1, nsm)], gather=True, name="gather_small")
    u_sm = _adamw(pack(attn_norm, mlp_norm, final_norm, fox_f_bias, swa_sinks), small_stack,
                  pack(m_attn_norm, m_mlp_norm, m_final_norm, m_fox_f_bias, m_swa_sinks),
                  pack(v_attn_norm, v_mlp_norm, v_final_norm, v_fox_f_bias, v_swa_sinks),
                  name="adamw_small", stacked=True)
    loss = u_sm[0][0, 3 * D + 32]

    def own_of(src):
        return lax.dynamic_index_in_dim(src, me, 0, keepdims=False)

    def update_t(stack, src, w, m, v, nm):
        g = _sum8(stack, own_of(src), name="sum_" + nm).T
        return _adamw(w[0], g, m[0], v[0], name="adamw_" + nm, stacked=False)

    def update(stack, src, w, m, v, nm, transposed=False):
        return _adamw(w[0], stack, m[0], v[0], name="adamw_" + nm, stacked=True, own=own_of(src),
                      transposed=transposed)

    (s_up, s_dn), (r_up, r_dn) = _exchange_wait(h_s1, u_sm[1], name="scatter_mlp_wait")
    u_up = update(r_up, s_up, w_up, m_w_up, v_w_up, "w_up", transposed=True)
    u_dn = update(r_dn, s_dn, w_down, m_w_down, v_w_down, "w_down")
    (s_bs, s_bf, s_o), (r_bs, r_bf, r_o) = _exchange_wait(h_s2, u_dn[1], name="scatter_attn_wait")
    u_bs = update(r_bs, s_bs, w_branch_swa, m_w_branch_swa, v_w_branch_swa, "w_bs", transposed=True)
    u_bf = update(r_bf, s_bf, w_branch_fox, m_w_branch_fox, v_w_branch_fox, "w_bf", transposed=True)
    u_o = update(r_o, s_o, w_out, m_w_out, v_w_out, "w_out")
    (s_w_in,), (r_in,) = _exchange_wait(h_s3, u_o[1], name="scatter_in_wait")
    u_in = update_t(r_in, s_w_in, w_in, m_w_in, v_w_in, "w_in")

    def small(kind):
        a = u_sm[kind][0]
        return dict(attn_norm=a[0:D][None], mlp_norm=a[D:2 * D][None], final_norm=a[2 * D:3 * D],
                    fox_f_bias=a[3 * D:3 * D + 16][None], swa_sinks=a[3 * D + 16:3 * D + 32][None])

    big = dict(w_in=u_in, w_branch_swa=u_bs, w_branch_fox=u_bf, w_out=u_o, w_up=u_up, w_down=u_dn)
    order = ["attn_norm", "w_in", "fox_f_bias", "swa_sinks", "w_branch_swa", "w_branch_fox", "w_out", "mlp_norm",
             "w_up", "w_down", "final_norm"]
    outs = [loss, dx[None]]
    for kind in range(4):
        sm = small(kind)
        for nm in order:
            outs.append(big[nm][kind][None] if nm in big else sm[nm])
    return tuple(outs)
```

```python
import functools

import jax
import jax.numpy as jnp
from jax import lax
from jax.experimental import pallas as pl
from jax.experimental.pallas import tpu as pltpu

F32 = jnp.float32
BF16 = jnp.bfloat16

N_DEV = 8
HEAD_DIM = 64
SWA_Q_W = 1024
SWA_KV_W = 128
SWA_GROUP = 8
WINDOW = 128
FOX_W = 1024
FOX_HEADS = 16
QKV_W = SWA_Q_W + 2 * SWA_KV_W + 3 * FOX_W
FL_PAD = 256
ROPE_THETA = 10000.0
RMS_EPS = 1e-6
ATT_SCALE = 0.125
NEG = -1e30

ADAM_LR = 0.001
ADAM_B1 = 0.9
ADAM_B2 = 0.999
ADAM_EPS = 1e-08
ADAM_WD = 0.01
ADAM_STEP = 10

FOX_FWD_BLOCKS = (1024, 1024)
FOX_BWD_BLOCKS = (1024, 512)
FOX_FWD_PAIRS = 2

LANES = 128
VMEM_LIMIT = 56 * 1024 * 1024
STEP_BYTES = 12 * 1024 * 1024


def _cparams(*sem):
    return pltpu.CompilerParams(dimension_semantics=sem, vmem_limit_bytes=VMEM_LIMIT)


def _pick(dim, pref, align=LANES):
    best = None
    t = align
    while t <= min(dim, pref):
        if dim % t == 0:
            best = t
        t += align
    return best if best is not None else dim


_DIMS = {"nn": ((1,), (0,)), "nt": ((1,), (1,)), "tn": ((0,), (0,))}


_ANY = pl.BlockSpec(memory_space=pl.ANY)


def _matmul(a, b, *, mode, name, out_dtypes, tm, tn, tk, extras=(), extra_maps=None, extra_shapes=None,
            a_fn=None, epilogue=None, deps=()):
    if mode == "nn":
        (M, K), (K2, N) = a.shape, b.shape
    elif mode == "nt":
        (M, K), (N, K2) = a.shape, b.shape
    else:
        (K, M), (K2, N) = a.shape, b.shape
    assert K == K2, (name, a.shape, b.shape)
    assert M % tm == 0 and N % tn == 0 and K % tk == 0, (name, M, N, K, tm, tn, tk)
    nk = K // tk
    ne, no = len(extras), len(out_dtypes)
    dims = (_DIMS[mode], ((), ()))

    def body(*refs):
        a_ref, b_ref = refs[0], refs[1]
        ex_refs = refs[2:2 + ne]
        out_refs = refs[2 + ne + len(deps):2 + ne + len(deps) + no]

        def finish(acc):
            res = (acc,) if epilogue is None else epilogue(acc, *[e[...] for e in ex_refs])
            for o_ref, r in zip(out_refs, res):
                o_ref[...] = r.astype(o_ref.dtype)

        def product():
            av = a_ref[...]
            if a_fn is not None:
                av = a_fn(av)
            return lax.dot_general(av, b_ref[...], dims, preferred_element_type=F32)

        if nk == 1:
            finish(product())
        else:
            acc_ref = refs[-1]
            k = pl.program_id(2)

            @pl.when(k == 0)
            def _():
                acc_ref[...] = jnp.zeros_like(acc_ref)

            acc_ref[...] += product()

            @pl.when(k == nk - 1)
            def _():
                finish(acc_ref[...])

    if mode == "tn":
        a_spec = pl.BlockSpec((tk, tm), lambda i, j, k: (k, i))
    else:
        a_spec = pl.BlockSpec((tm, tk), lambda i, j, k: (i, k))
    if mode == "nt":
        b_spec = pl.BlockSpec((tn, tk), lambda i, j, k: (j, k))
    else:
        b_spec = pl.BlockSpec((tk, tn), lambda i, j, k: (k, j))
    if extra_maps is None:
        extra_maps = [lambda i, j, k: (i, j)] * ne
    if extra_shapes is None:
        extra_shapes = [None] * ne
    ex_specs = [pl.BlockSpec(s or (tm, tn), m) for s, m in zip(extra_shapes, extra_maps)]
    out_spec = [pl.BlockSpec((tm, tn), lambda i, j, k: (i, j)) for _ in range(no)]
    res = pl.pallas_call(
        body,
        name=name,
        grid=(M // tm, N // tn, nk),
        in_specs=[a_spec, b_spec] + ex_specs + [_ANY] * len(deps),
        out_specs=out_spec,
        out_shape=[jax.ShapeDtypeStruct((M, N), d) for d in out_dtypes],
        scratch_shapes=[pltpu.VMEM((tm, tn), F32)] if nk > 1 else [],
        compiler_params=_cparams("parallel", "parallel", "arbitrary"),
    )(a, b, *extras, *deps)
    return res


def _square_bf16(t):
    tf = t.astype(F32)
    return (tf * tf).astype(BF16)


def _sigmoid(g):
    return 1.0 / (1.0 + jnp.exp(-g))


def _rms_fwd(x, gain, *, name, deps=()):
    S, D = x.shape
    tr = _pick(S, 512, 8)

    def body(x_ref, g_ref, *rest):
        h_ref = rest[-1]
        xv = x_ref[...]
        r = lax.rsqrt(jnp.mean(xv * xv, axis=-1, keepdims=True) + RMS_EPS)
        h_ref[...] = (xv * r * g_ref[...]).astype(BF16)

    return pl.pallas_call(
        body, name=name, grid=(S // tr,),
        in_specs=[pl.BlockSpec((tr, D), lambda i: (i, 0)), pl.BlockSpec((1, D), lambda i: (0, 0))] + [_ANY] * len(deps),
        out_specs=pl.BlockSpec((tr, D), lambda i: (i, 0)),
        out_shape=jax.ShapeDtypeStruct((S, D), BF16),
        compiler_params=_cparams("parallel"),
    )(x, gain, *deps)


def _rms_bwd(dh, x, gain, dres, *, name, out_dtype, deps=()):
    S, D = x.shape
    tr = _pick(S, 512, 8)

    def body(dh_ref, x_ref, g_ref, dres_ref, *rest):
        outs = rest[len(deps):]
        dx_ref, dg_ref = outs[0], outs[-1]
        xv = x_ref[...]
        r = lax.rsqrt(jnp.mean(xv * xv, axis=-1, keepdims=True) + RMS_EPS)
        xh = xv * r
        dhv = dh_ref[...].astype(F32)
        t = dhv * g_ref[...]
        dx = r * (t - xh * jnp.mean(t * xh, axis=-1, keepdims=True)) + dres_ref[...].astype(F32)
        dx_ref[...] = dx.astype(out_dtype)
        part = jnp.sum(dhv * xh, axis=0, keepdims=True)

        @pl.when(pl.program_id(0) == 0)
        def _():
            dg_ref[...] = part

        @pl.when(pl.program_id(0) > 0)
        def _():
            dg_ref[...] += part

    row = pl.BlockSpec((tr, D), lambda i: (i, 0))
    vec = pl.BlockSpec((1, D), lambda i: (0, 0))
    return pl.pallas_call(
        body, name=name, grid=(S // tr,),
        in_specs=[row, row, vec, row] + [_ANY] * len(deps), out_specs=[row, vec],
        out_shape=[jax.ShapeDtypeStruct((S, D), out_dtype), jax.ShapeDtypeStruct((1, D), F32)],
        compiler_params=_cparams("arbitrary"),
    )(dh, x, gain, dres, *deps)


def _loss_head(x3, target, gain, *, name):
    S, D = x3.shape
    tr = _pick(S, 512, 8)

    def body(x_ref, t_ref, g_ref, dxb_ref, dg_ref, loss_ref):
        xv = x_ref[...]
        r = lax.rsqrt(jnp.mean(xv * xv, axis=-1, keepdims=True) + RMS_EPS)
        xh = xv * r
        gv = g_ref[...]
        err = xh * gv - t_ref[...]
        lpart = jnp.zeros((1, LANES), F32) + (0.5 / D) * jnp.sum(err * err)
        dy = err * (1.0 / D)
        t = dy * gv
        dx = r * (t - xh * jnp.mean(t * xh, axis=-1, keepdims=True))
        dxb_ref[...] = dx.astype(BF16)
        part = jnp.sum(dy * xh, axis=0, keepdims=True)

        @pl.when(pl.program_id(0) == 0)
        def _():
            dg_ref[...] = part
            loss_ref[...] = lpart

        @pl.when(pl.program_id(0) > 0)
        def _():
            dg_ref[...] += part
            loss_ref[...] += lpart

    row = pl.BlockSpec((tr, D), lambda i: (i, 0))
    vec = pl.BlockSpec((1, D), lambda i: (0, 0))
    return pl.pallas_call(
        body, name=name, grid=(S // tr,),
        in_specs=[row, row, vec],
        out_specs=[row, vec, pl.BlockSpec((1, LANES), lambda i: (0, 0))],
        out_shape=[jax.ShapeDtypeStruct((S, D), BF16),
                   jax.ShapeDtypeStruct((1, D), F32), jax.ShapeDtypeStruct((1, LANES), F32)],
        compiler_params=_cparams("arbitrary"),
    )(x3, target, gain)


def _rope_tables(pos_col, invf, *, name):
    S = pos_col.shape[0]
    tr = _pick(S, 512, 8)

    def body(p_ref, f_ref, cos_ref, sin_ref):
        ang = p_ref[...].astype(F32) * f_ref[...]
        lane = lax.broadcasted_iota(jnp.int32, (1, LANES), 1)
        first = (lane % HEAD_DIM) < HEAD_DIM // 2
        sn = jnp.sin(ang)
        cos_ref[...] = jnp.cos(ang)
        sin_ref[...] = jnp.where(first, -sn, sn)

    return pl.pallas_call(
        body, name=name, grid=(S // tr,),
        in_specs=[pl.BlockSpec((tr, 1), lambda i: (i, 0)), pl.BlockSpec((1, LANES), lambda i: (0, 0))],
        out_specs=[pl.BlockSpec((tr, LANES), lambda i: (i, 0))] * 2,
        out_shape=[jax.ShapeDtypeStruct((S, LANES), F32)] * 2,
        compiler_params=_cparams("parallel"),
    )(pos_col, invf)


def _swap_halves(t):
    lane = lax.broadcasted_iota(jnp.int32, (1, LANES), 1)
    first = (lane % HEAD_DIM) < HEAD_DIM // 2
    return jnp.where(first, pltpu.roll(t, LANES - HEAD_DIM // 2, 1), pltpu.roll(t, HEAD_DIM // 2, 1))


def _rope_fwd(proj, cos_t, sin_t, *, q_off, k_off, name):
    S = proj.shape[0]
    tr = _pick(S, 512, 8)
    nqb = SWA_Q_W // LANES

    def body(q_ref, k_ref, c_ref, s_ref, qo_ref, ko_ref):
        cv, sv = c_ref[...], s_ref[...]
        for b in range(nqb):
            t = q_ref[:, b * LANES:(b + 1) * LANES].astype(F32)
            qo_ref[:, b * LANES:(b + 1) * LANES] = (t * cv + _swap_halves(t) * sv).astype(BF16)
        t = k_ref[...].astype(F32)
        ko_ref[...] = (t * cv + _swap_halves(t) * sv).astype(BF16)

    tab = pl.BlockSpec((tr, LANES), lambda i: (i, 0))
    return pl.pallas_call(
        body, name=name, grid=(S // tr,),
        in_specs=[pl.BlockSpec((tr, SWA_Q_W), lambda i: (i, q_off // SWA_Q_W)),
                  pl.BlockSpec((tr, LANES), lambda i: (i, k_off // LANES)), tab, tab],
        out_specs=[pl.BlockSpec((tr, SWA_Q_W), lambda i: (i, 0)), tab],
        out_shape=[jax.ShapeDtypeStruct((S, SWA_Q_W), BF16), jax.ShapeDtypeStruct((S, LANES), BF16)],
        compiler_params=_cparams("parallel"),
    )(proj, proj, cos_t, sin_t)


def _rope_bwd(dk_cur, dk_prev, dv_cur, dv_prev, cos_t, sin_t, *, name):
    S = dk_cur.shape[1]
    tr = _pick(S, 512)
    nb = S // tr

    def body(kc_ref, kp_ref, vc_ref, vp_ref, c_ref, s_ref, dko_ref, dvo_ref):
        cv, sv = c_ref[...], s_ref[...]
        row = pl.program_id(0) * tr + lax.broadcasted_iota(jnp.int32, (tr, 1), 0)
        has_next = row < S - WINDOW
        d = kc_ref[0] + kc_ref[1] + jnp.where(has_next, kp_ref[0] + kp_ref[1], 0.0)
        dko_ref[...] = (d * cv + _swap_halves(d * sv)).astype(BF16)
        dvo_ref[...] = (vc_ref[0] + vc_ref[1] + jnp.where(has_next, vp_ref[0] + vp_ref[1], 0.0)).astype(BF16)

    tab = pl.BlockSpec((tr, LANES), lambda i: (i, 0))
    cur = pl.BlockSpec((2, tr, LANES), lambda i: (0, i, 0))
    return pl.pallas_call(
        body, name=name, grid=(nb,),
        in_specs=[cur, cur, cur, cur, tab, tab],
        out_specs=[tab, tab],
        out_shape=[jax.ShapeDtypeStruct((S, LANES), BF16), jax.ShapeDtypeStruct((S, LANES), BF16)],
        compiler_params=_cparams("parallel"),
    )(dk_cur, dk_prev, dv_cur, dv_prev, cos_t, sin_t)


def _dot_nt(a, b):
    return lax.dot_general(a, b, (((1,), (1,)), ((), ())), preferred_element_type=F32)


def _dot_tn(a, b):
    return lax.dot_general(a, b, (((0,), (0,)), ((), ())), preferred_element_type=F32)


def _dot_nn(a, b):
    return lax.dot_general(a, b, (((1,), (0,)), ((), ())), preferred_element_type=F32)


def _roll_half(t):
    return pltpu.roll(t.astype(F32), HEAD_DIM, 1).astype(t.dtype)


SWA_STACK = SWA_GROUP // 2


def _swa_mask_bias():
    rows = SWA_STACK * WINDOW
    row = lax.broadcasted_iota(jnp.int32, (rows, 2 * WINDOW), 0) % WINDOW
    col = lax.broadcasted_iota(jnp.int32, (rows, 2 * WINDOW), 1)
    diff = row + WINDOW - col
    window = (diff >= 0) & (diff < WINDOW)
    return jnp.stack([jnp.where(window & (col >= WINDOW), 0.0, NEG), jnp.where(window, 0.0, NEG)]).astype(F32)


def _swa_common(kp_ref, kc_ref, vp_ref, vc_ref):
    k2 = jnp.concatenate([kp_ref[...], kc_ref[...]], axis=0)
    v2 = jnp.concatenate([vp_ref[...], vc_ref[...]], axis=0)
    k_sw, v_sw = _roll_half(k2), _roll_half(v2)
    lane = lax.broadcasted_iota(jnp.int32, (1, LANES), 1)
    half = [lane < HEAD_DIM, lane >= HEAD_DIM]
    kk = [[k2 if hk == a else k_sw for a in range(2)] for hk in range(2)]
    vv = [[v2 if hk == a else v_sw for a in range(2)] for hk in range(2)]
    return half, kk, vv


def _swa_stack(ref, hk, mask, scale=None):
    parts = []
    for t in range(SWA_STACK):
        blk = ref[:, (hk * SWA_STACK + t) * LANES:(hk * SWA_STACK + t + 1) * LANES]
        if scale is not None:
            blk = blk * jnp.asarray(scale, blk.dtype)
        parts.append(jnp.where(mask, blk, jnp.zeros_like(blk)))
    return jnp.concatenate(parts, axis=0)


def _swa_sink_column(sink_ref, hk, a):
    blk = lax.broadcasted_iota(jnp.int32, (SWA_STACK * WINDOW, 1), 0) // WINDOW
    col = jnp.zeros((SWA_STACK * WINDOW, 1), F32)
    for t in range(SWA_STACK):
        col = jnp.where(blk == t, sink_ref[hk * SWA_GROUP + 2 * t + a], col)
    return col


def _swa_probs(qm, kk, mask_bias, sink):
    s = _dot_nt(qm, kk) + mask_bias
    m = jnp.maximum(jnp.max(s, axis=1, keepdims=True), sink)
    e = jnp.exp(s - m)
    es = jnp.exp(sink - m)
    inv = 1.0 / (jnp.sum(e, axis=1, keepdims=True) + es)
    return e * inv, es * inv


def _swa_mask_spec():
    return pl.BlockSpec((1, SWA_STACK * WINDOW, 2 * WINDOW), lambda n: (jnp.minimum(n, 1), 0, 0))


def _swa_fwd(q_rope, k_rope, proj, sinks, mask_bias, *, v_off, name):
    S = q_rope.shape[0]
    nb = S // WINDOW

    def body(sink_ref, q_ref, kp_ref, kc_ref, vp_ref, vc_ref, mask_ref, o_ref):
        half, kk, vv = _swa_common(kp_ref, kc_ref, vp_ref, vc_ref)
        for hk in range(2):
            outs = []
            for a in range(2):
                qm = _swa_stack(q_ref, hk, half[a], ATT_SCALE)
                p, _ = _swa_probs(qm, kk[hk][a], mask_ref[0], _swa_sink_column(sink_ref, hk, a))
                outs.append(_dot_nn(p.astype(BF16), vv[hk][a]))
            for t in range(SWA_STACK):
                rows = slice(t * WINDOW, (t + 1) * WINDOW)
                c0 = (hk * SWA_STACK + t) * LANES
                o_ref[:, c0:c0 + LANES] = jnp.where(half[0], outs[0][rows], outs[1][rows]).astype(BF16)

    prev = lambda n: (jnp.maximum(n - 1, 0), 0)
    cur = lambda n: (n, 0)
    vprev = lambda n: (jnp.maximum(n - 1, 0), v_off // LANES)
    vcur = lambda n: (n, v_off // LANES)
    blk = lambda m: pl.BlockSpec((WINDOW, LANES), m)
    return pl.pallas_call(
        body, name=name, grid=(nb,),
        in_specs=[pl.BlockSpec(memory_space=pltpu.SMEM),
                  pl.BlockSpec((WINDOW, SWA_Q_W), lambda n: (n, 0)),
                  blk(prev), blk(cur), blk(vprev), blk(vcur), _swa_mask_spec()],
        out_specs=pl.BlockSpec((WINDOW, SWA_Q_W), lambda n: (n, 0)),
        out_shape=jax.ShapeDtypeStruct((S, SWA_Q_W), BF16),
        compiler_params=_cparams("parallel"),
    )(sinks, q_rope, k_rope, k_rope, proj, proj, mask_bias)


def _swa_bwd(q_rope, k_rope, proj, sinks, d_o, cos_t, sin_t, mask_bias, *, v_off, name):
    S = q_rope.shape[0]
    nb = S // WINDOW

    def body(sink_ref, q_ref, kp_ref, kc_ref, vp_ref, vc_ref, do_ref, c_ref, s_ref, mask_ref,
             dq_ref, dkc_ref, dkp_ref, dvc_ref, dvp_ref, dsink_ref):
        n = pl.program_id(0)
        half, kk, vv = _swa_common(kp_ref, kc_ref, vp_ref, vc_ref)
        allowed = mask_ref[0]
        cv, sv = c_ref[...], s_ref[...]
        srow = lax.broadcasted_iota(jnp.int32, (SWA_GROUP, LANES), 0)
        for hk in range(2):
            dk_acc = jnp.zeros((2 * WINDOW, LANES), F32)
            dv_acc = jnp.zeros((2 * WINDOW, LANES), F32)
            dsink = jnp.zeros((SWA_GROUP, LANES), F32)
            dqs = []
            for a in range(2):
                qm = _swa_stack(q_ref, hk, half[a], ATT_SCALE)
                dom = _swa_stack(do_ref, hk, half[a])
                p, psink = _swa_probs(qm, kk[hk][a], allowed, _swa_sink_column(sink_ref, hk, a))
                dp = _dot_nt(dom, vv[hk][a])
                delta = jnp.sum(p * dp, axis=1, keepdims=True)
                ds = (p * (dp - delta)).astype(BF16)
                dsk = psink * delta
                for t in range(SWA_STACK):
                    dsink = dsink + jnp.where(srow == 2 * t + a, -jnp.sum(dsk[t * WINDOW:(t + 1) * WINDOW]), 0.0)
                dqs.append(_dot_nn(ds, kk[hk][a]) * ATT_SCALE)
                dk_acc = dk_acc + _dot_tn(ds, qm)
                dv_acc = dv_acc + _dot_tn(p.astype(BF16), dom)
            for t in range(SWA_STACK):
                rows = slice(t * WINDOW, (t + 1) * WINDOW)
                d = jnp.where(half[0], dqs[0][rows], dqs[1][rows])
                c0 = (hk * SWA_STACK + t) * LANES
                dq_ref[:, c0:c0 + LANES] = (d * cv + _swap_halves(d * sv)).astype(BF16)
            dk_t = jnp.where(half[hk], dk_acc + pltpu.roll(dk_acc, HEAD_DIM, 1), 0.0)
            dv_t = jnp.where(half[hk], dv_acc + pltpu.roll(dv_acc, HEAD_DIM, 1), 0.0)
            dkp_ref[hk] = dk_t[:WINDOW]
            dkc_ref[hk] = dk_t[WINDOW:]
            dvp_ref[hk] = dv_t[:WINDOW]
            dvc_ref[hk] = dv_t[WINDOW:]

            @pl.when(n == 0)
            def _():
                dsink_ref[hk] = dsink

            @pl.when(n > 0)
            def _():
                dsink_ref[hk] += dsink

    prev = lambda n: (jnp.maximum(n - 1, 0), 0)
    cur = lambda n: (n, 0)
    vprev = lambda n: (jnp.maximum(n - 1, 0), v_off // LANES)
    vcur = lambda n: (n, v_off // LANES)
    blk = lambda m: pl.BlockSpec((WINDOW, LANES), m)
    qblk = pl.BlockSpec((WINDOW, SWA_Q_W), lambda n: (n, 0))
    part = pl.BlockSpec((2, WINDOW, LANES), lambda n: (0, n, 0))
    part_prev = pl.BlockSpec((2, WINDOW, LANES), lambda n: (0, jnp.maximum(n - 1, 0), 0))
    part_shape = jax.ShapeDtypeStruct((2, S, LANES), F32)
    return pl.pallas_call(
        body, name=name, grid=(nb,),
        in_specs=[pl.BlockSpec(memory_space=pltpu.SMEM), qblk, blk(prev), blk(cur), blk(vprev), blk(vcur), qblk,
                  blk(cur), blk(cur), _swa_mask_spec()],
        out_specs=[qblk, part, part_prev, part, part_prev,
                   pl.BlockSpec((2, SWA_GROUP, LANES), lambda n: (0, 0, 0))],
        out_shape=[jax.ShapeDtypeStruct((S, SWA_Q_W), BF16), part_shape, part_shape, part_shape, part_shape,
                   jax.ShapeDtypeStruct((2, SWA_GROUP, LANES), F32)],
        compiler_params=_cparams("arbitrary"),
    )(sinks, q_rope, k_rope, k_rope, proj, proj, d_o, cos_t, sin_t, mask_bias)


def _fox_prep(z_t, bias_col, *, name):
    H, S = z_t.shape
    tb = _pick(S, 512)

    def body(z_ref, b_ref, o_ref, carry_ref):
        @pl.when(pl.program_id(0) == 0)
        def _():
            carry_ref[...] = jnp.zeros_like(carry_ref)

        zz = z_ref[...] + b_ref[...]
        t = jnp.exp(-jnp.abs(zz))
        log1p = jnp.where(t < 1e-2, t * (1.0 - t * (0.5 - t * (1.0 / 3.0))), jnp.log(1.0 + t))
        logf = jnp.minimum(zz, 0.0) - log1p
        r = lax.broadcasted_iota(jnp.int32, (tb, tb), 0)
        c = lax.broadcasted_iota(jnp.int32, (tb, tb), 1)
        tri = (r <= c).astype(BF16)
        hi = logf.astype(BF16)
        r1 = logf - hi.astype(F32)
        mid = r1.astype(BF16)
        lo = (r1 - mid.astype(F32)).astype(BF16)
        cs = _dot_nn(hi, tri) + _dot_nn(mid, tri) + _dot_nn(lo, tri) + carry_ref[:, 0:1]
        o_ref[...] = -cs
        carry_ref[...] = jnp.zeros_like(carry_ref) + cs[:, tb - 1:tb]

    return pl.pallas_call(
        body, name=name, grid=(S // tb,),
        in_specs=[pl.BlockSpec((H, tb), lambda i: (0, i)), pl.BlockSpec((H, 1), lambda i: (0, 0))],
        out_specs=pl.BlockSpec((H, tb), lambda i: (0, i)),
        out_shape=jax.ShapeDtypeStruct((H, S), F32),
        scratch_shapes=[pltpu.VMEM((H, LANES), F32)],
        compiler_params=_cparams("arbitrary"),
    )(z_t, bias_col)


def _fox_post(drow, dcol, z_t, bias_col, *, name):
    H, S = z_t.shape
    tb = _pick(S, 512)
    nb = S // tb

    def body(dr_ref, d_ref, z_ref, b_ref, dz_ref, db_ref, carry_ref):
        @pl.when(pl.program_id(0) == 0)
        def _():
            carry_ref[...] = jnp.zeros_like(carry_ref)
            db_ref[...] = jnp.zeros_like(db_ref)

        dc = dr_ref[...] - d_ref[...]
        r = lax.broadcasted_iota(jnp.int32, (tb, tb), 0)
        c = lax.broadcasted_iota(jnp.int32, (tb, tb), 1)
        tri = (r >= c).astype(BF16)
        hi = dc.astype(BF16)
        r1 = dc - hi.astype(F32)
        mid = r1.astype(BF16)
        lo = (r1 - mid.astype(F32)).astype(BF16)
        dlogf = _dot_nn(hi, tri) + _dot_nn(mid, tri) + _dot_nn(lo, tri) + carry_ref[:, 0:1]
        carry_ref[...] = jnp.zeros_like(carry_ref) + dlogf[:, 0:1]
        dz = dlogf * _sigmoid(-(z_ref[...] + b_ref[...]))
        dz_ref[...] = dz
        db_ref[...] += jnp.sum(dz, axis=1, keepdims=True)

    rev = lambda i: (0, nb - 1 - i)
    return pl.pallas_call(
        body, name=name, grid=(nb,),
        in_specs=[pl.BlockSpec((H, tb), rev), pl.BlockSpec((H, tb), rev), pl.BlockSpec((H, tb), rev),
                  pl.BlockSpec((H, 1), lambda i: (0, 0))],
        out_specs=[pl.BlockSpec((H, tb), rev), pl.BlockSpec((H, LANES), lambda i: (0, 0))],
        out_shape=[jax.ShapeDtypeStruct((H, S), F32), jax.ShapeDtypeStruct((H, LANES), F32)],
        scratch_shapes=[pltpu.VMEM((H, LANES), F32)],
        compiler_params=_cparams("arbitrary"),
    )(drow, dcol, z_t, bias_col)


def _fox_blocks(S):
    cap = max(LANES, S // 4)
    return (min(FOX_FWD_BLOCKS[0], cap), min(FOX_FWD_BLOCKS[1], cap)), \
           (min(FOX_BWD_BLOCKS[0], cap), min(FOX_BWD_BLOCKS[1], cap))


def _key_bias_blocks(negc, bk):
    H, S = negc.shape
    return negc.reshape(H // 2, 2, S // bk, bk).transpose(0, 2, 1, 3)


def _fox_fwd(proj, negc4, *, q_off, k_off, v_off, bq, bk, name):
    S = proj.shape[0]
    nq, nk = S // bq, S // bk
    npair = FOX_HEADS // 2
    assert bq % bk == 0 or bk % bq == 0
    nmask = max(1, bq // bk)

    gp = FOX_FWD_PAIRS
    gw = gp * LANES
    assert q_off % gw == 0 and k_off % gw == 0 and v_off % gw == 0 and npair % gp == 0

    def body(q_ref, k_ref, v_ref, nc_ref, o_ref, lse_ref):
        i = pl.program_id(1)
        lane = lax.broadcasted_iota(jnp.int32, (1, LANES), 1)
        half = [lane < HEAD_DIM, lane >= HEAD_DIM]
        qh = []
        for g in range(gp):
            q2 = q_ref[:, g * LANES:(g + 1) * LANES] * jnp.asarray(ATT_SCALE, BF16)
            qh += [jnp.where(half[h], q2, jnp.zeros_like(q2)) for h in range(2)]
        row = lax.broadcasted_iota(jnp.int32, (bq, bk), 0)
        col = lax.broadcasted_iota(jnp.int32, (bq, bk), 1)
        rel = row - col
        nfull = (i * bq) // bk

        spare = [HEAD_DIM, 0]
        ones_lane = [lane == spare[h] for h in range(2)]

        def step(j, carry, masked):
            start = pl.multiple_of(j * bk, bk)
            new = []
            for g in range(gp):
                ks = k_ref[pl.ds(start, bk), g * LANES:(g + 1) * LANES]
                vs = v_ref[pl.ds(start, bk), g * LANES:(g + 1) * LANES]
                nb = nc_ref[g, j]
                for h in range(2):
                    m, acc = carry[4 * g + 2 * h:4 * g + 2 * h + 2]
                    vh = jnp.where(half[h], vs, jnp.where(ones_lane[h], jnp.ones_like(vs), jnp.zeros_like(vs)))
                    qs, bias = qh[2 * g + h], nb[h:h + 1, :]

                    def update(m, acc, rows, keys):
                        s = _dot_nt(qs[rows], ks[keys]) + bias[:, keys]
                        if masked:
                            s = jnp.where(rel[rows, keys] >= j * bk - i * bq, s, NEG)
                        m_new = jnp.maximum(m[rows], jnp.max(s, axis=1, keepdims=True))
                        p = jnp.exp(s - m_new).astype(BF16)
                        return m_new, jnp.exp(m[rows] - m_new) * acc[rows] + _dot_nn(p, vh[keys])

                    if masked and bq == bk:
                        top, bot, everything = slice(0, bq // 2), slice(bq // 2, bq), slice(0, bk)
                        m_t, acc_t = update(m, acc, top, top)
                        m_b, acc_b = update(m, acc, bot, everything)
                        new += [jnp.concatenate([m_t, m_b], axis=0), jnp.concatenate([acc_t, acc_b], axis=0)]
                    else:
                        new += list(update(m, acc, slice(0, bq), slice(0, bk)))
            return tuple(new)

        init = (jnp.full((bq, 1), NEG, F32), jnp.zeros((bq, LANES), F32)) * (2 * gp)
        carry = lax.fori_loop(0, nfull, lambda j, c: step(j, c, False), init)
        for t in range(nmask):
            carry = step(nfull + t, carry, True)
        for g in range(gp):
            outs, lses = [], []
            for h in range(2):
                m, acc = carry[4 * g + 2 * h:4 * g + 2 * h + 2]
                l = acc[:, spare[h]:spare[h] + 1]
                outs.append(acc * (1.0 / l))
                lses.append(m + jnp.log(l))
            o_ref[:, g * LANES:(g + 1) * LANES] = jnp.where(half[0], outs[0], outs[1]).astype(BF16)
            lse_ref[g] = jnp.where(half[0], lses[0], lses[1])

    seq = lambda off: pl.BlockSpec((S, gw), lambda hp, i: (0, off // gw + hp))
    return pl.pallas_call(
        body, name=name, grid=(npair // gp, nq),
        in_specs=[pl.BlockSpec((bq, gw), lambda hp, i: (i, q_off // gw + hp)), seq(k_off), seq(v_off),
                  pl.BlockSpec((gp, nk, 2, bk), lambda hp, i: (hp, 0, 0, 0))],
        out_specs=[pl.BlockSpec((bq, gw), lambda hp, i: (i, hp)),
                   pl.BlockSpec((gp, bq, LANES), lambda hp, i: (hp, i, 0))],
        out_shape=[jax.ShapeDtypeStruct((S, FOX_W), BF16), jax.ShapeDtypeStruct((npair, S, LANES), F32)],
        compiler_params=_cparams("parallel", "parallel"),
    )(proj, proj, proj, negc4)


def _fox_bwd(proj, negc4, o, lse, d_o, q_t, do_t, *, q_off, k_off, v_off, bq, bk, name, deps=()):
    S = proj.shape[0]
    nq, nk = S // bq, S // bk
    npair = FOX_HEADS // 2
    assert bq % bk == 0 or bk % bq == 0
    nmask = max(1, bk // bq)

    def body(q_ref, k_ref, v_ref, nc_ref, o_ref, lse_ref, do_ref, qt_ref, dot_ref, *rest):
        dqo_ref, dk_ref, dv_ref, dn_ref, dr_ref, delta_ref, rs_ref, dq_ref = rest[len(deps):]
        j = pl.program_id(1)
        lane = lax.broadcasted_iota(jnp.int32, (1, LANES), 1)
        half = [lane < HEAD_DIM, lane >= HEAD_DIM]
        spare = [HEAD_DIM, 0]
        ones_lane = [lane == spare[h] for h in range(2)]
        srow = lax.broadcasted_iota(jnp.int32, (LANES, 1), 0)
        rhalf = [srow < HEAD_DIM, srow >= HEAD_DIM]
        ones_row = [srow == spare[h] for h in range(2)]
        k2, v2 = k_ref[...], v_ref[...]
        one_k = jnp.ones_like(k2)
        kh = [jnp.where(half[h], k2, jnp.where(ones_lane[h], one_k, jnp.zeros_like(k2))) for h in range(2)]
        nb = nc_ref[0, 0]
        row = lax.broadcasted_iota(jnp.int32, (bq, bk), 0)
        col = lax.broadcasted_iota(jnp.int32, (bq, bk), 1)
        rel = row - col
        i_first = (j * bk) // bq

        @pl.when(j == 0)
        def _():
            dq_ref[...] = jnp.zeros_like(dq_ref)
            rs_ref[...] = jnp.zeros_like(rs_ref)
            for b in range(nq):
                prod = do_ref[b * bq:(b + 1) * bq, :].astype(F32) * o_ref[b * bq:(b + 1) * bq, :].astype(F32)
                d0 = jnp.sum(jnp.where(half[0], prod, 0.0), axis=1, keepdims=True)
                d1 = jnp.sum(jnp.where(half[1], prod, 0.0), axis=1, keepdims=True)
                delta_ref[b * bq:(b + 1) * bq, :] = jnp.where(half[0], d0, d1)

        def step(i, carry, masked, r0=0):
            dkt_a, dkt_b, dvt = carry
            dkts = [dkt_a, dkt_b]
            nr = bq - r0
            start = pl.multiple_of(i * bq + r0, LANES)
            q2 = q_ref[pl.ds(start, nr), :] * jnp.asarray(ATT_SCALE, BF16)
            do2 = do_ref[pl.ds(start, nr), :]
            qt = qt_ref[i][:, r0:] * jnp.asarray(ATT_SCALE, BF16)
            dot = dot_ref[i][:, r0:]
            lse2 = lse_ref[0, pl.ds(start, nr), :]
            del2 = delta_ref[pl.ds(start, nr), :]
            dqf = []
            for h in range(2):
                qm = jnp.where(half[h], q2, jnp.zeros_like(q2))
                dom = jnp.where(half[h], do2, jnp.zeros_like(do2))
                qtm = jnp.where(rhalf[h], qt, jnp.where(ones_row[h], jnp.ones_like(qt), jnp.zeros_like(qt)))
                dotm = jnp.where(rhalf[h], dot, jnp.zeros_like(dot))
                c0 = h * HEAD_DIM
                p = jnp.exp(_dot_nt(qm, k2) + nb[h:h + 1, :] - lse2[:, c0:c0 + 1])
                if masked:
                    p = jnp.where(rel[r0:] >= j * bk - i * bq, p, 0.0)
                dp = _dot_nt(dom, v2)
                dsb = (p * (dp - del2[:, c0:c0 + 1])).astype(BF16)
                dvt = dvt + _dot_nn(dotm, p.astype(BF16))
                dkts[h] = dkts[h] + _dot_nn(qtm, dsb)
                dqf.append(_dot_nn(dsb, kh[h]))
            dq_ref[pl.ds(start, nr), :] += jnp.where(half[0], dqf[0], dqf[1]) * ATT_SCALE
            rs_ref[pl.ds(start, nr), :] += jnp.where(ones_lane[0], dqf[0], jnp.where(ones_lane[1], dqf[1], 0.0))
            return dkts[0], dkts[1], dvt

        zero = jnp.zeros((LANES, bk), F32)
        carry = (zero, zero, zero)
        if bq > bk:
            sp = j % (bq // bk)
            carry = lax.switch(sp, [functools.partial(step, i_first, masked=True, r0=s * bk)
                                    for s in range(bq // bk)], carry)
        else:
            for t in range(nmask):
                carry = step(i_first + t, carry, True)
        dkt_a, dkt_b, dvt = lax.fori_loop(i_first + nmask, nq, lambda i, c: step(i, c, False), carry)
        dk_ref[...] = jnp.where(rhalf[0], dkt_a, dkt_b).T.astype(BF16)
        dv_ref[...] = dvt.T.astype(BF16)
        dn_ref[0, 0] = jnp.concatenate([dkt_a[spare[0]:spare[0] + 1], dkt_b[spare[1]:spare[1] + 1]], axis=0)

        @pl.when(j == nk - 1)
        def _():
            dqo_ref[...] = dq_ref[...].astype(BF16)
            for b in range(nq):
                t = rs_ref[b * bq:(b + 1) * bq, :].T
                dr_ref[0, b] = jnp.concatenate([t[spare[0]:spare[0] + 1], t[spare[1]:spare[1] + 1]], axis=0)

    once = pl.Buffered(1)
    seq = lambda off: pl.BlockSpec((S, LANES), lambda hp, j: (0, off // LANES + hp), pipeline_mode=once)
    blk = lambda off: pl.BlockSpec((bk, LANES), lambda hp, j: (j, off // LANES + hp))
    nc = pl.BlockSpec((1, 1, 2, bk), lambda hp, j: (hp, j, 0, 0))
    tsp = pl.BlockSpec((nq, LANES, bq), lambda hp, j: (0, hp, 0), pipeline_mode=once)
    return pl.pallas_call(
        body, name=name, grid=(npair, nk),
        in_specs=[seq(q_off), blk(k_off), blk(v_off), nc, seq(0),
                  pl.BlockSpec((1, S, LANES), lambda hp, j: (hp, 0, 0), pipeline_mode=once), seq(0),
                  tsp, tsp] + [_ANY] * len(deps),
        out_specs=[pl.BlockSpec((S, LANES), lambda hp, j: (0, hp)), blk(0), blk(0), nc,
                   pl.BlockSpec((1, nq, 2, bq), lambda hp, j: (hp, 0, 0, 0))],
        out_shape=[jax.ShapeDtypeStruct((S, FOX_W), BF16), jax.ShapeDtypeStruct((S, FOX_W), BF16),
                   jax.ShapeDtypeStruct((S, FOX_W), BF16), jax.ShapeDtypeStruct((npair, nk, 2, bk), F32),
                   jax.ShapeDtypeStruct((npair, nq, 2, bq), F32)],
        scratch_shapes=[pltpu.VMEM((S, LANES), F32), pltpu.VMEM((S, LANES), F32), pltpu.VMEM((S, LANES), F32)],
        compiler_params=_cparams("parallel", "arbitrary"),
    )(proj, proj, proj, negc4, o, lse, d_o, q_t, do_t, *deps)


def _exchange(arrs, *, gather, name):
    n = len(arrs)
    npeer = N_DEV - 1

    def body(*refs):
        ins, outs = refs[:n], refs[n:2 * n]
        send_sems, recv_sems, loc_sems = refs[2 * n:]
        x, y, c = lax.axis_index("x"), lax.axis_index("y"), lax.axis_index("c")
        me = 4 * x + 2 * y + c
        peers = []
        for k in range(1, N_DEV):
            px = 1 - x if k & 4 else x
            py = 1 - y if k & 2 else y
            pc = 1 - c if k & 1 else c
            peers.append(((px, py, pc), 4 * px + 2 * py + pc))

        def remote(w, k):
            dev, idx = peers[k]
            src = ins[w] if gather else ins[w].at[idx]
            return pltpu.make_async_remote_copy(
                src_ref=src, dst_ref=outs[w].at[me],
                send_sem=send_sems.at[w * npeer + k], recv_sem=recv_sems.at[w * npeer + k],
                device_id=dev, device_id_type=pl.DeviceIdType.MESH)

        def arrival(w, k):
            dev, idx = peers[k]
            src = ins[w] if gather else ins[w].at[idx]
            return pltpu.make_async_remote_copy(
                src_ref=src, dst_ref=outs[w].at[idx],
                send_sem=send_sems.at[w * npeer + k], recv_sem=recv_sems.at[w * npeer + k],
                device_id=dev, device_id_type=pl.DeviceIdType.MESH)

        local = []
        for w in range(n):
            for k in range(npeer):
                remote(w, k).start()
            cp = pltpu.make_async_copy(ins[w] if gather else ins[w].at[me], outs[w].at[me], loc_sems.at[w])
            cp.start()
            local.append(cp)
        for w in range(n):
            for k in range(npeer):
                arrival(w, k).wait_recv()
        for w in range(n):
            for k in range(npeer):
                remote(w, k).wait_send()
            local[w].wait()

    hbm = pl.BlockSpec(memory_space=pl.ANY)
    out_shape = [jax.ShapeDtypeStruct((N_DEV,) + (a.shape if gather else a.shape[1:]), a.dtype) for a in arrs]
    return pl.pallas_call(
        body, name=name,
        in_specs=[hbm] * n, out_specs=[hbm] * n, out_shape=out_shape,
        scratch_shapes=[pltpu.SemaphoreType.DMA((n * npeer,)), pltpu.SemaphoreType.DMA((n * npeer,)),
                        pltpu.SemaphoreType.DMA((n,))],
        compiler_params=pltpu.CompilerParams(has_side_effects=True),
    )(*arrs)


def _gather_two_level(shard, *, name):
    def body(x_ref, out_ref, send_sems, recv_sems, local_sem):
        x, y, c = lax.axis_index("x"), lax.axis_index("y"), lax.axis_index("c")
        me, sibling = (x, y, c), (x, y, 1 - c)
        chips = [(1 - x, y), (x, 1 - y), (1 - x, 1 - y)]

        def slot(px, py, pc):
            return out_ref.at[4 * px + 2 * py + pc]

        def copy(k, block, to, src=None):
            return pltpu.make_async_remote_copy(
                src_ref=slot(*block) if src is None else src, dst_ref=slot(*block),
                send_sem=send_sems.at[k], recv_sem=recv_sems.at[k],
                device_id=to, device_id_type=pl.DeviceIdType.MESH)

        mine = pltpu.make_async_copy(x_ref, slot(*me), local_sem)
        mine.start()
        first = [copy(0, me, sibling, src=x_ref)]
        first += [copy(1 + j, me, (*chip, c), src=x_ref) for j, chip in enumerate(chips)]
        for cp in first:
            cp.start()
        passed = [copy(4 + j, (*chip, c), sibling) for j, chip in enumerate(chips)]
        for j, chip in enumerate(chips):
            copy(1 + j, (*chip, c), me).wait_recv()
            passed[j].start()
        copy(0, sibling, me).wait_recv()
        for j, chip in enumerate(chips):
            copy(4 + j, (*chip, 1 - c), me).wait_recv()
        for cp in first + passed:
            cp.wait_send()
        mine.wait()

    return pl.pallas_call(
        body, name=name,
        in_specs=[_ANY], out_specs=_ANY,
        out_shape=jax.ShapeDtypeStruct((N_DEV,) + shard.shape, shard.dtype),
        scratch_shapes=[pltpu.SemaphoreType.DMA((N_DEV - 1,)), pltpu.SemaphoreType.DMA((N_DEV - 1,)),
                        pltpu.SemaphoreType.DMA],
        compiler_params=pltpu.CompilerParams(has_side_effects=True),
    )(shard)


_HBM = pl.BlockSpec(memory_space=pltpu.HBM)
_SEM = pl.BlockSpec(memory_space=pltpu.SEMAPHORE)
_EFFECT = pltpu.SideEffectType.DATAFLOW_SIDE_EFFECTING
NPEER = N_DEV - 1


def _peer_table():
    x, y, c = lax.axis_index("x"), lax.axis_index("y"), lax.axis_index("c")
    peers = []
    for k in range(1, N_DEV):
        px = 1 - x if k & 4 else x
        py = 1 - y if k & 2 else y
        pc = 1 - c if k & 1 else c
        peers.append(((px, py, pc), 4 * px + 2 * py + pc))
    return 4 * x + 2 * y + c, peers


def _split_copy(ins, lands, send_sems, recv_sems, gather, me, peers, w, k, arriving):
    dev, idx = peers[k]
    return pltpu.make_async_remote_copy(
        src_ref=ins[w] if gather else ins[w].at[idx],
        dst_ref=lands[w].at[idx if arriving else me],
        send_sem=send_sems.at[w * NPEER + k], recv_sem=recv_sems.at[w * NPEER + k],
        device_id=dev, device_id_type=pl.DeviceIdType.MESH)


def _exchange_start(arrs, *, gather, name, deps=()):
    n = len(arrs)
    land_shapes = [(N_DEV,) + (a.shape if gather else a.shape[1:]) for a in arrs]

    def body(*refs):
        ins, lands = refs[:n], refs[n:2 * n]
        send_sems, recv_sems = refs[2 * n + len(deps)], refs[2 * n + len(deps) + 1]
        token = refs[-1]
        me, peers = _peer_table()
        for w in range(n):
            for k in range(NPEER):
                _split_copy(ins, lands, send_sems, recv_sems, gather, me, peers, w, k, False).start()
        token[...] = jnp.zeros_like(token)

    out_shape = ([pltpu.SemaphoreType.DMA((n * NPEER,)), pltpu.SemaphoreType.DMA((n * NPEER,))]
                 + [pltpu.HBM(a.shape, a.dtype) for a in arrs]
                 + [pltpu.HBM(s, a.dtype) for s, a in zip(land_shapes, arrs)]
                 + [jax.ShapeDtypeStruct((8, LANES), F32)])
    res = pl.pallas_call(
        body, name=name,
        in_specs=[_HBM] * (2 * n) + [_ANY] * len(deps),
        out_specs=[_SEM, _SEM] + [_HBM] * (2 * n) + [pl.BlockSpec(memory_space=pltpu.VMEM)],
        out_shape=out_shape,
        input_output_aliases={i: 2 + i for i in range(2 * n)},
        compiler_params=pltpu.CompilerParams(has_side_effects=_EFFECT),
    )(*[pltpu.with_memory_space_constraint(a, pltpu.HBM) for a in arrs],
      *[pltpu.with_memory_space_constraint(lax.empty(s, a.dtype), pltpu.HBM) for s, a in zip(land_shapes, arrs)],
      *deps)
    return (n, gather, res[0], res[1], res[2:2 + n], res[2 + n:2 + 2 * n]), res[-1]


def _exchange_wait(handle, after, *, name):
    n, gather, send_sems, recv_sems, ins_thru, lands_thru = handle

    def body(*refs):
        ins, lands = refs[:n], refs[n:2 * n]
        send_s, recv_s = refs[2 * n], refs[2 * n + 1]
        me, peers = _peer_table()
        for w in range(n):
            for k in range(NPEER):
                _split_copy(ins, lands, send_s, recv_s, gather, me, peers, w, k, False).wait_send()
                _split_copy(ins, lands, send_s, recv_s, gather, me, peers, w, k, True).wait_recv()

    res = pl.pallas_call(
        body, name=name,
        in_specs=[_HBM] * (2 * n) + [_SEM, _SEM, pl.BlockSpec(memory_space=pl.ANY)],
        out_specs=[_HBM] * (2 * n),
        out_shape=[pltpu.HBM(a.shape, a.dtype) for a in list(ins_thru) + list(lands_thru)],
        input_output_aliases={i: i for i in range(2 * n)},
        compiler_params=pltpu.CompilerParams(has_side_effects=_EFFECT),
    )(*ins_thru, *lands_thru, send_sems, recv_sems, after)
    return res[:n], res[n:2 * n]


def _ordered_sum(s_ref, own_ref):
    if own_ref is None:
        blocks = [s_ref[q].astype(F32) for q in range(N_DEV)]
    else:
        me = 4 * lax.axis_index("x") + 2 * lax.axis_index("y") + lax.axis_index("c")
        own = own_ref[...]
        blocks = [jnp.where(me == q, own, s_ref[q]).astype(F32) for q in range(N_DEV)]
    acc = blocks[0]
    for b in blocks[1:]:
        acc = acc + b
    return acc


def _sum8(stack, own, *, name):
    _, R, C = stack.shape
    if R % 8 == 0:
        tr, tc = _pick(R, max(8, STEP_BYTES // (C * 4 * (N_DEV + 2))), 8), C
    else:
        tr, tc = R, _pick(C, max(LANES, STEP_BYTES // (R * 4 * (N_DEV + 2))))

    def body(s_ref, own_ref, o_ref):
        o_ref[...] = _ordered_sum(s_ref, own_ref)

    blk = pl.BlockSpec((tr, tc), lambda i, j: (i, j))
    return pl.pallas_call(
        body, name=name, grid=(R // tr, C // tc),
        in_specs=[pl.BlockSpec((N_DEV, tr, tc), lambda i, j: (0, i, j)), blk],
        out_specs=blk,
        out_shape=jax.ShapeDtypeStruct((R, C), F32),
        compiler_params=_cparams("parallel", "parallel"),
    )(stack, own)


def _adamw_math(w, g, m, v):
    m = ADAM_B1 * m + (1.0 - ADAM_B1) * g
    v = ADAM_B2 * v + (1.0 - ADAM_B2) * (g * g)
    m_hat = m / (1.0 - ADAM_B1 ** ADAM_STEP)
    v_hat = v / (1.0 - ADAM_B2 ** ADAM_STEP)
    delta = -ADAM_LR * (m_hat / (jnp.sqrt(v_hat) + ADAM_EPS) + ADAM_WD * w)
    return delta, m, v


def _adamw(w, g, m, v, *, name, stacked, own=None, transposed=False):
    R, C = w.shape
    if transposed:
        tr = _pick(R, max(LANES, STEP_BYTES // (C * 4 * (9 + N_DEV))))
    else:
        tr = _pick(R, max(8, STEP_BYTES // (C * 4 * (8 + (N_DEV if stacked else 1)))), 8)
    has_own = own is not None

    def body(w_ref, g_ref, m_ref, v_ref, *rest):
        go_ref, d_ref, mo_ref, vo_ref = rest[-4:]
        g = _ordered_sum(g_ref, rest[0] if has_own else None) if stacked else g_ref[...]
        if transposed:
            g = g.T
        delta, m2, v2 = _adamw_math(w_ref[...], g, m_ref[...], v_ref[...])
        go_ref[...] = g
        d_ref[...] = delta
        mo_ref[...] = m2
        vo_ref[...] = v2

    row = pl.BlockSpec((tr, C), lambda i: (i, 0))
    if transposed:
        g_spec, own_spec = pl.BlockSpec((N_DEV, C, tr), lambda i: (0, 0, i)), pl.BlockSpec((C, tr), lambda i: (0, i))
    else:
        g_spec, own_spec = (pl.BlockSpec((N_DEV, tr, C), lambda i: (0, i, 0)) if stacked else row), row
    return pl.pallas_call(
        body, name=name, grid=(R // tr,),
        in_specs=[row, g_spec, row, row] + [own_spec] * has_own, out_specs=[row] * 4,
        out_shape=[jax.ShapeDtypeStruct((R, C), F32)] * 4,
        compiler_params=_cparams("parallel"),
    )(w, g, m, v, *([own] if has_own else []))


def kernel(x, positions, attn_norm, w_in, fox_f_bias, swa_sinks, w_branch_swa, w_branch_fox, w_out, mlp_norm, w_up, w_down, final_norm, loss_target, m_attn_norm, m_w_in, m_fox_f_bias, m_swa_sinks, m_w_branch_swa, m_w_branch_fox, m_w_out, m_mlp_norm, m_w_up, m_w_down, m_final_norm, v_attn_norm, v_w_in, v_fox_f_bias, v_swa_sinks, v_w_branch_swa, v_w_branch_fox, v_w_out, v_mlp_norm, v_w_up, v_w_down, v_final_norm):
    S, D = x.shape[1], x.shape[2]
    DFF = w_up.shape[2] * N_DEV
    d_in = w_in.shape[2] * N_DEV
    assert d_in == QKV_W + FOX_HEADS + 2 * D and (2 * D) % SWA_Q_W == 0 and S % (4 * LANES) == 0
    q_off = 2 * D
    k_off = q_off + SWA_Q_W
    v_off = k_off + SWA_KV_W
    fq_off = v_off + SWA_KV_W
    fk_off = fq_off + FOX_W
    fv_off = fk_off + FOX_W
    fl_off = fv_off + FOX_W
    NP = fl_off + FL_PAD
    x2d, tgt = x[0], loss_target[0]

    shards = [w_in[0].T.astype(BF16), w_branch_swa[0].T.astype(BF16), w_branch_fox[0].T.astype(BF16),
              w_out[0].astype(BF16), w_up[0].T.astype(BF16), w_down[0].astype(BF16)]
    me = 4 * lax.axis_index("x") + 2 * lax.axis_index("y") + lax.axis_index("c")

    def filled(stack, own):
        return lax.dynamic_update_slice(stack, own[None], (me,) + (0,) * own.ndim)

    g_in = _gather_two_level(shards[0], name="gather_w_in")
    h_rest, tok_rest = _exchange_start(shards[1:], gather=True, name="gather_rest_start", deps=[g_in])

    tm = _pick(S, 1024)
    td = _pick(D, 1024)
    tf = _pick(DFF, 1024)
    tnp = _pick(NP, 1024)

    h1 = _rms_fwd(x2d, attn_norm, name="rms1", deps=[tok_rest])
    w_in_t = g_in.reshape(d_in, D)
    w_in_p = jnp.concatenate([w_in_t[QKV_W + FOX_HEADS:], w_in_t[:QKV_W], w_in_t[QKV_W:QKV_W + FOX_HEADS],
                              jnp.zeros((FL_PAD - FOX_HEADS, D), BF16)], axis=0)
    w_fl_t = w_in_t[QKV_W:QKV_W + FOX_HEADS]
    proj, = _matmul(h1, w_in_p, mode="nt", name="mm_in", out_dtypes=[BF16], tm=_pick(S, 2048), tn=tnp, tk=D)
    z_sd, = _matmul(h1, w_fl_t, mode="nt", name="mm_flogit", out_dtypes=[F32], tm=tm, tn=FOX_HEADS, tk=D)
    z_t = z_sd.T
    bias_col = fox_f_bias.reshape(FOX_HEADS, 1)
    negc = _fox_prep(z_t, bias_col, name="fox_prep")
    (fbq, fbk), (bbq, bbk) = _fox_blocks(S)
    inv_freq = ROPE_THETA ** (-jnp.arange(0, HEAD_DIM, 2, dtype=F32) / HEAD_DIM)
    invf = jnp.tile(inv_freq, LANES // (HEAD_DIM // 2)).reshape(1, LANES)
    cos_t, sin_t = _rope_tables(positions.reshape(S, 1), invf, name="rope_tables")
    q_rope, k_rope = _rope_fwd(proj, cos_t, sin_t, q_off=q_off, k_off=k_off, name="rope_fwd")
    sinks = swa_sinks.reshape(-1)
    swa_mask = _swa_mask_bias()
    o_a = _swa_fwd(q_rope, k_rope, proj, sinks, swa_mask, v_off=v_off, name="swa_fwd")
    o_b, lse = _fox_fwd(proj, _key_bias_blocks(negc, fbk), q_off=fq_off, k_off=fk_off, v_off=fv_off,
                        bq=fbq, bk=fbk, name="fox_fwd")
    s_rest, g_rest = _exchange_wait(h_rest, o_b, name="gather_rest_wait")
    g_bs, g_bf, g_o, g_up, g_dn = [filled(g, s) for g, s in zip(g_rest, s_rest)]
    w_bs_t = g_bs.reshape(D, SWA_Q_W)
    w_bf_t = g_bf.reshape(D, FOX_W)
    w_o = g_o.reshape(D, D)
    w_up_t = g_up.reshape(DFF, D)
    w_dn = g_dn.reshape(DFF, D)
    ya, = _matmul(o_a, w_bs_t, mode="nt", name="mm_branch_swa", out_dtypes=[BF16], tm=_pick(S, 512), tn=D, tk=SWA_Q_W)
    gate_maps = [lambda i, j, k: (i, j), lambda i, j, k: (i, j), lambda i, j, k: (i, j + D // td)]

    def merge_epi(acc, ya_t, ga_t, gb_t):
        merged = _sigmoid(ga_t.astype(F32)) * ya_t.astype(F32) + _sigmoid(gb_t.astype(F32)) * acc
        return acc, merged

    yb, merged = _matmul(o_b, w_bf_t, mode="nt", name="mm_branch_fox", out_dtypes=[BF16, BF16],
                         tm=tm, tn=td, tk=FOX_W, extras=[ya, proj, proj], extra_maps=gate_maps,
                         epilogue=merge_epi)
    def out_epi(acc, r, g):
        xm = acc + r
        rr = lax.rsqrt(jnp.mean(xm * xm, axis=-1, keepdims=True) + RMS_EPS)
        return xm, xm * rr * g

    x_mid, h2 = _matmul(merged, w_o, mode="nn", name="mm_out", out_dtypes=[F32, BF16], tm=_pick(S, 512), tn=D, tk=D,
                        extras=[x2d, mlp_norm], extra_maps=[lambda i, j, k: (i, j), lambda i, j, k: (0, 0)],
                        extra_shapes=[None, (1, D)], epilogue=out_epi)
    u, = _matmul(h2, w_up_t, mode="nt", name="mm_up", out_dtypes=[BF16], tm=_pick(S, 2048), tn=tf, tk=D,
                 epilogue=lambda acc: (jnp.maximum(acc, 0.0),))
    x_fin, = _matmul(u, w_dn, mode="nn", name="mm_down", out_dtypes=[F32], tm=tm, tn=td, tk=_pick(DFF, 2048),
                     a_fn=_square_bf16, extras=[x_mid], epilogue=lambda acc, r: (acc + r,))

    dx3b, dg3, loss_part = _loss_head(x_fin, tgt, final_norm.reshape(1, D), name="loss_head")
    d_up, = _matmul(dx3b, w_dn, mode="nt", name="mm_d_act", out_dtypes=[BF16], tm=_pick(S, 2048), tn=tf, tk=D,
                    extras=[u], epilogue=lambda acc, ut: (acc * (2.0 * ut.astype(F32)),))
    tks = _pick(S, 2048)
    dw_dn, = _matmul(u, dx3b, mode="tn", name="mm_dw_down", out_dtypes=[BF16], tm=tf, tn=td, tk=tks,
                     a_fn=_square_bf16)
    dh2, = _matmul(d_up, w_up_t, mode="nn", name="mm_dh2", out_dtypes=[BF16], tm=_pick(S, 512), tn=D,
                   tk=_pick(DFF, 2048))
    dw_up_t, = _matmul(d_up, h2, mode="tn", name="mm_dw_up", out_dtypes=[BF16], tm=tf, tn=td, tk=tks)
    h_s1, tok_s1 = _exchange_start([dw_up_t.reshape(N_DEV, DFF // N_DEV, D), dw_dn.reshape(N_DEV, DFF // N_DEV, D)],
                                   gather=False, name="scatter_mlp_start")
    dx2b, dg2 = _rms_bwd(dh2, x_mid, mlp_norm, dx3b, name="rms2_bwd", out_dtype=BF16, deps=[tok_s1])

    def gate_bwd_epi(dm, ya_t, yb_t, ga_t, gb_t):
        sa, sb = _sigmoid(ga_t.astype(F32)), _sigmoid(gb_t.astype(F32))
        return (dm * sa, dm * sb, dm * ya_t.astype(F32) * sa * (1.0 - sa), dm * yb_t.astype(F32) * sb * (1.0 - sb))

    gmaps = [lambda i, j, k: (i, j), lambda i, j, k: (i, j), lambda i, j, k: (i, j),
             lambda i, j, k: (i, j + D // td)]
    d_ya, d_yb, d_ga, d_gb = _matmul(dx2b, w_o, mode="nt", name="mm_d_merged", out_dtypes=[BF16] * 4,
                                     tm=tm, tn=td, tk=D, extras=[ya, yb, proj, proj], extra_maps=gmaps,
                                     epilogue=gate_bwd_epi)
    dw_o, = _matmul(merged, dx2b, mode="tn", name="mm_dw_out", out_dtypes=[BF16], tm=td, tn=td, tk=tks)
    d_oa, = _matmul(d_ya, w_bs_t, mode="nn", name="mm_d_oa", out_dtypes=[BF16], tm=tm, tn=SWA_Q_W, tk=D)
    d_ob, = _matmul(d_yb, w_bf_t, mode="nn", name="mm_d_ob", out_dtypes=[BF16], tm=tm, tn=FOX_W, tk=D)
    tkl = _pick(S, 4096)
    dw_bs_t, = _matmul(d_ya, o_a, mode="tn", name="mm_dw_bs", out_dtypes=[BF16], tm=td, tn=SWA_Q_W, tk=tkl)
    dw_bf_t, = _matmul(d_yb, o_b, mode="tn", name="mm_dw_bf", out_dtypes=[BF16], tm=td, tn=FOX_W, tk=tkl)
    h_s2, tok_s2 = _exchange_start([dw_bs_t.reshape(N_DEV, D // N_DEV, SWA_Q_W),
                                    dw_bf_t.reshape(N_DEV, D // N_DEV, FOX_W), dw_o.reshape(N_DEV, D // N_DEV, D)],
                                   gather=False, name="scatter_attn_start")
    def row_blocks_t(a):
        return a.reshape(S // bbq, bbq, FOX_W).transpose(0, 2, 1)

    d_fq, d_fk, d_fv, dcol4, drow4 = _fox_bwd(proj, _key_bias_blocks(negc, bbk), o_b, lse, d_ob,
                                              row_blocks_t(proj[:, fq_off:fq_off + FOX_W]), row_blocks_t(d_ob),
                                              q_off=fq_off, k_off=fk_off, v_off=fv_off, bq=bbq, bk=bbk,
                                              name="fox_bwd", deps=[tok_s2])
    dcol = dcol4.transpose(0, 2, 1, 3).reshape(FOX_HEADS, S)
    drow = drow4.transpose(0, 2, 1, 3).reshape(FOX_HEADS, S)
    dz_t, dbias_l = _fox_post(drow, dcol, z_t, bias_col, name="fox_post")
    d_aq, dk_c, dk_p, dv_c, dv_p, dsink_l = _swa_bwd(q_rope, k_rope, proj, sinks, d_oa, cos_t, sin_t, swa_mask,
                                                     v_off=v_off, name="swa_bwd")
    d_ak, d_av = _rope_bwd(dk_c, dk_p, dv_c, dv_p, cos_t, sin_t, name="rope_bwd")
    dz_pad = jnp.pad(dz_t.T.astype(BF16), ((0, 0), (0, FL_PAD - FOX_HEADS)))
    d_proj = jnp.concatenate([d_ga, d_gb, d_aq, d_ak, d_av, d_fq, d_fk, d_fv, dz_pad], axis=1)
    tkp = _pick(NP, 2304)
    dw_in_p, = _matmul(d_proj, h1, mode="tn", name="mm_dw_in", out_dtypes=[BF16], tm=_pick(NP, 512), tn=D, tk=tks)
    dw_in_t = jnp.concatenate([dw_in_p[q_off:q_off + QKV_W], dw_in_p[fl_off:fl_off + FOX_HEADS], dw_in_p[:q_off]],
                              axis=0)
    h_s3, tok_s3 = _exchange_start([dw_in_t.reshape(N_DEV, d_in // N_DEV, D)], gather=False,
                                   name="scatter_in_start")
    dh1, = _matmul(d_proj, w_in_p, mode="nn", name="mm_dh1", out_dtypes=[BF16], tm=tm, tn=td, tk=tkp, deps=[tok_s3])
    dx, dg1 = _rms_bwd(dh1, x2d, attn_norm, dx2b, name="rms1_bwd", out_dtype=F32)

    dbias = dbias_l[:, 0]
    dsinks = dsink_l[:, :, 0].reshape(-1)
    nsm = 3 * D + 2 * LANES
    tail = jnp.zeros((2 * LANES,), F32)
    small_g = jnp.concatenate([dg1[0], dg2[0], dg3[0],
                               tail.at[0:16].set(dbias).at[16:32].set(dsinks).at[32].set(loss_part[0, 0])])

    def pack(a_norm, b_norm, f_norm, bias, snk):
        return jnp.concatenate([a_norm[0], b_norm[0], f_norm,
                                tail.at[0:16].set(bias[0]).at[16:32].set(snk[0])]).reshape(1, nsm)

    small_stack, = _exchange([small_g.reshape(1, nsm)], gather=True, name="gather_small")
    u_sm = _adamw(pack(attn_norm, mlp_norm, final_norm, fox_f_bias, swa_sinks), small_stack,
                  pack(m_attn_norm, m_mlp_norm, m_final_norm, m_fox_f_bias, m_swa_sinks),
                  pack(v_attn_norm, v_mlp_norm, v_final_norm, v_fox_f_bias, v_swa_sinks),
                  name="adamw_small", stacked=True)
    loss = u_sm[0][0, 3 * D + 32]

    def own_of(src):
        return lax.dynamic_index_in_dim(src, me, 0, keepdims=False)

    def update_t(stack, src, w, m, v, nm):
        g = _sum8(stack, own_of(src), name="sum_" + nm).T
        return _adamw(w[0], g, m[0], v[0], name="adamw_" + nm, stacked=False)

    def update(stack, src, w, m, v, nm, transposed=False):
        return _adamw(w[0], stack, m[0], v[0], name="adamw_" + nm, stacked=True, own=own_of(src),
                      transposed=transposed)

    (s_up, s_dn), (r_up, r_dn) = _exchange_wait(h_s1, u_sm[1], name="scatter_mlp_wait")
    u_up = update(r_up, s_up, w_up, m_w_up, v_w_up, "w_up", transposed=True)
    u_dn = update(r_dn, s_dn, w_down, m_w_down, v_w_down, "w_down")
    (s_bs, s_bf, s_o), (r_bs, r_bf, r_o) = _exchange_wait(h_s2, u_dn[1], name="scatter_attn_wait")
    u_bs = update(r_bs, s_bs, w_branch_swa, m_w_branch_swa, v_w_branch_swa, "w_bs", transposed=True)
    u_bf = update(r_bf, s_bf, w_branch_fox, m_w_branch_fox, v_w_branch_fox, "w_bf", transposed=True)
    u_o = update(r_o, s_o, w_out, m_w_out, v_w_out, "w_out")
    (s_w_in,), (r_in,) = _exchange_wait(h_s3, u_o[1], name="scatter_in_wait")
    u_in = update_t(r_in, s_w_in, w_in, m_w_in, v_w_in, "w_in")

    def small(kind):
        a = u_sm[kind][0]
        return dict(attn_norm=a[0:D][None], mlp_norm=a[D:2 * D][None], final_norm=a[2 * D:3 * D],
                    fox_f_bias=a[3 * D:3 * D + 16][None], swa_sinks=a[3 * D + 16:3 * D + 32][None])

    big = dict(w_in=u_in, w_branch_swa=u_bs, w_branch_fox=u_bf, w_out=u_o, w_up=u_up, w_down=u_dn)
    order = ["attn_norm", "w_in", "fox_f_bias", "swa_sinks", "w_branch_swa", "w_branch_fox", "w_out", "mlp_norm",
             "w_up", "w_down", "final_norm"]
    outs = [loss, dx[None]]
    for kind in range(4):
        sm = small(kind)
        for nm in order:
            outs.append(big[nm][kind][None] if nm in big else sm[nm])
    return tuple(outs)
```

```python
import functools

import jax
import jax.numpy as jnp
from jax import lax
from jax.experimental import pallas as pl
from jax.experimental.pallas import tpu as pltpu

F32 = jnp.float32
BF16 = jnp.bfloat16

N_DEV = 8
HEAD_DIM = 64
SWA_Q_W = 1024
SWA_KV_W = 128
SWA_GROUP = 8
WINDOW = 128
FOX_W = 1024
FOX_HEADS = 16
QKV_W = SWA_Q_W + 2 * SWA_KV_W + 3 * FOX_W
FL_PAD = 256
ROPE_THETA = 10000.0
RMS_EPS = 1e-6
ATT_SCALE = 0.125
NEG = -1e30

ADAM_LR = 0.001
ADAM_B1 = 0.9
ADAM_B2 = 0.999
ADAM_EPS = 1e-08
ADAM_WD = 0.01
ADAM_STEP = 10

FOX_FWD_BLOCKS = (1024, 1024)
FOX_BWD_BLOCKS = (1024, 512)
FOX_FWD_PAIRS = 2

LANES = 128
VMEM_LIMIT = 56 * 1024 * 1024
STEP_BYTES = 12 * 1024 * 1024


def _cparams(*sem):
    return pltpu.CompilerParams(dimension_semantics=sem, vmem_limit_bytes=VMEM_LIMIT)


def _pick(dim, pref, align=LANES):
    best = None
    t = align
    while t <= min(dim, pref):
        if dim % t == 0:
            best = t
        t += align
    return best if best is not None else dim


_DIMS = {"nn": ((1,), (0,)), "nt": ((1,), (1,)), "tn": ((0,), (0,))}


_ANY = pl.BlockSpec(memory_space=pl.ANY)


def _matmul(a, b, *, mode, name, out_dtypes, tm, tn, tk, extras=(), extra_maps=None, extra_shapes=None,
            a_fn=None, epilogue=None, deps=()):
    if mode == "nn":
        (M, K), (K2, N) = a.shape, b.shape
    elif mode == "nt":
        (M, K), (N, K2) = a.shape, b.shape
    else:
        (K, M), (K2, N) = a.shape, b.shape
    assert K == K2, (name, a.shape, b.shape)
    assert M % tm == 0 and N % tn == 0 and K % tk == 0, (name, M, N, K, tm, tn, tk)
    nk = K // tk
    ne, no = len(extras), len(out_dtypes)
    dims = (_DIMS[mode], ((), ()))

    def body(*refs):
        a_ref, b_ref = refs[0], refs[1]
        ex_refs = refs[2:2 + ne]
        out_refs = refs[2 + ne + len(deps):2 + ne + len(deps) + no]

        def finish(acc):
            res = (acc,) if epilogue is None else epilogue(acc, *[e[...] for e in ex_refs])
            for o_ref, r in zip(out_refs, res):
                o_ref[...] = r.astype(o_ref.dtype)

        def product():
            av = a_ref[...]
            if a_fn is not None:
                av = a_fn(av)
            return lax.dot_general(av, b_ref[...], dims, preferred_element_type=F32)

        if nk == 1:
            finish(product())
        else:
            acc_ref = refs[-1]
            k = pl.program_id(2)

            @pl.when(k == 0)
            def _():
                acc_ref[...] = jnp.zeros_like(acc_ref)

            acc_ref[...] += product()

            @pl.when(k == nk - 1)
            def _():
                finish(acc_ref[...])

    if mode == "tn":
        a_spec = pl.BlockSpec((tk, tm), lambda i, j, k: (k, i))
    else:
        a_spec = pl.BlockSpec((tm, tk), lambda i, j, k: (i, k))
    if mode == "nt":
        b_spec = pl.BlockSpec((tn, tk), lambda i, j, k: (j, k))
    else:
        b_spec = pl.BlockSpec((tk, tn), lambda i, j, k: (k, j))
    if extra_maps is None:
        extra_maps = [lambda i, j, k: (i, j)] * ne
    if extra_shapes is None:
        extra_shapes = [None] * ne
    ex_specs = [pl.BlockSpec(s or (tm, tn), m) for s, m in zip(extra_shapes, extra_maps)]
    out_spec = [pl.BlockSpec((tm, tn), lambda i, j, k: (i, j)) for _ in range(no)]
    res = pl.pallas_call(
        body,
        name=name,
        grid=(M // tm, N // tn, nk),
        in_specs=[a_spec, b_spec] + ex_specs + [_ANY] * len(deps),
        out_specs=out_spec,
        out_shape=[jax.ShapeDtypeStruct((M, N), d) for d in out_dtypes],
        scratch_shapes=[pltpu.VMEM((tm, tn), F32)] if nk > 1 else [],
        compiler_params=_cparams("parallel", "parallel", "arbitrary"),
    )(a, b, *extras, *deps)
    return res


def _square_bf16(t):
    tf = t.astype(F32)
    return (tf * tf).astype(BF16)


def _sigmoid(g):
    return 1.0 / (1.0 + jnp.exp(-g))


def _rms_fwd(x, gain, *, name, deps=()):
    S, D = x.shape
    tr = _pick(S, 512, 8)

    def body(x_ref, g_ref, *rest):
        h_ref = rest[-1]
        xv = x_ref[...]
        r = lax.rsqrt(jnp.mean(xv * xv, axis=-1, keepdims=True) + RMS_EPS)
        h_ref[...] = (xv * r * g_ref[...]).astype(BF16)

    return pl.pallas_call(
        body, name=name, grid=(S // tr,),
        in_specs=[pl.BlockSpec((tr, D), lambda i: (i, 0)), pl.BlockSpec((1, D), lambda i: (0, 0))] + [_ANY] * len(deps),
        out_specs=pl.BlockSpec((tr, D), lambda i: (i, 0)),
        out_shape=jax.ShapeDtypeStruct((S, D), BF16),
        compiler_params=_cparams("parallel"),
    )(x, gain, *deps)


def _rms_bwd(dh, x, gain, dres, *, name, out_dtype, deps=()):
    S, D = x.shape
    tr = _pick(S, 512, 8)

    def body(dh_ref, x_ref, g_ref, dres_ref, *rest):
        outs = rest[len(deps):]
        dx_ref, dg_ref = outs[0], outs[-1]
        xv = x_ref[...]
        r = lax.rsqrt(jnp.mean(xv * xv, axis=-1, keepdims=True) + RMS_EPS)
        xh = xv * r
        dhv = dh_ref[...].astype(F32)
        t = dhv * g_ref[...]
        dx = r * (t - xh * jnp.mean(t * xh, axis=-1, keepdims=True)) + dres_ref[...].astype(F32)
        dx_ref[...] = dx.astype(out_dtype)
        part = jnp.sum(dhv * xh, axis=0, keepdims=True)

        @pl.when(pl.program_id(0) == 0)
        def _():
            dg_ref[...] = part

        @pl.when(pl.program_id(0) > 0)
        def _():
            dg_ref[...] += part

    row = pl.BlockSpec((tr, D), lambda i: (i, 0))
    vec = pl.BlockSpec((1, D), lambda i: (0, 0))
    return pl.pallas_call(
        body, name=name, grid=(S // tr,),
        in_specs=[row, row, vec, row] + [_ANY] * len(deps), out_specs=[row, vec],
        out_shape=[jax.ShapeDtypeStruct((S, D), out_dtype), jax.ShapeDtypeStruct((1, D), F32)],
        compiler_params=_cparams("arbitrary"),
    )(dh, x, gain, dres, *deps)


def _loss_head(x3, target, gain, *, name):
    S, D = x3.shape
    tr = _pick(S, 512, 8)

    def body(x_ref, t_ref, g_ref, dxb_ref, dg_ref, loss_ref):
        xv = x_ref[...]
        r = lax.rsqrt(jnp.mean(xv * xv, axis=-1, keepdims=True) + RMS_EPS)
        xh = xv * r
        gv = g_ref[...]
        err = xh * gv - t_ref[...]
        lpart = jnp.zeros((1, LANES), F32) + (0.5 / D) * jnp.sum(err * err)
        dy = err * (1.0 / D)
        t = dy * gv
        dx = r * (t - xh * jnp.mean(t * xh, axis=-1, keepdims=True))
        dxb_ref[...] = dx.astype(BF16)
        part = jnp.sum(dy * xh, axis=0, keepdims=True)

        @pl.when(pl.program_id(0) == 0)
        def _():
            dg_ref[...] = part
            loss_ref[...] = lpart

        @pl.when(pl.program_id(0) > 0)
        def _():
            dg_ref[...] += part
            loss_ref[...] += lpart

    row = pl.BlockSpec((tr, D), lambda i: (i, 0))
    vec = pl.BlockSpec((1, D), lambda i: (0, 0))
    return pl.pallas_call(
        body, name=name, grid=(S // tr,),
        in_specs=[row, row, vec],
        out_specs=[row, vec, pl.BlockSpec((1, LANES), lambda i: (0, 0))],
        out_shape=[jax.ShapeDtypeStruct((S, D), BF16),
                   jax.ShapeDtypeStruct((1, D), F32), jax.ShapeDtypeStruct((1, LANES), F32)],
        compiler_params=_cparams("arbitrary"),
    )(x3, target, gain)


def _rope_tables(pos_col, invf, *, name):
    S = pos_col.shape[0]
    tr = _pick(S, 512, 8)

    def body(p_ref, f_ref, cos_ref, sin_ref):
        ang = p_ref[...].astype(F32) * f_ref[...]
        lane = lax.broadcasted_iota(jnp.int32, (1, LANES), 1)
        first = (lane % HEAD_DIM) < HEAD_DIM // 2
        sn = jnp.sin(ang)
        cos_ref[...] = jnp.cos(ang)
        sin_ref[...] = jnp.where(first, -sn, sn)

    return pl.pallas_call(
        body, name=name, grid=(S // tr,),
        in_specs=[pl.BlockSpec((tr, 1), lambda i: (i, 0)), pl.BlockSpec((1, LANES), lambda i: (0, 0))],
        out_specs=[pl.BlockSpec((tr, LANES), lambda i: (i, 0))] * 2,
        out_shape=[jax.ShapeDtypeStruct((S, LANES), F32)] * 2,
        compiler_params=_cparams("parallel"),
    )(pos_col, invf)


def _swap_halves(t):
    lane = lax.broadcasted_iota(jnp.int32, (1, LANES), 1)
    first = (lane % HEAD_DIM) < HEAD_DIM // 2
    return jnp.where(first, pltpu.roll(t, LANES - HEAD_DIM // 2, 1), pltpu.roll(t, HEAD_DIM // 2, 1))


def _rope_fwd(proj, cos_t, sin_t, *, q_off, k_off, name):
    S = proj.shape[0]
    tr = _pick(S, 512, 8)
    nqb = SWA_Q_W // LANES

    def body(q_ref, k_ref, c_ref, s_ref, qo_ref, ko_ref):
        cv, sv = c_ref[...], s_ref[...]
        for b in range(nqb):
            t = q_ref[:, b * LANES:(b + 1) * LANES].astype(F32)
            qo_ref[:, b * LANES:(b + 1) * LANES] = (t * cv + _swap_halves(t) * sv).astype(BF16)
        t = k_ref[...].astype(F32)
        ko_ref[...] = (t * cv + _swap_halves(t) * sv).astype(BF16)

    tab = pl.BlockSpec((tr, LANES), lambda i: (i, 0))
    return pl.pallas_call(
        body, name=name, grid=(S // tr,),
        in_specs=[pl.BlockSpec((tr, SWA_Q_W), lambda i: (i, q_off // SWA_Q_W)),
                  pl.BlockSpec((tr, LANES), lambda i: (i, k_off // LANES)), tab, tab],
        out_specs=[pl.BlockSpec((tr, SWA_Q_W), lambda i: (i, 0)), tab],
        out_shape=[jax.ShapeDtypeStruct((S, SWA_Q_W), BF16), jax.ShapeDtypeStruct((S, LANES), BF16)],
        compiler_params=_cparams("parallel"),
    )(proj, proj, cos_t, sin_t)


def _rope_bwd(dk_cur, dk_prev, dv_cur, dv_prev, cos_t, sin_t, *, name):
    S = dk_cur.shape[1]
    tr = _pick(S, 512)
    nb = S // tr

    def body(kc_ref, kp_ref, vc_ref, vp_ref, c_ref, s_ref, dko_ref, dvo_ref):
        cv, sv = c_ref[...], s_ref[...]
        row = pl.program_id(0) * tr + lax.broadcasted_iota(jnp.int32, (tr, 1), 0)
        has_next = row < S - WINDOW
        d = kc_ref[0] + kc_ref[1] + jnp.where(has_next, kp_ref[0] + kp_ref[1], 0.0)
        dko_ref[...] = (d * cv + _swap_halves(d * sv)).astype(BF16)
        dvo_ref[...] = (vc_ref[0] + vc_ref[1] + jnp.where(has_next, vp_ref[0] + vp_ref[1], 0.0)).astype(BF16)

    tab = pl.BlockSpec((tr, LANES), lambda i: (i, 0))
    cur = pl.BlockSpec((2, tr, LANES), lambda i: (0, i, 0))
    return pl.pallas_call(
        body, name=name, grid=(nb,),
        in_specs=[cur, cur, cur, cur, tab, tab],
        out_specs=[tab, tab],
        out_shape=[jax.ShapeDtypeStruct((S, LANES), BF16), jax.ShapeDtypeStruct((S, LANES), BF16)],
        compiler_params=_cparams("parallel"),
    )(dk_cur, dk_prev, dv_cur, dv_prev, cos_t, sin_t)


def _dot_nt(a, b):
    return lax.dot_general(a, b, (((1,), (1,)), ((), ())), preferred_element_type=F32)


def _dot_tn(a, b):
    return lax.dot_general(a, b, (((0,), (0,)), ((), ())), preferred_element_type=F32)


def _dot_nn(a, b):
    return lax.dot_general(a, b, (((1,), (0,)), ((), ())), preferred_element_type=F32)


def _roll_half(t):
    return pltpu.roll(t.astype(F32), HEAD_DIM, 1).astype(t.dtype)


SWA_STACK = SWA_GROUP // 2


def _swa_mask_bias():
    rows = SWA_STACK * WINDOW
    row = lax.broadcasted_iota(jnp.int32, (rows, 2 * WINDOW), 0) % WINDOW
    col = lax.broadcasted_iota(jnp.int32, (rows, 2 * WINDOW), 1)
    diff = row + WINDOW - col
    window = (diff >= 0) & (diff < WINDOW)
    return jnp.stack([jnp.where(window & (col >= WINDOW), 0.0, NEG), jnp.where(window, 0.0, NEG)]).astype(F32)


def _swa_common(kp_ref, kc_ref, vp_ref, vc_ref):
    k2 = jnp.concatenate([kp_ref[...], kc_ref[...]], axis=0)
    v2 = jnp.concatenate([vp_ref[...], vc_ref[...]], axis=0)
    k_sw, v_sw = _roll_half(k2), _roll_half(v2)
    lane = lax.broadcasted_iota(jnp.int32, (1, LANES), 1)
    half = [lane < HEAD_DIM, lane >= HEAD_DIM]
    kk = [[k2 if hk == a else k_sw for a in range(2)] for hk in range(2)]
    vv = [[v2 if hk == a else v_sw for a in range(2)] for hk in range(2)]
    return half, kk, vv


def _swa_stack(ref, hk, mask, scale=None):
    parts = []
    for t in range(SWA_STACK):
        blk = ref[:, (hk * SWA_STACK + t) * LANES:(hk * SWA_STACK + t + 1) * LANES]
        if scale is not None:
            blk = blk * jnp.asarray(scale, blk.dtype)
        parts.append(jnp.where(mask, blk, jnp.zeros_like(blk)))
    return jnp.concatenate(parts, axis=0)


def _swa_sink_column(sink_ref, hk, a):
    blk = lax.broadcasted_iota(jnp.int32, (SWA_STACK * WINDOW, 1), 0) // WINDOW
    col = jnp.zeros((SWA_STACK * WINDOW, 1), F32)
    for t in range(SWA_STACK):
        col = jnp.where(blk == t, sink_ref[hk * SWA_GROUP + 2 * t + a], col)
    return col


def _swa_probs(qm, kk, mask_bias, sink):
    s = _dot_nt(qm, kk) + mask_bias
    m = jnp.maximum(jnp.max(s, axis=1, keepdims=True), sink)
    e = jnp.exp(s - m)
    es = jnp.exp(sink - m)
    inv = 1.0 / (jnp.sum(e, axis=1, keepdims=True) + es)
    return e * inv, es * inv


def _swa_mask_spec():
    return pl.BlockSpec((1, SWA_STACK * WINDOW, 2 * WINDOW), lambda n: (jnp.minimum(n, 1), 0, 0))


def _swa_fwd(q_rope, k_rope, proj, sinks, mask_bias, *, v_off, name):
    S = q_rope.shape[0]
    nb = S // WINDOW

    def body(sink_ref, q_ref, kp_ref, kc_ref, vp_ref, vc_ref, mask_ref, o_ref):
        half, kk, vv = _swa_common(kp_ref, kc_ref, vp_ref, vc_ref)
        for hk in range(2):
            outs = []
            for a in range(2):
                qm = _swa_stack(q_ref, hk, half[a], ATT_SCALE)
                p, _ = _swa_probs(qm, kk[hk][a], mask_ref[0], _swa_sink_column(sink_ref, hk, a))
                outs.append(_dot_nn(p.astype(BF16), vv[hk][a]))
            for t in range(SWA_STACK):
                rows = slice(t * WINDOW, (t + 1) * WINDOW)
                c0 = (hk * SWA_STACK + t) * LANES
                o_ref[:, c0:c0 + LANES] = jnp.where(half[0], outs[0][rows], outs[1][rows]).astype(BF16)

    prev = lambda n: (jnp.maximum(n - 1, 0), 0)
    cur = lambda n: (n, 0)
    vprev = lambda n: (jnp.maximum(n - 1, 0), v_off // LANES)
    vcur = lambda n: (n, v_off // LANES)
    blk = lambda m: pl.BlockSpec((WINDOW, LANES), m)
    return pl.pallas_call(
        body, name=name, grid=(nb,),
        in_specs=[pl.BlockSpec(memory_space=pltpu.SMEM),
                  pl.BlockSpec((WINDOW, SWA_Q_W), lambda n: (n, 0)),
                  blk(prev), blk(cur), blk(vprev), blk(vcur), _swa_mask_spec()],
        out_specs=pl.BlockSpec((WINDOW, SWA_Q_W), lambda n: (n, 0)),
        out_shape=jax.ShapeDtypeStruct((S, SWA_Q_W), BF16),
        compiler_params=_cparams("parallel"),
    )(sinks, q_rope, k_rope, k_rope, proj, proj, mask_bias)


def _swa_bwd(q_rope, k_rope, proj, sinks, d_o, cos_t, sin_t, mask_bias, *, v_off, name):
    S = q_rope.shape[0]
    nb = S // WINDOW

    def body(sink_ref, q_ref, kp_ref, kc_ref, vp_ref, vc_ref, do_ref, c_ref, s_ref, mask_ref,
             dq_ref, dkc_ref, dkp_ref, dvc_ref, dvp_ref, dsink_ref):
        n = pl.program_id(0)
        half, kk, vv = _swa_common(kp_ref, kc_ref, vp_ref, vc_ref)
        allowed = mask_ref[0]
        cv, sv = c_ref[...], s_ref[...]
        srow = lax.broadcasted_iota(jnp.int32, (SWA_GROUP, LANES), 0)
        for hk in range(2):
            dk_acc = jnp.zeros((2 * WINDOW, LANES), F32)
            dv_acc = jnp.zeros((2 * WINDOW, LANES), F32)
            dsink = jnp.zeros((SWA_GROUP, LANES), F32)
            dqs = []
            for a in range(2):
                qm = _swa_stack(q_ref, hk, half[a], ATT_SCALE)
                dom = _swa_stack(do_ref, hk, half[a])
                p, psink = _swa_probs(qm, kk[hk][a], allowed, _swa_sink_column(sink_ref, hk, a))
                dp = _dot_nt(dom, vv[hk][a])
                delta = jnp.sum(p * dp, axis=1, keepdims=True)
                ds = (p * (dp - delta)).astype(BF16)
                dsk = psink * delta
                for t in range(SWA_STACK):
                    dsink = dsink + jnp.where(srow == 2 * t + a, -jnp.sum(dsk[t * WINDOW:(t + 1) * WINDOW]), 0.0)
                dqs.append(_dot_nn(ds, kk[hk][a]) * ATT_SCALE)
                dk_acc = dk_acc + _dot_tn(ds, qm)
                dv_acc = dv_acc + _dot_tn(p.astype(BF16), dom)
            for t in range(SWA_STACK):
                rows = slice(t * WINDOW, (t + 1) * WINDOW)
                d = jnp.where(half[0], dqs[0][rows], dqs[1][rows])
                c0 = (hk * SWA_STACK + t) * LANES
                dq_ref[:, c0:c0 + LANES] = (d * cv + _swap_halves(d * sv)).astype(BF16)
            dk_t = jnp.where(half[hk], dk_acc + pltpu.roll(dk_acc, HEAD_DIM, 1), 0.0)
            dv_t = jnp.where(half[hk], dv_acc + pltpu.roll(dv_acc, HEAD_DIM, 1), 0.0)
            dkp_ref[hk] = dk_t[:WINDOW]
            dkc_ref[hk] = dk_t[WINDOW:]
            dvp_ref[hk] = dv_t[:WINDOW]
            dvc_ref[hk] = dv_t[WINDOW:]

            @pl.when(n == 0)
            def _():
                dsink_ref[hk] = dsink

            @pl.when(n > 0)
            def _():
                dsink_ref[hk] += dsink

    prev = lambda n: (jnp.maximum(n - 1, 0), 0)
    cur = lambda n: (n, 0)
    vprev = lambda n: (jnp.maximum(n - 1, 0), v_off // LANES)
    vcur = lambda n: (n, v_off // LANES)
    blk = lambda m: pl.BlockSpec((WINDOW, LANES), m)
    qblk = pl.BlockSpec((WINDOW, SWA_Q_W), lambda n: (n, 0))
    part = pl.BlockSpec((2, WINDOW, LANES), lambda n: (0, n, 0))
    part_prev = pl.BlockSpec((2, WINDOW, LANES), lambda n: (0, jnp.maximum(n - 1, 0), 0))
    part_shape = jax.ShapeDtypeStruct((2, S, LANES), F32)
    return pl.pallas_call(
        body, name=name, grid=(nb,),
        in_specs=[pl.BlockSpec(memory_space=pltpu.SMEM), qblk, blk(prev), blk(cur), blk(vprev), blk(vcur), qblk,
                  blk(cur), blk(cur), _swa_mask_spec()],
        out_specs=[qblk, part, part_prev, part, part_prev,
                   pl.BlockSpec((2, SWA_GROUP, LANES), lambda n: (0, 0, 0))],
        out_shape=[jax.ShapeDtypeStruct((S, SWA_Q_W), BF16), part_shape, part_shape, part_shape, part_shape,
                   jax.ShapeDtypeStruct((2, SWA_GROUP, LANES), F32)],
        compiler_params=_cparams("arbitrary"),
    )(sinks, q_rope, k_rope, k_rope, proj, proj, d_o, cos_t, sin_t, mask_bias)


def _fox_prep(z_t, bias_col, *, name):
    H, S = z_t.shape
    tb = _pick(S, 512)

    def body(z_ref, b_ref, o_ref, carry_ref):
        @pl.when(pl.program_id(0) == 0)
        def _():
            carry_ref[...] = jnp.zeros_like(carry_ref)

        zz = z_ref[...] + b_ref[...]
        t = jnp.exp(-jnp.abs(zz))
        log1p = jnp.where(t < 1e-2, t * (1.0 - t * (0.5 - t * (1.0 / 3.0))), jnp.log(1.0 + t))
        logf = jnp.minimum(zz, 0.0) - log1p
        r = lax.broadcasted_iota(jnp.int32, (tb, tb), 0)
        c = lax.broadcasted_iota(jnp.int32, (tb, tb), 1)
        tri = (r <= c).astype(BF16)
        hi = logf.astype(BF16)
        r1 = logf - hi.astype(F32)
        mid = r1.astype(BF16)
        lo = (r1 - mid.astype(F32)).astype(BF16)
        cs = _dot_nn(hi, tri) + _dot_nn(mid, tri) + _dot_nn(lo, tri) + carry_ref[:, 0:1]
        o_ref[...] = -cs
        carry_ref[...] = jnp.zeros_like(carry_ref) + cs[:, tb - 1:tb]

    return pl.pallas_call(
        body, name=name, grid=(S // tb,),
        in_specs=[pl.BlockSpec((H, tb), lambda i: (0, i)), pl.BlockSpec((H, 1), lambda i: (0, 0))],
        out_specs=pl.BlockSpec((H, tb), lambda i: (0, i)),
        out_shape=jax.ShapeDtypeStruct((H, S), F32),
        scratch_shapes=[pltpu.VMEM((H, LANES), F32)],
        compiler_params=_cparams("arbitrary"),
    )(z_t, bias_col)


def _fox_post(drow, dcol, z_t, bias_col, *, name):
    H, S = z_t.shape
    tb = _pick(S, 512)
    nb = S // tb

    def body(dr_ref, d_ref, z_ref, b_ref, dz_ref, db_ref, carry_ref):
        @pl.when(pl.program_id(0) == 0)
        def _():
            carry_ref[...] = jnp.zeros_like(carry_ref)
            db_ref[...] = jnp.zeros_like(db_ref)

        dc = dr_ref[...] - d_ref[...]
        r = lax.broadcasted_iota(jnp.int32, (tb, tb), 0)
        c = lax.broadcasted_iota(jnp.int32, (tb, tb), 1)
        tri = (r >= c).astype(BF16)
        hi = dc.astype(BF16)
        r1 = dc - hi.astype(F32)
        mid = r1.astype(BF16)
        lo = (r1 - mid.astype(F32)).astype(BF16)
        dlogf = _dot_nn(hi, tri) + _dot_nn(mid, tri) + _dot_nn(lo, tri) + carry_ref[:, 0:1]
        carry_ref[...] = jnp.zeros_like(carry_ref) + dlogf[:, 0:1]
        dz = dlogf * _sigmoid(-(z_ref[...] + b_ref[...]))
        dz_ref[...] = dz
        db_ref[...] += jnp.sum(dz, axis=1, keepdims=True)

    rev = lambda i: (0, nb - 1 - i)
    return pl.pallas_call(
        body, name=name, grid=(nb,),
        in_specs=[pl.BlockSpec((H, tb), rev), pl.BlockSpec((H, tb), rev), pl.BlockSpec((H, tb), rev),
                  pl.BlockSpec((H, 1), lambda i: (0, 0))],
        out_specs=[pl.BlockSpec((H, tb), rev), pl.BlockSpec((H, LANES), lambda i: (0, 0))],
        out_shape=[jax.ShapeDtypeStruct((H, S), F32), jax.ShapeDtypeStruct((H, LANES), F32)],
        scratch_shapes=[pltpu.VMEM((H, LANES), F32)],
        compiler_params=_cparams("arbitrary"),
    )(drow, dcol, z_t, bias_col)


def _fox_blocks(S):
    cap = max(LANES, S // 4)
    return (min(FOX_FWD_BLOCKS[0], cap), min(FOX_FWD_BLOCKS[1], cap)), \
           (min(FOX_BWD_BLOCKS[0], cap), min(FOX_BWD_BLOCKS[1], cap))


def _key_bias_blocks(negc, bk):
    H, S = negc.shape
    return negc.reshape(H // 2, 2, S // bk, bk).transpose(0, 2, 1, 3)


def _fox_fwd(proj, negc4, *, q_off, k_off, v_off, bq, bk, name):
    S = proj.shape[0]
    nq, nk = S // bq, S // bk
    npair = FOX_HEADS // 2
    assert bq % bk == 0 or bk % bq == 0
    nmask = max(1, bq // bk)

    gp = FOX_FWD_PAIRS
    gw = gp * LANES
    assert q_off % gw == 0 and k_off % gw == 0 and v_off % gw == 0 and npair % gp == 0

    def body(q_ref, k_ref, v_ref, nc_ref, o_ref, lse_ref):
        i = pl.program_id(1)
        lane = lax.broadcasted_iota(jnp.int32, (1, LANES), 1)
        half = [lane < HEAD_DIM, lane >= HEAD_DIM]
        qh = []
        for g in range(gp):
            q2 = q_ref[:, g * LANES:(g + 1) * LANES] * jnp.asarray(ATT_SCALE, BF16)
            qh += [jnp.where(half[h], q2, jnp.zeros_like(q2)) for h in range(2)]
        row = lax.broadcasted_iota(jnp.int32, (bq, bk), 0)
        col = lax.broadcasted_iota(jnp.int32, (bq, bk), 1)
        rel = row - col
        nfull = (i * bq) // bk

        spare = [HEAD_DIM, 0]
        ones_lane = [lane == spare[h] for h in range(2)]

        def step(j, carry, masked):
            start = pl.multiple_of(j * bk, bk)
            new = []
            for g in range(gp):
                ks = k_ref[pl.ds(start, bk), g * LANES:(g + 1) * LANES]
                vs = v_ref[pl.ds(start, bk), g * LANES:(g + 1) * LANES]
                nb = nc_ref[g, j]
                for h in range(2):
                    m, acc = carry[4 * g + 2 * h:4 * g + 2 * h + 2]
                    vh = jnp.where(half[h], vs, jnp.where(ones_lane[h], jnp.ones_like(vs), jnp.zeros_like(vs)))
                    qs, bias = qh[2 * g + h], nb[h:h + 1, :]

                    def update(m, acc, rows, keys):
                        s = _dot_nt(qs[rows], ks[keys]) + bias[:, keys]
                        if masked:
                            s = jnp.where(rel[rows, keys] >= j * bk - i * bq, s, NEG)
                        m_new = jnp.maximum(m[rows], jnp.max(s, axis=1, keepdims=True))
                        p = jnp.exp(s - m_new).astype(BF16)
                        return m_new, jnp.exp(m[rows] - m_new) * acc[rows] + _dot_nn(p, vh[keys])

                    if masked and bq == bk:
                        top, bot, everything = slice(0, bq // 2), slice(bq // 2, bq), slice(0, bk)
                        m_t, acc_t = update(m, acc, top, top)
                        m_b, acc_b = update(m, acc, bot, everything)
                        new += [jnp.concatenate([m_t, m_b], axis=0), jnp.concatenate([acc_t, acc_b], axis=0)]
                    else:
                        new += list(update(m, acc, slice(0, bq), slice(0, bk)))
            return tuple(new)

        init = (jnp.full((bq, 1), NEG, F32), jnp.zeros((bq, LANES), F32)) * (2 * gp)
        carry = lax.fori_loop(0, nfull, lambda j, c: step(j, c, False), init)
        for t in range(nmask):
            carry = step(nfull + t, carry, True)
        for g in range(gp):
            outs, lses = [], []
            for h in range(2):
                m, acc = carry[4 * g + 2 * h:4 * g + 2 * h + 2]
                l = acc[:, spare[h]:spare[h] + 1]
                outs.append(acc * (1.0 / l))
                lses.append(m + jnp.log(l))
            o_ref[:, g * LANES:(g + 1) * LANES] = jnp.where(half[0], outs[0], outs[1]).astype(BF16)
            lse_ref[g] = jnp.where(half[0], lses[0], lses[1])

    seq = lambda off: pl.BlockSpec((S, gw), lambda hp, i: (0, off // gw + hp))
    return pl.pallas_call(
        body, name=name, grid=(npair // gp, nq),
        in_specs=[pl.BlockSpec((bq, gw), lambda hp, i: (i, q_off // gw + hp)), seq(k_off), seq(v_off),
                  pl.BlockSpec((gp, nk, 2, bk), lambda hp, i: (hp, 0, 0, 0))],
        out_specs=[pl.BlockSpec((bq, gw), lambda hp, i: (i, hp)),
                   pl.BlockSpec((gp, bq, LANES), lambda hp, i: (hp, i, 0))],
        out_shape=[jax.ShapeDtypeStruct((S, FOX_W), BF16), jax.ShapeDtypeStruct((npair, S, LANES), F32)],
        compiler_params=_cparams("parallel", "parallel"),
    )(proj, proj, proj, negc4)


def _fox_bwd(proj, negc4, o, lse, d_o, q_t, do_t, *, q_off, k_off, v_off, bq, bk, name, deps=()):
    S = proj.shape[0]
    nq, nk = S // bq, S // bk
    npair = FOX_HEADS // 2
    assert bq % bk == 0 or bk % bq == 0
    nmask = max(1, bk // bq)

    def body(q_ref, k_ref, v_ref, nc_ref, o_ref, lse_ref, do_ref, qt_ref, dot_ref, *rest):
        dqo_ref, dk_ref, dv_ref, dn_ref, dr_ref, delta_ref, rs_ref, dq_ref = rest[len(deps):]
        j = pl.program_id(1)
        lane = lax.broadcasted_iota(jnp.int32, (1, LANES), 1)
        half = [lane < HEAD_DIM, lane >= HEAD_DIM]
        spare = [HEAD_DIM, 0]
        ones_lane = [lane == spare[h] for h in range(2)]
        srow = lax.broadcasted_iota(jnp.int32, (LANES, 1), 0)
        rhalf = [srow < HEAD_DIM, srow >= HEAD_DIM]
        ones_row = [srow == spare[h] for h in range(2)]
        k2, v2 = k_ref[...], v_ref[...]
        one_k = jnp.ones_like(k2)
        kh = [jnp.where(half[h], k2, jnp.where(ones_lane[h], one_k, jnp.zeros_like(k2))) for h in range(2)]
        nb = nc_ref[0, 0]
        row = lax.broadcasted_iota(jnp.int32, (bq, bk), 0)
        col = lax.broadcasted_iota(jnp.int32, (bq, bk), 1)
        rel = row - col
        i_first = (j * bk) // bq

        @pl.when(j == 0)
        def _():
            dq_ref[...] = jnp.zeros_like(dq_ref)
            rs_ref[...] = jnp.zeros_like(rs_ref)
            for b in range(nq):
                prod = do_ref[b * bq:(b + 1) * bq, :].astype(F32) * o_ref[b * bq:(b + 1) * bq, :].astype(F32)
                d0 = jnp.sum(jnp.where(half[0], prod, 0.0), axis=1, keepdims=True)
                d1 = jnp.sum(jnp.where(half[1], prod, 0.0), axis=1, keepdims=True)
                delta_ref[b * bq:(b + 1) * bq, :] = jnp.where(half[0], d0, d1)

        def step(i, carry, masked, r0=0):
            dkt_a, dkt_b, dvt = carry
            dkts = [dkt_a, dkt_b]
            nr = bq - r0
            start = pl.multiple_of(i * bq + r0, LANES)
            q2 = q_ref[pl.ds(start, nr), :] * jnp.asarray(ATT_SCALE, BF16)
            do2 = do_ref[pl.ds(start, nr), :]
            qt = qt_ref[i][:, r0:] * jnp.asarray(ATT_SCALE, BF16)
            dot = dot_ref[i][:, r0:]
            lse2 = lse_ref[0, pl.ds(start, nr), :]
            del2 = delta_ref[pl.ds(start, nr), :]
            dqf = []
            for h in range(2):
                qm = jnp.where(half[h], q2, jnp.zeros_like(q2))
                dom = jnp.where(half[h], do2, jnp.zeros_like(do2))
                qtm = jnp.where(rhalf[h], qt, jnp.where(ones_row[h], jnp.ones_like(qt), jnp.zeros_like(qt)))
                dotm = jnp.where(rhalf[h], dot, jnp.zeros_like(dot))
                c0 = h * HEAD_DIM
                p = jnp.exp(_dot_nt(qm, k2) + nb[h:h + 1, :] - lse2[:, c0:c0 + 1])
                if masked:
                    p = jnp.where(rel[r0:] >= j * bk - i * bq, p, 0.0)
                dp = _dot_nt(dom, v2)
                dsb = (p * (dp - del2[:, c0:c0 + 1])).astype(BF16)
                dvt = dvt + _dot_nn(dotm, p.astype(BF16))
                dkts[h] = dkts[h] + _dot_nn(qtm, dsb)
                dqf.append(_dot_nn(dsb, kh[h]))
            dq_ref[pl.ds(start, nr), :] += jnp.where(half[0], dqf[0], dqf[1]) * ATT_SCALE
            rs_ref[pl.ds(start, nr), :] += jnp.where(ones_lane[0], dqf[0], jnp.where(ones_lane[1], dqf[1], 0.0))
            return dkts[0], dkts[1], dvt

        zero = jnp.zeros((LANES, bk), F32)
        carry = (zero, zero, zero)
        if bq > bk:
            sp = j % (bq // bk)
            carry = lax.switch(sp, [functools.partial(step, i_first, masked=True, r0=s * bk)
                                    for s in range(bq // bk)], carry)
        else:
            for t in range(nmask):
                carry = step(i_first + t, carry, True)
        dkt_a, dkt_b, dvt = lax.fori_loop(i_first + nmask, nq, lambda i, c: step(i, c, False), carry)
        dk_ref[...] = jnp.where(rhalf[0], dkt_a, dkt_b).T.astype(BF16)
        dv_ref[...] = dvt.T.astype(BF16)
        dn_ref[0, 0] = jnp.concatenate([dkt_a[spare[0]:spare[0] + 1], dkt_b[spare[1]:spare[1] + 1]], axis=0)

        @pl.when(j == nk - 1)
        def _():
            dqo_ref[...] = dq_ref[...].astype(BF16)
            for b in range(nq):
                t = rs_ref[b * bq:(b + 1) * bq, :].T
                dr_ref[0, b] = jnp.concatenate([t[spare[0]:spare[0] + 1], t[spare[1]:spare[1] + 1]], axis=0)

    once = pl.Buffered(1)
    seq = lambda off: pl.BlockSpec((S, LANES), lambda hp, j: (0, off // LANES + hp), pipeline_mode=once)
    blk = lambda off: pl.BlockSpec((bk, LANES), lambda hp, j: (j, off // LANES + hp))
    nc = pl.BlockSpec((1, 1, 2, bk), lambda hp, j: (hp, j, 0, 0))
    tsp = pl.BlockSpec((nq, LANES, bq), lambda hp, j: (0, hp, 0), pipeline_mode=once)
    return pl.pallas_call(
        body, name=name, grid=(npair, nk),
        in_specs=[seq(q_off), blk(k_off), blk(v_off), nc, seq(0),
                  pl.BlockSpec((1, S, LANES), lambda hp, j: (hp, 0, 0), pipeline_mode=once), seq(0),
                  tsp, tsp] + [_ANY] * len(deps),
        out_specs=[pl.BlockSpec((S, LANES), lambda hp, j: (0, hp)), blk(0), blk(0), nc,
                   pl.BlockSpec((1, nq, 2, bq), lambda hp, j: (hp, 0, 0, 0))],
        out_shape=[jax.ShapeDtypeStruct((S, FOX_W), BF16), jax.ShapeDtypeStruct((S, FOX_W), BF16),
                   jax.ShapeDtypeStruct((S, FOX_W), BF16), jax.ShapeDtypeStruct((npair, nk, 2, bk), F32),
                   jax.ShapeDtypeStruct((npair, nq, 2, bq), F32)],
        scratch_shapes=[pltpu.VMEM((S, LANES), F32), pltpu.VMEM((S, LANES), F32), pltpu.VMEM((S, LANES), F32)],
        compiler_params=_cparams("parallel", "arbitrary"),
    )(proj, proj, proj, negc4, o, lse, d_o, q_t, do_t, *deps)


def _exchange(arrs, *, gather, name):
    n = len(arrs)
    npeer = N_DEV - 1

    def body(*refs):
        ins, outs = refs[:n], refs[n:2 * n]
        send_sems, recv_sems, loc_sems = refs[2 * n:]
        x, y, c = lax.axis_index("x"), lax.axis_index("y"), lax.axis_index("c")
        me = 4 * x + 2 * y + c
        peers = []
        for k in range(1, N_DEV):
            px = 1 - x if k & 4 else x
            py = 1 - y if k & 2 else y
            pc = 1 - c if k & 1 else c
            peers.append(((px, py, pc), 4 * px + 2 * py + pc))

        def remote(w, k):
            dev, idx = peers[k]
            src = ins[w] if gather else ins[w].at[idx]
            return pltpu.make_async_remote_copy(
                src_ref=src, dst_ref=outs[w].at[me],
                send_sem=send_sems.at[w * npeer + k], recv_sem=recv_sems.at[w * npeer + k],
                device_id=dev, device_id_type=pl.DeviceIdType.MESH)

        def arrival(w, k):
            dev, idx = peers[k]
            src = ins[w] if gather else ins[w].at[idx]
            return pltpu.make_async_remote_copy(
                src_ref=src, dst_ref=outs[w].at[idx],
                send_sem=send_sems.at[w * npeer + k], recv_sem=recv_sems.at[w * npeer + k],
                device_id=dev, device_id_type=pl.DeviceIdType.MESH)

        local = []
        for w in range(n):
            for k in range(npeer):
                remote(w, k).start()
            cp = pltpu.make_async_copy(ins[w] if gather else ins[w].at[me], outs[w].at[me], loc_sems.at[w])
            cp.start()
            local.append(cp)
        for w in range(n):
            for k in range(npeer):
                arrival(w, k).wait_recv()
        for w in range(n):
            for k in range(npeer):
                remote(w, k).wait_send()
            local[w].wait()

    hbm = pl.BlockSpec(memory_space=pl.ANY)
    out_shape = [jax.ShapeDtypeStruct((N_DEV,) + (a.shape if gather else a.shape[1:]), a.dtype) for a in arrs]
    return pl.pallas_call(
        body, name=name,
        in_specs=[hbm] * n, out_specs=[hbm] * n, out_shape=out_shape,
        scratch_shapes=[pltpu.SemaphoreType.DMA((n * npeer,)), pltpu.SemaphoreType.DMA((n * npeer,)),
                        pltpu.SemaphoreType.DMA((n,))],
        compiler_params=pltpu.CompilerParams(has_side_effects=True),
    )(*arrs)


def _gather_two_level(shard, *, name):
    def body(x_ref, out_ref, send_sems, recv_sems, local_sem):
        x, y, c = lax.axis_index("x"), lax.axis_index("y"), lax.axis_index("c")
        me, sibling = (x, y, c), (x, y, 1 - c)
        chips = [(1 - x, y), (x, 1 - y), (1 - x, 1 - y)]

        def slot(px, py, pc):
            return out_ref.at[4 * px + 2 * py + pc]

        def copy(k, block, to, src=None):
            return pltpu.make_async_remote_copy(
                src_ref=slot(*block) if src is None else src, dst_ref=slot(*block),
                send_sem=send_sems.at[k], recv_sem=recv_sems.at[k],
                device_id=to, device_id_type=pl.DeviceIdType.MESH)

        mine = pltpu.make_async_copy(x_ref, slot(*me), local_sem)
        mine.start()
        first = [copy(0, me, sibling, src=x_ref)]
        first += [copy(1 + j, me, (*chip, c), src=x_ref) for j, chip in enumerate(chips)]
        for cp in first:
            cp.start()
        passed = [copy(4 + j, (*chip, c), sibling) for j, chip in enumerate(chips)]
        for j, chip in enumerate(chips):
            copy(1 + j, (*chip, c), me).wait_recv()
            passed[j].start()
        copy(0, sibling, me).wait_recv()
        for j, chip in enumerate(chips):
            copy(4 + j, (*chip, 1 - c), me).wait_recv()
        for cp in first + passed:
            cp.wait_send()
        mine.wait()

    return pl.pallas_call(
        body, name=name,
        in_specs=[_ANY], out_specs=_ANY,
        out_shape=jax.ShapeDtypeStruct((N_DEV,) + shard.shape, shard.dtype),
        scratch_shapes=[pltpu.SemaphoreType.DMA((N_DEV - 1,)), pltpu.SemaphoreType.DMA((N_DEV - 1,)),
                        pltpu.SemaphoreType.DMA],
        compiler_params=pltpu.CompilerParams(has_side_effects=True),
    )(shard)


_HBM = pl.BlockSpec(memory_space=pltpu.HBM)
_SEM = pl.BlockSpec(memory_space=pltpu.SEMAPHORE)
_EFFECT = pltpu.SideEffectType.DATAFLOW_SIDE_EFFECTING
NPEER = N_DEV - 1


def _peer_table():
    x, y, c = lax.axis_index("x"), lax.axis_index("y"), lax.axis_index("c")
    peers = []
    for k in range(1, N_DEV):
        px = 1 - x if k & 4 else x
        py = 1 - y if k & 2 else y
        pc = 1 - c if k & 1 else c
        peers.append(((px, py, pc), 4 * px + 2 * py + pc))
    return 4 * x + 2 * y + c, peers


def _split_copy(ins, lands, send_sems, recv_sems, gather, me, peers, w, k, arriving):
    dev, idx = peers[k]
    return pltpu.make_async_remote_copy(
        src_ref=ins[w] if gather else ins[w].at[idx],
        dst_ref=lands[w].at[idx if arriving else me],
        send_sem=send_sems.at[w * NPEER + k], recv_sem=recv_sems.at[w * NPEER + k],
        device_id=dev, device_id_type=pl.DeviceIdType.MESH)


def _exchange_start(arrs, *, gather, name, deps=()):
    n = len(arrs)
    land_shapes = [(N_DEV,) + (a.shape if gather else a.shape[1:]) for a in arrs]

    def body(*refs):
        ins, lands = refs[:n], refs[n:2 * n]
        send_sems, recv_sems = refs[2 * n + len(deps)], refs[2 * n + len(deps) + 1]
        token = refs[-1]
        me, peers = _peer_table()
        for w in range(n):
            for k in range(NPEER):
                _split_copy(ins, lands, send_sems, recv_sems, gather, me, peers, w, k, False).start()
        token[...] = jnp.zeros_like(token)

    out_shape = ([pltpu.SemaphoreType.DMA((n * NPEER,)), pltpu.SemaphoreType.DMA((n * NPEER,))]
                 + [pltpu.HBM(a.shape, a.dtype) for a in arrs]
                 + [pltpu.HBM(s, a.dtype) for s, a in zip(land_shapes, arrs)]
                 + [jax.ShapeDtypeStruct((8, LANES), F32)])
    res = pl.pallas_call(
        body, name=name,
        in_specs=[_HBM] * (2 * n) + [_ANY] * len(deps),
        out_specs=[_SEM, _SEM] + [_HBM] * (2 * n) + [pl.BlockSpec(memory_space=pltpu.VMEM)],
        out_shape=out_shape,
        input_output_aliases={i: 2 + i for i in range(2 * n)},
        compiler_params=pltpu.CompilerParams(has_side_effects=_EFFECT),
    )(*[pltpu.with_memory_space_constraint(a, pltpu.HBM) for a in arrs],
      *[pltpu.with_memory_space_constraint(lax.empty(s, a.dtype), pltpu.HBM) for s, a in zip(land_shapes, arrs)],
      *deps)
    return (n, gather, res[0], res[1], res[2:2 + n], res[2 + n:2 + 2 * n]), res[-1]


def _exchange_wait(handle, after, *, name):
    n, gather, send_sems, recv_sems, ins_thru, lands_thru = handle

    def body(*refs):
        ins, lands = refs[:n], refs[n:2 * n]
        send_s, recv_s = refs[2 * n], refs[2 * n + 1]
        me, peers = _peer_table()
        for w in range(n):
            for k in range(NPEER):
                _split_copy(ins, lands, send_s, recv_s, gather, me, peers, w, k, False).wait_send()
                _split_copy(ins, lands, send_s, recv_s, gather, me, peers, w, k, True).wait_recv()

    res = pl.pallas_call(
        body, name=name,
        in_specs=[_HBM] * (2 * n) + [_SEM, _SEM, pl.BlockSpec(memory_space=pl.ANY)],
        out_specs=[_HBM] * (2 * n),
        out_shape=[pltpu.HBM(a.shape, a.dtype) for a in list(ins_thru) + list(lands_thru)],
        input_output_aliases={i: i for i in range(2 * n)},
        compiler_params=pltpu.CompilerParams(has_side_effects=_EFFECT),
    )(*ins_thru, *lands_thru, send_sems, recv_sems, after)
    return res[:n], res[n:2 * n]


def _ordered_sum(s_ref, own_ref):
    if own_ref is None:
        blocks = [s_ref[q].astype(F32) for q in range(N_DEV)]
    else:
        me = 4 * lax.axis_index("x") + 2 * lax.axis_index("y") + lax.axis_index("c")
        own = own_ref[...]
        blocks = [jnp.where(me == q, own, s_ref[q]).astype(F32) for q in range(N_DEV)]
    acc = blocks[0]
    for b in blocks[1:]:
        acc = acc + b
    return acc


def _sum8(stack, own, *, name):
    _, R, C = stack.shape
    if R % 8 == 0:
        tr, tc = _pick(R, max(8, STEP_BYTES // (C * 4 * (N_DEV + 2))), 8), C
    else:
        tr, tc = R, _pick(C, max(LANES, STEP_BYTES // (R * 4 * (N_DEV + 2))))

    def body(s_ref, own_ref, o_ref):
        o_ref[...] = _ordered_sum(s_ref, own_ref)

    blk = pl.BlockSpec((tr, tc), lambda i, j: (i, j))
    return pl.pallas_call(
        body, name=name, grid=(R // tr, C // tc),
        in_specs=[pl.BlockSpec((N_DEV, tr, tc), lambda i, j: (0, i, j)), blk],
        out_specs=blk,
        out_shape=jax.ShapeDtypeStruct((R, C), F32),
        compiler_params=_cparams("parallel", "parallel"),
    )(stack, own)


def _adamw_math(w, g, m, v):
    m = ADAM_B1 * m + (1.0 - ADAM_B1) * g
    v = ADAM_B2 * v + (1.0 - ADAM_B2) * (g * g)
    m_hat = m / (1.0 - ADAM_B1 ** ADAM_STEP)
    v_hat = v / (1.0 - ADAM_B2 ** ADAM_STEP)
    delta = -ADAM_LR * (m_hat / (jnp.sqrt(v_hat) + ADAM_EPS) + ADAM_WD * w)
    return delta, m, v


def _adamw(w, g, m, v, *, name, stacked, own=None, transposed=False):
    R, C = w.shape
    if transposed:
        tr = _pick(R, max(LANES, STEP_BYTES // (C * 4 * (9 + N_DEV))))
    else:
        tr = _pick(R, max(8, STEP_BYTES // (C * 4 * (8 + (N_DEV if stacked else 1)))), 8)
    has_own = own is not None

    def body(w_ref, g_ref, m_ref, v_ref, *rest):
        go_ref, d_ref, mo_ref, vo_ref = rest[-4:]
        g = _ordered_sum(g_ref, rest[0] if has_own else None) if stacked else g_ref[...]
        if transposed:
            g = g.T
        delta, m2, v2 = _adamw_math(w_ref[...], g, m_ref[...], v_ref[...])
        go_ref[...] = g
        d_ref[...] = delta
        mo_ref[...] = m2
        vo_ref[...] = v2

    row = pl.BlockSpec((tr, C), lambda i: (i, 0))
    if transposed:
        g_spec, own_spec = pl.BlockSpec((N_DEV, C, tr), lambda i: (0, 0, i)), pl.BlockSpec((C, tr), lambda i: (0, i))
    else:
        g_spec, own_spec = (pl.BlockSpec((N_DEV, tr, C), lambda i: (0, i, 0)) if stacked else row), row
    return pl.pallas_call(
        body, name=name, grid=(R // tr,),
        in_specs=[row, g_spec, row, row] + [own_spec] * has_own, out_specs=[row] * 4,
        out_shape=[jax.ShapeDtypeStruct((R, C), F32)] * 4,
        compiler_params=_cparams("parallel"),
    )(w, g, m, v, *([own] if has_own else []))


def kernel(x, positions, attn_norm, w_in, fox_f_bias, swa_sinks, w_branch_swa, w_branch_fox, w_out, mlp_norm, w_up, w_down, final_norm, loss_target, m_attn_norm, m_w_in, m_fox_f_bias, m_swa_sinks, m_w_branch_swa, m_w_branch_fox, m_w_out, m_mlp_norm, m_w_up, m_w_down, m_final_norm, v_attn_norm, v_w_in, v_fox_f_bias, v_swa_sinks, v_w_branch_swa, v_w_branch_fox, v_w_out, v_mlp_norm, v_w_up, v_w_down, v_final_norm):
    S, D = x.shape[1], x.shape[2]
    DFF = w_up.shape[2] * N_DEV
    d_in = w_in.shape[2] * N_DEV
    assert d_in == QKV_W + FOX_HEADS + 2 * D and (2 * D) % SWA_Q_W == 0 and S % (4 * LANES) == 0
    q_off = 2 * D
    k_off = q_off + SWA_Q_W
    v_off = k_off + SWA_KV_W
    fq_off = v_off + SWA_KV_W
    fk_off = fq_off + FOX_W
    fv_off = fk_off + FOX_W
    fl_off = fv_off + FOX_W
    NP = fl_off + FL_PAD
    x2d, tgt = x[0], loss_target[0]

    shards = [w_in[0].T.astype(BF16), w_branch_swa[0].T.astype(BF16), w_branch_fox[0].T.astype(BF16),
              w_out[0].astype(BF16), w_up[0].T.astype(BF16), w_down[0].astype(BF16)]
    me = 4 * lax.axis_index("x") + 2 * lax.axis_index("y") + lax.axis_index("c")

    def filled(stack, own):
        return lax.dynamic_update_slice(stack, own[None], (me,) + (0,) * own.ndim)

    g_in = _gather_two_level(shards[0], name="gather_w_in")
    h_rest, tok_rest = _exchange_start(shards[1:], gather=True, name="gather_rest_start", deps=[g_in])

    tm = _pick(S, 1024)
    td = _pick(D, 1024)
    tf = _pick(DFF, 1024)
    tnp = _pick(NP, 1024)

    h1 = _rms_fwd(x2d, attn_norm, name="rms1", deps=[tok_rest])
    w_in_t = g_in.reshape(d_in, D)
    w_in_p = jnp.concatenate([w_in_t[QKV_W + FOX_HEADS:], w_in_t[:QKV_W], w_in_t[QKV_W:QKV_W + FOX_HEADS],
                              jnp.zeros((FL_PAD - FOX_HEADS, D), BF16)], axis=0)
    w_fl_t = w_in_t[QKV_W:QKV_W + FOX_HEADS]
    proj, = _matmul(h1, w_in_p, mode="nt", name="mm_in", out_dtypes=[BF16], tm=_pick(S, 2048), tn=tnp, tk=D)
    z_sd, = _matmul(h1, w_fl_t, mode="nt", name="mm_flogit", out_dtypes=[F32], tm=tm, tn=FOX_HEADS, tk=D)
    z_t = z_sd.T
    bias_col = fox_f_bias.reshape(FOX_HEADS, 1)
    negc = _fox_prep(z_t, bias_col, name="fox_prep")
    (fbq, fbk), (bbq, bbk) = _fox_blocks(S)
    inv_freq = ROPE_THETA ** (-jnp.arange(0, HEAD_DIM, 2, dtype=F32) / HEAD_DIM)
    invf = jnp.tile(inv_freq, LANES // (HEAD_DIM // 2)).reshape(1, LANES)
    cos_t, sin_t = _rope_tables(positions.reshape(S, 1), invf, name="rope_tables")
    q_rope, k_rope = _rope_fwd(proj, cos_t, sin_t, q_off=q_off, k_off=k_off, name="rope_fwd")
    sinks = swa_sinks.reshape(-1)
    swa_mask = _swa_mask_bias()
    o_a = _swa_fwd(q_rope, k_rope, proj, sinks, swa_mask, v_off=v_off, name="swa_fwd")
    o_b, lse = _fox_fwd(proj, _key_bias_blocks(negc, fbk), q_off=fq_off, k_off=fk_off, v_off=fv_off,
                        bq=fbq, bk=fbk, name="fox_fwd")
    s_rest, g_rest = _exchange_wait(h_rest, o_b, name="gather_rest_wait")
    g_bs, g_bf, g_o, g_up, g_dn = [filled(g, s) for g, s in zip(g_rest, s_rest)]
    w_bs_t = g_bs.reshape(D, SWA_Q_W)
    w_bf_t = g_bf.reshape(D, FOX_W)
    w_o = g_o.reshape(D, D)
    w_up_t = g_up.reshape(DFF, D)
    w_dn = g_dn.reshape(DFF, D)
    ya, = _matmul(o_a, w_bs_t, mode="nt", name="mm_branch_swa", out_dtypes=[BF16], tm=_pick(S, 512), tn=D, tk=SWA_Q_W)
    gate_maps = [lambda i, j, k: (i, j), lambda i, j, k: (i, j), lambda i, j, k: (i, j + D // td)]

    def merge_epi(acc, ya_t, ga_t, gb_t):
        merged = _sigmoid(ga_t.astype(F32)) * ya_t.astype(F32) + _sigmoid(gb_t.astype(F32)) * acc
        return acc, merged

    yb, merged = _matmul(o_b, w_bf_t, mode="nt", name="mm_branch_fox", out_dtypes=[BF16, BF16],
                         tm=tm, tn=td, tk=FOX_W, extras=[ya, proj, proj], extra_maps=gate_maps,
                         epilogue=merge_epi)
    def out_epi(acc, r, g):
        xm = acc + r
        rr = lax.rsqrt(jnp.mean(xm * xm, axis=-1, keepdims=True) + RMS_EPS)
        return xm, xm * rr * g

    x_mid, h2 = _matmul(merged, w_o, mode="nn", name="mm_out", out_dtypes=[F32, BF16], tm=_pick(S, 512), tn=D, tk=D,
                        extras=[x2d, mlp_norm], extra_maps=[lambda i, j, k: (i, j), lambda i, j, k: (0, 0)],
                        extra_shapes=[None, (1, D)], epilogue=out_epi)
    u, = _matmul(h2, w_up_t, mode="nt", name="mm_up", out_dtypes=[BF16], tm=_pick(S, 2048), tn=tf, tk=D,
                 epilogue=lambda acc: (jnp.maximum(acc, 0.0),))
    x_fin, = _matmul(u, w_dn, mode="nn", name="mm_down", out_dtypes=[F32], tm=tm, tn=td, tk=_pick(DFF, 2048),
                     a_fn=_square_bf16, extras=[x_mid], epilogue=lambda acc, r: (acc + r,))

    dx3b, dg3, loss_part = _loss_head(x_fin, tgt, final_norm.reshape(1, D), name="loss_head")
    d_up, = _matmul(dx3b, w_dn, mode="nt", name="mm_d_act", out_dtypes=[BF16], tm=_pick(S, 2048), tn=tf, tk=D,
                    extras=[u], epilogue=lambda acc, ut: (acc * (2.0 * ut.astype(F32)),))
    tks = _pick(S, 2048)
    dw_dn, = _matmul(u, dx3b, mode="tn", name="mm_dw_down", out_dtypes=[BF16], tm=tf, tn=td, tk=tks,
                     a_fn=_square_bf16)
    dh2, = _matmul(d_up, w_up_t, mode="nn", name="mm_dh2", out_dtypes=[BF16], tm=_pick(S, 512), tn=D,
                   tk=_pick(DFF, 2048))
    dw_up_t, = _matmul(d_up, h2, mode="tn", name="mm_dw_up", out_dtypes=[BF16], tm=tf, tn=td, tk=tks)
    h_s1, tok_s1 = _exchange_start([dw_up_t.reshape(N_DEV, DFF // N_DEV, D), dw_dn.reshape(N_DEV, DFF // N_DEV, D)],
                                   gather=False, name="scatter_mlp_start")
    dx2b, dg2 = _rms_bwd(dh2, x_mid, mlp_norm, dx3b, name="rms2_bwd", out_dtype=BF16, deps=[tok_s1])

    def gate_bwd_epi(dm, ya_t, yb_t, ga_t, gb_t):
        sa, sb = _sigmoid(ga_t.astype(F32)), _sigmoid(gb_t.astype(F32))
        d_a, d_b = dm * sa, dm * sb
        return (d_a, d_b, d_a * (ya_t.astype(F32) * (1.0 - sa)), d_b * (yb_t.astype(F32) * (1.0 - sb)))

    gmaps = [lambda i, j, k: (i, j), lambda i, j, k: (i, j), lambda i, j, k: (i, j),
             lambda i, j, k: (i, j + D // td)]
    d_ya, d_yb, d_ga, d_gb = _matmul(dx2b, w_o, mode="nt", name="mm_d_merged", out_dtypes=[BF16] * 4,
                                     tm=tm, tn=td, tk=D, extras=[ya, yb, proj, proj], extra_maps=gmaps,
                                     epilogue=gate_bwd_epi)
    dw_o, = _matmul(merged, dx2b, mode="tn", name="mm_dw_out", out_dtypes=[BF16], tm=td, tn=td, tk=tks)
    d_oa, = _matmul(d_ya, w_bs_t, mode="nn", name="mm_d_oa", out_dtypes=[BF16], tm=tm, tn=SWA_Q_W, tk=D)
    d_ob, = _matmul(d_yb, w_bf_t, mode="nn", name="mm_d_ob", out_dtypes=[BF16], tm=tm, tn=FOX_W, tk=D)
    tkl = _pick(S, 4096)
    dw_bs_t, = _matmul(d_ya, o_a, mode="tn", name="mm_dw_bs", out_dtypes=[BF16], tm=td, tn=SWA_Q_W, tk=tkl)
    dw_bf_t, = _matmul(d_yb, o_b, mode="tn", name="mm_dw_bf", out_dtypes=[BF16], tm=td, tn=FOX_W, tk=tkl)
    h_s2, tok_s2 = _exchange_start([dw_bs_t.reshape(N_DEV, D // N_DEV, SWA_Q_W),
                                    dw_bf_t.reshape(N_DEV, D // N_DEV, FOX_W), dw_o.reshape(N_DEV, D // N_DEV, D)],
                                   gather=False, name="scatter_attn_start")
    def row_blocks_t(a):
        return a.reshape(S // bbq, bbq, FOX_W).transpose(0, 2, 1)

    d_fq, d_fk, d_fv, dcol4, drow4 = _fox_bwd(proj, _key_bias_blocks(negc, bbk), o_b, lse, d_ob,
                                              row_blocks_t(proj[:, fq_off:fq_off + FOX_W]), row_blocks_t(d_ob),
                                              q_off=fq_off, k_off=fk_off, v_off=fv_off, bq=bbq, bk=bbk,
                                              name="fox_bwd", deps=[tok_s2])
    dcol = dcol4.transpose(0, 2, 1, 3).reshape(FOX_HEADS, S)
    drow = drow4.transpose(0, 2, 1, 3).reshape(FOX_HEADS, S)
    dz_t, dbias_l = _fox_post(drow, dcol, z_t, bias_col, name="fox_post")
    d_aq, dk_c, dk_p, dv_c, dv_p, dsink_l = _swa_bwd(q_rope, k_rope, proj, sinks, d_oa, cos_t, sin_t, swa_mask,
                                                     v_off=v_off, name="swa_bwd")
    d_ak, d_av = _rope_bwd(dk_c, dk_p, dv_c, dv_p, cos_t, sin_t, name="rope_bwd")
    dz_pad = jnp.pad(dz_t.T.astype(BF16), ((0, 0), (0, FL_PAD - FOX_HEADS)))
    d_proj = jnp.concatenate([d_ga, d_gb, d_aq, d_ak, d_av, d_fq, d_fk, d_fv, dz_pad], axis=1)
    tkp = _pick(NP, 2304)
    dw_in_p, = _matmul(d_proj, h1, mode="tn", name="mm_dw_in", out_dtypes=[BF16], tm=_pick(NP, 512), tn=D, tk=tks)
    dw_in_t = jnp.concatenate([dw_in_p[q_off:q_off + QKV_W], dw_in_p[fl_off:fl_off + FOX_HEADS], dw_in_p[:q_off]],
                              axis=0)
    h_s3, tok_s3 = _exchange_start([dw_in_t.reshape(N_DEV, d_in // N_DEV, D)], gather=False,
                                   name="scatter_in_start")
    dh1, = _matmul(d_proj, w_in_p, mode="nn", name="mm_dh1", out_dtypes=[BF16], tm=tm, tn=td, tk=tkp, deps=[tok_s3])
    dx, dg1 = _rms_bwd(dh1, x2d, attn_norm, dx2b, name="rms1_bwd", out_dtype=F32)

    dbias = dbias_l[:, 0]
    dsinks = dsink_l[:, :, 0].reshape(-1)
    nsm = 3 * D + 2 * LANES
    tail = jnp.zeros((2 * LANES,), F32)
    small_g = jnp.concatenate([dg1[0], dg2[0], dg3[0],
                               tail.at[0:16].set(dbias).at[16:32].set(dsinks).at[32].set(loss_part[0, 0])])

    def pack(a_norm, b_norm, f_norm, bias, snk):
        return jnp.concatenate([a_norm[0], b_norm[0], f_norm,
                                tail.at[0:16].set(bias[0]).at[16:32].set(snk[0])]).reshape(1, nsm)

    small_stack, = _exchange([small_g.reshape(1, nsm)], gather=True, name="gather_small")
    u_sm = _adamw(pack(attn_norm, mlp_norm, final_norm, fox_f_bias, swa_sinks), small_stack,
                  pack(m_attn_norm, m_mlp_norm, m_final_norm, m_fox_f_bias, m_swa_sinks),
                  pack(v_attn_norm, v_mlp_norm, v_final_norm, v_fox_f_bias, v_swa_sinks),
                  name="adamw_small", stacked=True)
    loss = u_sm[0][0, 3 * D + 32]

    def own_of(src):
        return lax.dynamic_index_in_dim(src, me, 0, keepdims=False)

    def update_t(stack, src, w, m, v, nm):
        g = _sum8(stack, own_of(src), name="sum_" + nm).T
        return _adamw(w[0], g, m[0], v[0], name="adamw_" + nm, stacked=False)

    def update(stack, src, w, m, v, nm, transposed=False):
        return _adamw(w[0], stack, m[0], v[0], name="adamw_" + nm, stacked=True, own=own_of(src),
                      transposed=transposed)

    (s_up, s_dn), (r_up, r_dn) = _exchange_wait(h_s1, u_sm[1], name="scatter_mlp_wait")
    u_up = update(r_up, s_up, w_up, m_w_up, v_w_up, "w_up", transposed=True)
    u_dn = update(r_dn, s_dn, w_down, m_w_down, v_w_down, "w_down")
    (s_bs, s_bf, s_o), (r_bs, r_bf, r_o) = _exchange_wait(h_s2, u_dn[1], name="scatter_attn_wait")
    u_bs = update(r_bs, s_bs, w_branch_swa, m_w_branch_swa, v_w_branch_swa, "w_bs", transposed=True)
    u_bf = update(r_bf, s_bf, w_branch_fox, m_w_branch_fox, v_w_branch_fox, "w_bf", transposed=True)
    u_o = update(r_o, s_o, w_out, m_w_out, v_w_out, "w_out")
    (s_w_in,), (r_in,) = _exchange_wait(h_s3, u_o[1], name="scatter_in_wait")
    u_in = update_t(r_in, s_w_in, w_in, m_w_in, v_w_in, "w_in")

    def small(kind):
        a = u_sm[kind][0]
        return dict(attn_norm=a[0:D][None], mlp_norm=a[D:2 * D][None], final_norm=a[2 * D:3 * D],
                    fox_f_bias=a[3 * D:3 * D + 16][None], swa_sinks=a[3 * D + 16:3 * D + 32][None])

    big = dict(w_in=u_in, w_branch_swa=u_bs, w_branch_fox=u_bf, w_out=u_o, w_up=u_up, w_down=u_dn)
    order = ["attn_norm", "w_in", "fox_f_bias", "swa_sinks", "w_branch_swa", "w_branch_fox", "w_out", "mlp_norm",
             "w_up", "w_down", "final_norm"]
    outs = [loss, dx[None]]
    for kind in range(4):
        sm = small(kind)
        for nm in order:
            outs.append(big[nm][kind][None] if nm in big else sm[nm])
    return tuple(outs)
```

```python
import functools

import jax
import jax.numpy as jnp
from jax import lax
from jax.experimental import pallas as pl
from jax.experimental.pallas import tpu as pltpu

F32 = jnp.float32
BF16 = jnp.bfloat16

N_DEV = 8
HEAD_DIM = 64
SWA_Q_W = 1024
SWA_KV_W = 128
SWA_GROUP = 8
WINDOW = 128
FOX_W = 1024
FOX_HEADS = 16
QKV_W = SWA_Q_W + 2 * SWA_KV_W + 3 * FOX_W
FL_PAD = 256
ROPE_THETA = 10000.0
RMS_EPS = 1e-6
ATT_SCALE = 0.125
NEG = -1e30

ADAM_LR = 0.001
ADAM_B1 = 0.9
ADAM_B2 = 0.999
ADAM_EPS = 1e-08
ADAM_WD = 0.01
ADAM_STEP = 10

FOX_FWD_BLOCKS = (1024, 1024)
FOX_BWD_BLOCKS = (1024, 512)
FOX_FWD_PAIRS = 2

LANES = 128
VMEM_LIMIT = 56 * 1024 * 1024
STEP_BYTES = 12 * 1024 * 1024


def _cparams(*sem):
    return pltpu.CompilerParams(dimension_semantics=sem, vmem_limit_bytes=VMEM_LIMIT)


def _pick(dim, pref, align=LANES):
    best = None
    t = align
    while t <= min(dim, pref):
        if dim % t == 0:
            best = t
        t += align
    return best if best is not None else dim


_DIMS = {"nn": ((1,), (0,)), "nt": ((1,), (1,)), "tn": ((0,), (0,))}


_ANY = pl.BlockSpec(memory_space=pl.ANY)


def _matmul(a, b, *, mode, name, out_dtypes, tm, tn, tk, extras=(), extra_maps=None, extra_shapes=None,
            a_fn=None, epilogue=None, deps=()):
    if mode == "nn":
        (M, K), (K2, N) = a.shape, b.shape
    elif mode == "nt":
        (M, K), (N, K2) = a.shape, b.shape
    else:
        (K, M), (K2, N) = a.shape, b.shape
    assert K == K2, (name, a.shape, b.shape)
    assert M % tm == 0 and N % tn == 0 and K % tk == 0, (name, M, N, K, tm, tn, tk)
    nk = K // tk
    ne, no = len(extras), len(out_dtypes)
    dims = (_DIMS[mode], ((), ()))

    def body(*refs):
        a_ref, b_ref = refs[0], refs[1]
        ex_refs = refs[2:2 + ne]
        out_refs = refs[2 + ne + len(deps):2 + ne + len(deps) + no]

        def finish(acc):
            res = (acc,) if epilogue is None else epilogue(acc, *[e[...] for e in ex_refs])
            for o_ref, r in zip(out_refs, res):
                o_ref[...] = r.astype(o_ref.dtype)

        def product():
            av = a_ref[...]
            if a_fn is not None:
                av = a_fn(av)
            return lax.dot_general(av, b_ref[...], dims, preferred_element_type=F32)

        if nk == 1:
            finish(product())
        else:
            acc_ref = refs[-1]
            k = pl.program_id(2)

            @pl.when(k == 0)
            def _():
                acc_ref[...] = jnp.zeros_like(acc_ref)

            acc_ref[...] += product()

            @pl.when(k == nk - 1)
            def _():
                finish(acc_ref[...])

    if mode == "tn":
        a_spec = pl.BlockSpec((tk, tm), lambda i, j, k: (k, i))
    else:
        a_spec = pl.BlockSpec((tm, tk), lambda i, j, k: (i, k))
    if mode == "nt":
        b_spec = pl.BlockSpec((tn, tk), lambda i, j, k: (j, k))
    else:
        b_spec = pl.BlockSpec((tk, tn), lambda i, j, k: (k, j))
    if extra_maps is None:
        extra_maps = [lambda i, j, k: (i, j)] * ne
    if extra_shapes is None:
        extra_shapes = [None] * ne
    ex_specs = [pl.BlockSpec(s or (tm, tn), m) for s, m in zip(extra_shapes, extra_maps)]
    out_spec = [pl.BlockSpec((tm, tn), lambda i, j, k: (i, j)) for _ in range(no)]
    res = pl.pallas_call(
        body,
        name=name,
        grid=(M // tm, N // tn, nk),
        in_specs=[a_spec, b_spec] + ex_specs + [_ANY] * len(deps),
        out_specs=out_spec,
        out_shape=[jax.ShapeDtypeStruct((M, N), d) for d in out_dtypes],
        scratch_shapes=[pltpu.VMEM((tm, tn), F32)] if nk > 1 else [],
        compiler_params=_cparams("parallel", "parallel", "arbitrary"),
    )(a, b, *extras, *deps)
    return res


def _square_bf16(t):
    tf = t.astype(F32)
    return (tf * tf).astype(BF16)


def _sigmoid(g):
    return 1.0 / (1.0 + jnp.exp(-g))


def _rms_fwd(x, gain, *, name, deps=()):
    S, D = x.shape
    tr = _pick(S, 512, 8)

    def body(x_ref, g_ref, *rest):
        h_ref = rest[-1]
        xv = x_ref[...]
        r = lax.rsqrt(jnp.mean(xv * xv, axis=-1, keepdims=True) + RMS_EPS)
        h_ref[...] = (xv * r * g_ref[...]).astype(BF16)

    return pl.pallas_call(
        body, name=name, grid=(S // tr,),
        in_specs=[pl.BlockSpec((tr, D), lambda i: (i, 0)), pl.BlockSpec((1, D), lambda i: (0, 0))] + [_ANY] * len(deps),
        out_specs=pl.BlockSpec((tr, D), lambda i: (i, 0)),
        out_shape=jax.ShapeDtypeStruct((S, D), BF16),
        compiler_params=_cparams("parallel"),
    )(x, gain, *deps)


def _rms_bwd(dh, x, gain, dres, *, name, out_dtype, deps=()):
    S, D = x.shape
    tr = _pick(S, 512, 8)

    def body(dh_ref, x_ref, g_ref, dres_ref, *rest):
        outs = rest[len(deps):]
        dx_ref, dg_ref = outs[0], outs[-1]
        xv = x_ref[...]
        r = lax.rsqrt(jnp.mean(xv * xv, axis=-1, keepdims=True) + RMS_EPS)
        xh = xv * r
        dhv = dh_ref[...].astype(F32)
        t = dhv * g_ref[...]
        dx = r * (t - xh * jnp.mean(t * xh, axis=-1, keepdims=True)) + dres_ref[...].astype(F32)
        dx_ref[...] = dx.astype(out_dtype)
        part = jnp.sum(dhv * xh, axis=0, keepdims=True)

        @pl.when(pl.program_id(0) == 0)
        def _():
            dg_ref[...] = part

        @pl.when(pl.program_id(0) > 0)
        def _():
            dg_ref[...] += part

    row = pl.BlockSpec((tr, D), lambda i: (i, 0))
    vec = pl.BlockSpec((1, D), lambda i: (0, 0))
    return pl.pallas_call(
        body, name=name, grid=(S // tr,),
        in_specs=[row, row, vec, row] + [_ANY] * len(deps), out_specs=[row, vec],
        out_shape=[jax.ShapeDtypeStruct((S, D), out_dtype), jax.ShapeDtypeStruct((1, D), F32)],
        compiler_params=_cparams("arbitrary"),
    )(dh, x, gain, dres, *deps)


def _loss_head(x3, target, gain, *, name):
    S, D = x3.shape
    tr = _pick(S, 512, 8)

    def body(x_ref, t_ref, g_ref, dxb_ref, dg_ref, loss_ref):
        xv = x_ref[...]
        r = lax.rsqrt(jnp.mean(xv * xv, axis=-1, keepdims=True) + RMS_EPS)
        xh = xv * r
        gv = g_ref[...]
        err = xh * gv - t_ref[...]
        lpart = jnp.zeros((1, LANES), F32) + (0.5 / D) * jnp.sum(err * err)
        dy = err * (1.0 / D)
        t = dy * gv
        dx = r * (t - xh * jnp.mean(t * xh, axis=-1, keepdims=True))
        dxb_ref[...] = dx.astype(BF16)
        part = jnp.sum(dy * xh, axis=0, keepdims=True)

        @pl.when(pl.program_id(0) == 0)
        def _():
            dg_ref[...] = part
            loss_ref[...] = lpart

        @pl.when(pl.program_id(0) > 0)
        def _():
            dg_ref[...] += part
            loss_ref[...] += lpart

    row = pl.BlockSpec((tr, D), lambda i: (i, 0))
    vec = pl.BlockSpec((1, D), lambda i: (0, 0))
    return pl.pallas_call(
        body, name=name, grid=(S // tr,),
        in_specs=[row, row, vec],
        out_specs=[row, vec, pl.BlockSpec((1, LANES), lambda i: (0, 0))],
        out_shape=[jax.ShapeDtypeStruct((S, D), BF16),
                   jax.ShapeDtypeStruct((1, D), F32), jax.ShapeDtypeStruct((1, LANES), F32)],
        compiler_params=_cparams("arbitrary"),
    )(x3, target, gain)


def _rope_tables(pos_col, invf, *, name):
    S = pos_col.shape[0]
    tr = _pick(S, 512, 8)

    def body(p_ref, f_ref, cos_ref, sin_ref):
        ang = p_ref[...].astype(F32) * f_ref[...]
        lane = lax.broadcasted_iota(jnp.int32, (1, LANES), 1)
        first = (lane % HEAD_DIM) < HEAD_DIM // 2
        sn = jnp.sin(ang)
        cos_ref[...] = jnp.cos(ang)
        sin_ref[...] = jnp.where(first, -sn, sn)

    return pl.pallas_call(
        body, name=name, grid=(S // tr,),
        in_specs=[pl.BlockSpec((tr, 1), lambda i: (i, 0)), pl.BlockSpec((1, LANES), lambda i: (0, 0))],
        out_specs=[pl.BlockSpec((tr, LANES), lambda i: (i, 0))] * 2,
        out_shape=[jax.ShapeDtypeStruct((S, LANES), F32)] * 2,
        compiler_params=_cparams("parallel"),
    )(pos_col, invf)


def _swap_halves(t):
    lane = lax.broadcasted_iota(jnp.int32, (1, LANES), 1)
    first = (lane % HEAD_DIM) < HEAD_DIM // 2
    return jnp.where(first, pltpu.roll(t, LANES - HEAD_DIM // 2, 1), pltpu.roll(t, HEAD_DIM // 2, 1))


def _rope_fwd(proj, cos_t, sin_t, *, q_off, k_off, name):
    S = proj.shape[0]
    tr = _pick(S, 512, 8)
    nqb = SWA_Q_W // LANES

    def body(q_ref, k_ref, c_ref, s_ref, qo_ref, ko_ref):
        cv, sv = c_ref[...], s_ref[...]
        for b in range(nqb):
            t = q_ref[:, b * LANES:(b + 1) * LANES].astype(F32)
            qo_ref[:, b * LANES:(b + 1) * LANES] = (t * cv + _swap_halves(t) * sv).astype(BF16)
        t = k_ref[...].astype(F32)
        ko_ref[...] = (t * cv + _swap_halves(t) * sv).astype(BF16)

    tab = pl.BlockSpec((tr, LANES), lambda i: (i, 0))
    return pl.pallas_call(
        body, name=name, grid=(S // tr,),
        in_specs=[pl.BlockSpec((tr, SWA_Q_W), lambda i: (i, q_off // SWA_Q_W)),
                  pl.BlockSpec((tr, LANES), lambda i: (i, k_off // LANES)), tab, tab],
        out_specs=[pl.BlockSpec((tr, SWA_Q_W), lambda i: (i, 0)), tab],
        out_shape=[jax.ShapeDtypeStruct((S, SWA_Q_W), BF16), jax.ShapeDtypeStruct((S, LANES), BF16)],
        compiler_params=_cparams("parallel"),
    )(proj, proj, cos_t, sin_t)


def _rope_bwd(dk_cur, dk_prev, dv_cur, dv_prev, cos_t, sin_t, *, name):
    S = dk_cur.shape[1]
    tr = _pick(S, 512)
    nb = S // tr

    def body(kc_ref, kp_ref, vc_ref, vp_ref, c_ref, s_ref, dko_ref, dvo_ref):
        cv, sv = c_ref[...], s_ref[...]
        row = pl.program_id(0) * tr + lax.broadcasted_iota(jnp.int32, (tr, 1), 0)
        has_next = row < S - WINDOW
        d = kc_ref[0] + kc_ref[1] + jnp.where(has_next, kp_ref[0] + kp_ref[1], 0.0)
        dko_ref[...] = (d * cv + _swap_halves(d * sv)).astype(BF16)
        dvo_ref[...] = (vc_ref[0] + vc_ref[1] + jnp.where(has_next, vp_ref[0] + vp_ref[1], 0.0)).astype(BF16)

    tab = pl.BlockSpec((tr, LANES), lambda i: (i, 0))
    cur = pl.BlockSpec((2, tr, LANES), lambda i: (0, i, 0))
    return pl.pallas_call(
        body, name=name, grid=(nb,),
        in_specs=[cur, cur, cur, cur, tab, tab],
        out_specs=[tab, tab],
        out_shape=[jax.ShapeDtypeStruct((S, LANES), BF16), jax.ShapeDtypeStruct((S, LANES), BF16)],
        compiler_params=_cparams("parallel"),
    )(dk_cur, dk_prev, dv_cur, dv_prev, cos_t, sin_t)


def _dot_nt(a, b):
    return lax.dot_general(a, b, (((1,), (1,)), ((), ())), preferred_element_type=F32)


def _dot_tn(a, b):
    return lax.dot_general(a, b, (((0,), (0,)), ((), ())), preferred_element_type=F32)


def _dot_nn(a, b):
    return lax.dot_general(a, b, (((1,), (0,)), ((), ())), preferred_element_type=F32)


def _roll_half(t):
    return pltpu.roll(t.astype(F32), HEAD_DIM, 1).astype(t.dtype)


SWA_STACK = SWA_GROUP // 2


def _swa_mask_bias():
    rows = SWA_STACK * WINDOW
    row = lax.broadcasted_iota(jnp.int32, (rows, 2 * WINDOW), 0) % WINDOW
    col = lax.broadcasted_iota(jnp.int32, (rows, 2 * WINDOW), 1)
    diff = row + WINDOW - col
    window = (diff >= 0) & (diff < WINDOW)
    return jnp.stack([jnp.where(window & (col >= WINDOW), 0.0, NEG), jnp.where(window, 0.0, NEG)]).astype(F32)


def _swa_common(kp_ref, kc_ref, vp_ref, vc_ref):
    k2 = jnp.concatenate([kp_ref[...], kc_ref[...]], axis=0)
    v2 = jnp.concatenate([vp_ref[...], vc_ref[...]], axis=0)
    k_sw, v_sw = _roll_half(k2), _roll_half(v2)
    lane = lax.broadcasted_iota(jnp.int32, (1, LANES), 1)
    half = [lane < HEAD_DIM, lane >= HEAD_DIM]
    kk = [[k2 if hk == a else k_sw for a in range(2)] for hk in range(2)]
    vv = [[v2 if hk == a else v_sw for a in range(2)] for hk in range(2)]
    return half, kk, vv


def _swa_stack(ref, hk, mask, scale=None):
    parts = []
    for t in range(SWA_STACK):
        blk = ref[:, (hk * SWA_STACK + t) * LANES:(hk * SWA_STACK + t + 1) * LANES]
        if scale is not None:
            blk = blk * jnp.asarray(scale, blk.dtype)
        parts.append(jnp.where(mask, blk, jnp.zeros_like(blk)))
    return jnp.concatenate(parts, axis=0)


def _swa_sink_column(sink_ref, hk, a):
    blk = lax.broadcasted_iota(jnp.int32, (SWA_STACK * WINDOW, 1), 0) // WINDOW
    col = jnp.zeros((SWA_STACK * WINDOW, 1), F32)
    for t in range(SWA_STACK):
        col = jnp.where(blk == t, sink_ref[hk * SWA_GROUP + 2 * t + a], col)
    return col


def _swa_probs(qm, kk, mask_bias, sink):
    s = _dot_nt(qm, kk) + mask_bias
    m = jnp.maximum(jnp.max(s, axis=1, keepdims=True), sink)
    e = jnp.exp(s - m)
    es = jnp.exp(sink - m)
    inv = 1.0 / (jnp.sum(e, axis=1, keepdims=True) + es)
    return e * inv, es * inv


def _swa_mask_spec():
    return pl.BlockSpec((1, SWA_STACK * WINDOW, 2 * WINDOW), lambda n: (jnp.minimum(n, 1), 0, 0))


def _swa_fwd(q_rope, k_rope, proj, sinks, mask_bias, *, v_off, name):
    S = q_rope.shape[0]
    nb = S // WINDOW

    def body(sink_ref, q_ref, kp_ref, kc_ref, vp_ref, vc_ref, mask_ref, o_ref):
        half, kk, vv = _swa_common(kp_ref, kc_ref, vp_ref, vc_ref)
        for hk in range(2):
            outs = []
            for a in range(2):
                qm = _swa_stack(q_ref, hk, half[a], ATT_SCALE)
                p, _ = _swa_probs(qm, kk[hk][a], mask_ref[0], _swa_sink_column(sink_ref, hk, a))
                outs.append(_dot_nn(p.astype(BF16), vv[hk][a]))
            for t in range(SWA_STACK):
                rows = slice(t * WINDOW, (t + 1) * WINDOW)
                c0 = (hk * SWA_STACK + t) * LANES
                o_ref[:, c0:c0 + LANES] = jnp.where(half[0], outs[0][rows], outs[1][rows]).astype(BF16)

    prev = lambda n: (jnp.maximum(n - 1, 0), 0)
    cur = lambda n: (n, 0)
    vprev = lambda n: (jnp.maximum(n - 1, 0), v_off // LANES)
    vcur = lambda n: (n, v_off // LANES)
    blk = lambda m: pl.BlockSpec((WINDOW, LANES), m)
    return pl.pallas_call(
        body, name=name, grid=(nb,),
        in_specs=[pl.BlockSpec(memory_space=pltpu.SMEM),
                  pl.BlockSpec((WINDOW, SWA_Q_W), lambda n: (n, 0)),
                  blk(prev), blk(cur), blk(vprev), blk(vcur), _swa_mask_spec()],
        out_specs=pl.BlockSpec((WINDOW, SWA_Q_W), lambda n: (n, 0)),
        out_shape=jax.ShapeDtypeStruct((S, SWA_Q_W), BF16),
        compiler_params=_cparams("parallel"),
    )(sinks, q_rope, k_rope, k_rope, proj, proj, mask_bias)


def _swa_bwd(q_rope, k_rope, proj, sinks, d_o, cos_t, sin_t, mask_bias, *, v_off, name):
    S = q_rope.shape[0]
    nb = S // WINDOW

    def body(sink_ref, q_ref, kp_ref, kc_ref, vp_ref, vc_ref, do_ref, c_ref, s_ref, mask_ref,
             dq_ref, dkc_ref, dkp_ref, dvc_ref, dvp_ref, dsink_ref):
        n = pl.program_id(0)
        half, kk, vv = _swa_common(kp_ref, kc_ref, vp_ref, vc_ref)
        allowed = mask_ref[0]
        cv, sv = c_ref[...], s_ref[...]
        srow = lax.broadcasted_iota(jnp.int32, (SWA_GROUP, LANES), 0)
        for hk in range(2):
            dk_acc = jnp.zeros((2 * WINDOW, LANES), F32)
            dv_acc = jnp.zeros((2 * WINDOW, LANES), F32)
            dsink = jnp.zeros((SWA_GROUP, LANES), F32)
            dqs = []
            for a in range(2):
                qm = _swa_stack(q_ref, hk, half[a], ATT_SCALE)
                dom = _swa_stack(do_ref, hk, half[a])
                p, psink = _swa_probs(qm, kk[hk][a], allowed, _swa_sink_column(sink_ref, hk, a))
                dp = _dot_nt(dom, vv[hk][a])
                delta = jnp.sum(p * dp, axis=1, keepdims=True)
                ds = (p * (dp - delta)).astype(BF16)
                dsk = psink * delta
                for t in range(SWA_STACK):
                    dsink = dsink + jnp.where(srow == 2 * t + a, -jnp.sum(dsk[t * WINDOW:(t + 1) * WINDOW]), 0.0)
                dqs.append(_dot_nn(ds, kk[hk][a]) * ATT_SCALE)
                dk_acc = dk_acc + _dot_tn(ds, qm)
                dv_acc = dv_acc + _dot_tn(p.astype(BF16), dom)
            for t in range(SWA_STACK):
                rows = slice(t * WINDOW, (t + 1) * WINDOW)
                d = jnp.where(half[0], dqs[0][rows], dqs[1][rows])
                c0 = (hk * SWA_STACK + t) * LANES
                dq_ref[:, c0:c0 + LANES] = (d * cv + _swap_halves(d * sv)).astype(BF16)
            dk_t = jnp.where(half[hk], dk_acc + pltpu.roll(dk_acc, HEAD_DIM, 1), 0.0)
            dv_t = jnp.where(half[hk], dv_acc + pltpu.roll(dv_acc, HEAD_DIM, 1), 0.0)
            dkp_ref[hk] = dk_t[:WINDOW]
            dkc_ref[hk] = dk_t[WINDOW:]
            dvp_ref[hk] = dv_t[:WINDOW]
            dvc_ref[hk] = dv_t[WINDOW:]

            @pl.when(n == 0)
            def _():
                dsink_ref[hk] = dsink

            @pl.when(n > 0)
            def _():
                dsink_ref[hk] += dsink

    prev = lambda n: (jnp.maximum(n - 1, 0), 0)
    cur = lambda n: (n, 0)
    vprev = lambda n: (jnp.maximum(n - 1, 0), v_off // LANES)
    vcur = lambda n: (n, v_off // LANES)
    blk = lambda m: pl.BlockSpec((WINDOW, LANES), m)
    qblk = pl.BlockSpec((WINDOW, SWA_Q_W), lambda n: (n, 0))
    part = pl.BlockSpec((2, WINDOW, LANES), lambda n: (0, n, 0))
    part_prev = pl.BlockSpec((2, WINDOW, LANES), lambda n: (0, jnp.maximum(n - 1, 0), 0))
    part_shape = jax.ShapeDtypeStruct((2, S, LANES), F32)
    return pl.pallas_call(
        body, name=name, grid=(nb,),
        in_specs=[pl.BlockSpec(memory_space=pltpu.SMEM), qblk, blk(prev), blk(cur), blk(vprev), blk(vcur), qblk,
                  blk(cur), blk(cur), _swa_mask_spec()],
        out_specs=[qblk, part, part_prev, part, part_prev,
                   pl.BlockSpec((2, SWA_GROUP, LANES), lambda n: (0, 0, 0))],
        out_shape=[jax.ShapeDtypeStruct((S, SWA_Q_W), BF16), part_shape, part_shape, part_shape, part_shape,
                   jax.ShapeDtypeStruct((2, SWA_GROUP, LANES), F32)],
        compiler_params=_cparams("arbitrary"),
    )(sinks, q_rope, k_rope, k_rope, proj, proj, d_o, cos_t, sin_t, mask_bias)


def _fox_prep(z_t, bias_col, *, name):
    H, S = z_t.shape
    tb = _pick(S, 512)

    def body(z_ref, b_ref, o_ref, carry_ref):
        @pl.when(pl.program_id(0) == 0)
        def _():
            carry_ref[...] = jnp.zeros_like(carry_ref)

        zz = z_ref[...] + b_ref[...]
        t = jnp.exp(-jnp.abs(zz))
        log1p = jnp.where(t < 1e-2, t * (1.0 - t * (0.5 - t * (1.0 / 3.0))), jnp.log(1.0 + t))
        logf = jnp.minimum(zz, 0.0) - log1p
        r = lax.broadcasted_iota(jnp.int32, (tb, tb), 0)
        c = lax.broadcasted_iota(jnp.int32, (tb, tb), 1)
        tri = (r <= c).astype(BF16)
        hi = logf.astype(BF16)
        r1 = logf - hi.astype(F32)
        mid = r1.astype(BF16)
        lo = (r1 - mid.astype(F32)).astype(BF16)
        cs = _dot_nn(hi, tri) + _dot_nn(mid, tri) + _dot_nn(lo, tri) + carry_ref[:, 0:1]
        o_ref[...] = -cs
        carry_ref[...] = jnp.zeros_like(carry_ref) + cs[:, tb - 1:tb]

    return pl.pallas_call(
        body, name=name, grid=(S // tb,),
        in_specs=[pl.BlockSpec((H, tb), lambda i: (0, i)), pl.BlockSpec((H, 1), lambda i: (0, 0))],
        out_specs=pl.BlockSpec((H, tb), lambda i: (0, i)),
        out_shape=jax.ShapeDtypeStruct((H, S), F32),
        scratch_shapes=[pltpu.VMEM((H, LANES), F32)],
        compiler_params=_cparams("arbitrary"),
    )(z_t, bias_col)


def _fox_post(drow, dcol, z_t, bias_col, *, name):
    H, S = z_t.shape
    tb = _pick(S, 512)
    nb = S // tb

    def body(dr_ref, d_ref, z_ref, b_ref, dz_ref, db_ref, carry_ref):
        @pl.when(pl.program_id(0) == 0)
        def _():
            carry_ref[...] = jnp.zeros_like(carry_ref)
            db_ref[...] = jnp.zeros_like(db_ref)

        dc = dr_ref[...] - d_ref[...]
        r = lax.broadcasted_iota(jnp.int32, (tb, tb), 0)
        c = lax.broadcasted_iota(jnp.int32, (tb, tb), 1)
        tri = (r >= c).astype(BF16)
        hi = dc.astype(BF16)
        r1 = dc - hi.astype(F32)
        mid = r1.astype(BF16)
        lo = (r1 - mid.astype(F32)).astype(BF16)
        dlogf = _dot_nn(hi, tri) + _dot_nn(mid, tri) + _dot_nn(lo, tri) + carry_ref[:, 0:1]
        carry_ref[...] = jnp.zeros_like(carry_ref) + dlogf[:, 0:1]
        dz = dlogf * _sigmoid(-(z_ref[...] + b_ref[...]))
        dz_ref[...] = dz
        db_ref[...] += jnp.sum(dz, axis=1, keepdims=True)

    rev = lambda i: (0, nb - 1 - i)
    return pl.pallas_call(
        body, name=name, grid=(nb,),
        in_specs=[pl.BlockSpec((H, tb), rev), pl.BlockSpec((H, tb), rev), pl.BlockSpec((H, tb), rev),
                  pl.BlockSpec((H, 1), lambda i: (0, 0))],
        out_specs=[pl.BlockSpec((H, tb), rev), pl.BlockSpec((H, LANES), lambda i: (0, 0))],
        out_shape=[jax.ShapeDtypeStruct((H, S), F32), jax.ShapeDtypeStruct((H, LANES), F32)],
        scratch_shapes=[pltpu.VMEM((H, LANES), F32)],
        compiler_params=_cparams("arbitrary"),
    )(drow, dcol, z_t, bias_col)


def _fox_blocks(S):
    cap = max(LANES, S // 4)
    return (min(FOX_FWD_BLOCKS[0], cap), min(FOX_FWD_BLOCKS[1], cap)), \
           (min(FOX_BWD_BLOCKS[0], cap), min(FOX_BWD_BLOCKS[1], cap))


def _key_bias_blocks(negc, bk):
    H, S = negc.shape
    return negc.reshape(H // 2, 2, S // bk, bk).transpose(0, 2, 1, 3)


def _fox_fwd(proj, negc4, *, q_off, k_off, v_off, bq, bk, name):
    S = proj.shape[0]
    nq, nk = S // bq, S // bk
    npair = FOX_HEADS // 2
    assert bq % bk == 0 or bk % bq == 0
    nmask = max(1, bq // bk)

    gp = FOX_FWD_PAIRS
    gw = gp * LANES
    assert q_off % gw == 0 and k_off % gw == 0 and v_off % gw == 0 and npair % gp == 0

    def body(q_ref, k_ref, v_ref, nc_ref, o_ref, lse_ref):
        i = pl.program_id(1)
        lane = lax.broadcasted_iota(jnp.int32, (1, LANES), 1)
        half = [lane < HEAD_DIM, lane >= HEAD_DIM]
        qh = []
        for g in range(gp):
            q2 = q_ref[:, g * LANES:(g + 1) * LANES] * jnp.asarray(ATT_SCALE, BF16)
            qh += [jnp.where(half[h], q2, jnp.zeros_like(q2)) for h in range(2)]
        row = lax.broadcasted_iota(jnp.int32, (bq, bk), 0)
        col = lax.broadcasted_iota(jnp.int32, (bq, bk), 1)
        rel = row - col
        nfull = (i * bq) // bk

        spare = [HEAD_DIM, 0]
        ones_lane = [lane == spare[h] for h in range(2)]

        def step(j, carry, masked):
            start = pl.multiple_of(j * bk, bk)
            new = []
            for g in range(gp):
                ks = k_ref[pl.ds(start, bk), g * LANES:(g + 1) * LANES]
                vs = v_ref[pl.ds(start, bk), g * LANES:(g + 1) * LANES]
                nb = nc_ref[g, j]
                for h in range(2):
                    m, acc = carry[4 * g + 2 * h:4 * g + 2 * h + 2]
                    vh = jnp.where(half[h], vs, jnp.where(ones_lane[h], jnp.ones_like(vs), jnp.zeros_like(vs)))
                    qs, bias = qh[2 * g + h], nb[h:h + 1, :]

                    def update(m, acc, rows, keys):
                        s = _dot_nt(qs[rows], ks[keys]) + bias[:, keys]
                        if masked:
                            s = jnp.where(rel[rows, keys] >= j * bk - i * bq, s, NEG)
                        m_new = jnp.maximum(m[rows], jnp.max(s, axis=1, keepdims=True))
                        p = jnp.exp(s - m_new).astype(BF16)
                        return m_new, jnp.exp(m[rows] - m_new) * acc[rows] + _dot_nn(p, vh[keys])

                    if masked and bq == bk:
                        top, bot, everything = slice(0, bq // 2), slice(bq // 2, bq), slice(0, bk)
                        m_t, acc_t = update(m, acc, top, top)
                        m_b, acc_b = update(m, acc, bot, everything)
                        new += [jnp.concatenate([m_t, m_b], axis=0), jnp.concatenate([acc_t, acc_b], axis=0)]
                    else:
                        new += list(update(m, acc, slice(0, bq), slice(0, bk)))
            return tuple(new)

        init = (jnp.full((bq, 1), NEG, F32), jnp.zeros((bq, LANES), F32)) * (2 * gp)
        carry = lax.fori_loop(0, nfull, lambda j, c: step(j, c, False), init)
        for t in range(nmask):
            carry = step(nfull + t, carry, True)
        for g in range(gp):
            outs, lses = [], []
            for h in range(2):
                m, acc = carry[4 * g + 2 * h:4 * g + 2 * h + 2]
                l = acc[:, spare[h]:spare[h] + 1]
                outs.append(acc * (1.0 / l))
                lses.append(m + jnp.log(l))
            o_ref[:, g * LANES:(g + 1) * LANES] = jnp.where(half[0], outs[0], outs[1]).astype(BF16)
            lse_ref[g] = jnp.where(half[0], lses[0], lses[1])

    seq = lambda off: pl.BlockSpec((S, gw), lambda hp, i: (0, off // gw + hp))
    return pl.pallas_call(
        body, name=name, grid=(npair // gp, nq),
        in_specs=[pl.BlockSpec((bq, gw), lambda hp, i: (i, q_off // gw + hp)), seq(k_off), seq(v_off),
                  pl.BlockSpec((gp, nk, 2, bk), lambda hp, i: (hp, 0, 0, 0))],
        out_specs=[pl.BlockSpec((bq, gw), lambda hp, i: (i, hp)),
                   pl.BlockSpec((gp, bq, LANES), lambda hp, i: (hp, i, 0))],
        out_shape=[jax.ShapeDtypeStruct((S, FOX_W), BF16), jax.ShapeDtypeStruct((npair, S, LANES), F32)],
        compiler_params=_cparams("parallel", "parallel"),
    )(proj, proj, proj, negc4)


def _fox_bwd(proj, negc4, o, lse, d_o, q_t, do_t, *, q_off, k_off, v_off, bq, bk, name, deps=()):
    S = proj.shape[0]
    nq, nk = S // bq, S // bk
    npair = FOX_HEADS // 2
    assert bq % bk == 0 or bk % bq == 0
    nmask = max(1, bk // bq)

    def body(q_ref, k_ref, v_ref, nc_ref, o_ref, lse_ref, do_ref, qt_ref, dot_ref, *rest):
        dqo_ref, dk_ref, dv_ref, dn_ref, dr_ref, delta_ref, rs_ref, dq_ref = rest[len(deps):]
        j = pl.program_id(1)
        lane = lax.broadcasted_iota(jnp.int32, (1, LANES), 1)
        half = [lane < HEAD_DIM, lane >= HEAD_DIM]
        spare = [HEAD_DIM, 0]
        ones_lane = [lane == spare[h] for h in range(2)]
        srow = lax.broadcasted_iota(jnp.int32, (LANES, 1), 0)
        rhalf = [srow < HEAD_DIM, srow >= HEAD_DIM]
        ones_row = [srow == spare[h] for h in range(2)]
        k2, v2 = k_ref[...], v_ref[...]
        one_k = jnp.ones_like(k2)
        kh = [jnp.where(half[h], k2, jnp.where(ones_lane[h], one_k, jnp.zeros_like(k2))) for h in range(2)]
        nb = nc_ref[0, 0]
        row = lax.broadcasted_iota(jnp.int32, (bq, bk), 0)
        col = lax.broadcasted_iota(jnp.int32, (bq, bk), 1)
        rel = row - col
        i_first = (j * bk) // bq

        @pl.when(j == 0)
        def _():
            dq_ref[...] = jnp.zeros_like(dq_ref)
            rs_ref[...] = jnp.zeros_like(rs_ref)
            for b in range(nq):
                prod = do_ref[b * bq:(b + 1) * bq, :].astype(F32) * o_ref[b * bq:(b + 1) * bq, :].astype(F32)
                d0 = jnp.sum(jnp.where(half[0], prod, 0.0), axis=1, keepdims=True)
                d1 = jnp.sum(jnp.where(half[1], prod, 0.0), axis=1, keepdims=True)
                delta_ref[b * bq:(b + 1) * bq, :] = jnp.where(half[0], d0, d1)

        def step(i, carry, masked, r0=0):
            dkt_a, dkt_b, dvt = carry
            dkts = [dkt_a, dkt_b]
            nr = bq - r0
            start = pl.multiple_of(i * bq + r0, LANES)
            q2 = q_ref[pl.ds(start, nr), :] * jnp.asarray(ATT_SCALE, BF16)
            do2 = do_ref[pl.ds(start, nr), :]
            qt = qt_ref[i][:, r0:] * jnp.asarray(ATT_SCALE, BF16)
            dot = dot_ref[i][:, r0:]
            lse2 = lse_ref[0, pl.ds(start, nr), :]
            del2 = delta_ref[pl.ds(start, nr), :]
            dqf = []
            for h in range(2):
                qm = jnp.where(half[h], q2, jnp.zeros_like(q2))
                dom = jnp.where(half[h], do2, jnp.zeros_like(do2))
                qtm = jnp.where(rhalf[h], qt, jnp.where(ones_row[h], jnp.ones_like(qt), jnp.zeros_like(qt)))
                dotm = jnp.where(rhalf[h], dot, jnp.zeros_like(dot))
                c0 = h * HEAD_DIM
                p = jnp.exp(_dot_nt(qm, k2) + nb[h:h + 1, :] - lse2[:, c0:c0 + 1])
                if masked:
                    p = jnp.where(rel[r0:] >= j * bk - i * bq, p, 0.0)
                dp = _dot_nt(dom, v2)
                dsb = (p * (dp - del2[:, c0:c0 + 1])).astype(BF16)
                dvt = dvt + _dot_nn(dotm, p.astype(BF16))
                dkts[h] = dkts[h] + _dot_nn(qtm, dsb)
                dqf.append(_dot_nn(dsb, kh[h]))
            dq_ref[pl.ds(start, nr), :] += jnp.where(half[0], dqf[0], dqf[1]) * ATT_SCALE
            rs_ref[pl.ds(start, nr), :] += jnp.where(ones_lane[0], dqf[0], jnp.where(ones_lane[1], dqf[1], 0.0))
            return dkts[0], dkts[1], dvt

        zero = jnp.zeros((LANES, bk), F32)
        carry = (zero, zero, zero)
        if bq > bk:
            sp = j % (bq // bk)
            carry = lax.switch(sp, [functools.partial(step, i_first, masked=True, r0=s * bk)
                                    for s in range(bq // bk)], carry)
        else:
            for t in range(nmask):
                carry = step(i_first + t, carry, True)
        dkt_a, dkt_b, dvt = lax.fori_loop(i_first + nmask, nq, lambda i, c: step(i, c, False), carry)
        dk_ref[...] = jnp.where(rhalf[0], dkt_a, dkt_b).T.astype(BF16)
        dv_ref[...] = dvt.T.astype(BF16)
        dn_ref[0, 0] = jnp.concatenate([dkt_a[spare[0]:spare[0] + 1], dkt_b[spare[1]:spare[1] + 1]], axis=0)

        @pl.when(j == nk - 1)
        def _():
            dqo_ref[...] = dq_ref[...].astype(BF16)
            for b in range(nq):
                t = rs_ref[b * bq:(b + 1) * bq, :].T
                dr_ref[0, b] = jnp.concatenate([t[spare[0]:spare[0] + 1], t[spare[1]:spare[1] + 1]], axis=0)

    once = pl.Buffered(1)
    ahead = pl.Buffered(2)
    seq = lambda off: pl.BlockSpec((S, LANES), lambda hp, j: (0, off // LANES + hp), pipeline_mode=ahead)
    blk = lambda off: pl.BlockSpec((bk, LANES), lambda hp, j: (j, off // LANES + hp))
    nc = pl.BlockSpec((1, 1, 2, bk), lambda hp, j: (hp, j, 0, 0))
    tsp = pl.BlockSpec((nq, LANES, bq), lambda hp, j: (0, hp, 0), pipeline_mode=ahead)
    return pl.pallas_call(
        body, name=name, grid=(npair, nk),
        in_specs=[seq(q_off), blk(k_off), blk(v_off), nc, seq(0),
                  pl.BlockSpec((1, S, LANES), lambda hp, j: (hp, 0, 0), pipeline_mode=once), seq(0),
                  tsp, tsp] + [_ANY] * len(deps),
        out_specs=[pl.BlockSpec((S, LANES), lambda hp, j: (0, hp)), blk(0), blk(0), nc,
                   pl.BlockSpec((1, nq, 2, bq), lambda hp, j: (hp, 0, 0, 0))],
        out_shape=[jax.ShapeDtypeStruct((S, FOX_W), BF16), jax.ShapeDtypeStruct((S, FOX_W), BF16),
                   jax.ShapeDtypeStruct((S, FOX_W), BF16), jax.ShapeDtypeStruct((npair, nk, 2, bk), F32),
                   jax.ShapeDtypeStruct((npair, nq, 2, bq), F32)],
        scratch_shapes=[pltpu.VMEM((S, LANES), F32), pltpu.VMEM((S, LANES), F32), pltpu.VMEM((S, LANES), F32)],
        compiler_params=_cparams("parallel", "arbitrary"),
    )(proj, proj, proj, negc4, o, lse, d_o, q_t, do_t, *deps)


def _exchange(arrs, *, gather, name):
    n = len(arrs)
    npeer = N_DEV - 1

    def body(*refs):
        ins, outs = refs[:n], refs[n:2 * n]
        send_sems, recv_sems, loc_sems = refs[2 * n:]
        x, y, c = lax.axis_index("x"), lax.axis_index("y"), lax.axis_index("c")
        me = 4 * x + 2 * y + c
        peers = []
        for k in range(1, N_DEV):
            px = 1 - x if k & 4 else x
            py = 1 - y if k & 2 else y
            pc = 1 - c if k & 1 else c
            peers.append(((px, py, pc), 4 * px + 2 * py + pc))

        def remote(w, k):
            dev, idx = peers[k]
            src = ins[w] if gather else ins[w].at[idx]
            return pltpu.make_async_remote_copy(
                src_ref=src, dst_ref=outs[w].at[me],
                send_sem=send_sems.at[w * npeer + k], recv_sem=recv_sems.at[w * npeer + k],
                device_id=dev, device_id_type=pl.DeviceIdType.MESH)

        def arrival(w, k):
            dev, idx = peers[k]
            src = ins[w] if gather else ins[w].at[idx]
            return pltpu.make_async_remote_copy(
                src_ref=src, dst_ref=outs[w].at[idx],
                send_sem=send_sems.at[w * npeer + k], recv_sem=recv_sems.at[w * npeer + k],
                device_id=dev, device_id_type=pl.DeviceIdType.MESH)

        local = []
        for w in range(n):
            for k in range(npeer):
                remote(w, k).start()
            cp = pltpu.make_async_copy(ins[w] if gather else ins[w].at[me], outs[w].at[me], loc_sems.at[w])
            cp.start()
            local.append(cp)
        for w in range(n):
            for k in range(npeer):
                arrival(w, k).wait_recv()
        for w in range(n):
            for k in range(npeer):
                remote(w, k).wait_send()
            local[w].wait()

    hbm = pl.BlockSpec(memory_space=pl.ANY)
    out_shape = [jax.ShapeDtypeStruct((N_DEV,) + (a.shape if gather else a.shape[1:]), a.dtype) for a in arrs]
    return pl.pallas_call(
        body, name=name,
        in_specs=[hbm] * n, out_specs=[hbm] * n, out_shape=out_shape,
        scratch_shapes=[pltpu.SemaphoreType.DMA((n * npeer,)), pltpu.SemaphoreType.DMA((n * npeer,)),
                        pltpu.SemaphoreType.DMA((n,))],
        compiler_params=pltpu.CompilerParams(has_side_effects=True),
    )(*arrs)


def _gather_two_level(shard, *, name):
    def body(x_ref, out_ref, send_sems, recv_sems, local_sem):
        x, y, c = lax.axis_index("x"), lax.axis_index("y"), lax.axis_index("c")
        me, sibling = (x, y, c), (x, y, 1 - c)
        chips = [(1 - x, y), (x, 1 - y), (1 - x, 1 - y)]

        def slot(px, py, pc):
            return out_ref.at[4 * px + 2 * py + pc]

        def copy(k, block, to, src=None):
            return pltpu.make_async_remote_copy(
                src_ref=slot(*block) if src is None else src, dst_ref=slot(*block),
                send_sem=send_sems.at[k], recv_sem=recv_sems.at[k],
                device_id=to, device_id_type=pl.DeviceIdType.MESH)

        mine = pltpu.make_async_copy(x_ref, slot(*me), local_sem)
        mine.start()
        first = [copy(0, me, sibling, src=x_ref)]
        first += [copy(1 + j, me, (*chip, c), src=x_ref) for j, chip in enumerate(chips)]
        for cp in first:
            cp.start()
        passed = [copy(4 + j, (*chip, c), sibling) for j, chip in enumerate(chips)]
        for j, chip in enumerate(chips):
            copy(1 + j, (*chip, c), me).wait_recv()
            passed[j].start()
        copy(0, sibling, me).wait_recv()
        for j, chip in enumerate(chips):
            copy(4 + j, (*chip, 1 - c), me).wait_recv()
        for cp in first + passed:
            cp.wait_send()
        mine.wait()

    return pl.pallas_call(
        body, name=name,
        in_specs=[_ANY], out_specs=_ANY,
        out_shape=jax.ShapeDtypeStruct((N_DEV,) + shard.shape, shard.dtype),
        scratch_shapes=[pltpu.SemaphoreType.DMA((N_DEV - 1,)), pltpu.SemaphoreType.DMA((N_DEV - 1,)),
                        pltpu.SemaphoreType.DMA],
        compiler_params=pltpu.CompilerParams(has_side_effects=True),
    )(shard)


_HBM = pl.BlockSpec(memory_space=pltpu.HBM)
_SEM = pl.BlockSpec(memory_space=pltpu.SEMAPHORE)
_EFFECT = pltpu.SideEffectType.DATAFLOW_SIDE_EFFECTING
NPEER = N_DEV - 1


def _peer_table():
    x, y, c = lax.axis_index("x"), lax.axis_index("y"), lax.axis_index("c")
    peers = []
    for k in range(1, N_DEV):
        px = 1 - x if k & 4 else x
        py = 1 - y if k & 2 else y
        pc = 1 - c if k & 1 else c
        peers.append(((px, py, pc), 4 * px + 2 * py + pc))
    return 4 * x + 2 * y + c, peers


def _split_copy(ins, lands, send_sems, recv_sems, gather, me, peers, w, k, arriving):
    dev, idx = peers[k]
    return pltpu.make_async_remote_copy(
        src_ref=ins[w] if gather else ins[w].at[idx],
        dst_ref=lands[w].at[idx if arriving else me],
        send_sem=send_sems.at[w * NPEER + k], recv_sem=recv_sems.at[w * NPEER + k],
        device_id=dev, device_id_type=pl.DeviceIdType.MESH)


def _exchange_start(arrs, *, gather, name, deps=()):
    n = len(arrs)
    land_shapes = [(N_DEV,) + (a.shape if gather else a.shape[1:]) for a in arrs]

    def body(*refs):
        ins, lands = refs[:n], refs[n:2 * n]
        send_sems, recv_sems = refs[2 * n + len(deps)], refs[2 * n + len(deps) + 1]
        token = refs[-1]
        me, peers = _peer_table()
        for w in range(n):
            for k in range(NPEER):
                _split_copy(ins, lands, send_sems, recv_sems, gather, me, peers, w, k, False).start()
        token[...] = jnp.zeros_like(token)

    out_shape = ([pltpu.SemaphoreType.DMA((n * NPEER,)), pltpu.SemaphoreType.DMA((n * NPEER,))]
                 + [pltpu.HBM(a.shape, a.dtype) for a in arrs]
                 + [pltpu.HBM(s, a.dtype) for s, a in zip(land_shapes, arrs)]
                 + [jax.ShapeDtypeStruct((8, LANES), F32)])
    res = pl.pallas_call(
        body, name=name,
        in_specs=[_HBM] * (2 * n) + [_ANY] * len(deps),
        out_specs=[_SEM, _SEM] + [_HBM] * (2 * n) + [pl.BlockSpec(memory_space=pltpu.VMEM)],
        out_shape=out_shape,
        input_output_aliases={i: 2 + i for i in range(2 * n)},
        compiler_params=pltpu.CompilerParams(has_side_effects=_EFFECT),
    )(*[pltpu.with_memory_space_constraint(a, pltpu.HBM) for a in arrs],
      *[pltpu.with_memory_space_constraint(lax.empty(s, a.dtype), pltpu.HBM) for s, a in zip(land_shapes, arrs)],
      *deps)
    return (n, gather, res[0], res[1], res[2:2 + n], res[2 + n:2 + 2 * n]), res[-1]


def _exchange_wait(handle, after, *, name):
    n, gather, send_sems, recv_sems, ins_thru, lands_thru = handle

    def body(*refs):
        ins, lands = refs[:n], refs[n:2 * n]
        send_s, recv_s = refs[2 * n], refs[2 * n + 1]
        me, peers = _peer_table()
        for w in range(n):
            for k in range(NPEER):
                _split_copy(ins, lands, send_s, recv_s, gather, me, peers, w, k, False).wait_send()
                _split_copy(ins, lands, send_s, recv_s, gather, me, peers, w, k, True).wait_recv()

    res = pl.pallas_call(
        body, name=name,
        in_specs=[_HBM] * (2 * n) + [_SEM, _SEM, pl.BlockSpec(memory_space=pl.ANY)],
        out_specs=[_HBM] * (2 * n),
        out_shape=[pltpu.HBM(a.shape, a.dtype) for a in list(ins_thru) + list(lands_thru)],
        input_output_aliases={i: i for i in range(2 * n)},
        compiler_params=pltpu.CompilerParams(has_side_effects=_EFFECT),
    )(*ins_thru, *lands_thru, send_sems, recv_sems, after)
    return res[:n], res[n:2 * n]


def _ordered_sum(s_ref, own_ref):
    if own_ref is None:
        blocks = [s_ref[q].astype(F32) for q in range(N_DEV)]
    else:
        me = 4 * lax.axis_index("x") + 2 * lax.axis_index("y") + lax.axis_index("c")
        own = own_ref[...]
        blocks = [jnp.where(me == q, own, s_ref[q]).astype(F32) for q in range(N_DEV)]
    acc = blocks[0]
    for b in blocks[1:]:
        acc = acc + b
    return acc


def _sum8(stack, own, *, name):
    _, R, C = stack.shape
    if R % 8 == 0:
        tr, tc = _pick(R, max(8, STEP_BYTES // (C * 4 * (N_DEV + 2))), 8), C
    else:
        tr, tc = R, _pick(C, max(LANES, STEP_BYTES // (R * 4 * (N_DEV + 2))))

    def body(s_ref, own_ref, o_ref):
        o_ref[...] = _ordered_sum(s_ref, own_ref)

    blk = pl.BlockSpec((tr, tc), lambda i, j: (i, j))
    return pl.pallas_call(
        body, name=name, grid=(R // tr, C // tc),
        in_specs=[pl.BlockSpec((N_DEV, tr, tc), lambda i, j: (0, i, j)), blk],
        out_specs=blk,
        out_shape=jax.ShapeDtypeStruct((R, C), F32),
        compiler_params=_cparams("parallel", "parallel"),
    )(stack, own)


def _adamw_math(w, g, m, v):
    m = ADAM_B1 * m + (1.0 - ADAM_B1) * g
    v = ADAM_B2 * v + (1.0 - ADAM_B2) * (g * g)
    m_hat = m / (1.0 - ADAM_B1 ** ADAM_STEP)
    v_hat = v / (1.0 - ADAM_B2 ** ADAM_STEP)
    delta = -ADAM_LR * (m_hat / (jnp.sqrt(v_hat) + ADAM_EPS) + ADAM_WD * w)
    return delta, m, v


def _adamw(w, g, m, v, *, name, stacked, own=None, transposed=False):
    R, C = w.shape
    if transposed:
        tr = _pick(R, max(LANES, STEP_BYTES // (C * 4 * (9 + N_DEV))))
    else:
        tr = _pick(R, max(8, STEP_BYTES // (C * 4 * (8 + (N_DEV if stacked else 1)))), 8)
    has_own = own is not None

    def body(w_ref, g_ref, m_ref, v_ref, *rest):
        go_ref, d_ref, mo_ref, vo_ref = rest[-4:]
        g = _ordered_sum(g_ref, rest[0] if has_own else None) if stacked else g_ref[...]
        if transposed:
            g = g.T
        delta, m2, v2 = _adamw_math(w_ref[...], g, m_ref[...], v_ref[...])
        go_ref[...] = g
        d_ref[...] = delta
        mo_ref[...] = m2
        vo_ref[...] = v2

    row = pl.BlockSpec((tr, C), lambda i: (i, 0))
    if transposed:
        g_spec, own_spec = pl.BlockSpec((N_DEV, C, tr), lambda i: (0, 0, i)), pl.BlockSpec((C, tr), lambda i: (0, i))
    else:
        g_spec, own_spec = (pl.BlockSpec((N_DEV, tr, C), lambda i: (0, i, 0)) if stacked else row), row
    return pl.pallas_call(
        body, name=name, grid=(R // tr,),
        in_specs=[row, g_spec, row, row] + [own_spec] * has_own, out_specs=[row] * 4,
        out_shape=[jax.ShapeDtypeStruct((R, C), F32)] * 4,
        compiler_params=_cparams("parallel"),
    )(w, g, m, v, *([own] if has_own else []))


def kernel(x, positions, attn_norm, w_in, fox_f_bias, swa_sinks, w_branch_swa, w_branch_fox, w_out, mlp_norm, w_up, w_down, final_norm, loss_target, m_attn_norm, m_w_in, m_fox_f_bias, m_swa_sinks, m_w_branch_swa, m_w_branch_fox, m_w_out, m_mlp_norm, m_w_up, m_w_down, m_final_norm, v_attn_norm, v_w_in, v_fox_f_bias, v_swa_sinks, v_w_branch_swa, v_w_branch_fox, v_w_out, v_mlp_norm, v_w_up, v_w_down, v_final_norm):
    S, D = x.shape[1], x.shape[2]
    DFF = w_up.shape[2] * N_DEV
    d_in = w_in.shape[2] * N_DEV
    assert d_in == QKV_W + FOX_HEADS + 2 * D and (2 * D) % SWA_Q_W == 0 and S % (4 * LANES) == 0
    q_off = 2 * D
    k_off = q_off + SWA_Q_W
    v_off = k_off + SWA_KV_W
    fq_off = v_off + SWA_KV_W
    fk_off = fq_off + FOX_W
    fv_off = fk_off + FOX_W
    fl_off = fv_off + FOX_W
    NP = fl_off + FL_PAD
    x2d, tgt = x[0], loss_target[0]

    shards = [w_in[0].T.astype(BF16), w_branch_swa[0].T.astype(BF16), w_branch_fox[0].T.astype(BF16),
              w_out[0].astype(BF16), w_up[0].T.astype(BF16), w_down[0].astype(BF16)]
    me = 4 * lax.axis_index("x") + 2 * lax.axis_index("y") + lax.axis_index("c")

    def filled(stack, own):
        return lax.dynamic_update_slice(stack, own[None], (me,) + (0,) * own.ndim)

    g_in = _gather_two_level(shards[0], name="gather_w_in")
    h_rest, tok_rest = _exchange_start(shards[1:], gather=True, name="gather_rest_start", deps=[g_in])

    tm = _pick(S, 1024)
    td = _pick(D, 1024)
    tf = _pick(DFF, 1024)
    tnp = _pick(NP, 1024)

    h1 = _rms_fwd(x2d, attn_norm, name="rms1", deps=[tok_rest])
    w_in_t = g_in.reshape(d_in, D)
    w_in_p = jnp.concatenate([w_in_t[QKV_W + FOX_HEADS:], w_in_t[:QKV_W], w_in_t[QKV_W:QKV_W + FOX_HEADS],
                              jnp.zeros((FL_PAD - FOX_HEADS, D), BF16)], axis=0)
    w_fl_t = w_in_t[QKV_W:QKV_W + FOX_HEADS]
    proj, = _matmul(h1, w_in_p, mode="nt", name="mm_in", out_dtypes=[BF16], tm=_pick(S, 2048), tn=tnp, tk=D)
    z_sd, = _matmul(h1, w_fl_t, mode="nt", name="mm_flogit", out_dtypes=[F32], tm=tm, tn=FOX_HEADS, tk=D)
    z_t = z_sd.T
    bias_col = fox_f_bias.reshape(FOX_HEADS, 1)
    negc = _fox_prep(z_t, bias_col, name="fox_prep")
    (fbq, fbk), (bbq, bbk) = _fox_blocks(S)
    inv_freq = ROPE_THETA ** (-jnp.arange(0, HEAD_DIM, 2, dtype=F32) / HEAD_DIM)
    invf = jnp.tile(inv_freq, LANES // (HEAD_DIM // 2)).reshape(1, LANES)
    cos_t, sin_t = _rope_tables(positions.reshape(S, 1), invf, name="rope_tables")
    q_rope, k_rope = _rope_fwd(proj, cos_t, sin_t, q_off=q_off, k_off=k_off, name="rope_fwd")
    sinks = swa_sinks.reshape(-1)
    swa_mask = _swa_mask_bias()
    o_a = _swa_fwd(q_rope, k_rope, proj, sinks, swa_mask, v_off=v_off, name="swa_fwd")
    o_b, lse = _fox_fwd(proj, _key_bias_blocks(negc, fbk), q_off=fq_off, k_off=fk_off, v_off=fv_off,
                        bq=fbq, bk=fbk, name="fox_fwd")
    s_rest, g_rest = _exchange_wait(h_rest, o_b, name="gather_rest_wait")
    g_bs, g_bf, g_o, g_up, g_dn = [filled(g, s) for g, s in zip(g_rest, s_rest)]
    w_bs_t = g_bs.reshape(D, SWA_Q_W)
    w_bf_t = g_bf.reshape(D, FOX_W)
    w_o = g_o.reshape(D, D)
    w_up_t = g_up.reshape(DFF, D)
    w_dn = g_dn.reshape(DFF, D)
    ya, = _matmul(o_a, w_bs_t, mode="nt", name="mm_branch_swa", out_dtypes=[BF16], tm=_pick(S, 512), tn=D, tk=SWA_Q_W)
    gate_maps = [lambda i, j, k: (i, j), lambda i, j, k: (i, j), lambda i, j, k: (i, j + D // td)]

    def merge_epi(acc, ya_t, ga_t, gb_t):
        merged = _sigmoid(ga_t.astype(F32)) * ya_t.astype(F32) + _sigmoid(gb_t.astype(F32)) * acc
        return acc, merged

    yb, merged = _matmul(o_b, w_bf_t, mode="nt", name="mm_branch_fox", out_dtypes=[BF16, BF16],
                         tm=tm, tn=td, tk=FOX_W, extras=[ya, proj, proj], extra_maps=gate_maps,
                         epilogue=merge_epi)
    def out_epi(acc, r, g):
        xm = acc + r
        rr = lax.rsqrt(jnp.mean(xm * xm, axis=-1, keepdims=True) + RMS_EPS)
        return xm, xm * rr * g

    x_mid, h2 = _matmul(merged, w_o, mode="nn", name="mm_out", out_dtypes=[F32, BF16], tm=_pick(S, 512), tn=D, tk=D,
                        extras=[x2d, mlp_norm], extra_maps=[lambda i, j, k: (i, j), lambda i, j, k: (0, 0)],
                        extra_shapes=[None, (1, D)], epilogue=out_epi)
    u, = _matmul(h2, w_up_t, mode="nt", name="mm_up", out_dtypes=[BF16], tm=_pick(S, 2048), tn=tf, tk=D,
                 epilogue=lambda acc: (jnp.maximum(acc, 0.0),))
    x_fin, = _matmul(u, w_dn, mode="nn", name="mm_down", out_dtypes=[F32], tm=tm, tn=td, tk=_pick(DFF, 2048),
                     a_fn=_square_bf16, extras=[x_mid], epilogue=lambda acc, r: (acc + r,))

    dx3b, dg3, loss_part = _loss_head(x_fin, tgt, final_norm.reshape(1, D), name="loss_head")
    d_up, = _matmul(dx3b, w_dn, mode="nt", name="mm_d_act", out_dtypes=[BF16], tm=_pick(S, 2048), tn=tf, tk=D,
                    extras=[u], epilogue=lambda acc, ut: (acc * (2.0 * ut.astype(F32)),))
    tks = _pick(S, 2048)
    dw_dn, = _matmul(u, dx3b, mode="tn", name="mm_dw_down", out_dtypes=[BF16], tm=tf, tn=td, tk=tks,
                     a_fn=_square_bf16)
    dh2, = _matmul(d_up, w_up_t, mode="nn", name="mm_dh2", out_dtypes=[BF16], tm=_pick(S, 512), tn=D,
                   tk=_pick(DFF, 2048))
    dw_up_t, = _matmul(d_up, h2, mode="tn", name="mm_dw_up", out_dtypes=[BF16], tm=tf, tn=td, tk=tks)
    h_s1, tok_s1 = _exchange_start([dw_up_t.reshape(N_DEV, DFF // N_DEV, D), dw_dn.reshape(N_DEV, DFF // N_DEV, D)],
                                   gather=False, name="scatter_mlp_start")
    dx2b, dg2 = _rms_bwd(dh2, x_mid, mlp_norm, dx3b, name="rms2_bwd", out_dtype=BF16, deps=[tok_s1])

    def gate_bwd_epi(dm, ya_t, yb_t, ga_t, gb_t):
        sa, sb = _sigmoid(ga_t.astype(F32)), _sigmoid(gb_t.astype(F32))
        return (dm * sa, dm * sb, dm * ya_t.astype(F32) * sa * (1.0 - sa), dm * yb_t.astype(F32) * sb * (1.0 - sb))

    gmaps = [lambda i, j, k: (i, j), lambda i, j, k: (i, j), lambda i, j, k: (i, j),
             lambda i, j, k: (i, j + D // td)]
    d_ya, d_yb, d_ga, d_gb = _matmul(dx2b, w_o, mode="nt", name="mm_d_merged", out_dtypes=[BF16] * 4,
                                     tm=tm, tn=td, tk=D, extras=[ya, yb, proj, proj], extra_maps=gmaps,
                                     epilogue=gate_bwd_epi)
    dw_o, = _matmul(merged, dx2b, mode="tn", name="mm_dw_out", out_dtypes=[BF16], tm=td, tn=td, tk=tks)
    d_oa, = _matmul(d_ya, w_bs_t, mode="nn", name="mm_d_oa", out_dtypes=[BF16], tm=tm, tn=SWA_Q_W, tk=D)
    d_ob, = _matmul(d_yb, w_bf_t, mode="nn", name="mm_d_ob", out_dtypes=[BF16], tm=tm, tn=FOX_W, tk=D)
    tkl = _pick(S, 4096)
    dw_bs_t, = _matmul(d_ya, o_a, mode="tn", name="mm_dw_bs", out_dtypes=[BF16], tm=td, tn=SWA_Q_W, tk=tkl)
    dw_bf_t, = _matmul(d_yb, o_b, mode="tn", name="mm_dw_bf", out_dtypes=[BF16], tm=td, tn=FOX_W, tk=tkl)
    h_s2, tok_s2 = _exchange_start([dw_bs_t.reshape(N_DEV, D // N_DEV, SWA_Q_W),
                                    dw_bf_t.reshape(N_DEV, D // N_DEV, FOX_W), dw_o.reshape(N_DEV, D // N_DEV, D)],
                                   gather=False, name="scatter_attn_start")
    def row_blocks_t(a):
        return a.reshape(S // bbq, bbq, FOX_W).transpose(0, 2, 1)

    d_fq, d_fk, d_fv, dcol4, drow4 = _fox_bwd(proj, _key_bias_blocks(negc, bbk), o_b, lse, d_ob,
                                              row_blocks_t(proj[:, fq_off:fq_off + FOX_W]), row_blocks_t(d_ob),
                                              q_off=fq_off, k_off=fk_off, v_off=fv_off, bq=bbq, bk=bbk,
                                              name="fox_bwd", deps=[tok_s2])
    dcol = dcol4.transpose(0, 2, 1, 3).reshape(FOX_HEADS, S)
    drow = drow4.transpose(0, 2, 1, 3).reshape(FOX_HEADS, S)
    dz_t, dbias_l = _fox_post(drow, dcol, z_t, bias_col, name="fox_post")
    d_aq, dk_c, dk_p, dv_c, dv_p, dsink_l = _swa_bwd(q_rope, k_rope, proj, sinks, d_oa, cos_t, sin_t, swa_mask,
                                                     v_off=v_off, name="swa_bwd")
    d_ak, d_av = _rope_bwd(dk_c, dk_p, dv_c, dv_p, cos_t, sin_t, name="rope_bwd")
    dz_pad = jnp.pad(dz_t.T.astype(BF16), ((0, 0), (0, FL_PAD - FOX_HEADS)))
    d_proj = jnp.concatenate([d_ga, d_gb, d_aq, d_ak, d_av, d_fq, d_fk, d_fv, dz_pad], axis=1)
    tkp = _pick(NP, 2304)
    dw_in_p, = _matmul(d_proj, h1, mode="tn", name="mm_dw_in", out_dtypes=[BF16], tm=_pick(NP, 512), tn=D, tk=tks)
    dw_in_t = jnp.concatenate([dw_in_p[q_off:q_off + QKV_W], dw_in_p[fl_off:fl_off + FOX_HEADS], dw_in_p[:q_off]],
                              axis=0)
    h_s3, tok_s3 = _exchange_start([dw_in_t.reshape(N_DEV, d_in // N_DEV, D)], gather=False,
                                   name="scatter_in_start")
    dh1, = _matmul(d_proj, w_in_p, mode="nn", name="mm_dh1", out_dtypes=[BF16], tm=tm, tn=td, tk=tkp, deps=[tok_s3])
    dx, dg1 = _rms_bwd(dh1, x2d, attn_norm, dx2b, name="rms1_bwd", out_dtype=F32)

    dbias = dbias_l[:, 0]
    dsinks = dsink_l[:, :, 0].reshape(-1)
    nsm = 3 * D + 2 * LANES
    tail = jnp.zeros((2 * LANES,), F32)
    small_g = jnp.concatenate([dg1[0], dg2[0], dg3[0],
                               tail.at[0:16].set(dbias).at[16:32].set(dsinks).at[32].set(loss_part[0, 0])])

    def pack(a_norm, b_norm, f_norm, bias, snk):
        return jnp.concatenate([a_norm[0], b_norm[0], f_norm,
                                tail.at[0:16].set(bias[0]).at[16:32].set(snk[0])]).reshape(1, nsm)

    small_stack, = _exchange([small_g.reshape(1, nsm)], gather=True, name="gather_small")
    u_sm = _adamw(pack(attn_norm, mlp_norm, final_norm, fox_f_bias, swa_sinks), small_stack,
                  pack(m_attn_norm, m_mlp_norm, m_final_norm, m_fox_f_bias, m_swa_sinks),
                  pack(v_attn_norm, v_mlp_norm, v_final_norm, v_fox_f_bias, v_swa_sinks),
                  name="adamw_small", stacked=True)
    loss = u_sm[0][0, 3 * D + 32]

    def own_of(src):
        return lax.dynamic_index_in_dim(src, me, 0, keepdims=False)

    def update_t(stack, src, w, m, v, nm):
        g = _sum8(stack, own_of(src), name="sum_" + nm).T
        return _adamw(w[0], g, m[0], v[0], name="adamw_" + nm, stacked=False)

    def update(stack, src, w, m, v, nm, transposed=False):
        return _adamw(w[0], stack, m[0], v[0], name="adamw_" + nm, stacked=True, own=own_of(src),
                      transposed=transposed)

    (s_up, s_dn), (r_up, r_dn) = _exchange_wait(h_s1, u_sm[1], name="scatter_mlp_wait")
    u_up = update(r_up, s_up, w_up, m_w_up, v_w_up, "w_up", transposed=True)
    u_dn = update(r_dn, s_dn, w_down, m_w_down, v_w_down, "w_down")
    (s_bs, s_bf, s_o), (r_bs, r_bf, r_o) = _exchange_wait(h_s2, u_dn[1], name="scatter_attn_wait")
    u_bs = update(r_bs, s_bs, w_branch_swa, m_w_branch_swa, v_w_branch_swa, "w_bs", transposed=True)
    u_bf = update(r_bf, s_bf, w_branch_fox, m_w_branch_fox, v_w_branch_fox, "w_bf", transposed=True)
    u_o = update(r_o, s_o, w_out, m_w_out, v_w_out, "w_out")
    (s_w_in,), (r_in,) = _exchange_wait(h_s3, u_o[1], name="scatter_in_wait")
    u_in = update_t(r_in, s_w_in, w_in, m_w_in, v_w_in, "w_in")

    def small(kind):
        a = u_sm[kind][0]
        return dict(attn_norm=a[0:D][None], mlp_norm=a[D:2 * D][None], final_norm=a[2 * D:3 * D],
                    fox_f_bias=a[3 * D:3 * D + 16][None], swa_sinks=a[3 * D + 16:3 * D + 32][None])

    big = dict(w_in=u_in, w_branch_swa=u_bs, w_branch_fox=u_bf, w_out=u_o, w_up=u_up, w_down=u_dn)
    order = ["attn_norm", "w_in", "fox_f_bias", "swa_sinks", "w_branch_swa", "w_branch_fox", "w_out", "mlp_norm",
             "w_up", "w_down", "final_norm"]
    outs = [loss, dx[None]]
    for kind in range(4):
        sm = small(kind)
        for nm in order:
            outs.append(big[nm][kind][None] if nm in big else sm[nm])
    return tuple(outs)
```
